```python
import jax, jax.numpy as jnp
from jax import lax
import numpy as np

D_MODEL = 1024
BATCH = 8
SEQ = 2048
DEPTH = 4

CHUNK = 64
N_MIXERS = 3
N_GLA = (DEPTH + 2) // 3
N_MLA = (DEPTH + 1) // 3
N_CONV = DEPTH // 3
ALPHA = (2 * DEPTH) ** 0.25
BETA = (8 * DEPTH) ** -0.25
LN_EPS = 1e-5
RMS_EPS = 1e-6
PLE_DIM = 256
D_FF = 4 * D_MODEL
MAX_OFFSET = 4096

GLA_HEADS = 4
GLA_DK = D_MODEL // 2 // GLA_HEADS
GLA_DV = D_MODEL // GLA_HEADS
GLA_GATE_RANK = 16
GLA_TAU = 16.0
GLA_HK = GLA_HEADS * GLA_DK
GLA_HV = GLA_HEADS * GLA_DV
GLA_SPLITS = [GLA_HK, 2 * GLA_HK, 2 * GLA_HK + GLA_HV, 2 * GLA_HK + GLA_HV + D_MODEL]
GLA_IN = 2 * GLA_HK + GLA_HV + D_MODEL + GLA_GATE_RANK

MLA_HEADS = 8
MLA_NOPE = 128
MLA_ROPE = 64
MLA_V = 128
MLA_Q_RANK = 256
MLA_KV_RANK = 256
MLA_IN = MLA_Q_RANK + MLA_KV_RANK + MLA_ROPE
ROPE_BASE = 10000.0
Q_BLOCK = 128

CONV_WIDTH = 3

kernel_name = 'hybrid_gla_mla_shortconv_deepnorm_trunk'


def layer_norm(x, g, b):
    xf = x.astype(jnp.float32)
    mu = jnp.mean(xf, -1, keepdims=True)
    var = jnp.mean(jnp.square(xf - mu), -1, keepdims=True)
    return ((xf - mu) * lax.rsqrt(var + LN_EPS) * g + b).astype(x.dtype)


def rms_norm(x, g):
    xf = x.astype(jnp.float32)
    return (xf * lax.rsqrt(jnp.mean(xf * xf, -1, keepdims=True) + RMS_EPS) * g).astype(x.dtype)


def rope(x, cos, sin):
    x1, x2 = jnp.split(x, 2, axis=-1)
    return jnp.concatenate([x1 * cos - x2 * sin, x2 * cos + x1 * sin], axis=-1)


def gla_mixer(x, w_in, w_gate_up, b_gate, norm_g, w_out):
    B_, S_, _ = x.shape
    nc = S_ // CHUNK
    q, k, v, r, g_lr = jnp.split(x @ w_in, GLA_SPLITS, axis=-1)
    log_a = jax.nn.log_sigmoid((g_lr @ w_gate_up + b_gate).astype(jnp.float32)) / GLA_TAU

    def to_chunks(t, d):
        return t.astype(jnp.float32).reshape(B_, nc, CHUNK, GLA_HEADS, d).transpose(1, 0, 3, 2, 4)

    qc = to_chunks(q, GLA_DK) * GLA_DK ** -0.5
    kc = to_chunks(k, GLA_DK)
    vc = to_chunks(v, GLA_DV)
    lc = to_chunks(log_a, GLA_DK)

    def step(state, inp):
        q_, k_, v_, la = inp
        L = jnp.cumsum(la, axis=2)
        decay = jnp.exp(-jnp.abs(L[:, :, :, None, :] - L[:, :, None, :, :]))
        scores = jnp.einsum('bhtd,bhsd,bhtsd->bhts', q_, k_, decay)
        o = scores @ v_ + (q_ * jnp.exp(L)) @ state
        L_end = L[:, :, -1:, :]
        state = (jnp.exp(L_end[:, :, 0, :, None]) * state
                 + jnp.einsum('bhsd,bhse->bhde', k_ * jnp.exp(L_end - L), v_))
        return state, o

    s0 = jnp.zeros((B_, GLA_HEADS, GLA_DK, GLA_DV), jnp.float32)
    _, o = lax.scan(step, s0, (qc, kc, vc, lc))
    o = o.transpose(1, 0, 3, 2, 4).reshape(B_, S_, GLA_HEADS, GLA_DV)
    o = rms_norm(o, norm_g).reshape(B_, S_, GLA_HV) * jax.nn.silu(r.astype(jnp.float32))
    return o.astype(x.dtype) @ w_out


def mla_mixer(x, cos, sin, w_in, q_norm, kv_norm, w_uq, w_ukv, w_out):
    B_, S_, _ = x.shape
    c_q, c_kv, k_rope = jnp.split(x @ w_in, [MLA_Q_RANK, MLA_Q_RANK + MLA_KV_RANK], axis=-1)
    q = (rms_norm(c_q, q_norm) @ w_uq).reshape(B_, S_, MLA_HEADS, MLA_NOPE + MLA_ROPE)
    kv = (rms_norm(c_kv, kv_norm) @ w_ukv).reshape(B_, S_, MLA_HEADS, MLA_NOPE + MLA_V)
    q_nope, q_rope = jnp.split(q, [MLA_NOPE], axis=-1)
    k_nope, v = jnp.split(kv, [MLA_NOPE], axis=-1)
    q_rope = rope(q_rope, cos[:, :, None, :], sin[:, :, None, :])
    k_rope = rope(k_rope, cos, sin)
    qf = jnp.concatenate([q_nope.astype(jnp.float32), q_rope.astype(jnp.float32)], axis=-1)
    qf = qf * (MLA_NOPE + MLA_ROPE) ** -0.5
    kf = jnp.concatenate([k_nope.astype(jnp.float32),
                          jnp.broadcast_to(k_rope.astype(jnp.float32)[:, :, None, :],
                                           (B_, S_, MLA_HEADS, MLA_ROPE))], axis=-1)
    n_qb = S_ // Q_BLOCK
    q_blocks = qf.reshape(B_, n_qb, Q_BLOCK, MLA_HEADS, MLA_NOPE + MLA_ROPE).transpose(1, 0, 2, 3, 4)
    key_chunk = jnp.arange(S_) // CHUNK

    def attend(args):
        qb, bi = args
        q_chunk = (bi * Q_BLOCK + jnp.arange(Q_BLOCK)) // CHUNK
        s = jnp.einsum('bqhd,bkhd->bhqk', qb, kf)
        s = jnp.where(key_chunk[None, :] <= q_chunk[:, None], s, -jnp.inf)
        pr = jax.nn.softmax(s, axis=-1)
        return jnp.einsum('bhqk,bkhd->bqhd', pr.astype(v.dtype), v)

    o = lax.map(attend, (q_blocks, jnp.arange(n_qb)))
    o = o.transpose(1, 0, 2, 3, 4).reshape(B_, S_, MLA_HEADS * MLA_V)
    return o.astype(x.dtype) @ w_out


def conv_mixer(x, w_in, conv_w, w_out):
    b, c, u = jnp.split(x @ w_in, 3, axis=-1)
    z = lax.conv_general_dilated(c * u, conv_w[:, None, :], window_strides=(1,),
                                 padding=[(CONV_WIDTH - 1, 0)],
                                 dimension_numbers=('NWC', 'WIO', 'NWC'),
                                 feature_group_count=D_MODEL)
    return (b * z) @ w_out


def sq_relu_mlp(x, w1, w2):
    return jnp.square(jax.nn.relu(x @ w1)) @ w2


def _fwd_setup_inputs(seed: int = 0) -> dict:
    key = jax.random.key(seed)
    ks = jax.random.split(key, 24)

    def nrm(i, shape, scale):
        return jax.random.normal(ks[i], shape, jnp.float32) * scale

    x = nrm(0, (BATCH, SEQ, D_MODEL), 1.0)
    p = nrm(1, (DEPTH, BATCH, SEQ, PLE_DIM), 1.0)
    offsets = jax.random.randint(ks[2], (BATCH, 1), 0, MAX_OFFSET, dtype=jnp.int32)
    positions = (offsets + jnp.arange(SEQ, dtype=jnp.int32)[None, :]).astype(jnp.int32)
    return {
        'x': x,
        'p': p,
        'positions': positions,
        'gla_w_in': nrm(3, (N_GLA, D_MODEL, GLA_IN), D_MODEL ** -0.5),
        'gla_w_gate_up': nrm(4, (N_GLA, GLA_GATE_RANK, GLA_HK), GLA_GATE_RANK ** -0.5),
        'gla_b_gate': nrm(5, (N_GLA, GLA_HK), 0.1),
        'gla_norm_g': 1.0 + nrm(6, (N_GLA, GLA_DV), 0.01),
        'gla_w_out': nrm(7, (N_GLA, GLA_HV, D_MODEL), GLA_HV ** -0.5 * BETA),
        'mla_w_in': nrm(8, (N_MLA, D_MODEL, MLA_IN), D_MODEL ** -0.5),
        'mla_q_norm': 1.0 + nrm(9, (N_MLA, MLA_Q_RANK), 0.01),
        'mla_kv_norm': 1.0 + nrm(10, (N_MLA, MLA_KV_RANK), 0.01),
        'mla_w_uq': nrm(11, (N_MLA, MLA_Q_RANK, MLA_HEADS * (MLA_NOPE + MLA_ROPE)), MLA_Q_RANK ** -0.5),
        'mla_w_ukv': nrm(12, (N_MLA, MLA_KV_RANK, MLA_HEADS * (MLA_NOPE + MLA_V)), MLA_KV_RANK ** -0.5),
        'mla_w_out': nrm(13, (N_MLA, MLA_HEADS * MLA_V, D_MODEL), (MLA_HEADS * MLA_V) ** -0.5 * BETA),
        'conv_w_in': nrm(14, (N_CONV, D_MODEL, 3 * D_MODEL), D_MODEL ** -0.5),
        'conv_w': nrm(15, (N_CONV, CONV_WIDTH, D_MODEL), CONV_WIDTH ** -0.5),
        'conv_w_out': nrm(16, (N_CONV, D_MODEL, D_MODEL), D_MODEL ** -0.5 * BETA),
        'ln_g': 1.0 + nrm(17, (DEPTH, 2, D_MODEL), 0.01),
        'ln_b': nrm(18, (DEPTH, 2, D_MODEL), 0.01),
        'mlp_w1': nrm(19, (DEPTH, D_MODEL, D_FF), D_MODEL ** -0.5),
        'mlp_w2': nrm(20, (DEPTH, D_FF, D_MODEL), D_FF ** -0.5 * BETA),
        'ple_w_gate': nrm(21, (DEPTH, D_MODEL, D_MODEL), D_MODEL ** -0.5),
        'ple_w_proj': nrm(22, (DEPTH, PLE_DIM, D_MODEL), PLE_DIM ** -0.5),
    }


def _fwd_reference(x, p, positions, gla_w_in, gla_w_gate_up, gla_b_gate, gla_norm_g, gla_w_out,
              mla_w_in, mla_q_norm, mla_kv_norm, mla_w_uq, mla_w_ukv, mla_w_out,
              conv_w_in, conv_w, conv_w_out, ln_g, ln_b, mlp_w1, mlp_w2,
              ple_w_gate, ple_w_proj):
    inv_freq = ROPE_BASE ** (-jnp.arange(0, MLA_ROPE // 2, dtype=jnp.float32) * (2.0 / MLA_ROPE))
    ang = positions.astype(jnp.float32)[..., None] * inv_freq
    cos, sin = jnp.cos(ang), jnp.sin(ang)
    for i in range(DEPTH):
        j = i // N_MIXERS
        kind = i % N_MIXERS
        if kind == 0:
            h = gla_mixer(x, gla_w_in[j], gla_w_gate_up[j], gla_b_gate[j], gla_norm_g[j], gla_w_out[j])
        elif kind == 1:
            h = mla_mixer(x, cos, sin, mla_w_in[j], mla_q_norm[j], mla_kv_norm[j],
                          mla_w_uq[j], mla_w_ukv[j], mla_w_out[j])
        else:
            h = conv_mixer(x, conv_w_in[j], conv_w[j], conv_w_out[j])
        x = layer_norm(ALPHA * x + h, ln_g[i, 0], ln_b[i, 0])
        x = layer_norm(ALPHA * x + sq_relu_mlp(x, mlp_w1[i], mlp_w2[i]), ln_g[i, 1], ln_b[i, 1])
        x = x + jax.nn.sigmoid(x @ ple_w_gate[i]) * (p[i] @ ple_w_proj[i])
    return x


import jax as _jax
import jax.numpy as _jnp

TWIN_FORMAT = 'train_step'
FWD_PARAMS = ['x', 'p', 'positions', 'gla_w_in', 'gla_w_gate_up', 'gla_b_gate', 'gla_norm_g', 'gla_w_out', 'mla_w_in', 'mla_q_norm', 'mla_kv_norm', 'mla_w_uq', 'mla_w_ukv', 'mla_w_out', 'conv_w_in', 'conv_w', 'conv_w_out', 'ln_g', 'ln_b', 'mlp_w1', 'mlp_w2', 'ple_w_gate', 'ple_w_proj']
TWIN_WEIGHTS = ['gla_w_in', 'gla_w_gate_up', 'gla_b_gate', 'gla_norm_g', 'gla_w_out', 'mla_w_in', 'mla_q_norm', 'mla_kv_norm', 'mla_w_uq', 'mla_w_ukv', 'mla_w_out', 'conv_w_in', 'conv_w', 'conv_w_out', 'ln_g', 'ln_b', 'mlp_w1', 'mlp_w2', 'ple_w_gate', 'ple_w_proj']
TWIN_DIFF_INPUT = 'x'
TWIN_INPUTS = ['x', 'p', 'positions', 'gla_w_in', 'gla_w_gate_up', 'gla_b_gate', 'gla_norm_g', 'gla_w_out', 'mla_w_in', 'mla_q_norm', 'mla_kv_norm', 'mla_w_uq', 'mla_w_ukv', 'mla_w_out', 'conv_w_in', 'conv_w', 'conv_w_out', 'ln_g', 'ln_b', 'mlp_w1', 'mlp_w2', 'ple_w_gate', 'ple_w_proj', 'loss_target', 'm_gla_w_in', 'm_gla_w_gate_up', 'm_gla_b_gate', 'm_gla_norm_g', 'm_gla_w_out', 'm_mla_w_in', 'm_mla_q_norm', 'm_mla_kv_norm', 'm_mla_w_uq', 'm_mla_w_ukv', 'm_mla_w_out', 'm_conv_w_in', 'm_conv_w', 'm_conv_w_out', 'm_ln_g', 'm_ln_b', 'm_mlp_w1', 'm_mlp_w2', 'm_ple_w_gate', 'm_ple_w_proj', 'v_gla_w_in', 'v_gla_w_gate_up', 'v_gla_b_gate', 'v_gla_norm_g', 'v_gla_w_out', 'v_mla_w_in', 'v_mla_q_norm', 'v_mla_kv_norm', 'v_mla_w_uq', 'v_mla_w_ukv', 'v_mla_w_out', 'v_conv_w_in', 'v_conv_w', 'v_conv_w_out', 'v_ln_g', 'v_ln_b', 'v_mlp_w1', 'v_mlp_w2', 'v_ple_w_gate', 'v_ple_w_proj']
TWIN_OUTPUTS = ['loss', 'grad_x', 'grad_gla_w_in', 'grad_gla_w_gate_up', 'grad_gla_b_gate', 'grad_gla_norm_g', 'grad_gla_w_out', 'grad_mla_w_in', 'grad_mla_q_norm', 'grad_mla_kv_norm', 'grad_mla_w_uq', 'grad_mla_w_ukv', 'grad_mla_w_out', 'grad_conv_w_in', 'grad_conv_w', 'grad_conv_w_out', 'grad_ln_g', 'grad_ln_b', 'grad_mlp_w1', 'grad_mlp_w2', 'grad_ple_w_gate', 'grad_ple_w_proj', 'delta_gla_w_in', 'delta_gla_w_gate_up', 'delta_gla_b_gate', 'delta_gla_norm_g', 'delta_gla_w_out', 'delta_mla_w_in', 'delta_mla_q_norm', 'delta_mla_kv_norm', 'delta_mla_w_uq', 'delta_mla_w_ukv', 'delta_mla_w_out', 'delta_conv_w_in', 'delta_conv_w', 'delta_conv_w_out', 'delta_ln_g', 'delta_ln_b', 'delta_mlp_w1', 'delta_mlp_w2', 'delta_ple_w_gate', 'delta_ple_w_proj', 'new_m_gla_w_in', 'new_m_gla_w_gate_up', 'new_m_gla_b_gate', 'new_m_gla_norm_g', 'new_m_gla_w_out', 'new_m_mla_w_in', 'new_m_mla_q_norm', 'new_m_mla_kv_norm', 'new_m_mla_w_uq', 'new_m_mla_w_ukv', 'new_m_mla_w_out', 'new_m_conv_w_in', 'new_m_conv_w', 'new_m_conv_w_out', 'new_m_ln_g', 'new_m_ln_b', 'new_m_mlp_w1', 'new_m_mlp_w2', 'new_m_ple_w_gate', 'new_m_ple_w_proj', 'new_v_gla_w_in', 'new_v_gla_w_gate_up', 'new_v_gla_b_gate', 'new_v_gla_norm_g', 'new_v_gla_w_out', 'new_v_mla_w_in', 'new_v_mla_q_norm', 'new_v_mla_kv_norm', 'new_v_mla_w_uq', 'new_v_mla_w_ukv', 'new_v_mla_w_out', 'new_v_conv_w_in', 'new_v_conv_w', 'new_v_conv_w_out', 'new_v_ln_g', 'new_v_ln_b', 'new_v_mlp_w1', 'new_v_mlp_w2', 'new_v_ple_w_gate', 'new_v_ple_w_proj']
TWIN_LEAF_KINDS = {'loss': 'loss', 'grad_x': 'grad_x', 'grad_gla_w_in': 'grad_w', 'grad_gla_w_gate_up': 'grad_w', 'grad_gla_b_gate': 'grad_w', 'grad_gla_norm_g': 'grad_w', 'grad_gla_w_out': 'grad_w', 'grad_mla_w_in': 'grad_w', 'grad_mla_q_norm': 'grad_w', 'grad_mla_kv_norm': 'grad_w', 'grad_mla_w_uq': 'grad_w', 'grad_mla_w_ukv': 'grad_w', 'grad_mla_w_out': 'grad_w', 'grad_conv_w_in': 'grad_w', 'grad_conv_w': 'grad_w', 'grad_conv_w_out': 'grad_w', 'grad_ln_g': 'grad_w', 'grad_ln_b': 'grad_w', 'grad_mlp_w1': 'grad_w', 'grad_mlp_w2': 'grad_w', 'grad_ple_w_gate': 'grad_w', 'grad_ple_w_proj': 'grad_w', 'delta_gla_w_in': 'delta_w', 'delta_gla_w_gate_up': 'delta_w', 'delta_gla_b_gate': 'delta_w', 'delta_gla_norm_g': 'delta_w', 'delta_gla_w_out': 'delta_w', 'delta_mla_w_in': 'delta_w', 'delta_mla_q_norm': 'delta_w', 'delta_mla_kv_norm': 'delta_w', 'delta_mla_w_uq': 'delta_w', 'delta_mla_w_ukv': 'delta_w', 'delta_mla_w_out': 'delta_w', 'delta_conv_w_in': 'delta_w', 'delta_conv_w': 'delta_w', 'delta_conv_w_out': 'delta_w', 'delta_ln_g': 'delta_w', 'delta_ln_b': 'delta_w', 'delta_mlp_w1': 'delta_w', 'delta_mlp_w2': 'delta_w', 'delta_ple_w_gate': 'delta_w', 'delta_ple_w_proj': 'delta_w', 'new_m_gla_w_in': 'new_m', 'new_m_gla_w_gate_up': 'new_m', 'new_m_gla_b_gate': 'new_m', 'new_m_gla_norm_g': 'new_m', 'new_m_gla_w_out': 'new_m', 'new_m_mla_w_in': 'new_m', 'new_m_mla_q_norm': 'new_m', 'new_m_mla_kv_norm': 'new_m', 'new_m_mla_w_uq': 'new_m', 'new_m_mla_w_ukv': 'new_m', 'new_m_mla_w_out': 'new_m', 'new_m_conv_w_in': 'new_m', 'new_m_conv_w': 'new_m', 'new_m_conv_w_out': 'new_m', 'new_m_ln_g': 'new_m', 'new_m_ln_b': 'new_m', 'new_m_mlp_w1': 'new_m', 'new_m_mlp_w2': 'new_m', 'new_m_ple_w_gate': 'new_m', 'new_m_ple_w_proj': 'new_m', 'new_v_gla_w_in': 'new_v', 'new_v_gla_w_gate_up': 'new_v', 'new_v_gla_b_gate': 'new_v', 'new_v_gla_norm_g': 'new_v', 'new_v_gla_w_out': 'new_v', 'new_v_mla_w_in': 'new_v', 'new_v_mla_q_norm': 'new_v', 'new_v_mla_kv_norm': 'new_v', 'new_v_mla_w_uq': 'new_v', 'new_v_mla_w_ukv': 'new_v', 'new_v_mla_w_out': 'new_v', 'new_v_conv_w_in': 'new_v', 'new_v_conv_w': 'new_v', 'new_v_conv_w_out': 'new_v', 'new_v_ln_g': 'new_v', 'new_v_ln_b': 'new_v', 'new_v_mlp_w1': 'new_v', 'new_v_mlp_w2': 'new_v', 'new_v_ple_w_gate': 'new_v', 'new_v_ple_w_proj': 'new_v'}


def _forward(args):
    return _fwd_reference(*[args[k] for k in FWD_PARAMS])


def _output_shape():
    out = _jax.eval_shape(lambda: _forward(_fwd_setup_inputs(0)))
    return out.shape, out.dtype

N_MICROBATCH = 1
ADAM_LR = 0.001
ADAM_B1 = 0.9
ADAM_B2 = 0.999
ADAM_EPS = 1e-08
ADAM_WD = 0.01
ADAM_STEP = 10
PER_EXAMPLE_BATCH_AXIS = {'x': 0, 'p': 1, 'positions': 0, 'loss_target': 0}
SHARED_INPUTS = []
_WEIGHT_DTYPES = {'gla_w_in': _jnp.float32, 'gla_w_gate_up': _jnp.float32, 'gla_b_gate': _jnp.float32, 'gla_norm_g': _jnp.float32, 'gla_w_out': _jnp.float32, 'mla_w_in': _jnp.float32, 'mla_q_norm': _jnp.float32, 'mla_kv_norm': _jnp.float32, 'mla_w_uq': _jnp.float32, 'mla_w_ukv': _jnp.float32, 'mla_w_out': _jnp.float32, 'conv_w_in': _jnp.float32, 'conv_w': _jnp.float32, 'conv_w_out': _jnp.float32, 'ln_g': _jnp.float32, 'ln_b': _jnp.float32, 'mlp_w1': _jnp.float32, 'mlp_w2': _jnp.float32, 'ple_w_gate': _jnp.float32, 'ple_w_proj': _jnp.float32}
MOMENT_SCALE = {'gla_w_in': 2.480642e-02, 'gla_w_gate_up': 3.785384e-03, 'gla_b_gate': 1.266188e-02, 'gla_norm_g': 4.251168e-02, 'gla_w_out': 5.030647e-02, 'mla_w_in': 2.532865e-02, 'mla_q_norm': 1.021765e-02, 'mla_kv_norm': 3.489880e-02, 'mla_w_uq': 3.987254e-03, 'mla_w_ukv': 1.374585e-02, 'mla_w_out': 4.751867e-02, 'conv_w_in': 4.405162e-02, 'conv_w': 4.362581e-02, 'conv_w_out': 1.046779e-01, 'ln_g': 5.828479e+00, 'ln_b': 1.660875e+00, 'mlp_w1': 2.621346e-02, 'mlp_w2': 3.368849e-01, 'ple_w_gate': 1.452534e-01, 'ple_w_proj': 1.350920e-01}


def _to_microbatches(a, axis):
    t = _jnp.moveaxis(a, axis, 0)
    t = t.reshape((N_MICROBATCH, t.shape[0] // N_MICROBATCH) + t.shape[1:])
    return _jnp.moveaxis(t, 1, axis + 1)


def setup_inputs(seed: int = 0) -> dict:
    inp = _fwd_setup_inputs(seed)
    key = _jax.random.fold_in(_jax.random.key(seed), 7919)
    shape, _ = _output_shape()
    out = dict(inp)
    out["loss_target"] = _jax.random.normal(_jax.random.fold_in(key, 0), shape, _jnp.float32)
    for i, name in enumerate(TWIN_WEIGHTS):
        w = inp[name].astype(_jnp.float32)
        if MOMENT_SCALE is None:
            s = _jnp.sqrt(_jnp.mean(_jnp.square(w)) + 1e-30)
        else:
            s = MOMENT_SCALE[name]
        km, kv = _jax.random.split(_jax.random.fold_in(key, i + 1))
        out[name] = w
        out["m_" + name] = s * _jax.random.normal(km, w.shape, _jnp.float32)
        out["v_" + name] = (s * s) * _jax.random.uniform(kv, w.shape, _jnp.float32, 0.5, 1.5)
    if N_MICROBATCH > 1:
        for name, axis in PER_EXAMPLE_BATCH_AXIS.items():
            out[name] = _to_microbatches(out[name], axis)
    return {'x': out['x'], 'p': out['p'], 'positions': out['positions'], 'gla_w_in': out['gla_w_in'], 'gla_w_gate_up': out['gla_w_gate_up'], 'gla_b_gate': out['gla_b_gate'], 'gla_norm_g': out['gla_norm_g'], 'gla_w_out': out['gla_w_out'], 'mla_w_in': out['mla_w_in'], 'mla_q_norm': out['mla_q_norm'], 'mla_kv_norm': out['mla_kv_norm'], 'mla_w_uq': out['mla_w_uq'], 'mla_w_ukv': out['mla_w_ukv'], 'mla_w_out': out['mla_w_out'], 'conv_w_in': out['conv_w_in'], 'conv_w': out['conv_w'], 'conv_w_out': out['conv_w_out'], 'ln_g': out['ln_g'], 'ln_b': out['ln_b'], 'mlp_w1': out['mlp_w1'], 'mlp_w2': out['mlp_w2'], 'ple_w_gate': out['ple_w_gate'], 'ple_w_proj': out['ple_w_proj'], 'loss_target': out['loss_target'], 'm_gla_w_in': out['m_gla_w_in'], 'm_gla_w_gate_up': out['m_gla_w_gate_up'], 'm_gla_b_gate': out['m_gla_b_gate'], 'm_gla_norm_g': out['m_gla_norm_g'], 'm_gla_w_out': out['m_gla_w_out'], 'm_mla_w_in': out['m_mla_w_in'], 'm_mla_q_norm': out['m_mla_q_norm'], 'm_mla_kv_norm': out['m_mla_kv_norm'], 'm_mla_w_uq': out['m_mla_w_uq'], 'm_mla_w_ukv': out['m_mla_w_ukv'], 'm_mla_w_out': out['m_mla_w_out'], 'm_conv_w_in': out['m_conv_w_in'], 'm_conv_w': out['m_conv_w'], 'm_conv_w_out': out['m_conv_w_out'], 'm_ln_g': out['m_ln_g'], 'm_ln_b': out['m_ln_b'], 'm_mlp_w1': out['m_mlp_w1'], 'm_mlp_w2': out['m_mlp_w2'], 'm_ple_w_gate': out['m_ple_w_gate'], 'm_ple_w_proj': out['m_ple_w_proj'], 'v_gla_w_in': out['v_gla_w_in'], 'v_gla_w_gate_up': out['v_gla_w_gate_up'], 'v_gla_b_gate': out['v_gla_b_gate'], 'v_gla_norm_g': out['v_gla_norm_g'], 'v_gla_w_out': out['v_gla_w_out'], 'v_mla_w_in': out['v_mla_w_in'], 'v_mla_q_norm': out['v_mla_q_norm'], 'v_mla_kv_norm': out['v_mla_kv_norm'], 'v_mla_w_uq': out['v_mla_w_uq'], 'v_mla_w_ukv': out['v_mla_w_ukv'], 'v_mla_w_out': out['v_mla_w_out'], 'v_conv_w_in': out['v_conv_w_in'], 'v_conv_w': out['v_conv_w'], 'v_conv_w_out': out['v_conv_w_out'], 'v_ln_g': out['v_ln_g'], 'v_ln_b': out['v_ln_b'], 'v_mlp_w1': out['v_mlp_w1'], 'v_mlp_w2': out['v_mlp_w2'], 'v_ple_w_gate': out['v_ple_w_gate'], 'v_ple_w_proj': out['v_ple_w_proj']}


def _loss(weights, diff, rest, loss_target):
    with _jax.named_scope("forward"):
        args = {**rest, TWIN_DIFF_INPUT: diff, **{k: w.astype(_WEIGHT_DTYPES[k]) for k, w in weights.items()}}
        y = _forward(args)
    with _jax.named_scope("loss_head"):
        err = _jnp.square(y.astype(_jnp.float32) - loss_target)
        return 0.5 * _jnp.sum(_jnp.mean(err, axis=-1)) if err.ndim else 0.5 * err


def _adamw(w, g, m, v):
    m = ADAM_B1 * m + (1.0 - ADAM_B1) * g
    v = ADAM_B2 * v + (1.0 - ADAM_B2) * _jnp.square(g)
    m_hat = m / (1.0 - ADAM_B1 ** ADAM_STEP)
    v_hat = v / (1.0 - ADAM_B2 ** ADAM_STEP)
    delta = -ADAM_LR * (m_hat / (_jnp.sqrt(v_hat) + ADAM_EPS) + ADAM_WD * w)
    return delta, m, v


def reference(x, p, positions, gla_w_in, gla_w_gate_up, gla_b_gate, gla_norm_g, gla_w_out, mla_w_in, mla_q_norm, mla_kv_norm, mla_w_uq, mla_w_ukv, mla_w_out, conv_w_in, conv_w, conv_w_out, ln_g, ln_b, mlp_w1, mlp_w2, ple_w_gate, ple_w_proj, loss_target, m_gla_w_in, m_gla_w_gate_up, m_gla_b_gate, m_gla_norm_g, m_gla_w_out, m_mla_w_in, m_mla_q_norm, m_mla_kv_norm, m_mla_w_uq, m_mla_w_ukv, m_mla_w_out, m_conv_w_in, m_conv_w, m_conv_w_out, m_ln_g, m_ln_b, m_mlp_w1, m_mlp_w2, m_ple_w_gate, m_ple_w_proj, v_gla_w_in, v_gla_w_gate_up, v_gla_b_gate, v_gla_norm_g, v_gla_w_out, v_mla_w_in, v_mla_q_norm, v_mla_kv_norm, v_mla_w_uq, v_mla_w_ukv, v_mla_w_out, v_conv_w_in, v_conv_w, v_conv_w_out, v_ln_g, v_ln_b, v_mlp_w1, v_mlp_w2, v_ple_w_gate, v_ple_w_proj):
    given = dict(x=x, p=p, positions=positions, gla_w_in=gla_w_in, gla_w_gate_up=gla_w_gate_up, gla_b_gate=gla_b_gate, gla_norm_g=gla_norm_g, gla_w_out=gla_w_out, mla_w_in=mla_w_in, mla_q_norm=mla_q_norm, mla_kv_norm=mla_kv_norm, mla_w_uq=mla_w_uq, mla_w_ukv=mla_w_ukv, mla_w_out=mla_w_out, conv_w_in=conv_w_in, conv_w=conv_w, conv_w_out=conv_w_out, ln_g=ln_g, ln_b=ln_b, mlp_w1=mlp_w1, mlp_w2=mlp_w2, ple_w_gate=ple_w_gate, ple_w_proj=ple_w_proj, loss_target=loss_target, m_gla_w_in=m_gla_w_in, m_gla_w_gate_up=m_gla_w_gate_up, m_gla_b_gate=m_gla_b_gate, m_gla_norm_g=m_gla_norm_g, m_gla_w_out=m_gla_w_out, m_mla_w_in=m_mla_w_in, m_mla_q_norm=m_mla_q_norm, m_mla_kv_norm=m_mla_kv_norm, m_mla_w_uq=m_mla_w_uq, m_mla_w_ukv=m_mla_w_ukv, m_mla_w_out=m_mla_w_out, m_conv_w_in=m_conv_w_in, m_conv_w=m_conv_w, m_conv_w_out=m_conv_w_out, m_ln_g=m_ln_g, m_ln_b=m_ln_b, m_mlp_w1=m_mlp_w1, m_mlp_w2=m_mlp_w2, m_ple_w_gate=m_ple_w_gate, m_ple_w_proj=m_ple_w_proj, v_gla_w_in=v_gla_w_in, v_gla_w_gate_up=v_gla_w_gate_up, v_gla_b_gate=v_gla_b_gate, v_gla_norm_g=v_gla_norm_g, v_gla_w_out=v_gla_w_out, v_mla_w_in=v_mla_w_in, v_mla_q_norm=v_mla_q_norm, v_mla_kv_norm=v_mla_kv_norm, v_mla_w_uq=v_mla_w_uq, v_mla_w_ukv=v_mla_w_ukv, v_mla_w_out=v_mla_w_out, v_conv_w_in=v_conv_w_in, v_conv_w=v_conv_w, v_conv_w_out=v_conv_w_out, v_ln_g=v_ln_g, v_ln_b=v_ln_b, v_mlp_w1=v_mlp_w1, v_mlp_w2=v_mlp_w2, v_ple_w_gate=v_ple_w_gate, v_ple_w_proj=v_ple_w_proj)
    weights = {n: given[n] for n in TWIN_WEIGHTS}
    shared = {n: given[n] for n in SHARED_INPUTS}
    per_example = {n: given[n] for n in ['x', 'p', 'positions']}
    grad_fn = _jax.value_and_grad(_loss, argnums=(0, 1))

    def one_microbatch(ex, loss_target):
        ex = dict(ex)
        diff = ex.pop(TWIN_DIFF_INPUT)
        return grad_fn(weights, diff, {**shared, **ex}, loss_target)

    if N_MICROBATCH == 1:
        loss, (grad_w, grad_x) = one_microbatch(per_example, given["loss_target"])
    else:
        def body(carry, xs):
            loss_sum, grad_sum = carry
            l_k, (gw_k, gx_k) = one_microbatch(xs[0], xs[1])
            with _jax.named_scope("update"):
                return (loss_sum + l_k, _jax.tree.map(_jnp.add, grad_sum, gw_k)), gx_k

        init = (_jnp.zeros((), _jnp.float32), _jax.tree.map(_jnp.zeros_like, weights))
        (loss, grad_w), grad_x = _jax.lax.scan(body, init, (per_example, given["loss_target"]))
    with _jax.named_scope("update"):
        delta_w, new_m, new_v = {}, {}, {}
        for n in TWIN_WEIGHTS:
            delta_w[n], new_m[n], new_v[n] = _adamw(weights[n], grad_w[n], given["m_" + n], given["v_" + n])
    return (loss, grad_x, *[grad_w[n] for n in TWIN_WEIGHTS], *[delta_w[n] for n in TWIN_WEIGHTS],
            *[new_m[n] for n in TWIN_WEIGHTS], *[new_v[n] for n in TWIN_WEIGHTS])
```

```python
import functools

import jax
import jax.numpy as jnp
from jax import lax
from jax.experimental import pallas as pl
from jax.experimental.pallas import tpu as pltpu

F32, BF16 = jnp.float32, jnp.bfloat16
HIGHEST = lax.Precision.HIGHEST
MESH_AXES = ("x", "y", "c")
N_DEV = 8

D_MODEL = 1024
SEQ = 2048
DEPTH = 4
CHUNK = 64
ALPHA = (2 * DEPTH) ** 0.25
LN_EPS = 1e-5
RMS_EPS = 1e-6
PLE_DIM = 256
D_FF = 4 * D_MODEL
GLA_HEADS = 4
GLA_DK = 128
GLA_DV = 256
GLA_RANK = 16
GLA_TAU = 16.0
GLA_HK = GLA_HEADS * GLA_DK
GLA_HV = GLA_HEADS * GLA_DV
GLA_MAIN = 2 * GLA_HK + GLA_HV + D_MODEL
MLA_HEADS = 8
MLA_NOPE = 128
MLA_ROPE = 64
MLA_V = 128
MLA_RANK = 256
MLA_IN = 2 * MLA_RANK + MLA_ROPE
MLA_IN_PAD = 640
ROPE_BASE = 10000.0
LANES = 128
ADAM_LR, ADAM_B1, ADAM_B2, ADAM_EPS, ADAM_WD, ADAM_STEP = 0.001, 0.9, 0.999, 1e-08, 0.01, 10

V7X_VMEM_LIMIT_BYTES = 56 * 1024 * 1024
PACK_COLS = 1024
PACK_ROW_TILE = 256

WEIGHTS = {
    "gla_w_in": ((2, 1024, 3088), 2), "gla_w_gate_up": ((2, 16, 512), 2), "gla_b_gate": ((2, 512), 1),
    "gla_norm_g": ((2, 256), 1), "gla_w_out": ((2, 1024, 1024), 1), "mla_w_in": ((1, 1024, 576), 1),
    "mla_q_norm": ((1, 256), None), "mla_kv_norm": ((1, 256), None), "mla_w_uq": ((1, 256, 1536), 2),
    "mla_w_ukv": ((1, 256, 2048), 2), "mla_w_out": ((1, 1024, 1024), 1), "conv_w_in": ((1, 1024, 3072), 2),
    "conv_w": ((1, 3, 1024), 2), "conv_w_out": ((1, 1024, 1024), 1), "ln_g": ((4, 2, 1024), 2),
    "ln_b": ((4, 2, 1024), 2), "mlp_w1": ((4, 1024, 4096), 2), "mlp_w2": ((4, 4096, 1024), 1),
    "ple_w_gate": ((4, 1024, 1024), 1), "ple_w_proj": ((4, 256, 1024), 2),
}
WEIGHT_NAMES = list(WEIGHTS)
BIG = ["gla_w_in", "gla_w_out", "mla_w_in", "mla_w_uq", "mla_w_ukv", "mla_w_out", "conv_w_in", "conv_w_out",
       "mlp_w1", "mlp_w2", "ple_w_gate", "ple_w_proj"]
SMALL = ["gla_w_gate_up", "gla_b_gate", "gla_norm_g", "conv_w", "ln_g", "ln_b"]
REPLICATED = ["mla_q_norm", "mla_kv_norm"]


def _params(**kw):
    return pltpu.CompilerParams(vmem_limit_bytes=V7X_VMEM_LIMIT_BYTES, **kw)


def _dot(a, b, ca, cb, precision=None):
    return lax.dot_general(a, b, (((ca,), (cb,)), ((), ())), precision=precision, preferred_element_type=F32)


def _nn(a, b):
    return _dot(a.astype(BF16), b.astype(BF16), 1, 0)


def _nt(a, b):
    return _dot(a.astype(BF16), b.astype(BF16), 1, 1)


def _tn(a, b):
    return _dot(a.astype(BF16), b.astype(BF16), 0, 0)


@jax.custom_vjp
def mm_nn(a, b):
    return _nn(a, b)


def _mm_nn_fwd(a, b):
    return _nn(a, b), (a, b)


def _mm_nn_bwd(res, g):
    a, b = res
    return _nt(g, b).astype(a.dtype), _tn(a, g).astype(b.dtype)


mm_nn.defvjp(_mm_nn_fwd, _mm_nn_bwd)


@jax.custom_vjp
def mm_nt(a, b):
    return _nt(a, b)


def _mm_nt_fwd(a, b):
    return _nt(a, b), (a, b)


def _mm_nt_bwd(res, g):
    a, b = res
    return _nn(g, b).astype(a.dtype), _tn(g, a).astype(b.dtype)


mm_nt.defvjp(_mm_nt_fwd, _mm_nt_bwd)


@jax.custom_vjp
def mm_tn(a, b):
    return _tn(a, b)


def _mm_tn_fwd(a, b):
    return _tn(a, b), (a, b)


def _mm_tn_bwd(res, g):
    a, b = res
    return _nt(b, g).astype(a.dtype), _nn(a, g).astype(b.dtype)


mm_tn.defvjp(_mm_tn_fwd, _mm_tn_bwd)


def _iota2(shape, dim):
    return lax.broadcasted_iota(jnp.int32, shape, dim)


@jax.custom_vjp
def cumsum_rows(x):
    n = x.shape[0]
    tri = (_iota2((n, n), 0) >= _iota2((n, n), 1)).astype(F32)
    return _dot(tri, x, 1, 0, precision=HIGHEST)


def _cumsum_fwd(x):
    return cumsum_rows(x), None


def _cumsum_bwd(_, g):
    n = g.shape[0]
    tri_t = (_iota2((n, n), 0) <= _iota2((n, n), 1)).astype(F32)
    return (_dot(tri_t, g, 1, 0, precision=HIGHEST),)


cumsum_rows.defvjp(_cumsum_fwd, _cumsum_bwd)


def _rot_matrix(transposed):
    i, j = _iota2((LANES, LANES), 0), _iota2((LANES, LANES), 1)
    if transposed:
        i, j = j, i
    half = MLA_ROPE // 2
    plus = (i == j - half) & (j >= half) & (j < MLA_ROPE)
    minus = (i == j + half) & (j < half)
    return plus.astype(F32) - minus.astype(F32)


@jax.custom_vjp
def rot_half(x):
    return _dot(x, _rot_matrix(False), 1, 0, precision=HIGHEST)


def _rot_fwd(x):
    return rot_half(x), None


def _rot_bwd(_, g):
    return (_dot(g, _rot_matrix(True), 1, 0, precision=HIGHEST),)


rot_half.defvjp(_rot_fwd, _rot_bwd)


def _shift_rows_raw(x, s):
    n = x.shape[0]
    row = _iota2(x.shape, 0)
    rolled = pltpu.roll(x, s % n, 0)
    keep = (row >= s) if s > 0 else (row < n + s)
    return jnp.where(keep, rolled, 0.0)


@functools.partial(jax.custom_vjp, nondiff_argnums=(1,))
def shift_rows(x, s):
    return _shift_rows_raw(x, s)


def _shift_fwd(x, s):
    return _shift_rows_raw(x, s), None


def _shift_bwd(s, _, g):
    return (_shift_rows_raw(g, -s),)


shift_rows.defvjp(_shift_fwd, _shift_bwd)


def _layer_norm(a, g, b):
    mu = jnp.mean(a, -1, keepdims=True)
    xc = a - mu
    var = jnp.mean(xc * xc, -1, keepdims=True)
    return xc * lax.rsqrt(var + LN_EPS) * g + b


def _rms_norm(a, g):
    return a * lax.rsqrt(jnp.mean(a * a, -1, keepdims=True) + RMS_EPS) * g


def _log_sigmoid(z):
    return jnp.minimum(z, 0.0) - jnp.log(1.0 + jnp.exp(-jnp.abs(z)))


def _matmul(a, b, *, name, ta=False, tb=False, tm=512, tn=512, a_fn=None, epi=None, epi_ins=(), out_dtypes=(BF16,)):
    m = a.shape[1] if ta else a.shape[0]
    k = a.shape[0] if ta else a.shape[1]
    n = b.shape[0] if tb else b.shape[1]
    assert (b.shape[1] if tb else b.shape[0]) == k, (name, a.shape, b.shape)
    tm, tn = min(tm, m), min(tn, n)
    assert m % tm == 0 and n % tn == 0, (name, m, n, tm, tn)
    a_spec = pl.BlockSpec((k, tm), lambda i, j: (0, i)) if ta else pl.BlockSpec((tm, k), lambda i, j: (i, 0))
    b_spec = pl.BlockSpec((tn, k), lambda i, j: (j, 0)) if tb else pl.BlockSpec((k, tn), lambda i, j: (0, j))
    e_specs = []
    for e in epi_ins:
        if e.shape == (1, n):
            e_specs.append(pl.BlockSpec((1, tn), lambda i, j: (0, j)))
        else:
            assert e.shape == (m, n), (name, e.shape, m, n)
            e_specs.append(pl.BlockSpec((tm, tn), lambda i, j: (i, j)))
    n_epi = len(epi_ins)
    ca, cb = (0 if ta else 1), (1 if tb else 0)

    def body(a_ref, b_ref, *rest):
        av = a_ref[...]
        if a_fn is not None:
            av = a_fn(av)
        acc = _dot(av.astype(BF16), b_ref[...].astype(BF16), ca, cb)
        outs = epi(acc, *[r[...] for r in rest[:n_epi]]) if epi is not None else (acc,)
        for o_ref, val in zip(rest[n_epi:], outs):
            o_ref[...] = val.astype(o_ref.dtype)

    outs = pl.pallas_call(
        body, name=name, grid=(m // tm, n // tn),
        in_specs=[a_spec, b_spec, *e_specs],
        out_specs=[pl.BlockSpec((tm, tn), lambda i, j: (i, j)) for _ in out_dtypes],
        out_shape=[jax.ShapeDtypeStruct((m, n), dt) for dt in out_dtypes],
        compiler_params=_params(),
    )(a, b, *epi_ins)
    return outs[0] if len(out_dtypes) == 1 else tuple(outs)


def _tile_fwd(f, tiled, params, out_dtypes, *, tm, name):
    t = tiled[0].shape[0]
    assert t % tm == 0
    out_avals = jax.eval_shape(f, *[jax.ShapeDtypeStruct((tm, x.shape[1]), F32) for x in tiled],
                               *[jax.ShapeDtypeStruct(p.shape, F32) for p in params])
    nt, npar = len(tiled), len(params)

    def body(*refs):
        ins = [r[...].astype(F32) for r in refs[:nt + npar]]
        outs = f(*ins)
        for o_ref, val in zip(refs[nt + npar:], outs):
            o_ref[...] = val.astype(o_ref.dtype)

    return pl.pallas_call(
        body, name=name, grid=(t // tm,),
        in_specs=[pl.BlockSpec((tm, x.shape[1]), lambda i: (i, 0)) for x in tiled]
        + [pl.BlockSpec(p.shape, lambda i: (0, 0)) for p in params],
        out_specs=[pl.BlockSpec((tm, o.shape[1]), lambda i: (i, 0)) for o in out_avals],
        out_shape=[jax.ShapeDtypeStruct((t, o.shape[1]), dt) for o, dt in zip(out_avals, out_dtypes)],
        compiler_params=_params(),
    )(*tiled, *params)


def _tile_bwd(f, tiled, params, cots, d_tiled_dtypes, *, tm, name, diff_tiled=None):
    t = tiled[0].shape[0]
    assert t % tm == 0
    nt, npar, nc = len(tiled), len(params), len(cots)
    diff_tiled = list(range(nt)) if diff_tiled is None else diff_tiled

    def body(*refs):
        ins = [r[...].astype(F32) for r in refs[:nt + npar]]
        cts = [r[...].astype(F32) for r in refs[nt + npar:nt + npar + nc]]
        o_refs = refs[nt + npar + nc:]
        _, vjp = jax.vjp(f, *ins)
        grads = vjp(tuple(cts))
        for o_ref, idx in zip(o_refs[:len(diff_tiled)], diff_tiled):
            o_ref[...] = grads[idx].astype(o_ref.dtype)
        p_refs = o_refs[len(diff_tiled):]

        @pl.when(pl.program_id(0) == 0)
        def _():
            for p_ref in p_refs:
                p_ref[...] = jnp.zeros_like(p_ref)

        for p_ref, gp in zip(p_refs, grads[nt:]):
            p_ref[...] += gp

    outs = pl.pallas_call(
        body, name=name, grid=(t // tm,),
        in_specs=[pl.BlockSpec((tm, x.shape[1]), lambda i: (i, 0)) for x in tiled]
        + [pl.BlockSpec(p.shape, lambda i: (0, 0)) for p in params]
        + [pl.BlockSpec((tm, c.shape[1]), lambda i: (i, 0)) for c in cots],
        out_specs=[pl.BlockSpec((tm, tiled[idx].shape[1]), lambda i: (i, 0)) for idx in diff_tiled]
        + [pl.BlockSpec(p.shape, lambda i: (0, 0)) for p in params],
        out_shape=[jax.ShapeDtypeStruct(tiled[idx].shape, dt) for idx, dt in zip(diff_tiled, d_tiled_dtypes)]
        + [jax.ShapeDtypeStruct(p.shape, F32) for p in params],
        compiler_params=_params(),
    )(*tiled, *params, *cots)
    return outs[:len(diff_tiled)], outs[len(diff_tiled):]


def _gla_head(q, k, v, r, z, g, st):
    c = q.shape[0]
    causal = _iota2((c, c), 0) >= _iota2((c, c), 1)
    la = _log_sigmoid(z) * (1.0 / GLA_TAU)
    big_l = cumsum_rows(la)
    ep, en = jnp.exp(big_l), jnp.exp(-big_l)
    qs = q * (GLA_DK ** -0.5)
    qp = qs * ep
    s = jnp.where(causal, mm_nt(qp, k * en), mm_nt(qs * en, k * ep))
    o = mm_nn(s, v) + mm_nt(qp, st)
    l_end = jnp.sum(la, axis=0, keepdims=True)
    st_new = st * jnp.exp(l_end) + mm_tn(v, k * jnp.exp(l_end - big_l))
    u = _rms_norm(o, g) * (r * jax.nn.sigmoid(r))
    return u, st_new


def _gla_slices(h):
    q = slice(GLA_DK * h, GLA_DK * (h + 1))
    k = slice(GLA_HK + GLA_DK * h, GLA_HK + GLA_DK * (h + 1))
    v = slice(2 * GLA_HK + GLA_DV * h, 2 * GLA_HK + GLA_DV * (h + 1))
    r = slice(2 * GLA_HK + GLA_HV + GLA_DV * h, 2 * GLA_HK + GLA_HV + GLA_DV * (h + 1))
    return q, k, v, r


def _gla_fwd(proj, z, norm_g):
    t = proj.shape[0]
    nc = t // CHUNK

    def body(proj_ref, z_ref, g_ref, u_ref, st_save_ref, st_ref):
        @pl.when(pl.program_id(0) == 0)
        def _():
            st_ref[...] = jnp.zeros_like(st_ref)

        g = g_ref[...]
        for h in range(GLA_HEADS):
            sq, sk, sv, sr = _gla_slices(h)
            st = st_ref[h]
            st_save_ref[0, h] = st
            u, st_new = _gla_head(proj_ref[:, sq].astype(F32), proj_ref[:, sk].astype(F32), proj_ref[:, sv].astype(F32),
                                  proj_ref[:, sr].astype(F32), z_ref[:, GLA_DK * h:GLA_DK * (h + 1)], g, st)
            u_ref[:, GLA_DV * h:GLA_DV * (h + 1)] = u.astype(u_ref.dtype)
            st_ref[h] = st_new

    return pl.pallas_call(
        body, name="gla_fwd", grid=(nc,),
        in_specs=[pl.BlockSpec((CHUNK, GLA_MAIN), lambda i: (i, 0)), pl.BlockSpec((CHUNK, GLA_HK), lambda i: (i, 0)),
                  pl.BlockSpec((1, GLA_DV), lambda i: (0, 0))],
        out_specs=[pl.BlockSpec((CHUNK, GLA_HV), lambda i: (i, 0)),
                   pl.BlockSpec((1, GLA_HEADS, GLA_DV, GLA_DK), lambda i: (i, 0, 0, 0))],
        out_shape=[jax.ShapeDtypeStruct((t, GLA_HV), BF16), jax.ShapeDtypeStruct((nc, GLA_HEADS, GLA_DV, GLA_DK), F32)],
        scratch_shapes=[pltpu.VMEM((GLA_HEADS, GLA_DV, GLA_DK), F32)],
        compiler_params=_params(),
    )(proj, z, norm_g)


def _gla_bwd(proj, z, norm_g, states, du):
    t = proj.shape[0]
    nc = t // CHUNK

    def body(proj_ref, z_ref, g_ref, st_in_ref, du_ref, dproj_ref, dz_ref, dg_ref, dst_ref):
        @pl.when(pl.program_id(0) == 0)
        def _():
            dst_ref[...] = jnp.zeros_like(dst_ref)
            dg_ref[...] = jnp.zeros_like(dg_ref)

        g = g_ref[...]
        for h in range(GLA_HEADS):
            sq, sk, sv, sr = _gla_slices(h)
            ins = (proj_ref[:, sq].astype(F32), proj_ref[:, sk].astype(F32), proj_ref[:, sv].astype(F32),
                   proj_ref[:, sr].astype(F32), z_ref[:, GLA_DK * h:GLA_DK * (h + 1)], g, st_in_ref[0, h])
            _, vjp = jax.vjp(_gla_head, *ins)
            dq, dk, dv, dr, dz, dg, dst = vjp((du_ref[:, GLA_DV * h:GLA_DV * (h + 1)], dst_ref[h]))
            dproj_ref[:, sq] = dq.astype(dproj_ref.dtype)
            dproj_ref[:, sk] = dk.astype(dproj_ref.dtype)
            dproj_ref[:, sv] = dv.astype(dproj_ref.dtype)
            dproj_ref[:, sr] = dr.astype(dproj_ref.dtype)
            dz_ref[:, GLA_DK * h:GLA_DK * (h + 1)] = dz
            dg_ref[...] += dg
            dst_ref[h] = dst

    rev = lambda i: (nc - 1 - i, 0)
    return pl.pallas_call(
        body, name="gla_bwd", grid=(nc,),
        in_specs=[pl.BlockSpec((CHUNK, GLA_MAIN), rev), pl.BlockSpec((CHUNK, GLA_HK), rev),
                  pl.BlockSpec((1, GLA_DV), lambda i: (0, 0)),
                  pl.BlockSpec((1, GLA_HEADS, GLA_DV, GLA_DK), lambda i: (nc - 1 - i, 0, 0, 0)),
                  pl.BlockSpec((CHUNK, GLA_HV), rev)],
        out_specs=[pl.BlockSpec((CHUNK, GLA_MAIN), rev), pl.BlockSpec((CHUNK, GLA_HK), rev),
                   pl.BlockSpec((1, GLA_DV), lambda i: (0, 0))],
        out_shape=[jax.ShapeDtypeStruct((t, GLA_MAIN), BF16), jax.ShapeDtypeStruct((t, GLA_HK), F32),
                   jax.ShapeDtypeStruct((1, GLA_DV), F32)],
        scratch_shapes=[pltpu.VMEM((GLA_HEADS, GLA_DV, GLA_DK), F32)],
        compiler_params=_params(),
    )(proj, z, norm_g, states, du)


def _mla_pre(cq, cos, sin, gq, gkv, w_uq, w_ukv):
    qlat = _rms_norm(cq[:, :MLA_RANK], gq)
    kvlat = _rms_norm(cq[:, MLA_RANK:2 * MLA_RANK], gkv)
    kr = cq[:, 2 * MLA_RANK:]
    scale = (MLA_NOPE + MLA_ROPE) ** -0.5
    q = mm_nn(qlat, w_uq) * scale
    kv = mm_nn(kvlat, w_ukv)
    n_nope = MLA_HEADS * MLA_NOPE
    ropes = []
    for h in range(MLA_HEADS):
        qr = q[:, n_nope + LANES * h:n_nope + LANES * (h + 1)]
        ropes.append(qr * cos + rot_half(qr) * sin)
    return q[:, :n_nope], jnp.concatenate(ropes, axis=1), kv, kr * cos + rot_half(kr) * sin


MLA_Q_TILE = 256


def _mla_attn_block(qn, qr, kv, kr, q0):
    tq, nk = qn.shape[0], kv.shape[0]
    s = mm_nt(qn, kv[:, :MLA_NOPE]) + mm_nt(qr, kr)
    visible = (_iota2((tq, nk), 1) // CHUNK) <= ((q0 + _iota2((tq, nk), 0)) // CHUNK)
    s = jnp.where(visible, s, -1e30)
    e = jnp.exp(s - jnp.max(s, -1, keepdims=True))
    p = e / jnp.sum(e, -1, keepdims=True)
    return mm_nn(p, kv[:, MLA_NOPE:])


def _mla_attn_fwd(qn, qr, kv, kr):
    t = qn.shape[0]

    def body(qn_ref, qr_ref, kv_ref, kr_ref, o_ref):
        for i in range(t // MLA_Q_TILE):
            rows = slice(i * MLA_Q_TILE, (i + 1) * MLA_Q_TILE)
            keys = slice(0, (i + 1) * MLA_Q_TILE)
            o = _mla_attn_block(qn_ref[rows, :].astype(F32), qr_ref[rows, :].astype(F32), kv_ref[keys, :].astype(F32),
                                kr_ref[keys, :].astype(F32), i * MLA_Q_TILE)
            o_ref[rows, :] = o.astype(o_ref.dtype)

    return pl.pallas_call(
        body, name="mla_attn_fwd", grid=(MLA_HEADS,),
        in_specs=[pl.BlockSpec((t, MLA_NOPE), lambda h: (0, h)), pl.BlockSpec((t, LANES), lambda h: (0, h)),
                  pl.BlockSpec((t, MLA_NOPE + MLA_V), lambda h: (0, h)), pl.BlockSpec((t, LANES), lambda h: (0, 0))],
        out_specs=pl.BlockSpec((t, MLA_V), lambda h: (0, h)),
        out_shape=jax.ShapeDtypeStruct((t, MLA_HEADS * MLA_V), BF16),
        compiler_params=_params(),
    )(qn, qr, kv, kr)


def _mla_attn_bwd(qn, qr, kv, kr, do):
    t = qn.shape[0]

    def body(qn_ref, qr_ref, kv_ref, kr_ref, do_ref, dqn_ref, dqr_ref, dkv_ref, dkr_ref):
        dkv_ref[...] = jnp.zeros_like(dkv_ref)

        @pl.when(pl.program_id(0) == 0)
        def _():
            dkr_ref[...] = jnp.zeros_like(dkr_ref)

        for i in range(t // MLA_Q_TILE):
            rows = slice(i * MLA_Q_TILE, (i + 1) * MLA_Q_TILE)
            keys = slice(0, (i + 1) * MLA_Q_TILE)
            f = functools.partial(_mla_attn_block, q0=i * MLA_Q_TILE)
            _, vjp = jax.vjp(f, qn_ref[rows, :].astype(F32), qr_ref[rows, :].astype(F32), kv_ref[keys, :].astype(F32),
                             kr_ref[keys, :].astype(F32))
            dqn, dqr, dkv, dkr = vjp(do_ref[rows, :].astype(F32))
            dqn_ref[rows, :] = dqn
            dqr_ref[rows, :] = dqr
            dkv_ref[keys, :] += dkv
            dkr_ref[keys, :] += dkr

    return pl.pallas_call(
        body, name="mla_attn_bwd", grid=(MLA_HEADS,),
        in_specs=[pl.BlockSpec((t, MLA_NOPE), lambda h: (0, h)), pl.BlockSpec((t, LANES), lambda h: (0, h)),
                  pl.BlockSpec((t, MLA_NOPE + MLA_V), lambda h: (0, h)), pl.BlockSpec((t, LANES), lambda h: (0, 0)),
                  pl.BlockSpec((t, MLA_V), lambda h: (0, h))],
        out_specs=[pl.BlockSpec((t, MLA_NOPE), lambda h: (0, h)), pl.BlockSpec((t, LANES), lambda h: (0, h)),
                   pl.BlockSpec((t, MLA_NOPE + MLA_V), lambda h: (0, h)), pl.BlockSpec((t, LANES), lambda h: (0, 0))],
        out_shape=[jax.ShapeDtypeStruct(qn.shape, F32), jax.ShapeDtypeStruct(qr.shape, F32),
                   jax.ShapeDtypeStruct(kv.shape, F32), jax.ShapeDtypeStruct(kr.shape, F32)],
        compiler_params=_params(),
    )(qn, qr, kv, kr, do)


def _rope_tables(pos_col, inv_freq_row):
    t = pos_col.shape[0]

    def body(pos_ref, f_ref, cos_ref, sin_ref):
        ang = pos_ref[...].astype(F32) * f_ref[...]
        live = _iota2(ang.shape, 1) < MLA_ROPE
        cos_ref[...] = jnp.where(live, jnp.cos(ang), 0.0)
        sin_ref[...] = jnp.where(live, jnp.sin(ang), 0.0)

    return pl.pallas_call(
        body, name="rope_tables", out_shape=[jax.ShapeDtypeStruct((t, LANES), F32)] * 2, compiler_params=_params(),
    )(pos_col, inv_freq_row)


CONV_COL_TILE = 256


def _conv_gate(b, c, u, w0, w1, w2):
    cu = c * u
    return b * (w2 * cu + w1 * shift_rows(cu, 1) + w0 * shift_rows(cu, 2))


def _conv_specs(t):
    nb = D_MODEL // CONV_COL_TILE
    return [pl.BlockSpec((t, CONV_COL_TILE), lambda j, part=part: (0, part * nb + j)) for part in range(3)]


def _conv_fwd(bcu, w):
    t = bcu.shape[0]

    def body(b_ref, c_ref, u_ref, w_ref, o_ref):
        o_ref[...] = _conv_gate(b_ref[...], c_ref[...], u_ref[...], w_ref[0:1, :], w_ref[1:2, :],
                                w_ref[2:3, :]).astype(o_ref.dtype)

    return pl.pallas_call(
        body, name="conv_fwd", grid=(D_MODEL // CONV_COL_TILE,),
        in_specs=_conv_specs(t) + [pl.BlockSpec((3, CONV_COL_TILE), lambda j: (0, j))],
        out_specs=pl.BlockSpec((t, CONV_COL_TILE), lambda j: (0, j)),
        out_shape=jax.ShapeDtypeStruct((t, D_MODEL), BF16), compiler_params=_params(),
    )(bcu, bcu, bcu, w)


def _conv_bwd(bcu, w, dout):
    t = bcu.shape[0]

    def body(b_ref, c_ref, u_ref, w_ref, do_ref, db_ref, dc_ref, du_ref, dw_ref):
        _, vjp = jax.vjp(_conv_gate, b_ref[...], c_ref[...], u_ref[...], w_ref[0:1, :], w_ref[1:2, :], w_ref[2:3, :])
        db, dc, du, dw0, dw1, dw2 = vjp(do_ref[...])
        db_ref[...] = db.astype(db_ref.dtype)
        dc_ref[...] = dc.astype(dc_ref.dtype)
        du_ref[...] = du.astype(du_ref.dtype)
        dw_ref[0:1, :] = dw0
        dw_ref[1:2, :] = dw1
        dw_ref[2:3, :] = dw2

    col = pl.BlockSpec((t, CONV_COL_TILE), lambda j: (0, j))
    return pl.pallas_call(
        body, name="conv_bwd", grid=(D_MODEL // CONV_COL_TILE,),
        in_specs=_conv_specs(t) + [pl.BlockSpec((3, CONV_COL_TILE), lambda j: (0, j)), col],
        out_specs=[col, col, col, pl.BlockSpec((3, CONV_COL_TILE), lambda j: (0, j))],
        out_shape=[jax.ShapeDtypeStruct((t, D_MODEL), BF16)] * 3 + [jax.ShapeDtypeStruct((3, D_MODEL), F32)],
        compiler_params=_params(),
    )(bcu, bcu, bcu, w, dout)


def _loss_head(y, target):
    t, d = y.shape
    tm = 256

    def body(y_ref, t_ref, loss_ref, dy_ref):
        @pl.when(pl.program_id(0) == 0)
        def _():
            loss_ref[...] = jnp.zeros_like(loss_ref)

        err = y_ref[...] - t_ref[...]
        dy_ref[...] = err * (1.0 / d)
        loss_ref[...] += 0.5 * jnp.sum(jnp.sum(err * err, axis=-1, keepdims=True) * (1.0 / d))

    tile = pl.BlockSpec((tm, d), lambda i: (i, 0))
    return pl.pallas_call(
        body, name="loss_head", grid=(t // tm,), in_specs=[tile, tile],
        out_specs=[pl.BlockSpec((8, LANES), lambda i: (0, 0)), tile],
        out_shape=[jax.ShapeDtypeStruct((8, LANES), F32), jax.ShapeDtypeStruct((t, d), F32)],
        compiler_params=_params(),
    )(y, target)


def _ln_epi(acc, res, g, b):
    a = ALPHA * res + acc
    return a, _layer_norm(a, g, b)


def _ln_fn(a, g, b):
    return (_layer_norm(a, g, b),)


def _relu_sq(h):
    r = jnp.maximum(h.astype(F32), 0.0)
    return r * r


def _pad_cols(w, n):
    return jnp.pad(w, ((0, 0), (0, n - w.shape[1])))


def _pad_rows(w, n):
    return jnp.pad(w, ((0, n - w.shape[0]), (0, 0)))


def _uq_to_kernel_layout(w_uq):
    w = w_uq.reshape(MLA_RANK, MLA_HEADS, MLA_NOPE + MLA_ROPE)
    nope = w[:, :, :MLA_NOPE].reshape(MLA_RANK, MLA_HEADS * MLA_NOPE)
    rope = jnp.pad(w[:, :, MLA_NOPE:], ((0, 0), (0, 0), (0, LANES - MLA_ROPE))).reshape(MLA_RANK, MLA_HEADS * LANES)
    return jnp.concatenate([nope, rope], axis=1)


def _uq_from_kernel_layout(w):
    nope = w[:, :MLA_HEADS * MLA_NOPE].reshape(MLA_RANK, MLA_HEADS, MLA_NOPE)
    rope = w[:, MLA_HEADS * MLA_NOPE:].reshape(MLA_RANK, MLA_HEADS, LANES)[:, :, :MLA_ROPE]
    return jnp.concatenate([nope, rope], axis=2).reshape(MLA_RANK, MLA_HEADS * (MLA_NOPE + MLA_ROPE))


def _local_step(x, p, positions, target, w):
    t = x.shape[0]
    freqs = ROPE_BASE ** (-jnp.arange(0, MLA_ROPE // 2, dtype=F32) * (2.0 / MLA_ROPE))
    freq_row = jnp.concatenate([freqs, freqs, jnp.zeros((LANES - MLA_ROPE,), F32)])[None, :]
    cos, sin = _rope_tables(positions.reshape(t, 1), freq_row)

    saved = []
    for i in range(DEPTH):
        j, kind = i // 3, i % 3
        s = {"x": x}
        if kind == 0:
            w_in = w["gla_w_in"][j]
            s["w_main"] = w_in[:, :GLA_MAIN]
            s["w_lr"] = _pad_cols(w_in[:, GLA_MAIN:], LANES)
            s["w_up"] = _pad_rows(w["gla_w_gate_up"][j], LANES).astype(BF16)
            s["proj"] = _matmul(x, s["w_main"], name="gla_proj", tn=1024)
            s["glr"] = _matmul(x, s["w_lr"], name="gla_lr", out_dtypes=(F32,))
            s["z"] = _matmul(s["glr"], s["w_up"], name="gla_gate", epi=lambda acc, b: (acc + b,),
                             epi_ins=(w["gla_b_gate"][j][None, :],), out_dtypes=(F32,))
            s["u"], s["states"] = _gla_fwd(s["proj"], s["z"], w["gla_norm_g"][j][None, :])
            w_out = w["gla_w_out"][j]
        elif kind == 1:
            s["w_in"] = _pad_cols(w["mla_w_in"][j], MLA_IN_PAD)
            s["w_uq"] = _uq_to_kernel_layout(w["mla_w_uq"][j])
            s["cq"] = _matmul(x, s["w_in"], name="mla_proj", tn=MLA_IN_PAD, out_dtypes=(F32,))
            s["pre_params"] = (w["mla_q_norm"][j][None, :], w["mla_kv_norm"][j][None, :], s["w_uq"], w["mla_w_ukv"][j])
            s["qn"], s["qr"], s["kv"], s["kr"] = _tile_fwd(_mla_pre, (s["cq"], cos, sin), s["pre_params"],
                                                           (BF16, BF16, BF16, BF16), tm=256, name="mla_pre_fwd")
            s["u"] = _mla_attn_fwd(s["qn"], s["qr"], s["kv"], s["kr"])
            w_out = w["mla_w_out"][j]
        else:
            s["bcu"] = _matmul(x, w["conv_w_in"][j], name="conv_proj", tn=1024, out_dtypes=(F32,))
            s["u"] = _conv_fwd(s["bcu"], w["conv_w"][j])
            w_out = w["conv_w_out"][j]
        s["w_out"] = w_out
        g0, b0 = w["ln_g"][i, 0][None, :], w["ln_b"][i, 0][None, :]
        g1, b1 = w["ln_g"][i, 1][None, :], w["ln_b"][i, 1][None, :]
        s["a1"], s["x1"] = _matmul(s["u"], w_out, name="mixer_out_ln", tm=256, tn=D_MODEL, epi=_ln_epi,
                                   epi_ins=(x, g0, b0), out_dtypes=(F32, F32))
        s["hh"] = _matmul(s["x1"], w["mlp_w1"][i], name="mlp_up", tn=1024)
        s["a2"], s["x2"] = _matmul(s["hh"], w["mlp_w2"][i], name="mlp_down_ln", tm=256, tn=D_MODEL, a_fn=_relu_sq,
                                   epi=_ln_epi, epi_ins=(s["x1"], g1, b1), out_dtypes=(F32, F32))
        s["pp"] = _matmul(p[i], w["ple_w_proj"][i], name="ple_proj", tn=1024)
        x, s["gt"] = _matmul(s["x2"], w["ple_w_gate"][i], name="ple_gate", tn=1024,
                             epi=lambda acc, xr, pp: (xr + jax.nn.sigmoid(acc) * pp.astype(F32), acc),
                             epi_ins=(s["x2"], s["pp"]), out_dtypes=(F32, BF16))
        saved.append(s)

    loss_part, dx = _loss_head(x, target)

    gw = {n: [None] * WEIGHTS[n][0][0] for n in WEIGHT_NAMES}
    ln_g_grads, ln_b_grads = [[None, None] for _ in range(DEPTH)], [[None, None] for _ in range(DEPTH)]
    for i in reversed(range(DEPTH)):
        j, kind = i // 3, i % 3
        s = saved[i]

        def ple_bwd(dxo, gt, pp):
            sg = jax.nn.sigmoid(gt)
            return dxo * sg, dxo * pp * sg * (1.0 - sg)

        d_pp, d_gt = _tile_fwd(ple_bwd, (dx, s["gt"], s["pp"]), (), (BF16, BF16), tm=256, name="ple_bwd")
        gw["ple_w_proj"][i] = _matmul(p[i], d_pp, name="ple_proj_dw", ta=True, tn=1024, out_dtypes=(F32,))
        gw["ple_w_gate"][i] = _matmul(s["x2"], d_gt, name="ple_gate_dw", ta=True, tn=1024, out_dtypes=(F32,))
        dx2 = _matmul(d_gt, w["ple_w_gate"][i], name="ple_gate_dx", tb=True, tn=1024, epi=lambda acc, r: (acc + r,),
                      epi_ins=(dx,), out_dtypes=(F32,))
        g1, b1 = w["ln_g"][i, 1][None, :], w["ln_b"][i, 1][None, :]
        (d_a2,), (ln_g_grads[i][1], ln_b_grads[i][1]) = _tile_bwd(_ln_fn, (s["a2"],), (g1, b1), (dx2,), (F32,), tm=256,
                                                                    name="ln_bwd")
        gw["mlp_w2"][i] = _matmul(s["hh"], d_a2, name="mlp_down_dw", ta=True, tn=1024, a_fn=_relu_sq, out_dtypes=(F32,))
        d_hh = _matmul(d_a2, w["mlp_w2"][i], name="mlp_down_dx", tb=True, tn=1024,
                       epi=lambda acc, hh: (acc * 2.0 * jnp.maximum(hh.astype(F32), 0.0),), epi_ins=(s["hh"],))
        gw["mlp_w1"][i] = _matmul(s["x1"], d_hh, name="mlp_up_dw", ta=True, tn=1024, out_dtypes=(F32,))
        dx1 = _matmul(d_hh, w["mlp_w1"][i], name="mlp_up_dx", tb=True, tm=256, tn=1024,
                      epi=lambda acc, r: (acc + ALPHA * r,), epi_ins=(d_a2,), out_dtypes=(F32,))
        g0, b0 = w["ln_g"][i, 0][None, :], w["ln_b"][i, 0][None, :]
        (d_a1,), (ln_g_grads[i][0], ln_b_grads[i][0]) = _tile_bwd(_ln_fn, (s["a1"],), (g0, b0), (dx1,), (F32,), tm=256,
                                                                    name="ln_bwd")
        w_out_name = ("gla_w_out", "mla_w_out", "conv_w_out")[kind]
        gw[w_out_name][j] = _matmul(s["u"], d_a1, name="mixer_out_dw", ta=True, tn=1024, out_dtypes=(F32,))
        du = _matmul(d_a1, s["w_out"], name="mixer_out_dx", tb=True, tn=1024, out_dtypes=(F32,))
        resid = lambda acc, r: (acc + ALPHA * r,)
        plus = lambda acc, r: (acc + r,)
        if kind == 0:
            dproj, dz, dg = _gla_bwd(s["proj"], s["z"], w["gla_norm_g"][j][None, :], s["states"], du)
            gw["gla_norm_g"][j] = dg[0]
            gw["gla_b_gate"][j] = _tile_bwd(lambda zz, b: (zz + b,), (s["z"],), (w["gla_b_gate"][j][None, :],), (dz,), (),
                                            tm=256, name="gla_bias_bwd", diff_tiled=[])[1][0][0]
            gw["gla_w_gate_up"][j] = _matmul(s["glr"], dz, name="gla_gate_dw", ta=True, out_dtypes=(F32,))[:GLA_RANK]
            dglr = _matmul(dz, s["w_up"], name="gla_gate_dx", tb=True, out_dtypes=(F32,))
            dw_main = _matmul(s["x"], dproj, name="gla_proj_dw", ta=True, tn=1024, out_dtypes=(F32,))
            dw_lr = _matmul(s["x"], dglr, name="gla_lr_dw", ta=True, out_dtypes=(F32,))[:, :GLA_RANK]
            gw["gla_w_in"][j] = jnp.concatenate([dw_main, dw_lr], axis=1)
            dx = _matmul(dproj, s["w_main"], name="gla_proj_dx", tb=True, tn=1024, epi=resid, epi_ins=(d_a1,),
                         out_dtypes=(F32,))
            dx = _matmul(dglr, s["w_lr"], name="gla_lr_dx", tb=True, tn=1024, epi=plus, epi_ins=(dx,), out_dtypes=(F32,))
        elif kind == 1:
            dqn, dqr, dkv, dkr = _mla_attn_bwd(s["qn"], s["qr"], s["kv"], s["kr"], du)
            (d_cq,), (dgq, dgkv, dw_uq, dw_ukv) = _tile_bwd(_mla_pre, (s["cq"], cos, sin), s["pre_params"],
                                                           (dqn, dqr, dkv, dkr), (BF16,), tm=256, name="mla_pre_bwd",
                                                           diff_tiled=[0])
            gw["mla_q_norm"][j], gw["mla_kv_norm"][j] = dgq[0], dgkv[0]
            gw["mla_w_uq"][j] = _uq_from_kernel_layout(dw_uq)
            gw["mla_w_ukv"][j] = dw_ukv
            gw["mla_w_in"][j] = _matmul(s["x"], d_cq, name="mla_proj_dw", ta=True, tn=MLA_IN_PAD,
                                        out_dtypes=(F32,))[:, :MLA_IN]
            dx = _matmul(d_cq, s["w_in"], name="mla_proj_dx", tb=True, tn=1024, epi=resid, epi_ins=(d_a1,),
                         out_dtypes=(F32,))
        else:
            db, dc, du_, dcw = _conv_bwd(s["bcu"], w["conv_w"][j], du)
            gw["conv_w"][j] = dcw
            dbcu = jnp.concatenate([db, dc, du_], axis=1)
            gw["conv_w_in"][j] = _matmul(s["x"], dbcu, name="conv_proj_dw", ta=True, tn=1024, out_dtypes=(F32,))
            dx = _matmul(dbcu, w["conv_w_in"][j], name="conv_proj_dx", tb=True, tn=1024, epi=resid, epi_ins=(d_a1,),
                         out_dtypes=(F32,))

    gw["ln_g"] = [jnp.concatenate([a, b], axis=0) for a, b in ln_g_grads]
    gw["ln_b"] = [jnp.concatenate([a, b], axis=0) for a, b in ln_b_grads]
    grads = {n: jnp.stack(gw[n]).astype(F32) for n in WEIGHT_NAMES}
    return loss_part, dx, grads


MESH_IDS = pl.DeviceIdType.MESH
ANY = pl.BlockSpec(memory_space=pl.ANY)


def _my_place():
    return lax.axis_index("x"), lax.axis_index("y"), lax.axis_index("c")


def _all_gather(x, name):
    r, c = x.shape

    def body(x_ref, out_ref, send_sems, recv_sems, local_sem):
        mx, my, mc = _my_place()
        me, sibling = (mx, my, mc), (mx, my, 1 - mc)
        chips = [(1 - mx, my), (mx, 1 - my), (1 - mx, 1 - my)]

        def block(px, py, pc):
            return out_ref.at[4 * px + 2 * py + pc]

        def copy(k, blk, to, src=None):
            return pltpu.make_async_remote_copy(
                src_ref=block(*blk) if src is None else src, dst_ref=block(*blk), send_sem=send_sems.at[k],
                recv_sem=recv_sems.at[k], device_id=to, device_id_type=MESH_IDS)

        mine = pltpu.make_async_copy(x_ref, block(*me), local_sem)
        mine.start()
        first = [copy(0, me, sibling, src=x_ref)] + [copy(1 + n, me, (*chip, mc), src=x_ref) for n, chip in enumerate(chips)]
        for cp in first:
            cp.start()
        passed = [copy(4 + n, (*chip, mc), sibling) for n, chip in enumerate(chips)]
        for n, chip in enumerate(chips):
            copy(1 + n, (*chip, mc), me).wait_recv()
            passed[n].start()
        copy(0, sibling, me).wait_recv()
        for n, chip in enumerate(chips):
            copy(4 + n, (*chip, 1 - mc), me).wait_recv()
        for cp in first + passed:
            cp.wait_send()
        mine.wait()

    return pl.pallas_call(
        body, name=name, out_shape=jax.ShapeDtypeStruct((N_DEV, r, c), x.dtype), in_specs=[ANY], out_specs=ANY,
        scratch_shapes=[pltpu.SemaphoreType.DMA((7,)), pltpu.SemaphoreType.DMA((7,)), pltpu.SemaphoreType.DMA(())],
    )(x)


def _exchange_cores(g):
    _, r, c = g.shape

    def body(g_ref, recv_ref, send_sems, recv_sems):
        mx, my, mc = _my_place()
        copies = [pltpu.make_async_remote_copy(
            src_ref=g_ref.at[2 * n + (1 - mc)], dst_ref=recv_ref.at[n], send_sem=send_sems.at[n],
            recv_sem=recv_sems.at[n], device_id=(mx, my, 1 - mc), device_id_type=MESH_IDS) for n in range(4)]
        for cp in copies:
            cp.start()
        for cp in copies:
            cp.wait()

    return pl.pallas_call(
        body, name="rs_exchange_cores", out_shape=jax.ShapeDtypeStruct((4, r, c), g.dtype), in_specs=[ANY], out_specs=ANY,
        scratch_shapes=[pltpu.SemaphoreType.DMA((4,)), pltpu.SemaphoreType.DMA((4,))],
    )(g)


def _exchange_chips(h):
    _, r, c = h.shape

    def body(h_ref, recv_ref, send_sems, recv_sems):
        mx, my, mc = _my_place()
        chips = [(1 - mx, my), (mx, 1 - my), (1 - mx, 1 - my)]
        copies = [pltpu.make_async_remote_copy(
            src_ref=h_ref.at[2 * cx + cy], dst_ref=recv_ref.at[n], send_sem=send_sems.at[n], recv_sem=recv_sems.at[n],
            device_id=(cx, cy, mc), device_id_type=MESH_IDS) for n, (cx, cy) in enumerate(chips)]
        for cp in copies:
            cp.start()
        for cp in copies:
            cp.wait()

    return pl.pallas_call(
        body, name="rs_exchange_chips", out_shape=jax.ShapeDtypeStruct((3, r, c), h.dtype), in_specs=[ANY], out_specs=ANY,
        scratch_shapes=[pltpu.SemaphoreType.DMA((3,)), pltpu.SemaphoreType.DMA((3,))],
    )(h)


def _pair_sum(g, recv, my_c):
    _, r, c = g.shape
    tr = min(PACK_ROW_TILE, r)

    def body(c_ref, g_ref, r_ref, o_ref):
        o_ref[...] = (g_ref[...].astype(F32) + r_ref[...].astype(F32)).astype(o_ref.dtype)

    return pl.pallas_call(
        body, name="rs_pair_sum", out_shape=jax.ShapeDtypeStruct((4, r, c), g.dtype),
        grid_spec=pltpu.PrefetchScalarGridSpec(
            num_scalar_prefetch=1, grid=(4, r // tr),
            in_specs=[pl.BlockSpec((1, tr, c), lambda n, i, cr: (2 * n + cr[0], i, 0)),
                      pl.BlockSpec((1, tr, c), lambda n, i, cr: (n, i, 0))],
            out_specs=pl.BlockSpec((1, tr, c), lambda n, i, cr: (n, i, 0))),
        compiler_params=_params(),
    )(my_c, g, recv)


def _chip_sum(h, recv, my_chip):
    _, r, c = h.shape
    tr = min(PACK_ROW_TILE, r)

    def body(j_ref, h_ref, r0_ref, r1_ref, r2_ref, o_ref):
        o_ref[...] = ((h_ref[0].astype(F32) + r0_ref[0].astype(F32)) + r1_ref[0].astype(F32)) + r2_ref[0].astype(F32)

    return pl.pallas_call(
        body, name="rs_chip_sum", out_shape=jax.ShapeDtypeStruct((r, c), F32),
        grid_spec=pltpu.PrefetchScalarGridSpec(
            num_scalar_prefetch=1, grid=(r // tr,),
            in_specs=[pl.BlockSpec((1, tr, c), lambda i, jr: (jr[0], i, 0))]
            + [pl.BlockSpec((1, tr, c), lambda i, jr, n=n: (n, i, 0)) for n in range(3)],
            out_specs=pl.BlockSpec((tr, c), lambda i, jr: (i, 0))),
        compiler_params=_params(),
    )(my_chip, h, recv, recv, recv)


def _sum_blocks(g):
    n, r, c = g.shape

    def body(g_ref, o_ref):
        acc = g_ref[0]
        for k in range(1, n):
            acc = acc + g_ref[k]
        o_ref[...] = acc

    return pl.pallas_call(body, name="sum_blocks", out_shape=jax.ShapeDtypeStruct((r, c), F32), compiler_params=_params())(g)


def _pack(flat_parts, cols, row_multiple, dtype):
    flat = jnp.concatenate([f.astype(dtype) for f in flat_parts])
    per_row_block = cols * row_multiple
    padded = -(-flat.shape[0] // per_row_block) * per_row_block
    return jnp.pad(flat, (0, padded - flat.shape[0])).reshape(padded // cols, cols)


def _pack_blocks(parts, cols, row_multiple, dtype):
    cat = jnp.concatenate([q.astype(dtype) for q in parts], axis=1)
    per_row_block = cols * row_multiple
    padded = -(-cat.shape[1] // per_row_block) * per_row_block
    return jnp.pad(cat, ((0, 0), (0, padded - cat.shape[1]))).reshape(N_DEV, padded // cols, cols)


def _shard_shape(name):
    shape, axis = WEIGHTS[name]
    if axis is None:
        return shape
    return tuple(s // N_DEV if a == axis else s for a, s in enumerate(shape))


def _size(shape):
    n = 1
    for s in shape:
        n *= s
    return n


def _unshard(blocks, name):
    _, axis = WEIGHTS[name]
    return jnp.concatenate([blocks[k] for k in range(N_DEV)], axis=axis)


def _to_blocks(full, name):
    _, axis = WEIGHTS[name]
    return jnp.stack(jnp.split(full, N_DEV, axis=axis)).reshape(N_DEV, -1)


def _unpack(buf_flat, names):
    out, off = {}, 0
    for n in names:
        shp = _shard_shape(n)
        out[n] = buf_flat[off:off + _size(shp)].reshape(shp)
        off += _size(shp)
    return out


def _adamw(w, g, m, v, name):
    shape = w.shape
    cols = shape[-1]
    rows = _size(shape) // cols
    tr = rows
    for cand in (512, 256, 128, 64, 32, 16, 8):
        if rows > cand and rows % cand == 0:
            tr = cand
            break

    def body(w_ref, g_ref, m_ref, v_ref, d_ref, mo_ref, vo_ref):
        gv = g_ref[...]
        m2 = ADAM_B1 * m_ref[...] + (1.0 - ADAM_B1) * gv
        v2 = ADAM_B2 * v_ref[...] + (1.0 - ADAM_B2) * (gv * gv)
        m_hat = m2 / (1.0 - ADAM_B1 ** ADAM_STEP)
        v_hat = v2 / (1.0 - ADAM_B2 ** ADAM_STEP)
        d_ref[...] = -ADAM_LR * (m_hat / (jnp.sqrt(v_hat) + ADAM_EPS) + ADAM_WD * w_ref[...])
        mo_ref[...] = m2
        vo_ref[...] = v2

    spec = pl.BlockSpec((tr, cols), lambda i: (i, 0))
    outs = pl.pallas_call(
        body, name="adamw_" + name, grid=(rows // tr,), in_specs=[spec] * 4, out_specs=[spec] * 3,
        out_shape=[jax.ShapeDtypeStruct((rows, cols), F32)] * 3, compiler_params=_params(),
    )(*[a.reshape(rows, cols) for a in (w, g, m, v)])
    return [o.reshape(shape) for o in outs]


def kernel(x, p, positions, gla_w_in, gla_w_gate_up, gla_b_gate, gla_norm_g, gla_w_out, mla_w_in, mla_q_norm, mla_kv_norm, mla_w_uq, mla_w_ukv, mla_w_out, conv_w_in, conv_w, conv_w_out, ln_g, ln_b, mlp_w1, mlp_w2, ple_w_gate, ple_w_proj, loss_target, m_gla_w_in, m_gla_w_gate_up, m_gla_b_gate, m_gla_norm_g, m_gla_w_out, m_mla_w_in, m_mla_q_norm, m_mla_kv_norm, m_mla_w_uq, m_mla_w_ukv, m_mla_w_out, m_conv_w_in, m_conv_w, m_conv_w_out, m_ln_g, m_ln_b, m_mlp_w1, m_mlp_w2, m_ple_w_gate, m_ple_w_proj, v_gla_w_in, v_gla_w_gate_up, v_gla_b_gate, v_gla_norm_g, v_gla_w_out, v_mla_w_in, v_mla_q_norm, v_mla_kv_norm, v_mla_w_uq, v_mla_w_ukv, v_mla_w_out, v_conv_w_in, v_conv_w, v_conv_w_out, v_ln_g, v_ln_b, v_mlp_w1, v_mlp_w2, v_ple_w_gate, v_ple_w_proj):
    args = locals()
    shard = {n: args[n] for n in WEIGHT_NAMES}
    mom = {n: args["m_" + n] for n in WEIGHT_NAMES}
    var = {n: args["v_" + n] for n in WEIGHT_NAMES}
    mx, my, mc = _my_place()

    big = _all_gather(_pack([shard[n].reshape(-1) for n in BIG], PACK_COLS, PACK_ROW_TILE, BF16), "ag_weights")
    small = _all_gather(_pack([shard[n].reshape(-1) for n in SMALL], LANES, 8, F32), "ag_small")
    big_flat, small_flat = big.reshape(N_DEV, -1), small.reshape(N_DEV, -1)
    full = {n: shard[n] for n in REPLICATED}
    for names, flat in ((BIG, big_flat), (SMALL, small_flat)):
        off = 0
        for n in names:
            shp = _shard_shape(n)
            full[n] = _unshard(flat[:, off:off + _size(shp)].reshape((N_DEV,) + shp), n)
            off += _size(shp)

    loss_part, grad_x, grads = _local_step(x[0], p[:, 0], positions[0], loss_target[0], full)
    loss = lax.psum(loss_part[0, 0], MESH_AXES)

    g_big = _pack_blocks([_to_blocks(grads[n], n) for n in BIG], PACK_COLS, PACK_ROW_TILE, BF16)
    from_sibling = _exchange_cores(g_big)
    h = _pair_sum(g_big, from_sibling, mc.astype(jnp.int32).reshape(1))
    from_chips = _exchange_chips(h)
    red_big = _chip_sum(h, from_chips, (2 * mx + my).astype(jnp.int32).reshape(1))
    small_parts = [grads[n].reshape(-1) for n in SMALL + REPLICATED]
    red_small = _sum_blocks(_all_gather(_pack(small_parts, LANES, 8, F32), "ag_small_grads")).reshape(-1)

    my_grads = _unpack(red_big.reshape(-1), BIG)
    off = 0
    dev = 4 * mx + 2 * my + mc
    for n in SMALL + REPLICATED:
        shape, axis = WEIGHTS[n]
        full_g = red_small[off:off + _size(shape)].reshape(shape)
        off += _size(shape)
        if axis is None:
            my_grads[n] = full_g
        else:
            width = shape[axis] // N_DEV
            my_grads[n] = lax.dynamic_slice_in_dim(full_g, dev * width, width, axis=axis)

    deltas, new_m, new_v = {}, {}, {}
    for n in WEIGHT_NAMES:
        deltas[n], new_m[n], new_v[n] = _adamw(shard[n], my_grads[n], mom[n], var[n], n)
    return (loss, grad_x[None], *[my_grads[n] for n in WEIGHT_NAMES], *[deltas[n] for n in WEIGHT_NAMES],
            *[new_m[n] for n in WEIGHT_NAMES], *[new_v[n] for n in WEIGHT_NAMES])
```

```python
import functools

import jax
import jax.numpy as jnp
from jax import lax
from jax.experimental import pallas as pl
from jax.experimental.pallas import tpu as pltpu

F32, BF16 = jnp.float32, jnp.bfloat16
HIGHEST = lax.Precision.HIGHEST
MESH_AXES = ("x", "y", "c")
N_DEV = 8

D_MODEL = 1024
SEQ = 2048
DEPTH = 4
CHUNK = 64
ALPHA = (2 * DEPTH) ** 0.25
LN_EPS = 1e-5
RMS_EPS = 1e-6
PLE_DIM = 256
D_FF = 4 * D_MODEL
GLA_HEADS = 4
GLA_DK = 128
GLA_DV = 256
GLA_RANK = 16
GLA_TAU = 16.0
GLA_HK = GLA_HEADS * GLA_DK
GLA_HV = GLA_HEADS * GLA_DV
GLA_MAIN = 2 * GLA_HK + GLA_HV + D_MODEL
MLA_HEADS = 8
MLA_NOPE = 128
MLA_ROPE = 64
MLA_V = 128
MLA_RANK = 256
MLA_IN = 2 * MLA_RANK + MLA_ROPE
MLA_IN_PAD = 640
ROPE_BASE = 10000.0
LANES = 128
ADAM_LR, ADAM_B1, ADAM_B2, ADAM_EPS, ADAM_WD, ADAM_STEP = 0.001, 0.9, 0.999, 1e-08, 0.01, 10

V7X_VMEM_LIMIT_BYTES = 56 * 1024 * 1024
PACK_COLS = 1024
PACK_ROW_TILE = 256

WEIGHTS = {
    "gla_w_in": ((2, 1024, 3088), 2), "gla_w_gate_up": ((2, 16, 512), 2), "gla_b_gate": ((2, 512), 1),
    "gla_norm_g": ((2, 256), 1), "gla_w_out": ((2, 1024, 1024), 1), "mla_w_in": ((1, 1024, 576), 1),
    "mla_q_norm": ((1, 256), None), "mla_kv_norm": ((1, 256), None), "mla_w_uq": ((1, 256, 1536), 2),
    "mla_w_ukv": ((1, 256, 2048), 2), "mla_w_out": ((1, 1024, 1024), 1), "conv_w_in": ((1, 1024, 3072), 2),
    "conv_w": ((1, 3, 1024), 2), "conv_w_out": ((1, 1024, 1024), 1), "ln_g": ((4, 2, 1024), 2),
    "ln_b": ((4, 2, 1024), 2), "mlp_w1": ((4, 1024, 4096), 2), "mlp_w2": ((4, 4096, 1024), 1),
    "ple_w_gate": ((4, 1024, 1024), 1), "ple_w_proj": ((4, 256, 1024), 2),
}
WEIGHT_NAMES = list(WEIGHTS)
REG_W2, REG_W1T, REG_WOUT, REG_CONV, REG_WG = (0, 512), (1, 512), (8, 128), (3, 384), (12, 128)
A_ROWS = 1664
REG_WPT = (0, 128)
GROUP_C = ["gla_w_in", "mla_w_in", "mla_w_uq", "mla_w_ukv"]
SMALL = ["gla_w_gate_up", "gla_b_gate", "gla_norm_g", "conv_w", "ln_g", "ln_b"]
REPLICATED = ["mla_q_norm", "mla_kv_norm"]


def _params(**kw):
    return pltpu.CompilerParams(vmem_limit_bytes=V7X_VMEM_LIMIT_BYTES, **kw)


def _dot(a, b, ca, cb, precision=None):
    return lax.dot_general(a, b, (((ca,), (cb,)), ((), ())), precision=precision, preferred_element_type=F32)


def _nn(a, b):
    return _dot(a.astype(BF16), b.astype(BF16), 1, 0)


def _nt(a, b):
    return _dot(a.astype(BF16), b.astype(BF16), 1, 1)


def _tn(a, b):
    return _dot(a.astype(BF16), b.astype(BF16), 0, 0)


@jax.custom_vjp
def mm_nn(a, b):
    return _nn(a, b)


def _mm_nn_fwd(a, b):
    return _nn(a, b), (a, b)


def _mm_nn_bwd(res, g):
    a, b = res
    return _nt(g, b).astype(a.dtype), _tn(a, g).astype(b.dtype)


mm_nn.defvjp(_mm_nn_fwd, _mm_nn_bwd)


@jax.custom_vjp
def mm_nt(a, b):
    return _nt(a, b)


def _mm_nt_fwd(a, b):
    return _nt(a, b), (a, b)


def _mm_nt_bwd(res, g):
    a, b = res
    return _nn(g, b).astype(a.dtype), _tn(g, a).astype(b.dtype)


mm_nt.defvjp(_mm_nt_fwd, _mm_nt_bwd)


@jax.custom_vjp
def mm_tn(a, b):
    return _tn(a, b)


def _mm_tn_fwd(a, b):
    return _tn(a, b), (a, b)


def _mm_tn_bwd(res, g):
    a, b = res
    return _nt(b, g).astype(a.dtype), _nn(a, g).astype(b.dtype)


mm_tn.defvjp(_mm_tn_fwd, _mm_tn_bwd)


def _iota2(shape, dim):
    return lax.broadcasted_iota(jnp.int32, shape, dim)


@jax.custom_vjp
def cumsum_rows(x):
    n = x.shape[0]
    tri = (_iota2((n, n), 0) >= _iota2((n, n), 1)).astype(F32)
    return _dot(tri, x, 1, 0, precision=HIGHEST)


def _cumsum_fwd(x):
    return cumsum_rows(x), None


def _cumsum_bwd(_, g):
    n = g.shape[0]
    tri_t = (_iota2((n, n), 0) <= _iota2((n, n), 1)).astype(F32)
    return (_dot(tri_t, g, 1, 0, precision=HIGHEST),)


cumsum_rows.defvjp(_cumsum_fwd, _cumsum_bwd)


def _rot_matrix(transposed):
    i, j = _iota2((LANES, LANES), 0), _iota2((LANES, LANES), 1)
    if transposed:
        i, j = j, i
    half = MLA_ROPE // 2
    plus = (i == j - half) & (j >= half) & (j < MLA_ROPE)
    minus = (i == j + half) & (j < half)
    return plus.astype(F32) - minus.astype(F32)


@jax.custom_vjp
def rot_half(x):
    return _dot(x, _rot_matrix(False), 1, 0, precision=HIGHEST)


def _rot_fwd(x):
    return rot_half(x), None


def _rot_bwd(_, g):
    return (_dot(g, _rot_matrix(True), 1, 0, precision=HIGHEST),)


rot_half.defvjp(_rot_fwd, _rot_bwd)


def _shift_rows_raw(x, s):
    n = x.shape[0]
    row = _iota2(x.shape, 0)
    rolled = pltpu.roll(x, s % n, 0)
    keep = (row >= s) if s > 0 else (row < n + s)
    return jnp.where(keep, rolled, 0.0)


@functools.partial(jax.custom_vjp, nondiff_argnums=(1,))
def shift_rows(x, s):
    return _shift_rows_raw(x, s)


def _shift_fwd(x, s):
    return _shift_rows_raw(x, s), None


def _shift_bwd(s, _, g):
    return (_shift_rows_raw(g, -s),)


shift_rows.defvjp(_shift_fwd, _shift_bwd)


def _layer_norm(a, g, b):
    mu = jnp.mean(a, -1, keepdims=True)
    xc = a - mu
    var = jnp.mean(xc * xc, -1, keepdims=True)
    return xc * lax.rsqrt(var + LN_EPS) * g + b


def _rms_norm(a, g):
    return a * lax.rsqrt(jnp.mean(a * a, -1, keepdims=True) + RMS_EPS) * g


def _log_sigmoid(z):
    return jnp.minimum(z, 0.0) - jnp.log(1.0 + jnp.exp(-jnp.abs(z)))


def _matmul(a, b, *, name, ta=False, tb=False, tm=512, tn=512, a_fn=None, epi=None, epi_ins=(), out_dtypes=(BF16,),
            b_at=None, out_at=None, out_buf=None):
    m = a.shape[1] if ta else a.shape[0]
    k = a.shape[0] if ta else a.shape[1]
    if b_at is None:
        n, kb = (b.shape[0], b.shape[1]) if tb else (b.shape[1], b.shape[0])
    else:
        lyr, rb, r = b_at
        n, kb = (N_DEV * r, b.shape[3]) if tb else (b.shape[3], N_DEV * r)
    assert kb == k, (name, a.shape, b.shape, k, kb)
    tm, tn = min(tm, m), min(tn, n)
    assert m % tm == 0 and n % tn == 0, (name, m, n, tm, tn)
    a_spec = pl.BlockSpec((k, tm), lambda i, j: (0, i)) if ta else pl.BlockSpec((tm, k), lambda i, j: (i, 0))
    if b_at is None:
        b_spec = pl.BlockSpec((tn, k), lambda i, j: (j, 0)) if tb else pl.BlockSpec((k, tn), lambda i, j: (0, j))
        load_b = lambda ref: ref[...]
    elif tb and tn == n:
        b_spec = pl.BlockSpec((N_DEV, 1, r, k), lambda i, j: (0, lyr, rb, 0))
        load_b = lambda ref: ref[:, 0].reshape(n, k)
    elif tb:
        assert tn == r, (name, tn, r)
        b_spec = pl.BlockSpec((1, 1, r, k), lambda i, j: (j, lyr, rb, 0))
        load_b = lambda ref: ref[0, 0]
    else:
        b_spec = pl.BlockSpec((N_DEV, 1, r, tn), lambda i, j: (0, lyr, rb, j))
        load_b = lambda ref: ref[:, 0].reshape(k, tn)
    e_specs = []
    for e in epi_ins:
        if e.shape == (1, n):
            e_specs.append(pl.BlockSpec((1, tn), lambda i, j: (0, j)))
        else:
            assert e.shape == (m, n), (name, e.shape, m, n)
            e_specs.append(pl.BlockSpec((tm, tn), lambda i, j: (i, j)))
    n_epi = len(epi_ins)
    ca, cb = (0 if ta else 1), (1 if tb else 0)
    operands = [a, b, *epi_ins]
    in_specs = [a_spec, b_spec, *e_specs]
    if out_at is None:
        out_specs = [pl.BlockSpec((tm, tn), lambda i, j: (i, j)) for _ in out_dtypes]
        out_shape = [jax.ShapeDtypeStruct((m, n), dt) for dt in out_dtypes]
        aliases, n_buf = {}, 0
    else:
        olyr, orb, orows = out_at
        assert len(out_dtypes) == 1 and orows % tm == 0 and m == N_DEV * orows and n == out_buf.shape[3], (name, m, n)
        per = orows // tm
        out_specs = [pl.BlockSpec((1, 1, tm, tn), lambda i, j: (i // per, olyr, orb * per + i % per, j))]
        out_shape = [jax.ShapeDtypeStruct(out_buf.shape, out_buf.dtype)]
        operands.append(out_buf)
        in_specs.append(pl.BlockSpec(memory_space=pl.ANY))
        aliases, n_buf = {len(operands) - 1: 0}, 1

    def body(a_ref, b_ref, *rest):
        av = a_ref[...]
        if a_fn is not None:
            av = a_fn(av)
        acc = _dot(av.astype(BF16), load_b(b_ref).astype(BF16), ca, cb)
        outs = epi(acc, *[r_[...] for r_ in rest[:n_epi]]) if epi is not None else (acc,)
        for o_ref, val in zip(rest[n_epi + n_buf:], outs):
            o_ref[...] = val.astype(o_ref.dtype).reshape(o_ref.shape)

    outs = pl.pallas_call(
        body, name=name, grid=(m // tm, n // tn), in_specs=in_specs, out_specs=out_specs, out_shape=out_shape,
        input_output_aliases=aliases, compiler_params=_params(),
    )(*operands)
    return outs[0] if len(outs) == 1 else tuple(outs)


def _tile_fwd(f, tiled, params, out_dtypes, *, tm, name):
    t = tiled[0].shape[0]
    assert t % tm == 0
    out_avals = jax.eval_shape(f, *[jax.ShapeDtypeStruct((tm, x.shape[1]), F32) for x in tiled],
                               *[jax.ShapeDtypeStruct(p.shape, F32) for p in params])
    nt, npar = len(tiled), len(params)

    def body(*refs):
        ins = [r[...].astype(F32) for r in refs[:nt + npar]]
        outs = f(*ins)
        for o_ref, val in zip(refs[nt + npar:], outs):
            o_ref[...] = val.astype(o_ref.dtype)

    return pl.pallas_call(
        body, name=name, grid=(t // tm,),
        in_specs=[pl.BlockSpec((tm, x.shape[1]), lambda i: (i, 0)) for x in tiled]
        + [pl.BlockSpec(p.shape, lambda i: (0, 0)) for p in params],
        out_specs=[pl.BlockSpec((tm, o.shape[1]), lambda i: (i, 0)) for o in out_avals],
        out_shape=[jax.ShapeDtypeStruct((t, o.shape[1]), dt) for o, dt in zip(out_avals, out_dtypes)],
        compiler_params=_params(),
    )(*tiled, *params)


def _tile_bwd(f, tiled, params, cots, d_tiled_dtypes, *, tm, name, diff_tiled=None):
    t = tiled[0].shape[0]
    assert t % tm == 0
    nt, npar, nc = len(tiled), len(params), len(cots)
    diff_tiled = list(range(nt)) if diff_tiled is None else diff_tiled

    def body(*refs):
        ins = [r[...].astype(F32) for r in refs[:nt + npar]]
        cts = [r[...].astype(F32) for r in refs[nt + npar:nt + npar + nc]]
        o_refs = refs[nt + npar + nc:]
        _, vjp = jax.vjp(f, *ins)
        grads = vjp(tuple(cts))
        for o_ref, idx in zip(o_refs[:len(diff_tiled)], diff_tiled):
            o_ref[...] = grads[idx].astype(o_ref.dtype)
        p_refs = o_refs[len(diff_tiled):]

        @pl.when(pl.program_id(0) == 0)
        def _():
            for p_ref in p_refs:
                p_ref[...] = jnp.zeros_like(p_ref)

        for p_ref, gp in zip(p_refs, grads[nt:]):
            p_ref[...] += gp

    outs = pl.pallas_call(
        body, name=name, grid=(t // tm,),
        in_specs=[pl.BlockSpec((tm, x.shape[1]), lambda i: (i, 0)) for x in tiled]
        + [pl.BlockSpec(p.shape, lambda i: (0, 0)) for p in params]
        + [pl.BlockSpec((tm, c.shape[1]), lambda i: (i, 0)) for c in cots],
        out_specs=[pl.BlockSpec((tm, tiled[idx].shape[1]), lambda i: (i, 0)) for idx in diff_tiled]
        + [pl.BlockSpec(p.shape, lambda i: (0, 0)) for p in params],
        out_shape=[jax.ShapeDtypeStruct(tiled[idx].shape, dt) for idx, dt in zip(diff_tiled, d_tiled_dtypes)]
        + [jax.ShapeDtypeStruct(p.shape, F32) for p in params],
        compiler_params=_params(),
    )(*tiled, *params, *cots)
    return outs[:len(diff_tiled)], outs[len(diff_tiled):]


def _gla_head(q, k, v, r, z, g, st):
    c = q.shape[0]
    causal = _iota2((c, c), 0) >= _iota2((c, c), 1)
    la = _log_sigmoid(z) * (1.0 / GLA_TAU)
    big_l = cumsum_rows(la)
    ep, en = jnp.exp(big_l), jnp.exp(-big_l)
    qs = q * (GLA_DK ** -0.5)
    qp = qs * ep
    s = jnp.where(causal, mm_nt(qp, k * en), mm_nt(qs * en, k * ep))
    o = mm_nn(s, v) + mm_nt(qp, st)
    l_end = jnp.sum(la, axis=0, keepdims=True)
    st_new = st * jnp.exp(l_end) + mm_tn(v, k * jnp.exp(l_end - big_l))
    u = _rms_norm(o, g) * (r * jax.nn.sigmoid(r))
    return u, st_new


def _gla_slices(h):
    q = slice(GLA_DK * h, GLA_DK * (h + 1))
    k = slice(GLA_HK + GLA_DK * h, GLA_HK + GLA_DK * (h + 1))
    v = slice(2 * GLA_HK + GLA_DV * h, 2 * GLA_HK + GLA_DV * (h + 1))
    r = slice(2 * GLA_HK + GLA_HV + GLA_DV * h, 2 * GLA_HK + GLA_HV + GLA_DV * (h + 1))
    return q, k, v, r


def _gla_fwd(proj, z, norm_g):
    t = proj.shape[0]
    nc = t // CHUNK

    def body(proj_ref, z_ref, g_ref, u_ref, st_save_ref, st_ref):
        @pl.when(pl.program_id(0) == 0)
        def _():
            st_ref[...] = jnp.zeros_like(st_ref)

        g = g_ref[...]
        for h in range(GLA_HEADS):
            sq, sk, sv, sr = _gla_slices(h)
            st = st_ref[h]
            st_save_ref[0, h] = st
            u, st_new = _gla_head(proj_ref[:, sq].astype(F32), proj_ref[:, sk].astype(F32), proj_ref[:, sv].astype(F32),
                                  proj_ref[:, sr].astype(F32), z_ref[:, GLA_DK * h:GLA_DK * (h + 1)], g, st)
            u_ref[:, GLA_DV * h:GLA_DV * (h + 1)] = u.astype(u_ref.dtype)
            st_ref[h] = st_new

    return pl.pallas_call(
        body, name="gla_fwd", grid=(nc,),
        in_specs=[pl.BlockSpec((CHUNK, GLA_MAIN), lambda i: (i, 0)), pl.BlockSpec((CHUNK, GLA_HK), lambda i: (i, 0)),
                  pl.BlockSpec((1, GLA_DV), lambda i: (0, 0))],
        out_specs=[pl.BlockSpec((CHUNK, GLA_HV), lambda i: (i, 0)),
                   pl.BlockSpec((1, GLA_HEADS, GLA_DV, GLA_DK), lambda i: (i, 0, 0, 0))],
        out_shape=[jax.ShapeDtypeStruct((t, GLA_HV), BF16), jax.ShapeDtypeStruct((nc, GLA_HEADS, GLA_DV, GLA_DK), F32)],
        scratch_shapes=[pltpu.VMEM((GLA_HEADS, GLA_DV, GLA_DK), F32)],
        compiler_params=_params(),
    )(proj, z, norm_g)


def _gla_bwd(proj, z, norm_g, states, du):
    t = proj.shape[0]
    nc = t // CHUNK

    def body(proj_ref, z_ref, g_ref, st_in_ref, du_ref, dproj_ref, dz_ref, dg_ref, dst_ref):
        @pl.when(pl.program_id(0) == 0)
        def _():
            dst_ref[...] = jnp.zeros_like(dst_ref)
            dg_ref[...] = jnp.zeros_like(dg_ref)

        g = g_ref[...]
        for h in range(GLA_HEADS):
            sq, sk, sv, sr = _gla_slices(h)
            ins = (proj_ref[:, sq].astype(F32), proj_ref[:, sk].astype(F32), proj_ref[:, sv].astype(F32),
                   proj_ref[:, sr].astype(F32), z_ref[:, GLA_DK * h:GLA_DK * (h + 1)], g, st_in_ref[0, h])
            _, vjp = jax.vjp(_gla_head, *ins)
            dq, dk, dv, dr, dz, dg, dst = vjp((du_ref[:, GLA_DV * h:GLA_DV * (h + 1)], dst_ref[h]))
            dproj_ref[:, sq] = dq.astype(dproj_ref.dtype)
            dproj_ref[:, sk] = dk.astype(dproj_ref.dtype)
            dproj_ref[:, sv] = dv.astype(dproj_ref.dtype)
            dproj_ref[:, sr] = dr.astype(dproj_ref.dtype)
            dz_ref[:, GLA_DK * h:GLA_DK * (h + 1)] = dz
            dg_ref[...] += dg
            dst_ref[h] = dst

    rev = lambda i: (nc - 1 - i, 0)
    return pl.pallas_call(
        body, name="gla_bwd", grid=(nc,),
        in_specs=[pl.BlockSpec((CHUNK, GLA_MAIN), rev), pl.BlockSpec((CHUNK, GLA_HK), rev),
                  pl.BlockSpec((1, GLA_DV), lambda i: (0, 0)),
                  pl.BlockSpec((1, GLA_HEADS, GLA_DV, GLA_DK), lambda i: (nc - 1 - i, 0, 0, 0)),
                  pl.BlockSpec((CHUNK, GLA_HV), rev)],
        out_specs=[pl.BlockSpec((CHUNK, GLA_MAIN), rev), pl.BlockSpec((CHUNK, GLA_HK), rev),
                   pl.BlockSpec((1, GLA_DV), lambda i: (0, 0))],
        out_shape=[jax.ShapeDtypeStruct((t, GLA_MAIN), BF16), jax.ShapeDtypeStruct((t, GLA_HK), F32),
                   jax.ShapeDtypeStruct((1, GLA_DV), F32)],
        scratch_shapes=[pltpu.VMEM((GLA_HEADS, GLA_DV, GLA_DK), F32)],
        compiler_params=_params(),
    )(proj, z, norm_g, states, du)


def _mla_pre(cq, cos, sin, gq, gkv, w_uq, w_ukv):
    qlat = _rms_norm(cq[:, :MLA_RANK], gq)
    kvlat = _rms_norm(cq[:, MLA_RANK:2 * MLA_RANK], gkv)
    kr = cq[:, 2 * MLA_RANK:]
    scale = (MLA_NOPE + MLA_ROPE) ** -0.5
    q = mm_nn(qlat, w_uq) * scale
    kv = mm_nn(kvlat, w_ukv)
    n_nope = MLA_HEADS * MLA_NOPE
    ropes = []
    for h in range(MLA_HEADS):
        qr = q[:, n_nope + LANES * h:n_nope + LANES * (h + 1)]
        ropes.append(qr * cos + rot_half(qr) * sin)
    return q[:, :n_nope], jnp.concatenate(ropes, axis=1), kv, kr * cos + rot_half(kr) * sin


MLA_Q_TILE = 256


def _mla_attn_block(qn, qr, kv, kr, q0):
    tq, nk = qn.shape[0], kv.shape[0]
    s = mm_nt(qn, kv[:, :MLA_NOPE]) + mm_nt(qr, kr)
    visible = (_iota2((tq, nk), 1) // CHUNK) <= ((q0 + _iota2((tq, nk), 0)) // CHUNK)
    s = jnp.where(visible, s, -1e30)
    e = jnp.exp(s - jnp.max(s, -1, keepdims=True))
    p = e / jnp.sum(e, -1, keepdims=True)
    return mm_nn(p, kv[:, MLA_NOPE:])


def _mla_attn_fwd(qn, qr, kv, kr):
    t = qn.shape[0]

    def body(qn_ref, qr_ref, kv_ref, kr_ref, o_ref):
        for i in range(t // MLA_Q_TILE):
            rows = slice(i * MLA_Q_TILE, (i + 1) * MLA_Q_TILE)
            keys = slice(0, (i + 1) * MLA_Q_TILE)
            o = _mla_attn_block(qn_ref[rows, :].astype(F32), qr_ref[rows, :].astype(F32), kv_ref[keys, :].astype(F32),
                                kr_ref[keys, :].astype(F32), i * MLA_Q_TILE)
            o_ref[rows, :] = o.astype(o_ref.dtype)

    return pl.pallas_call(
        body, name="mla_attn_fwd", grid=(MLA_HEADS,),
        in_specs=[pl.BlockSpec((t, MLA_NOPE), lambda h: (0, h)), pl.BlockSpec((t, LANES), lambda h: (0, h)),
                  pl.BlockSpec((t, MLA_NOPE + MLA_V), lambda h: (0, h)), pl.BlockSpec((t, LANES), lambda h: (0, 0))],
        out_specs=pl.BlockSpec((t, MLA_V), lambda h: (0, h)),
        out_shape=jax.ShapeDtypeStruct((t, MLA_HEADS * MLA_V), BF16),
        compiler_params=_params(),
    )(qn, qr, kv, kr)


def _mla_attn_bwd(qn, qr, kv, kr, do):
    t = qn.shape[0]

    def body(qn_ref, qr_ref, kv_ref, kr_ref, do_ref, dqn_ref, dqr_ref, dkv_ref, dkr_ref):
        dkv_ref[...] = jnp.zeros_like(dkv_ref)

        @pl.when(pl.program_id(0) == 0)
        def _():
            dkr_ref[...] = jnp.zeros_like(dkr_ref)

        for i in range(t // MLA_Q_TILE):
            rows = slice(i * MLA_Q_TILE, (i + 1) * MLA_Q_TILE)
            keys = slice(0, (i + 1) * MLA_Q_TILE)
            f = functools.partial(_mla_attn_block, q0=i * MLA_Q_TILE)
            _, vjp = jax.vjp(f, qn_ref[rows, :].astype(F32), qr_ref[rows, :].astype(F32), kv_ref[keys, :].astype(F32),
                             kr_ref[keys, :].astype(F32))
            dqn, dqr, dkv, dkr = vjp(do_ref[rows, :].astype(F32))
            dqn_ref[rows, :] = dqn
            dqr_ref[rows, :] = dqr
            dkv_ref[keys, :] += dkv
            dkr_ref[keys, :] += dkr

    return pl.pallas_call(
        body, name="mla_attn_bwd", grid=(MLA_HEADS,),
        in_specs=[pl.BlockSpec((t, MLA_NOPE), lambda h: (0, h)), pl.BlockSpec((t, LANES), lambda h: (0, h)),
                  pl.BlockSpec((t, MLA_NOPE + MLA_V), lambda h: (0, h)), pl.BlockSpec((t, LANES), lambda h: (0, 0)),
                  pl.BlockSpec((t, MLA_V), lambda h: (0, h))],
        out_specs=[pl.BlockSpec((t, MLA_NOPE), lambda h: (0, h)), pl.BlockSpec((t, LANES), lambda h: (0, h)),
                   pl.BlockSpec((t, MLA_NOPE + MLA_V), lambda h: (0, h)), pl.BlockSpec((t, LANES), lambda h: (0, 0))],
        out_shape=[jax.ShapeDtypeStruct(qn.shape, F32), jax.ShapeDtypeStruct(qr.shape, F32),
                   jax.ShapeDtypeStruct(kv.shape, F32), jax.ShapeDtypeStruct(kr.shape, F32)],
        compiler_params=_params(),
    )(qn, qr, kv, kr, do)


def _rope_tables(pos_col, inv_freq_row):
    t = pos_col.shape[0]

    def body(pos_ref, f_ref, cos_ref, sin_ref):
        ang = pos_ref[...].astype(F32) * f_ref[...]
        live = _iota2(ang.shape, 1) < MLA_ROPE
        cos_ref[...] = jnp.where(live, jnp.cos(ang), 0.0)
        sin_ref[...] = jnp.where(live, jnp.sin(ang), 0.0)

    return pl.pallas_call(
        body, name="rope_tables", out_shape=[jax.ShapeDtypeStruct((t, LANES), F32)] * 2, compiler_params=_params(),
    )(pos_col, inv_freq_row)


CONV_COL_TILE = 256


def _conv_gate(b, c, u, w0, w1, w2):
    cu = c * u
    return b * (w2 * cu + w1 * shift_rows(cu, 1) + w0 * shift_rows(cu, 2))


def _conv_specs(t):
    nb = D_MODEL // CONV_COL_TILE
    return [pl.BlockSpec((t, CONV_COL_TILE), lambda j, part=part: (0, part * nb + j)) for part in range(3)]


def _conv_fwd(bcu, w):
    t = bcu.shape[0]

    def body(b_ref, c_ref, u_ref, w_ref, o_ref):
        o_ref[...] = _conv_gate(b_ref[...], c_ref[...], u_ref[...], w_ref[0:1, :], w_ref[1:2, :],
                                w_ref[2:3, :]).astype(o_ref.dtype)

    return pl.pallas_call(
        body, name="conv_fwd", grid=(D_MODEL // CONV_COL_TILE,),
        in_specs=_conv_specs(t) + [pl.BlockSpec((3, CONV_COL_TILE), lambda j: (0, j))],
        out_specs=pl.BlockSpec((t, CONV_COL_TILE), lambda j: (0, j)),
        out_shape=jax.ShapeDtypeStruct((t, D_MODEL), BF16), compiler_params=_params(),
    )(bcu, bcu, bcu, w)


def _conv_bwd(bcu, w, dout):
    t = bcu.shape[0]

    def body(b_ref, c_ref, u_ref, w_ref, do_ref, db_ref, dc_ref, du_ref, dw_ref):
        _, vjp = jax.vjp(_conv_gate, b_ref[...], c_ref[...], u_ref[...], w_ref[0:1, :], w_ref[1:2, :], w_ref[2:3, :])
        db, dc, du, dw0, dw1, dw2 = vjp(do_ref[...])
        db_ref[...] = db.astype(db_ref.dtype)
        dc_ref[...] = dc.astype(dc_ref.dtype)
        du_ref[...] = du.astype(du_ref.dtype)
        dw_ref[0:1, :] = dw0
        dw_ref[1:2, :] = dw1
        dw_ref[2:3, :] = dw2

    col = pl.BlockSpec((t, CONV_COL_TILE), lambda j: (0, j))
    return pl.pallas_call(
        body, name="conv_bwd", grid=(D_MODEL // CONV_COL_TILE,),
        in_specs=_conv_specs(t) + [pl.BlockSpec((3, CONV_COL_TILE), lambda j: (0, j)), col],
        out_specs=[col, col, col, pl.BlockSpec((3, CONV_COL_TILE), lambda j: (0, j))],
        out_shape=[jax.ShapeDtypeStruct((t, D_MODEL), BF16)] * 3 + [jax.ShapeDtypeStruct((3, D_MODEL), F32)],
        compiler_params=_params(),
    )(bcu, bcu, bcu, w, dout)


def _loss_head(y, target):
    t, d = y.shape
    tm = 256

    def body(y_ref, t_ref, loss_ref, dy_ref):
        @pl.when(pl.program_id(0) == 0)
        def _():
            loss_ref[...] = jnp.zeros_like(loss_ref)

        err = y_ref[...] - t_ref[...]
        dy_ref[...] = err * (1.0 / d)
        loss_ref[...] += 0.5 * jnp.sum(jnp.sum(err * err, axis=-1, keepdims=True) * (1.0 / d))

    tile = pl.BlockSpec((tm, d), lambda i: (i, 0))
    return pl.pallas_call(
        body, name="loss_head", grid=(t // tm,), in_specs=[tile, tile],
        out_specs=[pl.BlockSpec((8, LANES), lambda i: (0, 0)), tile],
        out_shape=[jax.ShapeDtypeStruct((8, LANES), F32), jax.ShapeDtypeStruct((t, d), F32)],
        compiler_params=_params(),
    )(y, target)


def _ln_epi(acc, res, g, b):
    a = ALPHA * res + acc
    return a, _layer_norm(a, g, b)


def _ln_fn(a, g, b):
    return (_layer_norm(a, g, b),)


def _relu_sq(h):
    r = jnp.maximum(h.astype(F32), 0.0)
    return r * r


def _pad_cols(w, n):
    return jnp.pad(w, ((0, 0), (0, n - w.shape[1])))


def _pad_rows(w, n):
    return jnp.pad(w, ((0, n - w.shape[0]), (0, 0)))


def _uq_to_kernel_layout(w_uq):
    w = w_uq.reshape(MLA_RANK, MLA_HEADS, MLA_NOPE + MLA_ROPE)
    nope = w[:, :, :MLA_NOPE].reshape(MLA_RANK, MLA_HEADS * MLA_NOPE)
    rope = jnp.pad(w[:, :, MLA_NOPE:], ((0, 0), (0, 0), (0, LANES - MLA_ROPE))).reshape(MLA_RANK, MLA_HEADS * LANES)
    return jnp.concatenate([nope, rope], axis=1)


def _uq_from_kernel_layout(w):
    nope = w[:, :MLA_HEADS * MLA_NOPE].reshape(MLA_RANK, MLA_HEADS, MLA_NOPE)
    rope = w[:, MLA_HEADS * MLA_NOPE:].reshape(MLA_RANK, MLA_HEADS, LANES)[:, :, :MLA_ROPE]
    return jnp.concatenate([nope, rope], axis=2).reshape(MLA_RANK, MLA_HEADS * (MLA_NOPE + MLA_ROPE))


def _local_step(x, p, positions, target, w, wa, wb):
    t = x.shape[0]
    freqs = ROPE_BASE ** (-jnp.arange(0, MLA_ROPE // 2, dtype=F32) * (2.0 / MLA_ROPE))
    freq_row = jnp.concatenate([freqs, freqs, jnp.zeros((LANES - MLA_ROPE,), F32)])[None, :]
    cos, sin = _rope_tables(positions.reshape(t, 1), freq_row)

    saved = []
    for i in range(DEPTH):
        j, kind = i // 3, i % 3
        s = {"x": x}
        if kind == 0:
            w_in = w["gla_w_in"][j]
            s["w_main"] = w_in[:, :GLA_MAIN]
            s["w_lr"] = _pad_cols(w_in[:, GLA_MAIN:], LANES)
            s["w_up"] = _pad_rows(w["gla_w_gate_up"][j], LANES).astype(BF16)
            s["proj"] = _matmul(x, s["w_main"], name="gla_proj", tn=1024)
            s["glr"] = _matmul(x, s["w_lr"], name="gla_lr", out_dtypes=(F32,))
            s["z"] = _matmul(s["glr"], s["w_up"], name="gla_gate", epi=lambda acc, b: (acc + b,),
                             epi_ins=(w["gla_b_gate"][j][None, :],), out_dtypes=(F32,))
            s["u"], s["states"] = _gla_fwd(s["proj"], s["z"], w["gla_norm_g"][j][None, :])
        elif kind == 1:
            s["w_in"] = _pad_cols(w["mla_w_in"][j], MLA_IN_PAD)
            s["w_uq"] = _uq_to_kernel_layout(w["mla_w_uq"][j])
            s["cq"] = _matmul(x, s["w_in"], name="mla_proj", tn=MLA_IN_PAD, out_dtypes=(F32,))
            s["pre_params"] = (w["mla_q_norm"][j][None, :], w["mla_kv_norm"][j][None, :], s["w_uq"], w["mla_w_ukv"][j])
            s["qn"], s["qr"], s["kv"], s["kr"] = _tile_fwd(_mla_pre, (s["cq"], cos, sin), s["pre_params"],
                                                           (BF16, BF16, BF16, BF16), tm=256, name="mla_pre_fwd")
            s["u"] = _mla_attn_fwd(s["qn"], s["qr"], s["kv"], s["kr"])
        else:
            s["bcu"] = _matmul(x, wa, name="conv_proj", tb=True, tn=REG_CONV[1], b_at=(i, *REG_CONV), out_dtypes=(F32,))
            s["u"] = _conv_fwd(s["bcu"], w["conv_w"][j])
        g0, b0 = w["ln_g"][i, 0][None, :], w["ln_b"][i, 0][None, :]
        g1, b1 = w["ln_g"][i, 1][None, :], w["ln_b"][i, 1][None, :]
        s["a1"], s["x1"] = _matmul(s["u"], wa, name="mixer_out_ln", tm=256, tn=D_MODEL, b_at=(i, *REG_WOUT), epi=_ln_epi,
                                   epi_ins=(x, g0, b0), out_dtypes=(F32, F32))
        s["hh"] = _matmul(s["x1"], wa, name="mlp_up", tb=True, tn=REG_W1T[1], b_at=(i, *REG_W1T))
        s["a2"], s["x2"] = _matmul(s["hh"], wa, name="mlp_down_ln", tm=256, tn=D_MODEL, b_at=(i, *REG_W2), a_fn=_relu_sq,
                                   epi=_ln_epi, epi_ins=(s["x1"], g1, b1), out_dtypes=(F32, F32))
        s["pp"] = _matmul(p[i], wb, name="ple_proj", tb=True, tn=D_MODEL, b_at=(i, *REG_WPT))
        x, s["gt"] = _matmul(s["x2"], wa, name="ple_gate", tn=1024, b_at=(i, *REG_WG),
                             epi=lambda acc, xr, pp: (xr + jax.nn.sigmoid(acc) * pp.astype(F32), acc),
                             epi_ins=(s["x2"], s["pp"]), out_dtypes=(F32, BF16))
        saved.append(s)

    loss_part, dx = _loss_head(x, target)

    ga, gb = jnp.zeros(wa.shape, BF16), jnp.zeros(wb.shape, BF16)
    gw = {n: [None] * WEIGHTS[n][0][0] for n in GROUP_C + SMALL + REPLICATED}
    ln_g_grads, ln_b_grads = [[None, None] for _ in range(DEPTH)], [[None, None] for _ in range(DEPTH)]
    resid = lambda acc, r: (acc + ALPHA * r,)
    plus = lambda acc, r: (acc + r,)
    for i in reversed(range(DEPTH)):
        j, kind = i // 3, i % 3
        s = saved[i]

        def ple_bwd(dxo, gt, pp):
            sg = jax.nn.sigmoid(gt)
            return dxo * sg, dxo * pp * sg * (1.0 - sg)

        d_pp, d_gt = _tile_fwd(ple_bwd, (dx, s["gt"], s["pp"]), (), (BF16, BF16), tm=256, name="ple_bwd")
        gb = _matmul(d_pp, p[i], name="ple_proj_dw", ta=True, tm=REG_WPT[1], tn=PLE_DIM, out_at=(i, *REG_WPT), out_buf=gb)
        ga = _matmul(s["x2"], d_gt, name="ple_gate_dw", ta=True, tm=REG_WG[1], tn=1024, out_at=(i, *REG_WG), out_buf=ga)
        dx2 = _matmul(d_gt, wa, name="ple_gate_dx", tb=True, tn=1024, b_at=(i, *REG_WG), epi=plus, epi_ins=(dx,),
                      out_dtypes=(F32,))
        g1, b1 = w["ln_g"][i, 1][None, :], w["ln_b"][i, 1][None, :]
        (d_a2,), (ln_g_grads[i][1], ln_b_grads[i][1]) = _tile_bwd(_ln_fn, (s["a2"],), (g1, b1), (dx2,), (F32,), tm=256,
                                                                    name="ln_bwd")
        ga = _matmul(s["hh"], d_a2, name="mlp_down_dw", ta=True, tm=REG_W2[1], tn=1024, a_fn=_relu_sq,
                     out_at=(i, *REG_W2), out_buf=ga)
        d_hh = _matmul(d_a2, wa, name="mlp_down_dx", tb=True, tn=REG_W2[1], b_at=(i, *REG_W2),
                       epi=lambda acc, hh: (acc * 2.0 * jnp.maximum(hh.astype(F32), 0.0),), epi_ins=(s["hh"],))
        ga = _matmul(d_hh, s["x1"], name="mlp_up_dw", ta=True, tm=REG_W1T[1], tn=1024, out_at=(i, *REG_W1T), out_buf=ga)
        dx1 = _matmul(d_hh, wa, name="mlp_up_dx", tm=256, tn=1024, b_at=(i, *REG_W1T), epi=resid, epi_ins=(d_a2,),
                      out_dtypes=(F32,))
        g0, b0 = w["ln_g"][i, 0][None, :], w["ln_b"][i, 0][None, :]
        (d_a1,), (ln_g_grads[i][0], ln_b_grads[i][0]) = _tile_bwd(_ln_fn, (s["a1"],), (g0, b0), (dx1,), (F32,), tm=256,
                                                                    name="ln_bwd")
        ga = _matmul(s["u"], d_a1, name="mixer_out_dw", ta=True, tm=REG_WOUT[1], tn=1024, out_at=(i, *REG_WOUT), out_buf=ga)
        du = _matmul(d_a1, wa, name="mixer_out_dx", tb=True, tn=1024, b_at=(i, *REG_WOUT), out_dtypes=(F32,))
        if kind == 0:
            dproj, dz, dg = _gla_bwd(s["proj"], s["z"], w["gla_norm_g"][j][None, :], s["states"], du)
            gw["gla_norm_g"][j] = dg[0]
            gw["gla_b_gate"][j] = _tile_bwd(lambda zz, b: (zz + b,), (s["z"],), (w["gla_b_gate"][j][None, :],), (dz,), (),
                                            tm=256, name="gla_bias_bwd", diff_tiled=[])[1][0][0]
            gw["gla_w_gate_up"][j] = _matmul(s["glr"], dz, name="gla_gate_dw", ta=True, out_dtypes=(F32,))[:GLA_RANK]
            dglr = _matmul(dz, s["w_up"], name="gla_gate_dx", tb=True, out_dtypes=(F32,))
            dw_main = _matmul(s["x"], dproj, name="gla_proj_dw", ta=True, tn=1024, out_dtypes=(F32,))
            dw_lr = _matmul(s["x"], dglr, name="gla_lr_dw", ta=True, out_dtypes=(F32,))[:, :GLA_RANK]
            gw["gla_w_in"][j] = jnp.concatenate([dw_main, dw_lr], axis=1)
            dx = _matmul(dproj, s["w_main"], name="gla_proj_dx", tb=True, tn=1024, epi=resid, epi_ins=(d_a1,),
                         out_dtypes=(F32,))
            dx = _matmul(dglr, s["w_lr"], name="gla_lr_dx", tb=True, tn=1024, epi=plus, epi_ins=(dx,), out_dtypes=(F32,))
        elif kind == 1:
            dqn, dqr, dkv, dkr = _mla_attn_bwd(s["qn"], s["qr"], s["kv"], s["kr"], du)
            (d_cq,), (dgq, dgkv, dw_uq, dw_ukv) = _tile_bwd(_mla_pre, (s["cq"], cos, sin), s["pre_params"],
                                                           (dqn, dqr, dkv, dkr), (BF16,), tm=256, name="mla_pre_bwd",
                                                           diff_tiled=[0])
            gw["mla_q_norm"][j], gw["mla_kv_norm"][j] = dgq[0], dgkv[0]
            gw["mla_w_uq"][j] = _uq_from_kernel_layout(dw_uq)
            gw["mla_w_ukv"][j] = dw_ukv
            gw["mla_w_in"][j] = _matmul(s["x"], d_cq, name="mla_proj_dw", ta=True, tn=MLA_IN_PAD,
                                        out_dtypes=(F32,))[:, :MLA_IN]
            dx = _matmul(d_cq, s["w_in"], name="mla_proj_dx", tb=True, tn=1024, epi=resid, epi_ins=(d_a1,),
                         out_dtypes=(F32,))
        else:
            db, dc, du_, dcw = _conv_bwd(s["bcu"], w["conv_w"][j], du)
            gw["conv_w"][j] = dcw
            dbcu = jnp.concatenate([db, dc, du_], axis=1)
            ga = _matmul(dbcu, s["x"], name="conv_proj_dw", ta=True, tm=REG_CONV[1], tn=1024, out_at=(i, *REG_CONV),
                         out_buf=ga)
            dx = _matmul(dbcu, wa, name="conv_proj_dx", tn=1024, b_at=(i, *REG_CONV), epi=resid, epi_ins=(d_a1,),
                         out_dtypes=(F32,))

    gw["ln_g"] = [jnp.concatenate([a, b], axis=0) for a, b in ln_g_grads]
    gw["ln_b"] = [jnp.concatenate([a, b], axis=0) for a, b in ln_b_grads]
    grads = {n: jnp.stack(gw[n]).astype(F32) for n in gw}
    return loss_part, dx, grads, ga, gb


MESH_IDS = pl.DeviceIdType.MESH
ANY = pl.BlockSpec(memory_space=pl.ANY)
AG_COPIES, CORE_COPIES, CHIP_COPIES = 7, 4, 3


def _my_place():
    return lax.axis_index("x"), lax.axis_index("y"), lax.axis_index("c")


def _all_gather(xs, name):
    n_arr = len(xs)

    def body(*refs):
        x_refs, out_refs = refs[:n_arr], refs[n_arr:2 * n_arr]
        send_sems, recv_sems, local_sems = refs[2 * n_arr:]
        mx, my, mc = _my_place()
        me, sibling = (mx, my, mc), (mx, my, 1 - mc)
        chips = [(1 - mx, my), (mx, 1 - my), (1 - mx, 1 - my)]

        def copy(a, k, blk, to, from_input=False):
            px, py, pc = blk
            dst = out_refs[a].at[4 * px + 2 * py + pc]
            return pltpu.make_async_remote_copy(
                src_ref=x_refs[a] if from_input else dst, dst_ref=dst, send_sem=send_sems.at[AG_COPIES * a + k],
                recv_sem=recv_sems.at[AG_COPIES * a + k], device_id=to, device_id_type=MESH_IDS)

        mine = [pltpu.make_async_copy(x_refs[a], out_refs[a].at[4 * mx + 2 * my + mc], local_sems.at[a])
                for a in range(n_arr)]
        for cp in mine:
            cp.start()
        first = []
        for a in range(n_arr):
            first.append(copy(a, 0, me, sibling, from_input=True))
            first += [copy(a, 1 + n, me, (*chip, mc), from_input=True) for n, chip in enumerate(chips)]
        for cp in first:
            cp.start()
        passed = []
        for n, chip in enumerate(chips):
            for a in range(n_arr):
                copy(a, 1 + n, (*chip, mc), me).wait_recv()
                passed.append(copy(a, 4 + n, (*chip, mc), sibling))
                passed[-1].start()
        for a in range(n_arr):
            copy(a, 0, sibling, me).wait_recv()
            for n, chip in enumerate(chips):
                copy(a, 4 + n, (*chip, 1 - mc), me).wait_recv()
        for cp in first + passed:
            cp.wait_send()
        for cp in mine:
            cp.wait()

    return pl.pallas_call(
        body, name=name, out_shape=[jax.ShapeDtypeStruct((N_DEV, *x.shape), x.dtype) for x in xs],
        in_specs=[ANY] * n_arr, out_specs=[ANY] * n_arr,
        scratch_shapes=[pltpu.SemaphoreType.DMA((AG_COPIES * n_arr,)), pltpu.SemaphoreType.DMA((AG_COPIES * n_arr,)),
                        pltpu.SemaphoreType.DMA((n_arr,))],
    )(*xs)


def _exchange_cores(gs):
    n_arr = len(gs)

    def body(*refs):
        g_refs, recv_refs = refs[:n_arr], refs[n_arr:2 * n_arr]
        send_sems, recv_sems = refs[2 * n_arr:]
        mx, my, mc = _my_place()
        copies = [pltpu.make_async_remote_copy(
            src_ref=g_refs[a].at[2 * n + (1 - mc)], dst_ref=recv_refs[a].at[n], send_sem=send_sems.at[CORE_COPIES * a + n],
            recv_sem=recv_sems.at[CORE_COPIES * a + n], device_id=(mx, my, 1 - mc), device_id_type=MESH_IDS)
            for a in range(n_arr) for n in range(CORE_COPIES)]
        for cp in copies:
            cp.start()
        for cp in copies:
            cp.wait()

    return pl.pallas_call(
        body, name="rs_exchange_cores", out_shape=[jax.ShapeDtypeStruct((4, *g.shape[1:]), g.dtype) for g in gs],
        in_specs=[ANY] * n_arr, out_specs=[ANY] * n_arr,
        scratch_shapes=[pltpu.SemaphoreType.DMA((CORE_COPIES * n_arr,)), pltpu.SemaphoreType.DMA((CORE_COPIES * n_arr,))],
    )(*gs)


def _exchange_chips(hs):
    n_arr = len(hs)

    def body(*refs):
        h_refs, recv_refs = refs[:n_arr], refs[n_arr:2 * n_arr]
        send_sems, recv_sems = refs[2 * n_arr:]
        mx, my, mc = _my_place()
        chips = [(1 - mx, my), (mx, 1 - my), (1 - mx, 1 - my)]
        copies = [pltpu.make_async_remote_copy(
            src_ref=h_refs[a].at[2 * cx + cy], dst_ref=recv_refs[a].at[n], send_sem=send_sems.at[CHIP_COPIES * a + n],
            recv_sem=recv_sems.at[CHIP_COPIES * a + n], device_id=(cx, cy, mc), device_id_type=MESH_IDS)
            for a in range(n_arr) for n, (cx, cy) in enumerate(chips)]
        for cp in copies:
            cp.start()
        for cp in copies:
            cp.wait()

    return pl.pallas_call(
        body, name="rs_exchange_chips", out_shape=[jax.ShapeDtypeStruct((3, *h.shape[1:]), h.dtype) for h in hs],
        in_specs=[ANY] * n_arr, out_specs=[ANY] * n_arr,
        scratch_shapes=[pltpu.SemaphoreType.DMA((CHIP_COPIES * n_arr,)), pltpu.SemaphoreType.DMA((CHIP_COPIES * n_arr,))],
    )(*hs)


def _pair_sum(g, recv, my_c):
    _, r, c = g.shape
    tr = min(PACK_ROW_TILE, r)

    def body(c_ref, g_ref, r_ref, o_ref):
        o_ref[...] = (g_ref[...].astype(F32) + r_ref[...].astype(F32)).astype(o_ref.dtype)

    return pl.pallas_call(
        body, name="rs_pair_sum", out_shape=jax.ShapeDtypeStruct((4, r, c), g.dtype),
        grid_spec=pltpu.PrefetchScalarGridSpec(
            num_scalar_prefetch=1, grid=(4, r // tr),
            in_specs=[pl.BlockSpec((1, tr, c), lambda n, i, cr: (2 * n + cr[0], i, 0)),
                      pl.BlockSpec((1, tr, c), lambda n, i, cr: (n, i, 0))],
            out_specs=pl.BlockSpec((1, tr, c), lambda n, i, cr: (n, i, 0))),
        compiler_params=_params(),
    )(my_c, g, recv)


def _chip_sum(h, recv, my_chip):
    _, r, c = h.shape
    tr = min(PACK_ROW_TILE, r)

    def body(j_ref, h_ref, r0_ref, r1_ref, r2_ref, o_ref):
        o_ref[...] = ((h_ref[0].astype(F32) + r0_ref[0].astype(F32)) + r1_ref[0].astype(F32)) + r2_ref[0].astype(F32)

    return pl.pallas_call(
        body, name="rs_chip_sum", out_shape=jax.ShapeDtypeStruct((r, c), F32),
        grid_spec=pltpu.PrefetchScalarGridSpec(
            num_scalar_prefetch=1, grid=(r // tr,),
            in_specs=[pl.BlockSpec((1, tr, c), lambda i, jr: (jr[0], i, 0))]
            + [pl.BlockSpec((1, tr, c), lambda i, jr, n=n: (n, i, 0)) for n in range(3)],
            out_specs=pl.BlockSpec((tr, c), lambda i, jr: (i, 0))),
        compiler_params=_params(),
    )(my_chip, h, recv, recv, recv)


def _sum_blocks(g):
    n, r, c = g.shape

    def body(g_ref, o_ref):
        acc = g_ref[0]
        for k in range(1, n):
            acc = acc + g_ref[k]
        o_ref[...] = acc

    return pl.pallas_call(body, name="sum_blocks", out_shape=jax.ShapeDtypeStruct((r, c), F32), compiler_params=_params())(g)


def _pack(flat_parts, cols, row_multiple, dtype):
    flat = jnp.concatenate([f.astype(dtype) for f in flat_parts])
    per_row_block = cols * row_multiple
    padded = -(-flat.shape[0] // per_row_block) * per_row_block
    return jnp.pad(flat, (0, padded - flat.shape[0])).reshape(padded // cols, cols)


def _pack_blocks(parts, cols, row_multiple, dtype):
    cat = jnp.concatenate([q.astype(dtype) for q in parts], axis=1)
    per_row_block = cols * row_multiple
    padded = -(-cat.shape[1] // per_row_block) * per_row_block
    return jnp.pad(cat, ((0, 0), (0, padded - cat.shape[1]))).reshape(N_DEV, padded // cols, cols)


def _shard_shape(name):
    shape, axis = WEIGHTS[name]
    if axis is None:
        return shape
    return tuple(s // N_DEV if a == axis else s for a, s in enumerate(shape))


def _size(shape):
    n = 1
    for s in shape:
        n *= s
    return n


def _unshard(blocks, name):
    _, axis = WEIGHTS[name]
    return jnp.concatenate([blocks[k] for k in range(N_DEV)], axis=axis)


def _to_blocks(full, name):
    _, axis = WEIGHTS[name]
    return jnp.stack(jnp.split(full, N_DEV, axis=axis)).reshape(N_DEV, -1)


def _unpack(buf_flat, names):
    out, off = {}, 0
    for n in names:
        shp = _shard_shape(n)
        out[n] = buf_flat[off:off + _size(shp)].reshape(shp)
        off += _size(shp)
    return out


def _group_a_slab(shard):
    layers = []
    for i in range(DEPTH):
        j, kind = i // 3, i % 3
        w_out = (shard["gla_w_out"], shard["mla_w_out"], shard["conv_w_out"])[kind][j]
        conv = shard["conv_w_in"][j].T if kind == 2 else jnp.zeros((REG_CONV[1], D_MODEL), F32)
        layers.append(jnp.concatenate([shard["mlp_w2"][i], shard["mlp_w1"][i].T, w_out, conv, shard["ple_w_gate"][i]], axis=0))
    return jnp.stack(layers).astype(BF16)


def _group_a_grads(red_a):
    def rows(reg):
        return slice(reg[0] * reg[1], (reg[0] + 1) * reg[1])

    w_out = red_a[:, rows(REG_WOUT)]
    return {
        "mlp_w2": red_a[:, rows(REG_W2)], "mlp_w1": red_a[:, rows(REG_W1T)].transpose(0, 2, 1),
        "gla_w_out": jnp.stack([w_out[0], w_out[3]]), "mla_w_out": w_out[1:2], "conv_w_out": w_out[2:3],
        "conv_w_in": red_a[2:3, rows(REG_CONV)].transpose(0, 2, 1), "ple_w_gate": red_a[:, rows(REG_WG)],
    }


def _adamw(w, g, m, v, name):
    shape = w.shape
    cols = shape[-1]
    rows = _size(shape) // cols
    tr = rows
    for cand in (512, 256, 128, 64, 32, 16, 8):
        if rows > cand and rows % cand == 0:
            tr = cand
            break

    def body(w_ref, g_ref, m_ref, v_ref, d_ref, mo_ref, vo_ref):
        gv = g_ref[...]
        m2 = ADAM_B1 * m_ref[...] + (1.0 - ADAM_B1) * gv
        v2 = ADAM_B2 * v_ref[...] + (1.0 - ADAM_B2) * (gv * gv)
        m_hat = m2 / (1.0 - ADAM_B1 ** ADAM_STEP)
        v_hat = v2 / (1.0 - ADAM_B2 ** ADAM_STEP)
        d_ref[...] = -ADAM_LR * (m_hat / (jnp.sqrt(v_hat) + ADAM_EPS) + ADAM_WD * w_ref[...])
        mo_ref[...] = m2
        vo_ref[...] = v2

    spec = pl.BlockSpec((tr, cols), lambda i: (i, 0))
    outs = pl.pallas_call(
        body, name="adamw_" + name, grid=(rows // tr,), in_specs=[spec] * 4, out_specs=[spec] * 3,
        out_shape=[jax.ShapeDtypeStruct((rows, cols), F32)] * 3, compiler_params=_params(),
    )(*[a.reshape(rows, cols) for a in (w, g, m, v)])
    return [o.reshape(shape) for o in outs]


def kernel(x, p, positions, gla_w_in, gla_w_gate_up, gla_b_gate, gla_norm_g, gla_w_out, mla_w_in, mla_q_norm, mla_kv_norm, mla_w_uq, mla_w_ukv, mla_w_out, conv_w_in, conv_w, conv_w_out, ln_g, ln_b, mlp_w1, mlp_w2, ple_w_gate, ple_w_proj, loss_target, m_gla_w_in, m_gla_w_gate_up, m_gla_b_gate, m_gla_norm_g, m_gla_w_out, m_mla_w_in, m_mla_q_norm, m_mla_kv_norm, m_mla_w_uq, m_mla_w_ukv, m_mla_w_out, m_conv_w_in, m_conv_w, m_conv_w_out, m_ln_g, m_ln_b, m_mlp_w1, m_mlp_w2, m_ple_w_gate, m_ple_w_proj, v_gla_w_in, v_gla_w_gate_up, v_gla_b_gate, v_gla_norm_g, v_gla_w_out, v_mla_w_in, v_mla_q_norm, v_mla_kv_norm, v_mla_w_uq, v_mla_w_ukv, v_mla_w_out, v_conv_w_in, v_conv_w, v_conv_w_out, v_ln_g, v_ln_b, v_mlp_w1, v_mlp_w2, v_ple_w_gate, v_ple_w_proj):
    args = locals()
    shard = {n: args[n] for n in WEIGHT_NAMES}
    mom = {n: args["m_" + n] for n in WEIGHT_NAMES}
    var = {n: args["v_" + n] for n in WEIGHT_NAMES}
    mx, my, mc = _my_place()

    wa, wb, wc = _all_gather(
        [_group_a_slab(shard), shard["ple_w_proj"].transpose(0, 2, 1).astype(BF16),
         _pack([shard[n].reshape(-1) for n in GROUP_C], PACK_COLS, PACK_ROW_TILE, BF16)], "ag_weights")
    (small,) = _all_gather([_pack([shard[n].reshape(-1) for n in SMALL], LANES, 8, F32)], "ag_small")
    full = {n: shard[n] for n in REPLICATED}
    for names, flat in ((GROUP_C, wc.reshape(N_DEV, -1)), (SMALL, small.reshape(N_DEV, -1))):
        off = 0
        for n in names:
            shp = _shard_shape(n)
            full[n] = _unshard(flat[:, off:off + _size(shp)].reshape((N_DEV,) + shp), n)
            off += _size(shp)

    loss_part, grad_x, grads, ga, gb = _local_step(x[0], p[:, 0], positions[0], loss_target[0], full, wa, wb)
    loss = lax.psum(loss_part[0, 0], MESH_AXES)

    gc = _pack_blocks([_to_blocks(grads[n], n) for n in GROUP_C], PACK_COLS, PACK_ROW_TILE, BF16)
    g3 = [ga.reshape(N_DEV, DEPTH * A_ROWS, D_MODEL), gb.reshape(N_DEV, DEPTH * REG_WPT[1], PLE_DIM), gc]
    my_c, my_chip = mc.astype(jnp.int32).reshape(1), (2 * mx + my).astype(jnp.int32).reshape(1)
    hs = [_pair_sum(g, f, my_c) for g, f in zip(g3, _exchange_cores(g3))]
    red_a, red_b, red_c = [_chip_sum(h, f, my_chip) for h, f in zip(hs, _exchange_chips(hs))]
    small_parts = [grads[n].reshape(-1) for n in SMALL + REPLICATED]
    (small_all,) = _all_gather([_pack(small_parts, LANES, 8, F32)], "ag_small_grads")
    red_small = _sum_blocks(small_all).reshape(-1)

    my_grads = _unpack(red_c.reshape(-1), GROUP_C)
    my_grads.update(_group_a_grads(red_a.reshape(DEPTH, A_ROWS, D_MODEL)))
    my_grads["ple_w_proj"] = red_b.reshape(DEPTH, REG_WPT[1], PLE_DIM).transpose(0, 2, 1)
    off = 0
    dev = 4 * mx + 2 * my + mc
    for n in SMALL + REPLICATED:
        shape, axis = WEIGHTS[n]
        full_g = red_small[off:off + _size(shape)].reshape(shape)
        off += _size(shape)
        if axis is None:
            my_grads[n] = full_g
        else:
            width = shape[axis] // N_DEV
            my_grads[n] = lax.dynamic_slice_in_dim(full_g, dev * width, width, axis=axis)

    deltas, new_m, new_v = {}, {}, {}
    for n in WEIGHT_NAMES:
        deltas[n], new_m[n], new_v[n] = _adamw(shard[n], my_grads[n], mom[n], var[n], n)
    return (loss, grad_x[None], *[my_grads[n] for n in WEIGHT_NAMES], *[deltas[n] for n in WEIGHT_NAMES],
            *[new_m[n] for n in WEIGHT_NAMES], *[new_v[n] for n in WEIGHT_NAMES])
```

```python
import functools

import jax
import jax.numpy as jnp
from jax import lax
from jax.experimental import pallas as pl
from jax.experimental.pallas import tpu as pltpu

F32, BF16 = jnp.float32, jnp.bfloat16
HIGHEST = lax.Precision.HIGHEST
MESH_AXES = ("x", "y", "c")
N_DEV = 8

D_MODEL = 1024
SEQ = 2048
DEPTH = 4
CHUNK = 64
ALPHA = (2 * DEPTH) ** 0.25
LN_EPS = 1e-5
RMS_EPS = 1e-6
PLE_DIM = 256
D_FF = 4 * D_MODEL
GLA_HEADS = 4
GLA_DK = 128
GLA_DV = 256
GLA_RANK = 16
GLA_TAU = 16.0
GLA_HK = GLA_HEADS * GLA_DK
GLA_HV = GLA_HEADS * GLA_DV
GLA_MAIN = 2 * GLA_HK + GLA_HV + D_MODEL
MLA_HEADS = 8
MLA_NOPE = 128
MLA_ROPE = 64
MLA_V = 128
MLA_RANK = 256
MLA_IN = 2 * MLA_RANK + MLA_ROPE
MLA_IN_PAD = 640
ROPE_BASE = 10000.0
LANES = 128
ADAM_LR, ADAM_B1, ADAM_B2, ADAM_EPS, ADAM_WD, ADAM_STEP = 0.001, 0.9, 0.999, 1e-08, 0.01, 10

V7X_VMEM_LIMIT_BYTES = 56 * 1024 * 1024
PACK_COLS = 1024
PACK_ROW_TILE = 256

WEIGHTS = {
    "gla_w_in": ((2, 1024, 3088), 2), "gla_w_gate_up": ((2, 16, 512), 2), "gla_b_gate": ((2, 512), 1),
    "gla_norm_g": ((2, 256), 1), "gla_w_out": ((2, 1024, 1024), 1), "mla_w_in": ((1, 1024, 576), 1),
    "mla_q_norm": ((1, 256), None), "mla_kv_norm": ((1, 256), None), "mla_w_uq": ((1, 256, 1536), 2),
    "mla_w_ukv": ((1, 256, 2048), 2), "mla_w_out": ((1, 1024, 1024), 1), "conv_w_in": ((1, 1024, 3072), 2),
    "conv_w": ((1, 3, 1024), 2), "conv_w_out": ((1, 1024, 1024), 1), "ln_g": ((4, 2, 1024), 2),
    "ln_b": ((4, 2, 1024), 2), "mlp_w1": ((4, 1024, 4096), 2), "mlp_w2": ((4, 4096, 1024), 1),
    "ple_w_gate": ((4, 1024, 1024), 1), "ple_w_proj": ((4, 256, 1024), 2),
}
WEIGHT_NAMES = list(WEIGHTS)
REG_W2, REG_W1T, REG_WOUT, REG_WG = (0, 512), (1, 512), (8, 128), (9, 128)
A_ROWS = 1280
REG_CONV = (0, 384)
REG_WPT = (0, 128)
MLA_PACKED = ["mla_w_in", "mla_w_uq", "mla_w_ukv"]
SMALL = ["gla_w_gate_up", "gla_b_gate", "gla_norm_g", "conv_w", "ln_g", "ln_b"]
REPLICATED = ["mla_q_norm", "mla_kv_norm"]


def _params(**kw):
    return pltpu.CompilerParams(vmem_limit_bytes=V7X_VMEM_LIMIT_BYTES, **kw)


def _dot(a, b, ca, cb, precision=None):
    return lax.dot_general(a, b, (((ca,), (cb,)), ((), ())), precision=precision, preferred_element_type=F32)


def _nn(a, b):
    return _dot(a.astype(BF16), b.astype(BF16), 1, 0)


def _nt(a, b):
    return _dot(a.astype(BF16), b.astype(BF16), 1, 1)


def _tn(a, b):
    return _dot(a.astype(BF16), b.astype(BF16), 0, 0)


@jax.custom_vjp
def mm_nn(a, b):
    return _nn(a, b)


def _mm_nn_fwd(a, b):
    return _nn(a, b), (a, b)


def _mm_nn_bwd(res, g):
    a, b = res
    return _nt(g, b).astype(a.dtype), _tn(a, g).astype(b.dtype)


mm_nn.defvjp(_mm_nn_fwd, _mm_nn_bwd)


@jax.custom_vjp
def mm_nt(a, b):
    return _nt(a, b)


def _mm_nt_fwd(a, b):
    return _nt(a, b), (a, b)


def _mm_nt_bwd(res, g):
    a, b = res
    return _nn(g, b).astype(a.dtype), _tn(g, a).astype(b.dtype)


mm_nt.defvjp(_mm_nt_fwd, _mm_nt_bwd)


@jax.custom_vjp
def mm_tn(a, b):
    return _tn(a, b)


def _mm_tn_fwd(a, b):
    return _tn(a, b), (a, b)


def _mm_tn_bwd(res, g):
    a, b = res
    return _nt(b, g).astype(a.dtype), _nn(a, g).astype(b.dtype)


mm_tn.defvjp(_mm_tn_fwd, _mm_tn_bwd)


def _iota2(shape, dim):
    return lax.broadcasted_iota(jnp.int32, shape, dim)


@jax.custom_vjp
def cumsum_rows(x):
    n = x.shape[0]
    tri = (_iota2((n, n), 0) >= _iota2((n, n), 1)).astype(F32)
    return _dot(tri, x, 1, 0, precision=HIGHEST)


def _cumsum_fwd(x):
    return cumsum_rows(x), None


def _cumsum_bwd(_, g):
    n = g.shape[0]
    tri_t = (_iota2((n, n), 0) <= _iota2((n, n), 1)).astype(F32)
    return (_dot(tri_t, g, 1, 0, precision=HIGHEST),)


cumsum_rows.defvjp(_cumsum_fwd, _cumsum_bwd)


def _rot_matrix(transposed):
    i, j = _iota2((LANES, LANES), 0), _iota2((LANES, LANES), 1)
    if transposed:
        i, j = j, i
    half = MLA_ROPE // 2
    plus = (i == j - half) & (j >= half) & (j < MLA_ROPE)
    minus = (i == j + half) & (j < half)
    return plus.astype(F32) - minus.astype(F32)


@jax.custom_vjp
def rot_half(x):
    return _dot(x, _rot_matrix(False), 1, 0, precision=HIGHEST)


def _rot_fwd(x):
    return rot_half(x), None


def _rot_bwd(_, g):
    return (_dot(g, _rot_matrix(True), 1, 0, precision=HIGHEST),)


rot_half.defvjp(_rot_fwd, _rot_bwd)


def _shift_rows_raw(x, s):
    n = x.shape[0]
    row = _iota2(x.shape, 0)
    rolled = pltpu.roll(x, s % n, 0)
    keep = (row >= s) if s > 0 else (row < n + s)
    return jnp.where(keep, rolled, 0.0)


@functools.partial(jax.custom_vjp, nondiff_argnums=(1,))
def shift_rows(x, s):
    return _shift_rows_raw(x, s)


def _shift_fwd(x, s):
    return _shift_rows_raw(x, s), None


def _shift_bwd(s, _, g):
    return (_shift_rows_raw(g, -s),)


shift_rows.defvjp(_shift_fwd, _shift_bwd)


def _layer_norm(a, g, b):
    mu = jnp.mean(a, -1, keepdims=True)
    xc = a - mu
    var = jnp.mean(xc * xc, -1, keepdims=True)
    return xc * lax.rsqrt(var + LN_EPS) * g + b


def _rms_norm(a, g):
    return a * lax.rsqrt(jnp.mean(a * a, -1, keepdims=True) + RMS_EPS) * g


def _log_sigmoid(z):
    return jnp.minimum(z, 0.0) - jnp.log(1.0 + jnp.exp(-jnp.abs(z)))


def _matmul(a, b, *, name, ta=False, tb=False, tm=512, tn=512, a_fn=None, epi=None, epi_ins=(), out_dtypes=(BF16,),
            b_at=None, out_at=None, out_buf=None, after=None):
    m = a.shape[1] if ta else a.shape[0]
    k = a.shape[0] if ta else a.shape[1]
    if b_at is None:
        n, kb = (b.shape[0], b.shape[1]) if tb else (b.shape[1], b.shape[0])
    else:
        rb, r = b_at
        n, kb = (N_DEV * r, b.shape[2]) if tb else (b.shape[2], N_DEV * r)
    assert kb == k, (name, a.shape, b.shape, k, kb)
    tm, tn = min(tm, m), min(tn, n)
    assert m % tm == 0 and n % tn == 0, (name, m, n, tm, tn)
    a_spec = pl.BlockSpec((k, tm), lambda i, j: (0, i)) if ta else pl.BlockSpec((tm, k), lambda i, j: (i, 0))
    if b_at is None:
        b_spec = pl.BlockSpec((tn, k), lambda i, j: (j, 0)) if tb else pl.BlockSpec((k, tn), lambda i, j: (0, j))
        load_b = lambda ref: ref[...]
    elif tb and tn == n:
        b_spec = pl.BlockSpec((N_DEV, r, k), lambda i, j: (0, rb, 0))
        load_b = lambda ref: ref[...].reshape(n, k)
    elif tb:
        assert tn == r, (name, tn, r)
        b_spec = pl.BlockSpec((1, r, k), lambda i, j: (j, rb, 0))
        load_b = lambda ref: ref[0]
    else:
        b_spec = pl.BlockSpec((N_DEV, r, tn), lambda i, j: (0, rb, j))
        load_b = lambda ref: ref[...].reshape(k, tn)
    e_specs = []
    for e in epi_ins:
        if e.shape == (1, n):
            e_specs.append(pl.BlockSpec((1, tn), lambda i, j: (0, j)))
        else:
            assert e.shape == (m, n), (name, e.shape, m, n)
            e_specs.append(pl.BlockSpec((tm, tn), lambda i, j: (i, j)))
    n_epi = len(epi_ins)
    ca, cb = (0 if ta else 1), (1 if tb else 0)
    operands = [a, b, *epi_ins]
    in_specs = [a_spec, b_spec, *e_specs]
    if out_at is None:
        out_specs = [pl.BlockSpec((tm, tn), lambda i, j: (i, j)) for _ in out_dtypes]
        out_shape = [jax.ShapeDtypeStruct((m, n), dt) for dt in out_dtypes]
        aliases, n_buf = {}, 0
    else:
        orb, orows = out_at
        assert len(out_dtypes) == 1 and orows % tm == 0 and m == N_DEV * orows and n == out_buf.shape[2], (name, m, n)
        per = orows // tm
        out_specs = [pl.BlockSpec((1, tm, tn), lambda i, j: (i // per, orb * per + i % per, j))]
        out_shape = [jax.ShapeDtypeStruct(out_buf.shape, out_buf.dtype)]
        operands.append(out_buf)
        in_specs.append(pl.BlockSpec(memory_space=pl.ANY))
        aliases, n_buf = {len(operands) - 1: 0}, 1
    if after is not None:
        operands.append(after)
        in_specs.append(pl.BlockSpec(memory_space=pl.ANY))
        n_buf += 1

    def body(a_ref, b_ref, *rest):
        av = a_ref[...]
        if a_fn is not None:
            av = a_fn(av)
        acc = _dot(av.astype(BF16), load_b(b_ref).astype(BF16), ca, cb)
        outs = epi(acc, *[r_[...] for r_ in rest[:n_epi]]) if epi is not None else (acc,)
        for o_ref, val in zip(rest[n_epi + n_buf:], outs):
            o_ref[...] = val.astype(o_ref.dtype).reshape(o_ref.shape)

    outs = pl.pallas_call(
        body, name=name, grid=(m // tm, n // tn), in_specs=in_specs, out_specs=out_specs, out_shape=out_shape,
        input_output_aliases=aliases, compiler_params=_params(),
    )(*operands)
    return outs[0] if len(outs) == 1 else tuple(outs)


def _tile_fwd(f, tiled, params, out_dtypes, *, tm, name):
    t = tiled[0].shape[0]
    assert t % tm == 0
    out_avals = jax.eval_shape(f, *[jax.ShapeDtypeStruct((tm, x.shape[1]), F32) for x in tiled],
                               *[jax.ShapeDtypeStruct(p.shape, F32) for p in params])
    nt, npar = len(tiled), len(params)

    def body(*refs):
        ins = [r[...].astype(F32) for r in refs[:nt + npar]]
        outs = f(*ins)
        for o_ref, val in zip(refs[nt + npar:], outs):
            o_ref[...] = val.astype(o_ref.dtype)

    return pl.pallas_call(
        body, name=name, grid=(t // tm,),
        in_specs=[pl.BlockSpec((tm, x.shape[1]), lambda i: (i, 0)) for x in tiled]
        + [pl.BlockSpec(p.shape, lambda i: (0, 0)) for p in params],
        out_specs=[pl.BlockSpec((tm, o.shape[1]), lambda i: (i, 0)) for o in out_avals],
        out_shape=[jax.ShapeDtypeStruct((t, o.shape[1]), dt) for o, dt in zip(out_avals, out_dtypes)],
        compiler_params=_params(),
    )(*tiled, *params)


def _tile_bwd(f, tiled, params, cots, d_tiled_dtypes, *, tm, name, diff_tiled=None):
    t = tiled[0].shape[0]
    assert t % tm == 0
    nt, npar, nc = len(tiled), len(params), len(cots)
    diff_tiled = list(range(nt)) if diff_tiled is None else diff_tiled

    def body(*refs):
        ins = [r[...].astype(F32) for r in refs[:nt + npar]]
        cts = [r[...].astype(F32) for r in refs[nt + npar:nt + npar + nc]]
        o_refs = refs[nt + npar + nc:]
        _, vjp = jax.vjp(f, *ins)
        grads = vjp(tuple(cts))
        for o_ref, idx in zip(o_refs[:len(diff_tiled)], diff_tiled):
            o_ref[...] = grads[idx].astype(o_ref.dtype)
        p_refs = o_refs[len(diff_tiled):]

        @pl.when(pl.program_id(0) == 0)
        def _():
            for p_ref in p_refs:
                p_ref[...] = jnp.zeros_like(p_ref)

        for p_ref, gp in zip(p_refs, grads[nt:]):
            p_ref[...] += gp

    outs = pl.pallas_call(
        body, name=name, grid=(t // tm,),
        in_specs=[pl.BlockSpec((tm, x.shape[1]), lambda i: (i, 0)) for x in tiled]
        + [pl.BlockSpec(p.shape, lambda i: (0, 0)) for p in params]
        + [pl.BlockSpec((tm, c.shape[1]), lambda i: (i, 0)) for c in cots],
        out_specs=[pl.BlockSpec((tm, tiled[idx].shape[1]), lambda i: (i, 0)) for idx in diff_tiled]
        + [pl.BlockSpec(p.shape, lambda i: (0, 0)) for p in params],
        out_shape=[jax.ShapeDtypeStruct(tiled[idx].shape, dt) for idx, dt in zip(diff_tiled, d_tiled_dtypes)]
        + [jax.ShapeDtypeStruct(p.shape, F32) for p in params],
        compiler_params=_params(),
    )(*tiled, *params, *cots)
    return outs[:len(diff_tiled)], outs[len(diff_tiled):]


def _gla_head(q, k, v, r, z, g, st):
    c = q.shape[0]
    causal = _iota2((c, c), 0) >= _iota2((c, c), 1)
    la = _log_sigmoid(z) * (1.0 / GLA_TAU)
    big_l = cumsum_rows(la)
    ep, en = jnp.exp(big_l), jnp.exp(-big_l)
    qs = q * (GLA_DK ** -0.5)
    qp = qs * ep
    s = jnp.where(causal, mm_nt(qp, k * en), mm_nt(qs * en, k * ep))
    o = mm_nn(s, v) + mm_nt(qp, st)
    l_end = jnp.sum(la, axis=0, keepdims=True)
    st_new = st * jnp.exp(l_end) + mm_tn(v, k * jnp.exp(l_end - big_l))
    u = _rms_norm(o, g) * (r * jax.nn.sigmoid(r))
    return u, st_new


def _gla_slices(h):
    q = slice(GLA_DK * h, GLA_DK * (h + 1))
    k = slice(GLA_HK + GLA_DK * h, GLA_HK + GLA_DK * (h + 1))
    v = slice(2 * GLA_HK + GLA_DV * h, 2 * GLA_HK + GLA_DV * (h + 1))
    r = slice(2 * GLA_HK + GLA_HV + GLA_DV * h, 2 * GLA_HK + GLA_HV + GLA_DV * (h + 1))
    return q, k, v, r


def _gla_fwd(proj, z, norm_g):
    t = proj.shape[0]
    nc = t // CHUNK

    def body(proj_ref, z_ref, g_ref, u_ref, st_save_ref, st_ref):
        @pl.when(pl.program_id(0) == 0)
        def _():
            st_ref[...] = jnp.zeros_like(st_ref)

        g = g_ref[...]
        for h in range(GLA_HEADS):
            sq, sk, sv, sr = _gla_slices(h)
            st = st_ref[h]
            st_save_ref[0, h] = st
            u, st_new = _gla_head(proj_ref[:, sq].astype(F32), proj_ref[:, sk].astype(F32), proj_ref[:, sv].astype(F32),
                                  proj_ref[:, sr].astype(F32), z_ref[:, GLA_DK * h:GLA_DK * (h + 1)], g, st)
            u_ref[:, GLA_DV * h:GLA_DV * (h + 1)] = u.astype(u_ref.dtype)
            st_ref[h] = st_new

    return pl.pallas_call(
        body, name="gla_fwd", grid=(nc,),
        in_specs=[pl.BlockSpec((CHUNK, GLA_MAIN), lambda i: (i, 0)), pl.BlockSpec((CHUNK, GLA_HK), lambda i: (i, 0)),
                  pl.BlockSpec((1, GLA_DV), lambda i: (0, 0))],
        out_specs=[pl.BlockSpec((CHUNK, GLA_HV), lambda i: (i, 0)),
                   pl.BlockSpec((1, GLA_HEADS, GLA_DV, GLA_DK), lambda i: (i, 0, 0, 0))],
        out_shape=[jax.ShapeDtypeStruct((t, GLA_HV), BF16), jax.ShapeDtypeStruct((nc, GLA_HEADS, GLA_DV, GLA_DK), F32)],
        scratch_shapes=[pltpu.VMEM((GLA_HEADS, GLA_DV, GLA_DK), F32)],
        compiler_params=_params(),
    )(proj, z, norm_g)


def _gla_bwd(proj, z, norm_g, states, du):
    t = proj.shape[0]
    nc = t // CHUNK

    def body(proj_ref, z_ref, g_ref, st_in_ref, du_ref, dproj_ref, dz_ref, dg_ref, dst_ref):
        @pl.when(pl.program_id(0) == 0)
        def _():
            dst_ref[...] = jnp.zeros_like(dst_ref)
            dg_ref[...] = jnp.zeros_like(dg_ref)

        g = g_ref[...]
        for h in range(GLA_HEADS):
            sq, sk, sv, sr = _gla_slices(h)
            ins = (proj_ref[:, sq].astype(F32), proj_ref[:, sk].astype(F32), proj_ref[:, sv].astype(F32),
                   proj_ref[:, sr].astype(F32), z_ref[:, GLA_DK * h:GLA_DK * (h + 1)], g, st_in_ref[0, h])
            _, vjp = jax.vjp(_gla_head, *ins)
            dq, dk, dv, dr, dz, dg, dst = vjp((du_ref[:, GLA_DV * h:GLA_DV * (h + 1)], dst_ref[h]))
            dproj_ref[:, sq] = dq.astype(dproj_ref.dtype)
            dproj_ref[:, sk] = dk.astype(dproj_ref.dtype)
            dproj_ref[:, sv] = dv.astype(dproj_ref.dtype)
            dproj_ref[:, sr] = dr.astype(dproj_ref.dtype)
            dz_ref[:, GLA_DK * h:GLA_DK * (h + 1)] = dz
            dg_ref[...] += dg
            dst_ref[h] = dst

    rev = lambda i: (nc - 1 - i, 0)
    return pl.pallas_call(
        body, name="gla_bwd", grid=(nc,),
        in_specs=[pl.BlockSpec((CHUNK, GLA_MAIN), rev), pl.BlockSpec((CHUNK, GLA_HK), rev),
                  pl.BlockSpec((1, GLA_DV), lambda i: (0, 0)),
                  pl.BlockSpec((1, GLA_HEADS, GLA_DV, GLA_DK), lambda i: (nc - 1 - i, 0, 0, 0)),
                  pl.BlockSpec((CHUNK, GLA_HV), rev)],
        out_specs=[pl.BlockSpec((CHUNK, GLA_MAIN), rev), pl.BlockSpec((CHUNK, GLA_HK), rev),
                   pl.BlockSpec((1, GLA_DV), lambda i: (0, 0))],
        out_shape=[jax.ShapeDtypeStruct((t, GLA_MAIN), BF16), jax.ShapeDtypeStruct((t, GLA_HK), F32),
                   jax.ShapeDtypeStruct((1, GLA_DV), F32)],
        scratch_shapes=[pltpu.VMEM((GLA_HEADS, GLA_DV, GLA_DK), F32)],
        compiler_params=_params(),
    )(proj, z, norm_g, states, du)


def _mla_pre(cq, cos, sin, gq, gkv, w_uq, w_ukv):
    qlat = _rms_norm(cq[:, :MLA_RANK], gq)
    kvlat = _rms_norm(cq[:, MLA_RANK:2 * MLA_RANK], gkv)
    kr = cq[:, 2 * MLA_RANK:]
    scale = (MLA_NOPE + MLA_ROPE) ** -0.5
    q = mm_nn(qlat, w_uq) * scale
    kv = mm_nn(kvlat, w_ukv)
    n_nope = MLA_HEADS * MLA_NOPE
    ropes = []
    for h in range(MLA_HEADS):
        qr = q[:, n_nope + LANES * h:n_nope + LANES * (h + 1)]
        ropes.append(qr * cos + rot_half(qr) * sin)
    return q[:, :n_nope], jnp.concatenate(ropes, axis=1), kv, kr * cos + rot_half(kr) * sin


MLA_Q_TILE = 256


def _mla_attn_block(qn, qr, kv, kr, q0):
    tq, nk = qn.shape[0], kv.shape[0]
    s = mm_nt(qn, kv[:, :MLA_NOPE]) + mm_nt(qr, kr)
    visible = (_iota2((tq, nk), 1) // CHUNK) <= ((q0 + _iota2((tq, nk), 0)) // CHUNK)
    s = jnp.where(visible, s, -1e30)
    e = jnp.exp(s - jnp.max(s, -1, keepdims=True))
    p = e / jnp.sum(e, -1, keepdims=True)
    return mm_nn(p, kv[:, MLA_NOPE:])


def _mla_attn_fwd(qn, qr, kv, kr):
    t = qn.shape[0]

    def body(qn_ref, qr_ref, kv_ref, kr_ref, o_ref):
        for i in range(t // MLA_Q_TILE):
            rows = slice(i * MLA_Q_TILE, (i + 1) * MLA_Q_TILE)
            keys = slice(0, (i + 1) * MLA_Q_TILE)
            o = _mla_attn_block(qn_ref[rows, :].astype(F32), qr_ref[rows, :].astype(F32), kv_ref[keys, :].astype(F32),
                                kr_ref[keys, :].astype(F32), i * MLA_Q_TILE)
            o_ref[rows, :] = o.astype(o_ref.dtype)

    return pl.pallas_call(
        body, name="mla_attn_fwd", grid=(MLA_HEADS,),
        in_specs=[pl.BlockSpec((t, MLA_NOPE), lambda h: (0, h)), pl.BlockSpec((t, LANES), lambda h: (0, h)),
                  pl.BlockSpec((t, MLA_NOPE + MLA_V), lambda h: (0, h)), pl.BlockSpec((t, LANES), lambda h: (0, 0))],
        out_specs=pl.BlockSpec((t, MLA_V), lambda h: (0, h)),
        out_shape=jax.ShapeDtypeStruct((t, MLA_HEADS * MLA_V), BF16),
        compiler_params=_params(),
    )(qn, qr, kv, kr)


def _mla_attn_bwd(qn, qr, kv, kr, do):
    t = qn.shape[0]

    def body(qn_ref, qr_ref, kv_ref, kr_ref, do_ref, dqn_ref, dqr_ref, dkv_ref, dkr_ref):
        dkv_ref[...] = jnp.zeros_like(dkv_ref)

        @pl.when(pl.program_id(0) == 0)
        def _():
            dkr_ref[...] = jnp.zeros_like(dkr_ref)

        for i in range(t // MLA_Q_TILE):
            rows = slice(i * MLA_Q_TILE, (i + 1) * MLA_Q_TILE)
            keys = slice(0, (i + 1) * MLA_Q_TILE)
            f = functools.partial(_mla_attn_block, q0=i * MLA_Q_TILE)
            _, vjp = jax.vjp(f, qn_ref[rows, :].astype(F32), qr_ref[rows, :].astype(F32), kv_ref[keys, :].astype(F32),
                             kr_ref[keys, :].astype(F32))
            dqn, dqr, dkv, dkr = vjp(do_ref[rows, :].astype(F32))
            dqn_ref[rows, :] = dqn
            dqr_ref[rows, :] = dqr
            dkv_ref[keys, :] += dkv
            dkr_ref[keys, :] += dkr

    return pl.pallas_call(
        body, name="mla_attn_bwd", grid=(MLA_HEADS,),
        in_specs=[pl.BlockSpec((t, MLA_NOPE), lambda h: (0, h)), pl.BlockSpec((t, LANES), lambda h: (0, h)),
                  pl.BlockSpec((t, MLA_NOPE + MLA_V), lambda h: (0, h)), pl.BlockSpec((t, LANES), lambda h: (0, 0)),
                  pl.BlockSpec((t, MLA_V), lambda h: (0, h))],
        out_specs=[pl.BlockSpec((t, MLA_NOPE), lambda h: (0, h)), pl.BlockSpec((t, LANES), lambda h: (0, h)),
                   pl.BlockSpec((t, MLA_NOPE + MLA_V), lambda h: (0, h)), pl.BlockSpec((t, LANES), lambda h: (0, 0))],
        out_shape=[jax.ShapeDtypeStruct(qn.shape, F32), jax.ShapeDtypeStruct(qr.shape, F32),
                   jax.ShapeDtypeStruct(kv.shape, F32), jax.ShapeDtypeStruct(kr.shape, F32)],
        compiler_params=_params(),
    )(qn, qr, kv, kr, do)


def _rope_tables(pos_col, inv_freq_row):
    t = pos_col.shape[0]

    def body(pos_ref, f_ref, cos_ref, sin_ref):
        ang = pos_ref[...].astype(F32) * f_ref[...]
        live = _iota2(ang.shape, 1) < MLA_ROPE
        cos_ref[...] = jnp.where(live, jnp.cos(ang), 0.0)
        sin_ref[...] = jnp.where(live, jnp.sin(ang), 0.0)

    return pl.pallas_call(
        body, name="rope_tables", out_shape=[jax.ShapeDtypeStruct((t, LANES), F32)] * 2, compiler_params=_params(),
    )(pos_col, inv_freq_row)


CONV_COL_TILE = 256


def _conv_gate(b, c, u, w0, w1, w2):
    cu = c * u
    return b * (w2 * cu + w1 * shift_rows(cu, 1) + w0 * shift_rows(cu, 2))


def _conv_specs(t):
    nb = D_MODEL // CONV_COL_TILE
    return [pl.BlockSpec((t, CONV_COL_TILE), lambda j, part=part: (0, part * nb + j)) for part in range(3)]


def _conv_fwd(bcu, w):
    t = bcu.shape[0]

    def body(b_ref, c_ref, u_ref, w_ref, o_ref):
        o_ref[...] = _conv_gate(b_ref[...], c_ref[...], u_ref[...], w_ref[0:1, :], w_ref[1:2, :],
                                w_ref[2:3, :]).astype(o_ref.dtype)

    return pl.pallas_call(
        body, name="conv_fwd", grid=(D_MODEL // CONV_COL_TILE,),
        in_specs=_conv_specs(t) + [pl.BlockSpec((3, CONV_COL_TILE), lambda j: (0, j))],
        out_specs=pl.BlockSpec((t, CONV_COL_TILE), lambda j: (0, j)),
        out_shape=jax.ShapeDtypeStruct((t, D_MODEL), BF16), compiler_params=_params(),
    )(bcu, bcu, bcu, w)


def _conv_bwd(bcu, w, dout):
    t = bcu.shape[0]

    def body(b_ref, c_ref, u_ref, w_ref, do_ref, db_ref, dc_ref, du_ref, dw_ref):
        _, vjp = jax.vjp(_conv_gate, b_ref[...], c_ref[...], u_ref[...], w_ref[0:1, :], w_ref[1:2, :], w_ref[2:3, :])
        db, dc, du, dw0, dw1, dw2 = vjp(do_ref[...])
        db_ref[...] = db.astype(db_ref.dtype)
        dc_ref[...] = dc.astype(dc_ref.dtype)
        du_ref[...] = du.astype(du_ref.dtype)
        dw_ref[0:1, :] = dw0
        dw_ref[1:2, :] = dw1
        dw_ref[2:3, :] = dw2

    col = pl.BlockSpec((t, CONV_COL_TILE), lambda j: (0, j))
    return pl.pallas_call(
        body, name="conv_bwd", grid=(D_MODEL // CONV_COL_TILE,),
        in_specs=_conv_specs(t) + [pl.BlockSpec((3, CONV_COL_TILE), lambda j: (0, j)), col],
        out_specs=[col, col, col, pl.BlockSpec((3, CONV_COL_TILE), lambda j: (0, j))],
        out_shape=[jax.ShapeDtypeStruct((t, D_MODEL), BF16)] * 3 + [jax.ShapeDtypeStruct((3, D_MODEL), F32)],
        compiler_params=_params(),
    )(bcu, bcu, bcu, w, dout)


def _loss_head(y, target):
    t, d = y.shape
    tm = 256

    def body(y_ref, t_ref, loss_ref, dy_ref):
        @pl.when(pl.program_id(0) == 0)
        def _():
            loss_ref[...] = jnp.zeros_like(loss_ref)

        err = y_ref[...] - t_ref[...]
        dy_ref[...] = err * (1.0 / d)
        loss_ref[...] += 0.5 * jnp.sum(jnp.sum(err * err, axis=-1, keepdims=True) * (1.0 / d))

    tile = pl.BlockSpec((tm, d), lambda i: (i, 0))
    return pl.pallas_call(
        body, name="loss_head", grid=(t // tm,), in_specs=[tile, tile],
        out_specs=[pl.BlockSpec((8, LANES), lambda i: (0, 0)), tile],
        out_shape=[jax.ShapeDtypeStruct((8, LANES), F32), jax.ShapeDtypeStruct((t, d), F32)],
        compiler_params=_params(),
    )(y, target)


def _ln_epi(acc, res, g, b):
    a = ALPHA * res + acc
    return a, _layer_norm(a, g, b)


def _ln_fn(a, g, b):
    return (_layer_norm(a, g, b),)


def _relu_sq(h):
    r = jnp.maximum(h.astype(F32), 0.0)
    return r * r


def _pad_cols(w, n):
    return jnp.pad(w, ((0, 0), (0, n - w.shape[1])))


def _pad_rows(w, n):
    return jnp.pad(w, ((0, n - w.shape[0]), (0, 0)))


def _uq_to_kernel_layout(w_uq):
    w = w_uq.reshape(MLA_RANK, MLA_HEADS, MLA_NOPE + MLA_ROPE)
    nope = w[:, :, :MLA_NOPE].reshape(MLA_RANK, MLA_HEADS * MLA_NOPE)
    rope = jnp.pad(w[:, :, MLA_NOPE:], ((0, 0), (0, 0), (0, LANES - MLA_ROPE))).reshape(MLA_RANK, MLA_HEADS * LANES)
    return jnp.concatenate([nope, rope], axis=1)


def _uq_from_kernel_layout(w):
    nope = w[:, :MLA_HEADS * MLA_NOPE].reshape(MLA_RANK, MLA_HEADS, MLA_NOPE)
    rope = w[:, MLA_HEADS * MLA_NOPE:].reshape(MLA_RANK, MLA_HEADS, LANES)[:, :, :MLA_ROPE]
    return jnp.concatenate([nope, rope], axis=2).reshape(MLA_RANK, MLA_HEADS * (MLA_NOPE + MLA_ROPE))


def _step(x, p, positions, target, small, comm):
    t = x.shape[0]
    w = small
    freqs = ROPE_BASE ** (-jnp.arange(0, MLA_ROPE // 2, dtype=F32) * (2.0 / MLA_ROPE))
    freq_row = jnp.concatenate([freqs, freqs, jnp.zeros((LANES - MLA_ROPE,), F32)])[None, :]
    cos, sin = _rope_tables(positions.reshape(t, 1), freq_row)

    saved = []
    for i in range(DEPTH):
        j, kind = i // 3, i % 3
        wl = comm.gathered(i)
        wa, wb = wl["a"], wl["b"]
        s = {"x": x, "wl": wl}
        tok = comm.fwd_point(i, 0, x)
        if kind == 0:
            w_in = wl["gla_w_in"]
            s["w_main"] = w_in[:, :GLA_MAIN]
            s["w_lr"] = _pad_cols(w_in[:, GLA_MAIN:], LANES)
            s["w_up"] = _pad_rows(w["gla_w_gate_up"][j], LANES).astype(BF16)
            s["proj"] = _matmul(x, s["w_main"], name="gla_proj", tn=1024, after=tok)
            s["glr"] = _matmul(x, s["w_lr"], name="gla_lr", out_dtypes=(F32,))
            s["z"] = _matmul(s["glr"], s["w_up"], name="gla_gate", epi=lambda acc, b: (acc + b,),
                             epi_ins=(w["gla_b_gate"][j][None, :],), out_dtypes=(F32,))
            s["u"], s["states"] = _gla_fwd(s["proj"], s["z"], w["gla_norm_g"][j][None, :])
        elif kind == 1:
            s["w_in"] = _pad_cols(wl["mla_w_in"], MLA_IN_PAD)
            s["w_uq"] = _uq_to_kernel_layout(wl["mla_w_uq"])
            s["cq"] = _matmul(x, s["w_in"], name="mla_proj", tn=MLA_IN_PAD, out_dtypes=(F32,), after=tok)
            s["pre_params"] = (w["mla_q_norm"][j][None, :], w["mla_kv_norm"][j][None, :], s["w_uq"], wl["mla_w_ukv"])
            s["qn"], s["qr"], s["kv"], s["kr"] = _tile_fwd(_mla_pre, (s["cq"], cos, sin), s["pre_params"],
                                                           (BF16, BF16, BF16, BF16), tm=256, name="mla_pre_fwd")
            s["u"] = _mla_attn_fwd(s["qn"], s["qr"], s["kv"], s["kr"])
        else:
            s["bcu"] = _matmul(x, wl["conv"], name="conv_proj", tb=True, tn=REG_CONV[1], b_at=REG_CONV, out_dtypes=(F32,),
                               after=tok)
            s["u"] = _conv_fwd(s["bcu"], w["conv_w"][j])
        g0, b0 = w["ln_g"][i, 0][None, :], w["ln_b"][i, 0][None, :]
        g1, b1 = w["ln_g"][i, 1][None, :], w["ln_b"][i, 1][None, :]
        s["a1"], s["x1"] = _matmul(s["u"], wa, name="mixer_out_ln", tm=256, tn=D_MODEL, b_at=REG_WOUT, epi=_ln_epi,
                                   epi_ins=(x, g0, b0), out_dtypes=(F32, F32))
        s["hh"] = _matmul(s["x1"], wa, name="mlp_up", tb=True, tn=REG_W1T[1], b_at=REG_W1T)
        tok = comm.fwd_point(i, 1, s["hh"])
        s["a2"], s["x2"] = _matmul(s["hh"], wa, name="mlp_down_ln", tm=256, tn=D_MODEL, b_at=REG_W2, a_fn=_relu_sq,
                                   epi=_ln_epi, epi_ins=(s["x1"], g1, b1), out_dtypes=(F32, F32), after=tok)
        s["pp"] = _matmul(p[i], wb, name="ple_proj", tb=True, tn=D_MODEL, b_at=REG_WPT)
        tok = comm.fwd_point(i, 2, s["pp"])
        x, s["gt"] = _matmul(s["x2"], wa, name="ple_gate", tn=1024, b_at=REG_WG,
                             epi=lambda acc, xr, pp: (xr + jax.nn.sigmoid(acc) * pp.astype(F32), acc),
                             epi_ins=(s["x2"], s["pp"]), out_dtypes=(F32, BF16), after=tok)
        saved.append(s)

    loss_part, dx = _loss_head(x, target)

    gw = {n: [None] * WEIGHTS[n][0][0] for n in SMALL + REPLICATED}
    ln_g_grads, ln_b_grads = [[None, None] for _ in range(DEPTH)], [[None, None] for _ in range(DEPTH)]
    resid = lambda acc, r: (acc + ALPHA * r,)
    plus = lambda acc, r: (acc + r,)
    for i in reversed(range(DEPTH)):
        j, kind = i // 3, i % 3
        s = saved[i]
        wa = s["wl"]["a"]
        ga = lax.empty((N_DEV, A_ROWS, D_MODEL), BF16)
        gb = lax.empty((N_DEV, REG_WPT[1], PLE_DIM), BF16)
        layer_grads = {}
        tok = comm.bwd_point(i, 0, dx)

        def ple_bwd(dxo, gt, pp):
            sg = jax.nn.sigmoid(gt)
            return dxo * sg, dxo * pp * sg * (1.0 - sg)

        d_pp, d_gt = _tile_fwd(ple_bwd, (dx, s["gt"], s["pp"]), (), (BF16, BF16), tm=256, name="ple_bwd")
        gb = _matmul(d_pp, p[i], name="ple_proj_dw", ta=True, tm=REG_WPT[1], tn=PLE_DIM, out_at=REG_WPT, out_buf=gb, after=tok)
        ga = _matmul(s["x2"], d_gt, name="ple_gate_dw", ta=True, tm=REG_WG[1], tn=1024, out_at=REG_WG, out_buf=ga)
        dx2 = _matmul(d_gt, wa, name="ple_gate_dx", tb=True, tn=1024, b_at=REG_WG, epi=plus, epi_ins=(dx,),
                      out_dtypes=(F32,))
        g1, b1 = w["ln_g"][i, 1][None, :], w["ln_b"][i, 1][None, :]
        (d_a2,), (ln_g_grads[i][1], ln_b_grads[i][1]) = _tile_bwd(_ln_fn, (s["a2"],), (g1, b1), (dx2,), (F32,), tm=256,
                                                                    name="ln_bwd")
        tok = comm.bwd_point(i, 1, d_a2)
        ga = _matmul(s["hh"], d_a2, name="mlp_down_dw", ta=True, tm=REG_W2[1], tn=1024, a_fn=_relu_sq, out_at=REG_W2,
                     out_buf=ga, after=tok)
        d_hh = _matmul(d_a2, wa, name="mlp_down_dx", tb=True, tn=REG_W2[1], b_at=REG_W2,
                       epi=lambda acc, hh: (acc * 2.0 * jnp.maximum(hh.astype(F32), 0.0),), epi_ins=(s["hh"],))
        ga = _matmul(d_hh, s["x1"], name="mlp_up_dw", ta=True, tm=REG_W1T[1], tn=1024, out_at=REG_W1T, out_buf=ga)
        dx1 = _matmul(d_hh, wa, name="mlp_up_dx", tm=256, tn=1024, b_at=REG_W1T, epi=resid, epi_ins=(d_a2,),
                      out_dtypes=(F32,))
        g0, b0 = w["ln_g"][i, 0][None, :], w["ln_b"][i, 0][None, :]
        (d_a1,), (ln_g_grads[i][0], ln_b_grads[i][0]) = _tile_bwd(_ln_fn, (s["a1"],), (g0, b0), (dx1,), (F32,), tm=256,
                                                                    name="ln_bwd")
        ga = _matmul(s["u"], d_a1, name="mixer_out_dw", ta=True, tm=REG_WOUT[1], tn=1024, out_at=REG_WOUT, out_buf=ga)
        du = _matmul(d_a1, wa, name="mixer_out_dx", tb=True, tn=1024, b_at=REG_WOUT, out_dtypes=(F32,))
        if kind == 0:
            dproj, dz, dg = _gla_bwd(s["proj"], s["z"], w["gla_norm_g"][j][None, :], s["states"], du)
            gw["gla_norm_g"][j] = dg[0]
            gw["gla_b_gate"][j] = _tile_bwd(lambda zz, b: (zz + b,), (s["z"],), (w["gla_b_gate"][j][None, :],), (dz,), (),
                                            tm=256, name="gla_bias_bwd", diff_tiled=[])[1][0][0]
            gw["gla_w_gate_up"][j] = _matmul(s["glr"], dz, name="gla_gate_dw", ta=True, out_dtypes=(F32,))[:GLA_RANK]
            dglr = _matmul(dz, s["w_up"], name="gla_gate_dx", tb=True, out_dtypes=(F32,))
            dw_main = _matmul(s["x"], dproj, name="gla_proj_dw", ta=True, tn=1024, out_dtypes=(F32,))
            dw_lr = _matmul(s["x"], dglr, name="gla_lr_dw", ta=True, out_dtypes=(F32,))[:, :GLA_RANK]
            layer_grads["gla_w_in"] = jnp.concatenate([dw_main, dw_lr], axis=1)
            dx = _matmul(dproj, s["w_main"], name="gla_proj_dx", tb=True, tn=1024, epi=resid, epi_ins=(d_a1,),
                         out_dtypes=(F32,))
            dx = _matmul(dglr, s["w_lr"], name="gla_lr_dx", tb=True, tn=1024, epi=plus, epi_ins=(dx,), out_dtypes=(F32,))
        elif kind == 1:
            dqn, dqr, dkv, dkr = _mla_attn_bwd(s["qn"], s["qr"], s["kv"], s["kr"], du)
            (d_cq,), (dgq, dgkv, dw_uq, dw_ukv) = _tile_bwd(_mla_pre, (s["cq"], cos, sin), s["pre_params"],
                                                           (dqn, dqr, dkv, dkr), (BF16,), tm=256, name="mla_pre_bwd",
                                                           diff_tiled=[0])
            gw["mla_q_norm"][j], gw["mla_kv_norm"][j] = dgq[0], dgkv[0]
            layer_grads["mla_w_uq"] = _uq_from_kernel_layout(dw_uq)
            layer_grads["mla_w_ukv"] = dw_ukv
            layer_grads["mla_w_in"] = _matmul(s["x"], d_cq, name="mla_proj_dw", ta=True, tn=MLA_IN_PAD,
                                              out_dtypes=(F32,))[:, :MLA_IN]
            dx = _matmul(d_cq, s["w_in"], name="mla_proj_dx", tb=True, tn=1024, epi=resid, epi_ins=(d_a1,),
                         out_dtypes=(F32,))
        else:
            db, dc, du_, dcw = _conv_bwd(s["bcu"], w["conv_w"][j], du)
            gw["conv_w"][j] = dcw
            dbcu = jnp.concatenate([db, dc, du_], axis=1)
            layer_grads["conv"] = _matmul(dbcu, s["x"], name="conv_proj_dw", ta=True, tm=REG_CONV[1], tn=1024,
                                          out_at=REG_CONV, out_buf=lax.empty((N_DEV, REG_CONV[1], D_MODEL), BF16))
            dx = _matmul(dbcu, s["wl"]["conv"], name="conv_proj_dx", tn=1024, b_at=REG_CONV, epi=resid, epi_ins=(d_a1,),
                         out_dtypes=(F32,))
        layer_grads["a"], layer_grads["b"] = ga, gb
        comm.grads_ready(i, layer_grads)
        comm.bwd_point(i, 2, dx)

    gw["ln_g"] = [jnp.concatenate([a, b], axis=0) for a, b in ln_g_grads]
    gw["ln_b"] = [jnp.concatenate([a, b], axis=0) for a, b in ln_b_grads]
    return loss_part, dx, {n: jnp.stack(gw[n]).astype(F32) for n in gw}


MESH_IDS = pl.DeviceIdType.MESH
ANY = pl.BlockSpec(memory_space=pl.ANY)
HBM_SPEC = pl.BlockSpec(memory_space=pltpu.HBM)
SEM_SPEC = pl.BlockSpec(memory_space=pltpu.SEMAPHORE)
DATAFLOW_EFFECT = pltpu.SideEffectType.DATAFLOW_SIDE_EFFECTING
CORE_COPIES, CHIP_COPIES = 4, 3


def _my_place():
    return lax.axis_index("x"), lax.axis_index("y"), lax.axis_index("c")


def _other_chips(mx, my):
    return [(1 - mx, my), (mx, 1 - my), (1 - mx, 1 - my)]


def _remote(src, dst, send_sems, recv_sems, k, to):
    return pltpu.make_async_remote_copy(src_ref=src, dst_ref=dst, send_sem=send_sems.at[k], recv_sem=recv_sems.at[k],
                                        device_id=to, device_id_type=MESH_IDS)


def _gather_first_copies(n_arr):
    def make(bufs, send_sems, recv_sems):
        mx, my, mc = _my_place()
        mine = 4 * mx + 2 * my + mc
        peers = [(mx, my, 1 - mc)] + [(cx, cy, mc) for cx, cy in _other_chips(mx, my)]
        return [_remote(bufs[a].at[mine], bufs[a].at[mine], send_sems, recv_sems, (1 + CHIP_COPIES) * a + k, to)
                for a in range(n_arr) for k, to in enumerate(peers)]
    return make, (1 + CHIP_COPIES) * n_arr


def _gather_forward_copies(n_arr):
    def make(bufs, send_sems, recv_sems):
        mx, my, mc = _my_place()
        blocks = [4 * cx + 2 * cy + mc for cx, cy in _other_chips(mx, my)]
        return [_remote(bufs[a].at[blk], bufs[a].at[blk], send_sems, recv_sems, CHIP_COPIES * a + k, (mx, my, 1 - mc))
                for a in range(n_arr) for k, blk in enumerate(blocks)]
    return make, CHIP_COPIES * n_arr


def _scatter_core_copies(n_arr):
    def make(bufs, send_sems, recv_sems):
        mx, my, mc = _my_place()
        return [_remote(bufs[a].at[2 * k + (1 - mc)], bufs[n_arr + a].at[k], send_sems, recv_sems, CORE_COPIES * a + k,
                        (mx, my, 1 - mc)) for a in range(n_arr) for k in range(CORE_COPIES)]
    return make, CORE_COPIES * n_arr


def _scatter_chip_copies(n_arr):
    def make(bufs, send_sems, recv_sems):
        mx, my, mc = _my_place()
        return [_remote(bufs[a].at[2 * cx + cy], bufs[n_arr + a].at[k], send_sems, recv_sems, CHIP_COPIES * a + k,
                        (cx, cy, mc)) for a in range(n_arr) for k, (cx, cy) in enumerate(_other_chips(mx, my))]
    return make, CHIP_COPIES * n_arr


def _exchange(name, bufs, copies):
    make, n_copies = copies
    n = len(bufs)

    def body(*refs):
        descs = make(refs[:n], refs[2 * n], refs[2 * n + 1])
        for cp in descs:
            cp.start()
        for cp in descs:
            cp.wait()

    return pl.pallas_call(
        body, name=name, out_shape=[jax.ShapeDtypeStruct(b.shape, b.dtype) for b in bufs], in_specs=[ANY] * n,
        out_specs=[ANY] * n, input_output_aliases={i: i for i in range(n)},
        scratch_shapes=[pltpu.SemaphoreType.DMA((n_copies,)), pltpu.SemaphoreType.DMA((n_copies,))],
    )(*bufs)


def _exchange_start(name, bufs, copies, after):
    make, n_copies = copies
    n = len(bufs)

    def body(*refs):
        for cp in make(refs[:n], refs[n + 1], refs[n + 2]):
            cp.start()
        refs[-1][...] = jnp.zeros_like(refs[-1])

    outs = pl.pallas_call(
        body, name=name,
        out_shape=(pltpu.SemaphoreType.DMA((n_copies,)), pltpu.SemaphoreType.DMA((n_copies,)),
                   *[pltpu.HBM(b.shape, b.dtype) for b in bufs], jax.ShapeDtypeStruct((8, LANES), F32)),
        in_specs=[HBM_SPEC] * n + [ANY],
        out_specs=(SEM_SPEC, SEM_SPEC, *[HBM_SPEC] * n, pl.BlockSpec(memory_space=pltpu.VMEM)),
        input_output_aliases={i: 2 + i for i in range(n)},
        compiler_params=pltpu.CompilerParams(has_side_effects=DATAFLOW_EFFECT),
    )(*[pltpu.with_memory_space_constraint(b, pltpu.HBM) for b in bufs], after)
    return (outs[0], outs[1]), list(outs[2:2 + n]), outs[-1]


def _exchange_wait(name, sems, bufs, copies, after):
    make, _ = copies
    n = len(bufs)

    def body(*refs):
        for cp in make(refs[:n], refs[n], refs[n + 1]):
            cp.wait_send()
            cp.wait_recv()

    return list(pl.pallas_call(
        body, name=name, out_shape=[pltpu.HBM(b.shape, b.dtype) for b in bufs],
        in_specs=[HBM_SPEC] * n + [SEM_SPEC, SEM_SPEC, ANY], out_specs=[HBM_SPEC] * n,
        input_output_aliases={i: i for i in range(n)},
        compiler_params=pltpu.CompilerParams(has_side_effects=DATAFLOW_EFFECT),
    )(*bufs, *sems, after))


def _row_tile(r):
    for cand in (PACK_ROW_TILE, 128, 64, 32, 16, 8):
        if r % cand == 0:
            return cand
    return r


def _pair_sum(g, recv, my_c):
    _, r, c = g.shape
    tr = _row_tile(r)

    def body(c_ref, g_ref, r_ref, o_ref):
        o_ref[...] = (g_ref[...].astype(F32) + r_ref[...].astype(F32)).astype(o_ref.dtype)

    return pl.pallas_call(
        body, name="rs_pair_sum", out_shape=jax.ShapeDtypeStruct((4, r, c), g.dtype),
        grid_spec=pltpu.PrefetchScalarGridSpec(
            num_scalar_prefetch=1, grid=(4, r // tr),
            in_specs=[pl.BlockSpec((1, tr, c), lambda n, i, cr: (2 * n + cr[0], i, 0)),
                      pl.BlockSpec((1, tr, c), lambda n, i, cr: (n, i, 0))],
            out_specs=pl.BlockSpec((1, tr, c), lambda n, i, cr: (n, i, 0))),
        compiler_params=_params(),
    )(my_c, g, recv)


def _chip_sum(h, recv, my_chip):
    _, r, c = h.shape
    tr = _row_tile(r)

    def body(j_ref, h_ref, r0_ref, r1_ref, r2_ref, o_ref):
        o_ref[...] = ((h_ref[0].astype(F32) + r0_ref[0].astype(F32)) + r1_ref[0].astype(F32)) + r2_ref[0].astype(F32)

    return pl.pallas_call(
        body, name="rs_chip_sum", out_shape=jax.ShapeDtypeStruct((r, c), F32),
        grid_spec=pltpu.PrefetchScalarGridSpec(
            num_scalar_prefetch=1, grid=(r // tr,),
            in_specs=[pl.BlockSpec((1, tr, c), lambda i, jr: (jr[0], i, 0))]
            + [pl.BlockSpec((1, tr, c), lambda i, jr, n=n: (n, i, 0)) for n in range(3)],
            out_specs=pl.BlockSpec((tr, c), lambda i, jr: (i, 0))),
        compiler_params=_params(),
    )(my_chip, h, recv, recv, recv)


def _sum_blocks(g):
    n, r, c = g.shape

    def body(g_ref, o_ref):
        acc = g_ref[0]
        for k in range(1, n):
            acc = acc + g_ref[k]
        o_ref[...] = acc

    return pl.pallas_call(body, name="sum_blocks", out_shape=jax.ShapeDtypeStruct((r, c), F32), compiler_params=_params())(g)


def _pack(flat_parts, cols, row_multiple, dtype):
    flat = jnp.concatenate([f.astype(dtype) for f in flat_parts])
    per_row_block = cols * row_multiple
    padded = -(-flat.shape[0] // per_row_block) * per_row_block
    return jnp.pad(flat, (0, padded - flat.shape[0])).reshape(padded // cols, cols)


def _shard_shape(name):
    shape, axis = WEIGHTS[name]
    if axis is None:
        return shape
    return tuple(s // N_DEV if a == axis else s for a, s in enumerate(shape))


def _size(shape):
    n = 1
    for s in shape:
        n *= s
    return n


def _unshard(blocks, name):
    _, axis = WEIGHTS[name]
    return jnp.concatenate([blocks[k] for k in range(N_DEV)], axis=axis)


def _unpack_blocks(flat, names, lead):
    out, off = {}, 0
    for n in names:
        shp = _shard_shape(n)[1:] if lead else _shard_shape(n)
        out[n] = flat[..., off:off + _size(shp)].reshape(flat.shape[:-1] + shp)
        off += _size(shp)
    return out


def _layer_slabs(shard, i):
    j, kind = i // 3, i % 3
    w_out = (shard["gla_w_out"], shard["mla_w_out"], shard["conv_w_out"])[kind][j]
    out = {"a": jnp.concatenate([shard["mlp_w2"][i], shard["mlp_w1"][i].T, w_out, shard["ple_w_gate"][i]], axis=0).astype(BF16),
           "b": shard["ple_w_proj"][i].T.astype(BF16)}
    if kind == 0:
        out["gla"] = shard["gla_w_in"][j].astype(BF16)
    elif kind == 1:
        out["mla"] = _pack([shard[n][j].reshape(-1) for n in MLA_PACKED], PACK_COLS, PACK_ROW_TILE, BF16)
    else:
        out["conv"] = shard["conv_w_in"][j].T.astype(BF16)
    return out


def _layer_weights(landed, i):
    kind = i % 3
    out = {"a": landed["a"], "b": landed["b"]}
    if kind == 0:
        out["gla_w_in"] = jnp.concatenate([landed["gla"][k] for k in range(N_DEV)], axis=1)
    elif kind == 1:
        blocks = _unpack_blocks(landed["mla"].reshape(N_DEV, -1), MLA_PACKED, lead=True)
        for n in MLA_PACKED:
            out[n] = jnp.concatenate([blocks[n][k] for k in range(N_DEV)], axis=WEIGHTS[n][1] - 1)
    else:
        out["conv"] = landed["conv"]
    return out


def _layer_grad_buffers(layer_grads, i):
    kind = i % 3
    out = {"a": layer_grads["a"], "b": layer_grads["b"]}
    if kind == 0:
        out["gla"] = jnp.stack(jnp.split(layer_grads["gla_w_in"], N_DEV, axis=1)).astype(BF16)
    elif kind == 1:
        parts = [jnp.stack(jnp.split(layer_grads[n], N_DEV, axis=WEIGHTS[n][1] - 1)).reshape(N_DEV, -1) for n in MLA_PACKED]
        cat = jnp.concatenate(parts, axis=1).astype(BF16)
        per = PACK_COLS * PACK_ROW_TILE
        padded = -(-cat.shape[1] // per) * per
        out["mla"] = jnp.pad(cat, ((0, 0), (0, padded - cat.shape[1]))).reshape(N_DEV, padded // PACK_COLS, PACK_COLS)
    else:
        out["conv"] = layer_grads["conv"]
    return out


class _Overlap:
    def __init__(self, shard):
        mx, my, mc = _my_place()
        self.my_c = mc.astype(jnp.int32).reshape(1)
        self.my_chip = (2 * mx + my).astype(jnp.int32).reshape(1)
        mine = 4 * mx + 2 * my + mc
        self.keys, self.landing = [], []
        for i in range(DEPTH):
            slabs = _layer_slabs(shard, i)
            self.keys.append(list(slabs))
            self.landing.append([lax.dynamic_update_index_in_dim(lax.empty((N_DEV, *v.shape), v.dtype), v, mine, 0)
                                 for v in slabs.values()])
        self.landed = [None] * DEPTH
        self.pending = None
        self.grad_bufs = [None] * DEPTH
        self.reduced = [None] * DEPTH
        bufs = _exchange("ag_first_l0", self.landing[0], _gather_first_copies(len(self.landing[0])))
        self.landed[0] = _exchange("ag_forward_l0", bufs, _gather_forward_copies(len(bufs)))

    def gathered(self, i):
        return _layer_weights(dict(zip(self.keys[i], self.landed[i])), i)

    def fwd_point(self, i, k, dep):
        nxt = i + 1
        if nxt >= DEPTH:
            return None
        n = len(self.landing[nxt])
        if k == 0:
            self.pending = _exchange_start(f"ag_first_start_l{nxt}", self.landing[nxt], _gather_first_copies(n), dep)
            return self.pending[2]
        if k == 1:
            sems, bufs, _ = self.pending
            bufs = _exchange_wait(f"ag_first_wait_l{nxt}", sems, bufs, _gather_first_copies(n), dep)
            self.pending = _exchange_start(f"ag_forward_start_l{nxt}", bufs, _gather_forward_copies(n), dep)
            return self.pending[2]
        sems, bufs, _ = self.pending
        self.landed[nxt] = _exchange_wait(f"ag_forward_wait_l{nxt}", sems, bufs, _gather_forward_copies(n), dep)
        return None

    def grads_ready(self, i, layer_grads):
        self.grad_bufs[i] = _layer_grad_buffers(layer_grads, i)

    def bwd_point(self, i, k, dep):
        src = i + 1
        if src >= DEPTH:
            return None
        gs = list(self.grad_bufs[src].values())
        n = len(gs)
        if k == 0:
            land = [lax.empty((4, *g.shape[1:]), g.dtype) for g in gs]
            self.pending = _exchange_start(f"rs_cores_start_l{src}", gs + land, _scatter_core_copies(n), dep)
            return self.pending[2]
        if k == 1:
            sems, bufs, _ = self.pending
            bufs = _exchange_wait(f"rs_cores_wait_l{src}", sems, bufs, _scatter_core_copies(n), dep)
            hs = [_pair_sum(g, r, self.my_c) for g, r in zip(bufs[:n], bufs[n:])]
            land = [lax.empty((3, *h.shape[1:]), h.dtype) for h in hs]
            self.pending = _exchange_start(f"rs_chips_start_l{src}", hs + land, _scatter_chip_copies(n), dep)
            return self.pending[2]
        sems, bufs, _ = self.pending
        bufs = _exchange_wait(f"rs_chips_wait_l{src}", sems, bufs, _scatter_chip_copies(n), dep)
        self.reduced[src] = dict(zip(self.grad_bufs[src], [_chip_sum(h, r, self.my_chip) for h, r in zip(bufs[:n], bufs[n:])]))
        return None

    def finish(self):
        gs = list(self.grad_bufs[0].values())
        n = len(gs)
        bufs = _exchange("rs_cores_l0", gs + [lax.empty((4, *g.shape[1:]), g.dtype) for g in gs], _scatter_core_copies(n))
        hs = [_pair_sum(g, r, self.my_c) for g, r in zip(bufs[:n], bufs[n:])]
        bufs = _exchange("rs_chips_l0", hs + [lax.empty((3, *h.shape[1:]), h.dtype) for h in hs], _scatter_chip_copies(n))
        self.reduced[0] = dict(zip(self.grad_bufs[0], [_chip_sum(h, r, self.my_chip) for h, r in zip(bufs[:n], bufs[n:])]))
        return self.reduced


def _all_gather_small(x, name):
    mx, my, mc = _my_place()
    land = lax.dynamic_update_index_in_dim(lax.empty((N_DEV, *x.shape), x.dtype), x, 4 * mx + 2 * my + mc, 0)
    (land,) = _exchange(name + "_first", [land], _gather_first_copies(1))
    (land,) = _exchange(name + "_forward", [land], _gather_forward_copies(1))
    return land


def _shard_grads(reduced):
    def rows(a, reg):
        return a[reg[0] * reg[1]:(reg[0] + 1) * reg[1]]

    a = [reduced[i]["a"] for i in range(DEPTH)]
    w_out = [rows(a[i], REG_WOUT) for i in range(DEPTH)]
    mla = _unpack_blocks(reduced[1]["mla"].reshape(-1), MLA_PACKED, lead=True)
    out = {
        "mlp_w2": jnp.stack([rows(a[i], REG_W2) for i in range(DEPTH)]),
        "mlp_w1": jnp.stack([rows(a[i], REG_W1T).T for i in range(DEPTH)]),
        "gla_w_out": jnp.stack([w_out[0], w_out[3]]), "mla_w_out": w_out[1][None], "conv_w_out": w_out[2][None],
        "ple_w_gate": jnp.stack([rows(a[i], REG_WG) for i in range(DEPTH)]),
        "ple_w_proj": jnp.stack([reduced[i]["b"].T for i in range(DEPTH)]),
        "conv_w_in": reduced[2]["conv"].T[None],
        "gla_w_in": jnp.stack([reduced[0]["gla"], reduced[3]["gla"]]),
    }
    out.update({n: mla[n][None] for n in MLA_PACKED})
    return out


def _adamw(w, g, m, v, name):
    shape = w.shape
    cols = shape[-1]
    rows = _size(shape) // cols
    tr = rows
    for cand in (512, 256, 128, 64, 32, 16, 8):
        if rows > cand and rows % cand == 0:
            tr = cand
            break

    def body(w_ref, g_ref, m_ref, v_ref, d_ref, mo_ref, vo_ref):
        gv = g_ref[...]
        m2 = ADAM_B1 * m_ref[...] + (1.0 - ADAM_B1) * gv
        v2 = ADAM_B2 * v_ref[...] + (1.0 - ADAM_B2) * (gv * gv)
        m_hat = m2 / (1.0 - ADAM_B1 ** ADAM_STEP)
        v_hat = v2 / (1.0 - ADAM_B2 ** ADAM_STEP)
        d_ref[...] = -ADAM_LR * (m_hat / (jnp.sqrt(v_hat) + ADAM_EPS) + ADAM_WD * w_ref[...])
        mo_ref[...] = m2
        vo_ref[...] = v2

    spec = pl.BlockSpec((tr, cols), lambda i: (i, 0))
    outs = pl.pallas_call(
        body, name="adamw_" + name, grid=(rows // tr,), in_specs=[spec] * 4, out_specs=[spec] * 3,
        out_shape=[jax.ShapeDtypeStruct((rows, cols), F32)] * 3, compiler_params=_params(),
    )(*[a.reshape(rows, cols) for a in (w, g, m, v)])
    return [o.reshape(shape) for o in outs]


def kernel(x, p, positions, gla_w_in, gla_w_gate_up, gla_b_gate, gla_norm_g, gla_w_out, mla_w_in, mla_q_norm, mla_kv_norm, mla_w_uq, mla_w_ukv, mla_w_out, conv_w_in, conv_w, conv_w_out, ln_g, ln_b, mlp_w1, mlp_w2, ple_w_gate, ple_w_proj, loss_target, m_gla_w_in, m_gla_w_gate_up, m_gla_b_gate, m_gla_norm_g, m_gla_w_out, m_mla_w_in, m_mla_q_norm, m_mla_kv_norm, m_mla_w_uq, m_mla_w_ukv, m_mla_w_out, m_conv_w_in, m_conv_w, m_conv_w_out, m_ln_g, m_ln_b, m_mlp_w1, m_mlp_w2, m_ple_w_gate, m_ple_w_proj, v_gla_w_in, v_gla_w_gate_up, v_gla_b_gate, v_gla_norm_g, v_gla_w_out, v_mla_w_in, v_mla_q_norm, v_mla_kv_norm, v_mla_w_uq, v_mla_w_ukv, v_mla_w_out, v_conv_w_in, v_conv_w, v_conv_w_out, v_ln_g, v_ln_b, v_mlp_w1, v_mlp_w2, v_ple_w_gate, v_ple_w_proj):
    args = locals()
    shard = {n: args[n] for n in WEIGHT_NAMES}
    mom = {n: args["m_" + n] for n in WEIGHT_NAMES}
    var = {n: args["v_" + n] for n in WEIGHT_NAMES}
    mx, my, mc = _my_place()

    small_all = _all_gather_small(_pack([shard[n].reshape(-1) for n in SMALL], LANES, 8, F32), "ag_small")
    small = {n: shard[n] for n in REPLICATED}
    small.update({n: _unshard(blk, n) for n, blk in _unpack_blocks(small_all.reshape(N_DEV, -1), SMALL, lead=False).items()})

    comm = _Overlap(shard)
    loss_part, grad_x, small_grads = _step(x[0], p[:, 0], positions[0], loss_target[0], small, comm)
    loss = lax.psum(loss_part[0, 0], MESH_AXES)
    my_grads = _shard_grads(comm.finish())

    small_parts = [small_grads[n].reshape(-1) for n in SMALL + REPLICATED]
    red_small = _sum_blocks(_all_gather_small(_pack(small_parts, LANES, 8, F32), "ag_small_grads")).reshape(-1)
    off = 0
    dev = 4 * mx + 2 * my + mc
    for n in SMALL + REPLICATED:
        shape, axis = WEIGHTS[n]
        full_g = red_small[off:off + _size(shape)].reshape(shape)
        off += _size(shape)
        if axis is None:
            my_grads[n] = full_g
        else:
            width = shape[axis] // N_DEV
            my_grads[n] = lax.dynamic_slice_in_dim(full_g, dev * width, width, axis=axis)

    deltas, new_m, new_v = {}, {}, {}
    for n in WEIGHT_NAMES:
        deltas[n], new_m[n], new_v[n] = _adamw(shard[n], my_grads[n], mom[n], var[n], n)
    return (loss, grad_x[None], *[my_grads[n] for n in WEIGHT_NAMES], *[deltas[n] for n in WEIGHT_NAMES],
            *[new_m[n] for n in WEIGHT_NAMES], *[new_v[n] for n in WEIGHT_NAMES])
```

```python
import functools

import jax
import jax.numpy as jnp
from jax import lax
from jax.experimental import pallas as pl
from jax.experimental.pallas import tpu as pltpu

F32, BF16 = jnp.float32, jnp.bfloat16
HIGHEST = lax.Precision.HIGHEST
MESH_AXES = ("x", "y", "c")
N_DEV = 8

D_MODEL = 1024
SEQ = 2048
DEPTH = 4
CHUNK = 64
ALPHA = (2 * DEPTH) ** 0.25
LN_EPS = 1e-5
RMS_EPS = 1e-6
PLE_DIM = 256
D_FF = 4 * D_MODEL
GLA_HEADS = 4
GLA_DK = 128
GLA_DV = 256
GLA_RANK = 16
GLA_TAU = 16.0
GLA_HK = GLA_HEADS * GLA_DK
GLA_HV = GLA_HEADS * GLA_DV
GLA_MAIN = 2 * GLA_HK + GLA_HV + D_MODEL
MLA_HEADS = 8
MLA_NOPE = 128
MLA_ROPE = 64
MLA_V = 128
MLA_RANK = 256
MLA_IN = 2 * MLA_RANK + MLA_ROPE
MLA_IN_PAD = 640
ROPE_BASE = 10000.0
LANES = 128
ADAM_LR, ADAM_B1, ADAM_B2, ADAM_EPS, ADAM_WD, ADAM_STEP = 0.001, 0.9, 0.999, 1e-08, 0.01, 10

V7X_VMEM_LIMIT_BYTES = 56 * 1024 * 1024
PACK_COLS = 1024
PACK_ROW_TILE = 256

WEIGHTS = {
    "gla_w_in": ((2, 1024, 3088), 2), "gla_w_gate_up": ((2, 16, 512), 2), "gla_b_gate": ((2, 512), 1),
    "gla_norm_g": ((2, 256), 1), "gla_w_out": ((2, 1024, 1024), 1), "mla_w_in": ((1, 1024, 576), 1),
    "mla_q_norm": ((1, 256), None), "mla_kv_norm": ((1, 256), None), "mla_w_uq": ((1, 256, 1536), 2),
    "mla_w_ukv": ((1, 256, 2048), 2), "mla_w_out": ((1, 1024, 1024), 1), "conv_w_in": ((1, 1024, 3072), 2),
    "conv_w": ((1, 3, 1024), 2), "conv_w_out": ((1, 1024, 1024), 1), "ln_g": ((4, 2, 1024), 2),
    "ln_b": ((4, 2, 1024), 2), "mlp_w1": ((4, 1024, 4096), 2), "mlp_w2": ((4, 4096, 1024), 1),
    "ple_w_gate": ((4, 1024, 1024), 1), "ple_w_proj": ((4, 256, 1024), 2),
}
WEIGHT_NAMES = list(WEIGHTS)
REG_W2, REG_W1T, REG_WOUT, REG_WG = (0, 512), (1, 512), (8, 128), (9, 128)
A_ROWS = 1280
REG_CONV = (0, 384)
REG_WPT = (0, 128)
MLA_PACKED = ["mla_w_in", "mla_w_uq", "mla_w_ukv"]
SMALL = ["gla_w_gate_up", "gla_b_gate", "gla_norm_g", "conv_w", "ln_g", "ln_b"]
REPLICATED = ["mla_q_norm", "mla_kv_norm"]


def _params(**kw):
    return pltpu.CompilerParams(vmem_limit_bytes=V7X_VMEM_LIMIT_BYTES, **kw)


def _dot(a, b, ca, cb, precision=None):
    return lax.dot_general(a, b, (((ca,), (cb,)), ((), ())), precision=precision, preferred_element_type=F32)


def _nn(a, b):
    return _dot(a.astype(BF16), b.astype(BF16), 1, 0)


def _nt(a, b):
    return _dot(a.astype(BF16), b.astype(BF16), 1, 1)


def _tn(a, b):
    return _dot(a.astype(BF16), b.astype(BF16), 0, 0)


@jax.custom_vjp
def mm_nn(a, b):
    return _nn(a, b)


def _mm_nn_fwd(a, b):
    return _nn(a, b), (a, b)


def _mm_nn_bwd(res, g):
    a, b = res
    return _nt(g, b).astype(a.dtype), _tn(a, g).astype(b.dtype)


mm_nn.defvjp(_mm_nn_fwd, _mm_nn_bwd)


@jax.custom_vjp
def mm_nt(a, b):
    return _nt(a, b)


def _mm_nt_fwd(a, b):
    return _nt(a, b), (a, b)


def _mm_nt_bwd(res, g):
    a, b = res
    return _nn(g, b).astype(a.dtype), _tn(g, a).astype(b.dtype)


mm_nt.defvjp(_mm_nt_fwd, _mm_nt_bwd)


@jax.custom_vjp
def mm_tn(a, b):
    return _tn(a, b)


def _mm_tn_fwd(a, b):
    return _tn(a, b), (a, b)


def _mm_tn_bwd(res, g):
    a, b = res
    return _nt(b, g).astype(a.dtype), _nn(a, g).astype(b.dtype)


mm_tn.defvjp(_mm_tn_fwd, _mm_tn_bwd)


def _iota2(shape, dim):
    return lax.broadcasted_iota(jnp.int32, shape, dim)


@jax.custom_vjp
def cumsum_rows(x):
    n = x.shape[0]
    tri = (_iota2((n, n), 0) >= _iota2((n, n), 1)).astype(F32)
    return _dot(tri, x, 1, 0, precision=HIGHEST)


def _cumsum_fwd(x):
    return cumsum_rows(x), None


def _cumsum_bwd(_, g):
    n = g.shape[0]
    tri_t = (_iota2((n, n), 0) <= _iota2((n, n), 1)).astype(F32)
    return (_dot(tri_t, g, 1, 0, precision=HIGHEST),)


cumsum_rows.defvjp(_cumsum_fwd, _cumsum_bwd)


def _rot_matrix(transposed):
    i, j = _iota2((LANES, LANES), 0), _iota2((LANES, LANES), 1)
    if transposed:
        i, j = j, i
    half = MLA_ROPE // 2
    plus = (i == j - half) & (j >= half) & (j < MLA_ROPE)
    minus = (i == j + half) & (j < half)
    return plus.astype(F32) - minus.astype(F32)


@jax.custom_vjp
def rot_half(x):
    return _dot(x, _rot_matrix(False), 1, 0, precision=HIGHEST)


def _rot_fwd(x):
    return rot_half(x), None


def _rot_bwd(_, g):
    return (_dot(g, _rot_matrix(True), 1, 0, precision=HIGHEST),)


rot_half.defvjp(_rot_fwd, _rot_bwd)


def _shift_rows_raw(x, s):
    n = x.shape[0]
    row = _iota2(x.shape, 0)
    rolled = pltpu.roll(x, s % n, 0)
    keep = (row >= s) if s > 0 else (row < n + s)
    return jnp.where(keep, rolled, 0.0)


@functools.partial(jax.custom_vjp, nondiff_argnums=(1,))
def shift_rows(x, s):
    return _shift_rows_raw(x, s)


def _shift_fwd(x, s):
    return _shift_rows_raw(x, s), None


def _shift_bwd(s, _, g):
    return (_shift_rows_raw(g, -s),)


shift_rows.defvjp(_shift_fwd, _shift_bwd)


def _layer_norm(a, g, b):
    mu = jnp.mean(a, -1, keepdims=True)
    xc = a - mu
    var = jnp.mean(xc * xc, -1, keepdims=True)
    return xc * lax.rsqrt(var + LN_EPS) * g + b


def _rms_norm(a, g):
    return a * lax.rsqrt(jnp.mean(a * a, -1, keepdims=True) + RMS_EPS) * g


def _log_sigmoid(z):
    return jnp.minimum(z, 0.0) - jnp.log(1.0 + jnp.exp(-jnp.abs(z)))


def _matmul(a, b, *, name, ta=False, tb=False, tm=512, tn=512, a_fn=None, epi=None, epi_ins=(), out_dtypes=(BF16,),
            b_at=None, out_at=None, out_buf=None, after=None):
    m = a.shape[1] if ta else a.shape[0]
    k = a.shape[0] if ta else a.shape[1]
    if b_at is None:
        n, kb = (b.shape[0], b.shape[1]) if tb else (b.shape[1], b.shape[0])
    else:
        rb, r = b_at
        n, kb = (N_DEV * r, b.shape[2]) if tb else (b.shape[2], N_DEV * r)
    assert kb == k, (name, a.shape, b.shape, k, kb)
    tm, tn = min(tm, m), min(tn, n)
    assert m % tm == 0 and n % tn == 0, (name, m, n, tm, tn)
    a_spec = pl.BlockSpec((k, tm), lambda i, j: (0, i)) if ta else pl.BlockSpec((tm, k), lambda i, j: (i, 0))
    if b_at is None:
        b_spec = pl.BlockSpec((tn, k), lambda i, j: (j, 0)) if tb else pl.BlockSpec((k, tn), lambda i, j: (0, j))
        load_b = lambda ref: ref[...]
    elif tb and tn == n:
        b_spec = pl.BlockSpec((N_DEV, r, k), lambda i, j: (0, rb, 0))
        load_b = lambda ref: ref[...].reshape(n, k)
    elif tb:
        assert tn == r, (name, tn, r)
        b_spec = pl.BlockSpec((1, r, k), lambda i, j: (j, rb, 0))
        load_b = lambda ref: ref[0]
    else:
        b_spec = pl.BlockSpec((N_DEV, r, tn), lambda i, j: (0, rb, j))
        load_b = lambda ref: ref[...].reshape(k, tn)
    e_specs = []
    for e in epi_ins:
        if e.shape == (1, n):
            e_specs.append(pl.BlockSpec((1, tn), lambda i, j: (0, j)))
        else:
            assert e.shape == (m, n), (name, e.shape, m, n)
            e_specs.append(pl.BlockSpec((tm, tn), lambda i, j: (i, j)))
    n_epi = len(epi_ins)
    ca, cb = (0 if ta else 1), (1 if tb else 0)
    operands = [a, b, *epi_ins]
    in_specs = [a_spec, b_spec, *e_specs]
    if out_at is None:
        out_specs = [pl.BlockSpec((tm, tn), lambda i, j: (i, j)) for _ in out_dtypes]
        out_shape = [jax.ShapeDtypeStruct((m, n), dt) for dt in out_dtypes]
        aliases, n_buf = {}, 0
    else:
        orb, orows = out_at
        assert len(out_dtypes) == 1 and orows % tm == 0 and m == N_DEV * orows and n == out_buf.shape[2], (name, m, n)
        per = orows // tm
        out_specs = [pl.BlockSpec((1, tm, tn), lambda i, j: (i // per, orb * per + i % per, j))]
        out_shape = [jax.ShapeDtypeStruct(out_buf.shape, out_buf.dtype)]
        operands.append(out_buf)
        in_specs.append(pl.BlockSpec(memory_space=pl.ANY))
        aliases, n_buf = {len(operands) - 1: 0}, 1
    for dep in ([] if after is None else after if isinstance(after, (list, tuple)) else [after]):
        if dep is not None:
            operands.append(dep)
            in_specs.append(pl.BlockSpec(memory_space=pl.ANY))
            n_buf += 1

    def body(a_ref, b_ref, *rest):
        av = a_ref[...]
        if a_fn is not None:
            av = a_fn(av)
        acc = _dot(av.astype(BF16), load_b(b_ref).astype(BF16), ca, cb)
        outs = epi(acc, *[r_[...] for r_ in rest[:n_epi]]) if epi is not None else (acc,)
        for o_ref, val in zip(rest[n_epi + n_buf:], outs):
            o_ref[...] = val.astype(o_ref.dtype).reshape(o_ref.shape)

    outs = pl.pallas_call(
        body, name=name, grid=(m // tm, n // tn), in_specs=in_specs, out_specs=out_specs, out_shape=out_shape,
        input_output_aliases=aliases, compiler_params=_params(),
    )(*operands)
    return outs[0] if len(outs) == 1 else tuple(outs)


def _tile_fwd(f, tiled, params, out_dtypes, *, tm, name):
    t = tiled[0].shape[0]
    assert t % tm == 0
    out_avals = jax.eval_shape(f, *[jax.ShapeDtypeStruct((tm, x.shape[1]), F32) for x in tiled],
                               *[jax.ShapeDtypeStruct(p.shape, F32) for p in params])
    nt, npar = len(tiled), len(params)

    def body(*refs):
        ins = [r[...].astype(F32) for r in refs[:nt + npar]]
        outs = f(*ins)
        for o_ref, val in zip(refs[nt + npar:], outs):
            o_ref[...] = val.astype(o_ref.dtype)

    return pl.pallas_call(
        body, name=name, grid=(t // tm,),
        in_specs=[pl.BlockSpec((tm, x.shape[1]), lambda i: (i, 0)) for x in tiled]
        + [pl.BlockSpec(p.shape, lambda i: (0, 0)) for p in params],
        out_specs=[pl.BlockSpec((tm, o.shape[1]), lambda i: (i, 0)) for o in out_avals],
        out_shape=[jax.ShapeDtypeStruct((t, o.shape[1]), dt) for o, dt in zip(out_avals, out_dtypes)],
        compiler_params=_params(),
    )(*tiled, *params)


def _tile_bwd(f, tiled, params, cots, d_tiled_dtypes, *, tm, name, diff_tiled=None):
    t = tiled[0].shape[0]
    assert t % tm == 0
    nt, npar, nc = len(tiled), len(params), len(cots)
    diff_tiled = list(range(nt)) if diff_tiled is None else diff_tiled

    def body(*refs):
        ins = [r[...].astype(F32) for r in refs[:nt + npar]]
        cts = [r[...].astype(F32) for r in refs[nt + npar:nt + npar + nc]]
        o_refs = refs[nt + npar + nc:]
        _, vjp = jax.vjp(f, *ins)
        grads = vjp(tuple(cts))
        for o_ref, idx in zip(o_refs[:len(diff_tiled)], diff_tiled):
            o_ref[...] = grads[idx].astype(o_ref.dtype)
        p_refs = o_refs[len(diff_tiled):]

        @pl.when(pl.program_id(0) == 0)
        def _():
            for p_ref in p_refs:
                p_ref[...] = jnp.zeros_like(p_ref)

        for p_ref, gp in zip(p_refs, grads[nt:]):
            p_ref[...] += gp

    outs = pl.pallas_call(
        body, name=name, grid=(t // tm,),
        in_specs=[pl.BlockSpec((tm, x.shape[1]), lambda i: (i, 0)) for x in tiled]
        + [pl.BlockSpec(p.shape, lambda i: (0, 0)) for p in params]
        + [pl.BlockSpec((tm, c.shape[1]), lambda i: (i, 0)) for c in cots],
        out_specs=[pl.BlockSpec((tm, tiled[idx].shape[1]), lambda i: (i, 0)) for idx in diff_tiled]
        + [pl.BlockSpec(p.shape, lambda i: (0, 0)) for p in params],
        out_shape=[jax.ShapeDtypeStruct(tiled[idx].shape, dt) for idx, dt in zip(diff_tiled, d_tiled_dtypes)]
        + [jax.ShapeDtypeStruct(p.shape, F32) for p in params],
        compiler_params=_params(),
    )(*tiled, *params, *cots)
    return outs[:len(diff_tiled)], outs[len(diff_tiled):]


def _gla_head(q, k, v, r, z, g, st):
    c = q.shape[0]
    causal = _iota2((c, c), 0) >= _iota2((c, c), 1)
    la = _log_sigmoid(z) * (1.0 / GLA_TAU)
    big_l = cumsum_rows(la)
    ep, en = jnp.exp(big_l), jnp.exp(-big_l)
    qs = q * (GLA_DK ** -0.5)
    qp = qs * ep
    s = jnp.where(causal, mm_nt(qp, k * en), mm_nt(qs * en, k * ep))
    o = mm_nn(s, v) + mm_nt(qp, st)
    l_end = jnp.sum(la, axis=0, keepdims=True)
    st_new = st * jnp.exp(l_end) + mm_tn(v, k * jnp.exp(l_end - big_l))
    u = _rms_norm(o, g) * (r * jax.nn.sigmoid(r))
    return u, st_new


def _gla_slices(h):
    q = slice(GLA_DK * h, GLA_DK * (h + 1))
    k = slice(GLA_HK + GLA_DK * h, GLA_HK + GLA_DK * (h + 1))
    v = slice(2 * GLA_HK + GLA_DV * h, 2 * GLA_HK + GLA_DV * (h + 1))
    r = slice(2 * GLA_HK + GLA_HV + GLA_DV * h, 2 * GLA_HK + GLA_HV + GLA_DV * (h + 1))
    return q, k, v, r


def _gla_fwd(proj, z, norm_g):
    t = proj.shape[0]
    nc = t // CHUNK

    def body(proj_ref, z_ref, g_ref, u_ref, st_save_ref, st_ref):
        @pl.when(pl.program_id(0) == 0)
        def _():
            st_ref[...] = jnp.zeros_like(st_ref)

        g = g_ref[...]
        for h in range(GLA_HEADS):
            sq, sk, sv, sr = _gla_slices(h)
            st = st_ref[h]
            st_save_ref[0, h] = st
            u, st_new = _gla_head(proj_ref[:, sq].astype(F32), proj_ref[:, sk].astype(F32), proj_ref[:, sv].astype(F32),
                                  proj_ref[:, sr].astype(F32), z_ref[:, GLA_DK * h:GLA_DK * (h + 1)], g, st)
            u_ref[:, GLA_DV * h:GLA_DV * (h + 1)] = u.astype(u_ref.dtype)
            st_ref[h] = st_new

    return pl.pallas_call(
        body, name="gla_fwd", grid=(nc,),
        in_specs=[pl.BlockSpec((CHUNK, GLA_MAIN), lambda i: (i, 0)), pl.BlockSpec((CHUNK, GLA_HK), lambda i: (i, 0)),
                  pl.BlockSpec((1, GLA_DV), lambda i: (0, 0))],
        out_specs=[pl.BlockSpec((CHUNK, GLA_HV), lambda i: (i, 0)),
                   pl.BlockSpec((1, GLA_HEADS, GLA_DV, GLA_DK), lambda i: (i, 0, 0, 0))],
        out_shape=[jax.ShapeDtypeStruct((t, GLA_HV), BF16), jax.ShapeDtypeStruct((nc, GLA_HEADS, GLA_DV, GLA_DK), F32)],
        scratch_shapes=[pltpu.VMEM((GLA_HEADS, GLA_DV, GLA_DK), F32)],
        compiler_params=_params(),
    )(proj, z, norm_g)


def _gla_bwd(proj, z, norm_g, states, du):
    t = proj.shape[0]
    nc = t // CHUNK

    def body(proj_ref, z_ref, g_ref, st_in_ref, du_ref, dproj_ref, dz_ref, dg_ref, dst_ref):
        @pl.when(pl.program_id(0) == 0)
        def _():
            dst_ref[...] = jnp.zeros_like(dst_ref)
            dg_ref[...] = jnp.zeros_like(dg_ref)

        g = g_ref[...]
        for h in range(GLA_HEADS):
            sq, sk, sv, sr = _gla_slices(h)
            ins = (proj_ref[:, sq].astype(F32), proj_ref[:, sk].astype(F32), proj_ref[:, sv].astype(F32),
                   proj_ref[:, sr].astype(F32), z_ref[:, GLA_DK * h:GLA_DK * (h + 1)], g, st_in_ref[0, h])
            _, vjp = jax.vjp(_gla_head, *ins)
            dq, dk, dv, dr, dz, dg, dst = vjp((du_ref[:, GLA_DV * h:GLA_DV * (h + 1)], dst_ref[h]))
            dproj_ref[:, sq] = dq.astype(dproj_ref.dtype)
            dproj_ref[:, sk] = dk.astype(dproj_ref.dtype)
            dproj_ref[:, sv] = dv.astype(dproj_ref.dtype)
            dproj_ref[:, sr] = dr.astype(dproj_ref.dtype)
            dz_ref[:, GLA_DK * h:GLA_DK * (h + 1)] = dz
            dg_ref[...] += dg
            dst_ref[h] = dst

    rev = lambda i: (nc - 1 - i, 0)
    return pl.pallas_call(
        body, name="gla_bwd", grid=(nc,),
        in_specs=[pl.BlockSpec((CHUNK, GLA_MAIN), rev), pl.BlockSpec((CHUNK, GLA_HK), rev),
                  pl.BlockSpec((1, GLA_DV), lambda i: (0, 0)),
                  pl.BlockSpec((1, GLA_HEADS, GLA_DV, GLA_DK), lambda i: (nc - 1 - i, 0, 0, 0)),
                  pl.BlockSpec((CHUNK, GLA_HV), rev)],
        out_specs=[pl.BlockSpec((CHUNK, GLA_MAIN), rev), pl.BlockSpec((CHUNK, GLA_HK), rev),
                   pl.BlockSpec((1, GLA_DV), lambda i: (0, 0))],
        out_shape=[jax.ShapeDtypeStruct((t, GLA_MAIN), BF16), jax.ShapeDtypeStruct((t, GLA_HK), F32),
                   jax.ShapeDtypeStruct((1, GLA_DV), F32)],
        scratch_shapes=[pltpu.VMEM((GLA_HEADS, GLA_DV, GLA_DK), F32)],
        compiler_params=_params(),
    )(proj, z, norm_g, states, du)


def _mla_pre(cq, cos, sin, gq, gkv, w_uq, w_ukv):
    qlat = _rms_norm(cq[:, :MLA_RANK], gq)
    kvlat = _rms_norm(cq[:, MLA_RANK:2 * MLA_RANK], gkv)
    kr = cq[:, 2 * MLA_RANK:]
    scale = (MLA_NOPE + MLA_ROPE) ** -0.5
    q = mm_nn(qlat, w_uq) * scale
    kv = mm_nn(kvlat, w_ukv)
    n_nope = MLA_HEADS * MLA_NOPE
    ropes = []
    for h in range(MLA_HEADS):
        qr = q[:, n_nope + LANES * h:n_nope + LANES * (h + 1)]
        ropes.append(qr * cos + rot_half(qr) * sin)
    return q[:, :n_nope], jnp.concatenate(ropes, axis=1), kv, kr * cos + rot_half(kr) * sin


MLA_Q_TILE = 256


def _mla_attn_block(qn, qr, kv, kr, q0):
    tq, nk = qn.shape[0], kv.shape[0]
    s = mm_nt(qn, kv[:, :MLA_NOPE]) + mm_nt(qr, kr)
    visible = (_iota2((tq, nk), 1) // CHUNK) <= ((q0 + _iota2((tq, nk), 0)) // CHUNK)
    s = jnp.where(visible, s, -1e30)
    e = jnp.exp(s - jnp.max(s, -1, keepdims=True))
    p = e / jnp.sum(e, -1, keepdims=True)
    return mm_nn(p, kv[:, MLA_NOPE:])


def _mla_attn_fwd(qn, qr, kv, kr):
    t = qn.shape[0]

    def body(qn_ref, qr_ref, kv_ref, kr_ref, o_ref):
        for i in range(t // MLA_Q_TILE):
            rows = slice(i * MLA_Q_TILE, (i + 1) * MLA_Q_TILE)
            keys = slice(0, (i + 1) * MLA_Q_TILE)
            o = _mla_attn_block(qn_ref[rows, :].astype(F32), qr_ref[rows, :].astype(F32), kv_ref[keys, :].astype(F32),
                                kr_ref[keys, :].astype(F32), i * MLA_Q_TILE)
            o_ref[rows, :] = o.astype(o_ref.dtype)

    return pl.pallas_call(
        body, name="mla_attn_fwd", grid=(MLA_HEADS,),
        in_specs=[pl.BlockSpec((t, MLA_NOPE), lambda h: (0, h)), pl.BlockSpec((t, LANES), lambda h: (0, h)),
                  pl.BlockSpec((t, MLA_NOPE + MLA_V), lambda h: (0, h)), pl.BlockSpec((t, LANES), lambda h: (0, 0))],
        out_specs=pl.BlockSpec((t, MLA_V), lambda h: (0, h)),
        out_shape=jax.ShapeDtypeStruct((t, MLA_HEADS * MLA_V), BF16),
        compiler_params=_params(),
    )(qn, qr, kv, kr)


def _mla_attn_bwd(qn, qr, kv, kr, do):
    t = qn.shape[0]

    def body(qn_ref, qr_ref, kv_ref, kr_ref, do_ref, dqn_ref, dqr_ref, dkv_ref, dkr_ref):
        dkv_ref[...] = jnp.zeros_like(dkv_ref)

        @pl.when(pl.program_id(0) == 0)
        def _():
            dkr_ref[...] = jnp.zeros_like(dkr_ref)

        for i in range(t // MLA_Q_TILE):
            rows = slice(i * MLA_Q_TILE, (i + 1) * MLA_Q_TILE)
            keys = slice(0, (i + 1) * MLA_Q_TILE)
            f = functools.partial(_mla_attn_block, q0=i * MLA_Q_TILE)
            _, vjp = jax.vjp(f, qn_ref[rows, :].astype(F32), qr_ref[rows, :].astype(F32), kv_ref[keys, :].astype(F32),
                             kr_ref[keys, :].astype(F32))
            dqn, dqr, dkv, dkr = vjp(do_ref[rows, :].astype(F32))
            dqn_ref[rows, :] = dqn
            dqr_ref[rows, :] = dqr
            dkv_ref[keys, :] += dkv
            dkr_ref[keys, :] += dkr

    return pl.pallas_call(
        body, name="mla_attn_bwd", grid=(MLA_HEADS,),
        in_specs=[pl.BlockSpec((t, MLA_NOPE), lambda h: (0, h)), pl.BlockSpec((t, LANES), lambda h: (0, h)),
                  pl.BlockSpec((t, MLA_NOPE + MLA_V), lambda h: (0, h)), pl.BlockSpec((t, LANES), lambda h: (0, 0)),
                  pl.BlockSpec((t, MLA_V), lambda h: (0, h))],
        out_specs=[pl.BlockSpec((t, MLA_NOPE), lambda h: (0, h)), pl.BlockSpec((t, LANES), lambda h: (0, h)),
                   pl.BlockSpec((t, MLA_NOPE + MLA_V), lambda h: (0, h)), pl.BlockSpec((t, LANES), lambda h: (0, 0))],
        out_shape=[jax.ShapeDtypeStruct(qn.shape, F32), jax.ShapeDtypeStruct(qr.shape, F32),
                   jax.ShapeDtypeStruct(kv.shape, F32), jax.ShapeDtypeStruct(kr.shape, F32)],
        compiler_params=_params(),
    )(qn, qr, kv, kr, do)


def _rope_tables(pos_col, inv_freq_row):
    t = pos_col.shape[0]

    def body(pos_ref, f_ref, cos_ref, sin_ref):
        ang = pos_ref[...].astype(F32) * f_ref[...]
        live = _iota2(ang.shape, 1) < MLA_ROPE
        cos_ref[...] = jnp.where(live, jnp.cos(ang), 0.0)
        sin_ref[...] = jnp.where(live, jnp.sin(ang), 0.0)

    return pl.pallas_call(
        body, name="rope_tables", out_shape=[jax.ShapeDtypeStruct((t, LANES), F32)] * 2, compiler_params=_params(),
    )(pos_col, inv_freq_row)


CONV_COL_TILE = 256


def _conv_gate(b, c, u, w0, w1, w2):
    cu = c * u
    return b * (w2 * cu + w1 * shift_rows(cu, 1) + w0 * shift_rows(cu, 2))


def _conv_specs(t):
    nb = D_MODEL // CONV_COL_TILE
    return [pl.BlockSpec((t, CONV_COL_TILE), lambda j, part=part: (0, part * nb + j)) for part in range(3)]


def _conv_fwd(bcu, w):
    t = bcu.shape[0]

    def body(b_ref, c_ref, u_ref, w_ref, o_ref):
        o_ref[...] = _conv_gate(b_ref[...], c_ref[...], u_ref[...], w_ref[0:1, :], w_ref[1:2, :],
                                w_ref[2:3, :]).astype(o_ref.dtype)

    return pl.pallas_call(
        body, name="conv_fwd", grid=(D_MODEL // CONV_COL_TILE,),
        in_specs=_conv_specs(t) + [pl.BlockSpec((3, CONV_COL_TILE), lambda j: (0, j))],
        out_specs=pl.BlockSpec((t, CONV_COL_TILE), lambda j: (0, j)),
        out_shape=jax.ShapeDtypeStruct((t, D_MODEL), BF16), compiler_params=_params(),
    )(bcu, bcu, bcu, w)


def _conv_bwd(bcu, w, dout):
    t = bcu.shape[0]

    def body(b_ref, c_ref, u_ref, w_ref, do_ref, db_ref, dc_ref, du_ref, dw_ref):
        _, vjp = jax.vjp(_conv_gate, b_ref[...], c_ref[...], u_ref[...], w_ref[0:1, :], w_ref[1:2, :], w_ref[2:3, :])
        db, dc, du, dw0, dw1, dw2 = vjp(do_ref[...])
        db_ref[...] = db.astype(db_ref.dtype)
        dc_ref[...] = dc.astype(dc_ref.dtype)
        du_ref[...] = du.astype(du_ref.dtype)
        dw_ref[0:1, :] = dw0
        dw_ref[1:2, :] = dw1
        dw_ref[2:3, :] = dw2

    col = pl.BlockSpec((t, CONV_COL_TILE), lambda j: (0, j))
    return pl.pallas_call(
        body, name="conv_bwd", grid=(D_MODEL // CONV_COL_TILE,),
        in_specs=_conv_specs(t) + [pl.BlockSpec((3, CONV_COL_TILE), lambda j: (0, j)), col],
        out_specs=[col, col, col, pl.BlockSpec((3, CONV_COL_TILE), lambda j: (0, j))],
        out_shape=[jax.ShapeDtypeStruct((t, D_MODEL), BF16)] * 3 + [jax.ShapeDtypeStruct((3, D_MODEL), F32)],
        compiler_params=_params(),
    )(bcu, bcu, bcu, w, dout)


def _loss_head(y, target):
    t, d = y.shape
    tm = 256

    def body(y_ref, t_ref, loss_ref, dy_ref):
        @pl.when(pl.program_id(0) == 0)
        def _():
            loss_ref[...] = jnp.zeros_like(loss_ref)

        err = y_ref[...] - t_ref[...]
        dy_ref[...] = err * (1.0 / d)
        loss_ref[...] += 0.5 * jnp.sum(jnp.sum(err * err, axis=-1, keepdims=True) * (1.0 / d))

    tile = pl.BlockSpec((tm, d), lambda i: (i, 0))
    return pl.pallas_call(
        body, name="loss_head", grid=(t // tm,), in_specs=[tile, tile],
        out_specs=[pl.BlockSpec((8, LANES), lambda i: (0, 0)), tile],
        out_shape=[jax.ShapeDtypeStruct((8, LANES), F32), jax.ShapeDtypeStruct((t, d), F32)],
        compiler_params=_params(),
    )(y, target)


def _ln_epi(acc, res, g, b):
    a = ALPHA * res + acc
    return a, _layer_norm(a, g, b)


def _ln_fn(a, g, b):
    return (_layer_norm(a, g, b),)


def _relu_sq(h):
    r = jnp.maximum(h.astype(F32), 0.0)
    return r * r


def _pad_cols(w, n):
    return jnp.pad(w, ((0, 0), (0, n - w.shape[1])))


def _pad_rows(w, n):
    return jnp.pad(w, ((0, n - w.shape[0]), (0, 0)))


def _uq_to_kernel_layout(w_uq):
    w = w_uq.reshape(MLA_RANK, MLA_HEADS, MLA_NOPE + MLA_ROPE)
    nope = w[:, :, :MLA_NOPE].reshape(MLA_RANK, MLA_HEADS * MLA_NOPE)
    rope = jnp.pad(w[:, :, MLA_NOPE:], ((0, 0), (0, 0), (0, LANES - MLA_ROPE))).reshape(MLA_RANK, MLA_HEADS * LANES)
    return jnp.concatenate([nope, rope], axis=1)


def _uq_from_kernel_layout(w):
    nope = w[:, :MLA_HEADS * MLA_NOPE].reshape(MLA_RANK, MLA_HEADS, MLA_NOPE)
    rope = w[:, MLA_HEADS * MLA_NOPE:].reshape(MLA_RANK, MLA_HEADS, LANES)[:, :, :MLA_ROPE]
    return jnp.concatenate([nope, rope], axis=2).reshape(MLA_RANK, MLA_HEADS * (MLA_NOPE + MLA_ROPE))


def _step(x, p, positions, target, small, comm):
    t = x.shape[0]
    w = small
    freqs = ROPE_BASE ** (-jnp.arange(0, MLA_ROPE // 2, dtype=F32) * (2.0 / MLA_ROPE))
    freq_row = jnp.concatenate([freqs, freqs, jnp.zeros((LANES - MLA_ROPE,), F32)])[None, :]
    cos, sin = _rope_tables(positions.reshape(t, 1), freq_row)

    saved = []
    for i in range(DEPTH):
        j, kind = i // 3, i % 3
        wl = comm.gathered(i)
        wa, wb = wl["a"], wl["b"]
        s = {"x": x, "wl": wl}
        tok = comm.fwd_point(i, 0, x)
        if kind == 0:
            w_in = wl["gla_w_in"]
            s["w_main"] = w_in[:, :GLA_MAIN]
            s["w_lr"] = _pad_cols(w_in[:, GLA_MAIN:], LANES)
            s["w_up"] = _pad_rows(w["gla_w_gate_up"][j], LANES).astype(BF16)
            s["proj"] = _matmul(x, s["w_main"], name="gla_proj", tn=1024, after=tok)
            s["glr"] = _matmul(x, s["w_lr"], name="gla_lr", out_dtypes=(F32,))
            s["z"] = _matmul(s["glr"], s["w_up"], name="gla_gate", epi=lambda acc, b: (acc + b,),
                             epi_ins=(w["gla_b_gate"][j][None, :],), out_dtypes=(F32,))
            s["u"], s["states"] = _gla_fwd(s["proj"], s["z"], w["gla_norm_g"][j][None, :])
        elif kind == 1:
            s["w_in"] = _pad_cols(wl["mla_w_in"], MLA_IN_PAD)
            s["w_uq"] = _uq_to_kernel_layout(wl["mla_w_uq"])
            s["cq"] = _matmul(x, s["w_in"], name="mla_proj", tn=MLA_IN_PAD, out_dtypes=(F32,), after=tok)
            s["pre_params"] = (w["mla_q_norm"][j][None, :], w["mla_kv_norm"][j][None, :], s["w_uq"], wl["mla_w_ukv"])
            s["qn"], s["qr"], s["kv"], s["kr"] = _tile_fwd(_mla_pre, (s["cq"], cos, sin), s["pre_params"],
                                                           (BF16, BF16, BF16, BF16), tm=256, name="mla_pre_fwd")
            s["u"] = _mla_attn_fwd(s["qn"], s["qr"], s["kv"], s["kr"])
        else:
            s["bcu"] = _matmul(x, wl["conv"], name="conv_proj", tb=True, tm=256, tn=3 * D_MODEL, b_at=REG_CONV,
                               out_dtypes=(F32,), after=tok)
            s["u"] = _conv_fwd(s["bcu"], w["conv_w"][j])
        g0, b0 = w["ln_g"][i, 0][None, :], w["ln_b"][i, 0][None, :]
        g1, b1 = w["ln_g"][i, 1][None, :], w["ln_b"][i, 1][None, :]
        s["a1"], s["x1"] = _matmul(s["u"], wa, name="mixer_out_ln", tm=256, tn=D_MODEL, b_at=REG_WOUT, epi=_ln_epi,
                                   epi_ins=(x, g0, b0), out_dtypes=(F32, F32))
        s["hh"] = _matmul(s["x1"], wa, name="mlp_up", tb=True, tm=256, tn=D_FF, b_at=REG_W1T)
        tok = comm.fwd_point(i, 1, s["hh"])
        s["a2"], s["x2"] = _matmul(s["hh"], wa, name="mlp_down_ln", tm=256, tn=D_MODEL, b_at=REG_W2, a_fn=_relu_sq,
                                   epi=_ln_epi, epi_ins=(s["x1"], g1, b1), out_dtypes=(F32, F32), after=tok)
        s["pp"] = _matmul(p[i], wb, name="ple_proj", tb=True, tn=D_MODEL, b_at=REG_WPT)
        tok = comm.fwd_point(i, 2, s["pp"])
        x, s["gt"] = _matmul(s["x2"], wa, name="ple_gate", tn=1024, b_at=REG_WG,
                             epi=lambda acc, xr, pp: (xr + jax.nn.sigmoid(acc) * pp.astype(F32), acc),
                             epi_ins=(s["x2"], s["pp"]), out_dtypes=(F32, BF16), after=tok)
        saved.append(s)

    loss_part, dx = _loss_head(x, target)

    gw = {n: [None] * WEIGHTS[n][0][0] for n in SMALL + REPLICATED}
    ln_g_grads, ln_b_grads = [[None, None] for _ in range(DEPTH)], [[None, None] for _ in range(DEPTH)]
    resid = lambda acc, r: (acc + ALPHA * r,)
    plus = lambda acc, r: (acc + r,)
    for i in reversed(range(DEPTH)):
        j, kind = i // 3, i % 3
        s = saved[i]
        wa = s["wl"]["a"]
        ga = lax.empty((N_DEV, A_ROWS, D_MODEL), BF16)
        gb = lax.empty((N_DEV, REG_WPT[1], PLE_DIM), BF16)
        layer_grads = {}
        tok = comm.bwd_point(i, 0, dx)

        def ple_bwd(dxo, gt, pp):
            sg = jax.nn.sigmoid(gt)
            return dxo * sg, dxo * pp * sg * (1.0 - sg)

        d_pp, d_gt = _tile_fwd(ple_bwd, (dx, s["gt"], s["pp"]), (), (BF16, BF16), tm=256, name="ple_bwd")
        gb = _matmul(d_pp, p[i], name="ple_proj_dw", ta=True, tm=REG_WPT[1], tn=PLE_DIM, out_at=REG_WPT, out_buf=gb, after=tok)
        ga = _matmul(s["x2"], d_gt, name="ple_gate_dw", ta=True, tm=REG_WG[1], tn=1024, out_at=REG_WG, out_buf=ga)
        dx2 = _matmul(d_gt, wa, name="ple_gate_dx", tb=True, tn=1024, b_at=REG_WG, epi=plus, epi_ins=(dx,),
                      out_dtypes=(F32,), after=[ga, gb])
        g1, b1 = w["ln_g"][i, 1][None, :], w["ln_b"][i, 1][None, :]
        (d_a2,), (ln_g_grads[i][1], ln_b_grads[i][1]) = _tile_bwd(_ln_fn, (s["a2"],), (g1, b1), (dx2,), (F32,), tm=256,
                                                                    name="ln_bwd")
        tok = comm.bwd_point(i, 1, d_a2)
        ga = _matmul(s["hh"], d_a2, name="mlp_down_dw", ta=True, tm=REG_W2[1], tn=1024, a_fn=_relu_sq, out_at=REG_W2,
                     out_buf=ga, after=tok)
        d_hh = _matmul(d_a2, wa, name="mlp_down_dx", tb=True, tm=256, tn=D_FF, b_at=REG_W2, after=ga,
                       epi=lambda acc, hh: (acc * 2.0 * jnp.maximum(hh.astype(F32), 0.0),), epi_ins=(s["hh"],))
        ga = _matmul(d_hh, s["x1"], name="mlp_up_dw", ta=True, tm=REG_W1T[1], tn=1024, out_at=REG_W1T, out_buf=ga)
        dx1 = _matmul(d_hh, wa, name="mlp_up_dx", tm=256, tn=1024, b_at=REG_W1T, epi=resid, epi_ins=(d_a2,),
                      out_dtypes=(F32,), after=ga)
        g0, b0 = w["ln_g"][i, 0][None, :], w["ln_b"][i, 0][None, :]
        (d_a1,), (ln_g_grads[i][0], ln_b_grads[i][0]) = _tile_bwd(_ln_fn, (s["a1"],), (g0, b0), (dx1,), (F32,), tm=256,
                                                                    name="ln_bwd")
        ga = _matmul(s["u"], d_a1, name="mixer_out_dw", ta=True, tm=REG_WOUT[1], tn=1024, out_at=REG_WOUT, out_buf=ga)
        du = _matmul(d_a1, wa, name="mixer_out_dx", tb=True, tn=1024, b_at=REG_WOUT, out_dtypes=(F32,), after=ga)
        if kind == 0:
            dproj, dz, dg = _gla_bwd(s["proj"], s["z"], w["gla_norm_g"][j][None, :], s["states"], du)
            gw["gla_norm_g"][j] = dg[0]
            gw["gla_b_gate"][j] = _tile_bwd(lambda zz, b: (zz + b,), (s["z"],), (w["gla_b_gate"][j][None, :],), (dz,), (),
                                            tm=256, name="gla_bias_bwd", diff_tiled=[])[1][0][0]
            gw["gla_w_gate_up"][j] = _matmul(s["glr"], dz, name="gla_gate_dw", ta=True, out_dtypes=(F32,))[:GLA_RANK]
            dglr = _matmul(dz, s["w_up"], name="gla_gate_dx", tb=True, out_dtypes=(F32,))
            dw_main = _matmul(s["x"], dproj, name="gla_proj_dw", ta=True, tn=1024, out_dtypes=(F32,))
            dw_lr = _matmul(s["x"], dglr, name="gla_lr_dw", ta=True, out_dtypes=(F32,))[:, :GLA_RANK]
            layer_grads["gla_w_in"] = jnp.concatenate([dw_main, dw_lr], axis=1)
            dx = _matmul(dproj, s["w_main"], name="gla_proj_dx", tb=True, tn=1024, epi=resid, epi_ins=(d_a1,),
                         out_dtypes=(F32,), after=[dw_main, dw_lr, gw["gla_w_gate_up"][j]])
            dx = _matmul(dglr, s["w_lr"], name="gla_lr_dx", tb=True, tn=1024, epi=plus, epi_ins=(dx,), out_dtypes=(F32,))
        elif kind == 1:
            dqn, dqr, dkv, dkr = _mla_attn_bwd(s["qn"], s["qr"], s["kv"], s["kr"], du)
            (d_cq,), (dgq, dgkv, dw_uq, dw_ukv) = _tile_bwd(_mla_pre, (s["cq"], cos, sin), s["pre_params"],
                                                           (dqn, dqr, dkv, dkr), (BF16,), tm=256, name="mla_pre_bwd",
                                                           diff_tiled=[0])
            gw["mla_q_norm"][j], gw["mla_kv_norm"][j] = dgq[0], dgkv[0]
            layer_grads["mla_w_uq"] = _uq_from_kernel_layout(dw_uq)
            layer_grads["mla_w_ukv"] = dw_ukv
            layer_grads["mla_w_in"] = _matmul(s["x"], d_cq, name="mla_proj_dw", ta=True, tn=MLA_IN_PAD,
                                              out_dtypes=(F32,))[:, :MLA_IN]
            dx = _matmul(d_cq, s["w_in"], name="mla_proj_dx", tb=True, tn=1024, epi=resid, epi_ins=(d_a1,),
                         out_dtypes=(F32,), after=layer_grads["mla_w_in"])
        else:
            db, dc, du_, dcw = _conv_bwd(s["bcu"], w["conv_w"][j], du)
            gw["conv_w"][j] = dcw
            dbcu = jnp.concatenate([db, dc, du_], axis=1)
            layer_grads["conv"] = _matmul(dbcu, s["x"], name="conv_proj_dw", ta=True, tm=REG_CONV[1], tn=1024,
                                          out_at=REG_CONV, out_buf=lax.empty((N_DEV, REG_CONV[1], D_MODEL), BF16))
            dx = _matmul(dbcu, s["wl"]["conv"], name="conv_proj_dx", tn=1024, b_at=REG_CONV, epi=resid, epi_ins=(d_a1,),
                         out_dtypes=(F32,), after=layer_grads["conv"])
        layer_grads["a"], layer_grads["b"] = ga, gb
        comm.grads_ready(i, layer_grads)
        comm.bwd_point(i, 2, dx)

    gw["ln_g"] = [jnp.concatenate([a, b], axis=0) for a, b in ln_g_grads]
    gw["ln_b"] = [jnp.concatenate([a, b], axis=0) for a, b in ln_b_grads]
    return loss_part, dx, {n: jnp.stack(gw[n]).astype(F32) for n in gw}


MESH_IDS = pl.DeviceIdType.MESH
ANY = pl.BlockSpec(memory_space=pl.ANY)
HBM_SPEC = pl.BlockSpec(memory_space=pltpu.HBM)
SEM_SPEC = pl.BlockSpec(memory_space=pltpu.SEMAPHORE)
DATAFLOW_EFFECT = pltpu.SideEffectType.DATAFLOW_SIDE_EFFECTING
CORE_COPIES, CHIP_COPIES = 4, 3


def _my_place():
    return lax.axis_index("x"), lax.axis_index("y"), lax.axis_index("c")


def _other_chips(mx, my):
    return [(1 - mx, my), (mx, 1 - my), (1 - mx, 1 - my)]


def _remote(src, dst, send_sems, recv_sems, k, to):
    return pltpu.make_async_remote_copy(src_ref=src, dst_ref=dst, send_sem=send_sems.at[k], recv_sem=recv_sems.at[k],
                                        device_id=to, device_id_type=MESH_IDS)


def _gather_first_copies(n_arr):
    def make(bufs, send_sems, recv_sems):
        mx, my, mc = _my_place()
        mine = 4 * mx + 2 * my + mc
        peers = [(mx, my, 1 - mc)] + [(cx, cy, mc) for cx, cy in _other_chips(mx, my)]
        return [_remote(bufs[a].at[mine], bufs[a].at[mine], send_sems, recv_sems, (1 + CHIP_COPIES) * a + k, to)
                for a in range(n_arr) for k, to in enumerate(peers)]
    return make, (1 + CHIP_COPIES) * n_arr


def _gather_forward_copies(n_arr):
    def make(bufs, send_sems, recv_sems):
        mx, my, mc = _my_place()
        blocks = [4 * cx + 2 * cy + mc for cx, cy in _other_chips(mx, my)]
        return [_remote(bufs[a].at[blk], bufs[a].at[blk], send_sems, recv_sems, CHIP_COPIES * a + k, (mx, my, 1 - mc))
                for a in range(n_arr) for k, blk in enumerate(blocks)]
    return make, CHIP_COPIES * n_arr


def _scatter_core_copies(n_arr):
    def make(bufs, send_sems, recv_sems):
        mx, my, mc = _my_place()
        return [_remote(bufs[a].at[2 * k + (1 - mc)], bufs[n_arr + a].at[k], send_sems, recv_sems, CORE_COPIES * a + k,
                        (mx, my, 1 - mc)) for a in range(n_arr) for k in range(CORE_COPIES)]
    return make, CORE_COPIES * n_arr


def _scatter_chip_copies(n_arr):
    def make(bufs, send_sems, recv_sems):
        mx, my, mc = _my_place()
        return [_remote(bufs[a].at[2 * cx + cy], bufs[n_arr + a].at[k], send_sems, recv_sems, CHIP_COPIES * a + k,
                        (cx, cy, mc)) for a in range(n_arr) for k, (cx, cy) in enumerate(_other_chips(mx, my))]
    return make, CHIP_COPIES * n_arr


def _exchange(name, bufs, copies):
    make, n_copies = copies
    n = len(bufs)

    def body(*refs):
        descs = make(refs[:n], refs[2 * n], refs[2 * n + 1])
        for cp in descs:
            cp.start()
        for cp in descs:
            cp.wait()

    return pl.pallas_call(
        body, name=name, out_shape=[jax.ShapeDtypeStruct(b.shape, b.dtype) for b in bufs], in_specs=[ANY] * n,
        out_specs=[ANY] * n, input_output_aliases={i: i for i in range(n)},
        scratch_shapes=[pltpu.SemaphoreType.DMA((n_copies,)), pltpu.SemaphoreType.DMA((n_copies,))],
    )(*bufs)


def _exchange_start(name, bufs, copies, after):
    make, n_copies = copies
    n = len(bufs)

    def body(*refs):
        for cp in make(refs[:n], refs[n + 1], refs[n + 2]):
            cp.start()
        refs[-1][...] = jnp.zeros_like(refs[-1])

    outs = pl.pallas_call(
        body, name=name,
        out_shape=(pltpu.SemaphoreType.DMA((n_copies,)), pltpu.SemaphoreType.DMA((n_copies,)),
                   *[pltpu.HBM(b.shape, b.dtype) for b in bufs], jax.ShapeDtypeStruct((8, LANES), F32)),
        in_specs=[HBM_SPEC] * n + [ANY],
        out_specs=(SEM_SPEC, SEM_SPEC, *[HBM_SPEC] * n, pl.BlockSpec(memory_space=pltpu.VMEM)),
        input_output_aliases={i: 2 + i for i in range(n)},
        compiler_params=pltpu.CompilerParams(has_side_effects=DATAFLOW_EFFECT),
    )(*[pltpu.with_memory_space_constraint(b, pltpu.HBM) for b in bufs], after)
    return (outs[0], outs[1]), list(outs[2:2 + n]), outs[-1]


def _exchange_wait(name, sems, bufs, copies, after):
    make, _ = copies
    n = len(bufs)

    def body(*refs):
        for cp in make(refs[:n], refs[n], refs[n + 1]):
            cp.wait_send()
            cp.wait_recv()

    return list(pl.pallas_call(
        body, name=name, out_shape=[pltpu.HBM(b.shape, b.dtype) for b in bufs],
        in_specs=[HBM_SPEC] * n + [SEM_SPEC, SEM_SPEC, ANY], out_specs=[HBM_SPEC] * n,
        input_output_aliases={i: i for i in range(n)},
        compiler_params=pltpu.CompilerParams(has_side_effects=DATAFLOW_EFFECT),
    )(*bufs, *sems, after))


def _row_tile(r):
    for cand in (PACK_ROW_TILE, 128, 64, 32, 16, 8):
        if r % cand == 0:
            return cand
    return r


def _pair_sum(g, recv, my_c):
    _, r, c = g.shape
    tr = _row_tile(r)

    def body(c_ref, g_ref, r_ref, o_ref):
        o_ref[...] = (g_ref[...].astype(F32) + r_ref[...].astype(F32)).astype(o_ref.dtype)

    return pl.pallas_call(
        body, name="rs_pair_sum", out_shape=jax.ShapeDtypeStruct((4, r, c), g.dtype),
        grid_spec=pltpu.PrefetchScalarGridSpec(
            num_scalar_prefetch=1, grid=(4, r // tr),
            in_specs=[pl.BlockSpec((1, tr, c), lambda n, i, cr: (2 * n + cr[0], i, 0)),
                      pl.BlockSpec((1, tr, c), lambda n, i, cr: (n, i, 0))],
            out_specs=pl.BlockSpec((1, tr, c), lambda n, i, cr: (n, i, 0))),
        compiler_params=_params(),
    )(my_c, g, recv)


def _chip_sum(h, recv, my_chip):
    _, r, c = h.shape
    tr = _row_tile(r)

    def body(j_ref, h_ref, r0_ref, r1_ref, r2_ref, o_ref):
        o_ref[...] = ((h_ref[0].astype(F32) + r0_ref[0].astype(F32)) + r1_ref[0].astype(F32)) + r2_ref[0].astype(F32)

    return pl.pallas_call(
        body, name="rs_chip_sum", out_shape=jax.ShapeDtypeStruct((r, c), F32),
        grid_spec=pltpu.PrefetchScalarGridSpec(
            num_scalar_prefetch=1, grid=(r // tr,),
            in_specs=[pl.BlockSpec((1, tr, c), lambda i, jr: (jr[0], i, 0))]
            + [pl.BlockSpec((1, tr, c), lambda i, jr, n=n: (n, i, 0)) for n in range(3)],
            out_specs=pl.BlockSpec((tr, c), lambda i, jr: (i, 0))),
        compiler_params=_params(),
    )(my_chip, h, recv, recv, recv)


def _sum_blocks(g):
    n, r, c = g.shape

    def body(g_ref, o_ref):
        acc = g_ref[0]
        for k in range(1, n):
            acc = acc + g_ref[k]
        o_ref[...] = acc

    return pl.pallas_call(body, name="sum_blocks", out_shape=jax.ShapeDtypeStruct((r, c), F32), compiler_params=_params())(g)


def _pack(flat_parts, cols, row_multiple, dtype):
    flat = jnp.concatenate([f.astype(dtype) for f in flat_parts])
    per_row_block = cols * row_multiple
    padded = -(-flat.shape[0] // per_row_block) * per_row_block
    return jnp.pad(flat, (0, padded - flat.shape[0])).reshape(padded // cols, cols)


def _shard_shape(name):
    shape, axis = WEIGHTS[name]
    if axis is None:
        return shape
    return tuple(s // N_DEV if a == axis else s for a, s in enumerate(shape))


def _size(shape):
    n = 1
    for s in shape:
        n *= s
    return n


def _unshard(blocks, name):
    _, axis = WEIGHTS[name]
    return jnp.concatenate([blocks[k] for k in range(N_DEV)], axis=axis)


def _unpack_blocks(flat, names, lead):
    out, off = {}, 0
    for n in names:
        shp = _shard_shape(n)[1:] if lead else _shard_shape(n)
        out[n] = flat[..., off:off + _size(shp)].reshape(flat.shape[:-1] + shp)
        off += _size(shp)
    return out


def _layer_slabs(shard, i):
    j, kind = i // 3, i % 3
    w_out = (shard["gla_w_out"], shard["mla_w_out"], shard["conv_w_out"])[kind][j]
    out = {"a": jnp.concatenate([shard["mlp_w2"][i], shard["mlp_w1"][i].T, w_out, shard["ple_w_gate"][i]], axis=0).astype(BF16),
           "b": shard["ple_w_proj"][i].T.astype(BF16)}
    if kind == 0:
        out["gla"] = shard["gla_w_in"][j].astype(BF16)
    elif kind == 1:
        out["mla"] = _pack([shard[n][j].reshape(-1) for n in MLA_PACKED], PACK_COLS, PACK_ROW_TILE, BF16)
    else:
        out["conv"] = shard["conv_w_in"][j].T.astype(BF16)
    return out


def _layer_weights(landed, i):
    kind = i % 3
    out = {"a": landed["a"], "b": landed["b"]}
    if kind == 0:
        out["gla_w_in"] = jnp.concatenate([landed["gla"][k] for k in range(N_DEV)], axis=1)
    elif kind == 1:
        blocks = _unpack_blocks(landed["mla"].reshape(N_DEV, -1), MLA_PACKED, lead=True)
        for n in MLA_PACKED:
            out[n] = jnp.concatenate([blocks[n][k] for k in range(N_DEV)], axis=WEIGHTS[n][1] - 1)
    else:
        out["conv"] = landed["conv"]
    return out


def _layer_grad_buffers(layer_grads, i):
    kind = i % 3
    out = {"a": layer_grads["a"], "b": layer_grads["b"]}
    if kind == 0:
        out["gla"] = jnp.stack(jnp.split(layer_grads["gla_w_in"], N_DEV, axis=1)).astype(BF16)
    elif kind == 1:
        parts = [jnp.stack(jnp.split(layer_grads[n], N_DEV, axis=WEIGHTS[n][1] - 1)).reshape(N_DEV, -1) for n in MLA_PACKED]
        cat = jnp.concatenate(parts, axis=1).astype(BF16)
        per = PACK_COLS * PACK_ROW_TILE
        padded = -(-cat.shape[1] // per) * per
        out["mla"] = jnp.pad(cat, ((0, 0), (0, padded - cat.shape[1]))).reshape(N_DEV, padded // PACK_COLS, PACK_COLS)
    else:
        out["conv"] = layer_grads["conv"]
    return out


class _Overlap:
    def __init__(self, shard):
        mx, my, mc = _my_place()
        self.my_c = mc.astype(jnp.int32).reshape(1)
        self.my_chip = (2 * mx + my).astype(jnp.int32).reshape(1)
        mine = 4 * mx + 2 * my + mc
        self.keys, self.landing = [], []
        for i in range(DEPTH):
            slabs = _layer_slabs(shard, i)
            self.keys.append(list(slabs))
            self.landing.append([lax.dynamic_update_index_in_dim(lax.empty((N_DEV, *v.shape), v.dtype), v, mine, 0)
                                 for v in slabs.values()])
        self.landed = [None] * DEPTH
        self.pending = None
        self.grad_bufs = [None] * DEPTH
        self.reduced = [None] * DEPTH
        bufs = _exchange("ag_first_l0", self.landing[0], _gather_first_copies(len(self.landing[0])))
        self.landed[0] = _exchange("ag_forward_l0", bufs, _gather_forward_copies(len(bufs)))

    def gathered(self, i):
        return _layer_weights(dict(zip(self.keys[i], self.landed[i])), i)

    def fwd_point(self, i, k, dep):
        nxt = i + 1
        if nxt >= DEPTH:
            return None
        n = len(self.landing[nxt])
        if k == 0:
            self.pending = _exchange_start(f"ag_first_start_l{nxt}", self.landing[nxt], _gather_first_copies(n), dep)
            return self.pending[2]
        if k == 1:
            sems, bufs, _ = self.pending
            bufs = _exchange_wait(f"ag_first_wait_l{nxt}", sems, bufs, _gather_first_copies(n), dep)
            self.pending = _exchange_start(f"ag_forward_start_l{nxt}", bufs, _gather_forward_copies(n), dep)
            return self.pending[2]
        sems, bufs, _ = self.pending
        self.landed[nxt] = _exchange_wait(f"ag_forward_wait_l{nxt}", sems, bufs, _gather_forward_copies(n), dep)
        return None

    def grads_ready(self, i, layer_grads):
        self.grad_bufs[i] = _layer_grad_buffers(layer_grads, i)

    def bwd_point(self, i, k, dep):
        src = i + 1
        if src >= DEPTH:
            return None
        gs = list(self.grad_bufs[src].values())
        n = len(gs)
        if k == 0:
            land = [lax.empty((4, *g.shape[1:]), g.dtype) for g in gs]
            self.pending = _exchange_start(f"rs_cores_start_l{src}", gs + land, _scatter_core_copies(n), dep)
            return self.pending[2]
        if k == 1:
            sems, bufs, _ = self.pending
            bufs = _exchange_wait(f"rs_cores_wait_l{src}", sems, bufs, _scatter_core_copies(n), dep)
            hs = [_pair_sum(g, r, self.my_c) for g, r in zip(bufs[:n], bufs[n:])]
            land = [lax.empty((3, *h.shape[1:]), h.dtype) for h in hs]
            self.pending = _exchange_start(f"rs_chips_start_l{src}", hs + land, _scatter_chip_copies(n), dep)
            return self.pending[2]
        sems, bufs, _ = self.pending
        bufs = _exchange_wait(f"rs_chips_wait_l{src}", sems, bufs, _scatter_chip_copies(n), dep)
        self.reduced[src] = dict(zip(self.grad_bufs[src], [_chip_sum(h, r, self.my_chip) for h, r in zip(bufs[:n], bufs[n:])]))
        return None

    def finish(self):
        gs = list(self.grad_bufs[0].values())
        n = len(gs)
        bufs = _exchange("rs_cores_l0", gs + [lax.empty((4, *g.shape[1:]), g.dtype) for g in gs], _scatter_core_copies(n))
        hs = [_pair_sum(g, r, self.my_c) for g, r in zip(bufs[:n], bufs[n:])]
        bufs = _exchange("rs_chips_l0", hs + [lax.empty((3, *h.shape[1:]), h.dtype) for h in hs], _scatter_chip_copies(n))
        self.reduced[0] = dict(zip(self.grad_bufs[0], [_chip_sum(h, r, self.my_chip) for h, r in zip(bufs[:n], bufs[n:])]))
        return self.reduced


def _all_gather_small(x, name):
    mx, my, mc = _my_place()
    land = lax.dynamic_update_index_in_dim(lax.empty((N_DEV, *x.shape), x.dtype), x, 4 * mx + 2 * my + mc, 0)
    (land,) = _exchange(name + "_first", [land], _gather_first_copies(1))
    (land,) = _exchange(name + "_forward", [land], _gather_forward_copies(1))
    return land


def _shard_grads(reduced):
    def rows(a, reg):
        return a[reg[0] * reg[1]:(reg[0] + 1) * reg[1]]

    a = [reduced[i]["a"] for i in range(DEPTH)]
    w_out = [rows(a[i], REG_WOUT) for i in range(DEPTH)]
    mla = _unpack_blocks(reduced[1]["mla"].reshape(-1), MLA_PACKED, lead=True)
    out = {
        "mlp_w2": jnp.stack([rows(a[i], REG_W2) for i in range(DEPTH)]),
        "mlp_w1": jnp.stack([rows(a[i], REG_W1T).T for i in range(DEPTH)]),
        "gla_w_out": jnp.stack([w_out[0], w_out[3]]), "mla_w_out": w_out[1][None], "conv_w_out": w_out[2][None],
        "ple_w_gate": jnp.stack([rows(a[i], REG_WG) for i in range(DEPTH)]),
        "ple_w_proj": jnp.stack([reduced[i]["b"].T for i in range(DEPTH)]),
        "conv_w_in": reduced[2]["conv"].T[None],
        "gla_w_in": jnp.stack([reduced[0]["gla"], reduced[3]["gla"]]),
    }
    out.update({n: mla[n][None] for n in MLA_PACKED})
    return out


def _adamw(w, g, m, v, name):
    shape = w.shape
    cols = shape[-1]
    rows = _size(shape) // cols
    tr = rows
    for cand in (512, 256, 128, 64, 32, 16, 8):
        if rows > cand and rows % cand == 0:
            tr = cand
            break

    def body(w_ref, g_ref, m_ref, v_ref, d_ref, mo_ref, vo_ref):
        gv = g_ref[...]
        m2 = ADAM_B1 * m_ref[...] + (1.0 - ADAM_B1) * gv
        v2 = ADAM_B2 * v_ref[...] + (1.0 - ADAM_B2) * (gv * gv)
        m_hat = m2 / (1.0 - ADAM_B1 ** ADAM_STEP)
        v_hat = v2 / (1.0 - ADAM_B2 ** ADAM_STEP)
        d_ref[...] = -ADAM_LR * (m_hat / (jnp.sqrt(v_hat) + ADAM_EPS) + ADAM_WD * w_ref[...])
        mo_ref[...] = m2
        vo_ref[...] = v2

    spec = pl.BlockSpec((tr, cols), lambda i: (i, 0))
    outs = pl.pallas_call(
        body, name="adamw_" + name, grid=(rows // tr,), in_specs=[spec] * 4, out_specs=[spec] * 3,
        out_shape=[jax.ShapeDtypeStruct((rows, cols), F32)] * 3, compiler_params=_params(),
    )(*[a.reshape(rows, cols) for a in (w, g, m, v)])
    return [o.reshape(shape) for o in outs]


def kernel(x, p, positions, gla_w_in, gla_w_gate_up, gla_b_gate, gla_norm_g, gla_w_out, mla_w_in, mla_q_norm, mla_kv_norm, mla_w_uq, mla_w_ukv, mla_w_out, conv_w_in, conv_w, conv_w_out, ln_g, ln_b, mlp_w1, mlp_w2, ple_w_gate, ple_w_proj, loss_target, m_gla_w_in, m_gla_w_gate_up, m_gla_b_gate, m_gla_norm_g, m_gla_w_out, m_mla_w_in, m_mla_q_norm, m_mla_kv_norm, m_mla_w_uq, m_mla_w_ukv, m_mla_w_out, m_conv_w_in, m_conv_w, m_conv_w_out, m_ln_g, m_ln_b, m_mlp_w1, m_mlp_w2, m_ple_w_gate, m_ple_w_proj, v_gla_w_in, v_gla_w_gate_up, v_gla_b_gate, v_gla_norm_g, v_gla_w_out, v_mla_w_in, v_mla_q_norm, v_mla_kv_norm, v_mla_w_uq, v_mla_w_ukv, v_mla_w_out, v_conv_w_in, v_conv_w, v_conv_w_out, v_ln_g, v_ln_b, v_mlp_w1, v_mlp_w2, v_ple_w_gate, v_ple_w_proj):
    args = locals()
    shard = {n: args[n] for n in WEIGHT_NAMES}
    mom = {n: args["m_" + n] for n in WEIGHT_NAMES}
    var = {n: args["v_" + n] for n in WEIGHT_NAMES}
    mx, my, mc = _my_place()

    small_all = _all_gather_small(_pack([shard[n].reshape(-1) for n in SMALL], LANES, 8, F32), "ag_small")
    small = {n: shard[n] for n in REPLICATED}
    small.update({n: _unshard(blk, n) for n, blk in _unpack_blocks(small_all.reshape(N_DEV, -1), SMALL, lead=False).items()})

    comm = _Overlap(shard)
    loss_part, grad_x, small_grads = _step(x[0], p[:, 0], positions[0], loss_target[0], small, comm)
    loss = lax.psum(loss_part[0, 0], MESH_AXES)
    my_grads = _shard_grads(comm.finish())

    small_parts = [small_grads[n].reshape(-1) for n in SMALL + REPLICATED]
    red_small = _sum_blocks(_all_gather_small(_pack(small_parts, LANES, 8, F32), "ag_small_grads")).reshape(-1)
    off = 0
    dev = 4 * mx + 2 * my + mc
    for n in SMALL + REPLICATED:
        shape, axis = WEIGHTS[n]
        full_g = red_small[off:off + _size(shape)].reshape(shape)
        off += _size(shape)
        if axis is None:
            my_grads[n] = full_g
        else:
            width = shape[axis] // N_DEV
            my_grads[n] = lax.dynamic_slice_in_dim(full_g, dev * width, width, axis=axis)

    deltas, new_m, new_v = {}, {}, {}
    for n in WEIGHT_NAMES:
        deltas[n], new_m[n], new_v[n] = _adamw(shard[n], my_grads[n], mom[n], var[n], n)
    return (loss, grad_x[None], *[my_grads[n] for n in WEIGHT_NAMES], *[deltas[n] for n in WEIGHT_NAMES],
            *[new_m[n] for n in WEIGHT_NAMES], *[new_v[n] for n in WEIGHT_NAMES])
```

```python
import functools

import jax
import jax.numpy as jnp
from jax import lax
from jax.experimental import pallas as pl
from jax.experimental.pallas import tpu as pltpu

F32, BF16 = jnp.float32, jnp.bfloat16
HIGHEST = lax.Precision.HIGHEST
MESH_AXES = ("x", "y", "c")
N_DEV = 8

D_MODEL = 1024
SEQ = 2048
DEPTH = 4
CHUNK = 64
ALPHA = (2 * DEPTH) ** 0.25
LN_EPS = 1e-5
RMS_EPS = 1e-6
PLE_DIM = 256
D_FF = 4 * D_MODEL
GLA_HEADS = 4
GLA_DK = 128
GLA_DV = 256
GLA_RANK = 16
GLA_TAU = 16.0
GLA_HK = GLA_HEADS * GLA_DK
GLA_HV = GLA_HEADS * GLA_DV
GLA_MAIN = 2 * GLA_HK + GLA_HV + D_MODEL
MLA_HEADS = 8
MLA_NOPE = 128
MLA_ROPE = 64
MLA_V = 128
MLA_RANK = 256
MLA_IN = 2 * MLA_RANK + MLA_ROPE
MLA_IN_PAD = 640
ROPE_BASE = 10000.0
LANES = 128
ADAM_LR, ADAM_B1, ADAM_B2, ADAM_EPS, ADAM_WD, ADAM_STEP = 0.001, 0.9, 0.999, 1e-08, 0.01, 10

V7X_VMEM_LIMIT_BYTES = 56 * 1024 * 1024
PACK_COLS = 1024
PACK_ROW_TILE = 256

WEIGHTS = {
    "gla_w_in": ((2, 1024, 3088), 2), "gla_w_gate_up": ((2, 16, 512), 2), "gla_b_gate": ((2, 512), 1),
    "gla_norm_g": ((2, 256), 1), "gla_w_out": ((2, 1024, 1024), 1), "mla_w_in": ((1, 1024, 576), 1),
    "mla_q_norm": ((1, 256), None), "mla_kv_norm": ((1, 256), None), "mla_w_uq": ((1, 256, 1536), 2),
    "mla_w_ukv": ((1, 256, 2048), 2), "mla_w_out": ((1, 1024, 1024), 1), "conv_w_in": ((1, 1024, 3072), 2),
    "conv_w": ((1, 3, 1024), 2), "conv_w_out": ((1, 1024, 1024), 1), "ln_g": ((4, 2, 1024), 2),
    "ln_b": ((4, 2, 1024), 2), "mlp_w1": ((4, 1024, 4096), 2), "mlp_w2": ((4, 4096, 1024), 1),
    "ple_w_gate": ((4, 1024, 1024), 1), "ple_w_proj": ((4, 256, 1024), 2),
}
WEIGHT_NAMES = list(WEIGHTS)
REG_W2, REG_W1T, REG_WOUT, REG_WG = (0, 512), (1, 512), (8, 128), (9, 128)
A_ROWS = 1280
REG_CONV = (0, 384)
REG_WPT = (0, 128)
MLA_PACKED = ["mla_w_in", "mla_w_uq", "mla_w_ukv"]
SMALL = ["gla_w_gate_up", "gla_b_gate", "gla_norm_g", "conv_w", "ln_g", "ln_b"]
REPLICATED = ["mla_q_norm", "mla_kv_norm"]


def _params(**kw):
    return pltpu.CompilerParams(vmem_limit_bytes=V7X_VMEM_LIMIT_BYTES, **kw)


def _dot(a, b, ca, cb, precision=None):
    return lax.dot_general(a, b, (((ca,), (cb,)), ((), ())), precision=precision, preferred_element_type=F32)


def _nn(a, b):
    return _dot(a.astype(BF16), b.astype(BF16), 1, 0)


def _nt(a, b):
    return _dot(a.astype(BF16), b.astype(BF16), 1, 1)


def _tn(a, b):
    return _dot(a.astype(BF16), b.astype(BF16), 0, 0)


@jax.custom_vjp
def mm_nn(a, b):
    return _nn(a, b)


def _mm_nn_fwd(a, b):
    return _nn(a, b), (a, b)


def _mm_nn_bwd(res, g):
    a, b = res
    return _nt(g, b).astype(a.dtype), _tn(a, g).astype(b.dtype)


mm_nn.defvjp(_mm_nn_fwd, _mm_nn_bwd)


@jax.custom_vjp
def mm_nt(a, b):
    return _nt(a, b)


def _mm_nt_fwd(a, b):
    return _nt(a, b), (a, b)


def _mm_nt_bwd(res, g):
    a, b = res
    return _nn(g, b).astype(a.dtype), _tn(g, a).astype(b.dtype)


mm_nt.defvjp(_mm_nt_fwd, _mm_nt_bwd)


@jax.custom_vjp
def mm_tn(a, b):
    return _tn(a, b)


def _mm_tn_fwd(a, b):
    return _tn(a, b), (a, b)


def _mm_tn_bwd(res, g):
    a, b = res
    return _nt(b, g).astype(a.dtype), _nn(a, g).astype(b.dtype)


mm_tn.defvjp(_mm_tn_fwd, _mm_tn_bwd)


def _iota2(shape, dim):
    return lax.broadcasted_iota(jnp.int32, shape, dim)


@jax.custom_vjp
def cumsum_rows(x):
    n = x.shape[0]
    tri = (_iota2((n, n), 0) >= _iota2((n, n), 1)).astype(F32)
    return _dot(tri, x, 1, 0, precision=HIGHEST)


def _cumsum_fwd(x):
    return cumsum_rows(x), None


def _cumsum_bwd(_, g):
    n = g.shape[0]
    tri_t = (_iota2((n, n), 0) <= _iota2((n, n), 1)).astype(F32)
    return (_dot(tri_t, g, 1, 0, precision=HIGHEST),)


cumsum_rows.defvjp(_cumsum_fwd, _cumsum_bwd)


def _rot_matrix(transposed):
    i, j = _iota2((LANES, LANES), 0), _iota2((LANES, LANES), 1)
    if transposed:
        i, j = j, i
    half = MLA_ROPE // 2
    plus = (i == j - half) & (j >= half) & (j < MLA_ROPE)
    minus = (i == j + half) & (j < half)
    return plus.astype(F32) - minus.astype(F32)


@jax.custom_vjp
def rot_half(x):
    return _dot(x, _rot_matrix(False), 1, 0, precision=HIGHEST)


def _rot_fwd(x):
    return rot_half(x), None


def _rot_bwd(_, g):
    return (_dot(g, _rot_matrix(True), 1, 0, precision=HIGHEST),)


rot_half.defvjp(_rot_fwd, _rot_bwd)


def _shift_rows_raw(x, s):
    n = x.shape[0]
    row = _iota2(x.shape, 0)
    rolled = pltpu.roll(x, s % n, 0)
    keep = (row >= s) if s > 0 else (row < n + s)
    return jnp.where(keep, rolled, 0.0)


@functools.partial(jax.custom_vjp, nondiff_argnums=(1,))
def shift_rows(x, s):
    return _shift_rows_raw(x, s)


def _shift_fwd(x, s):
    return _shift_rows_raw(x, s), None


def _shift_bwd(s, _, g):
    return (_shift_rows_raw(g, -s),)


shift_rows.defvjp(_shift_fwd, _shift_bwd)


def _layer_norm(a, g, b):
    mu = jnp.mean(a, -1, keepdims=True)
    xc = a - mu
    var = jnp.mean(xc * xc, -1, keepdims=True)
    return xc * lax.rsqrt(var + LN_EPS) * g + b


def _rms_norm(a, g):
    return a * lax.rsqrt(jnp.mean(a * a, -1, keepdims=True) + RMS_EPS) * g


def _log_sigmoid(z):
    return jnp.minimum(z, 0.0) - jnp.log(1.0 + jnp.exp(-jnp.abs(z)))


def _matmul(a, b, *, name, ta=False, tb=False, tm=512, tn=512, a_fn=None, epi=None, epi_ins=(), out_dtypes=(BF16,),
            b_at=None, out_at=None, out_buf=None, after=None):
    m = a.shape[1] if ta else a.shape[0]
    k = a.shape[0] if ta else a.shape[1]
    if b_at is None:
        n, kb = (b.shape[0], b.shape[1]) if tb else (b.shape[1], b.shape[0])
    else:
        rb, r = b_at
        n, kb = (N_DEV * r, b.shape[2]) if tb else (b.shape[2], N_DEV * r)
    assert kb == k, (name, a.shape, b.shape, k, kb)
    tm, tn = min(tm, m), min(tn, n)
    assert m % tm == 0 and n % tn == 0, (name, m, n, tm, tn)
    a_spec = pl.BlockSpec((k, tm), lambda i, j: (0, i)) if ta else pl.BlockSpec((tm, k), lambda i, j: (i, 0))
    if b_at is None:
        b_spec = pl.BlockSpec((tn, k), lambda i, j: (j, 0)) if tb else pl.BlockSpec((k, tn), lambda i, j: (0, j))
        load_b = lambda ref: ref[...]
    elif tb and tn == n:
        b_spec = pl.BlockSpec((N_DEV, r, k), lambda i, j: (0, rb, 0))
        load_b = lambda ref: ref[...].reshape(n, k)
    elif tb:
        assert tn == r, (name, tn, r)
        b_spec = pl.BlockSpec((1, r, k), lambda i, j: (j, rb, 0))
        load_b = lambda ref: ref[0]
    else:
        b_spec = pl.BlockSpec((N_DEV, r, tn), lambda i, j: (0, rb, j))
        load_b = lambda ref: ref[...].reshape(k, tn)
    e_specs = []
    for e in epi_ins:
        if e.shape == (1, n):
            e_specs.append(pl.BlockSpec((1, tn), lambda i, j: (0, j)))
        else:
            assert e.shape == (m, n), (name, e.shape, m, n)
            e_specs.append(pl.BlockSpec((tm, tn), lambda i, j: (i, j)))
    n_epi = len(epi_ins)
    ca, cb = (0 if ta else 1), (1 if tb else 0)
    operands = [a, b, *epi_ins]
    in_specs = [a_spec, b_spec, *e_specs]
    if out_at is None:
        out_specs = [pl.BlockSpec((tm, tn), lambda i, j: (i, j)) for _ in out_dtypes]
        out_shape = [jax.ShapeDtypeStruct((m, n), dt) for dt in out_dtypes]
        aliases, n_buf = {}, 0
    else:
        orb, orows = out_at
        assert len(out_dtypes) == 1 and orows % tm == 0 and m == N_DEV * orows and n == out_buf.shape[2], (name, m, n)
        per = orows // tm
        out_specs = [pl.BlockSpec((1, tm, tn), lambda i, j: (i // per, orb * per + i % per, j))]
        out_shape = [jax.ShapeDtypeStruct(out_buf.shape, out_buf.dtype)]
        operands.append(out_buf)
        in_specs.append(pl.BlockSpec(memory_space=pl.ANY))
        aliases, n_buf = {len(operands) - 1: 0}, 1
    for dep in ([] if after is None else after if isinstance(after, (list, tuple)) else [after]):
        if dep is not None:
            operands.append(dep)
            in_specs.append(pl.BlockSpec(memory_space=pl.ANY))
            n_buf += 1

    def body(a_ref, b_ref, *rest):
        av = a_ref[...]
        if a_fn is not None:
            av = a_fn(av)
        acc = _dot(av.astype(BF16), load_b(b_ref).astype(BF16), ca, cb)
        outs = epi(acc, *[r_[...] for r_ in rest[:n_epi]]) if epi is not None else (acc,)
        for o_ref, val in zip(rest[n_epi + n_buf:], outs):
            o_ref[...] = val.astype(o_ref.dtype).reshape(o_ref.shape)

    outs = pl.pallas_call(
        body, name=name, grid=(m // tm, n // tn), in_specs=in_specs, out_specs=out_specs, out_shape=out_shape,
        input_output_aliases=aliases, compiler_params=_params(),
    )(*operands)
    return outs[0] if len(outs) == 1 else tuple(outs)


def _tile_fwd(f, tiled, params, out_dtypes, *, tm, name):
    t = tiled[0].shape[0]
    assert t % tm == 0
    out_avals = jax.eval_shape(f, *[jax.ShapeDtypeStruct((tm, x.shape[1]), F32) for x in tiled],
                               *[jax.ShapeDtypeStruct(p.shape, F32) for p in params])
    nt, npar = len(tiled), len(params)

    def body(*refs):
        ins = [r[...].astype(F32) for r in refs[:nt + npar]]
        outs = f(*ins)
        for o_ref, val in zip(refs[nt + npar:], outs):
            o_ref[...] = val.astype(o_ref.dtype)

    return pl.pallas_call(
        body, name=name, grid=(t // tm,),
        in_specs=[pl.BlockSpec((tm, x.shape[1]), lambda i: (i, 0)) for x in tiled]
        + [pl.BlockSpec(p.shape, lambda i: (0, 0)) for p in params],
        out_specs=[pl.BlockSpec((tm, o.shape[1]), lambda i: (i, 0)) for o in out_avals],
        out_shape=[jax.ShapeDtypeStruct((t, o.shape[1]), dt) for o, dt in zip(out_avals, out_dtypes)],
        compiler_params=_params(),
    )(*tiled, *params)


def _tile_bwd(f, tiled, params, cots, d_tiled_dtypes, *, tm, name, diff_tiled=None):
    t = tiled[0].shape[0]
    assert t % tm == 0
    nt, npar, nc = len(tiled), len(params), len(cots)
    diff_tiled = list(range(nt)) if diff_tiled is None else diff_tiled

    def body(*refs):
        ins = [r[...].astype(F32) for r in refs[:nt + npar]]
        cts = [r[...].astype(F32) for r in refs[nt + npar:nt + npar + nc]]
        o_refs = refs[nt + npar + nc:]
        _, vjp = jax.vjp(f, *ins)
        grads = vjp(tuple(cts))
        for o_ref, idx in zip(o_refs[:len(diff_tiled)], diff_tiled):
            o_ref[...] = grads[idx].astype(o_ref.dtype)
        p_refs = o_refs[len(diff_tiled):]

        @pl.when(pl.program_id(0) == 0)
        def _():
            for p_ref in p_refs:
                p_ref[...] = jnp.zeros_like(p_ref)

        for p_ref, gp in zip(p_refs, grads[nt:]):
            p_ref[...] += gp

    outs = pl.pallas_call(
        body, name=name, grid=(t // tm,),
        in_specs=[pl.BlockSpec((tm, x.shape[1]), lambda i: (i, 0)) for x in tiled]
        + [pl.BlockSpec(p.shape, lambda i: (0, 0)) for p in params]
        + [pl.BlockSpec((tm, c.shape[1]), lambda i: (i, 0)) for c in cots],
        out_specs=[pl.BlockSpec((tm, tiled[idx].shape[1]), lambda i: (i, 0)) for idx in diff_tiled]
        + [pl.BlockSpec(p.shape, lambda i: (0, 0)) for p in params],
        out_shape=[jax.ShapeDtypeStruct(tiled[idx].shape, dt) for idx, dt in zip(diff_tiled, d_tiled_dtypes)]
        + [jax.ShapeDtypeStruct(p.shape, F32) for p in params],
        compiler_params=_params(),
    )(*tiled, *params, *cots)
    return outs[:len(diff_tiled)], outs[len(diff_tiled):]


def _gla_head(q, k, v, r, z, g, st):
    c = q.shape[0]
    causal = _iota2((c, c), 0) >= _iota2((c, c), 1)
    la = _log_sigmoid(z) * (1.0 / GLA_TAU)
    big_l = cumsum_rows(la)
    ep, en = jnp.exp(big_l), jnp.exp(-big_l)
    qs = q * (GLA_DK ** -0.5)
    qp = qs * ep
    s = jnp.where(causal, mm_nt(qp, k * en), mm_nt(qs * en, k * ep))
    o = mm_nn(s, v) + mm_nt(qp, st)
    l_end = jnp.sum(la, axis=0, keepdims=True)
    st_new = st * jnp.exp(l_end) + mm_tn(v, k * jnp.exp(l_end - big_l))
    u = _rms_norm(o, g) * (r * jax.nn.sigmoid(r))
    return u, st_new


def _gla_slices(h):
    q = slice(GLA_DK * h, GLA_DK * (h + 1))
    k = slice(GLA_HK + GLA_DK * h, GLA_HK + GLA_DK * (h + 1))
    v = slice(2 * GLA_HK + GLA_DV * h, 2 * GLA_HK + GLA_DV * (h + 1))
    r = slice(2 * GLA_HK + GLA_HV + GLA_DV * h, 2 * GLA_HK + GLA_HV + GLA_DV * (h + 1))
    return q, k, v, r


def _gla_fwd(proj, z, norm_g):
    t = proj.shape[0]
    nc = t // CHUNK

    def body(proj_ref, z_ref, g_ref, u_ref, st_save_ref, st_ref):
        @pl.when(pl.program_id(0) == 0)
        def _():
            st_ref[...] = jnp.zeros_like(st_ref)

        g = g_ref[...]
        for h in range(GLA_HEADS):
            sq, sk, sv, sr = _gla_slices(h)
            st = st_ref[h]
            st_save_ref[0, h] = st
            u, st_new = _gla_head(proj_ref[:, sq].astype(F32), proj_ref[:, sk].astype(F32), proj_ref[:, sv].astype(F32),
                                  proj_ref[:, sr].astype(F32), z_ref[:, GLA_DK * h:GLA_DK * (h + 1)], g, st)
            u_ref[:, GLA_DV * h:GLA_DV * (h + 1)] = u.astype(u_ref.dtype)
            st_ref[h] = st_new

    return pl.pallas_call(
        body, name="gla_fwd", grid=(nc,),
        in_specs=[pl.BlockSpec((CHUNK, GLA_MAIN), lambda i: (i, 0)), pl.BlockSpec((CHUNK, GLA_HK), lambda i: (i, 0)),
                  pl.BlockSpec((1, GLA_DV), lambda i: (0, 0))],
        out_specs=[pl.BlockSpec((CHUNK, GLA_HV), lambda i: (i, 0)),
                   pl.BlockSpec((1, GLA_HEADS, GLA_DV, GLA_DK), lambda i: (i, 0, 0, 0))],
        out_shape=[jax.ShapeDtypeStruct((t, GLA_HV), BF16), jax.ShapeDtypeStruct((nc, GLA_HEADS, GLA_DV, GLA_DK), F32)],
        scratch_shapes=[pltpu.VMEM((GLA_HEADS, GLA_DV, GLA_DK), F32)],
        compiler_params=_params(),
    )(proj, z, norm_g)


def _gla_bwd(proj, z, norm_g, states, du, after):
    t = proj.shape[0]
    nc = t // CHUNK
    after = [a for a in after if a is not None]

    def body(proj_ref, z_ref, g_ref, st_in_ref, du_ref, *rest):
        dproj_ref, dz_ref, dg_ref, dst_ref = rest[len(after):]

        @pl.when(pl.program_id(0) == 0)
        def _():
            dst_ref[...] = jnp.zeros_like(dst_ref)
            dg_ref[...] = jnp.zeros_like(dg_ref)

        g = g_ref[...]
        for h in range(GLA_HEADS):
            sq, sk, sv, sr = _gla_slices(h)
            ins = (proj_ref[:, sq].astype(F32), proj_ref[:, sk].astype(F32), proj_ref[:, sv].astype(F32),
                   proj_ref[:, sr].astype(F32), z_ref[:, GLA_DK * h:GLA_DK * (h + 1)], g, st_in_ref[0, h])
            _, vjp = jax.vjp(_gla_head, *ins)
            dq, dk, dv, dr, dz, dg, dst = vjp((du_ref[:, GLA_DV * h:GLA_DV * (h + 1)], dst_ref[h]))
            dproj_ref[:, sq] = dq.astype(dproj_ref.dtype)
            dproj_ref[:, sk] = dk.astype(dproj_ref.dtype)
            dproj_ref[:, sv] = dv.astype(dproj_ref.dtype)
            dproj_ref[:, sr] = dr.astype(dproj_ref.dtype)
            dz_ref[:, GLA_DK * h:GLA_DK * (h + 1)] = dz
            dg_ref[...] += dg
            dst_ref[h] = dst

    rev = lambda i: (nc - 1 - i, 0)
    return pl.pallas_call(
        body, name="gla_bwd", grid=(nc,),
        in_specs=[pl.BlockSpec((CHUNK, GLA_MAIN), rev), pl.BlockSpec((CHUNK, GLA_HK), rev),
                  pl.BlockSpec((1, GLA_DV), lambda i: (0, 0)),
                  pl.BlockSpec((1, GLA_HEADS, GLA_DV, GLA_DK), lambda i: (nc - 1 - i, 0, 0, 0)),
                  pl.BlockSpec((CHUNK, GLA_HV), rev)] + [pl.BlockSpec(memory_space=pl.ANY)] * len(after),
        out_specs=[pl.BlockSpec((CHUNK, GLA_MAIN), rev), pl.BlockSpec((CHUNK, GLA_HK), rev),
                   pl.BlockSpec((1, GLA_DV), lambda i: (0, 0))],
        out_shape=[jax.ShapeDtypeStruct((t, GLA_MAIN), BF16), jax.ShapeDtypeStruct((t, GLA_HK), F32),
                   jax.ShapeDtypeStruct((1, GLA_DV), F32)],
        scratch_shapes=[pltpu.VMEM((GLA_HEADS, GLA_DV, GLA_DK), F32)],
        compiler_params=_params(),
    )(proj, z, norm_g, states, du, *after)


def _mla_pre(cq, cos, sin, gq, gkv, w_uq, w_ukv):
    qlat = _rms_norm(cq[:, :MLA_RANK], gq)
    kvlat = _rms_norm(cq[:, MLA_RANK:2 * MLA_RANK], gkv)
    kr = cq[:, 2 * MLA_RANK:]
    scale = (MLA_NOPE + MLA_ROPE) ** -0.5
    q = mm_nn(qlat, w_uq) * scale
    kv = mm_nn(kvlat, w_ukv)
    n_nope = MLA_HEADS * MLA_NOPE
    ropes = []
    for h in range(MLA_HEADS):
        qr = q[:, n_nope + LANES * h:n_nope + LANES * (h + 1)]
        ropes.append(qr * cos + rot_half(qr) * sin)
    return q[:, :n_nope], jnp.concatenate(ropes, axis=1), kv, kr * cos + rot_half(kr) * sin


MLA_Q_TILE = 256


def _mla_attn_block(qn, qr, kv, kr, q0):
    tq, nk = qn.shape[0], kv.shape[0]
    s = mm_nt(qn, kv[:, :MLA_NOPE]) + mm_nt(qr, kr)
    visible = (_iota2((tq, nk), 1) // CHUNK) <= ((q0 + _iota2((tq, nk), 0)) // CHUNK)
    s = jnp.where(visible, s, -1e30)
    e = jnp.exp(s - jnp.max(s, -1, keepdims=True))
    p = e / jnp.sum(e, -1, keepdims=True)
    return mm_nn(p, kv[:, MLA_NOPE:])


def _mla_attn_fwd(qn, qr, kv, kr):
    t = qn.shape[0]

    def body(qn_ref, qr_ref, kv_ref, kr_ref, o_ref):
        for i in range(t // MLA_Q_TILE):
            rows = slice(i * MLA_Q_TILE, (i + 1) * MLA_Q_TILE)
            keys = slice(0, (i + 1) * MLA_Q_TILE)
            o = _mla_attn_block(qn_ref[rows, :].astype(F32), qr_ref[rows, :].astype(F32), kv_ref[keys, :].astype(F32),
                                kr_ref[keys, :].astype(F32), i * MLA_Q_TILE)
            o_ref[rows, :] = o.astype(o_ref.dtype)

    return pl.pallas_call(
        body, name="mla_attn_fwd", grid=(MLA_HEADS,),
        in_specs=[pl.BlockSpec((t, MLA_NOPE), lambda h: (0, h)), pl.BlockSpec((t, LANES), lambda h: (0, h)),
                  pl.BlockSpec((t, MLA_NOPE + MLA_V), lambda h: (0, h)), pl.BlockSpec((t, LANES), lambda h: (0, 0))],
        out_specs=pl.BlockSpec((t, MLA_V), lambda h: (0, h)),
        out_shape=jax.ShapeDtypeStruct((t, MLA_HEADS * MLA_V), BF16),
        compiler_params=_params(),
    )(qn, qr, kv, kr)


def _mla_attn_bwd(qn, qr, kv, kr, do, after):
    t = qn.shape[0]
    after = [a for a in after if a is not None]

    def body(qn_ref, qr_ref, kv_ref, kr_ref, do_ref, *rest):
        dqn_ref, dqr_ref, dkv_ref, dkr_ref = rest[len(after):]
        dkv_ref[...] = jnp.zeros_like(dkv_ref)

        @pl.when(pl.program_id(0) == 0)
        def _():
            dkr_ref[...] = jnp.zeros_like(dkr_ref)

        for i in range(t // MLA_Q_TILE):
            rows = slice(i * MLA_Q_TILE, (i + 1) * MLA_Q_TILE)
            keys = slice(0, (i + 1) * MLA_Q_TILE)
            f = functools.partial(_mla_attn_block, q0=i * MLA_Q_TILE)
            _, vjp = jax.vjp(f, qn_ref[rows, :].astype(F32), qr_ref[rows, :].astype(F32), kv_ref[keys, :].astype(F32),
                             kr_ref[keys, :].astype(F32))
            dqn, dqr, dkv, dkr = vjp(do_ref[rows, :].astype(F32))
            dqn_ref[rows, :] = dqn
            dqr_ref[rows, :] = dqr
            dkv_ref[keys, :] += dkv
            dkr_ref[keys, :] += dkr

    return pl.pallas_call(
        body, name="mla_attn_bwd", grid=(MLA_HEADS,),
        in_specs=[pl.BlockSpec((t, MLA_NOPE), lambda h: (0, h)), pl.BlockSpec((t, LANES), lambda h: (0, h)),
                  pl.BlockSpec((t, MLA_NOPE + MLA_V), lambda h: (0, h)), pl.BlockSpec((t, LANES), lambda h: (0, 0)),
                  pl.BlockSpec((t, MLA_V), lambda h: (0, h))] + [pl.BlockSpec(memory_space=pl.ANY)] * len(after),
        out_specs=[pl.BlockSpec((t, MLA_NOPE), lambda h: (0, h)), pl.BlockSpec((t, LANES), lambda h: (0, h)),
                   pl.BlockSpec((t, MLA_NOPE + MLA_V), lambda h: (0, h)), pl.BlockSpec((t, LANES), lambda h: (0, 0))],
        out_shape=[jax.ShapeDtypeStruct(qn.shape, F32), jax.ShapeDtypeStruct(qr.shape, F32),
                   jax.ShapeDtypeStruct(kv.shape, F32), jax.ShapeDtypeStruct(kr.shape, F32)],
        compiler_params=_params(),
    )(qn, qr, kv, kr, do, *after)


def _rope_tables(pos_col, inv_freq_row):
    t = pos_col.shape[0]

    def body(pos_ref, f_ref, cos_ref, sin_ref):
        ang = pos_ref[...].astype(F32) * f_ref[...]
        live = _iota2(ang.shape, 1) < MLA_ROPE
        cos_ref[...] = jnp.where(live, jnp.cos(ang), 0.0)
        sin_ref[...] = jnp.where(live, jnp.sin(ang), 0.0)

    return pl.pallas_call(
        body, name="rope_tables", out_shape=[jax.ShapeDtypeStruct((t, LANES), F32)] * 2, compiler_params=_params(),
    )(pos_col, inv_freq_row)


CONV_COL_TILE = 256


def _conv_gate(b, c, u, w0, w1, w2):
    cu = c * u
    return b * (w2 * cu + w1 * shift_rows(cu, 1) + w0 * shift_rows(cu, 2))


def _conv_specs(t):
    nb = D_MODEL // CONV_COL_TILE
    return [pl.BlockSpec((t, CONV_COL_TILE), lambda j, part=part: (0, part * nb + j)) for part in range(3)]


def _conv_fwd(bcu, w):
    t = bcu.shape[0]

    def body(b_ref, c_ref, u_ref, w_ref, o_ref):
        o_ref[...] = _conv_gate(b_ref[...], c_ref[...], u_ref[...], w_ref[0:1, :], w_ref[1:2, :],
                                w_ref[2:3, :]).astype(o_ref.dtype)

    return pl.pallas_call(
        body, name="conv_fwd", grid=(D_MODEL // CONV_COL_TILE,),
        in_specs=_conv_specs(t) + [pl.BlockSpec((3, CONV_COL_TILE), lambda j: (0, j))],
        out_specs=pl.BlockSpec((t, CONV_COL_TILE), lambda j: (0, j)),
        out_shape=jax.ShapeDtypeStruct((t, D_MODEL), BF16), compiler_params=_params(),
    )(bcu, bcu, bcu, w)


def _conv_bwd(bcu, w, dout, after):
    t = bcu.shape[0]
    after = [a for a in after if a is not None]

    def body(b_ref, c_ref, u_ref, w_ref, do_ref, *rest):
        db_ref, dc_ref, du_ref, dw_ref = rest[len(after):]
        _, vjp = jax.vjp(_conv_gate, b_ref[...], c_ref[...], u_ref[...], w_ref[0:1, :], w_ref[1:2, :], w_ref[2:3, :])
        db, dc, du, dw0, dw1, dw2 = vjp(do_ref[...])
        db_ref[...] = db.astype(db_ref.dtype)
        dc_ref[...] = dc.astype(dc_ref.dtype)
        du_ref[...] = du.astype(du_ref.dtype)
        dw_ref[0:1, :] = dw0
        dw_ref[1:2, :] = dw1
        dw_ref[2:3, :] = dw2

    col = pl.BlockSpec((t, CONV_COL_TILE), lambda j: (0, j))
    return pl.pallas_call(
        body, name="conv_bwd", grid=(D_MODEL // CONV_COL_TILE,),
        in_specs=_conv_specs(t) + [pl.BlockSpec((3, CONV_COL_TILE), lambda j: (0, j)), col]
        + [pl.BlockSpec(memory_space=pl.ANY)] * len(after),
        out_specs=[col, col, col, pl.BlockSpec((3, CONV_COL_TILE), lambda j: (0, j))],
        out_shape=[jax.ShapeDtypeStruct((t, D_MODEL), BF16)] * 3 + [jax.ShapeDtypeStruct((3, D_MODEL), F32)],
        compiler_params=_params(),
    )(bcu, bcu, bcu, w, dout, *after)


def _loss_head(y, target):
    t, d = y.shape
    tm = 256

    def body(y_ref, t_ref, loss_ref, dy_ref):
        @pl.when(pl.program_id(0) == 0)
        def _():
            loss_ref[...] = jnp.zeros_like(loss_ref)

        err = y_ref[...] - t_ref[...]
        dy_ref[...] = err * (1.0 / d)
        loss_ref[...] += 0.5 * jnp.sum(jnp.sum(err * err, axis=-1, keepdims=True) * (1.0 / d))

    tile = pl.BlockSpec((tm, d), lambda i: (i, 0))
    return pl.pallas_call(
        body, name="loss_head", grid=(t // tm,), in_specs=[tile, tile],
        out_specs=[pl.BlockSpec((8, LANES), lambda i: (0, 0)), tile],
        out_shape=[jax.ShapeDtypeStruct((8, LANES), F32), jax.ShapeDtypeStruct((t, d), F32)],
        compiler_params=_params(),
    )(y, target)


def _ln_epi(acc, res, g, b):
    a = ALPHA * res + acc
    return a, _layer_norm(a, g, b)


def _ln_fn(a, g, b):
    return (_layer_norm(a, g, b),)


def _relu_sq(h):
    r = jnp.maximum(h.astype(F32), 0.0)
    return r * r


def _pad_cols(w, n):
    return jnp.pad(w, ((0, 0), (0, n - w.shape[1])))


def _pad_rows(w, n):
    return jnp.pad(w, ((0, n - w.shape[0]), (0, 0)))


def _uq_to_kernel_layout(w_uq):
    w = w_uq.reshape(MLA_RANK, MLA_HEADS, MLA_NOPE + MLA_ROPE)
    nope = w[:, :, :MLA_NOPE].reshape(MLA_RANK, MLA_HEADS * MLA_NOPE)
    rope = jnp.pad(w[:, :, MLA_NOPE:], ((0, 0), (0, 0), (0, LANES - MLA_ROPE))).reshape(MLA_RANK, MLA_HEADS * LANES)
    return jnp.concatenate([nope, rope], axis=1)


def _uq_from_kernel_layout(w):
    nope = w[:, :MLA_HEADS * MLA_NOPE].reshape(MLA_RANK, MLA_HEADS, MLA_NOPE)
    rope = w[:, MLA_HEADS * MLA_NOPE:].reshape(MLA_RANK, MLA_HEADS, LANES)[:, :, :MLA_ROPE]
    return jnp.concatenate([nope, rope], axis=2).reshape(MLA_RANK, MLA_HEADS * (MLA_NOPE + MLA_ROPE))


def _step(x, p, positions, target, small, comm):
    t = x.shape[0]
    w = small
    freqs = ROPE_BASE ** (-jnp.arange(0, MLA_ROPE // 2, dtype=F32) * (2.0 / MLA_ROPE))
    freq_row = jnp.concatenate([freqs, freqs, jnp.zeros((LANES - MLA_ROPE,), F32)])[None, :]
    cos, sin = _rope_tables(positions.reshape(t, 1), freq_row)

    saved = []
    for i in range(DEPTH):
        j, kind = i // 3, i % 3
        wl = comm.mixer_weights(i)
        s = {"x": x, "wl": wl}
        tok = comm.at("fwd", i, "begin", x)
        if kind == 0:
            w_in = wl["gla_w_in"]
            s["w_main"] = w_in[:, :GLA_MAIN]
            s["w_lr"] = _pad_cols(w_in[:, GLA_MAIN:], LANES)
            s["w_up"] = _pad_rows(w["gla_w_gate_up"][j], LANES).astype(BF16)
            s["proj"] = _matmul(x, s["w_main"], name="gla_proj", tn=1024, after=tok)
            s["glr"] = _matmul(x, s["w_lr"], name="gla_lr", out_dtypes=(F32,))
            s["z"] = _matmul(s["glr"], s["w_up"], name="gla_gate", epi=lambda acc, b: (acc + b,),
                             epi_ins=(w["gla_b_gate"][j][None, :],), out_dtypes=(F32,))
            s["u"], s["states"] = _gla_fwd(s["proj"], s["z"], w["gla_norm_g"][j][None, :])
        elif kind == 1:
            s["w_in"] = _pad_cols(wl["mla_w_in"], MLA_IN_PAD)
            s["w_uq"] = _uq_to_kernel_layout(wl["mla_w_uq"])
            s["cq"] = _matmul(x, s["w_in"], name="mla_proj", tn=MLA_IN_PAD, out_dtypes=(F32,), after=tok)
            s["pre_params"] = (w["mla_q_norm"][j][None, :], w["mla_kv_norm"][j][None, :], s["w_uq"], wl["mla_w_ukv"])
            s["qn"], s["qr"], s["kv"], s["kr"] = _tile_fwd(_mla_pre, (s["cq"], cos, sin), s["pre_params"],
                                                           (BF16, BF16, BF16, BF16), tm=256, name="mla_pre_fwd")
            s["u"] = _mla_attn_fwd(s["qn"], s["qr"], s["kv"], s["kr"])
        else:
            s["bcu"] = _matmul(x, wl["conv"], name="conv_proj", tb=True, tm=256, tn=3 * D_MODEL, b_at=REG_CONV,
                               out_dtypes=(F32,), after=tok)
            s["u"] = _conv_fwd(s["bcu"], w["conv_w"][j])
        g0, b0 = w["ln_g"][i, 0][None, :], w["ln_b"][i, 0][None, :]
        g1, b1 = w["ln_g"][i, 1][None, :], w["ln_b"][i, 1][None, :]
        wa, wb = s["wa"], _ = comm.slab_weights(i, s["u"])
        s["a1"], s["x1"] = _matmul(s["u"], wa, name="mixer_out_ln", tm=256, tn=D_MODEL, b_at=REG_WOUT, epi=_ln_epi,
                                   epi_ins=(x, g0, b0), out_dtypes=(F32, F32))
        s["hh"] = _matmul(s["x1"], wa, name="mlp_up", tb=True, tm=256, tn=D_FF, b_at=REG_W1T)
        tok = comm.at("fwd", i, "mid", s["hh"])
        s["a2"], s["x2"] = _matmul(s["hh"], wa, name="mlp_down_ln", tm=256, tn=D_MODEL, b_at=REG_W2, a_fn=_relu_sq,
                                   epi=_ln_epi, epi_ins=(s["x1"], g1, b1), out_dtypes=(F32, F32), after=tok)
        s["pp"] = _matmul(p[i], wb, name="ple_proj", tb=True, tn=D_MODEL, b_at=REG_WPT)
        tok = comm.at("fwd", i, "end", s["pp"])
        x, s["gt"] = _matmul(s["x2"], wa, name="ple_gate", tn=1024, b_at=REG_WG,
                             epi=lambda acc, xr, pp: (xr + jax.nn.sigmoid(acc) * pp.astype(F32), acc),
                             epi_ins=(s["x2"], s["pp"]), out_dtypes=(F32, BF16), after=tok)
        saved.append(s)

    loss_part, dx = _loss_head(x, target)

    gw = {n: [None] * WEIGHTS[n][0][0] for n in SMALL + REPLICATED}
    ln_g_grads, ln_b_grads = [[None, None] for _ in range(DEPTH)], [[None, None] for _ in range(DEPTH)]
    resid = lambda acc, r: (acc + ALPHA * r,)
    plus = lambda acc, r: (acc + r,)
    for i in reversed(range(DEPTH)):
        j, kind = i // 3, i % 3
        s = saved[i]
        wa = s["wa"]
        ga = lax.empty((N_DEV, A_ROWS, D_MODEL), BF16)
        gb = lax.empty((N_DEV, REG_WPT[1], PLE_DIM), BF16)
        layer_grads = {}
        tok = comm.at("bwd", i, "begin", dx)

        def ple_bwd(dxo, gt, pp):
            sg = jax.nn.sigmoid(gt)
            return dxo * sg, dxo * pp * sg * (1.0 - sg)

        d_pp, d_gt = _tile_fwd(ple_bwd, (dx, s["gt"], s["pp"]), (), (BF16, BF16), tm=256, name="ple_bwd")
        gb = _matmul(d_pp, p[i], name="ple_proj_dw", ta=True, tm=REG_WPT[1], tn=PLE_DIM, out_at=REG_WPT, out_buf=gb, after=tok)
        ga = _matmul(s["x2"], d_gt, name="ple_gate_dw", ta=True, tm=REG_WG[1], tn=1024, out_at=REG_WG, out_buf=ga)
        dx2 = _matmul(d_gt, wa, name="ple_gate_dx", tb=True, tn=1024, b_at=REG_WG, epi=plus, epi_ins=(dx,),
                      out_dtypes=(F32,), after=[ga, gb])
        g1, b1 = w["ln_g"][i, 1][None, :], w["ln_b"][i, 1][None, :]
        (d_a2,), (ln_g_grads[i][1], ln_b_grads[i][1]) = _tile_bwd(_ln_fn, (s["a2"],), (g1, b1), (dx2,), (F32,), tm=256,
                                                                    name="ln_bwd")
        tok = comm.at("bwd", i, "ln", d_a2)
        ga = _matmul(s["hh"], d_a2, name="mlp_down_dw", ta=True, tm=REG_W2[1], tn=1024, a_fn=_relu_sq, out_at=REG_W2,
                     out_buf=ga, after=tok)
        d_hh = _matmul(d_a2, wa, name="mlp_down_dx", tb=True, tm=256, tn=D_FF, b_at=REG_W2, after=ga,
                       epi=lambda acc, hh: (acc * 2.0 * jnp.maximum(hh.astype(F32), 0.0),), epi_ins=(s["hh"],))
        ga = _matmul(d_hh, s["x1"], name="mlp_up_dw", ta=True, tm=REG_W1T[1], tn=1024, out_at=REG_W1T, out_buf=ga)
        dx1 = _matmul(d_hh, wa, name="mlp_up_dx", tm=256, tn=1024, b_at=REG_W1T, epi=resid, epi_ins=(d_a2,),
                      out_dtypes=(F32,), after=ga)
        g0, b0 = w["ln_g"][i, 0][None, :], w["ln_b"][i, 0][None, :]
        (d_a1,), (ln_g_grads[i][0], ln_b_grads[i][0]) = _tile_bwd(_ln_fn, (s["a1"],), (g0, b0), (dx1,), (F32,), tm=256,
                                                                    name="ln_bwd")
        ga = _matmul(s["u"], d_a1, name="mixer_out_dw", ta=True, tm=REG_WOUT[1], tn=1024, out_at=REG_WOUT, out_buf=ga)
        du = _matmul(d_a1, wa, name="mixer_out_dx", tb=True, tn=1024, b_at=REG_WOUT, out_dtypes=(F32,), after=ga)
        comm.slab_grads(i, ga, gb)
        tok = comm.at("bwd", i, "slab_done", du) or []
        if kind == 0:
            dproj, dz, dg = _gla_bwd(s["proj"], s["z"], w["gla_norm_g"][j][None, :], s["states"], du, tok)
            tok = comm.at("bwd", i, "mixer_done", dproj)
            gw["gla_norm_g"][j] = dg[0]
            gw["gla_b_gate"][j] = _tile_bwd(lambda zz, b: (zz + b,), (s["z"],), (w["gla_b_gate"][j][None, :],), (dz,), (),
                                            tm=256, name="gla_bias_bwd", diff_tiled=[])[1][0][0]
            gw["gla_w_gate_up"][j] = _matmul(s["glr"], dz, name="gla_gate_dw", ta=True, out_dtypes=(F32,),
                                             after=tok)[:GLA_RANK]
            dglr = _matmul(dz, s["w_up"], name="gla_gate_dx", tb=True, out_dtypes=(F32,))
            dw_main = _matmul(s["x"], dproj, name="gla_proj_dw", ta=True, tn=1024, out_dtypes=(F32,))
            dw_lr = _matmul(s["x"], dglr, name="gla_lr_dw", ta=True, out_dtypes=(F32,))[:, :GLA_RANK]
            layer_grads["gla_w_in"] = jnp.concatenate([dw_main, dw_lr], axis=1)
            dx = _matmul(dproj, s["w_main"], name="gla_proj_dx", tb=True, tn=1024, epi=resid, epi_ins=(d_a1,),
                         out_dtypes=(F32,), after=[dw_main, dw_lr, gw["gla_w_gate_up"][j]])
            dx = _matmul(dglr, s["w_lr"], name="gla_lr_dx", tb=True, tn=1024, epi=plus, epi_ins=(dx,), out_dtypes=(F32,))
        elif kind == 1:
            dqn, dqr, dkv, dkr = _mla_attn_bwd(s["qn"], s["qr"], s["kv"], s["kr"], du, tok)
            tok = comm.at("bwd", i, "mixer_done", dqn)
            (d_cq,), (dgq, dgkv, dw_uq, dw_ukv) = _tile_bwd(_mla_pre, (s["cq"], cos, sin), s["pre_params"],
                                                           (dqn, dqr, dkv, dkr), (BF16,), tm=256, name="mla_pre_bwd",
                                                           diff_tiled=[0])
            gw["mla_q_norm"][j], gw["mla_kv_norm"][j] = dgq[0], dgkv[0]
            layer_grads["mla_w_uq"] = _uq_from_kernel_layout(dw_uq)
            layer_grads["mla_w_ukv"] = dw_ukv
            layer_grads["mla_w_in"] = _matmul(s["x"], d_cq, name="mla_proj_dw", ta=True, tn=MLA_IN_PAD,
                                              out_dtypes=(F32,), after=tok)[:, :MLA_IN]
            dx = _matmul(d_cq, s["w_in"], name="mla_proj_dx", tb=True, tn=1024, epi=resid, epi_ins=(d_a1,),
                         out_dtypes=(F32,), after=layer_grads["mla_w_in"])
        else:
            db, dc, du_, dcw = _conv_bwd(s["bcu"], w["conv_w"][j], du, tok)
            tok = comm.at("bwd", i, "mixer_done", db)
            gw["conv_w"][j] = dcw
            dbcu = jnp.concatenate([db, dc, du_], axis=1)
            layer_grads["conv"] = _matmul(dbcu, s["x"], name="conv_proj_dw", ta=True, tm=REG_CONV[1], tn=1024,
                                          out_at=REG_CONV, out_buf=lax.empty((N_DEV, REG_CONV[1], D_MODEL), BF16),
                                          after=tok)
            dx = _matmul(dbcu, s["wl"]["conv"], name="conv_proj_dx", tn=1024, b_at=REG_CONV, epi=resid, epi_ins=(d_a1,),
                         out_dtypes=(F32,), after=layer_grads["conv"])
        comm.mixer_grads(i, layer_grads)

    gw["ln_g"] = [jnp.concatenate([a, b], axis=0) for a, b in ln_g_grads]
    gw["ln_b"] = [jnp.concatenate([a, b], axis=0) for a, b in ln_b_grads]
    return loss_part, dx, {n: jnp.stack(gw[n]).astype(F32) for n in gw}


MESH_IDS = pl.DeviceIdType.MESH
ANY = pl.BlockSpec(memory_space=pl.ANY)
HBM_SPEC = pl.BlockSpec(memory_space=pltpu.HBM)
SEM_SPEC = pl.BlockSpec(memory_space=pltpu.SEMAPHORE)
DATAFLOW_EFFECT = pltpu.SideEffectType.DATAFLOW_SIDE_EFFECTING
CORE_COPIES, CHIP_COPIES = 4, 3


def _my_place():
    return lax.axis_index("x"), lax.axis_index("y"), lax.axis_index("c")


def _other_chips(mx, my):
    return [(1 - mx, my), (mx, 1 - my), (1 - mx, 1 - my)]


def _remote(src, dst, send_sems, recv_sems, k, to):
    return pltpu.make_async_remote_copy(src_ref=src, dst_ref=dst, send_sem=send_sems.at[k], recv_sem=recv_sems.at[k],
                                        device_id=to, device_id_type=MESH_IDS)


def _gather_first_copies(n_arr):
    def make(bufs, send_sems, recv_sems):
        mx, my, mc = _my_place()
        mine = 4 * mx + 2 * my + mc
        peers = [(mx, my, 1 - mc)] + [(cx, cy, mc) for cx, cy in _other_chips(mx, my)]
        return [_remote(bufs[a].at[mine], bufs[a].at[mine], send_sems, recv_sems, (1 + CHIP_COPIES) * a + k, to)
                for a in range(n_arr) for k, to in enumerate(peers)]
    return make, (1 + CHIP_COPIES) * n_arr


def _gather_forward_copies(n_arr):
    def make(bufs, send_sems, recv_sems):
        mx, my, mc = _my_place()
        blocks = [4 * cx + 2 * cy + mc for cx, cy in _other_chips(mx, my)]
        return [_remote(bufs[a].at[blk], bufs[a].at[blk], send_sems, recv_sems, CHIP_COPIES * a + k, (mx, my, 1 - mc))
                for a in range(n_arr) for k, blk in enumerate(blocks)]
    return make, CHIP_COPIES * n_arr


def _scatter_core_copies(n_arr):
    def make(bufs, send_sems, recv_sems):
        mx, my, mc = _my_place()
        return [_remote(bufs[a].at[2 * k + (1 - mc)], bufs[n_arr + a].at[k], send_sems, recv_sems, CORE_COPIES * a + k,
                        (mx, my, 1 - mc)) for a in range(n_arr) for k in range(CORE_COPIES)]
    return make, CORE_COPIES * n_arr


def _scatter_chip_copies(n_arr):
    def make(bufs, send_sems, recv_sems):
        mx, my, mc = _my_place()
        return [_remote(bufs[a].at[2 * cx + cy], bufs[n_arr + a].at[k], send_sems, recv_sems, CHIP_COPIES * a + k,
                        (cx, cy, mc)) for a in range(n_arr) for k, (cx, cy) in enumerate(_other_chips(mx, my))]
    return make, CHIP_COPIES * n_arr


def _exchange(name, bufs, copies):
    make, n_copies = copies
    n = len(bufs)

    def body(*refs):
        descs = make(refs[:n], refs[2 * n], refs[2 * n + 1])
        for cp in descs:
            cp.start()
        for cp in descs:
            cp.wait()

    return pl.pallas_call(
        body, name=name, out_shape=[jax.ShapeDtypeStruct(b.shape, b.dtype) for b in bufs], in_specs=[ANY] * n,
        out_specs=[ANY] * n, input_output_aliases={i: i for i in range(n)},
        scratch_shapes=[pltpu.SemaphoreType.DMA((n_copies,)), pltpu.SemaphoreType.DMA((n_copies,))],
    )(*bufs)


def _exchange_start(name, bufs, copies, after):
    make, n_copies = copies
    n = len(bufs)

    def body(*refs):
        for cp in make(refs[:n], refs[n + 1], refs[n + 2]):
            cp.start()
        refs[-1][...] = jnp.zeros_like(refs[-1])

    outs = pl.pallas_call(
        body, name=name,
        out_shape=(pltpu.SemaphoreType.DMA((n_copies,)), pltpu.SemaphoreType.DMA((n_copies,)),
                   *[pltpu.HBM(b.shape, b.dtype) for b in bufs], jax.ShapeDtypeStruct((8, LANES), F32)),
        in_specs=[HBM_SPEC] * n + [ANY],
        out_specs=(SEM_SPEC, SEM_SPEC, *[HBM_SPEC] * n, pl.BlockSpec(memory_space=pltpu.VMEM)),
        input_output_aliases={i: 2 + i for i in range(n)},
        compiler_params=pltpu.CompilerParams(has_side_effects=DATAFLOW_EFFECT),
    )(*[pltpu.with_memory_space_constraint(b, pltpu.HBM) for b in bufs], after)
    return (outs[0], outs[1]), list(outs[2:2 + n]), outs[-1]


def _exchange_wait(name, sems, bufs, copies, after):
    make, _ = copies
    n = len(bufs)

    def body(*refs):
        for cp in make(refs[:n], refs[n], refs[n + 1]):
            cp.wait_send()
            cp.wait_recv()

    return list(pl.pallas_call(
        body, name=name, out_shape=[pltpu.HBM(b.shape, b.dtype) for b in bufs],
        in_specs=[HBM_SPEC] * n + [SEM_SPEC, SEM_SPEC, ANY], out_specs=[HBM_SPEC] * n,
        input_output_aliases={i: i for i in range(n)},
        compiler_params=pltpu.CompilerParams(has_side_effects=DATAFLOW_EFFECT),
    )(*bufs, *sems, after))


def _row_tile(r):
    for cand in (PACK_ROW_TILE, 128, 64, 32, 16, 8):
        if r % cand == 0:
            return cand
    return r


def _pair_sum(g, recv, my_c):
    _, r, c = g.shape
    tr = _row_tile(r)

    def body(c_ref, g_ref, r_ref, o_ref):
        o_ref[...] = (g_ref[...].astype(F32) + r_ref[...].astype(F32)).astype(o_ref.dtype)

    return pl.pallas_call(
        body, name="rs_pair_sum", out_shape=jax.ShapeDtypeStruct((4, r, c), g.dtype),
        grid_spec=pltpu.PrefetchScalarGridSpec(
            num_scalar_prefetch=1, grid=(4, r // tr),
            in_specs=[pl.BlockSpec((1, tr, c), lambda n, i, cr: (2 * n + cr[0], i, 0)),
                      pl.BlockSpec((1, tr, c), lambda n, i, cr: (n, i, 0))],
            out_specs=pl.BlockSpec((1, tr, c), lambda n, i, cr: (n, i, 0))),
        compiler_params=_params(),
    )(my_c, g, recv)


def _chip_sum(h, recv, my_chip):
    _, r, c = h.shape
    tr = _row_tile(r)

    def body(j_ref, h_ref, r0_ref, r1_ref, r2_ref, o_ref):
        o_ref[...] = ((h_ref[0].astype(F32) + r0_ref[0].astype(F32)) + r1_ref[0].astype(F32)) + r2_ref[0].astype(F32)

    return pl.pallas_call(
        body, name="rs_chip_sum", out_shape=jax.ShapeDtypeStruct((r, c), F32),
        grid_spec=pltpu.PrefetchScalarGridSpec(
            num_scalar_prefetch=1, grid=(r // tr,),
            in_specs=[pl.BlockSpec((1, tr, c), lambda i, jr: (jr[0], i, 0))]
            + [pl.BlockSpec((1, tr, c), lambda i, jr, n=n: (n, i, 0)) for n in range(3)],
            out_specs=pl.BlockSpec((tr, c), lambda i, jr: (i, 0))),
        compiler_params=_params(),
    )(my_chip, h, recv, recv, recv)


def _sum_blocks(g):
    n, r, c = g.shape

    def body(g_ref, o_ref):
        acc = g_ref[0]
        for k in range(1, n):
            acc = acc + g_ref[k]
        o_ref[...] = acc

    return pl.pallas_call(body, name="sum_blocks", out_shape=jax.ShapeDtypeStruct((r, c), F32), compiler_params=_params())(g)


def _pack(flat_parts, cols, row_multiple, dtype):
    flat = jnp.concatenate([f.astype(dtype) for f in flat_parts])
    per_row_block = cols * row_multiple
    padded = -(-flat.shape[0] // per_row_block) * per_row_block
    return jnp.pad(flat, (0, padded - flat.shape[0])).reshape(padded // cols, cols)


def _shard_shape(name):
    shape, axis = WEIGHTS[name]
    if axis is None:
        return shape
    return tuple(s // N_DEV if a == axis else s for a, s in enumerate(shape))


def _size(shape):
    n = 1
    for s in shape:
        n *= s
    return n


def _unshard(blocks, name):
    _, axis = WEIGHTS[name]
    return jnp.concatenate([blocks[k] for k in range(N_DEV)], axis=axis)


def _unpack_blocks(flat, names, lead):
    out, off = {}, 0
    for n in names:
        shp = _shard_shape(n)[1:] if lead else _shard_shape(n)
        out[n] = flat[..., off:off + _size(shp)].reshape(flat.shape[:-1] + shp)
        off += _size(shp)
    return out


def _layer_slabs(shard, i):
    j, kind = i // 3, i % 3
    w_out = (shard["gla_w_out"], shard["mla_w_out"], shard["conv_w_out"])[kind][j]
    out = {"a": jnp.concatenate([shard["mlp_w2"][i], shard["mlp_w1"][i].T, w_out, shard["ple_w_gate"][i]], axis=0).astype(BF16),
           "b": shard["ple_w_proj"][i].T.astype(BF16)}
    if kind == 0:
        out["gla"] = shard["gla_w_in"][j].astype(BF16)
    elif kind == 1:
        out["mla"] = _pack([shard[n][j].reshape(-1) for n in MLA_PACKED], PACK_COLS, PACK_ROW_TILE, BF16)
    else:
        out["conv"] = shard["conv_w_in"][j].T.astype(BF16)
    return out


def _mixer_weights(landed, i):
    kind = i % 3
    if kind == 0:
        return {"gla_w_in": jnp.concatenate([landed["gla"][k] for k in range(N_DEV)], axis=1)}
    if kind == 2:
        return {"conv": landed["conv"]}
    blocks = _unpack_blocks(landed["mla"].reshape(N_DEV, -1), MLA_PACKED, lead=True)
    return {n: jnp.concatenate([blocks[n][k] for k in range(N_DEV)], axis=WEIGHTS[n][1] - 1) for n in MLA_PACKED}


def _mixer_grad_buffers(layer_grads, i):
    kind = i % 3
    if kind == 0:
        return {"gla": jnp.stack(jnp.split(layer_grads["gla_w_in"], N_DEV, axis=1)).astype(BF16)}
    if kind == 2:
        return {"conv": layer_grads["conv"]}
    parts = [jnp.stack(jnp.split(layer_grads[n], N_DEV, axis=WEIGHTS[n][1] - 1)).reshape(N_DEV, -1) for n in MLA_PACKED]
    cat = jnp.concatenate(parts, axis=1).astype(BF16)
    per = PACK_COLS * PACK_ROW_TILE
    padded = -(-cat.shape[1] // per) * per
    return {"mla": jnp.pad(cat, ((0, 0), (0, padded - cat.shape[1]))).reshape(N_DEV, padded // PACK_COLS, PACK_COLS)}


SLAB_KEYS = ("a", "b")


class _Overlap:
    def __init__(self, shard):
        mx, my, mc = _my_place()
        self.my_c = mc.astype(jnp.int32).reshape(1)
        self.my_chip = (2 * mx + my).astype(jnp.int32).reshape(1)
        mine = 4 * mx + 2 * my + mc
        self.landing = [{k: lax.dynamic_update_index_in_dim(lax.empty((N_DEV, *v.shape), v.dtype), v, mine, 0)
                         for k, v in _layer_slabs(shard, i).items()} for i in range(DEPTH)]
        self.fly = {}
        self.gather_keys = {}
        self.grads = [{} for _ in range(DEPTH)]
        self.reduced = [{} for _ in range(DEPTH)]
        keys = self._keys(0, "mixer")
        tok = self._start("ag_first_mixer_l0", [self.landing[0][k] for k in keys], _gather_first_copies(len(keys)), shard["ln_g"])
        tok = self._start("ag_first_slab_l0", [self.landing[0][k] for k in SLAB_KEYS], _gather_first_copies(2), tok)
        bufs = self._wait("ag_first_mixer_l0", tok)
        self.landing[0].update(zip(keys, _exchange("ag_forward_mixer_l0", bufs, _gather_forward_copies(len(bufs)))))

    def _keys(self, i, group):
        return [k for k in self.landing[i] if (k in SLAB_KEYS) == (group == "slab")]

    def _start(self, name, bufs, copies, after):
        sems, bufs, tok = _exchange_start(name + "_start", bufs, copies, after)
        self.fly[name] = (sems, bufs, copies)
        return tok

    def _wait(self, name, after):
        sems, bufs, copies = self.fly.pop(name)
        return _exchange_wait(name + "_wait", sems, bufs, copies, after)

    def mixer_weights(self, i):
        return _mixer_weights(self.landing[i], i)

    def slab_weights(self, i, dep):
        if i == 0:
            bufs = self._wait("ag_first_slab_l0", dep)
            self.landing[0].update(zip(SLAB_KEYS, _exchange("ag_forward_slab_l0", bufs, _gather_forward_copies(2))))
        return self.landing[i]["a"], self.landing[i]["b"]

    def slab_grads(self, i, ga, gb):
        self.grads[i].update(a=ga, b=gb)

    def mixer_grads(self, i, layer_grads):
        self.grads[i].update(_mixer_grad_buffers(layer_grads, i))

    def at(self, phase, i, point, dep):
        toks = []
        if phase == "fwd":
            if point == "begin" and i == 0:
                toks.append(self._gather_first(1, self.landing[0][self._keys(0, "mixer")[0]]))
            if point == "mid" and i + 1 < DEPTH:
                bufs = self._wait(f"ag_first_l{i + 1}", dep)
                toks.append(self._start(f"ag_forward_l{i + 1}", bufs, _gather_forward_copies(len(bufs)), dep))
                if i + 2 < DEPTH:
                    toks.append(self._gather_first(i + 2, dep))
            if point == "end" and i + 1 < DEPTH:
                self.landing[i + 1].update(zip(self.gather_keys[i + 1], self._wait(f"ag_forward_l{i + 1}", dep)))
        else:
            if point == "begin" and i + 1 < DEPTH:
                toks.append(self._scatter_cores(i + 1, "mixer", dep))
            if point == "ln" and i + 1 < DEPTH:
                toks.append(self._scatter_chips(i + 1, "mixer", dep))
            if point == "slab_done":
                if i + 1 < DEPTH:
                    self._scatter_done(i + 1, "slab", dep)
                    self._scatter_done(i + 1, "mixer", dep)
                toks.append(self._scatter_cores(i, "slab", dep))
            if point == "mixer_done":
                toks.append(self._scatter_chips(i, "slab", dep))
        return toks or None

    def _gather_first(self, i, after):
        self.gather_keys[i] = list(self.landing[i])
        bufs = [self.landing[i][k] for k in self.gather_keys[i]]
        return self._start(f"ag_first_l{i}", bufs, _gather_first_copies(len(bufs)), after)

    def _scatter_cores(self, i, group, after):
        gs = [self.grads[i][k] for k in self._keys(i, group)]
        land = [lax.empty((4, *g.shape[1:]), g.dtype) for g in gs]
        return self._start(f"rs_cores_{group}_l{i}", gs + land, _scatter_core_copies(len(gs)), after)

    def _pair_sums(self, bufs):
        n = len(bufs) // 2
        hs = [_pair_sum(g, r, self.my_c) for g, r in zip(bufs[:n], bufs[n:])]
        return hs + [lax.empty((3, *h.shape[1:]), h.dtype) for h in hs]

    def _scatter_chips(self, i, group, after):
        bufs = self._pair_sums(self._wait(f"rs_cores_{group}_l{i}", after))
        return self._start(f"rs_chips_{group}_l{i}", bufs, _scatter_chip_copies(len(bufs) // 2), after)

    def _chip_sums(self, i, group, bufs):
        n = len(bufs) // 2
        for k, h, r in zip(self._keys(i, group), bufs[:n], bufs[n:]):
            self.reduced[i][k] = _chip_sum(h, r, self.my_chip)

    def _scatter_done(self, i, group, after):
        self._chip_sums(i, group, self._wait(f"rs_chips_{group}_l{i}", after))

    def finish(self, dep):
        self._scatter_done(0, "slab", dep)
        gs = [self.grads[0][k] for k in self._keys(0, "mixer")]
        bufs = _exchange("rs_cores_mixer_l0", gs + [lax.empty((4, *g.shape[1:]), g.dtype) for g in gs],
                         _scatter_core_copies(len(gs)))
        bufs = _exchange("rs_chips_mixer_l0", self._pair_sums(bufs), _scatter_chip_copies(len(gs)))
        self._chip_sums(0, "mixer", bufs)
        return self.reduced


def _all_gather_small(x, name):
    mx, my, mc = _my_place()
    land = lax.dynamic_update_index_in_dim(lax.empty((N_DEV, *x.shape), x.dtype), x, 4 * mx + 2 * my + mc, 0)
    (land,) = _exchange(name + "_first", [land], _gather_first_copies(1))
    (land,) = _exchange(name + "_forward", [land], _gather_forward_copies(1))
    return land


def _shard_grads(reduced):
    def rows(a, reg):
        return a[reg[0] * reg[1]:(reg[0] + 1) * reg[1]]

    a = [reduced[i]["a"] for i in range(DEPTH)]
    w_out = [rows(a[i], REG_WOUT) for i in range(DEPTH)]
    mla = _unpack_blocks(reduced[1]["mla"].reshape(-1), MLA_PACKED, lead=True)
    out = {
        "mlp_w2": jnp.stack([rows(a[i], REG_W2) for i in range(DEPTH)]),
        "mlp_w1": jnp.stack([rows(a[i], REG_W1T).T for i in range(DEPTH)]),
        "gla_w_out": jnp.stack([w_out[0], w_out[3]]), "mla_w_out": w_out[1][None], "conv_w_out": w_out[2][None],
        "ple_w_gate": jnp.stack([rows(a[i], REG_WG) for i in range(DEPTH)]),
        "ple_w_proj": jnp.stack([reduced[i]["b"].T for i in range(DEPTH)]),
        "conv_w_in": reduced[2]["conv"].T[None],
        "gla_w_in": jnp.stack([reduced[0]["gla"], reduced[3]["gla"]]),
    }
    out.update({n: mla[n][None] for n in MLA_PACKED})
    return out


def _adamw(w, g, m, v, name):
    shape = w.shape
    cols = shape[-1]
    rows = _size(shape) // cols
    tr = rows
    for cand in (512, 256, 128, 64, 32, 16, 8):
        if rows > cand and rows % cand == 0:
            tr = cand
            break

    def body(w_ref, g_ref, m_ref, v_ref, d_ref, mo_ref, vo_ref):
        gv = g_ref[...]
        m2 = ADAM_B1 * m_ref[...] + (1.0 - ADAM_B1) * gv
        v2 = ADAM_B2 * v_ref[...] + (1.0 - ADAM_B2) * (gv * gv)
        m_hat = m2 / (1.0 - ADAM_B1 ** ADAM_STEP)
        v_hat = v2 / (1.0 - ADAM_B2 ** ADAM_STEP)
        d_ref[...] = -ADAM_LR * (m_hat / (jnp.sqrt(v_hat) + ADAM_EPS) + ADAM_WD * w_ref[...])
        mo_ref[...] = m2
        vo_ref[...] = v2

    spec = pl.BlockSpec((tr, cols), lambda i: (i, 0))
    outs = pl.pallas_call(
        body, name="adamw_" + name, grid=(rows // tr,), in_specs=[spec] * 4, out_specs=[spec] * 3,
        out_shape=[jax.ShapeDtypeStruct((rows, cols), F32)] * 3, compiler_params=_params(),
    )(*[a.reshape(rows, cols) for a in (w, g, m, v)])
    return [o.reshape(shape) for o in outs]


def kernel(x, p, positions, gla_w_in, gla_w_gate_up, gla_b_gate, gla_norm_g, gla_w_out, mla_w_in, mla_q_norm, mla_kv_norm, mla_w_uq, mla_w_ukv, mla_w_out, conv_w_in, conv_w, conv_w_out, ln_g, ln_b, mlp_w1, mlp_w2, ple_w_gate, ple_w_proj, loss_target, m_gla_w_in, m_gla_w_gate_up, m_gla_b_gate, m_gla_norm_g, m_gla_w_out, m_mla_w_in, m_mla_q_norm, m_mla_kv_norm, m_mla_w_uq, m_mla_w_ukv, m_mla_w_out, m_conv_w_in, m_conv_w, m_conv_w_out, m_ln_g, m_ln_b, m_mlp_w1, m_mlp_w2, m_ple_w_gate, m_ple_w_proj, v_gla_w_in, v_gla_w_gate_up, v_gla_b_gate, v_gla_norm_g, v_gla_w_out, v_mla_w_in, v_mla_q_norm, v_mla_kv_norm, v_mla_w_uq, v_mla_w_ukv, v_mla_w_out, v_conv_w_in, v_conv_w, v_conv_w_out, v_ln_g, v_ln_b, v_mlp_w1, v_mlp_w2, v_ple_w_gate, v_ple_w_proj):
    args = locals()
    shard = {n: args[n] for n in WEIGHT_NAMES}
    mom = {n: args["m_" + n] for n in WEIGHT_NAMES}
    var = {n: args["v_" + n] for n in WEIGHT_NAMES}
    mx, my, mc = _my_place()

    small_all = _all_gather_small(_pack([shard[n].reshape(-1) for n in SMALL], LANES, 8, F32), "ag_small")
    small = {n: shard[n] for n in REPLICATED}
    small.update({n: _unshard(blk, n) for n, blk in _unpack_blocks(small_all.reshape(N_DEV, -1), SMALL, lead=False).items()})

    comm = _Overlap(shard)
    loss_part, grad_x, small_grads = _step(x[0], p[:, 0], positions[0], loss_target[0], small, comm)
    loss = lax.psum(loss_part[0, 0], MESH_AXES)
    my_grads = _shard_grads(comm.finish(grad_x))

    small_parts = [small_grads[n].reshape(-1) for n in SMALL + REPLICATED]
    red_small = _sum_blocks(_all_gather_small(_pack(small_parts, LANES, 8, F32), "ag_small_grads")).reshape(-1)
    off = 0
    dev = 4 * mx + 2 * my + mc
    for n in SMALL + REPLICATED:
        shape, axis = WEIGHTS[n]
        full_g = red_small[off:off + _size(shape)].reshape(shape)
        off += _size(shape)
        if axis is None:
            my_grads[n] = full_g
        else:
            width = shape[axis] // N_DEV
            my_grads[n] = lax.dynamic_slice_in_dim(full_g, dev * width, width, axis=axis)

    deltas, new_m, new_v = {}, {}, {}
    for n in WEIGHT_NAMES:
        deltas[n], new_m[n], new_v[n] = _adamw(shard[n], my_grads[n], mom[n], var[n], n)
    return (loss, grad_x[None], *[my_grads[n] for n in WEIGHT_NAMES], *[deltas[n] for n in WEIGHT_NAMES],
            *[new_m[n] for n in WEIGHT_NAMES], *[new_v[n] for n in WEIGHT_NAMES])
```

```python
import functools

import jax
import jax.numpy as jnp
from jax import lax
from jax.experimental import pallas as pl
from jax.experimental.pallas import tpu as pltpu

F32, BF16 = jnp.float32, jnp.bfloat16
HIGHEST = lax.Precision.HIGHEST
MESH_AXES = ("x", "y", "c")
N_DEV = 8

D_MODEL = 1024
SEQ = 2048
DEPTH = 4
CHUNK = 64
ALPHA = (2 * DEPTH) ** 0.25
LN_EPS = 1e-5
RMS_EPS = 1e-6
PLE_DIM = 256
D_FF = 4 * D_MODEL
GLA_HEADS = 4
GLA_DK = 128
GLA_DV = 256
GLA_RANK = 16
GLA_TAU = 16.0
GLA_HK = GLA_HEADS * GLA_DK
GLA_HV = GLA_HEADS * GLA_DV
GLA_MAIN = 2 * GLA_HK + GLA_HV + D_MODEL
MLA_HEADS = 8
MLA_NOPE = 128
MLA_ROPE = 64
MLA_V = 128
MLA_RANK = 256
MLA_IN = 2 * MLA_RANK + MLA_ROPE
MLA_IN_PAD = 640
ROPE_BASE = 10000.0
LANES = 128
ADAM_LR, ADAM_B1, ADAM_B2, ADAM_EPS, ADAM_WD, ADAM_STEP = 0.001, 0.9, 0.999, 1e-08, 0.01, 10

V7X_VMEM_LIMIT_BYTES = 56 * 1024 * 1024
PACK_COLS = 1024
PACK_ROW_TILE = 256

WEIGHTS = {
    "gla_w_in": ((2, 1024, 3088), 2), "gla_w_gate_up": ((2, 16, 512), 2), "gla_b_gate": ((2, 512), 1),
    "gla_norm_g": ((2, 256), 1), "gla_w_out": ((2, 1024, 1024), 1), "mla_w_in": ((1, 1024, 576), 1),
    "mla_q_norm": ((1, 256), None), "mla_kv_norm": ((1, 256), None), "mla_w_uq": ((1, 256, 1536), 2),
    "mla_w_ukv": ((1, 256, 2048), 2), "mla_w_out": ((1, 1024, 1024), 1), "conv_w_in": ((1, 1024, 3072), 2),
    "conv_w": ((1, 3, 1024), 2), "conv_w_out": ((1, 1024, 1024), 1), "ln_g": ((4, 2, 1024), 2),
    "ln_b": ((4, 2, 1024), 2), "mlp_w1": ((4, 1024, 4096), 2), "mlp_w2": ((4, 4096, 1024), 1),
    "ple_w_gate": ((4, 1024, 1024), 1), "ple_w_proj": ((4, 256, 1024), 2),
}
WEIGHT_NAMES = list(WEIGHTS)
REG_W2, REG_W1T, REG_WOUT, REG_WG = (0, 512), (1, 512), (8, 128), (9, 128)
A_ROWS = 1280
REG_CONV = (0, 384)
REG_WPT = (0, 128)
MLA_PACKED = ["mla_w_in", "mla_w_uq", "mla_w_ukv"]
SMALL = ["gla_w_gate_up", "gla_b_gate", "gla_norm_g", "conv_w", "ln_g", "ln_b"]
REPLICATED = ["mla_q_norm", "mla_kv_norm"]


def _params(**kw):
    return pltpu.CompilerParams(vmem_limit_bytes=V7X_VMEM_LIMIT_BYTES, **kw)


def _dot(a, b, ca, cb, precision=None):
    return lax.dot_general(a, b, (((ca,), (cb,)), ((), ())), precision=precision, preferred_element_type=F32)


def _nn(a, b):
    return _dot(a.astype(BF16), b.astype(BF16), 1, 0)


def _nt(a, b):
    return _dot(a.astype(BF16), b.astype(BF16), 1, 1)


def _tn(a, b):
    return _dot(a.astype(BF16), b.astype(BF16), 0, 0)


@jax.custom_vjp
def mm_nn(a, b):
    return _nn(a, b)


def _mm_nn_fwd(a, b):
    return _nn(a, b), (a, b)


def _mm_nn_bwd(res, g):
    a, b = res
    return _nt(g, b).astype(a.dtype), _tn(a, g).astype(b.dtype)


mm_nn.defvjp(_mm_nn_fwd, _mm_nn_bwd)


@jax.custom_vjp
def mm_nt(a, b):
    return _nt(a, b)


def _mm_nt_fwd(a, b):
    return _nt(a, b), (a, b)


def _mm_nt_bwd(res, g):
    a, b = res
    return _nn(g, b).astype(a.dtype), _tn(g, a).astype(b.dtype)


mm_nt.defvjp(_mm_nt_fwd, _mm_nt_bwd)


@jax.custom_vjp
def mm_tn(a, b):
    return _tn(a, b)


def _mm_tn_fwd(a, b):
    return _tn(a, b), (a, b)


def _mm_tn_bwd(res, g):
    a, b = res
    return _nt(b, g).astype(a.dtype), _nn(a, g).astype(b.dtype)


mm_tn.defvjp(_mm_tn_fwd, _mm_tn_bwd)


def _iota2(shape, dim):
    return lax.broadcasted_iota(jnp.int32, shape, dim)


@jax.custom_vjp
def cumsum_rows(x):
    n = x.shape[0]
    tri = (_iota2((n, n), 0) >= _iota2((n, n), 1)).astype(F32)
    return _dot(tri, x, 1, 0, precision=HIGHEST)


def _cumsum_fwd(x):
    return cumsum_rows(x), None


def _cumsum_bwd(_, g):
    n = g.shape[0]
    tri_t = (_iota2((n, n), 0) <= _iota2((n, n), 1)).astype(F32)
    return (_dot(tri_t, g, 1, 0, precision=HIGHEST),)


cumsum_rows.defvjp(_cumsum_fwd, _cumsum_bwd)


def _rot_matrix(transposed):
    i, j = _iota2((LANES, LANES), 0), _iota2((LANES, LANES), 1)
    if transposed:
        i, j = j, i
    half = MLA_ROPE // 2
    plus = (i == j - half) & (j >= half) & (j < MLA_ROPE)
    minus = (i == j + half) & (j < half)
    return plus.astype(F32) - minus.astype(F32)


@jax.custom_vjp
def rot_half(x):
    return _dot(x, _rot_matrix(False), 1, 0, precision=HIGHEST)


def _rot_fwd(x):
    return rot_half(x), None


def _rot_bwd(_, g):
    return (_dot(g, _rot_matrix(True), 1, 0, precision=HIGHEST),)


rot_half.defvjp(_rot_fwd, _rot_bwd)


def _shift_rows_raw(x, s):
    n = x.shape[0]
    row = _iota2(x.shape, 0)
    rolled = pltpu.roll(x, s % n, 0)
    keep = (row >= s) if s > 0 else (row < n + s)
    return jnp.where(keep, rolled, 0.0)


@functools.partial(jax.custom_vjp, nondiff_argnums=(1,))
def shift_rows(x, s):
    return _shift_rows_raw(x, s)


def _shift_fwd(x, s):
    return _shift_rows_raw(x, s), None


def _shift_bwd(s, _, g):
    return (_shift_rows_raw(g, -s),)


shift_rows.defvjp(_shift_fwd, _shift_bwd)


def _layer_norm(a, g, b):
    mu = jnp.mean(a, -1, keepdims=True)
    xc = a - mu
    var = jnp.mean(xc * xc, -1, keepdims=True)
    return xc * lax.rsqrt(var + LN_EPS) * g + b


def _rms_norm(a, g):
    return a * lax.rsqrt(jnp.mean(a * a, -1, keepdims=True) + RMS_EPS) * g


def _log_sigmoid(z):
    return jnp.minimum(z, 0.0) - jnp.log(1.0 + jnp.exp(-jnp.abs(z)))


def _matmul(a, b, *, name, ta=False, tb=False, tm=512, tn=512, a_fn=None, epi=None, epi_ins=(), out_dtypes=(BF16,),
            b_at=None, out_at=None, out_buf=None, after=None, n_row_sums=0):
    m = a.shape[1] if ta else a.shape[0]
    k = a.shape[0] if ta else a.shape[1]
    if b_at is None:
        n, kb = (b.shape[0], b.shape[1]) if tb else (b.shape[1], b.shape[0])
    else:
        rb, r = b_at
        n, kb = (N_DEV * r, b.shape[2]) if tb else (b.shape[2], N_DEV * r)
    assert kb == k, (name, a.shape, b.shape, k, kb)
    tm, tn = min(tm, m), min(tn, n)
    assert m % tm == 0 and n % tn == 0, (name, m, n, tm, tn)
    a_spec = pl.BlockSpec((k, tm), lambda i, j: (0, i)) if ta else pl.BlockSpec((tm, k), lambda i, j: (i, 0))
    if b_at is None:
        b_spec = pl.BlockSpec((tn, k), lambda i, j: (j, 0)) if tb else pl.BlockSpec((k, tn), lambda i, j: (0, j))
        load_b = lambda ref: ref[...]
    elif tb and tn == n:
        b_spec = pl.BlockSpec((N_DEV, r, k), lambda i, j: (0, rb, 0))
        load_b = lambda ref: ref[...].reshape(n, k)
    elif tb:
        assert tn == r, (name, tn, r)
        b_spec = pl.BlockSpec((1, r, k), lambda i, j: (j, rb, 0))
        load_b = lambda ref: ref[0]
    else:
        b_spec = pl.BlockSpec((N_DEV, r, tn), lambda i, j: (0, rb, j))
        load_b = lambda ref: ref[...].reshape(k, tn)
    e_specs = []
    for e in epi_ins:
        if e.shape == (1, n):
            e_specs.append(pl.BlockSpec((1, tn), lambda i, j: (0, j)))
        else:
            assert e.shape == (m, n), (name, e.shape, m, n)
            e_specs.append(pl.BlockSpec((tm, tn), lambda i, j: (i, j)))
    n_epi = len(epi_ins)
    ca, cb = (0 if ta else 1), (1 if tb else 0)
    operands = [a, b, *epi_ins]
    in_specs = [a_spec, b_spec, *e_specs]
    if out_at is None:
        assert n_row_sums == 0 or tn == n, (name, tn, n)
        out_specs = [pl.BlockSpec((tm, tn), lambda i, j: (i, j)) for _ in out_dtypes]
        out_specs += [pl.BlockSpec((1, n), lambda i, j: (0, 0))] * n_row_sums
        out_shape = [jax.ShapeDtypeStruct((m, n), dt) for dt in out_dtypes]
        out_shape += [jax.ShapeDtypeStruct((1, n), F32)] * n_row_sums
        aliases, n_buf = {}, 0
    else:
        orb, orows = out_at
        assert len(out_dtypes) == 1 and orows % tm == 0 and m == N_DEV * orows and n == out_buf.shape[2], (name, m, n)
        per = orows // tm
        out_specs = [pl.BlockSpec((1, tm, tn), lambda i, j: (i // per, orb * per + i % per, j))]
        out_shape = [jax.ShapeDtypeStruct(out_buf.shape, out_buf.dtype)]
        operands.append(out_buf)
        in_specs.append(pl.BlockSpec(memory_space=pl.ANY))
        aliases, n_buf = {len(operands) - 1: 0}, 1
    for dep in ([] if after is None else after if isinstance(after, (list, tuple)) else [after]):
        if dep is not None:
            operands.append(dep)
            in_specs.append(pl.BlockSpec(memory_space=pl.ANY))
            n_buf += 1

    def body(a_ref, b_ref, *rest):
        av = a_ref[...]
        if a_fn is not None:
            av = a_fn(av)
        acc = _dot(av.astype(BF16), load_b(b_ref).astype(BF16), ca, cb)
        outs = epi(acc, *[r_[...] for r_ in rest[:n_epi]]) if epi is not None else (acc,)
        o_refs = rest[n_epi + n_buf:]
        n_tiles = len(o_refs) - n_row_sums
        for o_ref, val in zip(o_refs[:n_tiles], outs):
            o_ref[...] = val.astype(o_ref.dtype).reshape(o_ref.shape)
        if n_row_sums:
            @pl.when(pl.program_id(0) == 0)
            def _():
                for o_ref in o_refs[n_tiles:]:
                    o_ref[...] = jnp.zeros_like(o_ref)

            for o_ref, val in zip(o_refs[n_tiles:], outs[n_tiles:]):
                o_ref[...] += val

    outs = pl.pallas_call(
        body, name=name, grid=(m // tm, n // tn), in_specs=in_specs, out_specs=out_specs, out_shape=out_shape,
        input_output_aliases=aliases, compiler_params=_params(),
    )(*operands)
    return outs[0] if len(outs) == 1 else tuple(outs)


def _tile_fwd(f, tiled, params, out_dtypes, *, tm, name):
    t = tiled[0].shape[0]
    assert t % tm == 0
    out_avals = jax.eval_shape(f, *[jax.ShapeDtypeStruct((tm, x.shape[1]), F32) for x in tiled],
                               *[jax.ShapeDtypeStruct(p.shape, F32) for p in params])
    nt, npar = len(tiled), len(params)

    def body(*refs):
        ins = [r[...].astype(F32) for r in refs[:nt + npar]]
        outs = f(*ins)
        for o_ref, val in zip(refs[nt + npar:], outs):
            o_ref[...] = val.astype(o_ref.dtype)

    return pl.pallas_call(
        body, name=name, grid=(t // tm,),
        in_specs=[pl.BlockSpec((tm, x.shape[1]), lambda i: (i, 0)) for x in tiled]
        + [pl.BlockSpec(p.shape, lambda i: (0, 0)) for p in params],
        out_specs=[pl.BlockSpec((tm, o.shape[1]), lambda i: (i, 0)) for o in out_avals],
        out_shape=[jax.ShapeDtypeStruct((t, o.shape[1]), dt) for o, dt in zip(out_avals, out_dtypes)],
        compiler_params=_params(),
    )(*tiled, *params)


def _tile_bwd(f, tiled, params, cots, d_tiled_dtypes, *, tm, name, diff_tiled=None):
    t = tiled[0].shape[0]
    assert t % tm == 0
    nt, npar, nc = len(tiled), len(params), len(cots)
    diff_tiled = list(range(nt)) if diff_tiled is None else diff_tiled

    def body(*refs):
        ins = [r[...].astype(F32) for r in refs[:nt + npar]]
        cts = [r[...].astype(F32) for r in refs[nt + npar:nt + npar + nc]]
        o_refs = refs[nt + npar + nc:]
        _, vjp = jax.vjp(f, *ins)
        grads = vjp(tuple(cts))
        for o_ref, idx in zip(o_refs[:len(diff_tiled)], diff_tiled):
            o_ref[...] = grads[idx].astype(o_ref.dtype)
        p_refs = o_refs[len(diff_tiled):]

        @pl.when(pl.program_id(0) == 0)
        def _():
            for p_ref in p_refs:
                p_ref[...] = jnp.zeros_like(p_ref)

        for p_ref, gp in zip(p_refs, grads[nt:]):
            p_ref[...] += gp

    outs = pl.pallas_call(
        body, name=name, grid=(t // tm,),
        in_specs=[pl.BlockSpec((tm, x.shape[1]), lambda i: (i, 0)) for x in tiled]
        + [pl.BlockSpec(p.shape, lambda i: (0, 0)) for p in params]
        + [pl.BlockSpec((tm, c.shape[1]), lambda i: (i, 0)) for c in cots],
        out_specs=[pl.BlockSpec((tm, tiled[idx].shape[1]), lambda i: (i, 0)) for idx in diff_tiled]
        + [pl.BlockSpec(p.shape, lambda i: (0, 0)) for p in params],
        out_shape=[jax.ShapeDtypeStruct(tiled[idx].shape, dt) for idx, dt in zip(diff_tiled, d_tiled_dtypes)]
        + [jax.ShapeDtypeStruct(p.shape, F32) for p in params],
        compiler_params=_params(),
    )(*tiled, *params, *cots)
    return outs[:len(diff_tiled)], outs[len(diff_tiled):]


def _gla_head(q, k, v, r, z, g, st):
    c = q.shape[0]
    causal = _iota2((c, c), 0) >= _iota2((c, c), 1)
    la = _log_sigmoid(z) * (1.0 / GLA_TAU)
    big_l = cumsum_rows(la)
    ep, en = jnp.exp(big_l), jnp.exp(-big_l)
    qs = q * (GLA_DK ** -0.5)
    qp = qs * ep
    s = jnp.where(causal, mm_nt(qp, k * en), mm_nt(qs * en, k * ep))
    o = mm_nn(s, v) + mm_nt(qp, st)
    l_end = jnp.sum(la, axis=0, keepdims=True)
    st_new = st * jnp.exp(l_end) + mm_tn(v, k * jnp.exp(l_end - big_l))
    u = _rms_norm(o, g) * (r * jax.nn.sigmoid(r))
    return u, st_new


def _gla_slices(h):
    q = slice(GLA_DK * h, GLA_DK * (h + 1))
    k = slice(GLA_HK + GLA_DK * h, GLA_HK + GLA_DK * (h + 1))
    v = slice(2 * GLA_HK + GLA_DV * h, 2 * GLA_HK + GLA_DV * (h + 1))
    r = slice(2 * GLA_HK + GLA_HV + GLA_DV * h, 2 * GLA_HK + GLA_HV + GLA_DV * (h + 1))
    return q, k, v, r


GLA_CHUNKS_PER_STEP = 2


def _gla_fwd(proj, z, norm_g):
    t = proj.shape[0]
    nc, per = t // CHUNK, GLA_CHUNKS_PER_STEP
    rows_per_step = per * CHUNK

    def body(proj_ref, z_ref, g_ref, u_ref, st_save_ref, st_ref):
        @pl.when(pl.program_id(0) == 0)
        def _():
            st_ref[...] = jnp.zeros_like(st_ref)

        g = g_ref[...]
        for h in range(GLA_HEADS):
            sq, sk, sv, sr = _gla_slices(h)
            st = st_ref[h]
            for c in range(per):
                rows = slice(c * CHUNK, (c + 1) * CHUNK)
                st_save_ref[c, h] = st
                u, st = _gla_head(proj_ref[rows, sq].astype(F32), proj_ref[rows, sk].astype(F32),
                                  proj_ref[rows, sv].astype(F32), proj_ref[rows, sr].astype(F32),
                                  z_ref[rows, GLA_DK * h:GLA_DK * (h + 1)], g, st)
                u_ref[rows, GLA_DV * h:GLA_DV * (h + 1)] = u.astype(u_ref.dtype)
            st_ref[h] = st

    return pl.pallas_call(
        body, name="gla_fwd", grid=(nc // per,),
        in_specs=[pl.BlockSpec((rows_per_step, GLA_MAIN), lambda i: (i, 0)),
                  pl.BlockSpec((rows_per_step, GLA_HK), lambda i: (i, 0)), pl.BlockSpec((1, GLA_DV), lambda i: (0, 0))],
        out_specs=[pl.BlockSpec((rows_per_step, GLA_HV), lambda i: (i, 0)),
                   pl.BlockSpec((per, GLA_HEADS, GLA_DV, GLA_DK), lambda i: (i, 0, 0, 0))],
        out_shape=[jax.ShapeDtypeStruct((t, GLA_HV), BF16), jax.ShapeDtypeStruct((nc, GLA_HEADS, GLA_DV, GLA_DK), F32)],
        scratch_shapes=[pltpu.VMEM((GLA_HEADS, GLA_DV, GLA_DK), F32)],
        compiler_params=_params(),
    )(proj, z, norm_g)


def _gla_bwd(proj, z, norm_g, states, du, after):
    t = proj.shape[0]
    nc, per = t // CHUNK, GLA_CHUNKS_PER_STEP
    rows_per_step = per * CHUNK
    n_steps = nc // per
    after = [a for a in after if a is not None]

    def body(proj_ref, z_ref, g_ref, st_in_ref, du_ref, *rest):
        dproj_ref, dz_ref, dg_ref, dst_ref = rest[len(after):]

        @pl.when(pl.program_id(0) == 0)
        def _():
            dst_ref[...] = jnp.zeros_like(dst_ref)
            dg_ref[...] = jnp.zeros_like(dg_ref)

        g = g_ref[...]
        for h in range(GLA_HEADS):
            sq, sk, sv, sr = _gla_slices(h)
            dst = dst_ref[h]
            for c in reversed(range(per)):
                rows = slice(c * CHUNK, (c + 1) * CHUNK)
                ins = (proj_ref[rows, sq].astype(F32), proj_ref[rows, sk].astype(F32), proj_ref[rows, sv].astype(F32),
                       proj_ref[rows, sr].astype(F32), z_ref[rows, GLA_DK * h:GLA_DK * (h + 1)], g, st_in_ref[c, h])
                _, vjp = jax.vjp(_gla_head, *ins)
                dq, dk, dv, dr, dz, dg, dst = vjp((du_ref[rows, GLA_DV * h:GLA_DV * (h + 1)], dst))
                dproj_ref[rows, sq] = dq.astype(dproj_ref.dtype)
                dproj_ref[rows, sk] = dk.astype(dproj_ref.dtype)
                dproj_ref[rows, sv] = dv.astype(dproj_ref.dtype)
                dproj_ref[rows, sr] = dr.astype(dproj_ref.dtype)
                dz_ref[rows, GLA_DK * h:GLA_DK * (h + 1)] = dz
                dg_ref[...] += dg
            dst_ref[h] = dst

    rev = lambda i: (n_steps - 1 - i, 0)
    return pl.pallas_call(
        body, name="gla_bwd", grid=(n_steps,),
        in_specs=[pl.BlockSpec((rows_per_step, GLA_MAIN), rev), pl.BlockSpec((rows_per_step, GLA_HK), rev),
                  pl.BlockSpec((1, GLA_DV), lambda i: (0, 0)),
                  pl.BlockSpec((per, GLA_HEADS, GLA_DV, GLA_DK), lambda i: (n_steps - 1 - i, 0, 0, 0)),
                  pl.BlockSpec((rows_per_step, GLA_HV), rev)] + [pl.BlockSpec(memory_space=pl.ANY)] * len(after),
        out_specs=[pl.BlockSpec((rows_per_step, GLA_MAIN), rev), pl.BlockSpec((rows_per_step, GLA_HK), rev),
                   pl.BlockSpec((1, GLA_DV), lambda i: (0, 0))],
        out_shape=[jax.ShapeDtypeStruct((t, GLA_MAIN), BF16), jax.ShapeDtypeStruct((t, GLA_HK), F32),
                   jax.ShapeDtypeStruct((1, GLA_DV), F32)],
        scratch_shapes=[pltpu.VMEM((GLA_HEADS, GLA_DV, GLA_DK), F32)],
        compiler_params=_params(),
    )(proj, z, norm_g, states, du, *after)


def _mla_pre(cq, cos, sin, gq, gkv, w_uq, w_ukv):
    qlat = _rms_norm(cq[:, :MLA_RANK], gq)
    kvlat = _rms_norm(cq[:, MLA_RANK:2 * MLA_RANK], gkv)
    kr = cq[:, 2 * MLA_RANK:]
    scale = (MLA_NOPE + MLA_ROPE) ** -0.5
    q = mm_nn(qlat, w_uq) * scale
    kv = mm_nn(kvlat, w_ukv)
    n_nope = MLA_HEADS * MLA_NOPE
    ropes = []
    for h in range(MLA_HEADS):
        qr = q[:, n_nope + LANES * h:n_nope + LANES * (h + 1)]
        ropes.append(qr * cos + rot_half(qr) * sin)
    return q[:, :n_nope], jnp.concatenate(ropes, axis=1), kv, kr * cos + rot_half(kr) * sin


MLA_Q_TILE = 256


def _mla_attn_block(qn, qr, kv, kr, q0):
    tq, nk = qn.shape[0], kv.shape[0]
    s = mm_nt(qn, kv[:, :MLA_NOPE]) + mm_nt(qr, kr)
    visible = (_iota2((tq, nk), 1) // CHUNK) <= ((q0 + _iota2((tq, nk), 0)) // CHUNK)
    s = jnp.where(visible, s, -1e30)
    e = jnp.exp(s - jnp.max(s, -1, keepdims=True))
    p = e / jnp.sum(e, -1, keepdims=True)
    return mm_nn(p, kv[:, MLA_NOPE:])


def _mla_attn_fwd(qn, qr, kv, kr):
    t = qn.shape[0]

    def body(qn_ref, qr_ref, kv_ref, kr_ref, o_ref):
        for i in range(t // MLA_Q_TILE):
            rows = slice(i * MLA_Q_TILE, (i + 1) * MLA_Q_TILE)
            keys = slice(0, (i + 1) * MLA_Q_TILE)
            o = _mla_attn_block(qn_ref[rows, :].astype(F32), qr_ref[rows, :].astype(F32), kv_ref[keys, :].astype(F32),
                                kr_ref[keys, :].astype(F32), i * MLA_Q_TILE)
            o_ref[rows, :] = o.astype(o_ref.dtype)

    return pl.pallas_call(
        body, name="mla_attn_fwd", grid=(MLA_HEADS,),
        in_specs=[pl.BlockSpec((t, MLA_NOPE), lambda h: (0, h)), pl.BlockSpec((t, LANES), lambda h: (0, h)),
                  pl.BlockSpec((t, MLA_NOPE + MLA_V), lambda h: (0, h)), pl.BlockSpec((t, LANES), lambda h: (0, 0))],
        out_specs=pl.BlockSpec((t, MLA_V), lambda h: (0, h)),
        out_shape=jax.ShapeDtypeStruct((t, MLA_HEADS * MLA_V), BF16),
        compiler_params=_params(),
    )(qn, qr, kv, kr)


def _mla_attn_bwd(qn, qr, kv, kr, do, after):
    t = qn.shape[0]
    after = [a for a in after if a is not None]

    def body(qn_ref, qr_ref, kv_ref, kr_ref, do_ref, *rest):
        dqn_ref, dqr_ref, dkv_ref, dkr_ref = rest[len(after):]
        dkv_ref[...] = jnp.zeros_like(dkv_ref)

        @pl.when(pl.program_id(0) == 0)
        def _():
            dkr_ref[...] = jnp.zeros_like(dkr_ref)

        for i in range(t // MLA_Q_TILE):
            rows = slice(i * MLA_Q_TILE, (i + 1) * MLA_Q_TILE)
            keys = slice(0, (i + 1) * MLA_Q_TILE)
            f = functools.partial(_mla_attn_block, q0=i * MLA_Q_TILE)
            _, vjp = jax.vjp(f, qn_ref[rows, :].astype(F32), qr_ref[rows, :].astype(F32), kv_ref[keys, :].astype(F32),
                             kr_ref[keys, :].astype(F32))
            dqn, dqr, dkv, dkr = vjp(do_ref[rows, :].astype(F32))
            dqn_ref[rows, :] = dqn
            dqr_ref[rows, :] = dqr
            dkv_ref[keys, :] += dkv
            dkr_ref[keys, :] += dkr

    return pl.pallas_call(
        body, name="mla_attn_bwd", grid=(MLA_HEADS,),
        in_specs=[pl.BlockSpec((t, MLA_NOPE), lambda h: (0, h)), pl.BlockSpec((t, LANES), lambda h: (0, h)),
                  pl.BlockSpec((t, MLA_NOPE + MLA_V), lambda h: (0, h)), pl.BlockSpec((t, LANES), lambda h: (0, 0)),
                  pl.BlockSpec((t, MLA_V), lambda h: (0, h))] + [pl.BlockSpec(memory_space=pl.ANY)] * len(after),
        out_specs=[pl.BlockSpec((t, MLA_NOPE), lambda h: (0, h)), pl.BlockSpec((t, LANES), lambda h: (0, h)),
                   pl.BlockSpec((t, MLA_NOPE + MLA_V), lambda h: (0, h)), pl.BlockSpec((t, LANES), lambda h: (0, 0))],
        out_shape=[jax.ShapeDtypeStruct(qn.shape, F32), jax.ShapeDtypeStruct(qr.shape, F32),
                   jax.ShapeDtypeStruct(kv.shape, F32), jax.ShapeDtypeStruct(kr.shape, F32)],
        compiler_params=_params(),
    )(qn, qr, kv, kr, do, *after)


def _rope_tables(pos_col, inv_freq_row):
    t = pos_col.shape[0]

    def body(pos_ref, f_ref, cos_ref, sin_ref):
        ang = pos_ref[...].astype(F32) * f_ref[...]
        live = _iota2(ang.shape, 1) < MLA_ROPE
        cos_ref[...] = jnp.where(live, jnp.cos(ang), 0.0)
        sin_ref[...] = jnp.where(live, jnp.sin(ang), 0.0)

    return pl.pallas_call(
        body, name="rope_tables", out_shape=[jax.ShapeDtypeStruct((t, LANES), F32)] * 2, compiler_params=_params(),
    )(pos_col, inv_freq_row)


CONV_COL_TILE = 256


def _conv_gate(b, c, u, w0, w1, w2):
    cu = c * u
    return b * (w2 * cu + w1 * shift_rows(cu, 1) + w0 * shift_rows(cu, 2))


def _conv_specs(t):
    nb = D_MODEL // CONV_COL_TILE
    return [pl.BlockSpec((t, CONV_COL_TILE), lambda j, part=part: (0, part * nb + j)) for part in range(3)]


def _conv_fwd(bcu, w):
    t = bcu.shape[0]

    def body(b_ref, c_ref, u_ref, w_ref, o_ref):
        o_ref[...] = _conv_gate(b_ref[...], c_ref[...], u_ref[...], w_ref[0:1, :], w_ref[1:2, :],
                                w_ref[2:3, :]).astype(o_ref.dtype)

    return pl.pallas_call(
        body, name="conv_fwd", grid=(D_MODEL // CONV_COL_TILE,),
        in_specs=_conv_specs(t) + [pl.BlockSpec((3, CONV_COL_TILE), lambda j: (0, j))],
        out_specs=pl.BlockSpec((t, CONV_COL_TILE), lambda j: (0, j)),
        out_shape=jax.ShapeDtypeStruct((t, D_MODEL), BF16), compiler_params=_params(),
    )(bcu, bcu, bcu, w)


def _conv_bwd(bcu, w, dout, after):
    t = bcu.shape[0]
    after = [a for a in after if a is not None]

    def body(b_ref, c_ref, u_ref, w_ref, do_ref, *rest):
        db_ref, dc_ref, du_ref, dw_ref = rest[len(after):]
        _, vjp = jax.vjp(_conv_gate, b_ref[...], c_ref[...], u_ref[...], w_ref[0:1, :], w_ref[1:2, :], w_ref[2:3, :])
        db, dc, du, dw0, dw1, dw2 = vjp(do_ref[...])
        db_ref[...] = db.astype(db_ref.dtype)
        dc_ref[...] = dc.astype(dc_ref.dtype)
        du_ref[...] = du.astype(du_ref.dtype)
        dw_ref[0:1, :] = dw0
        dw_ref[1:2, :] = dw1
        dw_ref[2:3, :] = dw2

    col = pl.BlockSpec((t, CONV_COL_TILE), lambda j: (0, j))
    return pl.pallas_call(
        body, name="conv_bwd", grid=(D_MODEL // CONV_COL_TILE,),
        in_specs=_conv_specs(t) + [pl.BlockSpec((3, CONV_COL_TILE), lambda j: (0, j)), col]
        + [pl.BlockSpec(memory_space=pl.ANY)] * len(after),
        out_specs=[col, col, col, pl.BlockSpec((3, CONV_COL_TILE), lambda j: (0, j))],
        out_shape=[jax.ShapeDtypeStruct((t, D_MODEL), BF16)] * 3 + [jax.ShapeDtypeStruct((3, D_MODEL), F32)],
        compiler_params=_params(),
    )(bcu, bcu, bcu, w, dout, *after)


def _loss_head(y, target):
    t, d = y.shape
    tm = 256

    def body(y_ref, t_ref, loss_ref, dy_ref):
        @pl.when(pl.program_id(0) == 0)
        def _():
            loss_ref[...] = jnp.zeros_like(loss_ref)

        err = y_ref[...] - t_ref[...]
        dy_ref[...] = err * (1.0 / d)
        loss_ref[...] += 0.5 * jnp.sum(jnp.sum(err * err, axis=-1, keepdims=True) * (1.0 / d))

    tile = pl.BlockSpec((tm, d), lambda i: (i, 0))
    return pl.pallas_call(
        body, name="loss_head", grid=(t // tm,), in_specs=[tile, tile],
        out_specs=[pl.BlockSpec((8, LANES), lambda i: (0, 0)), tile],
        out_shape=[jax.ShapeDtypeStruct((8, LANES), F32), jax.ShapeDtypeStruct((t, d), F32)],
        compiler_params=_params(),
    )(y, target)


def _ln_epi(acc, res, g, b):
    a = ALPHA * res + acc
    return a, _layer_norm(a, g, b)


def _ln_fn(a, g, b):
    return (_layer_norm(a, g, b),)


def _ln_bwd_epi(scale):
    def epi(acc, res, a, g, b):
        _, vjp = jax.vjp(_ln_fn, a, g, b)
        return vjp((acc + scale * res,))
    return epi


def _relu_sq(h):
    r = jnp.maximum(h.astype(F32), 0.0)
    return r * r


def _pad_cols(w, n):
    return jnp.pad(w, ((0, 0), (0, n - w.shape[1])))


def _pad_rows(w, n):
    return jnp.pad(w, ((0, n - w.shape[0]), (0, 0)))


def _uq_to_kernel_layout(w_uq):
    w = w_uq.reshape(MLA_RANK, MLA_HEADS, MLA_NOPE + MLA_ROPE)
    nope = w[:, :, :MLA_NOPE].reshape(MLA_RANK, MLA_HEADS * MLA_NOPE)
    rope = jnp.pad(w[:, :, MLA_NOPE:], ((0, 0), (0, 0), (0, LANES - MLA_ROPE))).reshape(MLA_RANK, MLA_HEADS * LANES)
    return jnp.concatenate([nope, rope], axis=1)


def _uq_from_kernel_layout(w):
    nope = w[:, :MLA_HEADS * MLA_NOPE].reshape(MLA_RANK, MLA_HEADS, MLA_NOPE)
    rope = w[:, MLA_HEADS * MLA_NOPE:].reshape(MLA_RANK, MLA_HEADS, LANES)[:, :, :MLA_ROPE]
    return jnp.concatenate([nope, rope], axis=2).reshape(MLA_RANK, MLA_HEADS * (MLA_NOPE + MLA_ROPE))


def _step(x, p, positions, target, small, comm):
    t = x.shape[0]
    w = small
    freqs = ROPE_BASE ** (-jnp.arange(0, MLA_ROPE // 2, dtype=F32) * (2.0 / MLA_ROPE))
    freq_row = jnp.concatenate([freqs, freqs, jnp.zeros((LANES - MLA_ROPE,), F32)])[None, :]
    cos, sin = _rope_tables(positions.reshape(t, 1), freq_row)

    saved = []
    for i in range(DEPTH):
        j, kind = i // 3, i % 3
        wl = comm.mixer_weights(i)
        s = {"x": x, "wl": wl}
        tok = comm.at("fwd", i, "begin", x)
        if kind == 0:
            w_in = wl["gla_w_in"]
            s["w_main"] = w_in[:, :GLA_MAIN]
            s["w_lr"] = _pad_cols(w_in[:, GLA_MAIN:], LANES)
            s["w_up"] = _pad_rows(w["gla_w_gate_up"][j], LANES).astype(BF16)
            s["proj"] = _matmul(x, s["w_main"], name="gla_proj", tn=1024, after=tok)
            s["glr"] = _matmul(x, s["w_lr"], name="gla_lr", out_dtypes=(F32,))
            s["z"] = _matmul(s["glr"], s["w_up"], name="gla_gate", epi=lambda acc, b: (acc + b,),
                             epi_ins=(w["gla_b_gate"][j][None, :],), out_dtypes=(F32,))
            s["u"], s["states"] = _gla_fwd(s["proj"], s["z"], w["gla_norm_g"][j][None, :])
        elif kind == 1:
            s["w_in"] = _pad_cols(wl["mla_w_in"], MLA_IN_PAD)
            s["w_uq"] = _uq_to_kernel_layout(wl["mla_w_uq"])
            s["cq"] = _matmul(x, s["w_in"], name="mla_proj", tn=MLA_IN_PAD, out_dtypes=(F32,), after=tok)
            s["pre_params"] = (w["mla_q_norm"][j][None, :], w["mla_kv_norm"][j][None, :], s["w_uq"], wl["mla_w_ukv"])
            s["qn"], s["qr"], s["kv"], s["kr"] = _tile_fwd(_mla_pre, (s["cq"], cos, sin), s["pre_params"],
                                                           (BF16, BF16, BF16, BF16), tm=256, name="mla_pre_fwd")
            s["u"] = _mla_attn_fwd(s["qn"], s["qr"], s["kv"], s["kr"])
        else:
            s["bcu"] = _matmul(x, wl["conv"], name="conv_proj", tb=True, tm=256, tn=3 * D_MODEL, b_at=REG_CONV,
                               out_dtypes=(F32,), after=tok)
            s["u"] = _conv_fwd(s["bcu"], w["conv_w"][j])
        g0, b0 = w["ln_g"][i, 0][None, :], w["ln_b"][i, 0][None, :]
        g1, b1 = w["ln_g"][i, 1][None, :], w["ln_b"][i, 1][None, :]
        wa, wb = s["wa"], _ = comm.slab_weights(i, s["u"])
        s["a1"], s["x1"] = _matmul(s["u"], wa, name="mixer_out_ln", tm=256, tn=D_MODEL, b_at=REG_WOUT, epi=_ln_epi,
                                   epi_ins=(x, g0, b0), out_dtypes=(F32, F32))
        s["hh"] = _matmul(s["x1"], wa, name="mlp_up", tb=True, tm=256, tn=D_FF, b_at=REG_W1T)
        tok = comm.at("fwd", i, "mid", s["hh"])
        s["a2"], s["x2"] = _matmul(s["hh"], wa, name="mlp_down_ln", tm=256, tn=D_MODEL, b_at=REG_W2, a_fn=_relu_sq,
                                   epi=_ln_epi, epi_ins=(s["x1"], g1, b1), out_dtypes=(F32, F32), after=tok)
        s["pp"] = _matmul(p[i], wb, name="ple_proj", tb=True, tn=D_MODEL, b_at=REG_WPT)
        tok = comm.at("fwd", i, "end", s["pp"])
        x, s["gt"] = _matmul(s["x2"], wa, name="ple_gate", tn=1024, b_at=REG_WG,
                             epi=lambda acc, xr, pp: (xr + jax.nn.sigmoid(acc) * pp.astype(F32), acc),
                             epi_ins=(s["x2"], s["pp"]), out_dtypes=(F32, BF16), after=tok)
        saved.append(s)

    loss_part, dx = _loss_head(x, target)

    gw = {n: [None] * WEIGHTS[n][0][0] for n in SMALL + REPLICATED}
    ln_g_grads, ln_b_grads = [[None, None] for _ in range(DEPTH)], [[None, None] for _ in range(DEPTH)]
    resid = lambda acc, r: (acc + ALPHA * r,)
    plus = lambda acc, r: (acc + r,)
    for i in reversed(range(DEPTH)):
        j, kind = i // 3, i % 3
        s = saved[i]
        wa = s["wa"]
        ga = lax.empty((N_DEV, A_ROWS, D_MODEL), BF16)
        gb = lax.empty((N_DEV, REG_WPT[1], PLE_DIM), BF16)
        layer_grads = {}
        tok = comm.at("bwd", i, "begin", dx)

        def ple_bwd(dxo, gt, pp):
            sg = jax.nn.sigmoid(gt)
            return dxo * sg, dxo * pp * sg * (1.0 - sg)

        d_pp, d_gt = _tile_fwd(ple_bwd, (dx, s["gt"], s["pp"]), (), (BF16, BF16), tm=256, name="ple_bwd")
        gb = _matmul(d_pp, p[i], name="ple_proj_dw", ta=True, tm=REG_WPT[1], tn=PLE_DIM, out_at=REG_WPT, out_buf=gb, after=tok)
        ga = _matmul(s["x2"], d_gt, name="ple_gate_dw", ta=True, tm=REG_WG[1], tn=1024, out_at=REG_WG, out_buf=ga)
        g1, b1 = w["ln_g"][i, 1][None, :], w["ln_b"][i, 1][None, :]
        d_a2, ln_g_grads[i][1], ln_b_grads[i][1] = _matmul(
            d_gt, wa, name="ple_gate_dx_ln", tb=True, tm=256, tn=D_MODEL, b_at=REG_WG, epi=_ln_bwd_epi(1.0),
            epi_ins=(dx, s["a2"], g1, b1), out_dtypes=(F32,), n_row_sums=2, after=[ga, gb])
        tok = comm.at("bwd", i, "ln", d_a2)
        ga = _matmul(s["hh"], d_a2, name="mlp_down_dw", ta=True, tm=REG_W2[1], tn=1024, a_fn=_relu_sq, out_at=REG_W2,
                     out_buf=ga, after=tok)
        d_hh = _matmul(d_a2, wa, name="mlp_down_dx", tb=True, tm=256, tn=D_FF, b_at=REG_W2, after=ga,
                       epi=lambda acc, hh: (acc * 2.0 * jnp.maximum(hh.astype(F32), 0.0),), epi_ins=(s["hh"],))
        ga = _matmul(d_hh, s["x1"], name="mlp_up_dw", ta=True, tm=REG_W1T[1], tn=1024, out_at=REG_W1T, out_buf=ga)
        g0, b0 = w["ln_g"][i, 0][None, :], w["ln_b"][i, 0][None, :]
        d_a1, ln_g_grads[i][0], ln_b_grads[i][0] = _matmul(
            d_hh, wa, name="mlp_up_dx_ln", tm=256, tn=D_MODEL, b_at=REG_W1T, epi=_ln_bwd_epi(ALPHA),
            epi_ins=(d_a2, s["a1"], g0, b0), out_dtypes=(F32,), n_row_sums=2, after=ga)
        ga = _matmul(s["u"], d_a1, name="mixer_out_dw", ta=True, tm=REG_WOUT[1], tn=1024, out_at=REG_WOUT, out_buf=ga)
        du = _matmul(d_a1, wa, name="mixer_out_dx", tb=True, tn=1024, b_at=REG_WOUT, out_dtypes=(F32,), after=ga)
        comm.slab_grads(i, ga, gb)
        tok = comm.at("bwd", i, "slab_done", du) or []
        if kind == 0:
            dproj, dz, dg = _gla_bwd(s["proj"], s["z"], w["gla_norm_g"][j][None, :], s["states"], du, tok)
            tok = comm.at("bwd", i, "mixer_done", dproj)
            gw["gla_norm_g"][j] = dg[0]
            gw["gla_b_gate"][j] = _tile_bwd(lambda zz, b: (zz + b,), (s["z"],), (w["gla_b_gate"][j][None, :],), (dz,), (),
                                            tm=256, name="gla_bias_bwd", diff_tiled=[])[1][0][0]
            gw["gla_w_gate_up"][j] = _matmul(s["glr"], dz, name="gla_gate_dw", ta=True, out_dtypes=(F32,),
                                             after=tok)[:GLA_RANK]
            dglr = _matmul(dz, s["w_up"], name="gla_gate_dx", tb=True, out_dtypes=(F32,))
            dw_main = _matmul(s["x"], dproj, name="gla_proj_dw", ta=True, tn=1024, out_dtypes=(F32,))
            dw_lr = _matmul(s["x"], dglr, name="gla_lr_dw", ta=True, out_dtypes=(F32,))[:, :GLA_RANK]
            layer_grads["gla_w_in"] = jnp.concatenate([dw_main, dw_lr], axis=1)
            dx = _matmul(dproj, s["w_main"], name="gla_proj_dx", tb=True, tn=1024, epi=resid, epi_ins=(d_a1,),
                         out_dtypes=(F32,), after=[dw_main, dw_lr, gw["gla_w_gate_up"][j]])
            dx = _matmul(dglr, s["w_lr"], name="gla_lr_dx", tb=True, tn=1024, epi=plus, epi_ins=(dx,), out_dtypes=(F32,))
        elif kind == 1:
            dqn, dqr, dkv, dkr = _mla_attn_bwd(s["qn"], s["qr"], s["kv"], s["kr"], du, tok)
            tok = comm.at("bwd", i, "mixer_done", dqn)
            (d_cq,), (dgq, dgkv, dw_uq, dw_ukv) = _tile_bwd(_mla_pre, (s["cq"], cos, sin), s["pre_params"],
                                                           (dqn, dqr, dkv, dkr), (BF16,), tm=256, name="mla_pre_bwd",
                                                           diff_tiled=[0])
            gw["mla_q_norm"][j], gw["mla_kv_norm"][j] = dgq[0], dgkv[0]
            layer_grads["mla_w_uq"] = _uq_from_kernel_layout(dw_uq)
            layer_grads["mla_w_ukv"] = dw_ukv
            layer_grads["mla_w_in"] = _matmul(s["x"], d_cq, name="mla_proj_dw", ta=True, tn=MLA_IN_PAD,
                                              out_dtypes=(F32,), after=tok)[:, :MLA_IN]
            dx = _matmul(d_cq, s["w_in"], name="mla_proj_dx", tb=True, tn=1024, epi=resid, epi_ins=(d_a1,),
                         out_dtypes=(F32,), after=layer_grads["mla_w_in"])
        else:
            db, dc, du_, dcw = _conv_bwd(s["bcu"], w["conv_w"][j], du, tok)
            tok = comm.at("bwd", i, "mixer_done", db)
            gw["conv_w"][j] = dcw
            dbcu = jnp.concatenate([db, dc, du_], axis=1)
            layer_grads["conv"] = _matmul(dbcu, s["x"], name="conv_proj_dw", ta=True, tm=REG_CONV[1], tn=1024,
                                          out_at=REG_CONV, out_buf=lax.empty((N_DEV, REG_CONV[1], D_MODEL), BF16),
                                          after=tok)
            dx = _matmul(dbcu, s["wl"]["conv"], name="conv_proj_dx", tn=1024, b_at=REG_CONV, epi=resid, epi_ins=(d_a1,),
                         out_dtypes=(F32,), after=layer_grads["conv"])
        comm.mixer_grads(i, layer_grads)

    gw["ln_g"] = [jnp.concatenate([a, b], axis=0) for a, b in ln_g_grads]
    gw["ln_b"] = [jnp.concatenate([a, b], axis=0) for a, b in ln_b_grads]
    return loss_part, dx, {n: jnp.stack(gw[n]).astype(F32) for n in gw}


MESH_IDS = pl.DeviceIdType.MESH
ANY = pl.BlockSpec(memory_space=pl.ANY)
HBM_SPEC = pl.BlockSpec(memory_space=pltpu.HBM)
SEM_SPEC = pl.BlockSpec(memory_space=pltpu.SEMAPHORE)
DATAFLOW_EFFECT = pltpu.SideEffectType.DATAFLOW_SIDE_EFFECTING
CORE_COPIES, CHIP_COPIES = 4, 3


def _my_place():
    return lax.axis_index("x"), lax.axis_index("y"), lax.axis_index("c")


def _other_chips(mx, my):
    return [(1 - mx, my), (mx, 1 - my), (1 - mx, 1 - my)]


def _remote(src, dst, send_sems, recv_sems, k, to):
    return pltpu.make_async_remote_copy(src_ref=src, dst_ref=dst, send_sem=send_sems.at[k], recv_sem=recv_sems.at[k],
                                        device_id=to, device_id_type=MESH_IDS)


def _gather_first_copies(n_arr):
    def make(bufs, send_sems, recv_sems):
        mx, my, mc = _my_place()
        mine = 4 * mx + 2 * my + mc
        peers = [(mx, my, 1 - mc)] + [(cx, cy, mc) for cx, cy in _other_chips(mx, my)]
        return [_remote(bufs[a].at[mine], bufs[a].at[mine], send_sems, recv_sems, (1 + CHIP_COPIES) * a + k, to)
                for a in range(n_arr) for k, to in enumerate(peers)]
    return make, (1 + CHIP_COPIES) * n_arr


def _gather_forward_copies(n_arr):
    def make(bufs, send_sems, recv_sems):
        mx, my, mc = _my_place()
        blocks = [4 * cx + 2 * cy + mc for cx, cy in _other_chips(mx, my)]
        return [_remote(bufs[a].at[blk], bufs[a].at[blk], send_sems, recv_sems, CHIP_COPIES * a + k, (mx, my, 1 - mc))
                for a in range(n_arr) for k, blk in enumerate(blocks)]
    return make, CHIP_COPIES * n_arr


def _scatter_core_copies(n_arr):
    def make(bufs, send_sems, recv_sems):
        mx, my, mc = _my_place()
        return [_remote(bufs[a].at[2 * k + (1 - mc)], bufs[n_arr + a].at[k], send_sems, recv_sems, CORE_COPIES * a + k,
                        (mx, my, 1 - mc)) for a in range(n_arr) for k in range(CORE_COPIES)]
    return make, CORE_COPIES * n_arr


def _scatter_chip_copies(n_arr):
    def make(bufs, send_sems, recv_sems):
        mx, my, mc = _my_place()
        return [_remote(bufs[a].at[2 * cx + cy], bufs[n_arr + a].at[k], send_sems, recv_sems, CHIP_COPIES * a + k,
                        (cx, cy, mc)) for a in range(n_arr) for k, (cx, cy) in enumerate(_other_chips(mx, my))]
    return make, CHIP_COPIES * n_arr


def _exchange(name, bufs, copies):
    make, n_copies = copies
    n = len(bufs)

    def body(*refs):
        descs = make(refs[:n], refs[2 * n], refs[2 * n + 1])
        for cp in descs:
            cp.start()
        for cp in descs:
            cp.wait()

    return pl.pallas_call(
        body, name=name, out_shape=[jax.ShapeDtypeStruct(b.shape, b.dtype) for b in bufs], in_specs=[ANY] * n,
        out_specs=[ANY] * n, input_output_aliases={i: i for i in range(n)},
        scratch_shapes=[pltpu.SemaphoreType.DMA((n_copies,)), pltpu.SemaphoreType.DMA((n_copies,))],
    )(*bufs)


def _exchange_start(name, bufs, copies, after):
    make, n_copies = copies
    n = len(bufs)

    def body(*refs):
        for cp in make(refs[:n], refs[n + 1], refs[n + 2]):
            cp.start()
        refs[-1][...] = jnp.zeros_like(refs[-1])

    outs = pl.pallas_call(
        body, name=name,
        out_shape=(pltpu.SemaphoreType.DMA((n_copies,)), pltpu.SemaphoreType.DMA((n_copies,)),
                   *[pltpu.HBM(b.shape, b.dtype) for b in bufs], jax.ShapeDtypeStruct((8, LANES), F32)),
        in_specs=[HBM_SPEC] * n + [ANY],
        out_specs=(SEM_SPEC, SEM_SPEC, *[HBM_SPEC] * n, pl.BlockSpec(memory_space=pltpu.VMEM)),
        input_output_aliases={i: 2 + i for i in range(n)},
        compiler_params=pltpu.CompilerParams(has_side_effects=DATAFLOW_EFFECT),
    )(*[pltpu.with_memory_space_constraint(b, pltpu.HBM) for b in bufs], after)
    return (outs[0], outs[1]), list(outs[2:2 + n]), outs[-1]


def _exchange_wait(name, sems, bufs, copies, after):
    make, _ = copies
    n = len(bufs)

    def body(*refs):
        for cp in make(refs[:n], refs[n], refs[n + 1]):
            cp.wait_send()
            cp.wait_recv()

    return list(pl.pallas_call(
        body, name=name, out_shape=[pltpu.HBM(b.shape, b.dtype) for b in bufs],
        in_specs=[HBM_SPEC] * n + [SEM_SPEC, SEM_SPEC, ANY], out_specs=[HBM_SPEC] * n,
        input_output_aliases={i: i for i in range(n)},
        compiler_params=pltpu.CompilerParams(has_side_effects=DATAFLOW_EFFECT),
    )(*bufs, *sems, after))


SUM_TILE_BYTES = 2 * 1024 * 1024


def _row_tile(r, c):
    best = None
    for cand in range(16, r + 1, 16):
        if r % cand == 0 and cand * c * 2 <= SUM_TILE_BYTES:
            best = cand
    return r if best is None else best


def _pair_sum(g, recv, my_c):
    _, r, c = g.shape
    tr = _row_tile(r, c)

    def body(c_ref, g_ref, r_ref, o_ref):
        o_ref[...] = (g_ref[...].astype(F32) + r_ref[...].astype(F32)).astype(o_ref.dtype)

    return pl.pallas_call(
        body, name="rs_pair_sum", out_shape=jax.ShapeDtypeStruct((4, r, c), g.dtype),
        grid_spec=pltpu.PrefetchScalarGridSpec(
            num_scalar_prefetch=1, grid=(4, r // tr),
            in_specs=[pl.BlockSpec((1, tr, c), lambda n, i, cr: (2 * n + cr[0], i, 0)),
                      pl.BlockSpec((1, tr, c), lambda n, i, cr: (n, i, 0))],
            out_specs=pl.BlockSpec((1, tr, c), lambda n, i, cr: (n, i, 0))),
        compiler_params=_params(),
    )(my_c, g, recv)


def _chip_sum(h, recv, my_chip):
    _, r, c = h.shape
    tr = _row_tile(r, c)

    def body(j_ref, h_ref, r0_ref, r1_ref, r2_ref, o_ref):
        o_ref[...] = ((h_ref[0].astype(F32) + r0_ref[0].astype(F32)) + r1_ref[0].astype(F32)) + r2_ref[0].astype(F32)

    return pl.pallas_call(
        body, name="rs_chip_sum", out_shape=jax.ShapeDtypeStruct((r, c), F32),
        grid_spec=pltpu.PrefetchScalarGridSpec(
            num_scalar_prefetch=1, grid=(r // tr,),
            in_specs=[pl.BlockSpec((1, tr, c), lambda i, jr: (jr[0], i, 0))]
            + [pl.BlockSpec((1, tr, c), lambda i, jr, n=n: (n, i, 0)) for n in range(3)],
            out_specs=pl.BlockSpec((tr, c), lambda i, jr: (i, 0))),
        compiler_params=_params(),
    )(my_chip, h, recv, recv, recv)


def _sum_blocks(g):
    n, r, c = g.shape

    def body(g_ref, o_ref):
        acc = g_ref[0]
        for k in range(1, n):
            acc = acc + g_ref[k]
        o_ref[...] = acc

    return pl.pallas_call(body, name="sum_blocks", out_shape=jax.ShapeDtypeStruct((r, c), F32), compiler_params=_params())(g)


def _pack(flat_parts, cols, row_multiple, dtype):
    flat = jnp.concatenate([f.astype(dtype) for f in flat_parts])
    per_row_block = cols * row_multiple
    padded = -(-flat.shape[0] // per_row_block) * per_row_block
    return jnp.pad(flat, (0, padded - flat.shape[0])).reshape(padded // cols, cols)


def _shard_shape(name):
    shape, axis = WEIGHTS[name]
    if axis is None:
        return shape
    return tuple(s // N_DEV if a == axis else s for a, s in enumerate(shape))


def _size(shape):
    n = 1
    for s in shape:
        n *= s
    return n


def _unshard(blocks, name):
    _, axis = WEIGHTS[name]
    return jnp.concatenate([blocks[k] for k in range(N_DEV)], axis=axis)


def _unpack_blocks(flat, names, lead):
    out, off = {}, 0
    for n in names:
        shp = _shard_shape(n)[1:] if lead else _shard_shape(n)
        out[n] = flat[..., off:off + _size(shp)].reshape(flat.shape[:-1] + shp)
        off += _size(shp)
    return out


def _layer_slabs(shard, i):
    j, kind = i // 3, i % 3
    w_out = (shard["gla_w_out"], shard["mla_w_out"], shard["conv_w_out"])[kind][j]
    out = {"a": jnp.concatenate([shard["mlp_w2"][i], shard["mlp_w1"][i].T, w_out, shard["ple_w_gate"][i]], axis=0).astype(BF16),
           "b": shard["ple_w_proj"][i].T.astype(BF16)}
    if kind == 0:
        out["gla"] = shard["gla_w_in"][j].astype(BF16)
    elif kind == 1:
        out["mla"] = _pack([shard[n][j].reshape(-1) for n in MLA_PACKED], PACK_COLS, PACK_ROW_TILE, BF16)
    else:
        out["conv"] = shard["conv_w_in"][j].T.astype(BF16)
    return out


def _mixer_weights(landed, i):
    kind = i % 3
    if kind == 0:
        return {"gla_w_in": jnp.concatenate([landed["gla"][k] for k in range(N_DEV)], axis=1)}
    if kind == 2:
        return {"conv": landed["conv"]}
    blocks = _unpack_blocks(landed["mla"].reshape(N_DEV, -1), MLA_PACKED, lead=True)
    return {n: jnp.concatenate([blocks[n][k] for k in range(N_DEV)], axis=WEIGHTS[n][1] - 1) for n in MLA_PACKED}


def _mixer_grad_buffers(layer_grads, i):
    kind = i % 3
    if kind == 0:
        return {"gla": jnp.stack(jnp.split(layer_grads["gla_w_in"], N_DEV, axis=1)).astype(BF16)}
    if kind == 2:
        return {"conv": layer_grads["conv"]}
    parts = [jnp.stack(jnp.split(layer_grads[n], N_DEV, axis=WEIGHTS[n][1] - 1)).reshape(N_DEV, -1) for n in MLA_PACKED]
    cat = jnp.concatenate(parts, axis=1).astype(BF16)
    per = PACK_COLS * PACK_ROW_TILE
    padded = -(-cat.shape[1] // per) * per
    return {"mla": jnp.pad(cat, ((0, 0), (0, padded - cat.shape[1]))).reshape(N_DEV, padded // PACK_COLS, PACK_COLS)}


SLAB_KEYS = ("a", "b")


class _Overlap:
    def __init__(self, shard, small_pack):
        mx, my, mc = _my_place()
        self.my_c = mc.astype(jnp.int32).reshape(1)
        self.my_chip = (2 * mx + my).astype(jnp.int32).reshape(1)
        mine = 4 * mx + 2 * my + mc
        slabs = [_layer_slabs(shard, i) for i in range(DEPTH)]
        slabs[0]["small"] = small_pack
        self.landing = [{k: lax.dynamic_update_index_in_dim(lax.empty((N_DEV, *v.shape), v.dtype), v, mine, 0)
                         for k, v in slabs[i].items()} for i in range(DEPTH)]
        self.fly = {}
        self.gather_keys = {}
        self.grads = [{} for _ in range(DEPTH)]
        self.reduced = [{} for _ in range(DEPTH)]
        keys = self._keys(self.landing[0], "mixer")
        tok = self._start("ag_first_mixer_l0", [self.landing[0][k] for k in keys], _gather_first_copies(len(keys)), shard["ln_g"])
        tok = self._start("ag_first_slab_l0", [self.landing[0][k] for k in SLAB_KEYS], _gather_first_copies(2), tok)
        bufs = self._wait("ag_first_mixer_l0", tok)
        self.landing[0].update(zip(keys, _exchange("ag_forward_mixer_l0", bufs, _gather_forward_copies(len(bufs)))))

    @staticmethod
    def _keys(names, group):
        return [k for k in names if (k in SLAB_KEYS) == (group == "slab")]

    def _start(self, name, bufs, copies, after):
        sems, bufs, tok = _exchange_start(name + "_start", bufs, copies, after)
        self.fly[name] = (sems, bufs, copies)
        return tok

    def _wait(self, name, after):
        sems, bufs, copies = self.fly.pop(name)
        return _exchange_wait(name + "_wait", sems, bufs, copies, after)

    def mixer_weights(self, i):
        return _mixer_weights(self.landing[i], i)

    def slab_weights(self, i, dep):
        if i == 0:
            bufs = self._wait("ag_first_slab_l0", dep)
            self.landing[0].update(zip(SLAB_KEYS, _exchange("ag_forward_slab_l0", bufs, _gather_forward_copies(2))))
        return self.landing[i]["a"], self.landing[i]["b"]

    def slab_grads(self, i, ga, gb):
        self.grads[i].update(a=ga, b=gb)

    def mixer_grads(self, i, layer_grads):
        self.grads[i].update(_mixer_grad_buffers(layer_grads, i))

    def at(self, phase, i, point, dep):
        toks = []
        if phase == "fwd":
            if point == "begin" and i == 0:
                toks.append(self._gather_first(1, self.landing[0][self._keys(self.landing[0], "mixer")[0]]))
            if point == "mid" and i + 1 < DEPTH:
                bufs = self._wait(f"ag_first_l{i + 1}", dep)
                toks.append(self._start(f"ag_forward_l{i + 1}", bufs, _gather_forward_copies(len(bufs)), dep))
                if i + 2 < DEPTH:
                    toks.append(self._gather_first(i + 2, dep))
            if point == "end" and i + 1 < DEPTH:
                self.landing[i + 1].update(zip(self.gather_keys[i + 1], self._wait(f"ag_forward_l{i + 1}", dep)))
        else:
            if point == "begin" and i + 1 < DEPTH:
                toks.append(self._scatter_cores(i + 1, "mixer", dep))
            if point == "ln" and i + 1 < DEPTH:
                toks.append(self._scatter_chips(i + 1, "mixer", dep))
            if point == "slab_done":
                if i + 1 < DEPTH:
                    self._scatter_done(i + 1, "slab", dep)
                    self._scatter_done(i + 1, "mixer", dep)
                toks.append(self._scatter_cores(i, "slab", dep))
            if point == "mixer_done":
                toks.append(self._scatter_chips(i, "slab", dep))
        return toks or None

    def _gather_first(self, i, after):
        self.gather_keys[i] = list(self.landing[i])
        bufs = [self.landing[i][k] for k in self.gather_keys[i]]
        return self._start(f"ag_first_l{i}", bufs, _gather_first_copies(len(bufs)), after)

    def _scatter_cores(self, i, group, after):
        gs = [self.grads[i][k] for k in self._keys(self.grads[i], group)]
        land = [lax.empty((4, *g.shape[1:]), g.dtype) for g in gs]
        return self._start(f"rs_cores_{group}_l{i}", gs + land, _scatter_core_copies(len(gs)), after)

    def _pair_sums(self, bufs):
        n = len(bufs) // 2
        hs = [_pair_sum(g, r, self.my_c) for g, r in zip(bufs[:n], bufs[n:])]
        return hs + [lax.empty((3, *h.shape[1:]), h.dtype) for h in hs]

    def _scatter_chips(self, i, group, after):
        bufs = self._pair_sums(self._wait(f"rs_cores_{group}_l{i}", after))
        return self._start(f"rs_chips_{group}_l{i}", bufs, _scatter_chip_copies(len(bufs) // 2), after)

    def _chip_sums(self, i, group, bufs):
        n = len(bufs) // 2
        for k, h, r in zip(self._keys(self.grads[i], group), bufs[:n], bufs[n:]):
            self.reduced[i][k] = _chip_sum(h, r, self.my_chip)

    def _scatter_done(self, i, group, after):
        self._chip_sums(i, group, self._wait(f"rs_chips_{group}_l{i}", after))

    def finish(self, dep):
        self._scatter_done(0, "slab", dep)
        gs = [self.grads[0][k] for k in self._keys(self.grads[0], "mixer")]
        bufs = _exchange("rs_cores_mixer_l0", gs + [lax.empty((4, *g.shape[1:]), g.dtype) for g in gs],
                         _scatter_core_copies(len(gs)))
        bufs = _exchange("rs_chips_mixer_l0", self._pair_sums(bufs), _scatter_chip_copies(len(gs)))
        self._chip_sums(0, "mixer", bufs)
        return self.reduced


def _all_gather_small(x, name):
    mx, my, mc = _my_place()
    land = lax.dynamic_update_index_in_dim(lax.empty((N_DEV, *x.shape), x.dtype), x, 4 * mx + 2 * my + mc, 0)
    (land,) = _exchange(name + "_first", [land], _gather_first_copies(1))
    (land,) = _exchange(name + "_forward", [land], _gather_forward_copies(1))
    return land


def _shard_grads(reduced):
    def rows(a, reg):
        return a[reg[0] * reg[1]:(reg[0] + 1) * reg[1]]

    a = [reduced[i]["a"] for i in range(DEPTH)]
    w_out = [rows(a[i], REG_WOUT) for i in range(DEPTH)]
    mla = _unpack_blocks(reduced[1]["mla"].reshape(-1), MLA_PACKED, lead=True)
    out = {
        "mlp_w2": jnp.stack([rows(a[i], REG_W2) for i in range(DEPTH)]),
        "mlp_w1": jnp.stack([rows(a[i], REG_W1T).T for i in range(DEPTH)]),
        "gla_w_out": jnp.stack([w_out[0], w_out[3]]), "mla_w_out": w_out[1][None], "conv_w_out": w_out[2][None],
        "ple_w_gate": jnp.stack([rows(a[i], REG_WG) for i in range(DEPTH)]),
        "ple_w_proj": jnp.stack([reduced[i]["b"].T for i in range(DEPTH)]),
        "conv_w_in": reduced[2]["conv"].T[None],
        "gla_w_in": jnp.stack([reduced[0]["gla"], reduced[3]["gla"]]),
    }
    out.update({n: mla[n][None] for n in MLA_PACKED})
    return out


def _adamw(w, g, m, v, name):
    shape = w.shape
    cols = shape[-1]
    rows = _size(shape) // cols
    tr = rows
    for cand in (512, 256, 128, 64, 32, 16, 8):
        if rows > cand and rows % cand == 0:
            tr = cand
            break

    def body(w_ref, g_ref, m_ref, v_ref, d_ref, mo_ref, vo_ref):
        gv = g_ref[...]
        m2 = ADAM_B1 * m_ref[...] + (1.0 - ADAM_B1) * gv
        v2 = ADAM_B2 * v_ref[...] + (1.0 - ADAM_B2) * (gv * gv)
        m_hat = m2 / (1.0 - ADAM_B1 ** ADAM_STEP)
        v_hat = v2 / (1.0 - ADAM_B2 ** ADAM_STEP)
        d_ref[...] = -ADAM_LR * (m_hat / (jnp.sqrt(v_hat) + ADAM_EPS) + ADAM_WD * w_ref[...])
        mo_ref[...] = m2
        vo_ref[...] = v2

    spec = pl.BlockSpec((tr, cols), lambda i: (i, 0))
    outs = pl.pallas_call(
        body, name="adamw_" + name, grid=(rows // tr,), in_specs=[spec] * 4, out_specs=[spec] * 3,
        out_shape=[jax.ShapeDtypeStruct((rows, cols), F32)] * 3, compiler_params=_params(),
    )(*[a.reshape(rows, cols) for a in (w, g, m, v)])
    return [o.reshape(shape) for o in outs]


def kernel(x, p, positions, gla_w_in, gla_w_gate_up, gla_b_gate, gla_norm_g, gla_w_out, mla_w_in, mla_q_norm, mla_kv_norm, mla_w_uq, mla_w_ukv, mla_w_out, conv_w_in, conv_w, conv_w_out, ln_g, ln_b, mlp_w1, mlp_w2, ple_w_gate, ple_w_proj, loss_target, m_gla_w_in, m_gla_w_gate_up, m_gla_b_gate, m_gla_norm_g, m_gla_w_out, m_mla_w_in, m_mla_q_norm, m_mla_kv_norm, m_mla_w_uq, m_mla_w_ukv, m_mla_w_out, m_conv_w_in, m_conv_w, m_conv_w_out, m_ln_g, m_ln_b, m_mlp_w1, m_mlp_w2, m_ple_w_gate, m_ple_w_proj, v_gla_w_in, v_gla_w_gate_up, v_gla_b_gate, v_gla_norm_g, v_gla_w_out, v_mla_w_in, v_mla_q_norm, v_mla_kv_norm, v_mla_w_uq, v_mla_w_ukv, v_mla_w_out, v_conv_w_in, v_conv_w, v_conv_w_out, v_ln_g, v_ln_b, v_mlp_w1, v_mlp_w2, v_ple_w_gate, v_ple_w_proj):
    args = locals()
    shard = {n: args[n] for n in WEIGHT_NAMES}
    mom = {n: args["m_" + n] for n in WEIGHT_NAMES}
    var = {n: args["v_" + n] for n in WEIGHT_NAMES}
    mx, my, mc = _my_place()

    comm = _Overlap(shard, _pack([shard[n].reshape(-1) for n in SMALL], LANES, 8, F32))
    small_all = comm.landing[0]["small"]
    small = {n: shard[n] for n in REPLICATED}
    small.update({n: _unshard(blk, n) for n, blk in _unpack_blocks(small_all.reshape(N_DEV, -1), SMALL, lead=False).items()})
    loss_part, grad_x, small_grads = _step(x[0], p[:, 0], positions[0], loss_target[0], small, comm)
    loss = lax.psum(loss_part[0, 0], MESH_AXES)
    my_grads = _shard_grads(comm.finish(grad_x))

    small_parts = [small_grads[n].reshape(-1) for n in SMALL + REPLICATED]
    red_small = _sum_blocks(_all_gather_small(_pack(small_parts, LANES, 8, F32), "ag_small_grads")).reshape(-1)
    off = 0
    dev = 4 * mx + 2 * my + mc
    for n in SMALL + REPLICATED:
        shape, axis = WEIGHTS[n]
        full_g = red_small[off:off + _size(shape)].reshape(shape)
        off += _size(shape)
        if axis is None:
            my_grads[n] = full_g
        else:
            width = shape[axis] // N_DEV
            my_grads[n] = lax.dynamic_slice_in_dim(full_g, dev * width, width, axis=axis)

    deltas, new_m, new_v = {}, {}, {}
    for n in WEIGHT_NAMES:
        deltas[n], new_m[n], new_v[n] = _adamw(shard[n], my_grads[n], mom[n], var[n], n)
    return (loss, grad_x[None], *[my_grads[n] for n in WEIGHT_NAMES], *[deltas[n] for n in WEIGHT_NAMES],
            *[new_m[n] for n in WEIGHT_NAMES], *[new_v[n] for n in WEIGHT_NAMES])
```

```python
import functools

import jax
import jax.numpy as jnp
from jax import lax
from jax.experimental import pallas as pl
from jax.experimental.pallas import tpu as pltpu

F32, BF16 = jnp.float32, jnp.bfloat16
HIGHEST = lax.Precision.HIGHEST
MESH_AXES = ("x", "y", "c")
N_DEV = 8

D_MODEL = 1024
SEQ = 2048
DEPTH = 4
CHUNK = 64
ALPHA = (2 * DEPTH) ** 0.25
LN_EPS = 1e-5
RMS_EPS = 1e-6
PLE_DIM = 256
D_FF = 4 * D_MODEL
GLA_HEADS = 4
GLA_DK = 128
GLA_DV = 256
GLA_RANK = 16
GLA_TAU = 16.0
GLA_HK = GLA_HEADS * GLA_DK
GLA_HV = GLA_HEADS * GLA_DV
GLA_MAIN = 2 * GLA_HK + GLA_HV + D_MODEL
MLA_HEADS = 8
MLA_NOPE = 128
MLA_ROPE = 64
MLA_V = 128
MLA_RANK = 256
MLA_IN = 2 * MLA_RANK + MLA_ROPE
MLA_IN_PAD = 640
ROPE_BASE = 10000.0
LANES = 128
ADAM_LR, ADAM_B1, ADAM_B2, ADAM_EPS, ADAM_WD, ADAM_STEP = 0.001, 0.9, 0.999, 1e-08, 0.01, 10

V7X_VMEM_LIMIT_BYTES = 56 * 1024 * 1024
PACK_COLS = 1024
PACK_ROW_TILE = 256

WEIGHTS = {
    "gla_w_in": ((2, 1024, 3088), 2), "gla_w_gate_up": ((2, 16, 512), 2), "gla_b_gate": ((2, 512), 1),
    "gla_norm_g": ((2, 256), 1), "gla_w_out": ((2, 1024, 1024), 1), "mla_w_in": ((1, 1024, 576), 1),
    "mla_q_norm": ((1, 256), None), "mla_kv_norm": ((1, 256), None), "mla_w_uq": ((1, 256, 1536), 2),
    "mla_w_ukv": ((1, 256, 2048), 2), "mla_w_out": ((1, 1024, 1024), 1), "conv_w_in": ((1, 1024, 3072), 2),
    "conv_w": ((1, 3, 1024), 2), "conv_w_out": ((1, 1024, 1024), 1), "ln_g": ((4, 2, 1024), 2),
    "ln_b": ((4, 2, 1024), 2), "mlp_w1": ((4, 1024, 4096), 2), "mlp_w2": ((4, 4096, 1024), 1),
    "ple_w_gate": ((4, 1024, 1024), 1), "ple_w_proj": ((4, 256, 1024), 2),
}
WEIGHT_NAMES = list(WEIGHTS)
REG_W2, REG_W1T, REG_WOUT, REG_WG = (0, 512), (1, 512), (8, 128), (9, 128)
A_ROWS = 1280
REG_CONV = (0, 384)
REG_WPT = (0, 128)
MLA_PACKED = ["mla_w_in", "mla_w_uq", "mla_w_ukv"]
SMALL = ["gla_w_gate_up", "gla_b_gate", "gla_norm_g", "conv_w", "ln_g", "ln_b"]
REPLICATED = ["mla_q_norm", "mla_kv_norm"]


def _params(**kw):
    return pltpu.CompilerParams(vmem_limit_bytes=V7X_VMEM_LIMIT_BYTES, **kw)


def _dot(a, b, ca, cb, precision=None):
    return lax.dot_general(a, b, (((ca,), (cb,)), ((), ())), precision=precision, preferred_element_type=F32)


def _nn(a, b):
    return _dot(a.astype(BF16), b.astype(BF16), 1, 0)


def _nt(a, b):
    return _dot(a.astype(BF16), b.astype(BF16), 1, 1)


def _tn(a, b):
    return _dot(a.astype(BF16), b.astype(BF16), 0, 0)


@jax.custom_vjp
def mm_nn(a, b):
    return _nn(a, b)


def _mm_nn_fwd(a, b):
    return _nn(a, b), (a, b)


def _mm_nn_bwd(res, g):
    a, b = res
    return _nt(g, b).astype(a.dtype), _tn(a, g).astype(b.dtype)


mm_nn.defvjp(_mm_nn_fwd, _mm_nn_bwd)


@jax.custom_vjp
def mm_nt(a, b):
    return _nt(a, b)


def _mm_nt_fwd(a, b):
    return _nt(a, b), (a, b)


def _mm_nt_bwd(res, g):
    a, b = res
    return _nn(g, b).astype(a.dtype), _tn(g, a).astype(b.dtype)


mm_nt.defvjp(_mm_nt_fwd, _mm_nt_bwd)


@jax.custom_vjp
def mm_tn(a, b):
    return _tn(a, b)


def _mm_tn_fwd(a, b):
    return _tn(a, b), (a, b)


def _mm_tn_bwd(res, g):
    a, b = res
    return _nt(b, g).astype(a.dtype), _nn(a, g).astype(b.dtype)


mm_tn.defvjp(_mm_tn_fwd, _mm_tn_bwd)


def _iota2(shape, dim):
    return lax.broadcasted_iota(jnp.int32, shape, dim)


@jax.custom_vjp
def cumsum_rows(x):
    n = x.shape[0]
    tri = (_iota2((n, n), 0) >= _iota2((n, n), 1)).astype(F32)
    return _dot(tri, x, 1, 0, precision=HIGHEST)


def _cumsum_fwd(x):
    return cumsum_rows(x), None


def _cumsum_bwd(_, g):
    n = g.shape[0]
    tri_t = (_iota2((n, n), 0) <= _iota2((n, n), 1)).astype(F32)
    return (_dot(tri_t, g, 1, 0, precision=HIGHEST),)


cumsum_rows.defvjp(_cumsum_fwd, _cumsum_bwd)


def _rot_matrix(transposed):
    i, j = _iota2((LANES, LANES), 0), _iota2((LANES, LANES), 1)
    if transposed:
        i, j = j, i
    half = MLA_ROPE // 2
    plus = (i == j - half) & (j >= half) & (j < MLA_ROPE)
    minus = (i == j + half) & (j < half)
    return plus.astype(F32) - minus.astype(F32)


@jax.custom_vjp
def rot_half(x):
    return _dot(x, _rot_matrix(False), 1, 0, precision=HIGHEST)


def _rot_fwd(x):
    return rot_half(x), None


def _rot_bwd(_, g):
    return (_dot(g, _rot_matrix(True), 1, 0, precision=HIGHEST),)


rot_half.defvjp(_rot_fwd, _rot_bwd)


def _shift_rows_raw(x, s):
    n = x.shape[0]
    row = _iota2(x.shape, 0)
    rolled = pltpu.roll(x, s % n, 0)
    keep = (row >= s) if s > 0 else (row < n + s)
    return jnp.where(keep, rolled, 0.0)


@functools.partial(jax.custom_vjp, nondiff_argnums=(1,))
def shift_rows(x, s):
    return _shift_rows_raw(x, s)


def _shift_fwd(x, s):
    return _shift_rows_raw(x, s), None


def _shift_bwd(s, _, g):
    return (_shift_rows_raw(g, -s),)


shift_rows.defvjp(_shift_fwd, _shift_bwd)


def _layer_norm(a, g, b):
    mu = jnp.mean(a, -1, keepdims=True)
    xc = a - mu
    var = jnp.mean(xc * xc, -1, keepdims=True)
    return xc * lax.rsqrt(var + LN_EPS) * g + b


def _rms_norm(a, g):
    return a * lax.rsqrt(jnp.mean(a * a, -1, keepdims=True) + RMS_EPS) * g


def _log_sigmoid(z):
    return jnp.minimum(z, 0.0) - jnp.log(1.0 + jnp.exp(-jnp.abs(z)))


def _matmul(a, b, *, name, ta=False, tb=False, tm=512, tn=512, a_fn=None, epi=None, epi_ins=(), out_dtypes=(BF16,),
            b_at=None, out_at=None, out_buf=None, after=None, n_row_sums=0):
    m = a.shape[1] if ta else a.shape[0]
    k = a.shape[0] if ta else a.shape[1]
    if b_at is None:
        n, kb = (b.shape[0], b.shape[1]) if tb else (b.shape[1], b.shape[0])
    else:
        rb, r = b_at
        n, kb = (N_DEV * r, b.shape[2]) if tb else (b.shape[2], N_DEV * r)
    assert kb == k, (name, a.shape, b.shape, k, kb)
    tm, tn = min(tm, m), min(tn, n)
    assert m % tm == 0 and n % tn == 0, (name, m, n, tm, tn)
    a_spec = pl.BlockSpec((k, tm), lambda i, j: (0, i)) if ta else pl.BlockSpec((tm, k), lambda i, j: (i, 0))
    if b_at is None:
        b_spec = pl.BlockSpec((tn, k), lambda i, j: (j, 0)) if tb else pl.BlockSpec((k, tn), lambda i, j: (0, j))
        load_b = lambda ref: ref[...]
    elif tb and tn == n:
        b_spec = pl.BlockSpec((N_DEV, r, k), lambda i, j: (0, rb, 0))
        load_b = lambda ref: ref[...].reshape(n, k)
    elif tb:
        assert tn == r, (name, tn, r)
        b_spec = pl.BlockSpec((1, r, k), lambda i, j: (j, rb, 0))
        load_b = lambda ref: ref[0]
    else:
        b_spec = pl.BlockSpec((N_DEV, r, tn), lambda i, j: (0, rb, j))
        load_b = lambda ref: ref[...].reshape(k, tn)
    e_specs = []
    for e in epi_ins:
        if e.shape == (1, n):
            e_specs.append(pl.BlockSpec((1, tn), lambda i, j: (0, j)))
        else:
            assert e.shape == (m, n), (name, e.shape, m, n)
            e_specs.append(pl.BlockSpec((tm, tn), lambda i, j: (i, j)))
    n_epi = len(epi_ins)
    ca, cb = (0 if ta else 1), (1 if tb else 0)
    operands = [a, b, *epi_ins]
    in_specs = [a_spec, b_spec, *e_specs]
    if out_at is None:
        assert n_row_sums == 0 or tn == n, (name, tn, n)
        out_specs = [pl.BlockSpec((tm, tn), lambda i, j: (i, j)) for _ in out_dtypes]
        out_specs += [pl.BlockSpec((1, n), lambda i, j: (0, 0))] * n_row_sums
        out_shape = [jax.ShapeDtypeStruct((m, n), dt) for dt in out_dtypes]
        out_shape += [jax.ShapeDtypeStruct((1, n), F32)] * n_row_sums
        aliases, n_buf = {}, 0
    else:
        orb, orows = out_at
        assert len(out_dtypes) == 1 and orows % tm == 0 and m == N_DEV * orows and n == out_buf.shape[2], (name, m, n)
        per = orows // tm
        out_specs = [pl.BlockSpec((1, tm, tn), lambda i, j: (i // per, orb * per + i % per, j))]
        out_shape = [jax.ShapeDtypeStruct(out_buf.shape, out_buf.dtype)]
        operands.append(out_buf)
        in_specs.append(pl.BlockSpec(memory_space=pl.ANY))
        aliases, n_buf = {len(operands) - 1: 0}, 1
    for dep in ([] if after is None else after if isinstance(after, (list, tuple)) else [after]):
        if dep is not None:
            operands.append(dep)
            in_specs.append(pl.BlockSpec(memory_space=pl.ANY))
            n_buf += 1

    def body(a_ref, b_ref, *rest):
        av = a_ref[...]
        if a_fn is not None:
            av = a_fn(av)
        acc = _dot(av.astype(BF16), load_b(b_ref).astype(BF16), ca, cb)
        outs = epi(acc, *[r_[...] for r_ in rest[:n_epi]]) if epi is not None else (acc,)
        o_refs = rest[n_epi + n_buf:]
        n_tiles = len(o_refs) - n_row_sums
        for o_ref, val in zip(o_refs[:n_tiles], outs):
            o_ref[...] = val.astype(o_ref.dtype).reshape(o_ref.shape)
        if n_row_sums:
            @pl.when(pl.program_id(0) == 0)
            def _():
                for o_ref in o_refs[n_tiles:]:
                    o_ref[...] = jnp.zeros_like(o_ref)

            for o_ref, val in zip(o_refs[n_tiles:], outs[n_tiles:]):
                o_ref[...] += val

    outs = pl.pallas_call(
        body, name=name, grid=(m // tm, n // tn), in_specs=in_specs, out_specs=out_specs, out_shape=out_shape,
        input_output_aliases=aliases, compiler_params=_params(),
    )(*operands)
    return outs[0] if len(outs) == 1 else tuple(outs)


def _tile_fwd(f, tiled, params, out_dtypes, *, tm, name):
    t = tiled[0].shape[0]
    assert t % tm == 0
    out_avals = jax.eval_shape(f, *[jax.ShapeDtypeStruct((tm, x.shape[1]), F32) for x in tiled],
                               *[jax.ShapeDtypeStruct(p.shape, F32) for p in params])
    nt, npar = len(tiled), len(params)

    def body(*refs):
        ins = [r[...].astype(F32) for r in refs[:nt + npar]]
        outs = f(*ins)
        for o_ref, val in zip(refs[nt + npar:], outs):
            o_ref[...] = val.astype(o_ref.dtype)

    return pl.pallas_call(
        body, name=name, grid=(t // tm,),
        in_specs=[pl.BlockSpec((tm, x.shape[1]), lambda i: (i, 0)) for x in tiled]
        + [pl.BlockSpec(p.shape, lambda i: (0, 0)) for p in params],
        out_specs=[pl.BlockSpec((tm, o.shape[1]), lambda i: (i, 0)) for o in out_avals],
        out_shape=[jax.ShapeDtypeStruct((t, o.shape[1]), dt) for o, dt in zip(out_avals, out_dtypes)],
        compiler_params=_params(),
    )(*tiled, *params)


def _tile_bwd(f, tiled, params, cots, d_tiled_dtypes, *, tm, name, diff_tiled=None):
    t = tiled[0].shape[0]
    assert t % tm == 0
    nt, npar, nc = len(tiled), len(params), len(cots)
    diff_tiled = list(range(nt)) if diff_tiled is None else diff_tiled

    def body(*refs):
        ins = [r[...].astype(F32) for r in refs[:nt + npar]]
        cts = [r[...].astype(F32) for r in refs[nt + npar:nt + npar + nc]]
        o_refs = refs[nt + npar + nc:]
        _, vjp = jax.vjp(f, *ins)
        grads = vjp(tuple(cts))
        for o_ref, idx in zip(o_refs[:len(diff_tiled)], diff_tiled):
            o_ref[...] = grads[idx].astype(o_ref.dtype)
        p_refs = o_refs[len(diff_tiled):]

        @pl.when(pl.program_id(0) == 0)
        def _():
            for p_ref in p_refs:
                p_ref[...] = jnp.zeros_like(p_ref)

        for p_ref, gp in zip(p_refs, grads[nt:]):
            p_ref[...] += gp

    outs = pl.pallas_call(
        body, name=name, grid=(t // tm,),
        in_specs=[pl.BlockSpec((tm, x.shape[1]), lambda i: (i, 0)) for x in tiled]
        + [pl.BlockSpec(p.shape, lambda i: (0, 0)) for p in params]
        + [pl.BlockSpec((tm, c.shape[1]), lambda i: (i, 0)) for c in cots],
        out_specs=[pl.BlockSpec((tm, tiled[idx].shape[1]), lambda i: (i, 0)) for idx in diff_tiled]
        + [pl.BlockSpec(p.shape, lambda i: (0, 0)) for p in params],
        out_shape=[jax.ShapeDtypeStruct(tiled[idx].shape, dt) for idx, dt in zip(diff_tiled, d_tiled_dtypes)]
        + [jax.ShapeDtypeStruct(p.shape, F32) for p in params],
        compiler_params=_params(),
    )(*tiled, *params, *cots)
    return outs[:len(diff_tiled)], outs[len(diff_tiled):]


def _gla_head(q, k, v, r, z, g, st):
    c = q.shape[0]
    causal = _iota2((c, c), 0) >= _iota2((c, c), 1)
    la = _log_sigmoid(z) * (1.0 / GLA_TAU)
    big_l = cumsum_rows(la)
    ep, en = jnp.exp(big_l), jnp.exp(-big_l)
    qs = q * (GLA_DK ** -0.5)
    qp = qs * ep
    s = jnp.where(causal, mm_nt(qp, k * en), mm_nt(qs * en, k * ep))
    o = mm_nn(s, v) + mm_nt(qp, st)
    l_end = jnp.sum(la, axis=0, keepdims=True)
    st_new = st * jnp.exp(l_end) + mm_tn(v, k * jnp.exp(l_end - big_l))
    u = _rms_norm(o, g) * (r * jax.nn.sigmoid(r))
    return u, st_new


def _gla_slices(h):
    q = slice(GLA_DK * h, GLA_DK * (h + 1))
    k = slice(GLA_HK + GLA_DK * h, GLA_HK + GLA_DK * (h + 1))
    v = slice(2 * GLA_HK + GLA_DV * h, 2 * GLA_HK + GLA_DV * (h + 1))
    r = slice(2 * GLA_HK + GLA_HV + GLA_DV * h, 2 * GLA_HK + GLA_HV + GLA_DV * (h + 1))
    return q, k, v, r


GLA_CHUNKS_PER_STEP = 2


def _gla_fwd(proj, z, norm_g):
    t = proj.shape[0]
    nc, per = t // CHUNK, GLA_CHUNKS_PER_STEP
    rows_per_step = per * CHUNK

    def body(proj_ref, z_ref, g_ref, u_ref, st_save_ref, st_ref):
        @pl.when(pl.program_id(0) == 0)
        def _():
            st_ref[...] = jnp.zeros_like(st_ref)

        g = g_ref[...]
        for h in range(GLA_HEADS):
            sq, sk, sv, sr = _gla_slices(h)
            st = st_ref[h]
            for c in range(per):
                rows = slice(c * CHUNK, (c + 1) * CHUNK)
                st_save_ref[c, h] = st
                u, st = _gla_head(proj_ref[rows, sq].astype(F32), proj_ref[rows, sk].astype(F32),
                                  proj_ref[rows, sv].astype(F32), proj_ref[rows, sr].astype(F32),
                                  z_ref[rows, GLA_DK * h:GLA_DK * (h + 1)], g, st)
                u_ref[rows, GLA_DV * h:GLA_DV * (h + 1)] = u.astype(u_ref.dtype)
            st_ref[h] = st

    return pl.pallas_call(
        body, name="gla_fwd", grid=(nc // per,),
        in_specs=[pl.BlockSpec((rows_per_step, GLA_MAIN), lambda i: (i, 0)),
                  pl.BlockSpec((rows_per_step, GLA_HK), lambda i: (i, 0)), pl.BlockSpec((1, GLA_DV), lambda i: (0, 0))],
        out_specs=[pl.BlockSpec((rows_per_step, GLA_HV), lambda i: (i, 0)),
                   pl.BlockSpec((per, GLA_HEADS, GLA_DV, GLA_DK), lambda i: (i, 0, 0, 0))],
        out_shape=[jax.ShapeDtypeStruct((t, GLA_HV), BF16), jax.ShapeDtypeStruct((nc, GLA_HEADS, GLA_DV, GLA_DK), F32)],
        scratch_shapes=[pltpu.VMEM((GLA_HEADS, GLA_DV, GLA_DK), F32)],
        compiler_params=_params(),
    )(proj, z, norm_g)


def _gla_bwd(proj, z, norm_g, states, du, after):
    t = proj.shape[0]
    nc, per = t // CHUNK, GLA_CHUNKS_PER_STEP
    rows_per_step = per * CHUNK
    n_steps = nc // per
    after = [a for a in after if a is not None]

    def body(proj_ref, z_ref, g_ref, st_in_ref, du_ref, *rest):
        dproj_ref, dz_ref, dg_ref, dst_ref = rest[len(after):]

        @pl.when(pl.program_id(0) == 0)
        def _():
            dst_ref[...] = jnp.zeros_like(dst_ref)
            dg_ref[...] = jnp.zeros_like(dg_ref)

        g = g_ref[...]
        for h in range(GLA_HEADS):
            sq, sk, sv, sr = _gla_slices(h)
            dst = dst_ref[h]
            for c in reversed(range(per)):
                rows = slice(c * CHUNK, (c + 1) * CHUNK)
                ins = (proj_ref[rows, sq].astype(F32), proj_ref[rows, sk].astype(F32), proj_ref[rows, sv].astype(F32),
                       proj_ref[rows, sr].astype(F32), z_ref[rows, GLA_DK * h:GLA_DK * (h + 1)], g, st_in_ref[c, h])
                _, vjp = jax.vjp(_gla_head, *ins)
                dq, dk, dv, dr, dz, dg, dst = vjp((du_ref[rows, GLA_DV * h:GLA_DV * (h + 1)], dst))
                dproj_ref[rows, sq] = dq.astype(dproj_ref.dtype)
                dproj_ref[rows, sk] = dk.astype(dproj_ref.dtype)
                dproj_ref[rows, sv] = dv.astype(dproj_ref.dtype)
                dproj_ref[rows, sr] = dr.astype(dproj_ref.dtype)
                dz_ref[rows, GLA_DK * h:GLA_DK * (h + 1)] = dz
                dg_ref[...] += dg
            dst_ref[h] = dst

    rev = lambda i: (n_steps - 1 - i, 0)
    return pl.pallas_call(
        body, name="gla_bwd", grid=(n_steps,),
        in_specs=[pl.BlockSpec((rows_per_step, GLA_MAIN), rev), pl.BlockSpec((rows_per_step, GLA_HK), rev),
                  pl.BlockSpec((1, GLA_DV), lambda i: (0, 0)),
                  pl.BlockSpec((per, GLA_HEADS, GLA_DV, GLA_DK), lambda i: (n_steps - 1 - i, 0, 0, 0)),
                  pl.BlockSpec((rows_per_step, GLA_HV), rev)] + [pl.BlockSpec(memory_space=pl.ANY)] * len(after),
        out_specs=[pl.BlockSpec((rows_per_step, GLA_MAIN), rev), pl.BlockSpec((rows_per_step, GLA_HK), rev),
                   pl.BlockSpec((1, GLA_DV), lambda i: (0, 0))],
        out_shape=[jax.ShapeDtypeStruct((t, GLA_MAIN), BF16), jax.ShapeDtypeStruct((t, GLA_HK), F32),
                   jax.ShapeDtypeStruct((1, GLA_DV), F32)],
        scratch_shapes=[pltpu.VMEM((GLA_HEADS, GLA_DV, GLA_DK), F32)],
        compiler_params=_params(),
    )(proj, z, norm_g, states, du, *after)


def _mla_pre(cq, cos, sin, gq, gkv, w_uq, w_ukv):
    qlat = _rms_norm(cq[:, :MLA_RANK], gq)
    kvlat = _rms_norm(cq[:, MLA_RANK:2 * MLA_RANK], gkv)
    kr = cq[:, 2 * MLA_RANK:]
    scale = (MLA_NOPE + MLA_ROPE) ** -0.5
    q = mm_nn(qlat, w_uq) * scale
    kv = mm_nn(kvlat, w_ukv)
    n_nope = MLA_HEADS * MLA_NOPE
    ropes = []
    for h in range(MLA_HEADS):
        qr = q[:, n_nope + LANES * h:n_nope + LANES * (h + 1)]
        ropes.append(qr * cos + rot_half(qr) * sin)
    return q[:, :n_nope], jnp.concatenate(ropes, axis=1), kv, kr * cos + rot_half(kr) * sin


MLA_Q_TILE = 256


def _mla_attn_block(qn, qr, kv, kr, q0):
    tq, nk = qn.shape[0], kv.shape[0]
    s = mm_nt(qn, kv[:, :MLA_NOPE]) + mm_nt(qr, kr)
    visible = (_iota2((tq, nk), 1) // CHUNK) <= ((q0 + _iota2((tq, nk), 0)) // CHUNK)
    s = jnp.where(visible, s, -1e30)
    e = jnp.exp(s - jnp.max(s, -1, keepdims=True))
    p = e / jnp.sum(e, -1, keepdims=True)
    return mm_nn(p, kv[:, MLA_NOPE:])


def _mla_attn_fwd(qn, qr, kv, kr):
    t = qn.shape[0]

    def body(qn_ref, qr_ref, kv_ref, kr_ref, o_ref):
        for i in range(t // MLA_Q_TILE):
            rows = slice(i * MLA_Q_TILE, (i + 1) * MLA_Q_TILE)
            keys = slice(0, (i + 1) * MLA_Q_TILE)
            o = _mla_attn_block(qn_ref[rows, :].astype(F32), qr_ref[rows, :].astype(F32), kv_ref[keys, :].astype(F32),
                                kr_ref[keys, :].astype(F32), i * MLA_Q_TILE)
            o_ref[rows, :] = o.astype(o_ref.dtype)

    return pl.pallas_call(
        body, name="mla_attn_fwd", grid=(MLA_HEADS,),
        in_specs=[pl.BlockSpec((t, MLA_NOPE), lambda h: (0, h)), pl.BlockSpec((t, LANES), lambda h: (0, h)),
                  pl.BlockSpec((t, MLA_NOPE + MLA_V), lambda h: (0, h)), pl.BlockSpec((t, LANES), lambda h: (0, 0))],
        out_specs=pl.BlockSpec((t, MLA_V), lambda h: (0, h)),
        out_shape=jax.ShapeDtypeStruct((t, MLA_HEADS * MLA_V), BF16),
        compiler_params=_params(),
    )(qn, qr, kv, kr)


def _mla_attn_bwd(qn, qr, kv, kr, do, after):
    t = qn.shape[0]
    after = [a for a in after if a is not None]

    def body(qn_ref, qr_ref, kv_ref, kr_ref, do_ref, *rest):
        dqn_ref, dqr_ref, dkv_ref, dkr_ref = rest[len(after):]
        dkv_ref[...] = jnp.zeros_like(dkv_ref)

        @pl.when(pl.program_id(0) == 0)
        def _():
            dkr_ref[...] = jnp.zeros_like(dkr_ref)

        for i in range(t // MLA_Q_TILE):
            rows = slice(i * MLA_Q_TILE, (i + 1) * MLA_Q_TILE)
            keys = slice(0, (i + 1) * MLA_Q_TILE)
            f = functools.partial(_mla_attn_block, q0=i * MLA_Q_TILE)
            _, vjp = jax.vjp(f, qn_ref[rows, :].astype(F32), qr_ref[rows, :].astype(F32), kv_ref[keys, :].astype(F32),
                             kr_ref[keys, :].astype(F32))
            dqn, dqr, dkv, dkr = vjp(do_ref[rows, :].astype(F32))
            dqn_ref[rows, :] = dqn
            dqr_ref[rows, :] = dqr
            dkv_ref[keys, :] += dkv
            dkr_ref[keys, :] += dkr

    return pl.pallas_call(
        body, name="mla_attn_bwd", grid=(MLA_HEADS,),
        in_specs=[pl.BlockSpec((t, MLA_NOPE), lambda h: (0, h)), pl.BlockSpec((t, LANES), lambda h: (0, h)),
                  pl.BlockSpec((t, MLA_NOPE + MLA_V), lambda h: (0, h)), pl.BlockSpec((t, LANES), lambda h: (0, 0)),
                  pl.BlockSpec((t, MLA_V), lambda h: (0, h))] + [pl.BlockSpec(memory_space=pl.ANY)] * len(after),
        out_specs=[pl.BlockSpec((t, MLA_NOPE), lambda h: (0, h)), pl.BlockSpec((t, LANES), lambda h: (0, h)),
                   pl.BlockSpec((t, MLA_NOPE + MLA_V), lambda h: (0, h)), pl.BlockSpec((t, LANES), lambda h: (0, 0))],
        out_shape=[jax.ShapeDtypeStruct(qn.shape, F32), jax.ShapeDtypeStruct(qr.shape, F32),
                   jax.ShapeDtypeStruct(kv.shape, F32), jax.ShapeDtypeStruct(kr.shape, F32)],
        compiler_params=_params(),
    )(qn, qr, kv, kr, do, *after)


def _rope_tables(pos_col, inv_freq_row):
    t = pos_col.shape[0]

    def body(pos_ref, f_ref, cos_ref, sin_ref):
        ang = pos_ref[...].astype(F32) * f_ref[...]
        live = _iota2(ang.shape, 1) < MLA_ROPE
        cos_ref[...] = jnp.where(live, jnp.cos(ang), 0.0)
        sin_ref[...] = jnp.where(live, jnp.sin(ang), 0.0)

    return pl.pallas_call(
        body, name="rope_tables", out_shape=[jax.ShapeDtypeStruct((t, LANES), F32)] * 2, compiler_params=_params(),
    )(pos_col, inv_freq_row)


CONV_COL_TILE = 256


def _conv_gate(b, c, u, w0, w1, w2):
    cu = c * u
    return b * (w2 * cu + w1 * shift_rows(cu, 1) + w0 * shift_rows(cu, 2))


def _conv_specs(t):
    nb = D_MODEL // CONV_COL_TILE
    return [pl.BlockSpec((t, CONV_COL_TILE), lambda j, part=part: (0, part * nb + j)) for part in range(3)]


def _conv_fwd(bcu, w):
    t = bcu.shape[0]

    def body(b_ref, c_ref, u_ref, w_ref, o_ref):
        o_ref[...] = _conv_gate(b_ref[...], c_ref[...], u_ref[...], w_ref[0:1, :], w_ref[1:2, :],
                                w_ref[2:3, :]).astype(o_ref.dtype)

    return pl.pallas_call(
        body, name="conv_fwd", grid=(D_MODEL // CONV_COL_TILE,),
        in_specs=_conv_specs(t) + [pl.BlockSpec((3, CONV_COL_TILE), lambda j: (0, j))],
        out_specs=pl.BlockSpec((t, CONV_COL_TILE), lambda j: (0, j)),
        out_shape=jax.ShapeDtypeStruct((t, D_MODEL), BF16), compiler_params=_params(),
    )(bcu, bcu, bcu, w)


def _conv_bwd(bcu, w, dout, after):
    t = bcu.shape[0]
    after = [a for a in after if a is not None]

    def body(b_ref, c_ref, u_ref, w_ref, do_ref, *rest):
        db_ref, dc_ref, du_ref, dw_ref = rest[len(after):]
        _, vjp = jax.vjp(_conv_gate, b_ref[...], c_ref[...], u_ref[...], w_ref[0:1, :], w_ref[1:2, :], w_ref[2:3, :])
        db, dc, du, dw0, dw1, dw2 = vjp(do_ref[...])
        db_ref[...] = db.astype(db_ref.dtype)
        dc_ref[...] = dc.astype(dc_ref.dtype)
        du_ref[...] = du.astype(du_ref.dtype)
        dw_ref[0:1, :] = dw0
        dw_ref[1:2, :] = dw1
        dw_ref[2:3, :] = dw2

    col = pl.BlockSpec((t, CONV_COL_TILE), lambda j: (0, j))
    return pl.pallas_call(
        body, name="conv_bwd", grid=(D_MODEL // CONV_COL_TILE,),
        in_specs=_conv_specs(t) + [pl.BlockSpec((3, CONV_COL_TILE), lambda j: (0, j)), col]
        + [pl.BlockSpec(memory_space=pl.ANY)] * len(after),
        out_specs=[col, col, col, pl.BlockSpec((3, CONV_COL_TILE), lambda j: (0, j))],
        out_shape=[jax.ShapeDtypeStruct((t, D_MODEL), BF16)] * 3 + [jax.ShapeDtypeStruct((3, D_MODEL), F32)],
        compiler_params=_params(),
    )(bcu, bcu, bcu, w, dout, *after)


def _loss_head(y, target):
    t, d = y.shape
    tm = 256

    def body(y_ref, t_ref, loss_ref, dy_ref):
        @pl.when(pl.program_id(0) == 0)
        def _():
            loss_ref[...] = jnp.zeros_like(loss_ref)

        err = y_ref[...] - t_ref[...]
        dy_ref[...] = err * (1.0 / d)
        loss_ref[...] += 0.5 * jnp.sum(jnp.sum(err * err, axis=-1, keepdims=True) * (1.0 / d))

    tile = pl.BlockSpec((tm, d), lambda i: (i, 0))
    return pl.pallas_call(
        body, name="loss_head", grid=(t // tm,), in_specs=[tile, tile],
        out_specs=[pl.BlockSpec((8, LANES), lambda i: (0, 0)), tile],
        out_shape=[jax.ShapeDtypeStruct((8, LANES), F32), jax.ShapeDtypeStruct((t, d), F32)],
        compiler_params=_params(),
    )(y, target)


def _ln_epi(acc, res, g, b):
    a = ALPHA * res + acc
    return a, _layer_norm(a, g, b)


def _ln_fn(a, g, b):
    return (_layer_norm(a, g, b),)


def _ln_bwd_epi(scale):
    def epi(acc, res, a, g, b):
        _, vjp = jax.vjp(_ln_fn, a, g, b)
        return vjp((acc + scale * res,))
    return epi


def _relu_sq(h):
    r = jnp.maximum(h.astype(F32), 0.0)
    return r * r


def _pad_cols(w, n):
    return jnp.pad(w, ((0, 0), (0, n - w.shape[1])))


def _pad_rows(w, n):
    return jnp.pad(w, ((0, n - w.shape[0]), (0, 0)))


def _uq_to_kernel_layout(w_uq):
    w = w_uq.reshape(MLA_RANK, MLA_HEADS, MLA_NOPE + MLA_ROPE)
    nope = w[:, :, :MLA_NOPE].reshape(MLA_RANK, MLA_HEADS * MLA_NOPE)
    rope = jnp.pad(w[:, :, MLA_NOPE:], ((0, 0), (0, 0), (0, LANES - MLA_ROPE))).reshape(MLA_RANK, MLA_HEADS * LANES)
    return jnp.concatenate([nope, rope], axis=1)


def _uq_from_kernel_layout(w):
    nope = w[:, :MLA_HEADS * MLA_NOPE].reshape(MLA_RANK, MLA_HEADS, MLA_NOPE)
    rope = w[:, MLA_HEADS * MLA_NOPE:].reshape(MLA_RANK, MLA_HEADS, LANES)[:, :, :MLA_ROPE]
    return jnp.concatenate([nope, rope], axis=2).reshape(MLA_RANK, MLA_HEADS * (MLA_NOPE + MLA_ROPE))


def _step(x, p, positions, target, small, comm):
    t = x.shape[0]
    w = small
    freqs = ROPE_BASE ** (-jnp.arange(0, MLA_ROPE // 2, dtype=F32) * (2.0 / MLA_ROPE))
    freq_row = jnp.concatenate([freqs, freqs, jnp.zeros((LANES - MLA_ROPE,), F32)])[None, :]
    cos, sin = _rope_tables(positions.reshape(t, 1), freq_row)

    saved = []
    for i in range(DEPTH):
        j, kind = i // 3, i % 3
        wl = comm.mixer_weights(i)
        s = {"x": x, "wl": wl}
        tok = comm.at("fwd", i, "begin", x)
        if kind == 0:
            s["w_main"] = wl["gla_w_in_t"][:GLA_MAIN]
            s["w_lr"] = _pad_rows(wl["gla_w_in_t"][GLA_MAIN:], LANES)
            s["w_up"] = _pad_rows(w["gla_w_gate_up"][j], LANES).astype(BF16)
            s["proj"] = _matmul(x, s["w_main"], name="gla_proj", tb=True, tn=1024, after=tok)
            s["glr"] = _matmul(x, s["w_lr"], name="gla_lr", tb=True, out_dtypes=(F32,))
            s["z"] = _matmul(s["glr"], s["w_up"], name="gla_gate", epi=lambda acc, b: (acc + b,),
                             epi_ins=(w["gla_b_gate"][j][None, :],), out_dtypes=(F32,))
            s["u"], s["states"] = _gla_fwd(s["proj"], s["z"], w["gla_norm_g"][j][None, :])
        elif kind == 1:
            s["w_in"] = _pad_cols(wl["mla_w_in"], MLA_IN_PAD)
            s["w_uq"] = _uq_to_kernel_layout(wl["mla_w_uq"])
            s["cq"] = _matmul(x, s["w_in"], name="mla_proj", tn=MLA_IN_PAD, out_dtypes=(F32,), after=tok)
            s["pre_params"] = (w["mla_q_norm"][j][None, :], w["mla_kv_norm"][j][None, :], s["w_uq"], wl["mla_w_ukv"])
            s["qn"], s["qr"], s["kv"], s["kr"] = _tile_fwd(_mla_pre, (s["cq"], cos, sin), s["pre_params"],
                                                           (BF16, BF16, BF16, BF16), tm=256, name="mla_pre_fwd")
            s["u"] = _mla_attn_fwd(s["qn"], s["qr"], s["kv"], s["kr"])
        else:
            s["bcu"] = _matmul(x, wl["conv"], name="conv_proj", tb=True, tm=256, tn=3 * D_MODEL, b_at=REG_CONV,
                               out_dtypes=(F32,), after=tok)
            s["u"] = _conv_fwd(s["bcu"], w["conv_w"][j])
        g0, b0 = w["ln_g"][i, 0][None, :], w["ln_b"][i, 0][None, :]
        g1, b1 = w["ln_g"][i, 1][None, :], w["ln_b"][i, 1][None, :]
        wa, wb = s["wa"], _ = comm.slab_weights(i, s["u"])
        s["a1"], s["x1"] = _matmul(s["u"], wa, name="mixer_out_ln", tm=256, tn=D_MODEL, b_at=REG_WOUT, epi=_ln_epi,
                                   epi_ins=(x, g0, b0), out_dtypes=(F32, F32))
        s["hh"] = _matmul(s["x1"], wa, name="mlp_up", tb=True, tm=256, tn=D_FF, b_at=REG_W1T)
        tok = comm.at("fwd", i, "mid", s["hh"])
        s["a2"], s["x2"] = _matmul(s["hh"], wa, name="mlp_down_ln", tm=256, tn=D_MODEL, b_at=REG_W2, a_fn=_relu_sq,
                                   epi=_ln_epi, epi_ins=(s["x1"], g1, b1), out_dtypes=(F32, F32), after=tok)
        s["pp"] = _matmul(p[i], wb, name="ple_proj", tb=True, tn=D_MODEL, b_at=REG_WPT)
        tok = comm.at("fwd", i, "end", s["pp"])
        x, s["gt"] = _matmul(s["x2"], wa, name="ple_gate", tn=1024, b_at=REG_WG,
                             epi=lambda acc, xr, pp: (xr + jax.nn.sigmoid(acc) * pp.astype(F32), acc),
                             epi_ins=(s["x2"], s["pp"]), out_dtypes=(F32, BF16), after=tok)
        saved.append(s)

    loss_part, dx = _loss_head(x, target)

    gw = {n: [None] * WEIGHTS[n][0][0] for n in SMALL + REPLICATED}
    ln_g_grads, ln_b_grads = [[None, None] for _ in range(DEPTH)], [[None, None] for _ in range(DEPTH)]
    resid = lambda acc, r: (acc + ALPHA * r,)
    plus = lambda acc, r: (acc + r,)
    for i in reversed(range(DEPTH)):
        j, kind = i // 3, i % 3
        s = saved[i]
        wa = s["wa"]
        ga = lax.empty((N_DEV, A_ROWS, D_MODEL), BF16)
        gb = lax.empty((N_DEV, REG_WPT[1], PLE_DIM), BF16)
        layer_grads = {}
        tok = comm.at("bwd", i, "begin", dx)

        def ple_bwd(dxo, gt, pp):
            sg = jax.nn.sigmoid(gt)
            return dxo * sg, dxo * pp * sg * (1.0 - sg)

        d_pp, d_gt = _tile_fwd(ple_bwd, (dx, s["gt"], s["pp"]), (), (BF16, BF16), tm=256, name="ple_bwd")
        gb = _matmul(d_pp, p[i], name="ple_proj_dw", ta=True, tm=REG_WPT[1], tn=PLE_DIM, out_at=REG_WPT, out_buf=gb, after=tok)
        ga = _matmul(s["x2"], d_gt, name="ple_gate_dw", ta=True, tm=REG_WG[1], tn=1024, out_at=REG_WG, out_buf=ga)
        g1, b1 = w["ln_g"][i, 1][None, :], w["ln_b"][i, 1][None, :]
        d_a2, ln_g_grads[i][1], ln_b_grads[i][1] = _matmul(
            d_gt, wa, name="ple_gate_dx_ln", tb=True, tm=256, tn=D_MODEL, b_at=REG_WG, epi=_ln_bwd_epi(1.0),
            epi_ins=(dx, s["a2"], g1, b1), out_dtypes=(F32,), n_row_sums=2, after=[ga, gb])
        tok = comm.at("bwd", i, "ln", d_a2)
        ga = _matmul(s["hh"], d_a2, name="mlp_down_dw", ta=True, tm=REG_W2[1], tn=1024, a_fn=_relu_sq, out_at=REG_W2,
                     out_buf=ga, after=tok)
        d_hh = _matmul(d_a2, wa, name="mlp_down_dx", tb=True, tm=256, tn=D_FF, b_at=REG_W2, after=ga,
                       epi=lambda acc, hh: (acc * 2.0 * jnp.maximum(hh.astype(F32), 0.0),), epi_ins=(s["hh"],))
        ga = _matmul(d_hh, s["x1"], name="mlp_up_dw", ta=True, tm=REG_W1T[1], tn=1024, out_at=REG_W1T, out_buf=ga)
        g0, b0 = w["ln_g"][i, 0][None, :], w["ln_b"][i, 0][None, :]
        d_a1, ln_g_grads[i][0], ln_b_grads[i][0] = _matmul(
            d_hh, wa, name="mlp_up_dx_ln", tm=256, tn=D_MODEL, b_at=REG_W1T, epi=_ln_bwd_epi(ALPHA),
            epi_ins=(d_a2, s["a1"], g0, b0), out_dtypes=(F32,), n_row_sums=2, after=ga)
        ga = _matmul(s["u"], d_a1, name="mixer_out_dw", ta=True, tm=REG_WOUT[1], tn=1024, out_at=REG_WOUT, out_buf=ga)
        du = _matmul(d_a1, wa, name="mixer_out_dx", tb=True, tn=1024, b_at=REG_WOUT, out_dtypes=(F32,), after=ga)
        comm.slab_grads(i, ga, gb)
        tok = comm.at("bwd", i, "slab_done", du) or []
        if kind == 0:
            dproj, dz, dg = _gla_bwd(s["proj"], s["z"], w["gla_norm_g"][j][None, :], s["states"], du, tok)
            tok = comm.at("bwd", i, "mixer_done", dproj)
            gw["gla_norm_g"][j] = dg[0]
            gw["gla_b_gate"][j] = _tile_bwd(lambda zz, b: (zz + b,), (s["z"],), (w["gla_b_gate"][j][None, :],), (dz,), (),
                                            tm=256, name="gla_bias_bwd", diff_tiled=[])[1][0][0]
            gw["gla_w_gate_up"][j] = _matmul(s["glr"], dz, name="gla_gate_dw", ta=True, out_dtypes=(F32,),
                                             after=tok)[:GLA_RANK]
            dglr = _matmul(dz, s["w_up"], name="gla_gate_dx", tb=True, out_dtypes=(F32,))
            dw_main = _matmul(dproj, s["x"], name="gla_proj_dw", ta=True, tn=1024, out_dtypes=(F32,))
            dw_lr = _matmul(dglr, s["x"], name="gla_lr_dw", ta=True, tn=1024, out_dtypes=(F32,))[:GLA_RANK]
            layer_grads["gla_w_in_t"] = jnp.concatenate([dw_main, dw_lr], axis=0)
            dx = _matmul(dproj, s["w_main"], name="gla_proj_dx", tn=1024, epi=resid, epi_ins=(d_a1,),
                         out_dtypes=(F32,), after=[dw_main, dw_lr, gw["gla_w_gate_up"][j]])
            dx = _matmul(dglr, s["w_lr"], name="gla_lr_dx", tn=1024, epi=plus, epi_ins=(dx,), out_dtypes=(F32,))
        elif kind == 1:
            dqn, dqr, dkv, dkr = _mla_attn_bwd(s["qn"], s["qr"], s["kv"], s["kr"], du, tok)
            tok = comm.at("bwd", i, "mixer_done", dqn)
            (d_cq,), (dgq, dgkv, dw_uq, dw_ukv) = _tile_bwd(_mla_pre, (s["cq"], cos, sin), s["pre_params"],
                                                           (dqn, dqr, dkv, dkr), (BF16,), tm=256, name="mla_pre_bwd",
                                                           diff_tiled=[0])
            gw["mla_q_norm"][j], gw["mla_kv_norm"][j] = dgq[0], dgkv[0]
            layer_grads["mla_w_uq"] = _uq_from_kernel_layout(dw_uq)
            layer_grads["mla_w_ukv"] = dw_ukv
            layer_grads["mla_w_in"] = _matmul(s["x"], d_cq, name="mla_proj_dw", ta=True, tn=MLA_IN_PAD,
                                              out_dtypes=(F32,), after=tok)[:, :MLA_IN]
            dx = _matmul(d_cq, s["w_in"], name="mla_proj_dx", tb=True, tn=1024, epi=resid, epi_ins=(d_a1,),
                         out_dtypes=(F32,), after=layer_grads["mla_w_in"])
        else:
            db, dc, du_, dcw = _conv_bwd(s["bcu"], w["conv_w"][j], du, tok)
            tok = comm.at("bwd", i, "mixer_done", db)
            gw["conv_w"][j] = dcw
            dbcu = jnp.concatenate([db, dc, du_], axis=1)
            layer_grads["conv"] = _matmul(dbcu, s["x"], name="conv_proj_dw", ta=True, tm=REG_CONV[1], tn=1024,
                                          out_at=REG_CONV, out_buf=lax.empty((N_DEV, REG_CONV[1], D_MODEL), BF16),
                                          after=tok)
            dx = _matmul(dbcu, s["wl"]["conv"], name="conv_proj_dx", tn=1024, b_at=REG_CONV, epi=resid, epi_ins=(d_a1,),
                         out_dtypes=(F32,), after=layer_grads["conv"])
        comm.mixer_grads(i, layer_grads)

    gw["ln_g"] = [jnp.concatenate([a, b], axis=0) for a, b in ln_g_grads]
    gw["ln_b"] = [jnp.concatenate([a, b], axis=0) for a, b in ln_b_grads]
    return loss_part, dx, {n: jnp.stack(gw[n]).astype(F32) for n in gw}


MESH_IDS = pl.DeviceIdType.MESH
ANY = pl.BlockSpec(memory_space=pl.ANY)
HBM_SPEC = pl.BlockSpec(memory_space=pltpu.HBM)
SEM_SPEC = pl.BlockSpec(memory_space=pltpu.SEMAPHORE)
DATAFLOW_EFFECT = pltpu.SideEffectType.DATAFLOW_SIDE_EFFECTING
CORE_COPIES, CHIP_COPIES = 4, 3


def _my_place():
    return lax.axis_index("x"), lax.axis_index("y"), lax.axis_index("c")


def _other_chips(mx, my):
    return [(1 - mx, my), (mx, 1 - my), (1 - mx, 1 - my)]


def _remote(src, dst, send_sems, recv_sems, k, to):
    return pltpu.make_async_remote_copy(src_ref=src, dst_ref=dst, send_sem=send_sems.at[k], recv_sem=recv_sems.at[k],
                                        device_id=to, device_id_type=MESH_IDS)


def _gather_first_copies(n_arr):
    def make(bufs, send_sems, recv_sems):
        mx, my, mc = _my_place()
        mine = 4 * mx + 2 * my + mc
        peers = [(mx, my, 1 - mc)] + [(cx, cy, mc) for cx, cy in _other_chips(mx, my)]
        return [_remote(bufs[a].at[mine], bufs[a].at[mine], send_sems, recv_sems, (1 + CHIP_COPIES) * a + k, to)
                for a in range(n_arr) for k, to in enumerate(peers)]
    return make, (1 + CHIP_COPIES) * n_arr


def _gather_forward_copies(n_arr):
    def make(bufs, send_sems, recv_sems):
        mx, my, mc = _my_place()
        blocks = [4 * cx + 2 * cy + mc for cx, cy in _other_chips(mx, my)]
        return [_remote(bufs[a].at[blk], bufs[a].at[blk], send_sems, recv_sems, CHIP_COPIES * a + k, (mx, my, 1 - mc))
                for a in range(n_arr) for k, blk in enumerate(blocks)]
    return make, CHIP_COPIES * n_arr


def _scatter_core_copies(n_arr):
    def make(bufs, send_sems, recv_sems):
        mx, my, mc = _my_place()
        return [_remote(bufs[a].at[2 * k + (1 - mc)], bufs[n_arr + a].at[k], send_sems, recv_sems, CORE_COPIES * a + k,
                        (mx, my, 1 - mc)) for a in range(n_arr) for k in range(CORE_COPIES)]
    return make, CORE_COPIES * n_arr


def _scatter_chip_copies(n_arr):
    def make(bufs, send_sems, recv_sems):
        mx, my, mc = _my_place()
        return [_remote(bufs[a].at[2 * cx + cy], bufs[n_arr + a].at[k], send_sems, recv_sems, CHIP_COPIES * a + k,
                        (cx, cy, mc)) for a in range(n_arr) for k, (cx, cy) in enumerate(_other_chips(mx, my))]
    return make, CHIP_COPIES * n_arr


def _exchange(name, bufs, copies):
    make, n_copies = copies
    n = len(bufs)

    def body(*refs):
        descs = make(refs[:n], refs[2 * n], refs[2 * n + 1])
        for cp in descs:
            cp.start()
        for cp in descs:
            cp.wait()

    return pl.pallas_call(
        body, name=name, out_shape=[jax.ShapeDtypeStruct(b.shape, b.dtype) for b in bufs], in_specs=[ANY] * n,
        out_specs=[ANY] * n, input_output_aliases={i: i for i in range(n)},
        scratch_shapes=[pltpu.SemaphoreType.DMA((n_copies,)), pltpu.SemaphoreType.DMA((n_copies,))],
    )(*bufs)


def _exchange_start(name, bufs, copies, after):
    make, n_copies = copies
    n = len(bufs)

    def body(*refs):
        for cp in make(refs[:n], refs[n + 1], refs[n + 2]):
            cp.start()
        refs[-1][...] = jnp.zeros_like(refs[-1])

    outs = pl.pallas_call(
        body, name=name,
        out_shape=(pltpu.SemaphoreType.DMA((n_copies,)), pltpu.SemaphoreType.DMA((n_copies,)),
                   *[pltpu.HBM(b.shape, b.dtype) for b in bufs], jax.ShapeDtypeStruct((8, LANES), F32)),
        in_specs=[HBM_SPEC] * n + [ANY],
        out_specs=(SEM_SPEC, SEM_SPEC, *[HBM_SPEC] * n, pl.BlockSpec(memory_space=pltpu.VMEM)),
        input_output_aliases={i: 2 + i for i in range(n)},
        compiler_params=pltpu.CompilerParams(has_side_effects=DATAFLOW_EFFECT),
    )(*[pltpu.with_memory_space_constraint(b, pltpu.HBM) for b in bufs], after)
    return (outs[0], outs[1]), list(outs[2:2 + n]), outs[-1]


def _exchange_wait(name, sems, bufs, copies, after):
    make, _ = copies
    n = len(bufs)

    def body(*refs):
        for cp in make(refs[:n], refs[n], refs[n + 1]):
            cp.wait_send()
            cp.wait_recv()

    return list(pl.pallas_call(
        body, name=name, out_shape=[pltpu.HBM(b.shape, b.dtype) for b in bufs],
        in_specs=[HBM_SPEC] * n + [SEM_SPEC, SEM_SPEC, ANY], out_specs=[HBM_SPEC] * n,
        input_output_aliases={i: i for i in range(n)},
        compiler_params=pltpu.CompilerParams(has_side_effects=DATAFLOW_EFFECT),
    )(*bufs, *sems, after))


SUM_TILE_BYTES = 2 * 1024 * 1024


def _row_tile(r, c):
    best = None
    for cand in range(16, r + 1, 16):
        if r % cand == 0 and cand * c * 2 <= SUM_TILE_BYTES:
            best = cand
    return r if best is None else best


def _pair_sum(g, recv, my_c):
    _, r, c = g.shape
    tr = _row_tile(r, c)

    def body(c_ref, g_ref, r_ref, o_ref):
        o_ref[...] = (g_ref[...].astype(F32) + r_ref[...].astype(F32)).astype(o_ref.dtype)

    return pl.pallas_call(
        body, name="rs_pair_sum", out_shape=jax.ShapeDtypeStruct((4, r, c), g.dtype),
        grid_spec=pltpu.PrefetchScalarGridSpec(
            num_scalar_prefetch=1, grid=(4, r // tr),
            in_specs=[pl.BlockSpec((1, tr, c), lambda n, i, cr: (2 * n + cr[0], i, 0)),
                      pl.BlockSpec((1, tr, c), lambda n, i, cr: (n, i, 0))],
            out_specs=pl.BlockSpec((1, tr, c), lambda n, i, cr: (n, i, 0))),
        compiler_params=_params(),
    )(my_c, g, recv)


def _chip_sum(h, recv, my_chip):
    _, r, c = h.shape
    tr = _row_tile(r, c)

    def body(j_ref, h_ref, r0_ref, r1_ref, r2_ref, o_ref):
        o_ref[...] = ((h_ref[0].astype(F32) + r0_ref[0].astype(F32)) + r1_ref[0].astype(F32)) + r2_ref[0].astype(F32)

    return pl.pallas_call(
        body, name="rs_chip_sum", out_shape=jax.ShapeDtypeStruct((r, c), F32),
        grid_spec=pltpu.PrefetchScalarGridSpec(
            num_scalar_prefetch=1, grid=(r // tr,),
            in_specs=[pl.BlockSpec((1, tr, c), lambda i, jr: (jr[0], i, 0))]
            + [pl.BlockSpec((1, tr, c), lambda i, jr, n=n: (n, i, 0)) for n in range(3)],
            out_specs=pl.BlockSpec((tr, c), lambda i, jr: (i, 0))),
        compiler_params=_params(),
    )(my_chip, h, recv, recv, recv)


def _sum_blocks(g):
    n, r, c = g.shape

    def body(g_ref, o_ref):
        acc = g_ref[0]
        for k in range(1, n):
            acc = acc + g_ref[k]
        o_ref[...] = acc

    return pl.pallas_call(body, name="sum_blocks", out_shape=jax.ShapeDtypeStruct((r, c), F32), compiler_params=_params())(g)


def _pack(flat_parts, cols, row_multiple, dtype):
    flat = jnp.concatenate([f.astype(dtype) for f in flat_parts])
    per_row_block = cols * row_multiple
    padded = -(-flat.shape[0] // per_row_block) * per_row_block
    return jnp.pad(flat, (0, padded - flat.shape[0])).reshape(padded // cols, cols)


def _shard_shape(name):
    shape, axis = WEIGHTS[name]
    if axis is None:
        return shape
    return tuple(s // N_DEV if a == axis else s for a, s in enumerate(shape))


def _size(shape):
    n = 1
    for s in shape:
        n *= s
    return n


def _unshard(blocks, name):
    _, axis = WEIGHTS[name]
    return jnp.concatenate([blocks[k] for k in range(N_DEV)], axis=axis)


def _unpack_blocks(flat, names, lead):
    out, off = {}, 0
    for n in names:
        shp = _shard_shape(n)[1:] if lead else _shard_shape(n)
        out[n] = flat[..., off:off + _size(shp)].reshape(flat.shape[:-1] + shp)
        off += _size(shp)
    return out


def _layer_slabs(shard, i):
    j, kind = i // 3, i % 3
    w_out = (shard["gla_w_out"], shard["mla_w_out"], shard["conv_w_out"])[kind][j]
    out = {"a": jnp.concatenate([shard["mlp_w2"][i], shard["mlp_w1"][i].T, w_out, shard["ple_w_gate"][i]], axis=0).astype(BF16),
           "b": shard["ple_w_proj"][i].T.astype(BF16)}
    if kind == 0:
        out["gla"] = shard["gla_w_in"][j].T.astype(BF16)
    elif kind == 1:
        out["mla"] = _pack([shard[n][j].reshape(-1) for n in MLA_PACKED], PACK_COLS, PACK_ROW_TILE, BF16)
    else:
        out["conv"] = shard["conv_w_in"][j].T.astype(BF16)
    return out


def _mixer_weights(landed, i):
    kind = i % 3
    if kind == 0:
        return {"gla_w_in_t": landed["gla"].reshape(-1, D_MODEL)}
    if kind == 2:
        return {"conv": landed["conv"]}
    blocks = _unpack_blocks(landed["mla"].reshape(N_DEV, -1), MLA_PACKED, lead=True)
    return {n: jnp.concatenate([blocks[n][k] for k in range(N_DEV)], axis=WEIGHTS[n][1] - 1) for n in MLA_PACKED}


def _mixer_grad_buffers(layer_grads, i):
    kind = i % 3
    if kind == 0:
        return {"gla": layer_grads["gla_w_in_t"].reshape(N_DEV, -1, D_MODEL).astype(BF16)}
    if kind == 2:
        return {"conv": layer_grads["conv"]}
    parts = [jnp.stack(jnp.split(layer_grads[n], N_DEV, axis=WEIGHTS[n][1] - 1)).reshape(N_DEV, -1) for n in MLA_PACKED]
    cat = jnp.concatenate(parts, axis=1).astype(BF16)
    per = PACK_COLS * PACK_ROW_TILE
    padded = -(-cat.shape[1] // per) * per
    return {"mla": jnp.pad(cat, ((0, 0), (0, padded - cat.shape[1]))).reshape(N_DEV, padded // PACK_COLS, PACK_COLS)}


SLAB_KEYS = ("a", "b")


class _Overlap:
    def __init__(self, shard, small_pack):
        mx, my, mc = _my_place()
        self.my_c = mc.astype(jnp.int32).reshape(1)
        self.my_chip = (2 * mx + my).astype(jnp.int32).reshape(1)
        mine = 4 * mx + 2 * my + mc
        slabs = [_layer_slabs(shard, i) for i in range(DEPTH)]
        slabs[0]["small"] = small_pack
        self.landing = [{k: lax.dynamic_update_index_in_dim(lax.empty((N_DEV, *v.shape), v.dtype), v, mine, 0)
                         for k, v in slabs[i].items()} for i in range(DEPTH)]
        self.fly = {}
        self.gather_keys = {}
        self.grads = [{} for _ in range(DEPTH)]
        self.reduced = [{} for _ in range(DEPTH)]
        keys = self._keys(self.landing[0], "mixer")
        tok = self._start("ag_first_mixer_l0", [self.landing[0][k] for k in keys], _gather_first_copies(len(keys)), shard["ln_g"])
        tok = self._start("ag_first_slab_l0", [self.landing[0][k] for k in SLAB_KEYS], _gather_first_copies(2), tok)
        bufs = self._wait("ag_first_mixer_l0", tok)
        self.landing[0].update(zip(keys, _exchange("ag_forward_mixer_l0", bufs, _gather_forward_copies(len(bufs)))))

    @staticmethod
    def _keys(names, group):
        return [k for k in names if (k in SLAB_KEYS) == (group == "slab")]

    def _start(self, name, bufs, copies, after):
        sems, bufs, tok = _exchange_start(name + "_start", bufs, copies, after)
        self.fly[name] = (sems, bufs, copies)
        return tok

    def _wait(self, name, after):
        sems, bufs, copies = self.fly.pop(name)
        return _exchange_wait(name + "_wait", sems, bufs, copies, after)

    def mixer_weights(self, i):
        return _mixer_weights(self.landing[i], i)

    def slab_weights(self, i, dep):
        if i == 0:
            bufs = self._wait("ag_first_slab_l0", dep)
            self.landing[0].update(zip(SLAB_KEYS, _exchange("ag_forward_slab_l0", bufs, _gather_forward_copies(2))))
        return self.landing[i]["a"], self.landing[i]["b"]

    def slab_grads(self, i, ga, gb):
        self.grads[i].update(a=ga, b=gb)

    def mixer_grads(self, i, layer_grads):
        self.grads[i].update(_mixer_grad_buffers(layer_grads, i))

    def at(self, phase, i, point, dep):
        toks = []
        if phase == "fwd":
            if point == "begin" and i == 0:
                toks.append(self._gather_first(1, self.landing[0][self._keys(self.landing[0], "mixer")[0]]))
            if point == "mid" and i + 1 < DEPTH:
                bufs = self._wait(f"ag_first_l{i + 1}", dep)
                toks.append(self._start(f"ag_forward_l{i + 1}", bufs, _gather_forward_copies(len(bufs)), dep))
                if i + 2 < DEPTH:
                    toks.append(self._gather_first(i + 2, dep))
            if point == "end" and i + 1 < DEPTH:
                self.landing[i + 1].update(zip(self.gather_keys[i + 1], self._wait(f"ag_forward_l{i + 1}", dep)))
        else:
            if point == "begin" and i + 1 < DEPTH:
                toks.append(self._scatter_cores(i + 1, "mixer", dep))
            if point == "ln" and i + 1 < DEPTH:
                toks.append(self._scatter_chips(i + 1, "mixer", dep))
            if point == "slab_done":
                if i + 1 < DEPTH:
                    self._scatter_done(i + 1, "slab", dep)
                    self._scatter_done(i + 1, "mixer", dep)
                toks.append(self._scatter_cores(i, "slab", dep))
            if point == "mixer_done":
                toks.append(self._scatter_chips(i, "slab", dep))
        return toks or None

    def _gather_first(self, i, after):
        self.gather_keys[i] = list(self.landing[i])
        bufs = [self.landing[i][k] for k in self.gather_keys[i]]
        return self._start(f"ag_first_l{i}", bufs, _gather_first_copies(len(bufs)), after)

    def _scatter_cores(self, i, group, after):
        gs = [self.grads[i][k] for k in self._keys(self.grads[i], group)]
        land = [lax.empty((4, *g.shape[1:]), g.dtype) for g in gs]
        return self._start(f"rs_cores_{group}_l{i}", gs + land, _scatter_core_copies(len(gs)), after)

    def _pair_sums(self, bufs):
        n = len(bufs) // 2
        hs = [_pair_sum(g, r, self.my_c) for g, r in zip(bufs[:n], bufs[n:])]
        return hs + [lax.empty((3, *h.shape[1:]), h.dtype) for h in hs]

    def _scatter_chips(self, i, group, after):
        bufs = self._pair_sums(self._wait(f"rs_cores_{group}_l{i}", after))
        return self._start(f"rs_chips_{group}_l{i}", bufs, _scatter_chip_copies(len(bufs) // 2), after)

    def _chip_sums(self, i, group, bufs):
        n = len(bufs) // 2
        for k, h, r in zip(self._keys(self.grads[i], group), bufs[:n], bufs[n:]):
            self.reduced[i][k] = _chip_sum(h, r, self.my_chip)

    def _scatter_done(self, i, group, after):
        self._chip_sums(i, group, self._wait(f"rs_chips_{group}_l{i}", after))

    def finish(self, dep):
        self._scatter_done(0, "slab", dep)
        gs = [self.grads[0][k] for k in self._keys(self.grads[0], "mixer")]
        bufs = _exchange("rs_cores_mixer_l0", gs + [lax.empty((4, *g.shape[1:]), g.dtype) for g in gs],
                         _scatter_core_copies(len(gs)))
        bufs = _exchange("rs_chips_mixer_l0", self._pair_sums(bufs), _scatter_chip_copies(len(gs)))
        self._chip_sums(0, "mixer", bufs)
        return self.reduced


def _all_gather_small(x, name):
    mx, my, mc = _my_place()
    land = lax.dynamic_update_index_in_dim(lax.empty((N_DEV, *x.shape), x.dtype), x, 4 * mx + 2 * my + mc, 0)
    (land,) = _exchange(name + "_first", [land], _gather_first_copies(1))
    (land,) = _exchange(name + "_forward", [land], _gather_forward_copies(1))
    return land


def _shard_grads(reduced):
    def rows(a, reg):
        return a[reg[0] * reg[1]:(reg[0] + 1) * reg[1]]

    a = [reduced[i]["a"] for i in range(DEPTH)]
    w_out = [rows(a[i], REG_WOUT) for i in range(DEPTH)]
    mla = _unpack_blocks(reduced[1]["mla"].reshape(-1), MLA_PACKED, lead=True)
    out = {
        "mlp_w2": jnp.stack([rows(a[i], REG_W2) for i in range(DEPTH)]),
        "mlp_w1": jnp.stack([rows(a[i], REG_W1T).T for i in range(DEPTH)]),
        "gla_w_out": jnp.stack([w_out[0], w_out[3]]), "mla_w_out": w_out[1][None], "conv_w_out": w_out[2][None],
        "ple_w_gate": jnp.stack([rows(a[i], REG_WG) for i in range(DEPTH)]),
        "ple_w_proj": jnp.stack([reduced[i]["b"].T for i in range(DEPTH)]),
        "conv_w_in": reduced[2]["conv"].T[None],
        "gla_w_in": jnp.stack([reduced[0]["gla"].T, reduced[3]["gla"].T]),
    }
    out.update({n: mla[n][None] for n in MLA_PACKED})
    return out


def _adamw(w, g, m, v, name):
    shape = w.shape
    cols = shape[-1]
    rows = _size(shape) // cols
    tr = rows
    for cand in (512, 256, 128, 64, 32, 16, 8):
        if rows > cand and rows % cand == 0:
            tr = cand
            break

    def body(w_ref, g_ref, m_ref, v_ref, d_ref, mo_ref, vo_ref):
        gv = g_ref[...]
        m2 = ADAM_B1 * m_ref[...] + (1.0 - ADAM_B1) * gv
        v2 = ADAM_B2 * v_ref[...] + (1.0 - ADAM_B2) * (gv * gv)
        m_hat = m2 / (1.0 - ADAM_B1 ** ADAM_STEP)
        v_hat = v2 / (1.0 - ADAM_B2 ** ADAM_STEP)
        d_ref[...] = -ADAM_LR * (m_hat / (jnp.sqrt(v_hat) + ADAM_EPS) + ADAM_WD * w_ref[...])
        mo_ref[...] = m2
        vo_ref[...] = v2

    spec = pl.BlockSpec((tr, cols), lambda i: (i, 0))
    outs = pl.pallas_call(
        body, name="adamw_" + name, grid=(rows // tr,), in_specs=[spec] * 4, out_specs=[spec] * 3,
        out_shape=[jax.ShapeDtypeStruct((rows, cols), F32)] * 3, compiler_params=_params(),
    )(*[a.reshape(rows, cols) for a in (w, g, m, v)])
    return [o.reshape(shape) for o in outs]


def kernel(x, p, positions, gla_w_in, gla_w_gate_up, gla_b_gate, gla_norm_g, gla_w_out, mla_w_in, mla_q_norm, mla_kv_norm, mla_w_uq, mla_w_ukv, mla_w_out, conv_w_in, conv_w, conv_w_out, ln_g, ln_b, mlp_w1, mlp_w2, ple_w_gate, ple_w_proj, loss_target, m_gla_w_in, m_gla_w_gate_up, m_gla_b_gate, m_gla_norm_g, m_gla_w_out, m_mla_w_in, m_mla_q_norm, m_mla_kv_norm, m_mla_w_uq, m_mla_w_ukv, m_mla_w_out, m_conv_w_in, m_conv_w, m_conv_w_out, m_ln_g, m_ln_b, m_mlp_w1, m_mlp_w2, m_ple_w_gate, m_ple_w_proj, v_gla_w_in, v_gla_w_gate_up, v_gla_b_gate, v_gla_norm_g, v_gla_w_out, v_mla_w_in, v_mla_q_norm, v_mla_kv_norm, v_mla_w_uq, v_mla_w_ukv, v_mla_w_out, v_conv_w_in, v_conv_w, v_conv_w_out, v_ln_g, v_ln_b, v_mlp_w1, v_mlp_w2, v_ple_w_gate, v_ple_w_proj):
    args = locals()
    shard = {n: args[n] for n in WEIGHT_NAMES}
    mom = {n: args["m_" + n] for n in WEIGHT_NAMES}
    var = {n: args["v_" + n] for n in WEIGHT_NAMES}
    mx, my, mc = _my_place()

    comm = _Overlap(shard, _pack([shard[n].reshape(-1) for n in SMALL], LANES, 8, F32))
    small_all = comm.landing[0]["small"]
    small = {n: shard[n] for n in REPLICATED}
    small.update({n: _unshard(blk, n) for n, blk in _unpack_blocks(small_all.reshape(N_DEV, -1), SMALL, lead=False).items()})
    loss_part, grad_x, small_grads = _step(x[0], p[:, 0], positions[0], loss_target[0], small, comm)
    my_grads = _shard_grads(comm.finish(grad_x))

    small_parts = [loss_part[0, :1]] + [small_grads[n].reshape(-1) for n in SMALL + REPLICATED]
    red_small = _sum_blocks(_all_gather_small(_pack(small_parts, LANES, 8, F32), "ag_small_grads")).reshape(-1)
    loss = red_small[0]
    off = 1
    dev = 4 * mx + 2 * my + mc
    for n in SMALL + REPLICATED:
        shape, axis = WEIGHTS[n]
        full_g = red_small[off:off + _size(shape)].reshape(shape)
        off += _size(shape)
        if axis is None:
            my_grads[n] = full_g
        else:
            width = shape[axis] // N_DEV
            my_grads[n] = lax.dynamic_slice_in_dim(full_g, dev * width, width, axis=axis)

    deltas, new_m, new_v = {}, {}, {}
    for n in WEIGHT_NAMES:
        deltas[n], new_m[n], new_v[n] = _adamw(shard[n], my_grads[n], mom[n], var[n], n)
    return (loss, grad_x[None], *[my_grads[n] for n in WEIGHT_NAMES], *[deltas[n] for n in WEIGHT_NAMES],
            *[new_m[n] for n in WEIGHT_NAMES], *[new_v[n] for n in WEIGHT_NAMES])
```

```python
import functools

import jax
import jax.numpy as jnp
from jax import lax
from jax.experimental import pallas as pl
from jax.experimental.pallas import tpu as pltpu

F32, BF16 = jnp.float32, jnp.bfloat16
HIGHEST = lax.Precision.HIGHEST
MESH_AXES = ("x", "y", "c")
N_DEV = 8

D_MODEL = 1024
SEQ = 2048
DEPTH = 4
CHUNK = 64
ALPHA = (2 * DEPTH) ** 0.25
LN_EPS = 1e-5
RMS_EPS = 1e-6
PLE_DIM = 256
D_FF = 4 * D_MODEL
GLA_HEADS = 4
GLA_DK = 128
GLA_DV = 256
GLA_RANK = 16
GLA_TAU = 16.0
GLA_HK = GLA_HEADS * GLA_DK
GLA_HV = GLA_HEADS * GLA_DV
GLA_MAIN = 2 * GLA_HK + GLA_HV + D_MODEL
MLA_HEADS = 8
MLA_NOPE = 128
MLA_ROPE = 64
MLA_V = 128
MLA_RANK = 256
MLA_IN = 2 * MLA_RANK + MLA_ROPE
MLA_IN_PAD = 640
ROPE_BASE = 10000.0
LANES = 128
ADAM_LR, ADAM_B1, ADAM_B2, ADAM_EPS, ADAM_WD, ADAM_STEP = 0.001, 0.9, 0.999, 1e-08, 0.01, 10

V7X_VMEM_LIMIT_BYTES = 56 * 1024 * 1024
PACK_COLS = 1024
PACK_ROW_TILE = 256

WEIGHTS = {
    "gla_w_in": ((2, 1024, 3088), 2), "gla_w_gate_up": ((2, 16, 512), 2), "gla_b_gate": ((2, 512), 1),
    "gla_norm_g": ((2, 256), 1), "gla_w_out": ((2, 1024, 1024), 1), "mla_w_in": ((1, 1024, 576), 1),
    "mla_q_norm": ((1, 256), None), "mla_kv_norm": ((1, 256), None), "mla_w_uq": ((1, 256, 1536), 2),
    "mla_w_ukv": ((1, 256, 2048), 2), "mla_w_out": ((1, 1024, 1024), 1), "conv_w_in": ((1, 1024, 3072), 2),
    "conv_w": ((1, 3, 1024), 2), "conv_w_out": ((1, 1024, 1024), 1), "ln_g": ((4, 2, 1024), 2),
    "ln_b": ((4, 2, 1024), 2), "mlp_w1": ((4, 1024, 4096), 2), "mlp_w2": ((4, 4096, 1024), 1),
    "ple_w_gate": ((4, 1024, 1024), 1), "ple_w_proj": ((4, 256, 1024), 2),
}
WEIGHT_NAMES = list(WEIGHTS)
REG_W2, REG_W1T, REG_WOUT, REG_WG = (0, 512), (1, 512), (8, 128), (9, 128)
A_ROWS = 1280
REG_CONV = (0, 384)
REG_WPT = (0, 128)
MLA_PACKED = ["mla_w_in", "mla_w_uq", "mla_w_ukv"]
SMALL = ["gla_w_gate_up", "gla_b_gate", "gla_norm_g", "conv_w", "ln_g", "ln_b"]
REPLICATED = ["mla_q_norm", "mla_kv_norm"]


def _params(**kw):
    return pltpu.CompilerParams(vmem_limit_bytes=V7X_VMEM_LIMIT_BYTES, **kw)


def _dot(a, b, ca, cb, precision=None):
    return lax.dot_general(a, b, (((ca,), (cb,)), ((), ())), precision=precision, preferred_element_type=F32)


def _nn(a, b):
    return _dot(a.astype(BF16), b.astype(BF16), 1, 0)


def _nt(a, b):
    return _dot(a.astype(BF16), b.astype(BF16), 1, 1)


def _tn(a, b):
    return _dot(a.astype(BF16), b.astype(BF16), 0, 0)


@jax.custom_vjp
def mm_nn(a, b):
    return _nn(a, b)


def _mm_nn_fwd(a, b):
    return _nn(a, b), (a, b)


def _mm_nn_bwd(res, g):
    a, b = res
    return _nt(g, b).astype(a.dtype), _tn(a, g).astype(b.dtype)


mm_nn.defvjp(_mm_nn_fwd, _mm_nn_bwd)


@jax.custom_vjp
def mm_nt(a, b):
    return _nt(a, b)


def _mm_nt_fwd(a, b):
    return _nt(a, b), (a, b)


def _mm_nt_bwd(res, g):
    a, b = res
    return _nn(g, b).astype(a.dtype), _tn(g, a).astype(b.dtype)


mm_nt.defvjp(_mm_nt_fwd, _mm_nt_bwd)


@jax.custom_vjp
def mm_tn(a, b):
    return _tn(a, b)


def _mm_tn_fwd(a, b):
    return _tn(a, b), (a, b)


def _mm_tn_bwd(res, g):
    a, b = res
    return _nt(b, g).astype(a.dtype), _nn(a, g).astype(b.dtype)


mm_tn.defvjp(_mm_tn_fwd, _mm_tn_bwd)


def _iota2(shape, dim):
    return lax.broadcasted_iota(jnp.int32, shape, dim)


@jax.custom_vjp
def cumsum_rows(x):
    n = x.shape[0]
    tri = (_iota2((n, n), 0) >= _iota2((n, n), 1)).astype(F32)
    return _dot(tri, x, 1, 0, precision=HIGHEST)


def _cumsum_fwd(x):
    return cumsum_rows(x), None


def _cumsum_bwd(_, g):
    n = g.shape[0]
    tri_t = (_iota2((n, n), 0) <= _iota2((n, n), 1)).astype(F32)
    return (_dot(tri_t, g, 1, 0, precision=HIGHEST),)


cumsum_rows.defvjp(_cumsum_fwd, _cumsum_bwd)


def _rot_matrix(transposed):
    i, j = _iota2((LANES, LANES), 0), _iota2((LANES, LANES), 1)
    if transposed:
        i, j = j, i
    half = MLA_ROPE // 2
    plus = (i == j - half) & (j >= half) & (j < MLA_ROPE)
    minus = (i == j + half) & (j < half)
    return plus.astype(F32) - minus.astype(F32)


@jax.custom_vjp
def rot_half(x):
    return _dot(x, _rot_matrix(False), 1, 0, precision=HIGHEST)


def _rot_fwd(x):
    return rot_half(x), None


def _rot_bwd(_, g):
    return (_dot(g, _rot_matrix(True), 1, 0, precision=HIGHEST),)


rot_half.defvjp(_rot_fwd, _rot_bwd)


def _shift_rows_raw(x, s):
    n = x.shape[0]
    row = _iota2(x.shape, 0)
    rolled = pltpu.roll(x, s % n, 0)
    keep = (row >= s) if s > 0 else (row < n + s)
    return jnp.where(keep, rolled, 0.0)


@functools.partial(jax.custom_vjp, nondiff_argnums=(1,))
def shift_rows(x, s):
    return _shift_rows_raw(x, s)


def _shift_fwd(x, s):
    return _shift_rows_raw(x, s), None


def _shift_bwd(s, _, g):
    return (_shift_rows_raw(g, -s),)


shift_rows.defvjp(_shift_fwd, _shift_bwd)


def _layer_norm(a, g, b):
    mu = jnp.mean(a, -1, keepdims=True)
    xc = a - mu
    var = jnp.mean(xc * xc, -1, keepdims=True)
    return xc * lax.rsqrt(var + LN_EPS) * g + b


def _rms_norm(a, g):
    return a * lax.rsqrt(jnp.mean(a * a, -1, keepdims=True) + RMS_EPS) * g


def _log_sigmoid(z):
    return jnp.minimum(z, 0.0) - jnp.log(1.0 + jnp.exp(-jnp.abs(z)))


def _matmul(a, b, *, name, ta=False, tb=False, tm=512, tn=512, a_fn=None, epi=None, epi_ins=(), out_dtypes=(BF16,),
            b_at=None, out_at=None, out_buf=None, after=None, n_row_sums=0):
    m = a.shape[1] if ta else a.shape[0]
    k = a.shape[0] if ta else a.shape[1]
    if b_at is None:
        n, kb = (b.shape[0], b.shape[1]) if tb else (b.shape[1], b.shape[0])
    else:
        rb, r = b_at
        n, kb = (N_DEV * r, b.shape[2]) if tb else (b.shape[2], N_DEV * r)
    assert kb == k, (name, a.shape, b.shape, k, kb)
    tm, tn = min(tm, m), min(tn, n)
    assert m % tm == 0 and n % tn == 0, (name, m, n, tm, tn)
    a_spec = pl.BlockSpec((k, tm), lambda i, j: (0, i)) if ta else pl.BlockSpec((tm, k), lambda i, j: (i, 0))
    if b_at is None:
        b_spec = pl.BlockSpec((tn, k), lambda i, j: (j, 0)) if tb else pl.BlockSpec((k, tn), lambda i, j: (0, j))
        load_b = lambda ref: ref[...]
    elif tb and tn == n:
        b_spec = pl.BlockSpec((N_DEV, r, k), lambda i, j: (0, rb, 0))
        load_b = lambda ref: ref[...].reshape(n, k)
    elif tb:
        assert tn == r, (name, tn, r)
        b_spec = pl.BlockSpec((1, r, k), lambda i, j: (j, rb, 0))
        load_b = lambda ref: ref[0]
    else:
        b_spec = pl.BlockSpec((N_DEV, r, tn), lambda i, j: (0, rb, j))
        load_b = lambda ref: ref[...].reshape(k, tn)
    e_specs = []
    for e in epi_ins:
        if e.shape == (1, n):
            e_specs.append(pl.BlockSpec((1, tn), lambda i, j: (0, j)))
        else:
            assert e.shape == (m, n), (name, e.shape, m, n)
            e_specs.append(pl.BlockSpec((tm, tn), lambda i, j: (i, j)))
    n_epi = len(epi_ins)
    ca, cb = (0 if ta else 1), (1 if tb else 0)
    operands = [a, b, *epi_ins]
    in_specs = [a_spec, b_spec, *e_specs]
    if out_at is None:
        assert n_row_sums == 0 or tn == n, (name, tn, n)
        out_specs = [pl.BlockSpec((tm, tn), lambda i, j: (i, j)) for _ in out_dtypes]
        out_specs += [pl.BlockSpec((1, n), lambda i, j: (0, 0))] * n_row_sums
        out_shape = [jax.ShapeDtypeStruct((m, n), dt) for dt in out_dtypes]
        out_shape += [jax.ShapeDtypeStruct((1, n), F32)] * n_row_sums
        aliases, n_buf = {}, 0
    else:
        orb, orows = out_at
        assert len(out_dtypes) == 1 and orows % tm == 0 and m == N_DEV * orows and n == out_buf.shape[2], (name, m, n)
        per = orows // tm
        out_specs = [pl.BlockSpec((1, tm, tn), lambda i, j: (i // per, orb * per + i % per, j))]
        out_shape = [jax.ShapeDtypeStruct(out_buf.shape, out_buf.dtype)]
        operands.append(out_buf)
        in_specs.append(pl.BlockSpec(memory_space=pl.ANY))
        aliases, n_buf = {len(operands) - 1: 0}, 1
    for dep in ([] if after is None else after if isinstance(after, (list, tuple)) else [after]):
        if dep is not None:
            operands.append(dep)
            in_specs.append(pl.BlockSpec(memory_space=pl.ANY))
            n_buf += 1

    def body(a_ref, b_ref, *rest):
        av = a_ref[...]
        if a_fn is not None:
            av = a_fn(av)
        acc = _dot(av.astype(BF16), load_b(b_ref).astype(BF16), ca, cb)
        outs = epi(acc, *[r_[...] for r_ in rest[:n_epi]]) if epi is not None else (acc,)
        o_refs = rest[n_epi + n_buf:]
        n_tiles = len(o_refs) - n_row_sums
        for o_ref, val in zip(o_refs[:n_tiles], outs):
            o_ref[...] = val.astype(o_ref.dtype).reshape(o_ref.shape)
        if n_row_sums:
            @pl.when(pl.program_id(0) == 0)
            def _():
                for o_ref in o_refs[n_tiles:]:
                    o_ref[...] = jnp.zeros_like(o_ref)

            for o_ref, val in zip(o_refs[n_tiles:], outs[n_tiles:]):
                o_ref[...] += val

    outs = pl.pallas_call(
        body, name=name, grid=(m // tm, n // tn), in_specs=in_specs, out_specs=out_specs, out_shape=out_shape,
        input_output_aliases=aliases, compiler_params=_params(),
    )(*operands)
    return outs[0] if len(outs) == 1 else tuple(outs)


def _tile_fwd(f, tiled, params, out_dtypes, *, tm, name):
    t = tiled[0].shape[0]
    assert t % tm == 0
    out_avals = jax.eval_shape(f, *[jax.ShapeDtypeStruct((tm, x.shape[1]), F32) for x in tiled],
                               *[jax.ShapeDtypeStruct(p.shape, F32) for p in params])
    nt, npar = len(tiled), len(params)

    def body(*refs):
        ins = [r[...].astype(F32) for r in refs[:nt + npar]]
        outs = f(*ins)
        for o_ref, val in zip(refs[nt + npar:], outs):
            o_ref[...] = val.astype(o_ref.dtype)

    return pl.pallas_call(
        body, name=name, grid=(t // tm,),
        in_specs=[pl.BlockSpec((tm, x.shape[1]), lambda i: (i, 0)) for x in tiled]
        + [pl.BlockSpec(p.shape, lambda i: (0, 0)) for p in params],
        out_specs=[pl.BlockSpec((tm, o.shape[1]), lambda i: (i, 0)) for o in out_avals],
        out_shape=[jax.ShapeDtypeStruct((t, o.shape[1]), dt) for o, dt in zip(out_avals, out_dtypes)],
        compiler_params=_params(),
    )(*tiled, *params)


def _tile_bwd(f, tiled, params, cots, d_tiled_dtypes, *, tm, name, diff_tiled=None):
    t = tiled[0].shape[0]
    assert t % tm == 0
    nt, npar, nc = len(tiled), len(params), len(cots)
    diff_tiled = list(range(nt)) if diff_tiled is None else diff_tiled

    def body(*refs):
        ins = [r[...].astype(F32) for r in refs[:nt + npar]]
        cts = [r[...].astype(F32) for r in refs[nt + npar:nt + npar + nc]]
        o_refs = refs[nt + npar + nc:]
        _, vjp = jax.vjp(f, *ins)
        grads = vjp(tuple(cts))
        for o_ref, idx in zip(o_refs[:len(diff_tiled)], diff_tiled):
            o_ref[...] = grads[idx].astype(o_ref.dtype)
        p_refs = o_refs[len(diff_tiled):]

        @pl.when(pl.program_id(0) == 0)
        def _():
            for p_ref in p_refs:
                p_ref[...] = jnp.zeros_like(p_ref)

        for p_ref, gp in zip(p_refs, grads[nt:]):
            p_ref[...] += gp

    outs = pl.pallas_call(
        body, name=name, grid=(t // tm,),
        in_specs=[pl.BlockSpec((tm, x.shape[1]), lambda i: (i, 0)) for x in tiled]
        + [pl.BlockSpec(p.shape, lambda i: (0, 0)) for p in params]
        + [pl.BlockSpec((tm, c.shape[1]), lambda i: (i, 0)) for c in cots],
        out_specs=[pl.BlockSpec((tm, tiled[idx].shape[1]), lambda i: (i, 0)) for idx in diff_tiled]
        + [pl.BlockSpec(p.shape, lambda i: (0, 0)) for p in params],
        out_shape=[jax.ShapeDtypeStruct(tiled[idx].shape, dt) for idx, dt in zip(diff_tiled, d_tiled_dtypes)]
        + [jax.ShapeDtypeStruct(p.shape, F32) for p in params],
        compiler_params=_params(),
    )(*tiled, *params, *cots)
    return outs[:len(diff_tiled)], outs[len(diff_tiled):]


def _gla_head(q, k, v, r, z, g, st):
    c = q.shape[0]
    causal = _iota2((c, c), 0) >= _iota2((c, c), 1)
    la = _log_sigmoid(z) * (1.0 / GLA_TAU)
    big_l = cumsum_rows(la)
    ep, en = jnp.exp(big_l), jnp.exp(-big_l)
    qs = q * (GLA_DK ** -0.5)
    qp = qs * ep
    s = jnp.where(causal, mm_nt(qp, k * en), mm_nt(qs * en, k * ep))
    o = mm_nn(s, v) + mm_nt(qp, st)
    l_end = jnp.sum(la, axis=0, keepdims=True)
    st_new = st * jnp.exp(l_end) + mm_tn(v, k * jnp.exp(l_end - big_l))
    u = _rms_norm(o, g) * (r * jax.nn.sigmoid(r))
    return u, st_new


def _gla_slices(h):
    q = slice(GLA_DK * h, GLA_DK * (h + 1))
    k = slice(GLA_HK + GLA_DK * h, GLA_HK + GLA_DK * (h + 1))
    v = slice(2 * GLA_HK + GLA_DV * h, 2 * GLA_HK + GLA_DV * (h + 1))
    r = slice(2 * GLA_HK + GLA_HV + GLA_DV * h, 2 * GLA_HK + GLA_HV + GLA_DV * (h + 1))
    return q, k, v, r


GLA_CHUNKS_PER_STEP = 2


def _gla_fwd(proj, z, norm_g):
    t = proj.shape[0]
    nc, per = t // CHUNK, GLA_CHUNKS_PER_STEP
    rows_per_step = per * CHUNK

    def body(proj_ref, z_ref, g_ref, u_ref, st_save_ref, st_ref):
        @pl.when(pl.program_id(0) == 0)
        def _():
            st_ref[...] = jnp.zeros_like(st_ref)

        g = g_ref[...]
        for h in range(GLA_HEADS):
            sq, sk, sv, sr = _gla_slices(h)
            st = st_ref[h]
            for c in range(per):
                rows = slice(c * CHUNK, (c + 1) * CHUNK)
                st_save_ref[c, h] = st
                u, st = _gla_head(proj_ref[rows, sq].astype(F32), proj_ref[rows, sk].astype(F32),
                                  proj_ref[rows, sv].astype(F32), proj_ref[rows, sr].astype(F32),
                                  z_ref[rows, GLA_DK * h:GLA_DK * (h + 1)], g, st)
                u_ref[rows, GLA_DV * h:GLA_DV * (h + 1)] = u.astype(u_ref.dtype)
            st_ref[h] = st

    return pl.pallas_call(
        body, name="gla_fwd", grid=(nc // per,),
        in_specs=[pl.BlockSpec((rows_per_step, GLA_MAIN), lambda i: (i, 0)),
                  pl.BlockSpec((rows_per_step, GLA_HK), lambda i: (i, 0)), pl.BlockSpec((1, GLA_DV), lambda i: (0, 0))],
        out_specs=[pl.BlockSpec((rows_per_step, GLA_HV), lambda i: (i, 0)),
                   pl.BlockSpec((per, GLA_HEADS, GLA_DV, GLA_DK), lambda i: (i, 0, 0, 0))],
        out_shape=[jax.ShapeDtypeStruct((t, GLA_HV), BF16), jax.ShapeDtypeStruct((nc, GLA_HEADS, GLA_DV, GLA_DK), F32)],
        scratch_shapes=[pltpu.VMEM((GLA_HEADS, GLA_DV, GLA_DK), F32)],
        compiler_params=_params(),
    )(proj, z, norm_g)


def _gla_bwd(proj, z, norm_g, states, du, after):
    t = proj.shape[0]
    nc, per = t // CHUNK, GLA_CHUNKS_PER_STEP
    rows_per_step = per * CHUNK
    n_steps = nc // per
    after = [a for a in after if a is not None]

    def body(proj_ref, z_ref, g_ref, st_in_ref, du_ref, *rest):
        dproj_ref, dz_ref, dg_ref, dst_ref = rest[len(after):]

        @pl.when(pl.program_id(0) == 0)
        def _():
            dst_ref[...] = jnp.zeros_like(dst_ref)
            dg_ref[...] = jnp.zeros_like(dg_ref)

        g = g_ref[...]
        for h in range(GLA_HEADS):
            sq, sk, sv, sr = _gla_slices(h)
            dst = dst_ref[h]
            for c in reversed(range(per)):
                rows = slice(c * CHUNK, (c + 1) * CHUNK)
                ins = (proj_ref[rows, sq].astype(F32), proj_ref[rows, sk].astype(F32), proj_ref[rows, sv].astype(F32),
                       proj_ref[rows, sr].astype(F32), z_ref[rows, GLA_DK * h:GLA_DK * (h + 1)], g, st_in_ref[c, h])
                _, vjp = jax.vjp(_gla_head, *ins)
                dq, dk, dv, dr, dz, dg, dst = vjp((du_ref[rows, GLA_DV * h:GLA_DV * (h + 1)], dst))
                dproj_ref[rows, sq] = dq.astype(dproj_ref.dtype)
                dproj_ref[rows, sk] = dk.astype(dproj_ref.dtype)
                dproj_ref[rows, sv] = dv.astype(dproj_ref.dtype)
                dproj_ref[rows, sr] = dr.astype(dproj_ref.dtype)
                dz_ref[rows, GLA_DK * h:GLA_DK * (h + 1)] = dz
                dg_ref[...] += dg
            dst_ref[h] = dst

    rev = lambda i: (n_steps - 1 - i, 0)
    return pl.pallas_call(
        body, name="gla_bwd", grid=(n_steps,),
        in_specs=[pl.BlockSpec((rows_per_step, GLA_MAIN), rev), pl.BlockSpec((rows_per_step, GLA_HK), rev),
                  pl.BlockSpec((1, GLA_DV), lambda i: (0, 0)),
                  pl.BlockSpec((per, GLA_HEADS, GLA_DV, GLA_DK), lambda i: (n_steps - 1 - i, 0, 0, 0)),
                  pl.BlockSpec((rows_per_step, GLA_HV), rev)] + [pl.BlockSpec(memory_space=pl.ANY)] * len(after),
        out_specs=[pl.BlockSpec((rows_per_step, GLA_MAIN), rev), pl.BlockSpec((rows_per_step, GLA_HK), rev),
                   pl.BlockSpec((1, GLA_DV), lambda i: (0, 0))],
        out_shape=[jax.ShapeDtypeStruct((t, GLA_MAIN), BF16), jax.ShapeDtypeStruct((t, GLA_HK), F32),
                   jax.ShapeDtypeStruct((1, GLA_DV), F32)],
        scratch_shapes=[pltpu.VMEM((GLA_HEADS, GLA_DV, GLA_DK), F32)],
        compiler_params=_params(),
    )(proj, z, norm_g, states, du, *after)


def _mla_pre(cq, cos, sin, gq, gkv, w_uq, w_ukv):
    qlat = _rms_norm(cq[:, :MLA_RANK], gq)
    kvlat = _rms_norm(cq[:, MLA_RANK:2 * MLA_RANK], gkv)
    kr = cq[:, 2 * MLA_RANK:]
    scale = (MLA_NOPE + MLA_ROPE) ** -0.5
    q = mm_nn(qlat, w_uq) * scale
    kv = mm_nn(kvlat, w_ukv)
    n_nope = MLA_HEADS * MLA_NOPE
    ropes = []
    for h in range(MLA_HEADS):
        qr = q[:, n_nope + LANES * h:n_nope + LANES * (h + 1)]
        ropes.append(qr * cos + rot_half(qr) * sin)
    return q[:, :n_nope], jnp.concatenate(ropes, axis=1), kv, kr * cos + rot_half(kr) * sin


MLA_Q_TILE = 256


def _mla_attn_block(qn, qr, kv, kr, q0):
    tq, nk = qn.shape[0], kv.shape[0]
    s = mm_nt(qn, kv[:, :MLA_NOPE]) + mm_nt(qr, kr)
    visible = (_iota2((tq, nk), 1) // CHUNK) <= ((q0 + _iota2((tq, nk), 0)) // CHUNK)
    s = jnp.where(visible, s, -1e30)
    e = jnp.exp(s - jnp.max(s, -1, keepdims=True))
    p = e / jnp.sum(e, -1, keepdims=True)
    return mm_nn(p, kv[:, MLA_NOPE:])


def _mla_attn_fwd(qn, qr, kv, kr):
    t = qn.shape[0]

    def body(qn_ref, qr_ref, kv_ref, kr_ref, o_ref):
        for i in range(t // MLA_Q_TILE):
            rows = slice(i * MLA_Q_TILE, (i + 1) * MLA_Q_TILE)
            keys = slice(0, (i + 1) * MLA_Q_TILE)
            o = _mla_attn_block(qn_ref[rows, :].astype(F32), qr_ref[rows, :].astype(F32), kv_ref[keys, :].astype(F32),
                                kr_ref[keys, :].astype(F32), i * MLA_Q_TILE)
            o_ref[rows, :] = o.astype(o_ref.dtype)

    return pl.pallas_call(
        body, name="mla_attn_fwd", grid=(MLA_HEADS,),
        in_specs=[pl.BlockSpec((t, MLA_NOPE), lambda h: (0, h)), pl.BlockSpec((t, LANES), lambda h: (0, h)),
                  pl.BlockSpec((t, MLA_NOPE + MLA_V), lambda h: (0, h)), pl.BlockSpec((t, LANES), lambda h: (0, 0))],
        out_specs=pl.BlockSpec((t, MLA_V), lambda h: (0, h)),
        out_shape=jax.ShapeDtypeStruct((t, MLA_HEADS * MLA_V), BF16),
        compiler_params=_params(),
    )(qn, qr, kv, kr)


def _mla_attn_bwd(qn, qr, kv, kr, do, after):
    t = qn.shape[0]
    after = [a for a in after if a is not None]

    def body(qn_ref, qr_ref, kv_ref, kr_ref, do_ref, *rest):
        dqn_ref, dqr_ref, dkv_ref, dkr_ref = rest[len(after):]
        dkv_ref[...] = jnp.zeros_like(dkv_ref)

        @pl.when(pl.program_id(0) == 0)
        def _():
            dkr_ref[...] = jnp.zeros_like(dkr_ref)

        for i in range(t // MLA_Q_TILE):
            rows = slice(i * MLA_Q_TILE, (i + 1) * MLA_Q_TILE)
            keys = slice(0, (i + 1) * MLA_Q_TILE)
            f = functools.partial(_mla_attn_block, q0=i * MLA_Q_TILE)
            _, vjp = jax.vjp(f, qn_ref[rows, :].astype(F32), qr_ref[rows, :].astype(F32), kv_ref[keys, :].astype(F32),
                             kr_ref[keys, :].astype(F32))
            dqn, dqr, dkv, dkr = vjp(do_ref[rows, :].astype(F32))
            dqn_ref[rows, :] = dqn
            dqr_ref[rows, :] = dqr
            dkv_ref[keys, :] += dkv
            dkr_ref[keys, :] += dkr

    return pl.pallas_call(
        body, name="mla_attn_bwd", grid=(MLA_HEADS,),
        in_specs=[pl.BlockSpec((t, MLA_NOPE), lambda h: (0, h)), pl.BlockSpec((t, LANES), lambda h: (0, h)),
                  pl.BlockSpec((t, MLA_NOPE + MLA_V), lambda h: (0, h)), pl.BlockSpec((t, LANES), lambda h: (0, 0)),
                  pl.BlockSpec((t, MLA_V), lambda h: (0, h))] + [pl.BlockSpec(memory_space=pl.ANY)] * len(after),
        out_specs=[pl.BlockSpec((t, MLA_NOPE), lambda h: (0, h)), pl.BlockSpec((t, LANES), lambda h: (0, h)),
                   pl.BlockSpec((t, MLA_NOPE + MLA_V), lambda h: (0, h)), pl.BlockSpec((t, LANES), lambda h: (0, 0))],
        out_shape=[jax.ShapeDtypeStruct(qn.shape, F32), jax.ShapeDtypeStruct(qr.shape, F32),
                   jax.ShapeDtypeStruct(kv.shape, F32), jax.ShapeDtypeStruct(kr.shape, F32)],
        compiler_params=_params(),
    )(qn, qr, kv, kr, do, *after)


def _rope_tables(pos_col, inv_freq_row):
    t = pos_col.shape[0]

    def body(pos_ref, f_ref, cos_ref, sin_ref):
        ang = pos_ref[...].astype(F32) * f_ref[...]
        live = _iota2(ang.shape, 1) < MLA_ROPE
        cos_ref[...] = jnp.where(live, jnp.cos(ang), 0.0)
        sin_ref[...] = jnp.where(live, jnp.sin(ang), 0.0)

    return pl.pallas_call(
        body, name="rope_tables", out_shape=[jax.ShapeDtypeStruct((t, LANES), F32)] * 2, compiler_params=_params(),
    )(pos_col, inv_freq_row)


CONV_COL_TILE = 256


def _conv_gate(b, c, u, w0, w1, w2):
    cu = c * u
    return b * (w2 * cu + w1 * shift_rows(cu, 1) + w0 * shift_rows(cu, 2))


def _conv_specs(t):
    nb = D_MODEL // CONV_COL_TILE
    return [pl.BlockSpec((t, CONV_COL_TILE), lambda j, part=part: (0, part * nb + j)) for part in range(3)]


def _conv_fwd(bcu, w):
    t = bcu.shape[0]

    def body(b_ref, c_ref, u_ref, w_ref, o_ref):
        o_ref[...] = _conv_gate(b_ref[...], c_ref[...], u_ref[...], w_ref[0:1, :], w_ref[1:2, :],
                                w_ref[2:3, :]).astype(o_ref.dtype)

    return pl.pallas_call(
        body, name="conv_fwd", grid=(D_MODEL // CONV_COL_TILE,),
        in_specs=_conv_specs(t) + [pl.BlockSpec((3, CONV_COL_TILE), lambda j: (0, j))],
        out_specs=pl.BlockSpec((t, CONV_COL_TILE), lambda j: (0, j)),
        out_shape=jax.ShapeDtypeStruct((t, D_MODEL), BF16), compiler_params=_params(),
    )(bcu, bcu, bcu, w)


def _conv_bwd(bcu, w, dout, after):
    t = bcu.shape[0]
    after = [a for a in after if a is not None]

    def body(b_ref, c_ref, u_ref, w_ref, do_ref, *rest):
        db_ref, dc_ref, du_ref, dw_ref = rest[len(after):]
        _, vjp = jax.vjp(_conv_gate, b_ref[...], c_ref[...], u_ref[...], w_ref[0:1, :], w_ref[1:2, :], w_ref[2:3, :])
        db, dc, du, dw0, dw1, dw2 = vjp(do_ref[...])
        db_ref[...] = db.astype(db_ref.dtype)
        dc_ref[...] = dc.astype(dc_ref.dtype)
        du_ref[...] = du.astype(du_ref.dtype)
        dw_ref[0:1, :] = dw0
        dw_ref[1:2, :] = dw1
        dw_ref[2:3, :] = dw2

    col = pl.BlockSpec((t, CONV_COL_TILE), lambda j: (0, j))
    return pl.pallas_call(
        body, name="conv_bwd", grid=(D_MODEL // CONV_COL_TILE,),
        in_specs=_conv_specs(t) + [pl.BlockSpec((3, CONV_COL_TILE), lambda j: (0, j)), col]
        + [pl.BlockSpec(memory_space=pl.ANY)] * len(after),
        out_specs=[col, col, col, pl.BlockSpec((3, CONV_COL_TILE), lambda j: (0, j))],
        out_shape=[jax.ShapeDtypeStruct((t, D_MODEL), BF16)] * 3 + [jax.ShapeDtypeStruct((3, D_MODEL), F32)],
        compiler_params=_params(),
    )(bcu, bcu, bcu, w, dout, *after)


def _loss_head(y, target):
    t, d = y.shape
    tm = 256

    def body(y_ref, t_ref, loss_ref, dy_ref):
        @pl.when(pl.program_id(0) == 0)
        def _():
            loss_ref[...] = jnp.zeros_like(loss_ref)

        err = y_ref[...] - t_ref[...]
        dy_ref[...] = err * (1.0 / d)
        loss_ref[...] += 0.5 * jnp.sum(jnp.sum(err * err, axis=-1, keepdims=True) * (1.0 / d))

    tile = pl.BlockSpec((tm, d), lambda i: (i, 0))
    return pl.pallas_call(
        body, name="loss_head", grid=(t // tm,), in_specs=[tile, tile],
        out_specs=[pl.BlockSpec((8, LANES), lambda i: (0, 0)), tile],
        out_shape=[jax.ShapeDtypeStruct((8, LANES), F32), jax.ShapeDtypeStruct((t, d), F32)],
        compiler_params=_params(),
    )(y, target)


def _ln_epi(acc, res, g, b):
    a = ALPHA * res + acc
    return a, _layer_norm(a, g, b)


def _ln_fn(a, g, b):
    return (_layer_norm(a, g, b),)


def _ln_bwd_epi(scale):
    def epi(acc, res, a, g, b):
        _, vjp = jax.vjp(_ln_fn, a, g, b)
        return vjp((acc + scale * res,))
    return epi


def _relu_sq(h):
    r = jnp.maximum(h.astype(F32), 0.0)
    return r * r


def _pad_cols(w, n):
    return jnp.pad(w, ((0, 0), (0, n - w.shape[1])))


def _pad_rows(w, n):
    return jnp.pad(w, ((0, n - w.shape[0]), (0, 0)))


def _uq_to_kernel_layout(w_uq):
    w = w_uq.reshape(MLA_RANK, MLA_HEADS, MLA_NOPE + MLA_ROPE)
    nope = w[:, :, :MLA_NOPE].reshape(MLA_RANK, MLA_HEADS * MLA_NOPE)
    rope = jnp.pad(w[:, :, MLA_NOPE:], ((0, 0), (0, 0), (0, LANES - MLA_ROPE))).reshape(MLA_RANK, MLA_HEADS * LANES)
    return jnp.concatenate([nope, rope], axis=1)


def _uq_from_kernel_layout(w):
    nope = w[:, :MLA_HEADS * MLA_NOPE].reshape(MLA_RANK, MLA_HEADS, MLA_NOPE)
    rope = w[:, MLA_HEADS * MLA_NOPE:].reshape(MLA_RANK, MLA_HEADS, LANES)[:, :, :MLA_ROPE]
    return jnp.concatenate([nope, rope], axis=2).reshape(MLA_RANK, MLA_HEADS * (MLA_NOPE + MLA_ROPE))


def _step(x, p, positions, target, small, comm):
    t = x.shape[0]
    w = small
    freqs = ROPE_BASE ** (-jnp.arange(0, MLA_ROPE // 2, dtype=F32) * (2.0 / MLA_ROPE))
    freq_row = jnp.concatenate([freqs, freqs, jnp.zeros((LANES - MLA_ROPE,), F32)])[None, :]
    cos, sin = _rope_tables(positions.reshape(t, 1), freq_row)

    saved = []
    for i in range(DEPTH):
        j, kind = i // 3, i % 3
        wl = comm.mixer_weights(i)
        s = {"x": x, "wl": wl}
        tok = comm.at("fwd", i, "begin", x)
        if kind == 0:
            s["w_main"] = wl["gla_w_in_t"][:GLA_MAIN]
            s["w_lr"] = _pad_rows(wl["gla_w_in_t"][GLA_MAIN:], LANES)
            s["w_up"] = _pad_rows(w["gla_w_gate_up"][j], LANES).astype(BF16)
            s["proj"] = _matmul(x, s["w_main"], name="gla_proj", tb=True, tn=1024, after=tok)
            s["glr"] = _matmul(x, s["w_lr"], name="gla_lr", tb=True, out_dtypes=(F32,))
            s["z"] = _matmul(s["glr"], s["w_up"], name="gla_gate", epi=lambda acc, b: (acc + b,),
                             epi_ins=(w["gla_b_gate"][j][None, :],), out_dtypes=(F32,))
            s["u"], s["states"] = _gla_fwd(s["proj"], s["z"], w["gla_norm_g"][j][None, :])
        elif kind == 1:
            s["w_in"] = _pad_cols(wl["mla_w_in"], MLA_IN_PAD)
            s["w_uq"] = _uq_to_kernel_layout(wl["mla_w_uq"])
            s["cq"] = _matmul(x, s["w_in"], name="mla_proj", tn=MLA_IN_PAD, out_dtypes=(F32,), after=tok)
            s["pre_params"] = (w["mla_q_norm"][j][None, :], w["mla_kv_norm"][j][None, :], s["w_uq"], wl["mla_w_ukv"])
            s["qn"], s["qr"], s["kv"], s["kr"] = _tile_fwd(_mla_pre, (s["cq"], cos, sin), s["pre_params"],
                                                           (BF16, BF16, BF16, BF16), tm=256, name="mla_pre_fwd")
            s["u"] = _mla_attn_fwd(s["qn"], s["qr"], s["kv"], s["kr"])
        else:
            s["bcu"] = _matmul(x, wl["conv"], name="conv_proj", tb=True, tm=256, tn=3 * D_MODEL, b_at=REG_CONV,
                               out_dtypes=(F32,), after=tok)
            s["u"] = _conv_fwd(s["bcu"], w["conv_w"][j])
        g0, b0 = w["ln_g"][i, 0][None, :], w["ln_b"][i, 0][None, :]
        g1, b1 = w["ln_g"][i, 1][None, :], w["ln_b"][i, 1][None, :]
        wa, wb = s["wa"], _ = comm.slab_weights(i, s["u"])
        s["a1"], s["x1"] = _matmul(s["u"], wa, name="mixer_out_ln", tm=256, tn=D_MODEL, b_at=REG_WOUT, epi=_ln_epi,
                                   epi_ins=(x, g0, b0), out_dtypes=(F32, F32))
        s["hh"] = _matmul(s["x1"], wa, name="mlp_up", tb=True, tm=256, tn=D_FF, b_at=REG_W1T)
        tok = comm.at("fwd", i, "mid", s["hh"])
        s["a2"], s["x2"] = _matmul(s["hh"], wa, name="mlp_down_ln", tm=256, tn=D_MODEL, b_at=REG_W2, a_fn=_relu_sq,
                                   epi=_ln_epi, epi_ins=(s["x1"], g1, b1), out_dtypes=(F32, F32), after=tok)
        s["pp"] = _matmul(p[i], wb, name="ple_proj", tb=True, tn=D_MODEL, b_at=REG_WPT)
        tok = comm.at("fwd", i, "end", s["pp"])
        x, s["gt"] = _matmul(s["x2"], wa, name="ple_gate", tn=1024, b_at=REG_WG,
                             epi=lambda acc, xr, pp: (xr + jax.nn.sigmoid(acc) * pp.astype(F32), acc),
                             epi_ins=(s["x2"], s["pp"]), out_dtypes=(F32, BF16), after=tok)
        saved.append(s)

    loss_part, dx = _loss_head(x, target)

    gw = {n: [None] * WEIGHTS[n][0][0] for n in SMALL + REPLICATED}
    ln_g_grads, ln_b_grads = [[None, None] for _ in range(DEPTH)], [[None, None] for _ in range(DEPTH)]
    resid = lambda acc, r: (acc + ALPHA * r,)
    plus = lambda acc, r: (acc + r,)
    for i in reversed(range(DEPTH)):
        j, kind = i // 3, i % 3
        s = saved[i]
        wa = s["wa"]
        ga = lax.empty((N_DEV, A_ROWS, D_MODEL), BF16)
        gb = lax.empty((N_DEV, REG_WPT[1], PLE_DIM), BF16)
        layer_grads = {}
        tok = comm.at("bwd", i, "begin", dx)

        def ple_bwd(dxo, gt, pp):
            sg = jax.nn.sigmoid(gt)
            return dxo * sg, dxo * pp * sg * (1.0 - sg)

        d_pp, d_gt = _tile_fwd(ple_bwd, (dx, s["gt"], s["pp"]), (), (BF16, BF16), tm=256, name="ple_bwd")
        gb = _matmul(d_pp, p[i], name="ple_proj_dw", ta=True, tm=REG_WPT[1], tn=PLE_DIM, out_at=REG_WPT, out_buf=gb, after=tok)
        ga = _matmul(s["x2"], d_gt, name="ple_gate_dw", ta=True, tm=REG_WG[1], tn=1024, out_at=REG_WG, out_buf=ga)
        g1, b1 = w["ln_g"][i, 1][None, :], w["ln_b"][i, 1][None, :]
        d_a2, ln_g_grads[i][1], ln_b_grads[i][1] = _matmul(
            d_gt, wa, name="ple_gate_dx_ln", tb=True, tm=256, tn=D_MODEL, b_at=REG_WG, epi=_ln_bwd_epi(1.0),
            epi_ins=(dx, s["a2"], g1, b1), out_dtypes=(F32,), n_row_sums=2, after=[ga, gb])
        tok = comm.at("bwd", i, "ln", d_a2)
        ga = _matmul(s["hh"], d_a2, name="mlp_down_dw", ta=True, tm=REG_W2[1], tn=1024, a_fn=_relu_sq, out_at=REG_W2,
                     out_buf=ga, after=tok)
        d_hh = _matmul(d_a2, wa, name="mlp_down_dx", tb=True, tm=256, tn=D_FF, b_at=REG_W2, after=ga,
                       epi=lambda acc, hh: (acc * 2.0 * jnp.maximum(hh.astype(F32), 0.0),), epi_ins=(s["hh"],))
        ga = _matmul(d_hh, s["x1"], name="mlp_up_dw", ta=True, tm=REG_W1T[1], tn=1024, out_at=REG_W1T, out_buf=ga)
        g0, b0 = w["ln_g"][i, 0][None, :], w["ln_b"][i, 0][None, :]
        d_a1, ln_g_grads[i][0], ln_b_grads[i][0] = _matmul(
            d_hh, wa, name="mlp_up_dx_ln", tm=256, tn=D_MODEL, b_at=REG_W1T, epi=_ln_bwd_epi(ALPHA),
            epi_ins=(d_a2, s["a1"], g0, b0), out_dtypes=(F32,), n_row_sums=2, after=ga)
        ga = _matmul(s["u"], d_a1, name="mixer_out_dw", ta=True, tm=REG_WOUT[1], tn=1024, out_at=REG_WOUT, out_buf=ga)
        du = _matmul(d_a1, wa, name="mixer_out_dx", tb=True, tn=1024, b_at=REG_WOUT, out_dtypes=(F32,), after=ga)
        comm.slab_grads(i, ga, gb)
        tok = comm.at("bwd", i, "slab_done", du) or []
        if kind == 0:
            dproj, dz, dg = _gla_bwd(s["proj"], s["z"], w["gla_norm_g"][j][None, :], s["states"], du, tok)
            tok = comm.at("bwd", i, "mixer_done", dproj)
            gw["gla_norm_g"][j] = dg[0]
            gw["gla_b_gate"][j] = _tile_bwd(lambda zz, b: (zz + b,), (s["z"],), (w["gla_b_gate"][j][None, :],), (dz,), (),
                                            tm=256, name="gla_bias_bwd", diff_tiled=[])[1][0][0]
            gw["gla_w_gate_up"][j] = _matmul(s["glr"], dz, name="gla_gate_dw", ta=True, out_dtypes=(F32,),
                                             after=tok)[:GLA_RANK]
            dglr = _matmul(dz, s["w_up"], name="gla_gate_dx", tb=True, out_dtypes=(F32,))
            dw_main = _matmul(dproj, s["x"], name="gla_proj_dw", ta=True, tn=1024, out_dtypes=(F32,))
            dw_lr = _matmul(dglr, s["x"], name="gla_lr_dw", ta=True, tn=1024, out_dtypes=(F32,))[:GLA_RANK]
            layer_grads["gla_w_in_t"] = jnp.concatenate([dw_main, dw_lr], axis=0)
            dx = _matmul(dproj, s["w_main"], name="gla_proj_dx", tn=1024, epi=resid, epi_ins=(d_a1,),
                         out_dtypes=(F32,), after=[dw_main, dw_lr, gw["gla_w_gate_up"][j]])
            dx = _matmul(dglr, s["w_lr"], name="gla_lr_dx", tn=1024, epi=plus, epi_ins=(dx,), out_dtypes=(F32,))
        elif kind == 1:
            dqn, dqr, dkv, dkr = _mla_attn_bwd(s["qn"], s["qr"], s["kv"], s["kr"], du, tok)
            tok = comm.at("bwd", i, "mixer_done", dqn)
            (d_cq,), (dgq, dgkv, dw_uq, dw_ukv) = _tile_bwd(_mla_pre, (s["cq"], cos, sin), s["pre_params"],
                                                           (dqn, dqr, dkv, dkr), (BF16,), tm=256, name="mla_pre_bwd",
                                                           diff_tiled=[0])
            gw["mla_q_norm"][j], gw["mla_kv_norm"][j] = dgq[0], dgkv[0]
            layer_grads["mla_w_uq"] = _uq_from_kernel_layout(dw_uq)
            layer_grads["mla_w_ukv"] = dw_ukv
            layer_grads["mla_w_in"] = _matmul(s["x"], d_cq, name="mla_proj_dw", ta=True, tn=MLA_IN_PAD,
                                              out_dtypes=(F32,), after=tok)[:, :MLA_IN]
            dx = _matmul(d_cq, s["w_in"], name="mla_proj_dx", tb=True, tn=1024, epi=resid, epi_ins=(d_a1,),
                         out_dtypes=(F32,), after=layer_grads["mla_w_in"])
        else:
            db, dc, du_, dcw = _conv_bwd(s["bcu"], w["conv_w"][j], du, tok)
            tok = comm.at("bwd", i, "mixer_done", db)
            gw["conv_w"][j] = dcw
            dbcu = jnp.concatenate([db, dc, du_], axis=1)
            layer_grads["conv"] = _matmul(dbcu, s["x"], name="conv_proj_dw", ta=True, tm=REG_CONV[1], tn=1024,
                                          out_at=REG_CONV, out_buf=lax.empty((N_DEV, REG_CONV[1], D_MODEL), BF16),
                                          after=tok)
            dx = _matmul(dbcu, s["wl"]["conv"], name="conv_proj_dx", tn=1024, b_at=REG_CONV, epi=resid, epi_ins=(d_a1,),
                         out_dtypes=(F32,), after=layer_grads["conv"])
        comm.mixer_grads(i, layer_grads)

    gw["ln_g"] = [jnp.concatenate([a, b], axis=0) for a, b in ln_g_grads]
    gw["ln_b"] = [jnp.concatenate([a, b], axis=0) for a, b in ln_b_grads]
    return loss_part, dx, {n: jnp.stack(gw[n]).astype(F32) for n in gw}


MESH_IDS = pl.DeviceIdType.MESH
ANY = pl.BlockSpec(memory_space=pl.ANY)
HBM_SPEC = pl.BlockSpec(memory_space=pltpu.HBM)
SEM_SPEC = pl.BlockSpec(memory_space=pltpu.SEMAPHORE)
DATAFLOW_EFFECT = pltpu.SideEffectType.DATAFLOW_SIDE_EFFECTING
CORE_COPIES, CHIP_COPIES = 4, 3


def _my_place():
    return lax.axis_index("x"), lax.axis_index("y"), lax.axis_index("c")


def _other_chips(mx, my):
    return [(1 - mx, my), (mx, 1 - my), (1 - mx, 1 - my)]


def _remote(src, dst, send_sems, recv_sems, k, to):
    return pltpu.make_async_remote_copy(src_ref=src, dst_ref=dst, send_sem=send_sems.at[k], recv_sem=recv_sems.at[k],
                                        device_id=to, device_id_type=MESH_IDS)


def _gather_first_copies(n_arr):
    def make(bufs, send_sems, recv_sems):
        mx, my, mc = _my_place()
        mine = 4 * mx + 2 * my + mc
        peers = [(mx, my, 1 - mc)] + [(cx, cy, mc) for cx, cy in _other_chips(mx, my)]
        return [_remote(bufs[a].at[mine], bufs[a].at[mine], send_sems, recv_sems, (1 + CHIP_COPIES) * a + k, to)
                for a in range(n_arr) for k, to in enumerate(peers)]
    return make, (1 + CHIP_COPIES) * n_arr


def _gather_forward_copies(n_arr):
    def make(bufs, send_sems, recv_sems):
        mx, my, mc = _my_place()
        blocks = [4 * cx + 2 * cy + mc for cx, cy in _other_chips(mx, my)]
        return [_remote(bufs[a].at[blk], bufs[a].at[blk], send_sems, recv_sems, CHIP_COPIES * a + k, (mx, my, 1 - mc))
                for a in range(n_arr) for k, blk in enumerate(blocks)]
    return make, CHIP_COPIES * n_arr


def _scatter_core_copies(n_arr):
    def make(bufs, send_sems, recv_sems):
        mx, my, mc = _my_place()
        return [_remote(bufs[a].at[2 * k + (1 - mc)], bufs[n_arr + a].at[k], send_sems, recv_sems, CORE_COPIES * a + k,
                        (mx, my, 1 - mc)) for a in range(n_arr) for k in range(CORE_COPIES)]
    return make, CORE_COPIES * n_arr


def _scatter_chip_copies(n_arr):
    def make(bufs, send_sems, recv_sems):
        mx, my, mc = _my_place()
        return [_remote(bufs[a].at[2 * cx + cy], bufs[n_arr + a].at[k], send_sems, recv_sems, CHIP_COPIES * a + k,
                        (cx, cy, mc)) for a in range(n_arr) for k, (cx, cy) in enumerate(_other_chips(mx, my))]
    return make, CHIP_COPIES * n_arr


def _exchange(name, bufs, copies):
    make, n_copies = copies
    n = len(bufs)

    def body(*refs):
        descs = make(refs[:n], refs[2 * n], refs[2 * n + 1])
        for cp in descs:
            cp.start()
        for cp in descs:
            cp.wait()

    return pl.pallas_call(
        body, name=name, out_shape=[jax.ShapeDtypeStruct(b.shape, b.dtype) for b in bufs], in_specs=[ANY] * n,
        out_specs=[ANY] * n, input_output_aliases={i: i for i in range(n)},
        scratch_shapes=[pltpu.SemaphoreType.DMA((n_copies,)), pltpu.SemaphoreType.DMA((n_copies,))],
    )(*bufs)


def _exchange_start(name, bufs, copies, after):
    make, n_copies = copies
    n = len(bufs)

    def body(*refs):
        for cp in make(refs[:n], refs[n + 1], refs[n + 2]):
            cp.start()
        refs[-1][...] = jnp.zeros_like(refs[-1])

    outs = pl.pallas_call(
        body, name=name,
        out_shape=(pltpu.SemaphoreType.DMA((n_copies,)), pltpu.SemaphoreType.DMA((n_copies,)),
                   *[pltpu.HBM(b.shape, b.dtype) for b in bufs], jax.ShapeDtypeStruct((8, LANES), F32)),
        in_specs=[HBM_SPEC] * n + [ANY],
        out_specs=(SEM_SPEC, SEM_SPEC, *[HBM_SPEC] * n, pl.BlockSpec(memory_space=pltpu.VMEM)),
        input_output_aliases={i: 2 + i for i in range(n)},
        compiler_params=pltpu.CompilerParams(has_side_effects=DATAFLOW_EFFECT),
    )(*[pltpu.with_memory_space_constraint(b, pltpu.HBM) for b in bufs], after)
    return (outs[0], outs[1]), list(outs[2:2 + n]), outs[-1]


def _exchange_wait(name, sems, bufs, copies, after):
    make, _ = copies
    n = len(bufs)

    def body(*refs):
        for cp in make(refs[:n], refs[n], refs[n + 1]):
            cp.wait_send()
            cp.wait_recv()

    return list(pl.pallas_call(
        body, name=name, out_shape=[pltpu.HBM(b.shape, b.dtype) for b in bufs],
        in_specs=[HBM_SPEC] * n + [SEM_SPEC, SEM_SPEC, ANY], out_specs=[HBM_SPEC] * n,
        input_output_aliases={i: i for i in range(n)},
        compiler_params=pltpu.CompilerParams(has_side_effects=DATAFLOW_EFFECT),
    )(*bufs, *sems, after))


SUM_TILE_BYTES = 2 * 1024 * 1024


def _row_tile(r, c):
    best = None
    for cand in range(16, r + 1, 16):
        if r % cand == 0 and cand * c * 2 <= SUM_TILE_BYTES:
            best = cand
    return r if best is None else best


def _pair_sum(g, recv, my_c):
    _, r, c = g.shape
    tr = _row_tile(r, c)

    def body(c_ref, g_ref, r_ref, o_ref):
        o_ref[...] = (g_ref[...].astype(F32) + r_ref[...].astype(F32)).astype(o_ref.dtype)

    return pl.pallas_call(
        body, name="rs_pair_sum", out_shape=jax.ShapeDtypeStruct((4, r, c), g.dtype),
        grid_spec=pltpu.PrefetchScalarGridSpec(
            num_scalar_prefetch=1, grid=(4, r // tr),
            in_specs=[pl.BlockSpec((1, tr, c), lambda n, i, cr: (2 * n + cr[0], i, 0)),
                      pl.BlockSpec((1, tr, c), lambda n, i, cr: (n, i, 0))],
            out_specs=pl.BlockSpec((1, tr, c), lambda n, i, cr: (n, i, 0))),
        compiler_params=_params(),
    )(my_c, g, recv)


def _chip_sum(h, recv, my_chip):
    _, r, c = h.shape
    tr = _row_tile(r, c)

    def body(j_ref, h_ref, r0_ref, r1_ref, r2_ref, o_ref):
        o_ref[...] = ((h_ref[0].astype(F32) + r0_ref[0].astype(F32)) + r1_ref[0].astype(F32)) + r2_ref[0].astype(F32)

    return pl.pallas_call(
        body, name="rs_chip_sum", out_shape=jax.ShapeDtypeStruct((r, c), F32),
        grid_spec=pltpu.PrefetchScalarGridSpec(
            num_scalar_prefetch=1, grid=(r // tr,),
            in_specs=[pl.BlockSpec((1, tr, c), lambda i, jr: (jr[0], i, 0))]
            + [pl.BlockSpec((1, tr, c), lambda i, jr, n=n: (n, i, 0)) for n in range(3)],
            out_specs=pl.BlockSpec((tr, c), lambda i, jr: (i, 0))),
        compiler_params=_params(),
    )(my_chip, h, recv, recv, recv)


def _sum_blocks(g):
    n, r, c = g.shape

    def body(g_ref, o_ref):
        acc = g_ref[0]
        for k in range(1, n):
            acc = acc + g_ref[k]
        o_ref[...] = acc

    return pl.pallas_call(body, name="sum_blocks", out_shape=jax.ShapeDtypeStruct((r, c), F32), compiler_params=_params())(g)


def _pack(flat_parts, cols, row_multiple, dtype):
    flat = jnp.concatenate([f.astype(dtype) for f in flat_parts])
    per_row_block = cols * row_multiple
    padded = -(-flat.shape[0] // per_row_block) * per_row_block
    return jnp.pad(flat, (0, padded - flat.shape[0])).reshape(padded // cols, cols)


def _shard_shape(name):
    shape, axis = WEIGHTS[name]
    if axis is None:
        return shape
    return tuple(s // N_DEV if a == axis else s for a, s in enumerate(shape))


def _size(shape):
    n = 1
    for s in shape:
        n *= s
    return n


def _unshard(blocks, name):
    _, axis = WEIGHTS[name]
    return jnp.concatenate([blocks[k] for k in range(N_DEV)], axis=axis)


def _unpack_blocks(flat, names, lead):
    out, off = {}, 0
    for n in names:
        shp = _shard_shape(n)[1:] if lead else _shard_shape(n)
        out[n] = flat[..., off:off + _size(shp)].reshape(flat.shape[:-1] + shp)
        off += _size(shp)
    return out


def _layer_slabs(shard, i):
    j, kind = i // 3, i % 3
    w_out = (shard["gla_w_out"], shard["mla_w_out"], shard["conv_w_out"])[kind][j]
    out = {"a": jnp.concatenate([shard["mlp_w2"][i], shard["mlp_w1"][i].T, w_out, shard["ple_w_gate"][i]], axis=0).astype(BF16),
           "b": shard["ple_w_proj"][i].T.astype(BF16)}
    if kind == 0:
        out["gla"] = shard["gla_w_in"][j].T.astype(BF16)
    elif kind == 1:
        out["mla"] = _pack([shard[n][j].reshape(-1) for n in MLA_PACKED], PACK_COLS, PACK_ROW_TILE, BF16)
    else:
        out["conv"] = shard["conv_w_in"][j].T.astype(BF16)
    return out


def _mixer_weights(landed, i):
    kind = i % 3
    if kind == 0:
        return {"gla_w_in_t": landed["gla"].reshape(-1, D_MODEL)}
    if kind == 2:
        return {"conv": landed["conv"]}
    blocks = _unpack_blocks(landed["mla"].reshape(N_DEV, -1), MLA_PACKED, lead=True)
    return {n: jnp.concatenate([blocks[n][k] for k in range(N_DEV)], axis=WEIGHTS[n][1] - 1) for n in MLA_PACKED}


def _mixer_grad_buffers(layer_grads, i):
    kind = i % 3
    if kind == 0:
        return {"gla": layer_grads["gla_w_in_t"].reshape(N_DEV, -1, D_MODEL).astype(BF16)}
    if kind == 2:
        return {"conv": layer_grads["conv"]}
    parts = [jnp.stack(jnp.split(layer_grads[n], N_DEV, axis=WEIGHTS[n][1] - 1)).reshape(N_DEV, -1) for n in MLA_PACKED]
    cat = jnp.concatenate(parts, axis=1).astype(BF16)
    per = PACK_COLS * PACK_ROW_TILE
    padded = -(-cat.shape[1] // per) * per
    return {"mla": jnp.pad(cat, ((0, 0), (0, padded - cat.shape[1]))).reshape(N_DEV, padded // PACK_COLS, PACK_COLS)}


SLAB_KEYS = ("a", "b")


class _Overlap:
    def __init__(self, shard, small_pack):
        mx, my, mc = _my_place()
        self.my_c = mc.astype(jnp.int32).reshape(1)
        self.my_chip = (2 * mx + my).astype(jnp.int32).reshape(1)
        mine = 4 * mx + 2 * my + mc
        slabs = [_layer_slabs(shard, i) for i in range(DEPTH)]
        slabs[0]["small"] = small_pack
        self.landing = [{k: lax.dynamic_update_index_in_dim(lax.empty((N_DEV, *v.shape), v.dtype), v, mine, 0)
                         for k, v in slabs[i].items()} for i in range(DEPTH)]
        self.fly = {}
        self.gather_keys = {}
        self.grads = [{} for _ in range(DEPTH)]
        self.reduced = [{} for _ in range(DEPTH)]
        keys = self._keys(self.landing[0], "mixer")
        tok = self._start("ag_first_mixer_l0", [self.landing[0][k] for k in keys], _gather_first_copies(len(keys)), shard["ln_g"])
        tok = self._start("ag_first_slab_l0", [self.landing[0][k] for k in SLAB_KEYS], _gather_first_copies(2), tok)
        bufs = self._wait("ag_first_mixer_l0", tok)
        self.landing[0].update(zip(keys, _exchange("ag_forward_mixer_l0", bufs, _gather_forward_copies(len(bufs)))))

    @staticmethod
    def _keys(names, group):
        return [k for k in names if (k in SLAB_KEYS) == (group == "slab")]

    def _start(self, name, bufs, copies, after):
        sems, bufs, tok = _exchange_start(name + "_start", bufs, copies, after)
        self.fly[name] = (sems, bufs, copies)
        return tok

    def _wait(self, name, after):
        sems, bufs, copies = self.fly.pop(name)
        return _exchange_wait(name + "_wait", sems, bufs, copies, after)

    def mixer_weights(self, i):
        return _mixer_weights(self.landing[i], i)

    def slab_weights(self, i, dep):
        if i == 0:
            bufs = self._wait("ag_first_slab_l0", dep)
            self.landing[0].update(zip(SLAB_KEYS, _exchange("ag_forward_slab_l0", bufs, _gather_forward_copies(2))))
        return self.landing[i]["a"], self.landing[i]["b"]

    def slab_grads(self, i, ga, gb):
        self.grads[i].update(a=ga, b=gb)

    def mixer_grads(self, i, layer_grads):
        self.grads[i].update(_mixer_grad_buffers(layer_grads, i))

    def at(self, phase, i, point, dep):
        toks = []
        if phase == "fwd":
            if point == "begin" and i == 0:
                toks.append(self._gather_first(1, self.landing[0][self._keys(self.landing[0], "mixer")[0]]))
            if point == "mid" and i + 1 < DEPTH:
                bufs = self._wait(f"ag_first_l{i + 1}", dep)
                toks.append(self._start(f"ag_forward_l{i + 1}", bufs, _gather_forward_copies(len(bufs)), dep))
                if i + 2 < DEPTH:
                    toks.append(self._gather_first(i + 2, dep))
            if point == "end" and i + 1 < DEPTH:
                self.landing[i + 1].update(zip(self.gather_keys[i + 1], self._wait(f"ag_forward_l{i + 1}", dep)))
        else:
            if point == "begin" and i + 1 < DEPTH:
                toks.append(self._scatter_cores(i + 1, "mixer", dep))
            if point == "ln" and i + 1 < DEPTH:
                toks.append(self._scatter_chips(i + 1, "mixer", dep))
            if point == "slab_done":
                if i + 1 < DEPTH:
                    self._scatter_done(i + 1, "slab", dep)
                    self._scatter_done(i + 1, "mixer", dep)
                toks.append(self._scatter_cores(i, "slab", dep))
            if point == "mixer_done":
                toks.append(self._scatter_chips(i, "slab", dep))
        return toks or None

    def _gather_first(self, i, after):
        self.gather_keys[i] = list(self.landing[i])
        bufs = [self.landing[i][k] for k in self.gather_keys[i]]
        return self._start(f"ag_first_l{i}", bufs, _gather_first_copies(len(bufs)), after)

    def _scatter_cores(self, i, group, after):
        gs = [self.grads[i][k] for k in self._keys(self.grads[i], group)]
        land = [lax.empty((4, *g.shape[1:]), g.dtype) for g in gs]
        return self._start(f"rs_cores_{group}_l{i}", gs + land, _scatter_core_copies(len(gs)), after)

    def _pair_sums(self, bufs):
        n = len(bufs) // 2
        hs = [_pair_sum(g, r, self.my_c) for g, r in zip(bufs[:n], bufs[n:])]
        return hs + [lax.empty((3, *h.shape[1:]), h.dtype) for h in hs]

    def _scatter_chips(self, i, group, after):
        bufs = self._pair_sums(self._wait(f"rs_cores_{group}_l{i}", after))
        return self._start(f"rs_chips_{group}_l{i}", bufs, _scatter_chip_copies(len(bufs) // 2), after)

    def _chip_sums(self, i, group, bufs):
        n = len(bufs) // 2
        for k, h, r in zip(self._keys(self.grads[i], group), bufs[:n], bufs[n:]):
            self.reduced[i][k] = _chip_sum(h, r, self.my_chip)

    def _scatter_done(self, i, group, after):
        self._chip_sums(i, group, self._wait(f"rs_chips_{group}_l{i}", after))

    def tail_begin(self, dep):
        return self._scatter_cores(0, "mixer", dep)

    def tail_middle(self, dep):
        self._scatter_done(0, "slab", dep)
        return self._scatter_chips(0, "mixer", dep)

    def tail_end(self, dep):
        self._scatter_done(0, "mixer", dep)


def _all_gather_small(x, name):
    mx, my, mc = _my_place()
    land = lax.dynamic_update_index_in_dim(lax.empty((N_DEV, *x.shape), x.dtype), x, 4 * mx + 2 * my + mc, 0)
    (land,) = _exchange(name + "_first", [land], _gather_first_copies(1))
    (land,) = _exchange(name + "_forward", [land], _gather_forward_copies(1))
    return land


def _adamw_math(w, g, m, v):
    m2 = ADAM_B1 * m + (1.0 - ADAM_B1) * g
    v2 = ADAM_B2 * v + (1.0 - ADAM_B2) * (g * g)
    m_hat = m2 / (1.0 - ADAM_B1 ** ADAM_STEP)
    v_hat = v2 / (1.0 - ADAM_B2 ** ADAM_STEP)
    return -ADAM_LR * (m_hat / (jnp.sqrt(v_hat) + ADAM_EPS) + ADAM_WD * w), m2, v2


ADAMW_TILE_BYTES = 1024 * 1024


def _adamw_layer(name, w, m, v, j, g, g_at, transposed, chain, after):
    n_layers, r, c = w.shape
    tr = max(t for t in range(8, r + 1, 8) if r % t == 0 and (t * c * 4 <= ADAMW_TILE_BYTES or t == 8))
    rb, rows = g_at
    if transposed:
        assert rows == c and g.shape[1] == r, (name, g.shape, g_at)
        g_spec = pl.BlockSpec((rows, tr), lambda i: (rb, i))
    else:
        assert rows == r and g.shape[1] == c, (name, g.shape, g_at)
        g_spec = pl.BlockSpec((tr, c), lambda i: (rb * (r // tr) + i, 0))
    extra = list(chain or []) + [a for a in (after or []) if a is not None]
    n_chain = 4 if chain else 0

    def body(w_ref, m_ref, v_ref, g_ref, *rest):
        g_out, d_out, m_out, v_out, tok_ref = rest[len(extra):]
        gv = g_ref[...].T if transposed else g_ref[...]
        g_out[0] = gv
        d_out[0], m_out[0], v_out[0] = _adamw_math(w_ref[0], gv, m_ref[0], v_ref[0])
        tok_ref[...] = jnp.zeros_like(tok_ref)

    layer_spec = pl.BlockSpec((1, tr, c), lambda i: (j, i, 0))
    outs = pl.pallas_call(
        body, name=f"adamw_{name}_l{j}", grid=(r // tr,),
        in_specs=[layer_spec] * 3 + [g_spec] + [pl.BlockSpec(memory_space=pl.ANY)] * len(extra),
        out_specs=[layer_spec] * 4 + [pl.BlockSpec((8, LANES), lambda i: (0, 0))],
        out_shape=[jax.ShapeDtypeStruct(w.shape, F32)] * 4 + [jax.ShapeDtypeStruct((8, LANES), F32)],
        input_output_aliases={4 + k: k for k in range(n_chain)}, compiler_params=_params(),
    )(w, m, v, g, *extra)
    return list(outs[:4]), outs[4]


def _adamw(w, g, m, v, name):
    shape = w.shape
    cols = shape[-1]
    rows = _size(shape) // cols
    tr = rows
    for cand in (512, 256, 128, 64, 32, 16, 8):
        if rows > cand and rows % cand == 0:
            tr = cand
            break

    def body(w_ref, g_ref, m_ref, v_ref, d_ref, mo_ref, vo_ref):
        d_ref[...], mo_ref[...], vo_ref[...] = _adamw_math(w_ref[...], g_ref[...], m_ref[...], v_ref[...])

    spec = pl.BlockSpec((tr, cols), lambda i: (i, 0))
    outs = pl.pallas_call(
        body, name="adamw_" + name, grid=(rows // tr,), in_specs=[spec] * 4, out_specs=[spec] * 3,
        out_shape=[jax.ShapeDtypeStruct((rows, cols), F32)] * 3, compiler_params=_params(),
    )(*[a.reshape(rows, cols) for a in (w, g, m, v)])
    return [o.reshape(shape) for o in outs]


def kernel(x, p, positions, gla_w_in, gla_w_gate_up, gla_b_gate, gla_norm_g, gla_w_out, mla_w_in, mla_q_norm, mla_kv_norm, mla_w_uq, mla_w_ukv, mla_w_out, conv_w_in, conv_w, conv_w_out, ln_g, ln_b, mlp_w1, mlp_w2, ple_w_gate, ple_w_proj, loss_target, m_gla_w_in, m_gla_w_gate_up, m_gla_b_gate, m_gla_norm_g, m_gla_w_out, m_mla_w_in, m_mla_q_norm, m_mla_kv_norm, m_mla_w_uq, m_mla_w_ukv, m_mla_w_out, m_conv_w_in, m_conv_w, m_conv_w_out, m_ln_g, m_ln_b, m_mlp_w1, m_mlp_w2, m_ple_w_gate, m_ple_w_proj, v_gla_w_in, v_gla_w_gate_up, v_gla_b_gate, v_gla_norm_g, v_gla_w_out, v_mla_w_in, v_mla_q_norm, v_mla_kv_norm, v_mla_w_uq, v_mla_w_ukv, v_mla_w_out, v_conv_w_in, v_conv_w, v_conv_w_out, v_ln_g, v_ln_b, v_mlp_w1, v_mlp_w2, v_ple_w_gate, v_ple_w_proj):
    args = locals()
    shard = {n: args[n] for n in WEIGHT_NAMES}
    mom = {n: args["m_" + n] for n in WEIGHT_NAMES}
    var = {n: args["v_" + n] for n in WEIGHT_NAMES}
    mx, my, mc = _my_place()

    comm = _Overlap(shard, _pack([shard[n].reshape(-1) for n in SMALL], LANES, 8, F32))
    small_all = comm.landing[0]["small"]
    small = {n: shard[n] for n in REPLICATED}
    small.update({n: _unshard(blk, n) for n, blk in _unpack_blocks(small_all.reshape(N_DEV, -1), SMALL, lead=False).items()})
    loss_part, grad_x, small_grads = _step(x[0], p[:, 0], positions[0], loss_target[0], small, comm)

    chains = {}

    def update(name, j, g, g_at, transposed, tok):
        chains[name], tok = _adamw_layer(name, shard[name], mom[name], var[name], j, g, g_at, transposed,
                                         chains.get(name), [tok])
        return tok

    def update_layer(i, groups, tok):
        j, kind = i // 3, i % 3
        red = comm.reduced[i]
        if "slab" in groups:
            tok = update("mlp_w2", i, red["a"], REG_W2, False, tok)
            tok = update("mlp_w1", i, red["a"], REG_W1T, True, tok)
            tok = update(("gla_w_out", "mla_w_out", "conv_w_out")[kind], j, red["a"], REG_WOUT, False, tok)
            tok = update("ple_w_gate", i, red["a"], REG_WG, False, tok)
            tok = update("ple_w_proj", i, red["b"], REG_WPT, True, tok)
        if "mixer" in groups:
            if kind == 0:
                tok = update("gla_w_in", j, red["gla"].T, (0, D_MODEL), False, tok)
            elif kind == 2:
                tok = update("conv_w_in", j, red["conv"], REG_CONV, True, tok)
            else:
                for n, g in _unpack_blocks(red["mla"].reshape(-1), MLA_PACKED, lead=True).items():
                    tok = update(n, j, g, (0, g.shape[0]), False, tok)
        return tok

    tok = comm.tail_begin(grad_x)
    tok = update_layer(3, ("slab", "mixer"), tok)
    tok = update_layer(2, ("slab", "mixer"), tok)
    tok = comm.tail_middle(tok)
    tok = update_layer(1, ("slab", "mixer"), tok)
    tok = update_layer(0, ("slab",), tok)
    comm.tail_end(tok)
    update_layer(0, ("mixer",), None)

    small_parts = [loss_part[0, :1]] + [small_grads[n].reshape(-1) for n in SMALL + REPLICATED]
    red_small = _sum_blocks(_all_gather_small(_pack(small_parts, LANES, 8, F32), "ag_small_grads")).reshape(-1)
    loss = red_small[0]
    off = 1
    dev = 4 * mx + 2 * my + mc
    for n in SMALL + REPLICATED:
        shape, axis = WEIGHTS[n]
        full_g = red_small[off:off + _size(shape)].reshape(shape)
        off += _size(shape)
        if axis is not None:
            width = shape[axis] // N_DEV
            full_g = lax.dynamic_slice_in_dim(full_g, dev * width, width, axis=axis)
        chains[n] = [full_g, *_adamw(shard[n], full_g, mom[n], var[n], n)]
    return (loss, grad_x[None], *[chains[n][k] for k in range(4) for n in WEIGHT_NAMES])
```

```python
import functools

import jax
import jax.numpy as jnp
from jax import lax
from jax.experimental import pallas as pl
from jax.experimental.pallas import tpu as pltpu

F32, BF16 = jnp.float32, jnp.bfloat16
HIGHEST = lax.Precision.HIGHEST
MESH_AXES = ("x", "y", "c")
N_DEV = 8

D_MODEL = 1024
SEQ = 2048
DEPTH = 4
CHUNK = 64
ALPHA = (2 * DEPTH) ** 0.25
LN_EPS = 1e-5
RMS_EPS = 1e-6
PLE_DIM = 256
D_FF = 4 * D_MODEL
GLA_HEADS = 4
GLA_DK = 128
GLA_DV = 256
GLA_RANK = 16
GLA_TAU = 16.0
GLA_HK = GLA_HEADS * GLA_DK
GLA_HV = GLA_HEADS * GLA_DV
GLA_MAIN = 2 * GLA_HK + GLA_HV + D_MODEL
MLA_HEADS = 8
MLA_NOPE = 128
MLA_ROPE = 64
MLA_V = 128
MLA_RANK = 256
MLA_IN = 2 * MLA_RANK + MLA_ROPE
MLA_IN_PAD = 640
ROPE_BASE = 10000.0
LANES = 128
ADAM_LR, ADAM_B1, ADAM_B2, ADAM_EPS, ADAM_WD, ADAM_STEP = 0.001, 0.9, 0.999, 1e-08, 0.01, 10

V7X_VMEM_LIMIT_BYTES = 56 * 1024 * 1024
PACK_COLS = 1024
PACK_ROW_TILE = 256

WEIGHTS = {
    "gla_w_in": ((2, 1024, 3088), 2), "gla_w_gate_up": ((2, 16, 512), 2), "gla_b_gate": ((2, 512), 1),
    "gla_norm_g": ((2, 256), 1), "gla_w_out": ((2, 1024, 1024), 1), "mla_w_in": ((1, 1024, 576), 1),
    "mla_q_norm": ((1, 256), None), "mla_kv_norm": ((1, 256), None), "mla_w_uq": ((1, 256, 1536), 2),
    "mla_w_ukv": ((1, 256, 2048), 2), "mla_w_out": ((1, 1024, 1024), 1), "conv_w_in": ((1, 1024, 3072), 2),
    "conv_w": ((1, 3, 1024), 2), "conv_w_out": ((1, 1024, 1024), 1), "ln_g": ((4, 2, 1024), 2),
    "ln_b": ((4, 2, 1024), 2), "mlp_w1": ((4, 1024, 4096), 2), "mlp_w2": ((4, 4096, 1024), 1),
    "ple_w_gate": ((4, 1024, 1024), 1), "ple_w_proj": ((4, 256, 1024), 2),
}
WEIGHT_NAMES = list(WEIGHTS)
REG_W2, REG_W1T, REG_WOUT, REG_WG = (0, 512), (1, 512), (8, 128), (9, 128)
A_ROWS = 1280
REG_CONV = (0, 384)
REG_WPT = (0, 128)
MLA_PACKED = ["mla_w_in", "mla_w_uq", "mla_w_ukv"]
SMALL = ["gla_w_gate_up", "gla_b_gate", "gla_norm_g", "conv_w", "ln_g", "ln_b"]
REPLICATED = ["mla_q_norm", "mla_kv_norm"]


def _params(**kw):
    return pltpu.CompilerParams(vmem_limit_bytes=V7X_VMEM_LIMIT_BYTES, **kw)


def _dot(a, b, ca, cb, precision=None):
    return lax.dot_general(a, b, (((ca,), (cb,)), ((), ())), precision=precision, preferred_element_type=F32)


def _nn(a, b):
    return _dot(a.astype(BF16), b.astype(BF16), 1, 0)


def _nt(a, b):
    return _dot(a.astype(BF16), b.astype(BF16), 1, 1)


def _tn(a, b):
    return _dot(a.astype(BF16), b.astype(BF16), 0, 0)


@jax.custom_vjp
def mm_nn(a, b):
    return _nn(a, b)


def _mm_nn_fwd(a, b):
    return _nn(a, b), (a, b)


def _mm_nn_bwd(res, g):
    a, b = res
    return _nt(g, b).astype(a.dtype), _tn(a, g).astype(b.dtype)


mm_nn.defvjp(_mm_nn_fwd, _mm_nn_bwd)


@jax.custom_vjp
def mm_nt(a, b):
    return _nt(a, b)


def _mm_nt_fwd(a, b):
    return _nt(a, b), (a, b)


def _mm_nt_bwd(res, g):
    a, b = res
    return _nn(g, b).astype(a.dtype), _tn(g, a).astype(b.dtype)


mm_nt.defvjp(_mm_nt_fwd, _mm_nt_bwd)


@jax.custom_vjp
def mm_tn(a, b):
    return _tn(a, b)


def _mm_tn_fwd(a, b):
    return _tn(a, b), (a, b)


def _mm_tn_bwd(res, g):
    a, b = res
    return _nt(b, g).astype(a.dtype), _nn(a, g).astype(b.dtype)


mm_tn.defvjp(_mm_tn_fwd, _mm_tn_bwd)


def _iota2(shape, dim):
    return lax.broadcasted_iota(jnp.int32, shape, dim)


@jax.custom_vjp
def cumsum_rows(x):
    n = x.shape[0]
    tri = (_iota2((n, n), 0) >= _iota2((n, n), 1)).astype(F32)
    return _dot(tri, x, 1, 0, precision=HIGHEST)


def _cumsum_fwd(x):
    return cumsum_rows(x), None


def _cumsum_bwd(_, g):
    n = g.shape[0]
    tri_t = (_iota2((n, n), 0) <= _iota2((n, n), 1)).astype(F32)
    return (_dot(tri_t, g, 1, 0, precision=HIGHEST),)


cumsum_rows.defvjp(_cumsum_fwd, _cumsum_bwd)


def _rot_matrix(transposed):
    i, j = _iota2((LANES, LANES), 0), _iota2((LANES, LANES), 1)
    if transposed:
        i, j = j, i
    half = MLA_ROPE // 2
    plus = (i == j - half) & (j >= half) & (j < MLA_ROPE)
    minus = (i == j + half) & (j < half)
    return plus.astype(F32) - minus.astype(F32)


@jax.custom_vjp
def rot_half(x):
    return _dot(x, _rot_matrix(False), 1, 0, precision=HIGHEST)


def _rot_fwd(x):
    return rot_half(x), None


def _rot_bwd(_, g):
    return (_dot(g, _rot_matrix(True), 1, 0, precision=HIGHEST),)


rot_half.defvjp(_rot_fwd, _rot_bwd)


def _shift_rows_raw(x, s):
    n = x.shape[0]
    row = _iota2(x.shape, 0)
    rolled = pltpu.roll(x, s % n, 0)
    keep = (row >= s) if s > 0 else (row < n + s)
    return jnp.where(keep, rolled, 0.0)


@functools.partial(jax.custom_vjp, nondiff_argnums=(1,))
def shift_rows(x, s):
    return _shift_rows_raw(x, s)


def _shift_fwd(x, s):
    return _shift_rows_raw(x, s), None


def _shift_bwd(s, _, g):
    return (_shift_rows_raw(g, -s),)


shift_rows.defvjp(_shift_fwd, _shift_bwd)


def _layer_norm(a, g, b):
    mu = jnp.mean(a, -1, keepdims=True)
    xc = a - mu
    var = jnp.mean(xc * xc, -1, keepdims=True)
    return xc * lax.rsqrt(var + LN_EPS) * g + b


def _rms_norm(a, g):
    return a * lax.rsqrt(jnp.mean(a * a, -1, keepdims=True) + RMS_EPS) * g


def _log_sigmoid(z):
    return jnp.minimum(z, 0.0) - jnp.log(1.0 + jnp.exp(-jnp.abs(z)))


def _matmul(a, b, *, name, ta=False, tb=False, tm=512, tn=512, a_fn=None, epi=None, epi_ins=(), out_dtypes=(BF16,),
            b_at=None, out_at=None, out_buf=None, after=None, n_row_sums=0):
    m = a.shape[1] if ta else a.shape[0]
    k = a.shape[0] if ta else a.shape[1]
    if b_at is None:
        n, kb = (b.shape[0], b.shape[1]) if tb else (b.shape[1], b.shape[0])
    else:
        rb, r = b_at
        n, kb = (N_DEV * r, b.shape[2]) if tb else (b.shape[2], N_DEV * r)
    assert kb == k, (name, a.shape, b.shape, k, kb)
    tm, tn = min(tm, m), min(tn, n)
    assert m % tm == 0 and n % tn == 0, (name, m, n, tm, tn)
    a_spec = pl.BlockSpec((k, tm), lambda i, j: (0, i)) if ta else pl.BlockSpec((tm, k), lambda i, j: (i, 0))
    if b_at is None:
        b_spec = pl.BlockSpec((tn, k), lambda i, j: (j, 0)) if tb else pl.BlockSpec((k, tn), lambda i, j: (0, j))
        load_b = lambda ref: ref[...]
    elif tb and tn == n:
        b_spec = pl.BlockSpec((N_DEV, r, k), lambda i, j: (0, rb, 0))
        load_b = lambda ref: ref[...].reshape(n, k)
    elif tb:
        assert tn == r, (name, tn, r)
        b_spec = pl.BlockSpec((1, r, k), lambda i, j: (j, rb, 0))
        load_b = lambda ref: ref[0]
    else:
        b_spec = pl.BlockSpec((N_DEV, r, tn), lambda i, j: (0, rb, j))
        load_b = lambda ref: ref[...].reshape(k, tn)
    e_specs = []
    for e in epi_ins:
        if e.shape == (1, n):
            e_specs.append(pl.BlockSpec((1, tn), lambda i, j: (0, j)))
        else:
            assert e.shape == (m, n), (name, e.shape, m, n)
            e_specs.append(pl.BlockSpec((tm, tn), lambda i, j: (i, j)))
    n_epi = len(epi_ins)
    ca, cb = (0 if ta else 1), (1 if tb else 0)
    operands = [a, b, *epi_ins]
    in_specs = [a_spec, b_spec, *e_specs]
    if out_at is None:
        assert n_row_sums == 0 or tn == n, (name, tn, n)
        out_specs = [pl.BlockSpec((tm, tn), lambda i, j: (i, j)) for _ in out_dtypes]
        out_specs += [pl.BlockSpec((1, n), lambda i, j: (0, 0))] * n_row_sums
        out_shape = [jax.ShapeDtypeStruct((m, n), dt) for dt in out_dtypes]
        out_shape += [jax.ShapeDtypeStruct((1, n), F32)] * n_row_sums
        aliases, n_buf = {}, 0
    else:
        orb, orows = out_at
        assert len(out_dtypes) == 1 and orows % tm == 0 and m == N_DEV * orows and n == out_buf.shape[2], (name, m, n)
        per = orows // tm
        out_specs = [pl.BlockSpec((1, tm, tn), lambda i, j: (i // per, orb * per + i % per, j))]
        out_shape = [jax.ShapeDtypeStruct(out_buf.shape, out_buf.dtype)]
        operands.append(out_buf)
        in_specs.append(pl.BlockSpec(memory_space=pl.ANY))
        aliases, n_buf = {len(operands) - 1: 0}, 1
    for dep in ([] if after is None else after if isinstance(after, (list, tuple)) else [after]):
        if dep is not None:
            operands.append(dep)
            in_specs.append(pl.BlockSpec(memory_space=pl.ANY))
            n_buf += 1

    def body(a_ref, b_ref, *rest):
        av = a_ref[...]
        if a_fn is not None:
            av = a_fn(av)
        acc = _dot(av.astype(BF16), load_b(b_ref).astype(BF16), ca, cb)
        outs = epi(acc, *[r_[...] for r_ in rest[:n_epi]]) if epi is not None else (acc,)
        o_refs = rest[n_epi + n_buf:]
        n_tiles = len(o_refs) - n_row_sums
        for o_ref, val in zip(o_refs[:n_tiles], outs):
            o_ref[...] = val.astype(o_ref.dtype).reshape(o_ref.shape)
        if n_row_sums:
            @pl.when(pl.program_id(0) == 0)
            def _():
                for o_ref in o_refs[n_tiles:]:
                    o_ref[...] = jnp.zeros_like(o_ref)

            for o_ref, val in zip(o_refs[n_tiles:], outs[n_tiles:]):
                o_ref[...] += val

    outs = pl.pallas_call(
        body, name=name, grid=(m // tm, n // tn), in_specs=in_specs, out_specs=out_specs, out_shape=out_shape,
        input_output_aliases=aliases, compiler_params=_params(),
    )(*operands)
    return outs[0] if len(outs) == 1 else tuple(outs)


def _tile_fwd(f, tiled, params, out_dtypes, *, tm, name):
    t = tiled[0].shape[0]
    assert t % tm == 0
    out_avals = jax.eval_shape(f, *[jax.ShapeDtypeStruct((tm, x.shape[1]), F32) for x in tiled],
                               *[jax.ShapeDtypeStruct(p.shape, F32) for p in params])
    nt, npar = len(tiled), len(params)

    def body(*refs):
        ins = [r[...].astype(F32) for r in refs[:nt + npar]]
        outs = f(*ins)
        for o_ref, val in zip(refs[nt + npar:], outs):
            o_ref[...] = val.astype(o_ref.dtype)

    return pl.pallas_call(
        body, name=name, grid=(t // tm,),
        in_specs=[pl.BlockSpec((tm, x.shape[1]), lambda i: (i, 0)) for x in tiled]
        + [pl.BlockSpec(p.shape, lambda i: (0, 0)) for p in params],
        out_specs=[pl.BlockSpec((tm, o.shape[1]), lambda i: (i, 0)) for o in out_avals],
        out_shape=[jax.ShapeDtypeStruct((t, o.shape[1]), dt) for o, dt in zip(out_avals, out_dtypes)],
        compiler_params=_params(),
    )(*tiled, *params)


def _tile_bwd(f, tiled, params, cots, d_tiled_dtypes, *, tm, name, diff_tiled=None):
    t = tiled[0].shape[0]
    assert t % tm == 0
    nt, npar, nc = len(tiled), len(params), len(cots)
    diff_tiled = list(range(nt)) if diff_tiled is None else diff_tiled

    def body(*refs):
        ins = [r[...].astype(F32) for r in refs[:nt + npar]]
        cts = [r[...].astype(F32) for r in refs[nt + npar:nt + npar + nc]]
        o_refs = refs[nt + npar + nc:]
        _, vjp = jax.vjp(f, *ins)
        grads = vjp(tuple(cts))
        for o_ref, idx in zip(o_refs[:len(diff_tiled)], diff_tiled):
            o_ref[...] = grads[idx].astype(o_ref.dtype)
        p_refs = o_refs[len(diff_tiled):]

        @pl.when(pl.program_id(0) == 0)
        def _():
            for p_ref in p_refs:
                p_ref[...] = jnp.zeros_like(p_ref)

        for p_ref, gp in zip(p_refs, grads[nt:]):
            p_ref[...] += gp

    outs = pl.pallas_call(
        body, name=name, grid=(t // tm,),
        in_specs=[pl.BlockSpec((tm, x.shape[1]), lambda i: (i, 0)) for x in tiled]
        + [pl.BlockSpec(p.shape, lambda i: (0, 0)) for p in params]
        + [pl.BlockSpec((tm, c.shape[1]), lambda i: (i, 0)) for c in cots],
        out_specs=[pl.BlockSpec((tm, tiled[idx].shape[1]), lambda i: (i, 0)) for idx in diff_tiled]
        + [pl.BlockSpec(p.shape, lambda i: (0, 0)) for p in params],
        out_shape=[jax.ShapeDtypeStruct(tiled[idx].shape, dt) for idx, dt in zip(diff_tiled, d_tiled_dtypes)]
        + [jax.ShapeDtypeStruct(p.shape, F32) for p in params],
        compiler_params=_params(),
    )(*tiled, *params, *cots)
    return outs[:len(diff_tiled)], outs[len(diff_tiled):]


def _gla_head(q, k, v, r, z, g, st):
    c = q.shape[0]
    causal = _iota2((c, c), 0) >= _iota2((c, c), 1)
    la = _log_sigmoid(z) * (1.0 / GLA_TAU)
    big_l = cumsum_rows(la)
    ep, en = jnp.exp(big_l), jnp.exp(-big_l)
    qs = q * (GLA_DK ** -0.5)
    qp = qs * ep
    s = jnp.where(causal, mm_nt(qp, k * en), mm_nt(qs * en, k * ep))
    o = mm_nn(s, v) + mm_nt(qp, st)
    l_end = jnp.sum(la, axis=0, keepdims=True)
    st_new = st * jnp.exp(l_end) + mm_tn(v, k * jnp.exp(l_end - big_l))
    u = _rms_norm(o, g) * (r * jax.nn.sigmoid(r))
    return u, st_new


def _gla_slices(h):
    q = slice(GLA_DK * h, GLA_DK * (h + 1))
    k = slice(GLA_HK + GLA_DK * h, GLA_HK + GLA_DK * (h + 1))
    v = slice(2 * GLA_HK + GLA_DV * h, 2 * GLA_HK + GLA_DV * (h + 1))
    r = slice(2 * GLA_HK + GLA_HV + GLA_DV * h, 2 * GLA_HK + GLA_HV + GLA_DV * (h + 1))
    return q, k, v, r


GLA_CHUNKS_PER_STEP = 2


def _gla_fwd(proj, z, norm_g):
    t = proj.shape[0]
    nc, per = t // CHUNK, GLA_CHUNKS_PER_STEP
    rows_per_step = per * CHUNK

    def body(proj_ref, z_ref, g_ref, u_ref, st_save_ref, st_ref):
        @pl.when(pl.program_id(0) == 0)
        def _():
            st_ref[...] = jnp.zeros_like(st_ref)

        g = g_ref[...]
        for h in range(GLA_HEADS):
            sq, sk, sv, sr = _gla_slices(h)
            st = st_ref[h]
            for c in range(per):
                rows = slice(c * CHUNK, (c + 1) * CHUNK)
                st_save_ref[c, h] = st
                u, st = _gla_head(proj_ref[rows, sq].astype(F32), proj_ref[rows, sk].astype(F32),
                                  proj_ref[rows, sv].astype(F32), proj_ref[rows, sr].astype(F32),
                                  z_ref[rows, GLA_DK * h:GLA_DK * (h + 1)], g, st)
                u_ref[rows, GLA_DV * h:GLA_DV * (h + 1)] = u.astype(u_ref.dtype)
            st_ref[h] = st

    return pl.pallas_call(
        body, name="gla_fwd", grid=(nc // per,),
        in_specs=[pl.BlockSpec((rows_per_step, GLA_MAIN), lambda i: (i, 0)),
                  pl.BlockSpec((rows_per_step, GLA_HK), lambda i: (i, 0)), pl.BlockSpec((1, GLA_DV), lambda i: (0, 0))],
        out_specs=[pl.BlockSpec((rows_per_step, GLA_HV), lambda i: (i, 0)),
                   pl.BlockSpec((per, GLA_HEADS, GLA_DV, GLA_DK), lambda i: (i, 0, 0, 0))],
        out_shape=[jax.ShapeDtypeStruct((t, GLA_HV), BF16), jax.ShapeDtypeStruct((nc, GLA_HEADS, GLA_DV, GLA_DK), F32)],
        scratch_shapes=[pltpu.VMEM((GLA_HEADS, GLA_DV, GLA_DK), F32)],
        compiler_params=_params(),
    )(proj, z, norm_g)


def _gla_bwd(proj, z, norm_g, states, du, after):
    t = proj.shape[0]
    nc, per = t // CHUNK, GLA_CHUNKS_PER_STEP
    rows_per_step = per * CHUNK
    n_steps = nc // per
    after = [a for a in after if a is not None]

    def body(proj_ref, z_ref, g_ref, st_in_ref, du_ref, *rest):
        dproj_ref, dz_ref, dg_ref, dst_ref = rest[len(after):]

        @pl.when(pl.program_id(0) == 0)
        def _():
            dst_ref[...] = jnp.zeros_like(dst_ref)
            dg_ref[...] = jnp.zeros_like(dg_ref)

        g = g_ref[...]
        for h in range(GLA_HEADS):
            sq, sk, sv, sr = _gla_slices(h)
            dst = dst_ref[h]
            for c in reversed(range(per)):
                rows = slice(c * CHUNK, (c + 1) * CHUNK)
                ins = (proj_ref[rows, sq].astype(F32), proj_ref[rows, sk].astype(F32), proj_ref[rows, sv].astype(F32),
                       proj_ref[rows, sr].astype(F32), z_ref[rows, GLA_DK * h:GLA_DK * (h + 1)], g, st_in_ref[c, h])
                _, vjp = jax.vjp(_gla_head, *ins)
                dq, dk, dv, dr, dz, dg, dst = vjp((du_ref[rows, GLA_DV * h:GLA_DV * (h + 1)], dst))
                dproj_ref[rows, sq] = dq.astype(dproj_ref.dtype)
                dproj_ref[rows, sk] = dk.astype(dproj_ref.dtype)
                dproj_ref[rows, sv] = dv.astype(dproj_ref.dtype)
                dproj_ref[rows, sr] = dr.astype(dproj_ref.dtype)
                dz_ref[rows, GLA_DK * h:GLA_DK * (h + 1)] = dz
                dg_ref[...] += dg
            dst_ref[h] = dst

    rev = lambda i: (n_steps - 1 - i, 0)
    return pl.pallas_call(
        body, name="gla_bwd", grid=(n_steps,),
        in_specs=[pl.BlockSpec((rows_per_step, GLA_MAIN), rev), pl.BlockSpec((rows_per_step, GLA_HK), rev),
                  pl.BlockSpec((1, GLA_DV), lambda i: (0, 0)),
                  pl.BlockSpec((per, GLA_HEADS, GLA_DV, GLA_DK), lambda i: (n_steps - 1 - i, 0, 0, 0)),
                  pl.BlockSpec((rows_per_step, GLA_HV), rev)] + [pl.BlockSpec(memory_space=pl.ANY)] * len(after),
        out_specs=[pl.BlockSpec((rows_per_step, GLA_MAIN), rev), pl.BlockSpec((rows_per_step, GLA_HK), rev),
                   pl.BlockSpec((1, GLA_DV), lambda i: (0, 0))],
        out_shape=[jax.ShapeDtypeStruct((t, GLA_MAIN), BF16), jax.ShapeDtypeStruct((t, GLA_HK), F32),
                   jax.ShapeDtypeStruct((1, GLA_DV), F32)],
        scratch_shapes=[pltpu.VMEM((GLA_HEADS, GLA_DV, GLA_DK), F32)],
        compiler_params=_params(),
    )(proj, z, norm_g, states, du, *after)


def _mla_pre(cq, cos, sin, gq, gkv, w_uq, w_ukv):
    qlat = _rms_norm(cq[:, :MLA_RANK], gq)
    kvlat = _rms_norm(cq[:, MLA_RANK:2 * MLA_RANK], gkv)
    kr = cq[:, 2 * MLA_RANK:]
    scale = (MLA_NOPE + MLA_ROPE) ** -0.5
    q = mm_nn(qlat, w_uq) * scale
    kv = mm_nn(kvlat, w_ukv)
    n_nope = MLA_HEADS * MLA_NOPE
    ropes = []
    for h in range(MLA_HEADS):
        qr = q[:, n_nope + LANES * h:n_nope + LANES * (h + 1)]
        ropes.append(qr * cos + rot_half(qr) * sin)
    return q[:, :n_nope], jnp.concatenate(ropes, axis=1), kv, kr * cos + rot_half(kr) * sin


MLA_Q_TILE = 256


def _mla_attn_block(qn, qr, kv, kr, q0):
    tq, nk = qn.shape[0], kv.shape[0]
    s = mm_nt(qn, kv[:, :MLA_NOPE]) + mm_nt(qr, kr)
    visible = (_iota2((tq, nk), 1) // CHUNK) <= ((q0 + _iota2((tq, nk), 0)) // CHUNK)
    s = jnp.where(visible, s, -1e30)
    e = jnp.exp(s - jnp.max(s, -1, keepdims=True))
    p = e / jnp.sum(e, -1, keepdims=True)
    return mm_nn(p, kv[:, MLA_NOPE:])


def _mla_attn_fwd(qn, qr, kv, kr):
    t = qn.shape[0]

    def body(qn_ref, qr_ref, kv_ref, kr_ref, o_ref):
        for i in range(t // MLA_Q_TILE):
            rows = slice(i * MLA_Q_TILE, (i + 1) * MLA_Q_TILE)
            keys = slice(0, (i + 1) * MLA_Q_TILE)
            o = _mla_attn_block(qn_ref[rows, :].astype(F32), qr_ref[rows, :].astype(F32), kv_ref[keys, :].astype(F32),
                                kr_ref[keys, :].astype(F32), i * MLA_Q_TILE)
            o_ref[rows, :] = o.astype(o_ref.dtype)

    return pl.pallas_call(
        body, name="mla_attn_fwd", grid=(MLA_HEADS,),
        in_specs=[pl.BlockSpec((t, MLA_NOPE), lambda h: (0, h)), pl.BlockSpec((t, LANES), lambda h: (0, h)),
                  pl.BlockSpec((t, MLA_NOPE + MLA_V), lambda h: (0, h)), pl.BlockSpec((t, LANES), lambda h: (0, 0))],
        out_specs=pl.BlockSpec((t, MLA_V), lambda h: (0, h)),
        out_shape=jax.ShapeDtypeStruct((t, MLA_HEADS * MLA_V), BF16),
        compiler_params=_params(),
    )(qn, qr, kv, kr)


def _mla_attn_bwd(qn, qr, kv, kr, do, after):
    t = qn.shape[0]
    after = [a for a in after if a is not None]

    def body(qn_ref, qr_ref, kv_ref, kr_ref, do_ref, *rest):
        dqn_ref, dqr_ref, dkv_ref, dkr_ref = rest[len(after):]
        dkv_ref[...] = jnp.zeros_like(dkv_ref)

        @pl.when(pl.program_id(0) == 0)
        def _():
            dkr_ref[...] = jnp.zeros_like(dkr_ref)

        for i in range(t // MLA_Q_TILE):
            rows = slice(i * MLA_Q_TILE, (i + 1) * MLA_Q_TILE)
            keys = slice(0, (i + 1) * MLA_Q_TILE)
            f = functools.partial(_mla_attn_block, q0=i * MLA_Q_TILE)
            _, vjp = jax.vjp(f, qn_ref[rows, :].astype(F32), qr_ref[rows, :].astype(F32), kv_ref[keys, :].astype(F32),
                             kr_ref[keys, :].astype(F32))
            dqn, dqr, dkv, dkr = vjp(do_ref[rows, :].astype(F32))
            dqn_ref[rows, :] = dqn
            dqr_ref[rows, :] = dqr
            dkv_ref[keys, :] += dkv
            dkr_ref[keys, :] += dkr

    return pl.pallas_call(
        body, name="mla_attn_bwd", grid=(MLA_HEADS,),
        in_specs=[pl.BlockSpec((t, MLA_NOPE), lambda h: (0, h)), pl.BlockSpec((t, LANES), lambda h: (0, h)),
                  pl.BlockSpec((t, MLA_NOPE + MLA_V), lambda h: (0, h)), pl.BlockSpec((t, LANES), lambda h: (0, 0)),
                  pl.BlockSpec((t, MLA_V), lambda h: (0, h))] + [pl.BlockSpec(memory_space=pl.ANY)] * len(after),
        out_specs=[pl.BlockSpec((t, MLA_NOPE), lambda h: (0, h)), pl.BlockSpec((t, LANES), lambda h: (0, h)),
                   pl.BlockSpec((t, MLA_NOPE + MLA_V), lambda h: (0, h)), pl.BlockSpec((t, LANES), lambda h: (0, 0))],
        out_shape=[jax.ShapeDtypeStruct(qn.shape, F32), jax.ShapeDtypeStruct(qr.shape, F32),
                   jax.ShapeDtypeStruct(kv.shape, F32), jax.ShapeDtypeStruct(kr.shape, F32)],
        compiler_params=_params(),
    )(qn, qr, kv, kr, do, *after)


def _rope_tables(pos_col, inv_freq_row):
    t = pos_col.shape[0]

    def body(pos_ref, f_ref, cos_ref, sin_ref):
        ang = pos_ref[...].astype(F32) * f_ref[...]
        live = _iota2(ang.shape, 1) < MLA_ROPE
        cos_ref[...] = jnp.where(live, jnp.cos(ang), 0.0)
        sin_ref[...] = jnp.where(live, jnp.sin(ang), 0.0)

    return pl.pallas_call(
        body, name="rope_tables", out_shape=[jax.ShapeDtypeStruct((t, LANES), F32)] * 2, compiler_params=_params(),
    )(pos_col, inv_freq_row)


CONV_COL_TILE = 256


def _conv_gate(b, c, u, w0, w1, w2):
    cu = c * u
    return b * (w2 * cu + w1 * shift_rows(cu, 1) + w0 * shift_rows(cu, 2))


def _conv_specs(t):
    nb = D_MODEL // CONV_COL_TILE
    return [pl.BlockSpec((t, CONV_COL_TILE), lambda j, part=part: (0, part * nb + j)) for part in range(3)]


def _conv_fwd(bcu, w):
    t = bcu.shape[0]

    def body(b_ref, c_ref, u_ref, w_ref, o_ref):
        o_ref[...] = _conv_gate(b_ref[...], c_ref[...], u_ref[...], w_ref[0:1, :], w_ref[1:2, :],
                                w_ref[2:3, :]).astype(o_ref.dtype)

    return pl.pallas_call(
        body, name="conv_fwd", grid=(D_MODEL // CONV_COL_TILE,),
        in_specs=_conv_specs(t) + [pl.BlockSpec((3, CONV_COL_TILE), lambda j: (0, j))],
        out_specs=pl.BlockSpec((t, CONV_COL_TILE), lambda j: (0, j)),
        out_shape=jax.ShapeDtypeStruct((t, D_MODEL), BF16), compiler_params=_params(),
    )(bcu, bcu, bcu, w)


def _conv_bwd(bcu, w, dout, after):
    t = bcu.shape[0]
    after = [a for a in after if a is not None]

    def body(b_ref, c_ref, u_ref, w_ref, do_ref, *rest):
        db_ref, dc_ref, du_ref, dw_ref = rest[len(after):]
        _, vjp = jax.vjp(_conv_gate, b_ref[...], c_ref[...], u_ref[...], w_ref[0:1, :], w_ref[1:2, :], w_ref[2:3, :])
        db, dc, du, dw0, dw1, dw2 = vjp(do_ref[...])
        db_ref[...] = db.astype(db_ref.dtype)
        dc_ref[...] = dc.astype(dc_ref.dtype)
        du_ref[...] = du.astype(du_ref.dtype)
        dw_ref[0:1, :] = dw0
        dw_ref[1:2, :] = dw1
        dw_ref[2:3, :] = dw2

    col = pl.BlockSpec((t, CONV_COL_TILE), lambda j: (0, j))
    return pl.pallas_call(
        body, name="conv_bwd", grid=(D_MODEL // CONV_COL_TILE,),
        in_specs=_conv_specs(t) + [pl.BlockSpec((3, CONV_COL_TILE), lambda j: (0, j)), col]
        + [pl.BlockSpec(memory_space=pl.ANY)] * len(after),
        out_specs=[col, col, col, pl.BlockSpec((3, CONV_COL_TILE), lambda j: (0, j))],
        out_shape=[jax.ShapeDtypeStruct((t, D_MODEL), BF16)] * 3 + [jax.ShapeDtypeStruct((3, D_MODEL), F32)],
        compiler_params=_params(),
    )(bcu, bcu, bcu, w, dout, *after)


def _loss_head(y, target):
    t, d = y.shape
    tm = 256

    def body(y_ref, t_ref, loss_ref, dy_ref):
        @pl.when(pl.program_id(0) == 0)
        def _():
            loss_ref[...] = jnp.zeros_like(loss_ref)

        err = y_ref[...] - t_ref[...]
        dy_ref[...] = err * (1.0 / d)
        loss_ref[...] += 0.5 * jnp.sum(jnp.sum(err * err, axis=-1, keepdims=True) * (1.0 / d))

    tile = pl.BlockSpec((tm, d), lambda i: (i, 0))
    return pl.pallas_call(
        body, name="loss_head", grid=(t // tm,), in_specs=[tile, tile],
        out_specs=[pl.BlockSpec((8, LANES), lambda i: (0, 0)), tile],
        out_shape=[jax.ShapeDtypeStruct((8, LANES), F32), jax.ShapeDtypeStruct((t, d), F32)],
        compiler_params=_params(),
    )(y, target)


def _ln_epi(acc, res, g, b):
    a = ALPHA * res + acc
    return a, _layer_norm(a, g, b)


def _ln_fn(a, g, b):
    return (_layer_norm(a, g, b),)


def _ln_bwd_epi(scale):
    def epi(acc, res, a, g, b):
        _, vjp = jax.vjp(_ln_fn, a, g, b)
        return vjp((acc + scale * res,))
    return epi


def _relu_sq(h):
    r = jnp.maximum(h.astype(F32), 0.0)
    return r * r


def _pad_cols(w, n):
    return jnp.pad(w, ((0, 0), (0, n - w.shape[1])))


def _pad_rows(w, n):
    return jnp.pad(w, ((0, n - w.shape[0]), (0, 0)))


def _uq_to_kernel_layout(w_uq):
    w = w_uq.reshape(MLA_RANK, MLA_HEADS, MLA_NOPE + MLA_ROPE)
    nope = w[:, :, :MLA_NOPE].reshape(MLA_RANK, MLA_HEADS * MLA_NOPE)
    rope = jnp.pad(w[:, :, MLA_NOPE:], ((0, 0), (0, 0), (0, LANES - MLA_ROPE))).reshape(MLA_RANK, MLA_HEADS * LANES)
    return jnp.concatenate([nope, rope], axis=1)


def _uq_from_kernel_layout(w):
    nope = w[:, :MLA_HEADS * MLA_NOPE].reshape(MLA_RANK, MLA_HEADS, MLA_NOPE)
    rope = w[:, MLA_HEADS * MLA_NOPE:].reshape(MLA_RANK, MLA_HEADS, LANES)[:, :, :MLA_ROPE]
    return jnp.concatenate([nope, rope], axis=2).reshape(MLA_RANK, MLA_HEADS * (MLA_NOPE + MLA_ROPE))


def _step(x, p, positions, target, small, comm):
    t = x.shape[0]
    w = small
    freqs = ROPE_BASE ** (-jnp.arange(0, MLA_ROPE // 2, dtype=F32) * (2.0 / MLA_ROPE))
    freq_row = jnp.concatenate([freqs, freqs, jnp.zeros((LANES - MLA_ROPE,), F32)])[None, :]
    cos, sin = _rope_tables(positions.reshape(t, 1), freq_row)

    saved = []
    for i in range(DEPTH):
        j, kind = i // 3, i % 3
        wl = comm.mixer_weights(i)
        s = {"x": x, "wl": wl}
        tok = comm.at("fwd", i, "begin", x)
        if kind == 0:
            s["w_main"] = wl["gla_w_in_t"][:GLA_MAIN]
            s["w_lr"] = _pad_rows(wl["gla_w_in_t"][GLA_MAIN:], LANES)
            s["w_up"] = _pad_rows(w["gla_w_gate_up"][j], LANES).astype(BF16)
            s["proj"] = _matmul(x, s["w_main"], name="gla_proj", tb=True, tn=1024, after=tok)
            s["glr"] = _matmul(x, s["w_lr"], name="gla_lr", tb=True, out_dtypes=(F32,))
            s["z"] = _matmul(s["glr"], s["w_up"], name="gla_gate", epi=lambda acc, b: (acc + b,),
                             epi_ins=(w["gla_b_gate"][j][None, :],), out_dtypes=(F32,))
            s["u"], s["states"] = _gla_fwd(s["proj"], s["z"], w["gla_norm_g"][j][None, :])
        elif kind == 1:
            s["w_in"] = _pad_cols(wl["mla_w_in"], MLA_IN_PAD)
            s["w_uq"] = _uq_to_kernel_layout(wl["mla_w_uq"])
            s["cq"] = _matmul(x, s["w_in"], name="mla_proj", tn=MLA_IN_PAD, out_dtypes=(F32,), after=tok)
            s["pre_params"] = (w["mla_q_norm"][j][None, :], w["mla_kv_norm"][j][None, :], s["w_uq"], wl["mla_w_ukv"])
            s["qn"], s["qr"], s["kv"], s["kr"] = _tile_fwd(_mla_pre, (s["cq"], cos, sin), s["pre_params"],
                                                           (BF16, BF16, BF16, BF16), tm=256, name="mla_pre_fwd")
            s["u"] = _mla_attn_fwd(s["qn"], s["qr"], s["kv"], s["kr"])
        else:
            s["bcu"] = _matmul(x, wl["conv"], name="conv_proj", tb=True, tm=256, tn=3 * D_MODEL, b_at=REG_CONV,
                               out_dtypes=(F32,), after=tok)
            s["u"] = _conv_fwd(s["bcu"], w["conv_w"][j])
        g0, b0 = w["ln_g"][i, 0][None, :], w["ln_b"][i, 0][None, :]
        g1, b1 = w["ln_g"][i, 1][None, :], w["ln_b"][i, 1][None, :]
        wa, wb = s["wa"], _ = comm.slab_weights(i, s["u"])
        s["a1"], s["x1"] = _matmul(s["u"], wa, name="mixer_out_ln", tm=256, tn=D_MODEL, b_at=REG_WOUT, epi=_ln_epi,
                                   epi_ins=(x, g0, b0), out_dtypes=(F32, F32))
        s["hh"] = _matmul(s["x1"], wa, name="mlp_up", tb=True, tm=256, tn=D_FF, b_at=REG_W1T)
        tok = comm.at("fwd", i, "mid", s["hh"])
        s["a2"], s["x2"] = _matmul(s["hh"], wa, name="mlp_down_ln", tm=256, tn=D_MODEL, b_at=REG_W2, a_fn=_relu_sq,
                                   epi=_ln_epi, epi_ins=(s["x1"], g1, b1), out_dtypes=(F32, F32), after=tok)
        s["pp"] = _matmul(p[i], wb, name="ple_proj", tb=True, tn=D_MODEL, b_at=REG_WPT)
        tok = comm.at("fwd", i, "end", s["pp"])
        x, s["gt"] = _matmul(s["x2"], wa, name="ple_gate", tn=1024, b_at=REG_WG,
                             epi=lambda acc, xr, pp: (xr + jax.nn.sigmoid(acc) * pp.astype(F32), acc),
                             epi_ins=(s["x2"], s["pp"]), out_dtypes=(F32, BF16), after=tok)
        saved.append(s)

    loss_part, dx = _loss_head(x, target)

    gw = {n: [None] * WEIGHTS[n][0][0] for n in SMALL + REPLICATED}
    ln_g_grads, ln_b_grads = [[None, None] for _ in range(DEPTH)], [[None, None] for _ in range(DEPTH)]
    resid = lambda acc, r: (acc + ALPHA * r,)
    plus = lambda acc, r: (acc + r,)
    for i in reversed(range(DEPTH)):
        j, kind = i // 3, i % 3
        s = saved[i]
        wa = s["wa"]
        ga = lax.empty((N_DEV, A_ROWS, D_MODEL), BF16)
        gb = lax.empty((N_DEV, REG_WPT[1], PLE_DIM), BF16)
        layer_grads = {}
        tok = comm.at("bwd", i, "begin", dx)

        def ple_bwd(dxo, gt, pp):
            sg = jax.nn.sigmoid(gt)
            return dxo * sg, dxo * pp * sg * (1.0 - sg)

        d_pp, d_gt = _tile_fwd(ple_bwd, (dx, s["gt"], s["pp"]), (), (BF16, BF16), tm=256, name="ple_bwd")
        gb = _matmul(d_pp, p[i], name="ple_proj_dw", ta=True, tm=REG_WPT[1], tn=PLE_DIM, out_at=REG_WPT, out_buf=gb, after=tok)
        ga = _matmul(s["x2"], d_gt, name="ple_gate_dw", ta=True, tm=REG_WG[1], tn=1024, out_at=REG_WG, out_buf=ga)
        g1, b1 = w["ln_g"][i, 1][None, :], w["ln_b"][i, 1][None, :]
        d_a2, ln_g_grads[i][1], ln_b_grads[i][1] = _matmul(
            d_gt, wa, name="ple_gate_dx_ln", tb=True, tm=256, tn=D_MODEL, b_at=REG_WG, epi=_ln_bwd_epi(1.0),
            epi_ins=(dx, s["a2"], g1, b1), out_dtypes=(F32,), n_row_sums=2, after=[ga, gb])
        tok = comm.at("bwd", i, "ln", d_a2)
        ga = _matmul(s["hh"], d_a2, name="mlp_down_dw", ta=True, tm=REG_W2[1], tn=1024, a_fn=_relu_sq, out_at=REG_W2,
                     out_buf=ga, after=tok)
        d_hh = _matmul(d_a2, wa, name="mlp_down_dx", tb=True, tm=256, tn=D_FF, b_at=REG_W2, after=ga,
                       epi=lambda acc, hh: (acc * 2.0 * jnp.maximum(hh.astype(F32), 0.0),), epi_ins=(s["hh"],))
        ga = _matmul(d_hh, s["x1"], name="mlp_up_dw", ta=True, tm=REG_W1T[1], tn=1024, out_at=REG_W1T, out_buf=ga)
        g0, b0 = w["ln_g"][i, 0][None, :], w["ln_b"][i, 0][None, :]
        d_a1, ln_g_grads[i][0], ln_b_grads[i][0] = _matmul(
            d_hh, wa, name="mlp_up_dx_ln", tm=256, tn=D_MODEL, b_at=REG_W1T, epi=_ln_bwd_epi(ALPHA),
            epi_ins=(d_a2, s["a1"], g0, b0), out_dtypes=(F32,), n_row_sums=2, after=ga)
        ga = _matmul(s["u"], d_a1, name="mixer_out_dw", ta=True, tm=REG_WOUT[1], tn=1024, out_at=REG_WOUT, out_buf=ga)
        du = _matmul(d_a1, wa, name="mixer_out_dx", tb=True, tn=1024, b_at=REG_WOUT, out_dtypes=(F32,), after=ga)
        comm.slab_grads(i, ga, gb)
        tok = comm.at("bwd", i, "slab_done", du) or []
        if kind == 0:
            dproj, dz, dg = _gla_bwd(s["proj"], s["z"], w["gla_norm_g"][j][None, :], s["states"], du, tok)
            tok = comm.at("bwd", i, "mixer_done", dproj)
            gw["gla_norm_g"][j] = dg[0]
            gw["gla_b_gate"][j] = _tile_bwd(lambda zz, b: (zz + b,), (s["z"],), (w["gla_b_gate"][j][None, :],), (dz,), (),
                                            tm=256, name="gla_bias_bwd", diff_tiled=[])[1][0][0]
            gw["gla_w_gate_up"][j] = _matmul(s["glr"], dz, name="gla_gate_dw", ta=True, out_dtypes=(F32,),
                                             after=tok)[:GLA_RANK]
            dglr = _matmul(dz, s["w_up"], name="gla_gate_dx", tb=True, out_dtypes=(F32,))
            dw_main = _matmul(dproj, s["x"], name="gla_proj_dw", ta=True, tn=1024, out_dtypes=(F32,))
            dw_lr = _matmul(dglr, s["x"], name="gla_lr_dw", ta=True, tn=1024, out_dtypes=(F32,))[:GLA_RANK]
            layer_grads["gla_w_in_t"] = jnp.concatenate([dw_main, dw_lr], axis=0)
            dx = _matmul(dproj, s["w_main"], name="gla_proj_dx", tn=1024, epi=resid, epi_ins=(d_a1,),
                         out_dtypes=(F32,), after=[dw_main, dw_lr, gw["gla_w_gate_up"][j]])
            dx = _matmul(dglr, s["w_lr"], name="gla_lr_dx", tn=1024, epi=plus, epi_ins=(dx,), out_dtypes=(F32,))
        elif kind == 1:
            dqn, dqr, dkv, dkr = _mla_attn_bwd(s["qn"], s["qr"], s["kv"], s["kr"], du, tok)
            tok = comm.at("bwd", i, "mixer_done", dqn)
            (d_cq,), (dgq, dgkv, dw_uq, dw_ukv) = _tile_bwd(_mla_pre, (s["cq"], cos, sin), s["pre_params"],
                                                           (dqn, dqr, dkv, dkr), (BF16,), tm=256, name="mla_pre_bwd",
                                                           diff_tiled=[0])
            gw["mla_q_norm"][j], gw["mla_kv_norm"][j] = dgq[0], dgkv[0]
            layer_grads["mla_w_uq"] = _uq_from_kernel_layout(dw_uq)
            layer_grads["mla_w_ukv"] = dw_ukv
            layer_grads["mla_w_in"] = _matmul(s["x"], d_cq, name="mla_proj_dw", ta=True, tn=MLA_IN_PAD,
                                              out_dtypes=(F32,), after=tok)[:, :MLA_IN]
            dx = _matmul(d_cq, s["w_in"], name="mla_proj_dx", tb=True, tn=1024, epi=resid, epi_ins=(d_a1,),
                         out_dtypes=(F32,), after=layer_grads["mla_w_in"])
        else:
            db, dc, du_, dcw = _conv_bwd(s["bcu"], w["conv_w"][j], du, tok)
            tok = comm.at("bwd", i, "mixer_done", db)
            gw["conv_w"][j] = dcw
            dbcu = jnp.concatenate([db, dc, du_], axis=1)
            layer_grads["conv"] = _matmul(dbcu, s["x"], name="conv_proj_dw", ta=True, tm=REG_CONV[1], tn=1024,
                                          out_at=REG_CONV, out_buf=lax.empty((N_DEV, REG_CONV[1], D_MODEL), BF16),
                                          after=tok)
            dx = _matmul(dbcu, s["wl"]["conv"], name="conv_proj_dx", tn=1024, b_at=REG_CONV, epi=resid, epi_ins=(d_a1,),
                         out_dtypes=(F32,), after=layer_grads["conv"])
        comm.mixer_grads(i, layer_grads)

    gw["ln_g"] = [jnp.concatenate([a, b], axis=0) for a, b in ln_g_grads]
    gw["ln_b"] = [jnp.concatenate([a, b], axis=0) for a, b in ln_b_grads]
    return loss_part, dx, {n: jnp.stack(gw[n]).astype(F32) for n in gw}


MESH_IDS = pl.DeviceIdType.MESH
ANY = pl.BlockSpec(memory_space=pl.ANY)
HBM_SPEC = pl.BlockSpec(memory_space=pltpu.HBM)
SEM_SPEC = pl.BlockSpec(memory_space=pltpu.SEMAPHORE)
DATAFLOW_EFFECT = pltpu.SideEffectType.DATAFLOW_SIDE_EFFECTING
CORE_COPIES, CHIP_COPIES = 4, 3


def _my_place():
    return lax.axis_index("x"), lax.axis_index("y"), lax.axis_index("c")


def _other_chips(mx, my):
    return [(1 - mx, my), (mx, 1 - my), (1 - mx, 1 - my)]


def _remote(src, dst, send_sems, recv_sems, k, to):
    return pltpu.make_async_remote_copy(src_ref=src, dst_ref=dst, send_sem=send_sems.at[k], recv_sem=recv_sems.at[k],
                                        device_id=to, device_id_type=MESH_IDS)


def _gather_first_copies(n_arr):
    def make(bufs, send_sems, recv_sems):
        mx, my, mc = _my_place()
        mine = 4 * mx + 2 * my + mc
        peers = [(mx, my, 1 - mc)] + [(cx, cy, mc) for cx, cy in _other_chips(mx, my)]
        return [_remote(bufs[a].at[mine], bufs[a].at[mine], send_sems, recv_sems, (1 + CHIP_COPIES) * a + k, to)
                for a in range(n_arr) for k, to in enumerate(peers)]
    return make, (1 + CHIP_COPIES) * n_arr


def _gather_forward_copies(n_arr):
    def make(bufs, send_sems, recv_sems):
        mx, my, mc = _my_place()
        blocks = [4 * cx + 2 * cy + mc for cx, cy in _other_chips(mx, my)]
        return [_remote(bufs[a].at[blk], bufs[a].at[blk], send_sems, recv_sems, CHIP_COPIES * a + k, (mx, my, 1 - mc))
                for a in range(n_arr) for k, blk in enumerate(blocks)]
    return make, CHIP_COPIES * n_arr


def _scatter_core_copies(n_arr):
    def make(bufs, send_sems, recv_sems):
        mx, my, mc = _my_place()
        return [_remote(bufs[a].at[2 * k + (1 - mc)], bufs[n_arr + a].at[k], send_sems, recv_sems, CORE_COPIES * a + k,
                        (mx, my, 1 - mc)) for a in range(n_arr) for k in range(CORE_COPIES)]
    return make, CORE_COPIES * n_arr


def _scatter_chip_copies(n_arr):
    def make(bufs, send_sems, recv_sems):
        mx, my, mc = _my_place()
        return [_remote(bufs[a].at[2 * cx + cy], bufs[n_arr + a].at[k], send_sems, recv_sems, CHIP_COPIES * a + k,
                        (cx, cy, mc)) for a in range(n_arr) for k, (cx, cy) in enumerate(_other_chips(mx, my))]
    return make, CHIP_COPIES * n_arr


def _exchange(name, bufs, copies):
    make, n_copies = copies
    n = len(bufs)

    def body(*refs):
        descs = make(refs[:n], refs[2 * n], refs[2 * n + 1])
        for cp in descs:
            cp.start()
        for cp in descs:
            cp.wait()

    return pl.pallas_call(
        body, name=name, out_shape=[jax.ShapeDtypeStruct(b.shape, b.dtype) for b in bufs], in_specs=[ANY] * n,
        out_specs=[ANY] * n, input_output_aliases={i: i for i in range(n)},
        scratch_shapes=[pltpu.SemaphoreType.DMA((n_copies,)), pltpu.SemaphoreType.DMA((n_copies,))],
    )(*bufs)


def _exchange_start(name, bufs, copies, after):
    make, n_copies = copies
    n = len(bufs)

    def body(*refs):
        for cp in make(refs[:n], refs[n + 1], refs[n + 2]):
            cp.start()
        refs[-1][...] = jnp.zeros_like(refs[-1])

    outs = pl.pallas_call(
        body, name=name,
        out_shape=(pltpu.SemaphoreType.DMA((n_copies,)), pltpu.SemaphoreType.DMA((n_copies,)),
                   *[pltpu.HBM(b.shape, b.dtype) for b in bufs], jax.ShapeDtypeStruct((8, LANES), F32)),
        in_specs=[HBM_SPEC] * n + [ANY],
        out_specs=(SEM_SPEC, SEM_SPEC, *[HBM_SPEC] * n, pl.BlockSpec(memory_space=pltpu.VMEM)),
        input_output_aliases={i: 2 + i for i in range(n)},
        compiler_params=pltpu.CompilerParams(has_side_effects=DATAFLOW_EFFECT),
    )(*[pltpu.with_memory_space_constraint(b, pltpu.HBM) for b in bufs], after)
    return (outs[0], outs[1]), list(outs[2:2 + n]), outs[-1]


def _exchange_wait(name, sems, bufs, copies, after):
    make, _ = copies
    n = len(bufs)

    def body(*refs):
        for cp in make(refs[:n], refs[n], refs[n + 1]):
            cp.wait_send()
            cp.wait_recv()

    return list(pl.pallas_call(
        body, name=name, out_shape=[pltpu.HBM(b.shape, b.dtype) for b in bufs],
        in_specs=[HBM_SPEC] * n + [SEM_SPEC, SEM_SPEC, ANY], out_specs=[HBM_SPEC] * n,
        input_output_aliases={i: i for i in range(n)},
        compiler_params=pltpu.CompilerParams(has_side_effects=DATAFLOW_EFFECT),
    )(*bufs, *sems, after))


SUM_TILE_BYTES = 2 * 1024 * 1024


def _row_tile(r, c):
    best = None
    for cand in range(16, r + 1, 16):
        if r % cand == 0 and cand * c * 2 <= SUM_TILE_BYTES:
            best = cand
    return r if best is None else best


def _pair_sum(g, recv, my_c):
    _, r, c = g.shape
    tr = _row_tile(r, c)

    def body(c_ref, g_ref, r_ref, o_ref):
        o_ref[...] = (g_ref[...].astype(F32) + r_ref[...].astype(F32)).astype(o_ref.dtype)

    return pl.pallas_call(
        body, name="rs_pair_sum", out_shape=jax.ShapeDtypeStruct((4, r, c), g.dtype),
        grid_spec=pltpu.PrefetchScalarGridSpec(
            num_scalar_prefetch=1, grid=(4, r // tr),
            in_specs=[pl.BlockSpec((1, tr, c), lambda n, i, cr: (2 * n + cr[0], i, 0)),
                      pl.BlockSpec((1, tr, c), lambda n, i, cr: (n, i, 0))],
            out_specs=pl.BlockSpec((1, tr, c), lambda n, i, cr: (n, i, 0))),
        compiler_params=_params(),
    )(my_c, g, recv)


def _chip_sum(h, recv, my_chip):
    _, r, c = h.shape
    tr = _row_tile(r, c)

    def body(j_ref, h_ref, r0_ref, r1_ref, r2_ref, o_ref):
        o_ref[...] = ((h_ref[0].astype(F32) + r0_ref[0].astype(F32)) + r1_ref[0].astype(F32)) + r2_ref[0].astype(F32)

    return pl.pallas_call(
        body, name="rs_chip_sum", out_shape=jax.ShapeDtypeStruct((r, c), F32),
        grid_spec=pltpu.PrefetchScalarGridSpec(
            num_scalar_prefetch=1, grid=(r // tr,),
            in_specs=[pl.BlockSpec((1, tr, c), lambda i, jr: (jr[0], i, 0))]
            + [pl.BlockSpec((1, tr, c), lambda i, jr, n=n: (n, i, 0)) for n in range(3)],
            out_specs=pl.BlockSpec((tr, c), lambda i, jr: (i, 0))),
        compiler_params=_params(),
    )(my_chip, h, recv, recv, recv)


def _sum_blocks(g):
    n, r, c = g.shape

    def body(g_ref, o_ref):
        acc = g_ref[0]
        for k in range(1, n):
            acc = acc + g_ref[k]
        o_ref[...] = acc

    return pl.pallas_call(body, name="sum_blocks", out_shape=jax.ShapeDtypeStruct((r, c), F32), compiler_params=_params())(g)


def _pack(flat_parts, cols, row_multiple, dtype):
    flat = jnp.concatenate([f.astype(dtype) for f in flat_parts])
    per_row_block = cols * row_multiple
    padded = -(-flat.shape[0] // per_row_block) * per_row_block
    return jnp.pad(flat, (0, padded - flat.shape[0])).reshape(padded // cols, cols)


def _shard_shape(name):
    shape, axis = WEIGHTS[name]
    if axis is None:
        return shape
    return tuple(s // N_DEV if a == axis else s for a, s in enumerate(shape))


def _size(shape):
    n = 1
    for s in shape:
        n *= s
    return n


def _unshard(blocks, name):
    _, axis = WEIGHTS[name]
    return jnp.concatenate([blocks[k] for k in range(N_DEV)], axis=axis)


def _unpack_blocks(flat, names, lead):
    out, off = {}, 0
    for n in names:
        shp = _shard_shape(n)[1:] if lead else _shard_shape(n)
        out[n] = flat[..., off:off + _size(shp)].reshape(flat.shape[:-1] + shp)
        off += _size(shp)
    return out


def _layer_slabs(shard, i):
    j, kind = i // 3, i % 3
    w_out = (shard["gla_w_out"], shard["mla_w_out"], shard["conv_w_out"])[kind][j]
    out = {"a": jnp.concatenate([shard["mlp_w2"][i], shard["mlp_w1"][i].T, w_out, shard["ple_w_gate"][i]], axis=0).astype(BF16),
           "b": shard["ple_w_proj"][i].T.astype(BF16)}
    if kind == 0:
        out["gla"] = shard["gla_w_in"][j].T.astype(BF16)
    elif kind == 1:
        out["mla"] = _pack([shard[n][j].reshape(-1) for n in MLA_PACKED], PACK_COLS, PACK_ROW_TILE, BF16)
    else:
        out["conv"] = shard["conv_w_in"][j].T.astype(BF16)
    return out


def _mixer_weights(landed, i):
    kind = i % 3
    if kind == 0:
        return {"gla_w_in_t": landed["gla"].reshape(-1, D_MODEL)}
    if kind == 2:
        return {"conv": landed["conv"]}
    blocks = _unpack_blocks(landed["mla"].reshape(N_DEV, -1), MLA_PACKED, lead=True)
    return {n: jnp.concatenate([blocks[n][k] for k in range(N_DEV)], axis=WEIGHTS[n][1] - 1) for n in MLA_PACKED}


def _mixer_grad_buffers(layer_grads, i):
    kind = i % 3
    if kind == 0:
        return {"gla": layer_grads["gla_w_in_t"].reshape(N_DEV, -1, D_MODEL).astype(BF16)}
    if kind == 2:
        return {"conv": layer_grads["conv"]}
    parts = [jnp.stack(jnp.split(layer_grads[n], N_DEV, axis=WEIGHTS[n][1] - 1)).reshape(N_DEV, -1) for n in MLA_PACKED]
    cat = jnp.concatenate(parts, axis=1).astype(BF16)
    per = PACK_COLS * PACK_ROW_TILE
    padded = -(-cat.shape[1] // per) * per
    return {"mla": jnp.pad(cat, ((0, 0), (0, padded - cat.shape[1]))).reshape(N_DEV, padded // PACK_COLS, PACK_COLS)}


SLAB_KEYS = ("a", "b")


class _Overlap:
    def __init__(self, shard, small_pack):
        mx, my, mc = _my_place()
        self.my_c = mc.astype(jnp.int32).reshape(1)
        self.my_chip = (2 * mx + my).astype(jnp.int32).reshape(1)
        mine = 4 * mx + 2 * my + mc
        slabs = [_layer_slabs(shard, i) for i in range(DEPTH)]
        slabs[0]["small"] = small_pack
        self.landing = [{k: lax.dynamic_update_index_in_dim(lax.empty((N_DEV, *v.shape), v.dtype), v, mine, 0)
                         for k, v in slabs[i].items()} for i in range(DEPTH)]
        self.fly = {}
        self.gather_keys = {}
        self.grads = [{} for _ in range(DEPTH)]
        self.reduced = [{} for _ in range(DEPTH)]
        keys = self._keys(self.landing[0], "mixer")
        tok = self._start("ag_first_mixer_l0", [self.landing[0][k] for k in keys], _gather_first_copies(len(keys)), shard["ln_g"])
        tok = self._start("ag_first_slab_l0", [self.landing[0][k] for k in SLAB_KEYS], _gather_first_copies(2), tok)
        bufs = self._wait("ag_first_mixer_l0", tok)
        self.landing[0].update(zip(keys, _exchange("ag_forward_mixer_l0", bufs, _gather_forward_copies(len(bufs)))))

    @staticmethod
    def _keys(names, group):
        return [k for k in names if (k in SLAB_KEYS) == (group == "slab")]

    def _start(self, name, bufs, copies, after):
        sems, bufs, tok = _exchange_start(name + "_start", bufs, copies, after)
        self.fly[name] = (sems, bufs, copies)
        return tok

    def _wait(self, name, after):
        sems, bufs, copies = self.fly.pop(name)
        return _exchange_wait(name + "_wait", sems, bufs, copies, after)

    def mixer_weights(self, i):
        return _mixer_weights(self.landing[i], i)

    def slab_weights(self, i, dep):
        if i == 0:
            bufs = self._wait("ag_first_slab_l0", dep)
            self.landing[0].update(zip(SLAB_KEYS, _exchange("ag_forward_slab_l0", bufs, _gather_forward_copies(2))))
        return self.landing[i]["a"], self.landing[i]["b"]

    def slab_grads(self, i, ga, gb):
        self.grads[i].update(a=ga, b=gb)

    def mixer_grads(self, i, layer_grads):
        self.grads[i].update(_mixer_grad_buffers(layer_grads, i))

    def at(self, phase, i, point, dep):
        toks = []
        if phase == "fwd":
            if point == "begin" and i == 0:
                toks.append(self._gather_first(1, self.landing[0][self._keys(self.landing[0], "mixer")[0]]))
            if point == "mid" and i + 1 < DEPTH:
                bufs = self._wait(f"ag_first_l{i + 1}", dep)
                toks.append(self._start(f"ag_forward_l{i + 1}", bufs, _gather_forward_copies(len(bufs)), dep))
                if i + 2 < DEPTH:
                    toks.append(self._gather_first(i + 2, dep))
            if point == "end" and i + 1 < DEPTH:
                self.landing[i + 1].update(zip(self.gather_keys[i + 1], self._wait(f"ag_forward_l{i + 1}", dep)))
        else:
            if point == "begin" and i + 1 < DEPTH:
                toks.append(self._scatter_cores(i + 1, "mixer", dep))
            if point == "ln" and i + 1 < DEPTH:
                toks.append(self._scatter_chips(i + 1, "mixer", dep))
            if point == "slab_done":
                if i + 1 < DEPTH:
                    self._scatter_done(i + 1, "slab", dep)
                    self._scatter_done(i + 1, "mixer", dep)
                toks.append(self._scatter_cores(i, "slab", dep))
            if point == "mixer_done":
                toks.append(self._scatter_chips(i, "slab", dep))
        return toks or None

    def _gather_first(self, i, after):
        self.gather_keys[i] = list(self.landing[i])
        bufs = [self.landing[i][k] for k in self.gather_keys[i]]
        return self._start(f"ag_first_l{i}", bufs, _gather_first_copies(len(bufs)), after)

    def _scatter_cores(self, i, group, after):
        gs = [self.grads[i][k] for k in self._keys(self.grads[i], group)]
        land = [lax.empty((4, *g.shape[1:]), g.dtype) for g in gs]
        return self._start(f"rs_cores_{group}_l{i}", gs + land, _scatter_core_copies(len(gs)), after)

    def _pair_sums(self, bufs):
        n = len(bufs) // 2
        hs = [_pair_sum(g, r, self.my_c) for g, r in zip(bufs[:n], bufs[n:])]
        return hs + [lax.empty((3, *h.shape[1:]), h.dtype) for h in hs]

    def _scatter_chips(self, i, group, after):
        bufs = self._pair_sums(self._wait(f"rs_cores_{group}_l{i}", after))
        return self._start(f"rs_chips_{group}_l{i}", bufs, _scatter_chip_copies(len(bufs) // 2), after)

    def _chip_sums(self, i, group, bufs):
        n = len(bufs) // 2
        for k, h, r in zip(self._keys(self.grads[i], group), bufs[:n], bufs[n:]):
            self.reduced[i][k] = _chip_sum(h, r, self.my_chip)

    def _scatter_done(self, i, group, after):
        self._chip_sums(i, group, self._wait(f"rs_chips_{group}_l{i}", after))

    def tail_begin(self, dep):
        return self._scatter_cores(0, "mixer", dep)

    def tail_middle(self, dep):
        self._scatter_done(0, "slab", dep)
        return self._scatter_chips(0, "mixer", dep)

    def tail_end(self, dep):
        self._scatter_done(0, "mixer", dep)


def _small_gather_start(x, name, after):
    mx, my, mc = _my_place()
    land = lax.dynamic_update_index_in_dim(lax.empty((N_DEV, *x.shape), x.dtype), x, 4 * mx + 2 * my + mc, 0)
    return name, _exchange_start(name + "_first_start", [land], _gather_first_copies(1), after)


def _small_gather_finish(started, after):
    name, (sems, bufs, _) = started
    bufs = _exchange_wait(name + "_first_wait", sems, bufs, _gather_first_copies(1), after)
    return _exchange(name + "_forward", bufs, _gather_forward_copies(1))[0]


def _adamw_math(w, g, m, v):
    m2 = ADAM_B1 * m + (1.0 - ADAM_B1) * g
    v2 = ADAM_B2 * v + (1.0 - ADAM_B2) * (g * g)
    m_hat = m2 / (1.0 - ADAM_B1 ** ADAM_STEP)
    v_hat = v2 / (1.0 - ADAM_B2 ** ADAM_STEP)
    return -ADAM_LR * (m_hat / (jnp.sqrt(v_hat) + ADAM_EPS) + ADAM_WD * w), m2, v2


ADAMW_TILE_BYTES = 1024 * 1024


def _adamw_layer(name, w, m, v, j, g, g_at, transposed, chain, after):
    n_layers, r, c = w.shape
    tr = max(t for t in range(8, r + 1, 8) if r % t == 0 and (t * c * 4 <= ADAMW_TILE_BYTES or t == 8))
    rb, rows = g_at
    if transposed:
        assert rows == c and g.shape[1] == r, (name, g.shape, g_at)
        g_spec = pl.BlockSpec((rows, tr), lambda i: (rb, i))
    else:
        assert rows == r and g.shape[1] == c, (name, g.shape, g_at)
        g_spec = pl.BlockSpec((tr, c), lambda i: (rb * (r // tr) + i, 0))
    extra = list(chain or []) + [a for a in (after or []) if a is not None]
    n_chain = 4 if chain else 0

    def body(w_ref, m_ref, v_ref, g_ref, *rest):
        g_out, d_out, m_out, v_out, tok_ref = rest[len(extra):]
        gv = g_ref[...].T if transposed else g_ref[...]
        g_out[0] = gv
        d_out[0], m_out[0], v_out[0] = _adamw_math(w_ref[0], gv, m_ref[0], v_ref[0])
        tok_ref[...] = jnp.zeros_like(tok_ref)

    layer_spec = pl.BlockSpec((1, tr, c), lambda i: (j, i, 0))
    outs = pl.pallas_call(
        body, name=f"adamw_{name}_l{j}", grid=(r // tr,),
        in_specs=[layer_spec] * 3 + [g_spec] + [pl.BlockSpec(memory_space=pl.ANY)] * len(extra),
        out_specs=[layer_spec] * 4 + [pl.BlockSpec((8, LANES), lambda i: (0, 0))],
        out_shape=[jax.ShapeDtypeStruct(w.shape, F32)] * 4 + [jax.ShapeDtypeStruct((8, LANES), F32)],
        input_output_aliases={4 + k: k for k in range(n_chain)}, compiler_params=_params(),
    )(w, m, v, g, *extra)
    return list(outs[:4]), outs[4]


def _adamw(w, g, m, v, name):
    shape = w.shape
    cols = shape[-1]
    rows = _size(shape) // cols
    tr = rows
    for cand in (512, 256, 128, 64, 32, 16, 8):
        if rows > cand and rows % cand == 0:
            tr = cand
            break

    def body(w_ref, g_ref, m_ref, v_ref, d_ref, mo_ref, vo_ref):
        d_ref[...], mo_ref[...], vo_ref[...] = _adamw_math(w_ref[...], g_ref[...], m_ref[...], v_ref[...])

    spec = pl.BlockSpec((tr, cols), lambda i: (i, 0))
    outs = pl.pallas_call(
        body, name="adamw_" + name, grid=(rows // tr,), in_specs=[spec] * 4, out_specs=[spec] * 3,
        out_shape=[jax.ShapeDtypeStruct((rows, cols), F32)] * 3, compiler_params=_params(),
    )(*[a.reshape(rows, cols) for a in (w, g, m, v)])
    return [o.reshape(shape) for o in outs]


def kernel(x, p, positions, gla_w_in, gla_w_gate_up, gla_b_gate, gla_norm_g, gla_w_out, mla_w_in, mla_q_norm, mla_kv_norm, mla_w_uq, mla_w_ukv, mla_w_out, conv_w_in, conv_w, conv_w_out, ln_g, ln_b, mlp_w1, mlp_w2, ple_w_gate, ple_w_proj, loss_target, m_gla_w_in, m_gla_w_gate_up, m_gla_b_gate, m_gla_norm_g, m_gla_w_out, m_mla_w_in, m_mla_q_norm, m_mla_kv_norm, m_mla_w_uq, m_mla_w_ukv, m_mla_w_out, m_conv_w_in, m_conv_w, m_conv_w_out, m_ln_g, m_ln_b, m_mlp_w1, m_mlp_w2, m_ple_w_gate, m_ple_w_proj, v_gla_w_in, v_gla_w_gate_up, v_gla_b_gate, v_gla_norm_g, v_gla_w_out, v_mla_w_in, v_mla_q_norm, v_mla_kv_norm, v_mla_w_uq, v_mla_w_ukv, v_mla_w_out, v_conv_w_in, v_conv_w, v_conv_w_out, v_ln_g, v_ln_b, v_mlp_w1, v_mlp_w2, v_ple_w_gate, v_ple_w_proj):
    args = locals()
    shard = {n: args[n] for n in WEIGHT_NAMES}
    mom = {n: args["m_" + n] for n in WEIGHT_NAMES}
    var = {n: args["v_" + n] for n in WEIGHT_NAMES}
    mx, my, mc = _my_place()

    comm = _Overlap(shard, _pack([shard[n].reshape(-1) for n in SMALL], LANES, 8, F32))
    small_all = comm.landing[0]["small"]
    small = {n: shard[n] for n in REPLICATED}
    small.update({n: _unshard(blk, n) for n, blk in _unpack_blocks(small_all.reshape(N_DEV, -1), SMALL, lead=False).items()})
    loss_part, grad_x, small_grads = _step(x[0], p[:, 0], positions[0], loss_target[0], small, comm)

    chains = {}

    def update(name, j, g, g_at, transposed, tok):
        chains[name], tok = _adamw_layer(name, shard[name], mom[name], var[name], j, g, g_at, transposed,
                                         chains.get(name), [tok])
        return tok

    def update_layer(i, groups, tok):
        j, kind = i // 3, i % 3
        red = comm.reduced[i]
        if "slab" in groups:
            tok = update("mlp_w2", i, red["a"], REG_W2, False, tok)
            tok = update("mlp_w1", i, red["a"], REG_W1T, True, tok)
            tok = update(("gla_w_out", "mla_w_out", "conv_w_out")[kind], j, red["a"], REG_WOUT, False, tok)
            tok = update("ple_w_gate", i, red["a"], REG_WG, False, tok)
            tok = update("ple_w_proj", i, red["b"], REG_WPT, True, tok)
        if "mixer" in groups:
            if kind == 0:
                tok = update("gla_w_in", j, red["gla"].T, (0, D_MODEL), False, tok)
            elif kind == 2:
                tok = update("conv_w_in", j, red["conv"], REG_CONV, True, tok)
            else:
                for n, g in _unpack_blocks(red["mla"].reshape(-1), MLA_PACKED, lead=True).items():
                    tok = update(n, j, g, (0, g.shape[0]), False, tok)
        return tok

    tok = comm.tail_begin(grad_x)
    tok = update_layer(3, ("slab", "mixer"), tok)
    tok = update_layer(2, ("slab", "mixer"), tok)
    tok = comm.tail_middle(tok)
    small_parts = [loss_part[0, :1]] + [small_grads[n].reshape(-1) for n in SMALL + REPLICATED]
    small_gather = _small_gather_start(_pack(small_parts, LANES, 8, F32), "ag_small_grads", tok)
    tok = update_layer(1, ("slab", "mixer"), small_gather[1][2])
    tok = update_layer(0, ("slab",), tok)
    comm.tail_end(tok)
    tok = update_layer(0, ("mixer",), tok)
    red_small = _sum_blocks(_small_gather_finish(small_gather, tok)).reshape(-1)
    loss = red_small[0]
    off = 1
    dev = 4 * mx + 2 * my + mc
    for n in SMALL + REPLICATED:
        shape, axis = WEIGHTS[n]
        full_g = red_small[off:off + _size(shape)].reshape(shape)
        off += _size(shape)
        if axis is not None:
            width = shape[axis] // N_DEV
            full_g = lax.dynamic_slice_in_dim(full_g, dev * width, width, axis=axis)
        chains[n] = [full_g, *_adamw(shard[n], full_g, mom[n], var[n], n)]
    return (loss, grad_x[None], *[chains[n][k] for k in range(4) for n in WEIGHT_NAMES])
```

```python
import functools

import jax
import jax.numpy as jnp
from jax import lax
from jax.experimental import pallas as pl
from jax.experimental.pallas import tpu as pltpu

F32, BF16 = jnp.float32, jnp.bfloat16
HIGHEST = lax.Precision.HIGHEST
MESH_AXES = ("x", "y", "c")
N_DEV = 8

D_MODEL = 1024
SEQ = 2048
DEPTH = 4
CHUNK = 64
ALPHA = (2 * DEPTH) ** 0.25
LN_EPS = 1e-5
RMS_EPS = 1e-6
PLE_DIM = 256
D_FF = 4 * D_MODEL
GLA_HEADS = 4
GLA_DK = 128
GLA_DV = 256
GLA_RANK = 16
GLA_TAU = 16.0
GLA_HK = GLA_HEADS * GLA_DK
GLA_HV = GLA_HEADS * GLA_DV
GLA_MAIN = 2 * GLA_HK + GLA_HV + D_MODEL
MLA_HEADS = 8
MLA_NOPE = 128
MLA_ROPE = 64
MLA_V = 128
MLA_RANK = 256
MLA_IN = 2 * MLA_RANK + MLA_ROPE
MLA_IN_PAD = 640
ROPE_BASE = 10000.0
LANES = 128
ADAM_LR, ADAM_B1, ADAM_B2, ADAM_EPS, ADAM_WD, ADAM_STEP = 0.001, 0.9, 0.999, 1e-08, 0.01, 10

V7X_VMEM_LIMIT_BYTES = 56 * 1024 * 1024
PACK_COLS = 1024
PACK_ROW_TILE = 256

WEIGHTS = {
    "gla_w_in": ((2, 1024, 3088), 2), "gla_w_gate_up": ((2, 16, 512), 2), "gla_b_gate": ((2, 512), 1),
    "gla_norm_g": ((2, 256), 1), "gla_w_out": ((2, 1024, 1024), 1), "mla_w_in": ((1, 1024, 576), 1),
    "mla_q_norm": ((1, 256), None), "mla_kv_norm": ((1, 256), None), "mla_w_uq": ((1, 256, 1536), 2),
    "mla_w_ukv": ((1, 256, 2048), 2), "mla_w_out": ((1, 1024, 1024), 1), "conv_w_in": ((1, 1024, 3072), 2),
    "conv_w": ((1, 3, 1024), 2), "conv_w_out": ((1, 1024, 1024), 1), "ln_g": ((4, 2, 1024), 2),
    "ln_b": ((4, 2, 1024), 2), "mlp_w1": ((4, 1024, 4096), 2), "mlp_w2": ((4, 4096, 1024), 1),
    "ple_w_gate": ((4, 1024, 1024), 1), "ple_w_proj": ((4, 256, 1024), 2),
}
WEIGHT_NAMES = list(WEIGHTS)
REG_W2, REG_W1T, REG_WOUT, REG_WG = (0, 512), (1, 512), (8, 128), (9, 128)
A_ROWS = 1280
REG_CONV = (0, 384)
REG_WPT = (0, 128)
MLA_PACKED = ["mla_w_in", "mla_w_uq", "mla_w_ukv"]
SMALL = ["gla_w_gate_up", "gla_b_gate", "gla_norm_g", "conv_w", "ln_g", "ln_b"]
REPLICATED = ["mla_q_norm", "mla_kv_norm"]


def _params(**kw):
    return pltpu.CompilerParams(vmem_limit_bytes=V7X_VMEM_LIMIT_BYTES, **kw)


def _dot(a, b, ca, cb, precision=None):
    return lax.dot_general(a, b, (((ca,), (cb,)), ((), ())), precision=precision, preferred_element_type=F32)


def _nn(a, b):
    return _dot(a.astype(BF16), b.astype(BF16), 1, 0)


def _nt(a, b):
    return _dot(a.astype(BF16), b.astype(BF16), 1, 1)


def _tn(a, b):
    return _dot(a.astype(BF16), b.astype(BF16), 0, 0)


@jax.custom_vjp
def mm_nn(a, b):
    return _nn(a, b)


def _mm_nn_fwd(a, b):
    return _nn(a, b), (a, b)


def _mm_nn_bwd(res, g):
    a, b = res
    return _nt(g, b).astype(a.dtype), _tn(a, g).astype(b.dtype)


mm_nn.defvjp(_mm_nn_fwd, _mm_nn_bwd)


@jax.custom_vjp
def mm_nt(a, b):
    return _nt(a, b)


def _mm_nt_fwd(a, b):
    return _nt(a, b), (a, b)


def _mm_nt_bwd(res, g):
    a, b = res
    return _nn(g, b).astype(a.dtype), _tn(g, a).astype(b.dtype)


mm_nt.defvjp(_mm_nt_fwd, _mm_nt_bwd)


@jax.custom_vjp
def mm_tn(a, b):
    return _tn(a, b)


def _mm_tn_fwd(a, b):
    return _tn(a, b), (a, b)


def _mm_tn_bwd(res, g):
    a, b = res
    return _nt(b, g).astype(a.dtype), _nn(a, g).astype(b.dtype)


mm_tn.defvjp(_mm_tn_fwd, _mm_tn_bwd)


def _iota2(shape, dim):
    return lax.broadcasted_iota(jnp.int32, shape, dim)


@jax.custom_vjp
def cumsum_rows(x):
    n = x.shape[0]
    tri = (_iota2((n, n), 0) >= _iota2((n, n), 1)).astype(F32)
    return _dot(tri, x, 1, 0, precision=HIGHEST)


def _cumsum_fwd(x):
    return cumsum_rows(x), None


def _cumsum_bwd(_, g):
    n = g.shape[0]
    tri_t = (_iota2((n, n), 0) <= _iota2((n, n), 1)).astype(F32)
    return (_dot(tri_t, g, 1, 0, precision=HIGHEST),)


cumsum_rows.defvjp(_cumsum_fwd, _cumsum_bwd)


def _rot_matrix(transposed):
    i, j = _iota2((LANES, LANES), 0), _iota2((LANES, LANES), 1)
    if transposed:
        i, j = j, i
    half = MLA_ROPE // 2
    plus = (i == j - half) & (j >= half) & (j < MLA_ROPE)
    minus = (i == j + half) & (j < half)
    return plus.astype(F32) - minus.astype(F32)


@jax.custom_vjp
def rot_half(x):
    return _dot(x, _rot_matrix(False), 1, 0, precision=HIGHEST)


def _rot_fwd(x):
    return rot_half(x), None


def _rot_bwd(_, g):
    return (_dot(g, _rot_matrix(True), 1, 0, precision=HIGHEST),)


rot_half.defvjp(_rot_fwd, _rot_bwd)


def _shift_rows_raw(x, s):
    n = x.shape[0]
    row = _iota2(x.shape, 0)
    rolled = pltpu.roll(x, s % n, 0)
    keep = (row >= s) if s > 0 else (row < n + s)
    return jnp.where(keep, rolled, 0.0)


@functools.partial(jax.custom_vjp, nondiff_argnums=(1,))
def shift_rows(x, s):
    return _shift_rows_raw(x, s)


def _shift_fwd(x, s):
    return _shift_rows_raw(x, s), None


def _shift_bwd(s, _, g):
    return (_shift_rows_raw(g, -s),)


shift_rows.defvjp(_shift_fwd, _shift_bwd)


def _layer_norm(a, g, b):
    mu = jnp.mean(a, -1, keepdims=True)
    xc = a - mu
    var = jnp.mean(xc * xc, -1, keepdims=True)
    return xc * lax.rsqrt(var + LN_EPS) * g + b


def _rms_norm(a, g):
    return a * lax.rsqrt(jnp.mean(a * a, -1, keepdims=True) + RMS_EPS) * g


def _log_sigmoid(z):
    return jnp.minimum(z, 0.0) - jnp.log(1.0 + jnp.exp(-jnp.abs(z)))


def _matmul(a, b, *, name, ta=False, tb=False, tm=512, tn=512, a_fn=None, epi=None, epi_ins=(), out_dtypes=(BF16,),
            b_at=None, out_at=None, out_buf=None, after=None, n_row_sums=0):
    m = a.shape[1] if ta else a.shape[0]
    k = a.shape[0] if ta else a.shape[1]
    if b_at is None:
        n, kb = (b.shape[0], b.shape[1]) if tb else (b.shape[1], b.shape[0])
    else:
        rb, r = b_at
        n, kb = (N_DEV * r, b.shape[2]) if tb else (b.shape[2], N_DEV * r)
    assert kb == k, (name, a.shape, b.shape, k, kb)
    tm, tn = min(tm, m), min(tn, n)
    assert m % tm == 0 and n % tn == 0, (name, m, n, tm, tn)
    a_spec = pl.BlockSpec((k, tm), lambda i, j: (0, i)) if ta else pl.BlockSpec((tm, k), lambda i, j: (i, 0))
    if b_at is None:
        b_spec = pl.BlockSpec((tn, k), lambda i, j: (j, 0)) if tb else pl.BlockSpec((k, tn), lambda i, j: (0, j))
        load_b = lambda ref: ref[...]
    elif tb and tn == n:
        b_spec = pl.BlockSpec((N_DEV, r, k), lambda i, j: (0, rb, 0))
        load_b = lambda ref: ref[...].reshape(n, k)
    elif tb:
        assert tn == r, (name, tn, r)
        b_spec = pl.BlockSpec((1, r, k), lambda i, j: (j, rb, 0))
        load_b = lambda ref: ref[0]
    else:
        b_spec = pl.BlockSpec((N_DEV, r, tn), lambda i, j: (0, rb, j))
        load_b = lambda ref: ref[...].reshape(k, tn)
    e_specs = []
    for e in epi_ins:
        if e.shape == (1, n):
            e_specs.append(pl.BlockSpec((1, tn), lambda i, j: (0, j)))
        else:
            assert e.shape == (m, n), (name, e.shape, m, n)
            e_specs.append(pl.BlockSpec((tm, tn), lambda i, j: (i, j)))
    n_epi = len(epi_ins)
    ca, cb = (0 if ta else 1), (1 if tb else 0)
    operands = [a, b, *epi_ins]
    in_specs = [a_spec, b_spec, *e_specs]
    if out_at is None:
        assert n_row_sums == 0 or tn == n, (name, tn, n)
        out_specs = [pl.BlockSpec((tm, tn), lambda i, j: (i, j)) for _ in out_dtypes]
        out_specs += [pl.BlockSpec((1, n), lambda i, j: (0, 0))] * n_row_sums
        out_shape = [jax.ShapeDtypeStruct((m, n), dt) for dt in out_dtypes]
        out_shape += [jax.ShapeDtypeStruct((1, n), F32)] * n_row_sums
        aliases, n_buf = {}, 0
    else:
        orb, orows = out_at
        assert len(out_dtypes) == 1 and orows % tm == 0 and m == N_DEV * orows and n == out_buf.shape[2], (name, m, n)
        per = orows // tm
        out_specs = [pl.BlockSpec((1, tm, tn), lambda i, j: (i // per, orb * per + i % per, j))]
        out_shape = [jax.ShapeDtypeStruct(out_buf.shape, out_buf.dtype)]
        operands.append(out_buf)
        in_specs.append(pl.BlockSpec(memory_space=pl.ANY))
        aliases, n_buf = {len(operands) - 1: 0}, 1
    for dep in ([] if after is None else after if isinstance(after, (list, tuple)) else [after]):
        if dep is not None:
            operands.append(dep)
            in_specs.append(pl.BlockSpec(memory_space=pl.ANY))
            n_buf += 1

    def body(a_ref, b_ref, *rest):
        av = a_ref[...]
        if a_fn is not None:
            av = a_fn(av)
        acc = _dot(av.astype(BF16), load_b(b_ref).astype(BF16), ca, cb)
        outs = epi(acc, *[r_[...] for r_ in rest[:n_epi]]) if epi is not None else (acc,)
        o_refs = rest[n_epi + n_buf:]
        n_tiles = len(o_refs) - n_row_sums
        for o_ref, val in zip(o_refs[:n_tiles], outs):
            o_ref[...] = val.astype(o_ref.dtype).reshape(o_ref.shape)
        if n_row_sums:
            @pl.when(pl.program_id(0) == 0)
            def _():
                for o_ref in o_refs[n_tiles:]:
                    o_ref[...] = jnp.zeros_like(o_ref)

            for o_ref, val in zip(o_refs[n_tiles:], outs[n_tiles:]):
                o_ref[...] += val

    outs = pl.pallas_call(
        body, name=name, grid=(m // tm, n // tn), in_specs=in_specs, out_specs=out_specs, out_shape=out_shape,
        input_output_aliases=aliases, compiler_params=_params(),
    )(*operands)
    return outs[0] if len(outs) == 1 else tuple(outs)


def _tile_fwd(f, tiled, params, out_dtypes, *, tm, name):
    t = tiled[0].shape[0]
    assert t % tm == 0
    out_avals = jax.eval_shape(f, *[jax.ShapeDtypeStruct((tm, x.shape[1]), F32) for x in tiled],
                               *[jax.ShapeDtypeStruct(p.shape, F32) for p in params])
    nt, npar = len(tiled), len(params)

    def body(*refs):
        ins = [r[...].astype(F32) for r in refs[:nt + npar]]
        outs = f(*ins)
        for o_ref, val in zip(refs[nt + npar:], outs):
            o_ref[...] = val.astype(o_ref.dtype)

    return pl.pallas_call(
        body, name=name, grid=(t // tm,),
        in_specs=[pl.BlockSpec((tm, x.shape[1]), lambda i: (i, 0)) for x in tiled]
        + [pl.BlockSpec(p.shape, lambda i: (0, 0)) for p in params],
        out_specs=[pl.BlockSpec((tm, o.shape[1]), lambda i: (i, 0)) for o in out_avals],
        out_shape=[jax.ShapeDtypeStruct((t, o.shape[1]), dt) for o, dt in zip(out_avals, out_dtypes)],
        compiler_params=_params(),
    )(*tiled, *params)


def _tile_bwd(f, tiled, params, cots, d_tiled_dtypes, *, tm, name, diff_tiled=None):
    t = tiled[0].shape[0]
    assert t % tm == 0
    nt, npar, nc = len(tiled), len(params), len(cots)
    diff_tiled = list(range(nt)) if diff_tiled is None else diff_tiled

    def body(*refs):
        ins = [r[...].astype(F32) for r in refs[:nt + npar]]
        cts = [r[...].astype(F32) for r in refs[nt + npar:nt + npar + nc]]
        o_refs = refs[nt + npar + nc:]
        _, vjp = jax.vjp(f, *ins)
        grads = vjp(tuple(cts))
        for o_ref, idx in zip(o_refs[:len(diff_tiled)], diff_tiled):
            o_ref[...] = grads[idx].astype(o_ref.dtype)
        p_refs = o_refs[len(diff_tiled):]

        @pl.when(pl.program_id(0) == 0)
        def _():
            for p_ref in p_refs:
                p_ref[...] = jnp.zeros_like(p_ref)

        for p_ref, gp in zip(p_refs, grads[nt:]):
            p_ref[...] += gp

    outs = pl.pallas_call(
        body, name=name, grid=(t // tm,),
        in_specs=[pl.BlockSpec((tm, x.shape[1]), lambda i: (i, 0)) for x in tiled]
        + [pl.BlockSpec(p.shape, lambda i: (0, 0)) for p in params]
        + [pl.BlockSpec((tm, c.shape[1]), lambda i: (i, 0)) for c in cots],
        out_specs=[pl.BlockSpec((tm, tiled[idx].shape[1]), lambda i: (i, 0)) for idx in diff_tiled]
        + [pl.BlockSpec(p.shape, lambda i: (0, 0)) for p in params],
        out_shape=[jax.ShapeDtypeStruct(tiled[idx].shape, dt) for idx, dt in zip(diff_tiled, d_tiled_dtypes)]
        + [jax.ShapeDtypeStruct(p.shape, F32) for p in params],
        compiler_params=_params(),
    )(*tiled, *params, *cots)
    return outs[:len(diff_tiled)], outs[len(diff_tiled):]


def _gla_head(q, k, v, r, z, g, st):
    c = q.shape[0]
    causal = _iota2((c, c), 0) >= _iota2((c, c), 1)
    la = _log_sigmoid(z) * (1.0 / GLA_TAU)
    big_l = cumsum_rows(la)
    ep, en = jnp.exp(big_l), jnp.exp(-big_l)
    qs = q * (GLA_DK ** -0.5)
    qp = qs * ep
    s = jnp.where(causal, mm_nt(qp, k * en), mm_nt(qs * en, k * ep))
    o = mm_nn(s, v) + mm_nt(qp, st)
    l_end = jnp.sum(la, axis=0, keepdims=True)
    st_new = st * jnp.exp(l_end) + mm_tn(v, k * jnp.exp(l_end - big_l))
    u = _rms_norm(o, g) * (r * jax.nn.sigmoid(r))
    return u, st_new


def _gla_slices(h):
    q = slice(GLA_DK * h, GLA_DK * (h + 1))
    k = slice(GLA_HK + GLA_DK * h, GLA_HK + GLA_DK * (h + 1))
    v = slice(2 * GLA_HK + GLA_DV * h, 2 * GLA_HK + GLA_DV * (h + 1))
    r = slice(2 * GLA_HK + GLA_HV + GLA_DV * h, 2 * GLA_HK + GLA_HV + GLA_DV * (h + 1))
    return q, k, v, r


GLA_CHUNKS_PER_STEP = 2


def _gla_fwd(proj, z, norm_g, after):
    t = proj.shape[0]
    nc, per = t // CHUNK, GLA_CHUNKS_PER_STEP
    rows_per_step = per * CHUNK
    after = [a for a in after if a is not None]

    def body(proj_ref, z_ref, g_ref, *rest):
        u_ref, st_save_ref, st_ref = rest[len(after):]

        @pl.when(pl.program_id(0) == 0)
        def _():
            st_ref[...] = jnp.zeros_like(st_ref)

        g = g_ref[...]
        for h in range(GLA_HEADS):
            sq, sk, sv, sr = _gla_slices(h)
            st = st_ref[h]
            for c in range(per):
                rows = slice(c * CHUNK, (c + 1) * CHUNK)
                st_save_ref[c, h] = st
                u, st = _gla_head(proj_ref[rows, sq].astype(F32), proj_ref[rows, sk].astype(F32),
                                  proj_ref[rows, sv].astype(F32), proj_ref[rows, sr].astype(F32),
                                  z_ref[rows, GLA_DK * h:GLA_DK * (h + 1)], g, st)
                u_ref[rows, GLA_DV * h:GLA_DV * (h + 1)] = u.astype(u_ref.dtype)
            st_ref[h] = st

    return pl.pallas_call(
        body, name="gla_fwd", grid=(nc // per,),
        in_specs=[pl.BlockSpec((rows_per_step, GLA_MAIN), lambda i: (i, 0)),
                  pl.BlockSpec((rows_per_step, GLA_HK), lambda i: (i, 0)), pl.BlockSpec((1, GLA_DV), lambda i: (0, 0))]
        + [pl.BlockSpec(memory_space=pl.ANY)] * len(after),
        out_specs=[pl.BlockSpec((rows_per_step, GLA_HV), lambda i: (i, 0)),
                   pl.BlockSpec((per, GLA_HEADS, GLA_DV, GLA_DK), lambda i: (i, 0, 0, 0))],
        out_shape=[jax.ShapeDtypeStruct((t, GLA_HV), BF16), jax.ShapeDtypeStruct((nc, GLA_HEADS, GLA_DV, GLA_DK), F32)],
        scratch_shapes=[pltpu.VMEM((GLA_HEADS, GLA_DV, GLA_DK), F32)],
        compiler_params=_params(),
    )(proj, z, norm_g, *after)


def _gla_bwd(proj, z, norm_g, states, du, after):
    t = proj.shape[0]
    nc, per = t // CHUNK, GLA_CHUNKS_PER_STEP
    rows_per_step = per * CHUNK
    n_steps = nc // per
    after = [a for a in after if a is not None]

    def body(proj_ref, z_ref, g_ref, st_in_ref, du_ref, *rest):
        dproj_ref, dz_ref, dg_ref, dst_ref = rest[len(after):]

        @pl.when(pl.program_id(0) == 0)
        def _():
            dst_ref[...] = jnp.zeros_like(dst_ref)
            dg_ref[...] = jnp.zeros_like(dg_ref)

        g = g_ref[...]
        for h in range(GLA_HEADS):
            sq, sk, sv, sr = _gla_slices(h)
            dst = dst_ref[h]
            for c in reversed(range(per)):
                rows = slice(c * CHUNK, (c + 1) * CHUNK)
                ins = (proj_ref[rows, sq].astype(F32), proj_ref[rows, sk].astype(F32), proj_ref[rows, sv].astype(F32),
                       proj_ref[rows, sr].astype(F32), z_ref[rows, GLA_DK * h:GLA_DK * (h + 1)], g, st_in_ref[c, h])
                _, vjp = jax.vjp(_gla_head, *ins)
                dq, dk, dv, dr, dz, dg, dst = vjp((du_ref[rows, GLA_DV * h:GLA_DV * (h + 1)], dst))
                dproj_ref[rows, sq] = dq.astype(dproj_ref.dtype)
                dproj_ref[rows, sk] = dk.astype(dproj_ref.dtype)
                dproj_ref[rows, sv] = dv.astype(dproj_ref.dtype)
                dproj_ref[rows, sr] = dr.astype(dproj_ref.dtype)
                dz_ref[rows, GLA_DK * h:GLA_DK * (h + 1)] = dz
                dg_ref[...] += dg
            dst_ref[h] = dst

    rev = lambda i: (n_steps - 1 - i, 0)
    return pl.pallas_call(
        body, name="gla_bwd", grid=(n_steps,),
        in_specs=[pl.BlockSpec((rows_per_step, GLA_MAIN), rev), pl.BlockSpec((rows_per_step, GLA_HK), rev),
                  pl.BlockSpec((1, GLA_DV), lambda i: (0, 0)),
                  pl.BlockSpec((per, GLA_HEADS, GLA_DV, GLA_DK), lambda i: (n_steps - 1 - i, 0, 0, 0)),
                  pl.BlockSpec((rows_per_step, GLA_HV), rev)] + [pl.BlockSpec(memory_space=pl.ANY)] * len(after),
        out_specs=[pl.BlockSpec((rows_per_step, GLA_MAIN), rev), pl.BlockSpec((rows_per_step, GLA_HK), rev),
                   pl.BlockSpec((1, GLA_DV), lambda i: (0, 0))],
        out_shape=[jax.ShapeDtypeStruct((t, GLA_MAIN), BF16), jax.ShapeDtypeStruct((t, GLA_HK), F32),
                   jax.ShapeDtypeStruct((1, GLA_DV), F32)],
        scratch_shapes=[pltpu.VMEM((GLA_HEADS, GLA_DV, GLA_DK), F32)],
        compiler_params=_params(),
    )(proj, z, norm_g, states, du, *after)


def _mla_pre(cq, cos, sin, gq, gkv, w_uq, w_ukv):
    qlat = _rms_norm(cq[:, :MLA_RANK], gq)
    kvlat = _rms_norm(cq[:, MLA_RANK:2 * MLA_RANK], gkv)
    kr = cq[:, 2 * MLA_RANK:]
    scale = (MLA_NOPE + MLA_ROPE) ** -0.5
    q = mm_nn(qlat, w_uq) * scale
    kv = mm_nn(kvlat, w_ukv)
    n_nope = MLA_HEADS * MLA_NOPE
    ropes = []
    for h in range(MLA_HEADS):
        qr = q[:, n_nope + LANES * h:n_nope + LANES * (h + 1)]
        ropes.append(qr * cos + rot_half(qr) * sin)
    return q[:, :n_nope], jnp.concatenate(ropes, axis=1), kv, kr * cos + rot_half(kr) * sin


MLA_Q_TILE = 256


def _mla_attn_block(qn, qr, kv, kr, q0):
    tq, nk = qn.shape[0], kv.shape[0]
    s = mm_nt(qn, kv[:, :MLA_NOPE]) + mm_nt(qr, kr)
    visible = (_iota2((tq, nk), 1) // CHUNK) <= ((q0 + _iota2((tq, nk), 0)) // CHUNK)
    s = jnp.where(visible, s, -1e30)
    e = jnp.exp(s - jnp.max(s, -1, keepdims=True))
    p = e / jnp.sum(e, -1, keepdims=True)
    return mm_nn(p, kv[:, MLA_NOPE:])


def _mla_attn_fwd(qn, qr, kv, kr, after):
    t = qn.shape[0]
    after = [a for a in after if a is not None]

    def body(qn_ref, qr_ref, kv_ref, kr_ref, *rest):
        (o_ref,) = rest[len(after):]
        for i in range(t // MLA_Q_TILE):
            rows = slice(i * MLA_Q_TILE, (i + 1) * MLA_Q_TILE)
            keys = slice(0, (i + 1) * MLA_Q_TILE)
            o = _mla_attn_block(qn_ref[rows, :].astype(F32), qr_ref[rows, :].astype(F32), kv_ref[keys, :].astype(F32),
                                kr_ref[keys, :].astype(F32), i * MLA_Q_TILE)
            o_ref[rows, :] = o.astype(o_ref.dtype)

    return pl.pallas_call(
        body, name="mla_attn_fwd", grid=(MLA_HEADS,),
        in_specs=[pl.BlockSpec((t, MLA_NOPE), lambda h: (0, h)), pl.BlockSpec((t, LANES), lambda h: (0, h)),
                  pl.BlockSpec((t, MLA_NOPE + MLA_V), lambda h: (0, h)), pl.BlockSpec((t, LANES), lambda h: (0, 0))]
        + [pl.BlockSpec(memory_space=pl.ANY)] * len(after),
        out_specs=pl.BlockSpec((t, MLA_V), lambda h: (0, h)),
        out_shape=jax.ShapeDtypeStruct((t, MLA_HEADS * MLA_V), BF16),
        compiler_params=_params(),
    )(qn, qr, kv, kr, *after)


def _mla_attn_bwd(qn, qr, kv, kr, do, after):
    t = qn.shape[0]
    after = [a for a in after if a is not None]

    def body(qn_ref, qr_ref, kv_ref, kr_ref, do_ref, *rest):
        dqn_ref, dqr_ref, dkv_ref, dkr_ref = rest[len(after):]
        dkv_ref[...] = jnp.zeros_like(dkv_ref)

        @pl.when(pl.program_id(0) == 0)
        def _():
            dkr_ref[...] = jnp.zeros_like(dkr_ref)

        for i in range(t // MLA_Q_TILE):
            rows = slice(i * MLA_Q_TILE, (i + 1) * MLA_Q_TILE)
            keys = slice(0, (i + 1) * MLA_Q_TILE)
            f = functools.partial(_mla_attn_block, q0=i * MLA_Q_TILE)
            _, vjp = jax.vjp(f, qn_ref[rows, :].astype(F32), qr_ref[rows, :].astype(F32), kv_ref[keys, :].astype(F32),
                             kr_ref[keys, :].astype(F32))
            dqn, dqr, dkv, dkr = vjp(do_ref[rows, :].astype(F32))
            dqn_ref[rows, :] = dqn
            dqr_ref[rows, :] = dqr
            dkv_ref[keys, :] += dkv
            dkr_ref[keys, :] += dkr

    return pl.pallas_call(
        body, name="mla_attn_bwd", grid=(MLA_HEADS,),
        in_specs=[pl.BlockSpec((t, MLA_NOPE), lambda h: (0, h)), pl.BlockSpec((t, LANES), lambda h: (0, h)),
                  pl.BlockSpec((t, MLA_NOPE + MLA_V), lambda h: (0, h)), pl.BlockSpec((t, LANES), lambda h: (0, 0)),
                  pl.BlockSpec((t, MLA_V), lambda h: (0, h))] + [pl.BlockSpec(memory_space=pl.ANY)] * len(after),
        out_specs=[pl.BlockSpec((t, MLA_NOPE), lambda h: (0, h)), pl.BlockSpec((t, LANES), lambda h: (0, h)),
                   pl.BlockSpec((t, MLA_NOPE + MLA_V), lambda h: (0, h)), pl.BlockSpec((t, LANES), lambda h: (0, 0))],
        out_shape=[jax.ShapeDtypeStruct(qn.shape, F32), jax.ShapeDtypeStruct(qr.shape, F32),
                   jax.ShapeDtypeStruct(kv.shape, F32), jax.ShapeDtypeStruct(kr.shape, F32)],
        compiler_params=_params(),
    )(qn, qr, kv, kr, do, *after)


def _rope_tables(pos_col, inv_freq_row):
    t = pos_col.shape[0]

    def body(pos_ref, f_ref, cos_ref, sin_ref):
        ang = pos_ref[...].astype(F32) * f_ref[...]
        live = _iota2(ang.shape, 1) < MLA_ROPE
        cos_ref[...] = jnp.where(live, jnp.cos(ang), 0.0)
        sin_ref[...] = jnp.where(live, jnp.sin(ang), 0.0)

    return pl.pallas_call(
        body, name="rope_tables", out_shape=[jax.ShapeDtypeStruct((t, LANES), F32)] * 2, compiler_params=_params(),
    )(pos_col, inv_freq_row)


CONV_COL_TILE = 256


def _conv_gate(b, c, u, w0, w1, w2):
    cu = c * u
    return b * (w2 * cu + w1 * shift_rows(cu, 1) + w0 * shift_rows(cu, 2))


def _conv_specs(t):
    nb = D_MODEL // CONV_COL_TILE
    return [pl.BlockSpec((t, CONV_COL_TILE), lambda j, part=part: (0, part * nb + j)) for part in range(3)]


def _conv_fwd(bcu, w, after):
    t = bcu.shape[0]
    after = [a for a in after if a is not None]

    def body(b_ref, c_ref, u_ref, w_ref, *rest):
        (o_ref,) = rest[len(after):]
        o_ref[...] = _conv_gate(b_ref[...], c_ref[...], u_ref[...], w_ref[0:1, :], w_ref[1:2, :],
                                w_ref[2:3, :]).astype(o_ref.dtype)

    return pl.pallas_call(
        body, name="conv_fwd", grid=(D_MODEL // CONV_COL_TILE,),
        in_specs=_conv_specs(t) + [pl.BlockSpec((3, CONV_COL_TILE), lambda j: (0, j))]
        + [pl.BlockSpec(memory_space=pl.ANY)] * len(after),
        out_specs=pl.BlockSpec((t, CONV_COL_TILE), lambda j: (0, j)),
        out_shape=jax.ShapeDtypeStruct((t, D_MODEL), BF16), compiler_params=_params(),
    )(bcu, bcu, bcu, w, *after)


def _conv_bwd(bcu, w, dout, after):
    t = bcu.shape[0]
    after = [a for a in after if a is not None]

    def body(b_ref, c_ref, u_ref, w_ref, do_ref, *rest):
        db_ref, dc_ref, du_ref, dw_ref = rest[len(after):]
        _, vjp = jax.vjp(_conv_gate, b_ref[...], c_ref[...], u_ref[...], w_ref[0:1, :], w_ref[1:2, :], w_ref[2:3, :])
        db, dc, du, dw0, dw1, dw2 = vjp(do_ref[...])
        db_ref[...] = db.astype(db_ref.dtype)
        dc_ref[...] = dc.astype(dc_ref.dtype)
        du_ref[...] = du.astype(du_ref.dtype)
        dw_ref[0:1, :] = dw0
        dw_ref[1:2, :] = dw1
        dw_ref[2:3, :] = dw2

    col = pl.BlockSpec((t, CONV_COL_TILE), lambda j: (0, j))
    return pl.pallas_call(
        body, name="conv_bwd", grid=(D_MODEL // CONV_COL_TILE,),
        in_specs=_conv_specs(t) + [pl.BlockSpec((3, CONV_COL_TILE), lambda j: (0, j)), col]
        + [pl.BlockSpec(memory_space=pl.ANY)] * len(after),
        out_specs=[col, col, col, pl.BlockSpec((3, CONV_COL_TILE), lambda j: (0, j))],
        out_shape=[jax.ShapeDtypeStruct((t, D_MODEL), BF16)] * 3 + [jax.ShapeDtypeStruct((3, D_MODEL), F32)],
        compiler_params=_params(),
    )(bcu, bcu, bcu, w, dout, *after)


def _loss_head(y, target):
    t, d = y.shape
    tm = 256

    def body(y_ref, t_ref, loss_ref, dy_ref):
        @pl.when(pl.program_id(0) == 0)
        def _():
            loss_ref[...] = jnp.zeros_like(loss_ref)

        err = y_ref[...] - t_ref[...]
        dy_ref[...] = err * (1.0 / d)
        loss_ref[...] += 0.5 * jnp.sum(jnp.sum(err * err, axis=-1, keepdims=True) * (1.0 / d))

    tile = pl.BlockSpec((tm, d), lambda i: (i, 0))
    return pl.pallas_call(
        body, name="loss_head", grid=(t // tm,), in_specs=[tile, tile],
        out_specs=[pl.BlockSpec((8, LANES), lambda i: (0, 0)), tile],
        out_shape=[jax.ShapeDtypeStruct((8, LANES), F32), jax.ShapeDtypeStruct((t, d), F32)],
        compiler_params=_params(),
    )(y, target)


def _ln_epi(acc, res, g, b):
    a = ALPHA * res + acc
    return a, _layer_norm(a, g, b)


def _ln_fn(a, g, b):
    return (_layer_norm(a, g, b),)


def _ln_bwd_epi(scale):
    def epi(acc, res, a, g, b):
        _, vjp = jax.vjp(_ln_fn, a, g, b)
        return vjp((acc + scale * res,))
    return epi


def _relu_sq(h):
    r = jnp.maximum(h.astype(F32), 0.0)
    return r * r


def _pad_cols(w, n):
    return jnp.pad(w, ((0, 0), (0, n - w.shape[1])))


def _pad_rows(w, n):
    return jnp.pad(w, ((0, n - w.shape[0]), (0, 0)))


def _uq_to_kernel_layout(w_uq):
    w = w_uq.reshape(MLA_RANK, MLA_HEADS, MLA_NOPE + MLA_ROPE)
    nope = w[:, :, :MLA_NOPE].reshape(MLA_RANK, MLA_HEADS * MLA_NOPE)
    rope = jnp.pad(w[:, :, MLA_NOPE:], ((0, 0), (0, 0), (0, LANES - MLA_ROPE))).reshape(MLA_RANK, MLA_HEADS * LANES)
    return jnp.concatenate([nope, rope], axis=1)


def _uq_from_kernel_layout(w):
    nope = w[:, :MLA_HEADS * MLA_NOPE].reshape(MLA_RANK, MLA_HEADS, MLA_NOPE)
    rope = w[:, MLA_HEADS * MLA_NOPE:].reshape(MLA_RANK, MLA_HEADS, LANES)[:, :, :MLA_ROPE]
    return jnp.concatenate([nope, rope], axis=2).reshape(MLA_RANK, MLA_HEADS * (MLA_NOPE + MLA_ROPE))


def _step(x, p, positions, target, small, comm):
    t = x.shape[0]
    w = small
    freqs = ROPE_BASE ** (-jnp.arange(0, MLA_ROPE // 2, dtype=F32) * (2.0 / MLA_ROPE))
    freq_row = jnp.concatenate([freqs, freqs, jnp.zeros((LANES - MLA_ROPE,), F32)])[None, :]
    cos, sin = _rope_tables(positions.reshape(t, 1), freq_row)

    saved = []
    for i in range(DEPTH):
        j, kind = i // 3, i % 3
        wl = comm.mixer_weights(i)
        s = {"x": x, "wl": wl}
        tok = comm.at("fwd", i, "begin", x)
        if kind == 0:
            s["w_main"] = wl["gla_w_in_t"][:GLA_MAIN]
            s["w_lr"] = _pad_rows(wl["gla_w_in_t"][GLA_MAIN:], LANES)
            s["w_up"] = _pad_rows(w["gla_w_gate_up"][j], LANES).astype(BF16)
            s["proj"] = _matmul(x, s["w_main"], name="gla_proj", tb=True, tn=1024, after=tok)
            s["glr"] = _matmul(x, s["w_lr"], name="gla_lr", tb=True, out_dtypes=(F32,))
            s["z"] = _matmul(s["glr"], s["w_up"], name="gla_gate", epi=lambda acc, b: (acc + b,),
                             epi_ins=(w["gla_b_gate"][j][None, :],), out_dtypes=(F32,))
            tok = comm.at("fwd", i, "proj_done", s["z"]) or []
            s["u"], s["states"] = _gla_fwd(s["proj"], s["z"], w["gla_norm_g"][j][None, :], tok)
        elif kind == 1:
            s["w_in"] = _pad_cols(wl["mla_w_in"], MLA_IN_PAD)
            s["w_uq"] = _uq_to_kernel_layout(wl["mla_w_uq"])
            s["cq"] = _matmul(x, s["w_in"], name="mla_proj", tn=MLA_IN_PAD, out_dtypes=(F32,), after=tok)
            s["pre_params"] = (w["mla_q_norm"][j][None, :], w["mla_kv_norm"][j][None, :], s["w_uq"], wl["mla_w_ukv"])
            s["qn"], s["qr"], s["kv"], s["kr"] = _tile_fwd(_mla_pre, (s["cq"], cos, sin), s["pre_params"],
                                                           (BF16, BF16, BF16, BF16), tm=256, name="mla_pre_fwd")
            tok = comm.at("fwd", i, "proj_done", s["kv"]) or []
            s["u"] = _mla_attn_fwd(s["qn"], s["qr"], s["kv"], s["kr"], tok)
        else:
            s["bcu"] = _matmul(x, wl["conv"], name="conv_proj", tb=True, tm=256, tn=3 * D_MODEL, b_at=REG_CONV,
                               out_dtypes=(F32,), after=tok)
            tok = comm.at("fwd", i, "proj_done", s["bcu"]) or []
            s["u"] = _conv_fwd(s["bcu"], w["conv_w"][j], tok)
        g0, b0 = w["ln_g"][i, 0][None, :], w["ln_b"][i, 0][None, :]
        g1, b1 = w["ln_g"][i, 1][None, :], w["ln_b"][i, 1][None, :]
        wa, wb = s["wa"], _ = comm.slab_weights(i, s["u"])
        s["a1"], s["x1"] = _matmul(s["u"], wa, name="mixer_out_ln", tm=256, tn=D_MODEL, b_at=REG_WOUT, epi=_ln_epi,
                                   epi_ins=(x, g0, b0), out_dtypes=(F32, F32))
        s["hh"] = _matmul(s["x1"], wa, name="mlp_up", tb=True, tm=256, tn=D_FF, b_at=REG_W1T)
        tok = comm.at("fwd", i, "mid", s["hh"])
        s["a2"], s["x2"] = _matmul(s["hh"], wa, name="mlp_down_ln", tm=256, tn=D_MODEL, b_at=REG_W2, a_fn=_relu_sq,
                                   epi=_ln_epi, epi_ins=(s["x1"], g1, b1), out_dtypes=(F32, F32), after=tok)
        s["pp"] = _matmul(p[i], wb, name="ple_proj", tb=True, tn=D_MODEL, b_at=REG_WPT)
        tok = comm.at("fwd", i, "end", s["pp"])
        x, s["gt"] = _matmul(s["x2"], wa, name="ple_gate", tn=1024, b_at=REG_WG,
                             epi=lambda acc, xr, pp: (xr + jax.nn.sigmoid(acc) * pp.astype(F32), acc),
                             epi_ins=(s["x2"], s["pp"]), out_dtypes=(F32, BF16), after=tok)
        saved.append(s)

    loss_part, dx = _loss_head(x, target)

    gw = {n: [None] * WEIGHTS[n][0][0] for n in SMALL + REPLICATED}
    ln_g_grads, ln_b_grads = [[None, None] for _ in range(DEPTH)], [[None, None] for _ in range(DEPTH)]
    resid = lambda acc, r: (acc + ALPHA * r,)
    plus = lambda acc, r: (acc + r,)
    for i in reversed(range(DEPTH)):
        j, kind = i // 3, i % 3
        s = saved[i]
        wa = s["wa"]
        ga = lax.empty((N_DEV, A_ROWS, D_MODEL), BF16)
        gb = lax.empty((N_DEV, REG_WPT[1], PLE_DIM), BF16)
        layer_grads = {}
        tok = comm.at("bwd", i, "begin", dx)

        def ple_bwd(dxo, gt, pp):
            sg = jax.nn.sigmoid(gt)
            return dxo * sg, dxo * pp * sg * (1.0 - sg)

        d_pp, d_gt = _tile_fwd(ple_bwd, (dx, s["gt"], s["pp"]), (), (BF16, BF16), tm=256, name="ple_bwd")
        gb = _matmul(d_pp, p[i], name="ple_proj_dw", ta=True, tm=REG_WPT[1], tn=PLE_DIM, out_at=REG_WPT, out_buf=gb, after=tok)
        ga = _matmul(s["x2"], d_gt, name="ple_gate_dw", ta=True, tm=REG_WG[1], tn=1024, out_at=REG_WG, out_buf=ga)
        g1, b1 = w["ln_g"][i, 1][None, :], w["ln_b"][i, 1][None, :]
        d_a2, ln_g_grads[i][1], ln_b_grads[i][1] = _matmul(
            d_gt, wa, name="ple_gate_dx_ln", tb=True, tm=256, tn=D_MODEL, b_at=REG_WG, epi=_ln_bwd_epi(1.0),
            epi_ins=(dx, s["a2"], g1, b1), out_dtypes=(F32,), n_row_sums=2, after=[ga, gb])
        tok = comm.at("bwd", i, "ln", d_a2)
        ga = _matmul(s["hh"], d_a2, name="mlp_down_dw", ta=True, tm=REG_W2[1], tn=1024, a_fn=_relu_sq, out_at=REG_W2,
                     out_buf=ga, after=tok)
        d_hh = _matmul(d_a2, wa, name="mlp_down_dx", tb=True, tm=256, tn=D_FF, b_at=REG_W2, after=ga,
                       epi=lambda acc, hh: (acc * 2.0 * jnp.maximum(hh.astype(F32), 0.0),), epi_ins=(s["hh"],))
        ga = _matmul(d_hh, s["x1"], name="mlp_up_dw", ta=True, tm=REG_W1T[1], tn=1024, out_at=REG_W1T, out_buf=ga)
        g0, b0 = w["ln_g"][i, 0][None, :], w["ln_b"][i, 0][None, :]
        d_a1, ln_g_grads[i][0], ln_b_grads[i][0] = _matmul(
            d_hh, wa, name="mlp_up_dx_ln", tm=256, tn=D_MODEL, b_at=REG_W1T, epi=_ln_bwd_epi(ALPHA),
            epi_ins=(d_a2, s["a1"], g0, b0), out_dtypes=(F32,), n_row_sums=2, after=ga)
        ga = _matmul(s["u"], d_a1, name="mixer_out_dw", ta=True, tm=REG_WOUT[1], tn=1024, out_at=REG_WOUT, out_buf=ga)
        du = _matmul(d_a1, wa, name="mixer_out_dx", tb=True, tn=1024, b_at=REG_WOUT, out_dtypes=(F32,), after=ga)
        comm.slab_grads(i, ga, gb)
        tok = comm.at("bwd", i, "slab_done", du) or []
        if kind == 0:
            dproj, dz, dg = _gla_bwd(s["proj"], s["z"], w["gla_norm_g"][j][None, :], s["states"], du, tok)
            tok = comm.at("bwd", i, "mixer_done", dproj)
            gw["gla_norm_g"][j] = dg[0]
            gw["gla_b_gate"][j] = _tile_bwd(lambda zz, b: (zz + b,), (s["z"],), (w["gla_b_gate"][j][None, :],), (dz,), (),
                                            tm=256, name="gla_bias_bwd", diff_tiled=[])[1][0][0]
            gw["gla_w_gate_up"][j] = _matmul(s["glr"], dz, name="gla_gate_dw", ta=True, out_dtypes=(F32,),
                                             after=tok)[:GLA_RANK]
            dglr = _matmul(dz, s["w_up"], name="gla_gate_dx", tb=True, out_dtypes=(F32,))
            dw_main = _matmul(dproj, s["x"], name="gla_proj_dw", ta=True, tn=1024, out_dtypes=(F32,))
            dw_lr = _matmul(dglr, s["x"], name="gla_lr_dw", ta=True, tn=1024, out_dtypes=(F32,))[:GLA_RANK]
            layer_grads["gla_w_in_t"] = jnp.concatenate([dw_main, dw_lr], axis=0)
            dx = _matmul(dproj, s["w_main"], name="gla_proj_dx", tn=1024, epi=resid, epi_ins=(d_a1,),
                         out_dtypes=(F32,), after=[dw_main, dw_lr, gw["gla_w_gate_up"][j]])
            dx = _matmul(dglr, s["w_lr"], name="gla_lr_dx", tn=1024, epi=plus, epi_ins=(dx,), out_dtypes=(F32,))
        elif kind == 1:
            dqn, dqr, dkv, dkr = _mla_attn_bwd(s["qn"], s["qr"], s["kv"], s["kr"], du, tok)
            tok = comm.at("bwd", i, "mixer_done", dqn)
            (d_cq,), (dgq, dgkv, dw_uq, dw_ukv) = _tile_bwd(_mla_pre, (s["cq"], cos, sin), s["pre_params"],
                                                           (dqn, dqr, dkv, dkr), (BF16,), tm=256, name="mla_pre_bwd",
                                                           diff_tiled=[0])
            gw["mla_q_norm"][j], gw["mla_kv_norm"][j] = dgq[0], dgkv[0]
            layer_grads["mla_w_uq"] = _uq_from_kernel_layout(dw_uq)
            layer_grads["mla_w_ukv"] = dw_ukv
            layer_grads["mla_w_in"] = _matmul(s["x"], d_cq, name="mla_proj_dw", ta=True, tn=MLA_IN_PAD,
                                              out_dtypes=(F32,), after=tok)[:, :MLA_IN]
            dx = _matmul(d_cq, s["w_in"], name="mla_proj_dx", tb=True, tn=1024, epi=resid, epi_ins=(d_a1,),
                         out_dtypes=(F32,), after=layer_grads["mla_w_in"])
        else:
            db, dc, du_, dcw = _conv_bwd(s["bcu"], w["conv_w"][j], du, tok)
            tok = comm.at("bwd", i, "mixer_done", db)
            gw["conv_w"][j] = dcw
            dbcu = jnp.concatenate([db, dc, du_], axis=1)
            layer_grads["conv"] = _matmul(dbcu, s["x"], name="conv_proj_dw", ta=True, tm=REG_CONV[1], tn=1024,
                                          out_at=REG_CONV, out_buf=lax.empty((N_DEV, REG_CONV[1], D_MODEL), BF16),
                                          after=tok)
            dx = _matmul(dbcu, s["wl"]["conv"], name="conv_proj_dx", tn=1024, b_at=REG_CONV, epi=resid, epi_ins=(d_a1,),
                         out_dtypes=(F32,), after=layer_grads["conv"])
        comm.mixer_grads(i, layer_grads)

    gw["ln_g"] = [jnp.concatenate([a, b], axis=0) for a, b in ln_g_grads]
    gw["ln_b"] = [jnp.concatenate([a, b], axis=0) for a, b in ln_b_grads]
    return loss_part, dx, {n: jnp.stack(gw[n]).astype(F32) for n in gw}


MESH_IDS = pl.DeviceIdType.MESH
ANY = pl.BlockSpec(memory_space=pl.ANY)
HBM_SPEC = pl.BlockSpec(memory_space=pltpu.HBM)
SEM_SPEC = pl.BlockSpec(memory_space=pltpu.SEMAPHORE)
DATAFLOW_EFFECT = pltpu.SideEffectType.DATAFLOW_SIDE_EFFECTING
CORE_COPIES, CHIP_COPIES = 4, 3


def _my_place():
    return lax.axis_index("x"), lax.axis_index("y"), lax.axis_index("c")


def _other_chips(mx, my):
    return [(1 - mx, my), (mx, 1 - my), (1 - mx, 1 - my)]


def _remote(src, dst, send_sems, recv_sems, k, to):
    return pltpu.make_async_remote_copy(src_ref=src, dst_ref=dst, send_sem=send_sems.at[k], recv_sem=recv_sems.at[k],
                                        device_id=to, device_id_type=MESH_IDS)


def _gather_first_copies(n_arr):
    def make(bufs, send_sems, recv_sems):
        mx, my, mc = _my_place()
        mine = 4 * mx + 2 * my + mc
        peers = [(mx, my, 1 - mc)] + [(cx, cy, mc) for cx, cy in _other_chips(mx, my)]
        return [_remote(bufs[a].at[mine], bufs[a].at[mine], send_sems, recv_sems, (1 + CHIP_COPIES) * a + k, to)
                for a in range(n_arr) for k, to in enumerate(peers)]
    return make, (1 + CHIP_COPIES) * n_arr


def _gather_forward_copies(n_arr):
    def make(bufs, send_sems, recv_sems):
        mx, my, mc = _my_place()
        blocks = [4 * cx + 2 * cy + mc for cx, cy in _other_chips(mx, my)]
        return [_remote(bufs[a].at[blk], bufs[a].at[blk], send_sems, recv_sems, CHIP_COPIES * a + k, (mx, my, 1 - mc))
                for a in range(n_arr) for k, blk in enumerate(blocks)]
    return make, CHIP_COPIES * n_arr


def _scatter_core_copies(n_arr):
    def make(bufs, send_sems, recv_sems):
        mx, my, mc = _my_place()
        return [_remote(bufs[a].at[2 * k + (1 - mc)], bufs[n_arr + a].at[k], send_sems, recv_sems, CORE_COPIES * a + k,
                        (mx, my, 1 - mc)) for a in range(n_arr) for k in range(CORE_COPIES)]
    return make, CORE_COPIES * n_arr


def _scatter_chip_copies(n_arr):
    def make(bufs, send_sems, recv_sems):
        mx, my, mc = _my_place()
        return [_remote(bufs[a].at[2 * cx + cy], bufs[n_arr + a].at[k], send_sems, recv_sems, CHIP_COPIES * a + k,
                        (cx, cy, mc)) for a in range(n_arr) for k, (cx, cy) in enumerate(_other_chips(mx, my))]
    return make, CHIP_COPIES * n_arr


def _exchange(name, bufs, copies):
    make, n_copies = copies
    n = len(bufs)

    def body(*refs):
        descs = make(refs[:n], refs[2 * n], refs[2 * n + 1])
        for cp in descs:
            cp.start()
        for cp in descs:
            cp.wait()

    return pl.pallas_call(
        body, name=name, out_shape=[jax.ShapeDtypeStruct(b.shape, b.dtype) for b in bufs], in_specs=[ANY] * n,
        out_specs=[ANY] * n, input_output_aliases={i: i for i in range(n)},
        scratch_shapes=[pltpu.SemaphoreType.DMA((n_copies,)), pltpu.SemaphoreType.DMA((n_copies,))],
    )(*bufs)


def _exchange_start(name, bufs, copies, after):
    make, n_copies = copies
    n = len(bufs)

    def body(*refs):
        for cp in make(refs[:n], refs[n + 1], refs[n + 2]):
            cp.start()
        refs[-1][...] = jnp.zeros_like(refs[-1])

    outs = pl.pallas_call(
        body, name=name,
        out_shape=(pltpu.SemaphoreType.DMA((n_copies,)), pltpu.SemaphoreType.DMA((n_copies,)),
                   *[pltpu.HBM(b.shape, b.dtype) for b in bufs], jax.ShapeDtypeStruct((8, LANES), F32)),
        in_specs=[HBM_SPEC] * n + [ANY],
        out_specs=(SEM_SPEC, SEM_SPEC, *[HBM_SPEC] * n, pl.BlockSpec(memory_space=pltpu.VMEM)),
        input_output_aliases={i: 2 + i for i in range(n)},
        compiler_params=pltpu.CompilerParams(has_side_effects=DATAFLOW_EFFECT),
    )(*[pltpu.with_memory_space_constraint(b, pltpu.HBM) for b in bufs], after)
    return (outs[0], outs[1]), list(outs[2:2 + n]), outs[-1]


def _exchange_wait(name, sems, bufs, copies, after):
    make, _ = copies
    n = len(bufs)

    def body(*refs):
        for cp in make(refs[:n], refs[n], refs[n + 1]):
            cp.wait_send()
            cp.wait_recv()

    return list(pl.pallas_call(
        body, name=name, out_shape=[pltpu.HBM(b.shape, b.dtype) for b in bufs],
        in_specs=[HBM_SPEC] * n + [SEM_SPEC, SEM_SPEC, ANY], out_specs=[HBM_SPEC] * n,
        input_output_aliases={i: i for i in range(n)},
        compiler_params=pltpu.CompilerParams(has_side_effects=DATAFLOW_EFFECT),
    )(*bufs, *sems, after))


SUM_TILE_BYTES = 2 * 1024 * 1024


def _row_tile(r, c):
    best = None
    for cand in range(16, r + 1, 16):
        if r % cand == 0 and cand * c * 2 <= SUM_TILE_BYTES:
            best = cand
    return r if best is None else best


def _pair_sum(g, recv, my_c):
    _, r, c = g.shape
    tr = _row_tile(r, c)

    def body(c_ref, g_ref, r_ref, o_ref):
        o_ref[...] = (g_ref[...].astype(F32) + r_ref[...].astype(F32)).astype(o_ref.dtype)

    return pl.pallas_call(
        body, name="rs_pair_sum", out_shape=jax.ShapeDtypeStruct((4, r, c), g.dtype),
        grid_spec=pltpu.PrefetchScalarGridSpec(
            num_scalar_prefetch=1, grid=(4, r // tr),
            in_specs=[pl.BlockSpec((1, tr, c), lambda n, i, cr: (2 * n + cr[0], i, 0)),
                      pl.BlockSpec((1, tr, c), lambda n, i, cr: (n, i, 0))],
            out_specs=pl.BlockSpec((1, tr, c), lambda n, i, cr: (n, i, 0))),
        compiler_params=_params(),
    )(my_c, g, recv)


def _chip_sum(h, recv, my_chip):
    _, r, c = h.shape
    tr = _row_tile(r, c)

    def body(j_ref, h_ref, r0_ref, r1_ref, r2_ref, o_ref):
        o_ref[...] = ((h_ref[0].astype(F32) + r0_ref[0].astype(F32)) + r1_ref[0].astype(F32)) + r2_ref[0].astype(F32)

    return pl.pallas_call(
        body, name="rs_chip_sum", out_shape=jax.ShapeDtypeStruct((r, c), F32),
        grid_spec=pltpu.PrefetchScalarGridSpec(
            num_scalar_prefetch=1, grid=(r // tr,),
            in_specs=[pl.BlockSpec((1, tr, c), lambda i, jr: (jr[0], i, 0))]
            + [pl.BlockSpec((1, tr, c), lambda i, jr, n=n: (n, i, 0)) for n in range(3)],
            out_specs=pl.BlockSpec((tr, c), lambda i, jr: (i, 0))),
        compiler_params=_params(),
    )(my_chip, h, recv, recv, recv)


def _sum_blocks(g):
    n, r, c = g.shape

    def body(g_ref, o_ref):
        acc = g_ref[0]
        for k in range(1, n):
            acc = acc + g_ref[k]
        o_ref[...] = acc

    return pl.pallas_call(body, name="sum_blocks", out_shape=jax.ShapeDtypeStruct((r, c), F32), compiler_params=_params())(g)


def _pack(flat_parts, cols, row_multiple, dtype):
    flat = jnp.concatenate([f.astype(dtype) for f in flat_parts])
    per_row_block = cols * row_multiple
    padded = -(-flat.shape[0] // per_row_block) * per_row_block
    return jnp.pad(flat, (0, padded - flat.shape[0])).reshape(padded // cols, cols)


def _shard_shape(name):
    shape, axis = WEIGHTS[name]
    if axis is None:
        return shape
    return tuple(s // N_DEV if a == axis else s for a, s in enumerate(shape))


def _size(shape):
    n = 1
    for s in shape:
        n *= s
    return n


def _unshard(blocks, name):
    _, axis = WEIGHTS[name]
    return jnp.concatenate([blocks[k] for k in range(N_DEV)], axis=axis)


def _unpack_blocks(flat, names, lead):
    out, off = {}, 0
    for n in names:
        shp = _shard_shape(n)[1:] if lead else _shard_shape(n)
        out[n] = flat[..., off:off + _size(shp)].reshape(flat.shape[:-1] + shp)
        off += _size(shp)
    return out


def _layer_slabs(shard, i):
    j, kind = i // 3, i % 3
    w_out = (shard["gla_w_out"], shard["mla_w_out"], shard["conv_w_out"])[kind][j]
    out = {"a": jnp.concatenate([shard["mlp_w2"][i], shard["mlp_w1"][i].T, w_out, shard["ple_w_gate"][i]], axis=0).astype(BF16),
           "b": shard["ple_w_proj"][i].T.astype(BF16)}
    if kind == 0:
        out["gla"] = shard["gla_w_in"][j].T.astype(BF16)
    elif kind == 1:
        out["mla"] = _pack([shard[n][j].reshape(-1) for n in MLA_PACKED], PACK_COLS, PACK_ROW_TILE, BF16)
    else:
        out["conv"] = shard["conv_w_in"][j].T.astype(BF16)
    return out


def _mixer_weights(landed, i):
    kind = i % 3
    if kind == 0:
        return {"gla_w_in_t": landed["gla"].reshape(-1, D_MODEL)}
    if kind == 2:
        return {"conv": landed["conv"]}
    blocks = _unpack_blocks(landed["mla"].reshape(N_DEV, -1), MLA_PACKED, lead=True)
    return {n: jnp.concatenate([blocks[n][k] for k in range(N_DEV)], axis=WEIGHTS[n][1] - 1) for n in MLA_PACKED}


def _mixer_grad_buffers(layer_grads, i):
    kind = i % 3
    if kind == 0:
        return {"gla": layer_grads["gla_w_in_t"].reshape(N_DEV, -1, D_MODEL).astype(BF16)}
    if kind == 2:
        return {"conv": layer_grads["conv"]}
    parts = [jnp.stack(jnp.split(layer_grads[n], N_DEV, axis=WEIGHTS[n][1] - 1)).reshape(N_DEV, -1) for n in MLA_PACKED]
    cat = jnp.concatenate(parts, axis=1).astype(BF16)
    per = PACK_COLS * PACK_ROW_TILE
    padded = -(-cat.shape[1] // per) * per
    return {"mla": jnp.pad(cat, ((0, 0), (0, padded - cat.shape[1]))).reshape(N_DEV, padded // PACK_COLS, PACK_COLS)}


SLAB_KEYS = ("a", "b")


class _Overlap:
    def __init__(self, shard, small_pack):
        mx, my, mc = _my_place()
        self.my_c = mc.astype(jnp.int32).reshape(1)
        self.my_chip = (2 * mx + my).astype(jnp.int32).reshape(1)
        mine = 4 * mx + 2 * my + mc
        slabs = [_layer_slabs(shard, i) for i in range(DEPTH)]
        slabs[0]["small"] = small_pack
        self.landing = [{k: lax.dynamic_update_index_in_dim(lax.empty((N_DEV, *v.shape), v.dtype), v, mine, 0)
                         for k, v in slabs[i].items()} for i in range(DEPTH)]
        self.fly = {}
        self.grads = [{} for _ in range(DEPTH)]
        self.reduced = [{} for _ in range(DEPTH)]
        tok = self._gather_first(0, "mixer", shard["ln_g"])
        tok = self._gather_first(0, "slab", tok)
        bufs = self._wait("ag_first_mixer_l0", tok)
        self.landing[0].update(zip(self._keys(self.landing[0], "mixer"),
                                   _exchange("ag_forward_mixer_l0", bufs, _gather_forward_copies(len(bufs)))))

    @staticmethod
    def _keys(names, group):
        return [k for k in names if (k in SLAB_KEYS) == (group == "slab")]

    def _start(self, name, bufs, copies, after):
        sems, bufs, tok = _exchange_start(name + "_start", bufs, copies, after)
        self.fly[name] = (sems, bufs, copies)
        return tok

    def _wait(self, name, after):
        sems, bufs, copies = self.fly.pop(name)
        return _exchange_wait(name + "_wait", sems, bufs, copies, after)

    def mixer_weights(self, i):
        return _mixer_weights(self.landing[i], i)

    def slab_weights(self, i, dep):
        self._gather_done(i, "slab", dep)
        return self.landing[i]["a"], self.landing[i]["b"]

    def slab_grads(self, i, ga, gb):
        self.grads[i].update(a=ga, b=gb)

    def mixer_grads(self, i, layer_grads):
        self.grads[i].update(_mixer_grad_buffers(layer_grads, i))

    def at(self, phase, i, point, dep):
        toks = []
        if phase == "fwd":
            if point == "begin" and i == 0:
                toks.append(self._gather_first(1, "mixer", self.landing[0][self._keys(self.landing[0], "mixer")[0]]))
                toks.append(self._gather_first(1, "slab", toks[-1]))
            if point == "proj_done":
                toks.append(self._gather_forward(i, "slab", dep))
            if point == "mid" and i + 1 < DEPTH:
                toks.append(self._gather_forward(i + 1, "mixer", dep))
                if i + 2 < DEPTH:
                    toks.append(self._gather_first(i + 2, "mixer", dep))
                    toks.append(self._gather_first(i + 2, "slab", toks[-1]))
            if point == "end" and i + 1 < DEPTH:
                self._gather_done(i + 1, "mixer", dep)
        else:
            if point == "begin" and i + 1 < DEPTH:
                toks.append(self._scatter_cores(i + 1, "mixer", dep))
            if point == "ln" and i + 1 < DEPTH:
                toks.append(self._scatter_chips(i + 1, "mixer", dep))
            if point == "slab_done":
                if i + 1 < DEPTH:
                    self._scatter_done(i + 1, "slab", dep)
                    self._scatter_done(i + 1, "mixer", dep)
                toks.append(self._scatter_cores(i, "slab", dep))
            if point == "mixer_done":
                toks.append(self._scatter_chips(i, "slab", dep))
        return toks or None

    def _gather_first(self, i, group, after):
        bufs = [self.landing[i][k] for k in self._keys(self.landing[i], group)]
        return self._start(f"ag_first_{group}_l{i}", bufs, _gather_first_copies(len(bufs)), after)

    def _gather_forward(self, i, group, after):
        bufs = self._wait(f"ag_first_{group}_l{i}", after)
        return self._start(f"ag_forward_{group}_l{i}", bufs, _gather_forward_copies(len(bufs)), after)

    def _gather_done(self, i, group, after):
        keys = self._keys(self.landing[i], group)
        self.landing[i].update(zip(keys, self._wait(f"ag_forward_{group}_l{i}", after)))

    def _scatter_cores(self, i, group, after):
        gs = [self.grads[i][k] for k in self._keys(self.grads[i], group)]
        land = [lax.empty((4, *g.shape[1:]), g.dtype) for g in gs]
        return self._start(f"rs_cores_{group}_l{i}", gs + land, _scatter_core_copies(len(gs)), after)

    def _pair_sums(self, bufs):
        n = len(bufs) // 2
        hs = [_pair_sum(g, r, self.my_c) for g, r in zip(bufs[:n], bufs[n:])]
        return hs + [lax.empty((3, *h.shape[1:]), h.dtype) for h in hs]

    def _scatter_chips(self, i, group, after):
        bufs = self._pair_sums(self._wait(f"rs_cores_{group}_l{i}", after))
        return self._start(f"rs_chips_{group}_l{i}", bufs, _scatter_chip_copies(len(bufs) // 2), after)

    def _chip_sums(self, i, group, bufs):
        n = len(bufs) // 2
        for k, h, r in zip(self._keys(self.grads[i], group), bufs[:n], bufs[n:]):
            self.reduced[i][k] = _chip_sum(h, r, self.my_chip)

    def _scatter_done(self, i, group, after):
        self._chip_sums(i, group, self._wait(f"rs_chips_{group}_l{i}", after))

    def tail_begin(self, dep):
        return self._scatter_cores(0, "mixer", dep)

    def tail_middle(self, dep):
        self._scatter_done(0, "slab", dep)
        return self._scatter_chips(0, "mixer", dep)

    def tail_end(self, dep):
        self._scatter_done(0, "mixer", dep)


def _small_gather_start(x, name, after):
    mx, my, mc = _my_place()
    land = lax.dynamic_update_index_in_dim(lax.empty((N_DEV, *x.shape), x.dtype), x, 4 * mx + 2 * my + mc, 0)
    return name, _exchange_start(name + "_first_start", [land], _gather_first_copies(1), after)


def _small_gather_finish(started, after):
    name, (sems, bufs, _) = started
    bufs = _exchange_wait(name + "_first_wait", sems, bufs, _gather_first_copies(1), after)
    return _exchange(name + "_forward", bufs, _gather_forward_copies(1))[0]


def _adamw_math(w, g, m, v):
    m2 = ADAM_B1 * m + (1.0 - ADAM_B1) * g
    v2 = ADAM_B2 * v + (1.0 - ADAM_B2) * (g * g)
    m_hat = m2 / (1.0 - ADAM_B1 ** ADAM_STEP)
    v_hat = v2 / (1.0 - ADAM_B2 ** ADAM_STEP)
    return -ADAM_LR * (m_hat / (jnp.sqrt(v_hat) + ADAM_EPS) + ADAM_WD * w), m2, v2


ADAMW_TILE_BYTES = 1024 * 1024


def _adamw_layer(name, w, m, v, j, g, g_at, transposed, chain, after):
    n_layers, r, c = w.shape
    tr = max(t for t in range(8, r + 1, 8) if r % t == 0 and (t * c * 4 <= ADAMW_TILE_BYTES or t == 8))
    rb, rows = g_at
    if transposed:
        assert rows == c and g.shape[1] == r, (name, g.shape, g_at)
        g_spec = pl.BlockSpec((rows, tr), lambda i: (rb, i))
    else:
        assert rows == r and g.shape[1] == c, (name, g.shape, g_at)
        g_spec = pl.BlockSpec((tr, c), lambda i: (rb * (r // tr) + i, 0))
    extra = list(chain or []) + [a for a in (after or []) if a is not None]
    n_chain = 4 if chain else 0

    def body(w_ref, m_ref, v_ref, g_ref, *rest):
        g_out, d_out, m_out, v_out, tok_ref = rest[len(extra):]
        gv = g_ref[...].T if transposed else g_ref[...]
        g_out[0] = gv
        d_out[0], m_out[0], v_out[0] = _adamw_math(w_ref[0], gv, m_ref[0], v_ref[0])
        tok_ref[...] = jnp.zeros_like(tok_ref)

    layer_spec = pl.BlockSpec((1, tr, c), lambda i: (j, i, 0))
    outs = pl.pallas_call(
        body, name=f"adamw_{name}_l{j}", grid=(r // tr,),
        in_specs=[layer_spec] * 3 + [g_spec] + [pl.BlockSpec(memory_space=pl.ANY)] * len(extra),
        out_specs=[layer_spec] * 4 + [pl.BlockSpec((8, LANES), lambda i: (0, 0))],
        out_shape=[jax.ShapeDtypeStruct(w.shape, F32)] * 4 + [jax.ShapeDtypeStruct((8, LANES), F32)],
        input_output_aliases={4 + k: k for k in range(n_chain)}, compiler_params=_params(),
    )(w, m, v, g, *extra)
    return list(outs[:4]), outs[4]


def _adamw(w, g, m, v, name):
    shape = w.shape
    cols = shape[-1]
    rows = _size(shape) // cols
    tr = rows
    for cand in (512, 256, 128, 64, 32, 16, 8):
        if rows > cand and rows % cand == 0:
            tr = cand
            break

    def body(w_ref, g_ref, m_ref, v_ref, d_ref, mo_ref, vo_ref):
        d_ref[...], mo_ref[...], vo_ref[...] = _adamw_math(w_ref[...], g_ref[...], m_ref[...], v_ref[...])

    spec = pl.BlockSpec((tr, cols), lambda i: (i, 0))
    outs = pl.pallas_call(
        body, name="adamw_" + name, grid=(rows // tr,), in_specs=[spec] * 4, out_specs=[spec] * 3,
        out_shape=[jax.ShapeDtypeStruct((rows, cols), F32)] * 3, compiler_params=_params(),
    )(*[a.reshape(rows, cols) for a in (w, g, m, v)])
    return [o.reshape(shape) for o in outs]


def kernel(x, p, positions, gla_w_in, gla_w_gate_up, gla_b_gate, gla_norm_g, gla_w_out, mla_w_in, mla_q_norm, mla_kv_norm, mla_w_uq, mla_w_ukv, mla_w_out, conv_w_in, conv_w, conv_w_out, ln_g, ln_b, mlp_w1, mlp_w2, ple_w_gate, ple_w_proj, loss_target, m_gla_w_in, m_gla_w_gate_up, m_gla_b_gate, m_gla_norm_g, m_gla_w_out, m_mla_w_in, m_mla_q_norm, m_mla_kv_norm, m_mla_w_uq, m_mla_w_ukv, m_mla_w_out, m_conv_w_in, m_conv_w, m_conv_w_out, m_ln_g, m_ln_b, m_mlp_w1, m_mlp_w2, m_ple_w_gate, m_ple_w_proj, v_gla_w_in, v_gla_w_gate_up, v_gla_b_gate, v_gla_norm_g, v_gla_w_out, v_mla_w_in, v_mla_q_norm, v_mla_kv_norm, v_mla_w_uq, v_mla_w_ukv, v_mla_w_out, v_conv_w_in, v_conv_w, v_conv_w_out, v_ln_g, v_ln_b, v_mlp_w1, v_mlp_w2, v_ple_w_gate, v_ple_w_proj):
    args = locals()
    shard = {n: args[n] for n in WEIGHT_NAMES}
    mom = {n: args["m_" + n] for n in WEIGHT_NAMES}
    var = {n: args["v_" + n] for n in WEIGHT_NAMES}
    mx, my, mc = _my_place()

    comm = _Overlap(shard, _pack([shard[n].reshape(-1) for n in SMALL], LANES, 8, F32))
    small_all = comm.landing[0]["small"]
    small = {n: shard[n] for n in REPLICATED}
    small.update({n: _unshard(blk, n) for n, blk in _unpack_blocks(small_all.reshape(N_DEV, -1), SMALL, lead=False).items()})
    loss_part, grad_x, small_grads = _step(x[0], p[:, 0], positions[0], loss_target[0], small, comm)

    chains = {}

    def update(name, j, g, g_at, transposed, tok):
        chains[name], tok = _adamw_layer(name, shard[name], mom[name], var[name], j, g, g_at, transposed,
                                         chains.get(name), [tok])
        return tok

    def update_layer(i, groups, tok):
        j, kind = i // 3, i % 3
        red = comm.reduced[i]
        if "slab" in groups:
            tok = update("mlp_w2", i, red["a"], REG_W2, False, tok)
            tok = update("mlp_w1", i, red["a"], REG_W1T, True, tok)
            tok = update(("gla_w_out", "mla_w_out", "conv_w_out")[kind], j, red["a"], REG_WOUT, False, tok)
            tok = update("ple_w_gate", i, red["a"], REG_WG, False, tok)
            tok = update("ple_w_proj", i, red["b"], REG_WPT, True, tok)
        if "mixer" in groups:
            if kind == 0:
                tok = update("gla_w_in", j, red["gla"].T, (0, D_MODEL), False, tok)
            elif kind == 2:
                tok = update("conv_w_in", j, red["conv"], REG_CONV, True, tok)
            else:
                for n, g in _unpack_blocks(red["mla"].reshape(-1), MLA_PACKED, lead=True).items():
                    tok = update(n, j, g, (0, g.shape[0]), False, tok)
        return tok

    tok = comm.tail_begin(grad_x)
    tok = update_layer(3, ("slab", "mixer"), tok)
    tok = update_layer(2, ("slab", "mixer"), tok)
    tok = comm.tail_middle(tok)
    small_parts = [loss_part[0, :1]] + [small_grads[n].reshape(-1) for n in SMALL + REPLICATED]
    small_gather = _small_gather_start(_pack(small_parts, LANES, 8, F32), "ag_small_grads", tok)
    tok = update_layer(1, ("slab", "mixer"), small_gather[1][2])
    tok = update_layer(0, ("slab",), tok)
    comm.tail_end(tok)
    tok = update_layer(0, ("mixer",), tok)
    red_small = _sum_blocks(_small_gather_finish(small_gather, tok)).reshape(-1)
    loss = red_small[0]
    off = 1
    dev = 4 * mx + 2 * my + mc
    for n in SMALL + REPLICATED:
        shape, axis = WEIGHTS[n]
        full_g = red_small[off:off + _size(shape)].reshape(shape)
        off += _size(shape)
        if axis is not None:
            width = shape[axis] // N_DEV
            full_g = lax.dynamic_slice_in_dim(full_g, dev * width, width, axis=axis)
        chains[n] = [full_g, *_adamw(shard[n], full_g, mom[n], var[n], n)]
    return (loss, grad_x[None], *[chains[n][k] for k in range(4) for n in WEIGHT_NAMES])
```

```python
import functools

import jax
import jax.numpy as jnp
from jax import lax
from jax.experimental import pallas as pl
from jax.experimental.pallas import tpu as pltpu

F32, BF16 = jnp.float32, jnp.bfloat16
HIGHEST = lax.Precision.HIGHEST
MESH_AXES = ("x", "y", "c")
N_DEV = 8

D_MODEL = 1024
SEQ = 2048
DEPTH = 4
CHUNK = 64
ALPHA = (2 * DEPTH) ** 0.25
LN_EPS = 1e-5
RMS_EPS = 1e-6
PLE_DIM = 256
D_FF = 4 * D_MODEL
GLA_HEADS = 4
GLA_DK = 128
GLA_DV = 256
GLA_RANK = 16
GLA_TAU = 16.0
GLA_HK = GLA_HEADS * GLA_DK
GLA_HV = GLA_HEADS * GLA_DV
GLA_MAIN = 2 * GLA_HK + GLA_HV + D_MODEL
MLA_HEADS = 8
MLA_NOPE = 128
MLA_ROPE = 64
MLA_V = 128
MLA_RANK = 256
MLA_IN = 2 * MLA_RANK + MLA_ROPE
MLA_IN_PAD = 640
ROPE_BASE = 10000.0
LANES = 128
ADAM_LR, ADAM_B1, ADAM_B2, ADAM_EPS, ADAM_WD, ADAM_STEP = 0.001, 0.9, 0.999, 1e-08, 0.01, 10

V7X_VMEM_LIMIT_BYTES = 56 * 1024 * 1024
PACK_COLS = 1024
PACK_ROW_TILE = 256

WEIGHTS = {
    "gla_w_in": ((2, 1024, 3088), 2), "gla_w_gate_up": ((2, 16, 512), 2), "gla_b_gate": ((2, 512), 1),
    "gla_norm_g": ((2, 256), 1), "gla_w_out": ((2, 1024, 1024), 1), "mla_w_in": ((1, 1024, 576), 1),
    "mla_q_norm": ((1, 256), None), "mla_kv_norm": ((1, 256), None), "mla_w_uq": ((1, 256, 1536), 2),
    "mla_w_ukv": ((1, 256, 2048), 2), "mla_w_out": ((1, 1024, 1024), 1), "conv_w_in": ((1, 1024, 3072), 2),
    "conv_w": ((1, 3, 1024), 2), "conv_w_out": ((1, 1024, 1024), 1), "ln_g": ((4, 2, 1024), 2),
    "ln_b": ((4, 2, 1024), 2), "mlp_w1": ((4, 1024, 4096), 2), "mlp_w2": ((4, 4096, 1024), 1),
    "ple_w_gate": ((4, 1024, 1024), 1), "ple_w_proj": ((4, 256, 1024), 2),
}
WEIGHT_NAMES = list(WEIGHTS)
REG_W2, REG_W1T, REG_WOUT, REG_WG = (0, 512), (1, 512), (8, 128), (9, 128)
A_ROWS = 1280
REG_CONV = (0, 384)
REG_WPT = (0, 128)
REG_MLA_IN = (0, 128)
MLA_HEAD_PAD = 2 * LANES
SMALL = ["gla_w_gate_up", "gla_b_gate", "gla_norm_g", "conv_w", "ln_g", "ln_b"]
REPLICATED = ["mla_q_norm", "mla_kv_norm"]


def _params(**kw):
    return pltpu.CompilerParams(vmem_limit_bytes=V7X_VMEM_LIMIT_BYTES, **kw)


def _dot(a, b, ca, cb, precision=None):
    return lax.dot_general(a, b, (((ca,), (cb,)), ((), ())), precision=precision, preferred_element_type=F32)


def _nn(a, b):
    return _dot(a.astype(BF16), b.astype(BF16), 1, 0)


def _nt(a, b):
    return _dot(a.astype(BF16), b.astype(BF16), 1, 1)


def _tn(a, b):
    return _dot(a.astype(BF16), b.astype(BF16), 0, 0)


@jax.custom_vjp
def mm_nn(a, b):
    return _nn(a, b)


def _mm_nn_fwd(a, b):
    return _nn(a, b), (a, b)


def _mm_nn_bwd(res, g):
    a, b = res
    return _nt(g, b).astype(a.dtype), _tn(a, g).astype(b.dtype)


mm_nn.defvjp(_mm_nn_fwd, _mm_nn_bwd)


@jax.custom_vjp
def mm_nt(a, b):
    return _nt(a, b)


def _mm_nt_fwd(a, b):
    return _nt(a, b), (a, b)


def _mm_nt_bwd(res, g):
    a, b = res
    return _nn(g, b).astype(a.dtype), _tn(g, a).astype(b.dtype)


mm_nt.defvjp(_mm_nt_fwd, _mm_nt_bwd)


@jax.custom_vjp
def mm_tn(a, b):
    return _tn(a, b)


def _mm_tn_fwd(a, b):
    return _tn(a, b), (a, b)


def _mm_tn_bwd(res, g):
    a, b = res
    return _nt(b, g).astype(a.dtype), _nn(a, g).astype(b.dtype)


mm_tn.defvjp(_mm_tn_fwd, _mm_tn_bwd)


def _iota2(shape, dim):
    return lax.broadcasted_iota(jnp.int32, shape, dim)


@jax.custom_vjp
def cumsum_rows(x):
    n = x.shape[0]
    tri = (_iota2((n, n), 0) >= _iota2((n, n), 1)).astype(F32)
    return _dot(tri, x, 1, 0, precision=HIGHEST)


def _cumsum_fwd(x):
    return cumsum_rows(x), None


def _cumsum_bwd(_, g):
    n = g.shape[0]
    tri_t = (_iota2((n, n), 0) <= _iota2((n, n), 1)).astype(F32)
    return (_dot(tri_t, g, 1, 0, precision=HIGHEST),)


cumsum_rows.defvjp(_cumsum_fwd, _cumsum_bwd)


def _rot_matrix(transposed):
    i, j = _iota2((LANES, LANES), 0), _iota2((LANES, LANES), 1)
    if transposed:
        i, j = j, i
    half = MLA_ROPE // 2
    plus = (i == j - half) & (j >= half) & (j < MLA_ROPE)
    minus = (i == j + half) & (j < half)
    return plus.astype(F32) - minus.astype(F32)


@jax.custom_vjp
def rot_half(x):
    return _dot(x, _rot_matrix(False), 1, 0, precision=HIGHEST)


def _rot_fwd(x):
    return rot_half(x), None


def _rot_bwd(_, g):
    return (_dot(g, _rot_matrix(True), 1, 0, precision=HIGHEST),)


rot_half.defvjp(_rot_fwd, _rot_bwd)


def _shift_rows_raw(x, s):
    n = x.shape[0]
    row = _iota2(x.shape, 0)
    rolled = pltpu.roll(x, s % n, 0)
    keep = (row >= s) if s > 0 else (row < n + s)
    return jnp.where(keep, rolled, 0.0)


@functools.partial(jax.custom_vjp, nondiff_argnums=(1,))
def shift_rows(x, s):
    return _shift_rows_raw(x, s)


def _shift_fwd(x, s):
    return _shift_rows_raw(x, s), None


def _shift_bwd(s, _, g):
    return (_shift_rows_raw(g, -s),)


shift_rows.defvjp(_shift_fwd, _shift_bwd)


def _layer_norm(a, g, b):
    mu = jnp.mean(a, -1, keepdims=True)
    xc = a - mu
    var = jnp.mean(xc * xc, -1, keepdims=True)
    return xc * lax.rsqrt(var + LN_EPS) * g + b


def _rms_norm(a, g):
    return a * lax.rsqrt(jnp.mean(a * a, -1, keepdims=True) + RMS_EPS) * g


def _log_sigmoid(z):
    return jnp.minimum(z, 0.0) - jnp.log(1.0 + jnp.exp(-jnp.abs(z)))


def _matmul(a, b, *, name, ta=False, tb=False, tm=512, tn=512, a_fn=None, epi=None, epi_ins=(), out_dtypes=(BF16,),
            b_at=None, out_at=None, out_buf=None, after=None, n_row_sums=0):
    m = a.shape[1] if ta else a.shape[0]
    k = a.shape[0] if ta else a.shape[1]
    if b_at is None:
        n, kb = (b.shape[0], b.shape[1]) if tb else (b.shape[1], b.shape[0])
    else:
        rb, r = b_at
        n, kb = (N_DEV * r, b.shape[2]) if tb else (b.shape[2], N_DEV * r)
    assert kb == k, (name, a.shape, b.shape, k, kb)
    tm, tn = min(tm, m), min(tn, n)
    assert m % tm == 0 and n % tn == 0, (name, m, n, tm, tn)
    a_spec = pl.BlockSpec((k, tm), lambda i, j: (0, i)) if ta else pl.BlockSpec((tm, k), lambda i, j: (i, 0))
    if b_at is None:
        b_spec = pl.BlockSpec((tn, k), lambda i, j: (j, 0)) if tb else pl.BlockSpec((k, tn), lambda i, j: (0, j))
        load_b = lambda ref: ref[...]
    elif tb and tn == n:
        b_spec = pl.BlockSpec((N_DEV, r, k), lambda i, j: (0, rb, 0))
        load_b = lambda ref: ref[...].reshape(n, k)
    elif tb:
        assert tn == r, (name, tn, r)
        b_spec = pl.BlockSpec((1, r, k), lambda i, j: (j, rb, 0))
        load_b = lambda ref: ref[0]
    else:
        b_spec = pl.BlockSpec((N_DEV, r, tn), lambda i, j: (0, rb, j))
        load_b = lambda ref: ref[...].reshape(k, tn)
    e_specs = []
    for e in epi_ins:
        if e.shape == (1, n):
            e_specs.append(pl.BlockSpec((1, tn), lambda i, j: (0, j)))
        else:
            assert e.shape == (m, n), (name, e.shape, m, n)
            e_specs.append(pl.BlockSpec((tm, tn), lambda i, j: (i, j)))
    n_epi = len(epi_ins)
    ca, cb = (0 if ta else 1), (1 if tb else 0)
    operands = [a, b, *epi_ins]
    in_specs = [a_spec, b_spec, *e_specs]
    if out_at is None:
        assert n_row_sums == 0 or tn == n, (name, tn, n)
        out_specs = [pl.BlockSpec((tm, tn), lambda i, j: (i, j)) for _ in out_dtypes]
        out_specs += [pl.BlockSpec((1, n), lambda i, j: (0, 0))] * n_row_sums
        out_shape = [jax.ShapeDtypeStruct((m, n), dt) for dt in out_dtypes]
        out_shape += [jax.ShapeDtypeStruct((1, n), F32)] * n_row_sums
        aliases, n_buf = {}, 0
    else:
        orb, orows = out_at
        assert len(out_dtypes) == 1 and orows % tm == 0 and m == N_DEV * orows and n == out_buf.shape[2], (name, m, n)
        per = orows // tm
        out_specs = [pl.BlockSpec((1, tm, tn), lambda i, j: (i // per, orb * per + i % per, j))]
        out_shape = [jax.ShapeDtypeStruct(out_buf.shape, out_buf.dtype)]
        operands.append(out_buf)
        in_specs.append(pl.BlockSpec(memory_space=pl.ANY))
        aliases, n_buf = {len(operands) - 1: 0}, 1
    for dep in ([] if after is None else after if isinstance(after, (list, tuple)) else [after]):
        if dep is not None:
            operands.append(dep)
            in_specs.append(pl.BlockSpec(memory_space=pl.ANY))
            n_buf += 1

    def body(a_ref, b_ref, *rest):
        av = a_ref[...]
        if a_fn is not None:
            av = a_fn(av)
        acc = _dot(av.astype(BF16), load_b(b_ref).astype(BF16), ca, cb)
        outs = epi(acc, *[r_[...] for r_ in rest[:n_epi]]) if epi is not None else (acc,)
        o_refs = rest[n_epi + n_buf:]
        n_tiles = len(o_refs) - n_row_sums
        for o_ref, val in zip(o_refs[:n_tiles], outs):
            o_ref[...] = val.astype(o_ref.dtype).reshape(o_ref.shape)
        if n_row_sums:
            @pl.when(pl.program_id(0) == 0)
            def _():
                for o_ref in o_refs[n_tiles:]:
                    o_ref[...] = jnp.zeros_like(o_ref)

            for o_ref, val in zip(o_refs[n_tiles:], outs[n_tiles:]):
                o_ref[...] += val

    outs = pl.pallas_call(
        body, name=name, grid=(m // tm, n // tn), in_specs=in_specs, out_specs=out_specs, out_shape=out_shape,
        input_output_aliases=aliases, compiler_params=_params(),
    )(*operands)
    return outs[0] if len(outs) == 1 else tuple(outs)


def _tile_fwd(f, tiled, params, out_dtypes, *, tm, name):
    t = tiled[0].shape[0]
    assert t % tm == 0
    out_avals = jax.eval_shape(f, *[jax.ShapeDtypeStruct((tm, x.shape[1]), F32) for x in tiled],
                               *[jax.ShapeDtypeStruct(p.shape, F32) for p in params])
    nt, npar = len(tiled), len(params)

    def body(*refs):
        ins = [r[...].astype(F32) for r in refs[:nt + npar]]
        outs = f(*ins)
        for o_ref, val in zip(refs[nt + npar:], outs):
            o_ref[...] = val.astype(o_ref.dtype)

    return pl.pallas_call(
        body, name=name, grid=(t // tm,),
        in_specs=[pl.BlockSpec((tm, x.shape[1]), lambda i: (i, 0)) for x in tiled]
        + [pl.BlockSpec(p.shape, lambda i: (0, 0)) for p in params],
        out_specs=[pl.BlockSpec((tm, o.shape[1]), lambda i: (i, 0)) for o in out_avals],
        out_shape=[jax.ShapeDtypeStruct((t, o.shape[1]), dt) for o, dt in zip(out_avals, out_dtypes)],
        compiler_params=_params(),
    )(*tiled, *params)


def _tile_bwd(f, tiled, params, cots, d_tiled_dtypes, *, tm, name, diff_tiled=None):
    t = tiled[0].shape[0]
    assert t % tm == 0
    nt, npar, nc = len(tiled), len(params), len(cots)
    diff_tiled = list(range(nt)) if diff_tiled is None else diff_tiled

    def body(*refs):
        ins = [r[...].astype(F32) for r in refs[:nt + npar]]
        cts = [r[...].astype(F32) for r in refs[nt + npar:nt + npar + nc]]
        o_refs = refs[nt + npar + nc:]
        _, vjp = jax.vjp(f, *ins)
        grads = vjp(tuple(cts))
        for o_ref, idx in zip(o_refs[:len(diff_tiled)], diff_tiled):
            o_ref[...] = grads[idx].astype(o_ref.dtype)
        p_refs = o_refs[len(diff_tiled):]

        @pl.when(pl.program_id(0) == 0)
        def _():
            for p_ref in p_refs:
                p_ref[...] = jnp.zeros_like(p_ref)

        for p_ref, gp in zip(p_refs, grads[nt:]):
            p_ref[...] += gp

    outs = pl.pallas_call(
        body, name=name, grid=(t // tm,),
        in_specs=[pl.BlockSpec((tm, x.shape[1]), lambda i: (i, 0)) for x in tiled]
        + [pl.BlockSpec(p.shape, lambda i: (0, 0)) for p in params]
        + [pl.BlockSpec((tm, c.shape[1]), lambda i: (i, 0)) for c in cots],
        out_specs=[pl.BlockSpec((tm, tiled[idx].shape[1]), lambda i: (i, 0)) for idx in diff_tiled]
        + [pl.BlockSpec(p.shape, lambda i: (0, 0)) for p in params],
        out_shape=[jax.ShapeDtypeStruct(tiled[idx].shape, dt) for idx, dt in zip(diff_tiled, d_tiled_dtypes)]
        + [jax.ShapeDtypeStruct(p.shape, F32) for p in params],
        compiler_params=_params(),
    )(*tiled, *params, *cots)
    return outs[:len(diff_tiled)], outs[len(diff_tiled):]


def _gla_head(q, k, v, r, z, g, st):
    c = q.shape[0]
    causal = _iota2((c, c), 0) >= _iota2((c, c), 1)
    la = _log_sigmoid(z) * (1.0 / GLA_TAU)
    big_l = cumsum_rows(la)
    ep, en = jnp.exp(big_l), jnp.exp(-big_l)
    qs = q * (GLA_DK ** -0.5)
    qp = qs * ep
    s = jnp.where(causal, mm_nt(qp, k * en), mm_nt(qs * en, k * ep))
    o = mm_nn(s, v) + mm_nt(qp, st)
    l_end = jnp.sum(la, axis=0, keepdims=True)
    st_new = st * jnp.exp(l_end) + mm_tn(v, k * jnp.exp(l_end - big_l))
    u = _rms_norm(o, g) * (r * jax.nn.sigmoid(r))
    return u, st_new


def _gla_slices(h):
    q = slice(GLA_DK * h, GLA_DK * (h + 1))
    k = slice(GLA_HK + GLA_DK * h, GLA_HK + GLA_DK * (h + 1))
    v = slice(2 * GLA_HK + GLA_DV * h, 2 * GLA_HK + GLA_DV * (h + 1))
    r = slice(2 * GLA_HK + GLA_HV + GLA_DV * h, 2 * GLA_HK + GLA_HV + GLA_DV * (h + 1))
    return q, k, v, r


GLA_CHUNKS_PER_STEP = 2


def _gla_fwd(proj, z, norm_g, after):
    t = proj.shape[0]
    nc, per = t // CHUNK, GLA_CHUNKS_PER_STEP
    rows_per_step = per * CHUNK
    after = [a for a in after if a is not None]

    def body(proj_ref, z_ref, g_ref, *rest):
        u_ref, st_save_ref, st_ref = rest[len(after):]

        @pl.when(pl.program_id(0) == 0)
        def _():
            st_ref[...] = jnp.zeros_like(st_ref)

        g = g_ref[...]
        for h in range(GLA_HEADS):
            sq, sk, sv, sr = _gla_slices(h)
            st = st_ref[h]
            for c in range(per):
                rows = slice(c * CHUNK, (c + 1) * CHUNK)
                st_save_ref[c, h] = st
                u, st = _gla_head(proj_ref[rows, sq].astype(F32), proj_ref[rows, sk].astype(F32),
                                  proj_ref[rows, sv].astype(F32), proj_ref[rows, sr].astype(F32),
                                  z_ref[rows, GLA_DK * h:GLA_DK * (h + 1)], g, st)
                u_ref[rows, GLA_DV * h:GLA_DV * (h + 1)] = u.astype(u_ref.dtype)
            st_ref[h] = st

    return pl.pallas_call(
        body, name="gla_fwd", grid=(nc // per,),
        in_specs=[pl.BlockSpec((rows_per_step, GLA_MAIN), lambda i: (i, 0)),
                  pl.BlockSpec((rows_per_step, GLA_HK), lambda i: (i, 0)), pl.BlockSpec((1, GLA_DV), lambda i: (0, 0))]
        + [pl.BlockSpec(memory_space=pl.ANY)] * len(after),
        out_specs=[pl.BlockSpec((rows_per_step, GLA_HV), lambda i: (i, 0)),
                   pl.BlockSpec((per, GLA_HEADS, GLA_DV, GLA_DK), lambda i: (i, 0, 0, 0))],
        out_shape=[jax.ShapeDtypeStruct((t, GLA_HV), BF16), jax.ShapeDtypeStruct((nc, GLA_HEADS, GLA_DV, GLA_DK), F32)],
        scratch_shapes=[pltpu.VMEM((GLA_HEADS, GLA_DV, GLA_DK), F32)],
        compiler_params=_params(),
    )(proj, z, norm_g, *after)


def _gla_bwd(proj, z, norm_g, states, du, after):
    t = proj.shape[0]
    nc, per = t // CHUNK, GLA_CHUNKS_PER_STEP
    rows_per_step = per * CHUNK
    n_steps = nc // per
    after = [a for a in after if a is not None]

    def body(proj_ref, z_ref, g_ref, st_in_ref, du_ref, *rest):
        dproj_ref, dz_ref, dg_ref, dst_ref = rest[len(after):]

        @pl.when(pl.program_id(0) == 0)
        def _():
            dst_ref[...] = jnp.zeros_like(dst_ref)
            dg_ref[...] = jnp.zeros_like(dg_ref)

        g = g_ref[...]
        for h in range(GLA_HEADS):
            sq, sk, sv, sr = _gla_slices(h)
            dst = dst_ref[h]
            for c in reversed(range(per)):
                rows = slice(c * CHUNK, (c + 1) * CHUNK)
                ins = (proj_ref[rows, sq].astype(F32), proj_ref[rows, sk].astype(F32), proj_ref[rows, sv].astype(F32),
                       proj_ref[rows, sr].astype(F32), z_ref[rows, GLA_DK * h:GLA_DK * (h + 1)], g, st_in_ref[c, h])
                _, vjp = jax.vjp(_gla_head, *ins)
                dq, dk, dv, dr, dz, dg, dst = vjp((du_ref[rows, GLA_DV * h:GLA_DV * (h + 1)], dst))
                dproj_ref[rows, sq] = dq.astype(dproj_ref.dtype)
                dproj_ref[rows, sk] = dk.astype(dproj_ref.dtype)
                dproj_ref[rows, sv] = dv.astype(dproj_ref.dtype)
                dproj_ref[rows, sr] = dr.astype(dproj_ref.dtype)
                dz_ref[rows, GLA_DK * h:GLA_DK * (h + 1)] = dz
                dg_ref[...] += dg
            dst_ref[h] = dst

    rev = lambda i: (n_steps - 1 - i, 0)
    return pl.pallas_call(
        body, name="gla_bwd", grid=(n_steps,),
        in_specs=[pl.BlockSpec((rows_per_step, GLA_MAIN), rev), pl.BlockSpec((rows_per_step, GLA_HK), rev),
                  pl.BlockSpec((1, GLA_DV), lambda i: (0, 0)),
                  pl.BlockSpec((per, GLA_HEADS, GLA_DV, GLA_DK), lambda i: (n_steps - 1 - i, 0, 0, 0)),
                  pl.BlockSpec((rows_per_step, GLA_HV), rev)] + [pl.BlockSpec(memory_space=pl.ANY)] * len(after),
        out_specs=[pl.BlockSpec((rows_per_step, GLA_MAIN), rev), pl.BlockSpec((rows_per_step, GLA_HK), rev),
                   pl.BlockSpec((1, GLA_DV), lambda i: (0, 0))],
        out_shape=[jax.ShapeDtypeStruct((t, GLA_MAIN), BF16), jax.ShapeDtypeStruct((t, GLA_HK), F32),
                   jax.ShapeDtypeStruct((1, GLA_DV), F32)],
        scratch_shapes=[pltpu.VMEM((GLA_HEADS, GLA_DV, GLA_DK), F32)],
        compiler_params=_params(),
    )(proj, z, norm_g, states, du, *after)


def _mla_pre(cq, cos, sin, gq, gkv, w_uq, w_ukv):
    qlat = _rms_norm(cq[:, :MLA_RANK], gq)
    kvlat = _rms_norm(cq[:, MLA_RANK:2 * MLA_RANK], gkv)
    kr = cq[:, 2 * MLA_RANK:]
    q = mm_nn(qlat, w_uq) * ((MLA_NOPE + MLA_ROPE) ** -0.5)
    kv = mm_nn(kvlat, w_ukv)
    pieces = []
    for h in range(MLA_HEADS):
        qr = q[:, MLA_HEAD_PAD * h + MLA_NOPE:MLA_HEAD_PAD * (h + 1)]
        pieces += [q[:, MLA_HEAD_PAD * h:MLA_HEAD_PAD * h + MLA_NOPE], qr * cos + rot_half(qr) * sin]
    return jnp.concatenate(pieces, axis=1), kv, kr * cos + rot_half(kr) * sin


MLA_Q_TILE = 256


def _mla_attn_block(qn, qr, kv, kr, q0):
    tq, nk = qn.shape[0], kv.shape[0]
    s = mm_nt(qn, kv[:, :MLA_NOPE]) + mm_nt(qr, kr)
    visible = (_iota2((tq, nk), 1) // CHUNK) <= ((q0 + _iota2((tq, nk), 0)) // CHUNK)
    s = jnp.where(visible, s, -1e30)
    e = jnp.exp(s - jnp.max(s, -1, keepdims=True))
    p = e / jnp.sum(e, -1, keepdims=True)
    return mm_nn(p, kv[:, MLA_NOPE:])


def _mla_attn_fwd(q, kv, kr, after):
    t = q.shape[0]
    after = [a for a in after if a is not None]

    def body(q_ref, kv_ref, kr_ref, *rest):
        (o_ref,) = rest[len(after):]
        for i in range(t // MLA_Q_TILE):
            rows = slice(i * MLA_Q_TILE, (i + 1) * MLA_Q_TILE)
            keys = slice(0, (i + 1) * MLA_Q_TILE)
            o = _mla_attn_block(q_ref[rows, :MLA_NOPE].astype(F32), q_ref[rows, MLA_NOPE:].astype(F32),
                                kv_ref[keys, :].astype(F32), kr_ref[keys, :].astype(F32), i * MLA_Q_TILE)
            o_ref[rows, :] = o.astype(o_ref.dtype)

    return pl.pallas_call(
        body, name="mla_attn_fwd", grid=(MLA_HEADS,),
        in_specs=[pl.BlockSpec((t, MLA_HEAD_PAD), lambda h: (0, h)),
                  pl.BlockSpec((t, MLA_NOPE + MLA_V), lambda h: (0, h)), pl.BlockSpec((t, LANES), lambda h: (0, 0))]
        + [pl.BlockSpec(memory_space=pl.ANY)] * len(after),
        out_specs=pl.BlockSpec((t, MLA_V), lambda h: (0, h)),
        out_shape=jax.ShapeDtypeStruct((t, MLA_HEADS * MLA_V), BF16),
        compiler_params=_params(),
    )(q, kv, kr, *after)


def _mla_attn_bwd(q, kv, kr, do, after):
    t = q.shape[0]
    after = [a for a in after if a is not None]

    def body(q_ref, kv_ref, kr_ref, do_ref, *rest):
        dq_ref, dkv_ref, dkr_ref = rest[len(after):]
        dkv_ref[...] = jnp.zeros_like(dkv_ref)

        @pl.when(pl.program_id(0) == 0)
        def _():
            dkr_ref[...] = jnp.zeros_like(dkr_ref)

        for i in range(t // MLA_Q_TILE):
            rows = slice(i * MLA_Q_TILE, (i + 1) * MLA_Q_TILE)
            keys = slice(0, (i + 1) * MLA_Q_TILE)
            f = functools.partial(_mla_attn_block, q0=i * MLA_Q_TILE)
            _, vjp = jax.vjp(f, q_ref[rows, :MLA_NOPE].astype(F32), q_ref[rows, MLA_NOPE:].astype(F32),
                             kv_ref[keys, :].astype(F32), kr_ref[keys, :].astype(F32))
            dqn, dqr, dkv, dkr = vjp(do_ref[rows, :].astype(F32))
            dq_ref[rows, :MLA_NOPE] = dqn
            dq_ref[rows, MLA_NOPE:] = dqr
            dkv_ref[keys, :] += dkv
            dkr_ref[keys, :] += dkr

    return pl.pallas_call(
        body, name="mla_attn_bwd", grid=(MLA_HEADS,),
        in_specs=[pl.BlockSpec((t, MLA_HEAD_PAD), lambda h: (0, h)),
                  pl.BlockSpec((t, MLA_NOPE + MLA_V), lambda h: (0, h)), pl.BlockSpec((t, LANES), lambda h: (0, 0)),
                  pl.BlockSpec((t, MLA_V), lambda h: (0, h))] + [pl.BlockSpec(memory_space=pl.ANY)] * len(after),
        out_specs=[pl.BlockSpec((t, MLA_HEAD_PAD), lambda h: (0, h)),
                   pl.BlockSpec((t, MLA_NOPE + MLA_V), lambda h: (0, h)), pl.BlockSpec((t, LANES), lambda h: (0, 0))],
        out_shape=[jax.ShapeDtypeStruct(q.shape, F32), jax.ShapeDtypeStruct(kv.shape, F32),
                   jax.ShapeDtypeStruct(kr.shape, F32)],
        compiler_params=_params(),
    )(q, kv, kr, do, *after)


def _rope_tables(pos_col, inv_freq_row):
    t = pos_col.shape[0]

    def body(pos_ref, f_ref, cos_ref, sin_ref):
        ang = pos_ref[...].astype(F32) * f_ref[...]
        live = _iota2(ang.shape, 1) < MLA_ROPE
        cos_ref[...] = jnp.where(live, jnp.cos(ang), 0.0)
        sin_ref[...] = jnp.where(live, jnp.sin(ang), 0.0)

    return pl.pallas_call(
        body, name="rope_tables", out_shape=[jax.ShapeDtypeStruct((t, LANES), F32)] * 2, compiler_params=_params(),
    )(pos_col, inv_freq_row)


CONV_COL_TILE = 256


def _conv_gate(b, c, u, w0, w1, w2):
    cu = c * u
    return b * (w2 * cu + w1 * shift_rows(cu, 1) + w0 * shift_rows(cu, 2))


def _conv_specs(t):
    nb = D_MODEL // CONV_COL_TILE
    return [pl.BlockSpec((t, CONV_COL_TILE), lambda j, part=part: (0, part * nb + j)) for part in range(3)]


def _conv_fwd(bcu, w, after):
    t = bcu.shape[0]
    after = [a for a in after if a is not None]

    def body(b_ref, c_ref, u_ref, w_ref, *rest):
        (o_ref,) = rest[len(after):]
        o_ref[...] = _conv_gate(b_ref[...], c_ref[...], u_ref[...], w_ref[0:1, :], w_ref[1:2, :],
                                w_ref[2:3, :]).astype(o_ref.dtype)

    return pl.pallas_call(
        body, name="conv_fwd", grid=(D_MODEL // CONV_COL_TILE,),
        in_specs=_conv_specs(t) + [pl.BlockSpec((3, CONV_COL_TILE), lambda j: (0, j))]
        + [pl.BlockSpec(memory_space=pl.ANY)] * len(after),
        out_specs=pl.BlockSpec((t, CONV_COL_TILE), lambda j: (0, j)),
        out_shape=jax.ShapeDtypeStruct((t, D_MODEL), BF16), compiler_params=_params(),
    )(bcu, bcu, bcu, w, *after)


def _conv_bwd(bcu, w, dout, after):
    t = bcu.shape[0]
    after = [a for a in after if a is not None]

    def body(b_ref, c_ref, u_ref, w_ref, do_ref, *rest):
        db_ref, dc_ref, du_ref, dw_ref = rest[len(after):]
        _, vjp = jax.vjp(_conv_gate, b_ref[...], c_ref[...], u_ref[...], w_ref[0:1, :], w_ref[1:2, :], w_ref[2:3, :])
        db, dc, du, dw0, dw1, dw2 = vjp(do_ref[...])
        db_ref[...] = db.astype(db_ref.dtype)
        dc_ref[...] = dc.astype(dc_ref.dtype)
        du_ref[...] = du.astype(du_ref.dtype)
        dw_ref[0:1, :] = dw0
        dw_ref[1:2, :] = dw1
        dw_ref[2:3, :] = dw2

    col = pl.BlockSpec((t, CONV_COL_TILE), lambda j: (0, j))
    return pl.pallas_call(
        body, name="conv_bwd", grid=(D_MODEL // CONV_COL_TILE,),
        in_specs=_conv_specs(t) + [pl.BlockSpec((3, CONV_COL_TILE), lambda j: (0, j)), col]
        + [pl.BlockSpec(memory_space=pl.ANY)] * len(after),
        out_specs=[col, col, col, pl.BlockSpec((3, CONV_COL_TILE), lambda j: (0, j))],
        out_shape=[jax.ShapeDtypeStruct((t, D_MODEL), BF16)] * 3 + [jax.ShapeDtypeStruct((3, D_MODEL), F32)],
        compiler_params=_params(),
    )(bcu, bcu, bcu, w, dout, *after)


def _loss_head(y, target):
    t, d = y.shape
    tm = 256

    def body(y_ref, t_ref, loss_ref, dy_ref):
        @pl.when(pl.program_id(0) == 0)
        def _():
            loss_ref[...] = jnp.zeros_like(loss_ref)

        err = y_ref[...] - t_ref[...]
        dy_ref[...] = err * (1.0 / d)
        loss_ref[...] += 0.5 * jnp.sum(jnp.sum(err * err, axis=-1, keepdims=True) * (1.0 / d))

    tile = pl.BlockSpec((tm, d), lambda i: (i, 0))
    return pl.pallas_call(
        body, name="loss_head", grid=(t // tm,), in_specs=[tile, tile],
        out_specs=[pl.BlockSpec((8, LANES), lambda i: (0, 0)), tile],
        out_shape=[jax.ShapeDtypeStruct((8, LANES), F32), jax.ShapeDtypeStruct((t, d), F32)],
        compiler_params=_params(),
    )(y, target)


def _ln_epi(acc, res, g, b):
    a = ALPHA * res + acc
    return a, _layer_norm(a, g, b)


def _ln_fn(a, g, b):
    return (_layer_norm(a, g, b),)


def _ln_bwd_epi(scale):
    def epi(acc, res, a, g, b):
        _, vjp = jax.vjp(_ln_fn, a, g, b)
        return vjp((acc + scale * res,))
    return epi


def _relu_sq(h):
    r = jnp.maximum(h.astype(F32), 0.0)
    return r * r


def _pad_cols(w, n):
    return jnp.pad(w, ((0, 0), (0, n - w.shape[1])))


def _pad_rows(w, n):
    return jnp.pad(w, ((0, n - w.shape[0]), (0, 0)))


def _step(x, p, positions, target, small, comm):
    t = x.shape[0]
    w = small
    freqs = ROPE_BASE ** (-jnp.arange(0, MLA_ROPE // 2, dtype=F32) * (2.0 / MLA_ROPE))
    freq_row = jnp.concatenate([freqs, freqs, jnp.zeros((LANES - MLA_ROPE,), F32)])[None, :]
    cos, sin = _rope_tables(positions.reshape(t, 1), freq_row)

    saved = []
    for i in range(DEPTH):
        j, kind = i // 3, i % 3
        wl = comm.mixer_weights(i)
        s = {"x": x, "wl": wl}
        tok = comm.at("fwd", i, "begin", x)
        if kind == 0:
            s["w_main"] = wl["gla_w_in_t"][:GLA_MAIN]
            s["w_lr"] = _pad_rows(wl["gla_w_in_t"][GLA_MAIN:], LANES)
            s["w_up"] = _pad_rows(w["gla_w_gate_up"][j], LANES).astype(BF16)
            s["proj"] = _matmul(x, s["w_main"], name="gla_proj", tb=True, tn=1024, after=tok)
            s["glr"] = _matmul(x, s["w_lr"], name="gla_lr", tb=True, out_dtypes=(F32,))
            s["z"] = _matmul(s["glr"], s["w_up"], name="gla_gate", epi=lambda acc, b: (acc + b,),
                             epi_ins=(w["gla_b_gate"][j][None, :],), out_dtypes=(F32,))
            tok = comm.at("fwd", i, "proj_done", s["z"]) or []
            s["u"], s["states"] = _gla_fwd(s["proj"], s["z"], w["gla_norm_g"][j][None, :], tok)
        elif kind == 1:
            s["cq"] = _matmul(x, wl["mla_in"], name="mla_proj", tn=MLA_IN_PAD, b_at=REG_MLA_IN, out_dtypes=(F32,),
                              after=tok)
            s["pre_params"] = (w["mla_q_norm"][j][None, :], w["mla_kv_norm"][j][None, :], wl["mla_w_uq"], wl["mla_w_ukv"])
            s["q"], s["kv"], s["kr"] = _tile_fwd(_mla_pre, (s["cq"], cos, sin), s["pre_params"], (BF16, BF16, BF16),
                                                 tm=256, name="mla_pre_fwd")
            tok = comm.at("fwd", i, "proj_done", s["kv"]) or []
            s["u"] = _mla_attn_fwd(s["q"], s["kv"], s["kr"], tok)
        else:
            s["bcu"] = _matmul(x, wl["conv"], name="conv_proj", tb=True, tm=256, tn=3 * D_MODEL, b_at=REG_CONV,
                               out_dtypes=(F32,), after=tok)
            tok = comm.at("fwd", i, "proj_done", s["bcu"]) or []
            s["u"] = _conv_fwd(s["bcu"], w["conv_w"][j], tok)
        g0, b0 = w["ln_g"][i, 0][None, :], w["ln_b"][i, 0][None, :]
        g1, b1 = w["ln_g"][i, 1][None, :], w["ln_b"][i, 1][None, :]
        wa, wb = s["wa"], _ = comm.slab_weights(i, s["u"])
        s["a1"], s["x1"] = _matmul(s["u"], wa, name="mixer_out_ln", tm=256, tn=D_MODEL, b_at=REG_WOUT, epi=_ln_epi,
                                   epi_ins=(x, g0, b0), out_dtypes=(F32, F32))
        s["hh"] = _matmul(s["x1"], wa, name="mlp_up", tb=True, tm=256, tn=D_FF, b_at=REG_W1T)
        tok = comm.at("fwd", i, "mid", s["hh"])
        s["a2"], s["x2"] = _matmul(s["hh"], wa, name="mlp_down_ln", tm=256, tn=D_MODEL, b_at=REG_W2, a_fn=_relu_sq,
                                   epi=_ln_epi, epi_ins=(s["x1"], g1, b1), out_dtypes=(F32, F32), after=tok)
        s["pp"] = _matmul(p[i], wb, name="ple_proj", tb=True, tn=D_MODEL, b_at=REG_WPT)
        tok = comm.at("fwd", i, "end", s["pp"])
        x, s["gt"] = _matmul(s["x2"], wa, name="ple_gate", tn=1024, b_at=REG_WG,
                             epi=lambda acc, xr, pp: (xr + jax.nn.sigmoid(acc) * pp.astype(F32), acc),
                             epi_ins=(s["x2"], s["pp"]), out_dtypes=(F32, BF16), after=tok)
        saved.append(s)

    loss_part, dx = _loss_head(x, target)

    gw = {n: [None] * WEIGHTS[n][0][0] for n in SMALL + REPLICATED}
    ln_g_grads, ln_b_grads = [[None, None] for _ in range(DEPTH)], [[None, None] for _ in range(DEPTH)]
    resid = lambda acc, r: (acc + ALPHA * r,)
    plus = lambda acc, r: (acc + r,)
    for i in reversed(range(DEPTH)):
        j, kind = i // 3, i % 3
        s = saved[i]
        wa = s["wa"]
        ga = lax.empty((N_DEV, A_ROWS, D_MODEL), BF16)
        gb = lax.empty((N_DEV, REG_WPT[1], PLE_DIM), BF16)
        layer_grads = {}
        tok = comm.at("bwd", i, "begin", dx)

        def ple_bwd(dxo, gt, pp):
            sg = jax.nn.sigmoid(gt)
            return dxo * sg, dxo * pp * sg * (1.0 - sg)

        d_pp, d_gt = _tile_fwd(ple_bwd, (dx, s["gt"], s["pp"]), (), (BF16, BF16), tm=256, name="ple_bwd")
        gb = _matmul(d_pp, p[i], name="ple_proj_dw", ta=True, tm=REG_WPT[1], tn=PLE_DIM, out_at=REG_WPT, out_buf=gb, after=tok)
        ga = _matmul(s["x2"], d_gt, name="ple_gate_dw", ta=True, tm=REG_WG[1], tn=1024, out_at=REG_WG, out_buf=ga)
        g1, b1 = w["ln_g"][i, 1][None, :], w["ln_b"][i, 1][None, :]
        d_a2, ln_g_grads[i][1], ln_b_grads[i][1] = _matmul(
            d_gt, wa, name="ple_gate_dx_ln", tb=True, tm=256, tn=D_MODEL, b_at=REG_WG, epi=_ln_bwd_epi(1.0),
            epi_ins=(dx, s["a2"], g1, b1), out_dtypes=(F32,), n_row_sums=2, after=[ga, gb])
        tok = comm.at("bwd", i, "ln", d_a2)
        ga = _matmul(s["hh"], d_a2, name="mlp_down_dw", ta=True, tm=REG_W2[1], tn=1024, a_fn=_relu_sq, out_at=REG_W2,
                     out_buf=ga, after=tok)
        d_hh = _matmul(d_a2, wa, name="mlp_down_dx", tb=True, tm=256, tn=D_FF, b_at=REG_W2, after=ga,
                       epi=lambda acc, hh: (acc * 2.0 * jnp.maximum(hh.astype(F32), 0.0),), epi_ins=(s["hh"],))
        ga = _matmul(d_hh, s["x1"], name="mlp_up_dw", ta=True, tm=REG_W1T[1], tn=1024, out_at=REG_W1T, out_buf=ga)
        g0, b0 = w["ln_g"][i, 0][None, :], w["ln_b"][i, 0][None, :]
        d_a1, ln_g_grads[i][0], ln_b_grads[i][0] = _matmul(
            d_hh, wa, name="mlp_up_dx_ln", tm=256, tn=D_MODEL, b_at=REG_W1T, epi=_ln_bwd_epi(ALPHA),
            epi_ins=(d_a2, s["a1"], g0, b0), out_dtypes=(F32,), n_row_sums=2, after=ga)
        ga = _matmul(s["u"], d_a1, name="mixer_out_dw", ta=True, tm=REG_WOUT[1], tn=1024, out_at=REG_WOUT, out_buf=ga)
        du = _matmul(d_a1, wa, name="mixer_out_dx", tb=True, tn=1024, b_at=REG_WOUT, out_dtypes=(F32,), after=ga)
        comm.slab_grads(i, ga, gb)
        tok = comm.at("bwd", i, "slab_done", du) or []
        if kind == 0:
            dproj, dz, dg = _gla_bwd(s["proj"], s["z"], w["gla_norm_g"][j][None, :], s["states"], du, tok)
            tok = comm.at("bwd", i, "mixer_done", dproj)
            gw["gla_norm_g"][j] = dg[0]
            gw["gla_b_gate"][j] = _tile_bwd(lambda zz, b: (zz + b,), (s["z"],), (w["gla_b_gate"][j][None, :],), (dz,), (),
                                            tm=256, name="gla_bias_bwd", diff_tiled=[])[1][0][0]
            gw["gla_w_gate_up"][j] = _matmul(s["glr"], dz, name="gla_gate_dw", ta=True, out_dtypes=(F32,),
                                             after=tok)[:GLA_RANK]
            dglr = _matmul(dz, s["w_up"], name="gla_gate_dx", tb=True, out_dtypes=(F32,))
            dw_main = _matmul(dproj, s["x"], name="gla_proj_dw", ta=True, tn=1024, out_dtypes=(F32,))
            dw_lr = _matmul(dglr, s["x"], name="gla_lr_dw", ta=True, tn=1024, out_dtypes=(F32,))[:GLA_RANK]
            layer_grads["gla_w_in_t"] = jnp.concatenate([dw_main, dw_lr], axis=0)
            dx = _matmul(dproj, s["w_main"], name="gla_proj_dx", tn=1024, epi=resid, epi_ins=(d_a1,),
                         out_dtypes=(F32,), after=[dw_main, dw_lr, gw["gla_w_gate_up"][j]])
            dx = _matmul(dglr, s["w_lr"], name="gla_lr_dx", tn=1024, epi=plus, epi_ins=(dx,), out_dtypes=(F32,))
        elif kind == 1:
            dq, dkv, dkr = _mla_attn_bwd(s["q"], s["kv"], s["kr"], du, tok)
            tok = comm.at("bwd", i, "mixer_done", dq)
            (d_cq,), (dgq, dgkv, layer_grads["mla_uq"], layer_grads["mla_ukv"]) = _tile_bwd(
                _mla_pre, (s["cq"], cos, sin), s["pre_params"], (dq, dkv, dkr), (BF16,), tm=256, name="mla_pre_bwd",
                diff_tiled=[0])
            gw["mla_q_norm"][j], gw["mla_kv_norm"][j] = dgq[0], dgkv[0]
            layer_grads["mla_in"] = _matmul(s["x"], d_cq, name="mla_proj_dw", ta=True, tm=REG_MLA_IN[1], tn=MLA_IN_PAD,
                                            out_at=REG_MLA_IN, after=tok,
                                            out_buf=lax.empty((N_DEV, REG_MLA_IN[1], MLA_IN_PAD), BF16))
            dx = _matmul(d_cq, s["wl"]["mla_in"], name="mla_proj_dx", tb=True, tn=1024, b_at=REG_MLA_IN, epi=resid,
                         epi_ins=(d_a1,), out_dtypes=(F32,), after=layer_grads["mla_in"])
        else:
            db, dc, du_, dcw = _conv_bwd(s["bcu"], w["conv_w"][j], du, tok)
            tok = comm.at("bwd", i, "mixer_done", db)
            gw["conv_w"][j] = dcw
            dbcu = jnp.concatenate([db, dc, du_], axis=1)
            layer_grads["conv"] = _matmul(dbcu, s["x"], name="conv_proj_dw", ta=True, tm=REG_CONV[1], tn=1024,
                                          out_at=REG_CONV, out_buf=lax.empty((N_DEV, REG_CONV[1], D_MODEL), BF16),
                                          after=tok)
            dx = _matmul(dbcu, s["wl"]["conv"], name="conv_proj_dx", tn=1024, b_at=REG_CONV, epi=resid, epi_ins=(d_a1,),
                         out_dtypes=(F32,), after=layer_grads["conv"])
        comm.mixer_grads(i, layer_grads)

    gw["ln_g"] = [jnp.concatenate([a, b], axis=0) for a, b in ln_g_grads]
    gw["ln_b"] = [jnp.concatenate([a, b], axis=0) for a, b in ln_b_grads]
    return loss_part, dx, {n: jnp.stack(gw[n]).astype(F32) for n in gw}


MESH_IDS = pl.DeviceIdType.MESH
ANY = pl.BlockSpec(memory_space=pl.ANY)
HBM_SPEC = pl.BlockSpec(memory_space=pltpu.HBM)
SEM_SPEC = pl.BlockSpec(memory_space=pltpu.SEMAPHORE)
DATAFLOW_EFFECT = pltpu.SideEffectType.DATAFLOW_SIDE_EFFECTING
CORE_COPIES, CHIP_COPIES = 4, 3


def _my_place():
    return lax.axis_index("x"), lax.axis_index("y"), lax.axis_index("c")


def _other_chips(mx, my):
    return [(1 - mx, my), (mx, 1 - my), (1 - mx, 1 - my)]


def _remote(src, dst, send_sems, recv_sems, k, to):
    return pltpu.make_async_remote_copy(src_ref=src, dst_ref=dst, send_sem=send_sems.at[k], recv_sem=recv_sems.at[k],
                                        device_id=to, device_id_type=MESH_IDS)


def _gather_first_copies(n_arr):
    def make(bufs, send_sems, recv_sems):
        mx, my, mc = _my_place()
        mine = 4 * mx + 2 * my + mc
        peers = [(mx, my, 1 - mc)] + [(cx, cy, mc) for cx, cy in _other_chips(mx, my)]
        return [_remote(bufs[a].at[mine], bufs[a].at[mine], send_sems, recv_sems, (1 + CHIP_COPIES) * a + k, to)
                for a in range(n_arr) for k, to in enumerate(peers)]
    return make, (1 + CHIP_COPIES) * n_arr


def _gather_forward_copies(n_arr):
    def make(bufs, send_sems, recv_sems):
        mx, my, mc = _my_place()
        blocks = [4 * cx + 2 * cy + mc for cx, cy in _other_chips(mx, my)]
        return [_remote(bufs[a].at[blk], bufs[a].at[blk], send_sems, recv_sems, CHIP_COPIES * a + k, (mx, my, 1 - mc))
                for a in range(n_arr) for k, blk in enumerate(blocks)]
    return make, CHIP_COPIES * n_arr


def _scatter_core_copies(n_arr):
    def make(bufs, send_sems, recv_sems):
        mx, my, mc = _my_place()
        return [_remote(bufs[a].at[2 * k + (1 - mc)], bufs[n_arr + a].at[k], send_sems, recv_sems, CORE_COPIES * a + k,
                        (mx, my, 1 - mc)) for a in range(n_arr) for k in range(CORE_COPIES)]
    return make, CORE_COPIES * n_arr


def _scatter_chip_copies(n_arr):
    def make(bufs, send_sems, recv_sems):
        mx, my, mc = _my_place()
        return [_remote(bufs[a].at[2 * cx + cy], bufs[n_arr + a].at[k], send_sems, recv_sems, CHIP_COPIES * a + k,
                        (cx, cy, mc)) for a in range(n_arr) for k, (cx, cy) in enumerate(_other_chips(mx, my))]
    return make, CHIP_COPIES * n_arr


def _exchange(name, bufs, copies):
    make, n_copies = copies
    n = len(bufs)

    def body(*refs):
        descs = make(refs[:n], refs[2 * n], refs[2 * n + 1])
        for cp in descs:
            cp.start()
        for cp in descs:
            cp.wait()

    return pl.pallas_call(
        body, name=name, out_shape=[jax.ShapeDtypeStruct(b.shape, b.dtype) for b in bufs], in_specs=[ANY] * n,
        out_specs=[ANY] * n, input_output_aliases={i: i for i in range(n)},
        scratch_shapes=[pltpu.SemaphoreType.DMA((n_copies,)), pltpu.SemaphoreType.DMA((n_copies,))],
    )(*bufs)


def _exchange_start(name, bufs, copies, after):
    make, n_copies = copies
    n = len(bufs)

    def body(*refs):
        for cp in make(refs[:n], refs[n + 1], refs[n + 2]):
            cp.start()
        refs[-1][...] = jnp.zeros_like(refs[-1])

    outs = pl.pallas_call(
        body, name=name,
        out_shape=(pltpu.SemaphoreType.DMA((n_copies,)), pltpu.SemaphoreType.DMA((n_copies,)),
                   *[pltpu.HBM(b.shape, b.dtype) for b in bufs], jax.ShapeDtypeStruct((8, LANES), F32)),
        in_specs=[HBM_SPEC] * n + [ANY],
        out_specs=(SEM_SPEC, SEM_SPEC, *[HBM_SPEC] * n, pl.BlockSpec(memory_space=pltpu.VMEM)),
        input_output_aliases={i: 2 + i for i in range(n)},
        compiler_params=pltpu.CompilerParams(has_side_effects=DATAFLOW_EFFECT),
    )(*[pltpu.with_memory_space_constraint(b, pltpu.HBM) for b in bufs], after)
    return (outs[0], outs[1]), list(outs[2:2 + n]), outs[-1]


def _exchange_wait(name, sems, bufs, copies, after):
    make, _ = copies
    n = len(bufs)

    def body(*refs):
        for cp in make(refs[:n], refs[n], refs[n + 1]):
            cp.wait_send()
            cp.wait_recv()

    return list(pl.pallas_call(
        body, name=name, out_shape=[pltpu.HBM(b.shape, b.dtype) for b in bufs],
        in_specs=[HBM_SPEC] * n + [SEM_SPEC, SEM_SPEC, ANY], out_specs=[HBM_SPEC] * n,
        input_output_aliases={i: i for i in range(n)},
        compiler_params=pltpu.CompilerParams(has_side_effects=DATAFLOW_EFFECT),
    )(*bufs, *sems, after))


SUM_TILE_BYTES = 2 * 1024 * 1024


def _row_tile(r, c):
    best = None
    for cand in range(16, r + 1, 16):
        if r % cand == 0 and cand * c * 2 <= SUM_TILE_BYTES:
            best = cand
    return r if best is None else best


def _pair_sum(g, recv, my_c):
    _, r, c = g.shape
    tr = _row_tile(r, c)

    def body(c_ref, g_ref, r_ref, o_ref):
        o_ref[...] = (g_ref[...].astype(F32) + r_ref[...].astype(F32)).astype(o_ref.dtype)

    return pl.pallas_call(
        body, name="rs_pair_sum", out_shape=jax.ShapeDtypeStruct((4, r, c), g.dtype),
        grid_spec=pltpu.PrefetchScalarGridSpec(
            num_scalar_prefetch=1, grid=(4, r // tr),
            in_specs=[pl.BlockSpec((1, tr, c), lambda n, i, cr: (2 * n + cr[0], i, 0)),
                      pl.BlockSpec((1, tr, c), lambda n, i, cr: (n, i, 0))],
            out_specs=pl.BlockSpec((1, tr, c), lambda n, i, cr: (n, i, 0))),
        compiler_params=_params(),
    )(my_c, g, recv)


def _chip_sum(h, recv, my_chip):
    _, r, c = h.shape
    tr = _row_tile(r, c)

    def body(j_ref, h_ref, r0_ref, r1_ref, r2_ref, o_ref):
        o_ref[...] = ((h_ref[0].astype(F32) + r0_ref[0].astype(F32)) + r1_ref[0].astype(F32)) + r2_ref[0].astype(F32)

    return pl.pallas_call(
        body, name="rs_chip_sum", out_shape=jax.ShapeDtypeStruct((r, c), F32),
        grid_spec=pltpu.PrefetchScalarGridSpec(
            num_scalar_prefetch=1, grid=(r // tr,),
            in_specs=[pl.BlockSpec((1, tr, c), lambda i, jr: (jr[0], i, 0))]
            + [pl.BlockSpec((1, tr, c), lambda i, jr, n=n: (n, i, 0)) for n in range(3)],
            out_specs=pl.BlockSpec((tr, c), lambda i, jr: (i, 0))),
        compiler_params=_params(),
    )(my_chip, h, recv, recv, recv)


def _sum_blocks(g):
    n, r, c = g.shape

    def body(g_ref, o_ref):
        acc = g_ref[0]
        for k in range(1, n):
            acc = acc + g_ref[k]
        o_ref[...] = acc

    return pl.pallas_call(body, name="sum_blocks", out_shape=jax.ShapeDtypeStruct((r, c), F32), compiler_params=_params())(g)


def _pack(flat_parts, cols, row_multiple, dtype):
    flat = jnp.concatenate([f.astype(dtype) for f in flat_parts])
    per_row_block = cols * row_multiple
    padded = -(-flat.shape[0] // per_row_block) * per_row_block
    return jnp.pad(flat, (0, padded - flat.shape[0])).reshape(padded // cols, cols)


def _shard_shape(name):
    shape, axis = WEIGHTS[name]
    if axis is None:
        return shape
    return tuple(s // N_DEV if a == axis else s for a, s in enumerate(shape))


def _size(shape):
    n = 1
    for s in shape:
        n *= s
    return n


def _unshard(blocks, name):
    _, axis = WEIGHTS[name]
    return jnp.concatenate([blocks[k] for k in range(N_DEV)], axis=axis)


def _unpack_blocks(flat, names, lead):
    out, off = {}, 0
    for n in names:
        shp = _shard_shape(n)[1:] if lead else _shard_shape(n)
        out[n] = flat[..., off:off + _size(shp)].reshape(flat.shape[:-1] + shp)
        off += _size(shp)
    return out


def _layer_slabs(shard, i):
    j, kind = i // 3, i % 3
    w_out = (shard["gla_w_out"], shard["mla_w_out"], shard["conv_w_out"])[kind][j]
    out = {"a": jnp.concatenate([shard["mlp_w2"][i], shard["mlp_w1"][i].T, w_out, shard["ple_w_gate"][i]], axis=0).astype(BF16),
           "b": shard["ple_w_proj"][i].T.astype(BF16)}
    if kind == 0:
        out["gla"] = shard["gla_w_in"][j].T.astype(BF16)
    elif kind == 1:
        out["mla_in"] = _pad_cols(shard["mla_w_in"][j], MLA_IN_PAD).astype(BF16)
        out["mla_uq"] = _pad_cols(shard["mla_w_uq"][j], MLA_HEAD_PAD).astype(BF16)
        out["mla_ukv"] = shard["mla_w_ukv"][j].astype(BF16)
    else:
        out["conv"] = shard["conv_w_in"][j].T.astype(BF16)
    return out


def _mixer_weights(landed, i):
    kind = i % 3
    if kind == 0:
        return {"gla_w_in_t": landed["gla"].reshape(-1, D_MODEL)}
    if kind == 2:
        return {"conv": landed["conv"]}
    heads_side_by_side = lambda g: g.transpose(1, 0, 2).reshape(g.shape[1], -1)
    return {"mla_in": landed["mla_in"], "mla_w_uq": heads_side_by_side(landed["mla_uq"]),
            "mla_w_ukv": heads_side_by_side(landed["mla_ukv"])}


def _mixer_grad_buffers(layer_grads, i):
    kind = i % 3
    if kind == 0:
        return {"gla": layer_grads["gla_w_in_t"].reshape(N_DEV, -1, D_MODEL).astype(BF16)}
    if kind == 2:
        return {"conv": layer_grads["conv"]}
    head_blocks = lambda g: g.reshape(g.shape[0], N_DEV, -1).transpose(1, 0, 2).astype(BF16)
    return {"mla_in": layer_grads["mla_in"], "mla_uq": head_blocks(layer_grads["mla_uq"]),
            "mla_ukv": head_blocks(layer_grads["mla_ukv"])}


SLAB_KEYS = ("a", "b")


class _Overlap:
    def __init__(self, shard, small_pack):
        mx, my, mc = _my_place()
        self.my_c = mc.astype(jnp.int32).reshape(1)
        self.my_chip = (2 * mx + my).astype(jnp.int32).reshape(1)
        mine = 4 * mx + 2 * my + mc
        slabs = [_layer_slabs(shard, i) for i in range(DEPTH)]
        slabs[0]["small"] = small_pack
        self.landing = [{k: lax.dynamic_update_index_in_dim(lax.empty((N_DEV, *v.shape), v.dtype), v, mine, 0)
                         for k, v in slabs[i].items()} for i in range(DEPTH)]
        self.fly = {}
        self.grads = [{} for _ in range(DEPTH)]
        self.reduced = [{} for _ in range(DEPTH)]
        tok = self._gather_first(0, "mixer", shard["ln_g"])
        tok = self._gather_first(0, "slab", tok)
        bufs = self._wait("ag_first_mixer_l0", tok)
        self.landing[0].update(zip(self._keys(self.landing[0], "mixer"),
                                   _exchange("ag_forward_mixer_l0", bufs, _gather_forward_copies(len(bufs)))))

    @staticmethod
    def _keys(names, group):
        return [k for k in names if (k in SLAB_KEYS) == (group == "slab")]

    def _start(self, name, bufs, copies, after):
        sems, bufs, tok = _exchange_start(name + "_start", bufs, copies, after)
        self.fly[name] = (sems, bufs, copies)
        return tok

    def _wait(self, name, after):
        sems, bufs, copies = self.fly.pop(name)
        return _exchange_wait(name + "_wait", sems, bufs, copies, after)

    def mixer_weights(self, i):
        return _mixer_weights(self.landing[i], i)

    def slab_weights(self, i, dep):
        self._gather_done(i, "slab", dep)
        return self.landing[i]["a"], self.landing[i]["b"]

    def slab_grads(self, i, ga, gb):
        self.grads[i].update(a=ga, b=gb)

    def mixer_grads(self, i, layer_grads):
        self.grads[i].update(_mixer_grad_buffers(layer_grads, i))

    def at(self, phase, i, point, dep):
        toks = []
        if phase == "fwd":
            if point == "begin" and i == 0:
                toks.append(self._gather_first(1, "mixer", self.landing[0][self._keys(self.landing[0], "mixer")[0]]))
                toks.append(self._gather_first(1, "slab", toks[-1]))
            if point == "proj_done":
                toks.append(self._gather_forward(i, "slab", dep))
            if point == "mid" and i + 1 < DEPTH:
                toks.append(self._gather_forward(i + 1, "mixer", dep))
                if i + 2 < DEPTH:
                    toks.append(self._gather_first(i + 2, "mixer", dep))
                    toks.append(self._gather_first(i + 2, "slab", toks[-1]))
            if point == "end" and i + 1 < DEPTH:
                self._gather_done(i + 1, "mixer", dep)
        else:
            if point == "begin" and i + 1 < DEPTH:
                toks.append(self._scatter_cores(i + 1, "mixer", dep))
            if point == "ln" and i + 1 < DEPTH:
                toks.append(self._scatter_chips(i + 1, "mixer", dep))
            if point == "slab_done":
                if i + 1 < DEPTH:
                    self._scatter_done(i + 1, "slab", dep)
                    self._scatter_done(i + 1, "mixer", dep)
                toks.append(self._scatter_cores(i, "slab", dep))
            if point == "mixer_done":
                toks.append(self._scatter_chips(i, "slab", dep))
        return toks or None

    def _gather_first(self, i, group, after):
        bufs = [self.landing[i][k] for k in self._keys(self.landing[i], group)]
        return self._start(f"ag_first_{group}_l{i}", bufs, _gather_first_copies(len(bufs)), after)

    def _gather_forward(self, i, group, after):
        bufs = self._wait(f"ag_first_{group}_l{i}", after)
        return self._start(f"ag_forward_{group}_l{i}", bufs, _gather_forward_copies(len(bufs)), after)

    def _gather_done(self, i, group, after):
        keys = self._keys(self.landing[i], group)
        self.landing[i].update(zip(keys, self._wait(f"ag_forward_{group}_l{i}", after)))

    def _scatter_cores(self, i, group, after):
        gs = [self.grads[i][k] for k in self._keys(self.grads[i], group)]
        land = [lax.empty((4, *g.shape[1:]), g.dtype) for g in gs]
        return self._start(f"rs_cores_{group}_l{i}", gs + land, _scatter_core_copies(len(gs)), after)

    def _pair_sums(self, bufs):
        n = len(bufs) // 2
        hs = [_pair_sum(g, r, self.my_c) for g, r in zip(bufs[:n], bufs[n:])]
        return hs + [lax.empty((3, *h.shape[1:]), h.dtype) for h in hs]

    def _scatter_chips(self, i, group, after):
        bufs = self._pair_sums(self._wait(f"rs_cores_{group}_l{i}", after))
        return self._start(f"rs_chips_{group}_l{i}", bufs, _scatter_chip_copies(len(bufs) // 2), after)

    def _chip_sums(self, i, group, bufs):
        n = len(bufs) // 2
        for k, h, r in zip(self._keys(self.grads[i], group), bufs[:n], bufs[n:]):
            self.reduced[i][k] = _chip_sum(h, r, self.my_chip)

    def _scatter_done(self, i, group, after):
        self._chip_sums(i, group, self._wait(f"rs_chips_{group}_l{i}", after))

    def tail_begin(self, dep):
        return self._scatter_cores(0, "mixer", dep)

    def tail_middle(self, dep):
        self._scatter_done(0, "slab", dep)
        return self._scatter_chips(0, "mixer", dep)

    def tail_end(self, dep):
        self._scatter_done(0, "mixer", dep)


def _small_gather_start(x, name, after):
    mx, my, mc = _my_place()
    land = lax.dynamic_update_index_in_dim(lax.empty((N_DEV, *x.shape), x.dtype), x, 4 * mx + 2 * my + mc, 0)
    return name, _exchange_start(name + "_first_start", [land], _gather_first_copies(1), after)


def _small_gather_finish(started, after):
    name, (sems, bufs, _) = started
    bufs = _exchange_wait(name + "_first_wait", sems, bufs, _gather_first_copies(1), after)
    return _exchange(name + "_forward", bufs, _gather_forward_copies(1))[0]


def _adamw_math(w, g, m, v):
    m2 = ADAM_B1 * m + (1.0 - ADAM_B1) * g
    v2 = ADAM_B2 * v + (1.0 - ADAM_B2) * (g * g)
    m_hat = m2 / (1.0 - ADAM_B1 ** ADAM_STEP)
    v_hat = v2 / (1.0 - ADAM_B2 ** ADAM_STEP)
    return -ADAM_LR * (m_hat / (jnp.sqrt(v_hat) + ADAM_EPS) + ADAM_WD * w), m2, v2


ADAMW_TILE_BYTES = 1024 * 1024


def _adamw_layer(name, w, m, v, j, g, g_at, transposed, chain, after):
    n_layers, r, c = w.shape
    tr = max(t for t in range(8, r + 1, 8) if r % t == 0 and (t * c * 4 <= ADAMW_TILE_BYTES or t == 8))
    rb, rows = g_at
    if transposed:
        assert rows == c and g.shape[1] == r, (name, g.shape, g_at)
        g_spec = pl.BlockSpec((rows, tr), lambda i: (rb, i))
    else:
        assert rows == r and g.shape[1] == c, (name, g.shape, g_at)
        g_spec = pl.BlockSpec((tr, c), lambda i: (rb * (r // tr) + i, 0))
    extra = list(chain or []) + [a for a in (after or []) if a is not None]
    n_chain = 4 if chain else 0

    def body(w_ref, m_ref, v_ref, g_ref, *rest):
        g_out, d_out, m_out, v_out, tok_ref = rest[len(extra):]
        gv = g_ref[...].T if transposed else g_ref[...]
        g_out[0] = gv
        d_out[0], m_out[0], v_out[0] = _adamw_math(w_ref[0], gv, m_ref[0], v_ref[0])
        tok_ref[...] = jnp.zeros_like(tok_ref)

    layer_spec = pl.BlockSpec((1, tr, c), lambda i: (j, i, 0))
    outs = pl.pallas_call(
        body, name=f"adamw_{name}_l{j}", grid=(r // tr,),
        in_specs=[layer_spec] * 3 + [g_spec] + [pl.BlockSpec(memory_space=pl.ANY)] * len(extra),
        out_specs=[layer_spec] * 4 + [pl.BlockSpec((8, LANES), lambda i: (0, 0))],
        out_shape=[jax.ShapeDtypeStruct(w.shape, F32)] * 4 + [jax.ShapeDtypeStruct((8, LANES), F32)],
        input_output_aliases={4 + k: k for k in range(n_chain)}, compiler_params=_params(),
    )(w, m, v, g, *extra)
    return list(outs[:4]), outs[4]


def _adamw(w, g, m, v, name):
    shape = w.shape
    cols = shape[-1]
    rows = _size(shape) // cols
    tr = rows
    for cand in (512, 256, 128, 64, 32, 16, 8):
        if rows > cand and rows % cand == 0:
            tr = cand
            break

    def body(w_ref, g_ref, m_ref, v_ref, d_ref, mo_ref, vo_ref):
        d_ref[...], mo_ref[...], vo_ref[...] = _adamw_math(w_ref[...], g_ref[...], m_ref[...], v_ref[...])

    spec = pl.BlockSpec((tr, cols), lambda i: (i, 0))
    outs = pl.pallas_call(
        body, name="adamw_" + name, grid=(rows // tr,), in_specs=[spec] * 4, out_specs=[spec] * 3,
        out_shape=[jax.ShapeDtypeStruct((rows, cols), F32)] * 3, compiler_params=_params(),
    )(*[a.reshape(rows, cols) for a in (w, g, m, v)])
    return [o.reshape(shape) for o in outs]


def kernel(x, p, positions, gla_w_in, gla_w_gate_up, gla_b_gate, gla_norm_g, gla_w_out, mla_w_in, mla_q_norm, mla_kv_norm, mla_w_uq, mla_w_ukv, mla_w_out, conv_w_in, conv_w, conv_w_out, ln_g, ln_b, mlp_w1, mlp_w2, ple_w_gate, ple_w_proj, loss_target, m_gla_w_in, m_gla_w_gate_up, m_gla_b_gate, m_gla_norm_g, m_gla_w_out, m_mla_w_in, m_mla_q_norm, m_mla_kv_norm, m_mla_w_uq, m_mla_w_ukv, m_mla_w_out, m_conv_w_in, m_conv_w, m_conv_w_out, m_ln_g, m_ln_b, m_mlp_w1, m_mlp_w2, m_ple_w_gate, m_ple_w_proj, v_gla_w_in, v_gla_w_gate_up, v_gla_b_gate, v_gla_norm_g, v_gla_w_out, v_mla_w_in, v_mla_q_norm, v_mla_kv_norm, v_mla_w_uq, v_mla_w_ukv, v_mla_w_out, v_conv_w_in, v_conv_w, v_conv_w_out, v_ln_g, v_ln_b, v_mlp_w1, v_mlp_w2, v_ple_w_gate, v_ple_w_proj):
    args = locals()
    shard = {n: args[n] for n in WEIGHT_NAMES}
    mom = {n: args["m_" + n] for n in WEIGHT_NAMES}
    var = {n: args["v_" + n] for n in WEIGHT_NAMES}
    mx, my, mc = _my_place()

    comm = _Overlap(shard, _pack([shard[n].reshape(-1) for n in SMALL], LANES, 8, F32))
    small_all = comm.landing[0]["small"]
    small = {n: shard[n] for n in REPLICATED}
    small.update({n: _unshard(blk, n) for n, blk in _unpack_blocks(small_all.reshape(N_DEV, -1), SMALL, lead=False).items()})
    loss_part, grad_x, small_grads = _step(x[0], p[:, 0], positions[0], loss_target[0], small, comm)

    chains = {}

    def update(name, j, g, g_at, transposed, tok):
        chains[name], tok = _adamw_layer(name, shard[name], mom[name], var[name], j, g, g_at, transposed,
                                         chains.get(name), [tok])
        return tok

    def update_layer(i, groups, tok):
        j, kind = i // 3, i % 3
        red = comm.reduced[i]
        if "slab" in groups:
            tok = update("mlp_w2", i, red["a"], REG_W2, False, tok)
            tok = update("mlp_w1", i, red["a"], REG_W1T, True, tok)
            tok = update(("gla_w_out", "mla_w_out", "conv_w_out")[kind], j, red["a"], REG_WOUT, False, tok)
            tok = update("ple_w_gate", i, red["a"], REG_WG, False, tok)
            tok = update("ple_w_proj", i, red["b"], REG_WPT, True, tok)
        if "mixer" in groups:
            if kind == 0:
                tok = update("gla_w_in", j, red["gla"].T, (0, D_MODEL), False, tok)
            elif kind == 2:
                tok = update("conv_w_in", j, red["conv"], REG_CONV, True, tok)
            else:
                for n, g in (("mla_w_in", red["mla_in"][:, :MLA_IN]), ("mla_w_ukv", red["mla_ukv"]),
                             ("mla_w_uq", red["mla_uq"][:, :MLA_NOPE + MLA_ROPE])):
                    tok = update(n, j, g, (0, g.shape[0]), False, tok)
        return tok

    tok = comm.tail_begin(grad_x)
    tok = update_layer(3, ("slab", "mixer"), tok)
    tok = update_layer(2, ("slab", "mixer"), tok)
    tok = comm.tail_middle(tok)
    small_parts = [loss_part[0, :1]] + [small_grads[n].reshape(-1) for n in SMALL + REPLICATED]
    small_gather = _small_gather_start(_pack(small_parts, LANES, 8, F32), "ag_small_grads", tok)
    tok = update_layer(1, ("slab", "mixer"), small_gather[1][2])
    tok = update_layer(0, ("slab",), tok)
    comm.tail_end(tok)
    tok = update_layer(0, ("mixer",), tok)
    red_small = _sum_blocks(_small_gather_finish(small_gather, tok)).reshape(-1)
    loss = red_small[0]
    off = 1
    dev = 4 * mx + 2 * my + mc
    for n in SMALL + REPLICATED:
        shape, axis = WEIGHTS[n]
        full_g = red_small[off:off + _size(shape)].reshape(shape)
        off += _size(shape)
        if axis is not None:
            width = shape[axis] // N_DEV
            full_g = lax.dynamic_slice_in_dim(full_g, dev * width, width, axis=axis)
        chains[n] = [full_g, *_adamw(shard[n], full_g, mom[n], var[n], n)]
    return (loss, grad_x[None], *[chains[n][k] for k in range(4) for n in WEIGHT_NAMES])
```

```python
import functools

import jax
import jax.numpy as jnp
from jax import lax
from jax.experimental import pallas as pl
from jax.experimental.pallas import tpu as pltpu

F32, BF16 = jnp.float32, jnp.bfloat16
HIGHEST = lax.Precision.HIGHEST
MESH_AXES = ("x", "y", "c")
N_DEV = 8

D_MODEL = 1024
SEQ = 2048
DEPTH = 4
CHUNK = 64
ALPHA = (2 * DEPTH) ** 0.25
LN_EPS = 1e-5
RMS_EPS = 1e-6
PLE_DIM = 256
D_FF = 4 * D_MODEL
GLA_HEADS = 4
GLA_DK = 128
GLA_DV = 256
GLA_RANK = 16
GLA_TAU = 16.0
GLA_HK = GLA_HEADS * GLA_DK
GLA_HV = GLA_HEADS * GLA_DV
GLA_MAIN = 2 * GLA_HK + GLA_HV + D_MODEL
MLA_HEADS = 8
MLA_NOPE = 128
MLA_ROPE = 64
MLA_V = 128
MLA_RANK = 256
MLA_IN = 2 * MLA_RANK + MLA_ROPE
MLA_IN_PAD = 640
ROPE_BASE = 10000.0
LANES = 128
ADAM_LR, ADAM_B1, ADAM_B2, ADAM_EPS, ADAM_WD, ADAM_STEP = 0.001, 0.9, 0.999, 1e-08, 0.01, 10

V7X_VMEM_LIMIT_BYTES = 56 * 1024 * 1024
PACK_COLS = 1024
PACK_ROW_TILE = 256

WEIGHTS = {
    "gla_w_in": ((2, 1024, 3088), 2), "gla_w_gate_up": ((2, 16, 512), 2), "gla_b_gate": ((2, 512), 1),
    "gla_norm_g": ((2, 256), 1), "gla_w_out": ((2, 1024, 1024), 1), "mla_w_in": ((1, 1024, 576), 1),
    "mla_q_norm": ((1, 256), None), "mla_kv_norm": ((1, 256), None), "mla_w_uq": ((1, 256, 1536), 2),
    "mla_w_ukv": ((1, 256, 2048), 2), "mla_w_out": ((1, 1024, 1024), 1), "conv_w_in": ((1, 1024, 3072), 2),
    "conv_w": ((1, 3, 1024), 2), "conv_w_out": ((1, 1024, 1024), 1), "ln_g": ((4, 2, 1024), 2),
    "ln_b": ((4, 2, 1024), 2), "mlp_w1": ((4, 1024, 4096), 2), "mlp_w2": ((4, 4096, 1024), 1),
    "ple_w_gate": ((4, 1024, 1024), 1), "ple_w_proj": ((4, 256, 1024), 2),
}
WEIGHT_NAMES = list(WEIGHTS)
REG_W2, REG_W1T, REG_WOUT, REG_WG = (0, 512), (1, 512), (8, 128), (9, 128)
A_ROWS = 1280
REG_CONV = (0, 384)
REG_WPT = (0, 128)
REG_MLA_IN = (0, 128)
MLA_HEAD_PAD = 2 * LANES
SMALL = ["gla_w_gate_up", "gla_b_gate", "gla_norm_g", "conv_w", "ln_g", "ln_b"]
REPLICATED = ["mla_q_norm", "mla_kv_norm"]


def _params(**kw):
    return pltpu.CompilerParams(vmem_limit_bytes=V7X_VMEM_LIMIT_BYTES, **kw)


def _dot(a, b, ca, cb, precision=None):
    return lax.dot_general(a, b, (((ca,), (cb,)), ((), ())), precision=precision, preferred_element_type=F32)


def _nn(a, b):
    return _dot(a.astype(BF16), b.astype(BF16), 1, 0)


def _nt(a, b):
    return _dot(a.astype(BF16), b.astype(BF16), 1, 1)


def _tn(a, b):
    return _dot(a.astype(BF16), b.astype(BF16), 0, 0)


@jax.custom_vjp
def mm_nn(a, b):
    return _nn(a, b)


def _mm_nn_fwd(a, b):
    return _nn(a, b), (a, b)


def _mm_nn_bwd(res, g):
    a, b = res
    return _nt(g, b).astype(a.dtype), _tn(a, g).astype(b.dtype)


mm_nn.defvjp(_mm_nn_fwd, _mm_nn_bwd)


@jax.custom_vjp
def mm_nt(a, b):
    return _nt(a, b)


def _mm_nt_fwd(a, b):
    return _nt(a, b), (a, b)


def _mm_nt_bwd(res, g):
    a, b = res
    return _nn(g, b).astype(a.dtype), _tn(g, a).astype(b.dtype)


mm_nt.defvjp(_mm_nt_fwd, _mm_nt_bwd)


@jax.custom_vjp
def mm_tn(a, b):
    return _tn(a, b)


def _mm_tn_fwd(a, b):
    return _tn(a, b), (a, b)


def _mm_tn_bwd(res, g):
    a, b = res
    return _nt(b, g).astype(a.dtype), _nn(a, g).astype(b.dtype)


mm_tn.defvjp(_mm_tn_fwd, _mm_tn_bwd)


def _iota2(shape, dim):
    return lax.broadcasted_iota(jnp.int32, shape, dim)


@jax.custom_vjp
def cumsum_rows(x):
    n = x.shape[0]
    tri = (_iota2((n, n), 0) >= _iota2((n, n), 1)).astype(F32)
    return _dot(tri, x, 1, 0, precision=HIGHEST)


def _cumsum_fwd(x):
    return cumsum_rows(x), None


def _cumsum_bwd(_, g):
    n = g.shape[0]
    tri_t = (_iota2((n, n), 0) <= _iota2((n, n), 1)).astype(F32)
    return (_dot(tri_t, g, 1, 0, precision=HIGHEST),)


cumsum_rows.defvjp(_cumsum_fwd, _cumsum_bwd)


def _rot_matrix(transposed):
    i, j = _iota2((LANES, LANES), 0), _iota2((LANES, LANES), 1)
    if transposed:
        i, j = j, i
    half = MLA_ROPE // 2
    plus = (i == j - half) & (j >= half) & (j < MLA_ROPE)
    minus = (i == j + half) & (j < half)
    return plus.astype(F32) - minus.astype(F32)


@jax.custom_vjp
def rot_half(x):
    return _dot(x, _rot_matrix(False), 1, 0, precision=HIGHEST)


def _rot_fwd(x):
    return rot_half(x), None


def _rot_bwd(_, g):
    return (_dot(g, _rot_matrix(True), 1, 0, precision=HIGHEST),)


rot_half.defvjp(_rot_fwd, _rot_bwd)


def _shift_rows_raw(x, s):
    n = x.shape[0]
    row = _iota2(x.shape, 0)
    rolled = pltpu.roll(x, s % n, 0)
    keep = (row >= s) if s > 0 else (row < n + s)
    return jnp.where(keep, rolled, 0.0)


@functools.partial(jax.custom_vjp, nondiff_argnums=(1,))
def shift_rows(x, s):
    return _shift_rows_raw(x, s)


def _shift_fwd(x, s):
    return _shift_rows_raw(x, s), None


def _shift_bwd(s, _, g):
    return (_shift_rows_raw(g, -s),)


shift_rows.defvjp(_shift_fwd, _shift_bwd)


def _layer_norm(a, g, b):
    mu = jnp.mean(a, -1, keepdims=True)
    xc = a - mu
    var = jnp.mean(xc * xc, -1, keepdims=True)
    return xc * lax.rsqrt(var + LN_EPS) * g + b


def _rms_norm(a, g):
    return a * lax.rsqrt(jnp.mean(a * a, -1, keepdims=True) + RMS_EPS) * g


def _log_sigmoid(z):
    return jnp.minimum(z, 0.0) - jnp.log(1.0 + jnp.exp(-jnp.abs(z)))


def _matmul(a, b, *, name, ta=False, tb=False, tm=512, tn=512, a_fn=None, epi=None, epi_ins=(), out_dtypes=(BF16,),
            b_at=None, out_at=None, out_buf=None, after=None, n_row_sums=0):
    m = a.shape[1] if ta else a.shape[0]
    k = a.shape[0] if ta else a.shape[1]
    if b_at is None:
        n, kb = (b.shape[0], b.shape[1]) if tb else (b.shape[1], b.shape[0])
    else:
        rb, r = b_at
        n, kb = (N_DEV * r, b.shape[2]) if tb else (b.shape[2], N_DEV * r)
    assert kb == k, (name, a.shape, b.shape, k, kb)
    tm, tn = min(tm, m), min(tn, n)
    assert m % tm == 0 and n % tn == 0, (name, m, n, tm, tn)
    a_spec = pl.BlockSpec((k, tm), lambda i, j: (0, i)) if ta else pl.BlockSpec((tm, k), lambda i, j: (i, 0))
    if b_at is None:
        b_spec = pl.BlockSpec((tn, k), lambda i, j: (j, 0)) if tb else pl.BlockSpec((k, tn), lambda i, j: (0, j))
        load_b = lambda ref: ref[...]
    elif tb and tn == n:
        b_spec = pl.BlockSpec((N_DEV, r, k), lambda i, j: (0, rb, 0))
        load_b = lambda ref: ref[...].reshape(n, k)
    elif tb:
        assert tn == r, (name, tn, r)
        b_spec = pl.BlockSpec((1, r, k), lambda i, j: (j, rb, 0))
        load_b = lambda ref: ref[0]
    else:
        b_spec = pl.BlockSpec((N_DEV, r, tn), lambda i, j: (0, rb, j))
        load_b = lambda ref: ref[...].reshape(k, tn)
    e_specs = []
    for e in epi_ins:
        if e.shape == (1, n):
            e_specs.append(pl.BlockSpec((1, tn), lambda i, j: (0, j)))
        else:
            assert e.shape == (m, n), (name, e.shape, m, n)
            e_specs.append(pl.BlockSpec((tm, tn), lambda i, j: (i, j)))
    n_epi = len(epi_ins)
    ca, cb = (0 if ta else 1), (1 if tb else 0)
    operands = [a, b, *epi_ins]
    in_specs = [a_spec, b_spec, *e_specs]
    if out_at is None:
        assert n_row_sums == 0 or tn == n, (name, tn, n)
        out_specs = [pl.BlockSpec((tm, tn), lambda i, j: (i, j)) for _ in out_dtypes]
        out_specs += [pl.BlockSpec((1, n), lambda i, j: (0, 0))] * n_row_sums
        out_shape = [jax.ShapeDtypeStruct((m, n), dt) for dt in out_dtypes]
        out_shape += [jax.ShapeDtypeStruct((1, n), F32)] * n_row_sums
        aliases, n_buf = {}, 0
    else:
        orb, orows = out_at
        assert len(out_dtypes) == 1 and m == N_DEV * orows and n == out_buf.shape[2], (name, m, n)
        if tm > orows:
            assert tm % orows == 0, (name, tm, orows)
            out_specs = [pl.BlockSpec((tm // orows, orows, tn), lambda i, j: (i, orb, j))]
        else:
            per = orows // tm
            out_specs = [pl.BlockSpec((1, tm, tn), lambda i, j: (i // per, orb * per + i % per, j))]
        out_shape = [jax.ShapeDtypeStruct(out_buf.shape, out_buf.dtype)]
        operands.append(out_buf)
        in_specs.append(pl.BlockSpec(memory_space=pl.ANY))
        aliases, n_buf = {len(operands) - 1: 0}, 1
    for dep in ([] if after is None else after if isinstance(after, (list, tuple)) else [after]):
        if dep is not None:
            operands.append(dep)
            in_specs.append(pl.BlockSpec(memory_space=pl.ANY))
            n_buf += 1

    def body(a_ref, b_ref, *rest):
        av = a_ref[...]
        if a_fn is not None:
            av = a_fn(av)
        acc = _dot(av.astype(BF16), load_b(b_ref).astype(BF16), ca, cb)
        outs = epi(acc, *[r_[...] for r_ in rest[:n_epi]]) if epi is not None else (acc,)
        o_refs = rest[n_epi + n_buf:]
        n_tiles = len(o_refs) - n_row_sums
        for o_ref, val in zip(o_refs[:n_tiles], outs):
            o_ref[...] = val.astype(o_ref.dtype).reshape(o_ref.shape)
        if n_row_sums:
            @pl.when(pl.program_id(0) == 0)
            def _():
                for o_ref in o_refs[n_tiles:]:
                    o_ref[...] = jnp.zeros_like(o_ref)

            for o_ref, val in zip(o_refs[n_tiles:], outs[n_tiles:]):
                o_ref[...] += val

    outs = pl.pallas_call(
        body, name=name, grid=(m // tm, n // tn), in_specs=in_specs, out_specs=out_specs, out_shape=out_shape,
        input_output_aliases=aliases, compiler_params=_params(),
    )(*operands)
    return outs[0] if len(outs) == 1 else tuple(outs)


def _tile_fwd(f, tiled, params, out_dtypes, *, tm, name):
    t = tiled[0].shape[0]
    assert t % tm == 0
    out_avals = jax.eval_shape(f, *[jax.ShapeDtypeStruct((tm, x.shape[1]), F32) for x in tiled],
                               *[jax.ShapeDtypeStruct(p.shape, F32) for p in params])
    nt, npar = len(tiled), len(params)

    def body(*refs):
        ins = [r[...].astype(F32) for r in refs[:nt + npar]]
        outs = f(*ins)
        for o_ref, val in zip(refs[nt + npar:], outs):
            o_ref[...] = val.astype(o_ref.dtype)

    return pl.pallas_call(
        body, name=name, grid=(t // tm,),
        in_specs=[pl.BlockSpec((tm, x.shape[1]), lambda i: (i, 0)) for x in tiled]
        + [pl.BlockSpec(p.shape, lambda i: (0, 0)) for p in params],
        out_specs=[pl.BlockSpec((tm, o.shape[1]), lambda i: (i, 0)) for o in out_avals],
        out_shape=[jax.ShapeDtypeStruct((t, o.shape[1]), dt) for o, dt in zip(out_avals, out_dtypes)],
        compiler_params=_params(),
    )(*tiled, *params)


def _tile_bwd(f, tiled, params, cots, d_tiled_dtypes, *, tm, name, diff_tiled=None):
    t = tiled[0].shape[0]
    assert t % tm == 0
    nt, npar, nc = len(tiled), len(params), len(cots)
    diff_tiled = list(range(nt)) if diff_tiled is None else diff_tiled

    def body(*refs):
        ins = [r[...].astype(F32) for r in refs[:nt + npar]]
        cts = [r[...].astype(F32) for r in refs[nt + npar:nt + npar + nc]]
        o_refs = refs[nt + npar + nc:]
        _, vjp = jax.vjp(f, *ins)
        grads = vjp(tuple(cts))
        for o_ref, idx in zip(o_refs[:len(diff_tiled)], diff_tiled):
            o_ref[...] = grads[idx].astype(o_ref.dtype)
        p_refs = o_refs[len(diff_tiled):]

        @pl.when(pl.program_id(0) == 0)
        def _():
            for p_ref in p_refs:
                p_ref[...] = jnp.zeros_like(p_ref)

        for p_ref, gp in zip(p_refs, grads[nt:]):
            p_ref[...] += gp

    outs = pl.pallas_call(
        body, name=name, grid=(t // tm,),
        in_specs=[pl.BlockSpec((tm, x.shape[1]), lambda i: (i, 0)) for x in tiled]
        + [pl.BlockSpec(p.shape, lambda i: (0, 0)) for p in params]
        + [pl.BlockSpec((tm, c.shape[1]), lambda i: (i, 0)) for c in cots],
        out_specs=[pl.BlockSpec((tm, tiled[idx].shape[1]), lambda i: (i, 0)) for idx in diff_tiled]
        + [pl.BlockSpec(p.shape, lambda i: (0, 0)) for p in params],
        out_shape=[jax.ShapeDtypeStruct(tiled[idx].shape, dt) for idx, dt in zip(diff_tiled, d_tiled_dtypes)]
        + [jax.ShapeDtypeStruct(p.shape, F32) for p in params],
        compiler_params=_params(),
    )(*tiled, *params, *cots)
    return outs[:len(diff_tiled)], outs[len(diff_tiled):]


def _gla_head(q, k, v, r, z, g, st):
    c = q.shape[0]
    causal = _iota2((c, c), 0) >= _iota2((c, c), 1)
    la = _log_sigmoid(z) * (1.0 / GLA_TAU)
    big_l = cumsum_rows(la)
    ep, en = jnp.exp(big_l), jnp.exp(-big_l)
    qs = q * (GLA_DK ** -0.5)
    qp = qs * ep
    s = jnp.where(causal, mm_nt(qp, k * en), mm_nt(qs * en, k * ep))
    o = mm_nn(s, v) + mm_nt(qp, st)
    l_end = jnp.sum(la, axis=0, keepdims=True)
    st_new = st * jnp.exp(l_end) + mm_tn(v, k * jnp.exp(l_end - big_l))
    u = _rms_norm(o, g) * (r * jax.nn.sigmoid(r))
    return u, st_new


def _gla_slices(h):
    q = slice(GLA_DK * h, GLA_DK * (h + 1))
    k = slice(GLA_HK + GLA_DK * h, GLA_HK + GLA_DK * (h + 1))
    v = slice(2 * GLA_HK + GLA_DV * h, 2 * GLA_HK + GLA_DV * (h + 1))
    r = slice(2 * GLA_HK + GLA_HV + GLA_DV * h, 2 * GLA_HK + GLA_HV + GLA_DV * (h + 1))
    return q, k, v, r


GLA_CHUNKS_PER_STEP = 2


def _gla_fwd(proj, z, norm_g, after):
    t = proj.shape[0]
    nc, per = t // CHUNK, GLA_CHUNKS_PER_STEP
    rows_per_step = per * CHUNK
    after = [a for a in after if a is not None]

    def body(proj_ref, z_ref, g_ref, *rest):
        u_ref, st_save_ref, st_ref = rest[len(after):]

        @pl.when(pl.program_id(0) == 0)
        def _():
            st_ref[...] = jnp.zeros_like(st_ref)

        g = g_ref[...]
        for h in range(GLA_HEADS):
            sq, sk, sv, sr = _gla_slices(h)
            st = st_ref[h]
            for c in range(per):
                rows = slice(c * CHUNK, (c + 1) * CHUNK)
                st_save_ref[c, h] = st
                u, st = _gla_head(proj_ref[rows, sq].astype(F32), proj_ref[rows, sk].astype(F32),
                                  proj_ref[rows, sv].astype(F32), proj_ref[rows, sr].astype(F32),
                                  z_ref[rows, GLA_DK * h:GLA_DK * (h + 1)], g, st)
                u_ref[rows, GLA_DV * h:GLA_DV * (h + 1)] = u.astype(u_ref.dtype)
            st_ref[h] = st

    return pl.pallas_call(
        body, name="gla_fwd", grid=(nc // per,),
        in_specs=[pl.BlockSpec((rows_per_step, GLA_MAIN), lambda i: (i, 0)),
                  pl.BlockSpec((rows_per_step, GLA_HK), lambda i: (i, 0)), pl.BlockSpec((1, GLA_DV), lambda i: (0, 0))]
        + [pl.BlockSpec(memory_space=pl.ANY)] * len(after),
        out_specs=[pl.BlockSpec((rows_per_step, GLA_HV), lambda i: (i, 0)),
                   pl.BlockSpec((per, GLA_HEADS, GLA_DV, GLA_DK), lambda i: (i, 0, 0, 0))],
        out_shape=[jax.ShapeDtypeStruct((t, GLA_HV), BF16), jax.ShapeDtypeStruct((nc, GLA_HEADS, GLA_DV, GLA_DK), F32)],
        scratch_shapes=[pltpu.VMEM((GLA_HEADS, GLA_DV, GLA_DK), F32)],
        compiler_params=_params(),
    )(proj, z, norm_g, *after)


def _gla_bwd(proj, z, norm_g, states, du, after):
    t = proj.shape[0]
    nc, per = t // CHUNK, GLA_CHUNKS_PER_STEP
    rows_per_step = per * CHUNK
    n_steps = nc // per
    after = [a for a in after if a is not None]

    def body(proj_ref, z_ref, g_ref, st_in_ref, du_ref, *rest):
        dproj_ref, dz_ref, dg_ref, dst_ref = rest[len(after):]

        @pl.when(pl.program_id(0) == 0)
        def _():
            dst_ref[...] = jnp.zeros_like(dst_ref)
            dg_ref[...] = jnp.zeros_like(dg_ref)

        g = g_ref[...]
        for h in range(GLA_HEADS):
            sq, sk, sv, sr = _gla_slices(h)
            dst = dst_ref[h]
            for c in reversed(range(per)):
                rows = slice(c * CHUNK, (c + 1) * CHUNK)
                ins = (proj_ref[rows, sq].astype(F32), proj_ref[rows, sk].astype(F32), proj_ref[rows, sv].astype(F32),
                       proj_ref[rows, sr].astype(F32), z_ref[rows, GLA_DK * h:GLA_DK * (h + 1)], g, st_in_ref[c, h])
                _, vjp = jax.vjp(_gla_head, *ins)
                dq, dk, dv, dr, dz, dg, dst = vjp((du_ref[rows, GLA_DV * h:GLA_DV * (h + 1)], dst))
                dproj_ref[rows, sq] = dq.astype(dproj_ref.dtype)
                dproj_ref[rows, sk] = dk.astype(dproj_ref.dtype)
                dproj_ref[rows, sv] = dv.astype(dproj_ref.dtype)
                dproj_ref[rows, sr] = dr.astype(dproj_ref.dtype)
                dz_ref[rows, GLA_DK * h:GLA_DK * (h + 1)] = dz
                dg_ref[...] += dg
            dst_ref[h] = dst

    rev = lambda i: (n_steps - 1 - i, 0)
    return pl.pallas_call(
        body, name="gla_bwd", grid=(n_steps,),
        in_specs=[pl.BlockSpec((rows_per_step, GLA_MAIN), rev), pl.BlockSpec((rows_per_step, GLA_HK), rev),
                  pl.BlockSpec((1, GLA_DV), lambda i: (0, 0)),
                  pl.BlockSpec((per, GLA_HEADS, GLA_DV, GLA_DK), lambda i: (n_steps - 1 - i, 0, 0, 0)),
                  pl.BlockSpec((rows_per_step, GLA_HV), rev)] + [pl.BlockSpec(memory_space=pl.ANY)] * len(after),
        out_specs=[pl.BlockSpec((rows_per_step, GLA_MAIN), rev), pl.BlockSpec((rows_per_step, GLA_HK), rev),
                   pl.BlockSpec((1, GLA_DV), lambda i: (0, 0))],
        out_shape=[jax.ShapeDtypeStruct((t, GLA_MAIN), BF16), jax.ShapeDtypeStruct((t, GLA_HK), F32),
                   jax.ShapeDtypeStruct((1, GLA_DV), F32)],
        scratch_shapes=[pltpu.VMEM((GLA_HEADS, GLA_DV, GLA_DK), F32)],
        compiler_params=_params(),
    )(proj, z, norm_g, states, du, *after)


def _mla_pre(cq, cos, sin, gq, gkv, w_uq, w_ukv):
    qlat = _rms_norm(cq[:, :MLA_RANK], gq)
    kvlat = _rms_norm(cq[:, MLA_RANK:2 * MLA_RANK], gkv)
    kr = cq[:, 2 * MLA_RANK:]
    q = mm_nn(qlat, w_uq) * ((MLA_NOPE + MLA_ROPE) ** -0.5)
    kv = mm_nn(kvlat, w_ukv)
    pieces = []
    for h in range(MLA_HEADS):
        qr = q[:, MLA_HEAD_PAD * h + MLA_NOPE:MLA_HEAD_PAD * (h + 1)]
        pieces += [q[:, MLA_HEAD_PAD * h:MLA_HEAD_PAD * h + MLA_NOPE], qr * cos + rot_half(qr) * sin]
    return jnp.concatenate(pieces, axis=1), kv, kr * cos + rot_half(kr) * sin


MLA_Q_TILE = 256


def _mla_attn_block(qn, qr, kv, kr, q0):
    tq, nk = qn.shape[0], kv.shape[0]
    s = mm_nt(qn, kv[:, :MLA_NOPE]) + mm_nt(qr, kr)
    visible = (_iota2((tq, nk), 1) // CHUNK) <= ((q0 + _iota2((tq, nk), 0)) // CHUNK)
    s = jnp.where(visible, s, -1e30)
    e = jnp.exp(s - jnp.max(s, -1, keepdims=True))
    p = e / jnp.sum(e, -1, keepdims=True)
    return mm_nn(p, kv[:, MLA_NOPE:])


def _mla_attn_fwd(q, kv, kr, after):
    t = q.shape[0]
    after = [a for a in after if a is not None]

    def body(q_ref, kv_ref, kr_ref, *rest):
        (o_ref,) = rest[len(after):]
        for i in range(t // MLA_Q_TILE):
            rows = slice(i * MLA_Q_TILE, (i + 1) * MLA_Q_TILE)
            keys = slice(0, (i + 1) * MLA_Q_TILE)
            o = _mla_attn_block(q_ref[rows, :MLA_NOPE].astype(F32), q_ref[rows, MLA_NOPE:].astype(F32),
                                kv_ref[keys, :].astype(F32), kr_ref[keys, :].astype(F32), i * MLA_Q_TILE)
            o_ref[rows, :] = o.astype(o_ref.dtype)

    return pl.pallas_call(
        body, name="mla_attn_fwd", grid=(MLA_HEADS,),
        in_specs=[pl.BlockSpec((t, MLA_HEAD_PAD), lambda h: (0, h)),
                  pl.BlockSpec((t, MLA_NOPE + MLA_V), lambda h: (0, h)), pl.BlockSpec((t, LANES), lambda h: (0, 0))]
        + [pl.BlockSpec(memory_space=pl.ANY)] * len(after),
        out_specs=pl.BlockSpec((t, MLA_V), lambda h: (0, h)),
        out_shape=jax.ShapeDtypeStruct((t, MLA_HEADS * MLA_V), BF16),
        compiler_params=_params(),
    )(q, kv, kr, *after)


def _mla_attn_bwd(q, kv, kr, do, after):
    t = q.shape[0]
    after = [a for a in after if a is not None]

    def body(q_ref, kv_ref, kr_ref, do_ref, *rest):
        dq_ref, dkv_ref, dkr_ref = rest[len(after):]
        dkv_ref[...] = jnp.zeros_like(dkv_ref)

        @pl.when(pl.program_id(0) == 0)
        def _():
            dkr_ref[...] = jnp.zeros_like(dkr_ref)

        for i in range(t // MLA_Q_TILE):
            rows = slice(i * MLA_Q_TILE, (i + 1) * MLA_Q_TILE)
            keys = slice(0, (i + 1) * MLA_Q_TILE)
            f = functools.partial(_mla_attn_block, q0=i * MLA_Q_TILE)
            _, vjp = jax.vjp(f, q_ref[rows, :MLA_NOPE].astype(F32), q_ref[rows, MLA_NOPE:].astype(F32),
                             kv_ref[keys, :].astype(F32), kr_ref[keys, :].astype(F32))
            dqn, dqr, dkv, dkr = vjp(do_ref[rows, :].astype(F32))
            dq_ref[rows, :MLA_NOPE] = dqn
            dq_ref[rows, MLA_NOPE:] = dqr
            dkv_ref[keys, :] += dkv
            dkr_ref[keys, :] += dkr

    return pl.pallas_call(
        body, name="mla_attn_bwd", grid=(MLA_HEADS,),
        in_specs=[pl.BlockSpec((t, MLA_HEAD_PAD), lambda h: (0, h)),
                  pl.BlockSpec((t, MLA_NOPE + MLA_V), lambda h: (0, h)), pl.BlockSpec((t, LANES), lambda h: (0, 0)),
                  pl.BlockSpec((t, MLA_V), lambda h: (0, h))] + [pl.BlockSpec(memory_space=pl.ANY)] * len(after),
        out_specs=[pl.BlockSpec((t, MLA_HEAD_PAD), lambda h: (0, h)),
                   pl.BlockSpec((t, MLA_NOPE + MLA_V), lambda h: (0, h)), pl.BlockSpec((t, LANES), lambda h: (0, 0))],
        out_shape=[jax.ShapeDtypeStruct(q.shape, F32), jax.ShapeDtypeStruct(kv.shape, F32),
                   jax.ShapeDtypeStruct(kr.shape, F32)],
        compiler_params=_params(),
    )(q, kv, kr, do, *after)


def _rope_tables(pos_col, inv_freq_row):
    t = pos_col.shape[0]

    def body(pos_ref, f_ref, cos_ref, sin_ref):
        ang = pos_ref[...].astype(F32) * f_ref[...]
        live = _iota2(ang.shape, 1) < MLA_ROPE
        cos_ref[...] = jnp.where(live, jnp.cos(ang), 0.0)
        sin_ref[...] = jnp.where(live, jnp.sin(ang), 0.0)

    return pl.pallas_call(
        body, name="rope_tables", out_shape=[jax.ShapeDtypeStruct((t, LANES), F32)] * 2, compiler_params=_params(),
    )(pos_col, inv_freq_row)


CONV_COL_TILE = 256


def _conv_gate(b, c, u, w0, w1, w2):
    cu = c * u
    return b * (w2 * cu + w1 * shift_rows(cu, 1) + w0 * shift_rows(cu, 2))


def _conv_specs(t):
    nb = D_MODEL // CONV_COL_TILE
    return [pl.BlockSpec((t, CONV_COL_TILE), lambda j, part=part: (0, part * nb + j)) for part in range(3)]


def _conv_fwd(bcu, w, after):
    t = bcu.shape[0]
    after = [a for a in after if a is not None]

    def body(b_ref, c_ref, u_ref, w_ref, *rest):
        (o_ref,) = rest[len(after):]
        o_ref[...] = _conv_gate(b_ref[...], c_ref[...], u_ref[...], w_ref[0:1, :], w_ref[1:2, :],
                                w_ref[2:3, :]).astype(o_ref.dtype)

    return pl.pallas_call(
        body, name="conv_fwd", grid=(D_MODEL // CONV_COL_TILE,),
        in_specs=_conv_specs(t) + [pl.BlockSpec((3, CONV_COL_TILE), lambda j: (0, j))]
        + [pl.BlockSpec(memory_space=pl.ANY)] * len(after),
        out_specs=pl.BlockSpec((t, CONV_COL_TILE), lambda j: (0, j)),
        out_shape=jax.ShapeDtypeStruct((t, D_MODEL), BF16), compiler_params=_params(),
    )(bcu, bcu, bcu, w, *after)


def _conv_bwd(bcu, w, dout, after):
    t = bcu.shape[0]
    after = [a for a in after if a is not None]

    def body(b_ref, c_ref, u_ref, w_ref, do_ref, *rest):
        db_ref, dc_ref, du_ref, dw_ref = rest[len(after):]
        _, vjp = jax.vjp(_conv_gate, b_ref[...], c_ref[...], u_ref[...], w_ref[0:1, :], w_ref[1:2, :], w_ref[2:3, :])
        db, dc, du, dw0, dw1, dw2 = vjp(do_ref[...])
        db_ref[...] = db.astype(db_ref.dtype)
        dc_ref[...] = dc.astype(dc_ref.dtype)
        du_ref[...] = du.astype(du_ref.dtype)
        dw_ref[0:1, :] = dw0
        dw_ref[1:2, :] = dw1
        dw_ref[2:3, :] = dw2

    col = pl.BlockSpec((t, CONV_COL_TILE), lambda j: (0, j))
    return pl.pallas_call(
        body, name="conv_bwd", grid=(D_MODEL // CONV_COL_TILE,),
        in_specs=_conv_specs(t) + [pl.BlockSpec((3, CONV_COL_TILE), lambda j: (0, j)), col]
        + [pl.BlockSpec(memory_space=pl.ANY)] * len(after),
        out_specs=[col, col, col, pl.BlockSpec((3, CONV_COL_TILE), lambda j: (0, j))],
        out_shape=[jax.ShapeDtypeStruct((t, D_MODEL), BF16)] * 3 + [jax.ShapeDtypeStruct((3, D_MODEL), F32)],
        compiler_params=_params(),
    )(bcu, bcu, bcu, w, dout, *after)


def _loss_head(y, target):
    t, d = y.shape
    tm = 256

    def body(y_ref, t_ref, loss_ref, dy_ref):
        @pl.when(pl.program_id(0) == 0)
        def _():
            loss_ref[...] = jnp.zeros_like(loss_ref)

        err = y_ref[...] - t_ref[...]
        dy_ref[...] = err * (1.0 / d)
        loss_ref[...] += 0.5 * jnp.sum(jnp.sum(err * err, axis=-1, keepdims=True) * (1.0 / d))

    tile = pl.BlockSpec((tm, d), lambda i: (i, 0))
    return pl.pallas_call(
        body, name="loss_head", grid=(t // tm,), in_specs=[tile, tile],
        out_specs=[pl.BlockSpec((8, LANES), lambda i: (0, 0)), tile],
        out_shape=[jax.ShapeDtypeStruct((8, LANES), F32), jax.ShapeDtypeStruct((t, d), F32)],
        compiler_params=_params(),
    )(y, target)


def _ln_epi(acc, res, g, b):
    a = ALPHA * res + acc
    y = _layer_norm(a, g, b)
    return a, y, y


def _ln_fn(a, g, b):
    return (_layer_norm(a, g, b),)


def _ln_bwd_epi(scale):
    def epi(acc, res, a, g, b):
        _, vjp = jax.vjp(_ln_fn, a, g, b)
        da, dg, db = vjp((acc + scale * res,))
        return da, da, dg, db
    return epi


def _relu_sq(h):
    r = jnp.maximum(h.astype(F32), 0.0)
    return r * r


def _pad_cols(w, n):
    return jnp.pad(w, ((0, 0), (0, n - w.shape[1])))


def _pad_rows(w, n):
    return jnp.pad(w, ((0, n - w.shape[0]), (0, 0)))


def _step(x, p, positions, target, small, comm):
    t = x.shape[0]
    w = small
    freqs = ROPE_BASE ** (-jnp.arange(0, MLA_ROPE // 2, dtype=F32) * (2.0 / MLA_ROPE))
    freq_row = jnp.concatenate([freqs, freqs, jnp.zeros((LANES - MLA_ROPE,), F32)])[None, :]
    cos, sin = _rope_tables(positions.reshape(t, 1), freq_row)

    saved = []
    xb = x.astype(BF16)
    for i in range(DEPTH):
        j, kind = i // 3, i % 3
        wl = comm.mixer_weights(i)
        s = {"x": xb, "wl": wl}
        tok = comm.at("fwd", i, "begin", x)
        if kind == 0:
            s["w_main"] = wl["gla_w_in_t"][:GLA_MAIN]
            s["w_lr"] = _pad_rows(wl["gla_w_in_t"][GLA_MAIN:], LANES)
            s["w_up"] = _pad_rows(w["gla_w_gate_up"][j], LANES).astype(BF16)
            s["proj"] = _matmul(xb, s["w_main"], name="gla_proj", tb=True, tn=1024, after=tok)
            s["glr"] = _matmul(xb, s["w_lr"], name="gla_lr", tb=True, out_dtypes=(F32,))
            s["z"] = _matmul(s["glr"], s["w_up"], name="gla_gate", epi=lambda acc, b: (acc + b,),
                             epi_ins=(w["gla_b_gate"][j][None, :],), out_dtypes=(F32,))
            tok = comm.at("fwd", i, "proj_done", s["z"]) or []
            s["u"], s["states"] = _gla_fwd(s["proj"], s["z"], w["gla_norm_g"][j][None, :], tok)
        elif kind == 1:
            s["cq"] = _matmul(xb, wl["mla_in"], name="mla_proj", tn=MLA_IN_PAD, b_at=REG_MLA_IN, out_dtypes=(F32,),
                              after=tok)
            s["pre_params"] = (w["mla_q_norm"][j][None, :], w["mla_kv_norm"][j][None, :], wl["mla_w_uq"], wl["mla_w_ukv"])
            s["q"], s["kv"], s["kr"] = _tile_fwd(_mla_pre, (s["cq"], cos, sin), s["pre_params"], (BF16, BF16, BF16),
                                                 tm=256, name="mla_pre_fwd")
            tok = comm.at("fwd", i, "proj_done", s["kv"]) or []
            s["u"] = _mla_attn_fwd(s["q"], s["kv"], s["kr"], tok)
        else:
            s["bcu"] = _matmul(xb, wl["conv"], name="conv_proj", tb=True, tm=256, tn=3 * D_MODEL, b_at=REG_CONV,
                               out_dtypes=(F32,), after=tok)
            tok = comm.at("fwd", i, "proj_done", s["bcu"]) or []
            s["u"] = _conv_fwd(s["bcu"], w["conv_w"][j], tok)
        g0, b0 = w["ln_g"][i, 0][None, :], w["ln_b"][i, 0][None, :]
        g1, b1 = w["ln_g"][i, 1][None, :], w["ln_b"][i, 1][None, :]
        wa, wb = s["wa"], _ = comm.slab_weights(i, s["u"])
        s["a1"], s["x1"], s["x1b"] = _matmul(s["u"], wa, name="mixer_out_ln", tm=256, tn=D_MODEL, b_at=REG_WOUT,
                                             epi=_ln_epi, epi_ins=(x, g0, b0), out_dtypes=(F32, F32, BF16))
        s["hh"] = _matmul(s["x1b"], wa, name="mlp_up", tb=True, tm=256, tn=D_FF, b_at=REG_W1T)
        tok = comm.at("fwd", i, "mid", s["hh"])
        s["a2"], s["x2"], s["x2b"] = _matmul(s["hh"], wa, name="mlp_down_ln", tm=256, tn=D_MODEL, b_at=REG_W2,
                                             a_fn=_relu_sq, epi=_ln_epi, epi_ins=(s["x1"], g1, b1),
                                             out_dtypes=(F32, F32, BF16), after=tok)
        s["pp"] = _matmul(p[i], wb, name="ple_proj", tb=True, tn=D_MODEL, b_at=REG_WPT)
        tok = comm.at("fwd", i, "end", s["pp"])
        def ple_epi(acc, xr, pp):
            y = xr + jax.nn.sigmoid(acc) * pp.astype(F32)
            return y, y, acc

        x, xb, s["gt"] = _matmul(s["x2b"], wa, name="ple_gate", tn=1024, b_at=REG_WG, epi=ple_epi,
                                 epi_ins=(s["x2"], s["pp"]), out_dtypes=(F32, BF16, BF16), after=tok)
        saved.append(s)

    loss_part, dx = _loss_head(x, target)

    gw = {n: [None] * WEIGHTS[n][0][0] for n in SMALL + REPLICATED}
    ln_g_grads, ln_b_grads = [[None, None] for _ in range(DEPTH)], [[None, None] for _ in range(DEPTH)]
    resid = lambda acc, r: (acc + ALPHA * r,)
    plus = lambda acc, r: (acc + r,)
    for i in reversed(range(DEPTH)):
        j, kind = i // 3, i % 3
        s = saved[i]
        wa = s["wa"]
        ga = lax.empty((N_DEV, A_ROWS, D_MODEL), BF16)
        gb = lax.empty((N_DEV, REG_WPT[1], PLE_DIM), BF16)
        layer_grads = {}
        tok = comm.at("bwd", i, "begin", dx)

        def ple_bwd(dxo, gt, pp):
            sg = jax.nn.sigmoid(gt)
            return dxo * sg, dxo * pp * sg * (1.0 - sg)

        d_pp, d_gt = _tile_fwd(ple_bwd, (dx, s["gt"], s["pp"]), (), (BF16, BF16), tm=256, name="ple_bwd")
        gb = _matmul(d_pp, p[i], name="ple_proj_dw", ta=True, tm=512, tn=PLE_DIM, out_at=REG_WPT, out_buf=gb, after=tok)
        ga = _matmul(s["x2b"], d_gt, name="ple_gate_dw", ta=True, tm=512, tn=1024, out_at=REG_WG, out_buf=ga)
        g1, b1 = w["ln_g"][i, 1][None, :], w["ln_b"][i, 1][None, :]
        d_a2, d_a2b, ln_g_grads[i][1], ln_b_grads[i][1] = _matmul(
            d_gt, wa, name="ple_gate_dx_ln", tb=True, tm=256, tn=D_MODEL, b_at=REG_WG, epi=_ln_bwd_epi(1.0),
            epi_ins=(dx, s["a2"], g1, b1), out_dtypes=(F32, BF16), n_row_sums=2, after=[ga, gb])
        tok = comm.at("bwd", i, "ln", d_a2)
        ga = _matmul(s["hh"], d_a2b, name="mlp_down_dw", ta=True, tm=REG_W2[1], tn=1024, a_fn=_relu_sq, out_at=REG_W2,
                     out_buf=ga, after=tok)
        d_hh = _matmul(d_a2b, wa, name="mlp_down_dx", tb=True, tm=256, tn=D_FF, b_at=REG_W2, after=ga,
                       epi=lambda acc, hh: (acc * 2.0 * jnp.maximum(hh.astype(F32), 0.0),), epi_ins=(s["hh"],))
        ga = _matmul(d_hh, s["x1b"], name="mlp_up_dw", ta=True, tm=REG_W1T[1], tn=1024, out_at=REG_W1T, out_buf=ga)
        g0, b0 = w["ln_g"][i, 0][None, :], w["ln_b"][i, 0][None, :]
        d_a1, d_a1b, ln_g_grads[i][0], ln_b_grads[i][0] = _matmul(
            d_hh, wa, name="mlp_up_dx_ln", tm=256, tn=D_MODEL, b_at=REG_W1T, epi=_ln_bwd_epi(ALPHA),
            epi_ins=(d_a2, s["a1"], g0, b0), out_dtypes=(F32, BF16), n_row_sums=2, after=ga)
        ga = _matmul(s["u"], d_a1b, name="mixer_out_dw", ta=True, tm=512, tn=1024, out_at=REG_WOUT, out_buf=ga)
        du = _matmul(d_a1b, wa, name="mixer_out_dx", tb=True, tn=1024, b_at=REG_WOUT, out_dtypes=(F32,), after=ga)
        comm.slab_grads(i, ga, gb)
        tok = comm.at("bwd", i, "slab_done", du) or []
        if kind == 0:
            dproj, dz, dg = _gla_bwd(s["proj"], s["z"], w["gla_norm_g"][j][None, :], s["states"], du, tok)
            tok = comm.at("bwd", i, "mixer_done", dproj)
            gw["gla_norm_g"][j] = dg[0]
            gw["gla_b_gate"][j] = _tile_bwd(lambda zz, b: (zz + b,), (s["z"],), (w["gla_b_gate"][j][None, :],), (dz,), (),
                                            tm=256, name="gla_bias_bwd", diff_tiled=[])[1][0][0]
            gw["gla_w_gate_up"][j] = _matmul(s["glr"], dz, name="gla_gate_dw", ta=True, out_dtypes=(F32,),
                                             after=tok)[:GLA_RANK]
            dglr = _matmul(dz, s["w_up"], name="gla_gate_dx", tb=True, out_dtypes=(F32,))
            dw_main = _matmul(dproj, s["x"], name="gla_proj_dw", ta=True, tn=1024, out_dtypes=(F32,))
            dw_lr = _matmul(dglr, s["x"], name="gla_lr_dw", ta=True, tn=1024, out_dtypes=(F32,))[:GLA_RANK]
            layer_grads["gla_w_in_t"] = jnp.concatenate([dw_main, dw_lr], axis=0)
            dx = _matmul(dproj, s["w_main"], name="gla_proj_dx", tn=1024, epi=resid, epi_ins=(d_a1,),
                         out_dtypes=(F32,), after=[dw_main, dw_lr, gw["gla_w_gate_up"][j]])
            dx = _matmul(dglr, s["w_lr"], name="gla_lr_dx", tn=1024, epi=plus, epi_ins=(dx,), out_dtypes=(F32,))
        elif kind == 1:
            dq, dkv, dkr = _mla_attn_bwd(s["q"], s["kv"], s["kr"], du, tok)
            tok = comm.at("bwd", i, "mixer_done", dq)
            (d_cq,), (dgq, dgkv, layer_grads["mla_uq"], layer_grads["mla_ukv"]) = _tile_bwd(
                _mla_pre, (s["cq"], cos, sin), s["pre_params"], (dq, dkv, dkr), (BF16,), tm=256, name="mla_pre_bwd",
                diff_tiled=[0])
            gw["mla_q_norm"][j], gw["mla_kv_norm"][j] = dgq[0], dgkv[0]
            layer_grads["mla_in"] = _matmul(s["x"], d_cq, name="mla_proj_dw", ta=True, tm=512, tn=MLA_IN_PAD,
                                            out_at=REG_MLA_IN, after=tok,
                                            out_buf=lax.empty((N_DEV, REG_MLA_IN[1], MLA_IN_PAD), BF16))
            dx = _matmul(d_cq, s["wl"]["mla_in"], name="mla_proj_dx", tb=True, tn=1024, b_at=REG_MLA_IN, epi=resid,
                         epi_ins=(d_a1,), out_dtypes=(F32,), after=layer_grads["mla_in"])
        else:
            db, dc, du_, dcw = _conv_bwd(s["bcu"], w["conv_w"][j], du, tok)
            tok = comm.at("bwd", i, "mixer_done", db)
            gw["conv_w"][j] = dcw
            dbcu = jnp.concatenate([db, dc, du_], axis=1)
            layer_grads["conv"] = _matmul(dbcu, s["x"], name="conv_proj_dw", ta=True, tm=REG_CONV[1], tn=1024,
                                          out_at=REG_CONV, out_buf=lax.empty((N_DEV, REG_CONV[1], D_MODEL), BF16),
                                          after=tok)
            dx = _matmul(dbcu, s["wl"]["conv"], name="conv_proj_dx", tn=1024, b_at=REG_CONV, epi=resid, epi_ins=(d_a1,),
                         out_dtypes=(F32,), after=layer_grads["conv"])
        comm.mixer_grads(i, layer_grads)

    gw["ln_g"] = [jnp.concatenate([a, b], axis=0) for a, b in ln_g_grads]
    gw["ln_b"] = [jnp.concatenate([a, b], axis=0) for a, b in ln_b_grads]
    return loss_part, dx, {n: jnp.stack(gw[n]).astype(F32) for n in gw}


MESH_IDS = pl.DeviceIdType.MESH
ANY = pl.BlockSpec(memory_space=pl.ANY)
HBM_SPEC = pl.BlockSpec(memory_space=pltpu.HBM)
SEM_SPEC = pl.BlockSpec(memory_space=pltpu.SEMAPHORE)
DATAFLOW_EFFECT = pltpu.SideEffectType.DATAFLOW_SIDE_EFFECTING
CORE_COPIES, CHIP_COPIES = 4, 3


def _my_place():
    return lax.axis_index("x"), lax.axis_index("y"), lax.axis_index("c")


def _other_chips(mx, my):
    return [(1 - mx, my), (mx, 1 - my), (1 - mx, 1 - my)]


def _remote(src, dst, send_sems, recv_sems, k, to):
    return pltpu.make_async_remote_copy(src_ref=src, dst_ref=dst, send_sem=send_sems.at[k], recv_sem=recv_sems.at[k],
                                        device_id=to, device_id_type=MESH_IDS)


def _gather_first_copies(n_arr):
    def make(bufs, send_sems, recv_sems):
        mx, my, mc = _my_place()
        mine = 4 * mx + 2 * my + mc
        peers = [(mx, my, 1 - mc)] + [(cx, cy, mc) for cx, cy in _other_chips(mx, my)]
        return [_remote(bufs[a].at[mine], bufs[a].at[mine], send_sems, recv_sems, (1 + CHIP_COPIES) * a + k, to)
                for a in range(n_arr) for k, to in enumerate(peers)]
    return make, (1 + CHIP_COPIES) * n_arr


def _gather_forward_copies(n_arr):
    def make(bufs, send_sems, recv_sems):
        mx, my, mc = _my_place()
        blocks = [4 * cx + 2 * cy + mc for cx, cy in _other_chips(mx, my)]
        return [_remote(bufs[a].at[blk], bufs[a].at[blk], send_sems, recv_sems, CHIP_COPIES * a + k, (mx, my, 1 - mc))
                for a in range(n_arr) for k, blk in enumerate(blocks)]
    return make, CHIP_COPIES * n_arr


def _scatter_core_copies(n_arr):
    def make(bufs, send_sems, recv_sems):
        mx, my, mc = _my_place()
        return [_remote(bufs[a].at[2 * k + (1 - mc)], bufs[n_arr + a].at[k], send_sems, recv_sems, CORE_COPIES * a + k,
                        (mx, my, 1 - mc)) for a in range(n_arr) for k in range(CORE_COPIES)]
    return make, CORE_COPIES * n_arr


def _scatter_chip_copies(n_arr):
    def make(bufs, send_sems, recv_sems):
        mx, my, mc = _my_place()
        return [_remote(bufs[a].at[2 * cx + cy], bufs[n_arr + a].at[k], send_sems, recv_sems, CHIP_COPIES * a + k,
                        (cx, cy, mc)) for a in range(n_arr) for k, (cx, cy) in enumerate(_other_chips(mx, my))]
    return make, CHIP_COPIES * n_arr


def _exchange(name, bufs, copies):
    make, n_copies = copies
    n = len(bufs)

    def body(*refs):
        descs = make(refs[:n], refs[2 * n], refs[2 * n + 1])
        for cp in descs:
            cp.start()
        for cp in descs:
            cp.wait()

    return pl.pallas_call(
        body, name=name, out_shape=[jax.ShapeDtypeStruct(b.shape, b.dtype) for b in bufs], in_specs=[ANY] * n,
        out_specs=[ANY] * n, input_output_aliases={i: i for i in range(n)},
        scratch_shapes=[pltpu.SemaphoreType.DMA((n_copies,)), pltpu.SemaphoreType.DMA((n_copies,))],
    )(*bufs)


def _exchange_start(name, bufs, copies, after):
    make, n_copies = copies
    n = len(bufs)

    def body(*refs):
        for cp in make(refs[:n], refs[n + 1], refs[n + 2]):
            cp.start()
        refs[-1][...] = jnp.zeros_like(refs[-1])

    outs = pl.pallas_call(
        body, name=name,
        out_shape=(pltpu.SemaphoreType.DMA((n_copies,)), pltpu.SemaphoreType.DMA((n_copies,)),
                   *[pltpu.HBM(b.shape, b.dtype) for b in bufs], jax.ShapeDtypeStruct((8, LANES), F32)),
        in_specs=[HBM_SPEC] * n + [ANY],
        out_specs=(SEM_SPEC, SEM_SPEC, *[HBM_SPEC] * n, pl.BlockSpec(memory_space=pltpu.VMEM)),
        input_output_aliases={i: 2 + i for i in range(n)},
        compiler_params=pltpu.CompilerParams(has_side_effects=DATAFLOW_EFFECT),
    )(*[pltpu.with_memory_space_constraint(b, pltpu.HBM) for b in bufs], after)
    return (outs[0], outs[1]), list(outs[2:2 + n]), outs[-1]


def _exchange_wait(name, sems, bufs, copies, after):
    make, _ = copies
    n = len(bufs)

    def body(*refs):
        for cp in make(refs[:n], refs[n], refs[n + 1]):
            cp.wait_send()
            cp.wait_recv()

    return list(pl.pallas_call(
        body, name=name, out_shape=[pltpu.HBM(b.shape, b.dtype) for b in bufs],
        in_specs=[HBM_SPEC] * n + [SEM_SPEC, SEM_SPEC, ANY], out_specs=[HBM_SPEC] * n,
        input_output_aliases={i: i for i in range(n)},
        compiler_params=pltpu.CompilerParams(has_side_effects=DATAFLOW_EFFECT),
    )(*bufs, *sems, after))


SUM_TILE_BYTES = 2 * 1024 * 1024


def _row_tile(r, c):
    best = None
    for cand in range(16, r + 1, 16):
        if r % cand == 0 and cand * c * 2 <= SUM_TILE_BYTES:
            best = cand
    return r if best is None else best


def _pair_sum(g, recv, my_c):
    _, r, c = g.shape
    tr = _row_tile(r, c)

    def body(c_ref, g_ref, r_ref, o_ref):
        o_ref[...] = (g_ref[...].astype(F32) + r_ref[...].astype(F32)).astype(o_ref.dtype)

    return pl.pallas_call(
        body, name="rs_pair_sum", out_shape=jax.ShapeDtypeStruct((4, r, c), g.dtype),
        grid_spec=pltpu.PrefetchScalarGridSpec(
            num_scalar_prefetch=1, grid=(4, r // tr),
            in_specs=[pl.BlockSpec((1, tr, c), lambda n, i, cr: (2 * n + cr[0], i, 0)),
                      pl.BlockSpec((1, tr, c), lambda n, i, cr: (n, i, 0))],
            out_specs=pl.BlockSpec((1, tr, c), lambda n, i, cr: (n, i, 0))),
        compiler_params=_params(),
    )(my_c, g, recv)


def _chip_sum(h, recv, my_chip):
    _, r, c = h.shape
    tr = _row_tile(r, c)

    def body(j_ref, h_ref, r0_ref, r1_ref, r2_ref, o_ref):
        o_ref[...] = ((h_ref[0].astype(F32) + r0_ref[0].astype(F32)) + r1_ref[0].astype(F32)) + r2_ref[0].astype(F32)

    return pl.pallas_call(
        body, name="rs_chip_sum", out_shape=jax.ShapeDtypeStruct((r, c), F32),
        grid_spec=pltpu.PrefetchScalarGridSpec(
            num_scalar_prefetch=1, grid=(r // tr,),
            in_specs=[pl.BlockSpec((1, tr, c), lambda i, jr: (jr[0], i, 0))]
            + [pl.BlockSpec((1, tr, c), lambda i, jr, n=n: (n, i, 0)) for n in range(3)],
            out_specs=pl.BlockSpec((tr, c), lambda i, jr: (i, 0))),
        compiler_params=_params(),
    )(my_chip, h, recv, recv, recv)


def _sum_blocks(g):
    n, r, c = g.shape

    def body(g_ref, o_ref):
        acc = g_ref[0]
        for k in range(1, n):
            acc = acc + g_ref[k]
        o_ref[...] = acc

    return pl.pallas_call(body, name="sum_blocks", out_shape=jax.ShapeDtypeStruct((r, c), F32), compiler_params=_params())(g)


def _pack(flat_parts, cols, row_multiple, dtype):
    flat = jnp.concatenate([f.astype(dtype) for f in flat_parts])
    per_row_block = cols * row_multiple
    padded = -(-flat.shape[0] // per_row_block) * per_row_block
    return jnp.pad(flat, (0, padded - flat.shape[0])).reshape(padded // cols, cols)


def _shard_shape(name):
    shape, axis = WEIGHTS[name]
    if axis is None:
        return shape
    return tuple(s // N_DEV if a == axis else s for a, s in enumerate(shape))


def _size(shape):
    n = 1
    for s in shape:
        n *= s
    return n


def _unshard(blocks, name):
    _, axis = WEIGHTS[name]
    return jnp.concatenate([blocks[k] for k in range(N_DEV)], axis=axis)


def _unpack_blocks(flat, names, lead):
    out, off = {}, 0
    for n in names:
        shp = _shard_shape(n)[1:] if lead else _shard_shape(n)
        out[n] = flat[..., off:off + _size(shp)].reshape(flat.shape[:-1] + shp)
        off += _size(shp)
    return out


def _layer_slabs(shard, i):
    j, kind = i // 3, i % 3
    w_out = (shard["gla_w_out"], shard["mla_w_out"], shard["conv_w_out"])[kind][j]
    out = {"a": jnp.concatenate([shard["mlp_w2"][i], shard["mlp_w1"][i].T, w_out, shard["ple_w_gate"][i]], axis=0).astype(BF16),
           "b": shard["ple_w_proj"][i].T.astype(BF16)}
    if kind == 0:
        out["gla"] = shard["gla_w_in"][j].T.astype(BF16)
    elif kind == 1:
        out["mla_in"] = _pad_cols(shard["mla_w_in"][j], MLA_IN_PAD).astype(BF16)
        out["mla_uq"] = _pad_cols(shard["mla_w_uq"][j], MLA_HEAD_PAD).astype(BF16)
        out["mla_ukv"] = shard["mla_w_ukv"][j].astype(BF16)
    else:
        out["conv"] = shard["conv_w_in"][j].T.astype(BF16)
    return out


def _mixer_weights(landed, i):
    kind = i % 3
    if kind == 0:
        return {"gla_w_in_t": landed["gla"].reshape(-1, D_MODEL)}
    if kind == 2:
        return {"conv": landed["conv"]}
    heads_side_by_side = lambda g: g.transpose(1, 0, 2).reshape(g.shape[1], -1)
    return {"mla_in": landed["mla_in"], "mla_w_uq": heads_side_by_side(landed["mla_uq"]),
            "mla_w_ukv": heads_side_by_side(landed["mla_ukv"])}


def _mixer_grad_buffers(layer_grads, i):
    kind = i % 3
    if kind == 0:
        return {"gla": layer_grads["gla_w_in_t"].reshape(N_DEV, -1, D_MODEL).astype(BF16)}
    if kind == 2:
        return {"conv": layer_grads["conv"]}
    head_blocks = lambda g: g.reshape(g.shape[0], N_DEV, -1).transpose(1, 0, 2).astype(BF16)
    return {"mla_in": layer_grads["mla_in"], "mla_uq": head_blocks(layer_grads["mla_uq"]),
            "mla_ukv": head_blocks(layer_grads["mla_ukv"])}


SLAB_KEYS = ("a", "b")


class _Overlap:
    def __init__(self, shard, small_pack):
        mx, my, mc = _my_place()
        self.my_c = mc.astype(jnp.int32).reshape(1)
        self.my_chip = (2 * mx + my).astype(jnp.int32).reshape(1)
        mine = 4 * mx + 2 * my + mc
        slabs = [_layer_slabs(shard, i) for i in range(DEPTH)]
        slabs[0]["small"] = small_pack
        self.landing = [{k: lax.dynamic_update_index_in_dim(lax.empty((N_DEV, *v.shape), v.dtype), v, mine, 0)
                         for k, v in slabs[i].items()} for i in range(DEPTH)]
        self.fly = {}
        self.grads = [{} for _ in range(DEPTH)]
        self.reduced = [{} for _ in range(DEPTH)]
        tok = self._gather_first(0, "mixer", shard["ln_g"])
        tok = self._gather_first(0, "slab", tok)
        bufs = self._wait("ag_first_mixer_l0", tok)
        self.landing[0].update(zip(self._keys(self.landing[0], "mixer"),
                                   _exchange("ag_forward_mixer_l0", bufs, _gather_forward_copies(len(bufs)))))

    @staticmethod
    def _keys(names, group):
        return [k for k in names if (k in SLAB_KEYS) == (group == "slab")]

    def _start(self, name, bufs, copies, after):
        sems, bufs, tok = _exchange_start(name + "_start", bufs, copies, after)
        self.fly[name] = (sems, bufs, copies)
        return tok

    def _wait(self, name, after):
        sems, bufs, copies = self.fly.pop(name)
        return _exchange_wait(name + "_wait", sems, bufs, copies, after)

    def mixer_weights(self, i):
        return _mixer_weights(self.landing[i], i)

    def slab_weights(self, i, dep):
        self._gather_done(i, "slab", dep)
        return self.landing[i]["a"], self.landing[i]["b"]

    def slab_grads(self, i, ga, gb):
        self.grads[i].update(a=ga, b=gb)

    def mixer_grads(self, i, layer_grads):
        self.grads[i].update(_mixer_grad_buffers(layer_grads, i))

    def at(self, phase, i, point, dep):
        toks = []
        if phase == "fwd":
            if point == "begin" and i == 0:
                toks.append(self._gather_first(1, "mixer", self.landing[0][self._keys(self.landing[0], "mixer")[0]]))
                toks.append(self._gather_first(1, "slab", toks[-1]))
            if point == "proj_done":
                toks.append(self._gather_forward(i, "slab", dep))
            if point == "mid" and i + 1 < DEPTH:
                toks.append(self._gather_forward(i + 1, "mixer", dep))
                if i + 2 < DEPTH:
                    toks.append(self._gather_first(i + 2, "mixer", dep))
                    toks.append(self._gather_first(i + 2, "slab", toks[-1]))
            if point == "end" and i + 1 < DEPTH:
                self._gather_done(i + 1, "mixer", dep)
        else:
            if point == "begin" and i + 1 < DEPTH:
                toks.append(self._scatter_cores(i + 1, "mixer", dep))
            if point == "ln" and i + 1 < DEPTH:
                toks.append(self._scatter_chips(i + 1, "mixer", dep))
            if point == "slab_done":
                if i + 1 < DEPTH:
                    self._scatter_done(i + 1, "slab", dep)
                    self._scatter_done(i + 1, "mixer", dep)
                toks.append(self._scatter_cores(i, "slab", dep))
            if point == "mixer_done":
                toks.append(self._scatter_chips(i, "slab", dep))
        return toks or None

    def _gather_first(self, i, group, after):
        bufs = [self.landing[i][k] for k in self._keys(self.landing[i], group)]
        return self._start(f"ag_first_{group}_l{i}", bufs, _gather_first_copies(len(bufs)), after)

    def _gather_forward(self, i, group, after):
        bufs = self._wait(f"ag_first_{group}_l{i}", after)
        return self._start(f"ag_forward_{group}_l{i}", bufs, _gather_forward_copies(len(bufs)), after)

    def _gather_done(self, i, group, after):
        keys = self._keys(self.landing[i], group)
        self.landing[i].update(zip(keys, self._wait(f"ag_forward_{group}_l{i}", after)))

    def _scatter_cores(self, i, group, after):
        gs = [self.grads[i][k] for k in self._keys(self.grads[i], group)]
        land = [lax.empty((4, *g.shape[1:]), g.dtype) for g in gs]
        return self._start(f"rs_cores_{group}_l{i}", gs + land, _scatter_core_copies(len(gs)), after)

    def _pair_sums(self, bufs):
        n = len(bufs) // 2
        hs = [_pair_sum(g, r, self.my_c) for g, r in zip(bufs[:n], bufs[n:])]
        return hs + [lax.empty((3, *h.shape[1:]), h.dtype) for h in hs]

    def _scatter_chips(self, i, group, after):
        bufs = self._pair_sums(self._wait(f"rs_cores_{group}_l{i}", after))
        return self._start(f"rs_chips_{group}_l{i}", bufs, _scatter_chip_copies(len(bufs) // 2), after)

    def _chip_sums(self, i, group, bufs):
        n = len(bufs) // 2
        for k, h, r in zip(self._keys(self.grads[i], group), bufs[:n], bufs[n:]):
            self.reduced[i][k] = _chip_sum(h, r, self.my_chip)

    def _scatter_done(self, i, group, after):
        self._chip_sums(i, group, self._wait(f"rs_chips_{group}_l{i}", after))

    def tail_begin(self, dep):
        return self._scatter_cores(0, "mixer", dep)

    def tail_middle(self, dep):
        self._scatter_done(0, "slab", dep)
        return self._scatter_chips(0, "mixer", dep)

    def tail_end(self, dep):
        self._scatter_done(0, "mixer", dep)


def _small_gather_start(x, name, after):
    mx, my, mc = _my_place()
    land = lax.dynamic_update_index_in_dim(lax.empty((N_DEV, *x.shape), x.dtype), x, 4 * mx + 2 * my + mc, 0)
    return name, _exchange_start(name + "_first_start", [land], _gather_first_copies(1), after)


def _small_gather_finish(started, after):
    name, (sems, bufs, _) = started
    bufs = _exchange_wait(name + "_first_wait", sems, bufs, _gather_first_copies(1), after)
    return _exchange(name + "_forward", bufs, _gather_forward_copies(1))[0]


def _adamw_math(w, g, m, v):
    m2 = ADAM_B1 * m + (1.0 - ADAM_B1) * g
    v2 = ADAM_B2 * v + (1.0 - ADAM_B2) * (g * g)
    m_hat = m2 / (1.0 - ADAM_B1 ** ADAM_STEP)
    v_hat = v2 / (1.0 - ADAM_B2 ** ADAM_STEP)
    return -ADAM_LR * (m_hat / (jnp.sqrt(v_hat) + ADAM_EPS) + ADAM_WD * w), m2, v2


ADAMW_TILE_BYTES = 1024 * 1024


def _adamw_layer(name, w, m, v, j, g, g_at, transposed, chain, after):
    n_layers, r, c = w.shape
    tr = max(t for t in range(8, r + 1, 8) if r % t == 0 and (t * c * 4 <= ADAMW_TILE_BYTES or t == 8))
    rb, rows = g_at
    if transposed:
        assert rows == c and g.shape[1] == r, (name, g.shape, g_at)
        g_spec = pl.BlockSpec((rows, tr), lambda i: (rb, i))
    else:
        assert rows == r and g.shape[1] == c, (name, g.shape, g_at)
        g_spec = pl.BlockSpec((tr, c), lambda i: (rb * (r // tr) + i, 0))
    extra = list(chain or []) + [a for a in (after or []) if a is not None]
    n_chain = 4 if chain else 0

    def body(w_ref, m_ref, v_ref, g_ref, *rest):
        g_out, d_out, m_out, v_out, tok_ref = rest[len(extra):]
        gv = g_ref[...].T if transposed else g_ref[...]
        g_out[0] = gv
        d_out[0], m_out[0], v_out[0] = _adamw_math(w_ref[0], gv, m_ref[0], v_ref[0])
        tok_ref[...] = jnp.zeros_like(tok_ref)

    layer_spec = pl.BlockSpec((1, tr, c), lambda i: (j, i, 0))
    outs = pl.pallas_call(
        body, name=f"adamw_{name}_l{j}", grid=(r // tr,),
        in_specs=[layer_spec] * 3 + [g_spec] + [pl.BlockSpec(memory_space=pl.ANY)] * len(extra),
        out_specs=[layer_spec] * 4 + [pl.BlockSpec((8, LANES), lambda i: (0, 0))],
        out_shape=[jax.ShapeDtypeStruct(w.shape, F32)] * 4 + [jax.ShapeDtypeStruct((8, LANES), F32)],
        input_output_aliases={4 + k: k for k in range(n_chain)}, compiler_params=_params(),
    )(w, m, v, g, *extra)
    return list(outs[:4]), outs[4]


def _adamw(w, g, m, v, name):
    shape = w.shape
    cols = shape[-1]
    rows = _size(shape) // cols
    tr = rows
    for cand in (512, 256, 128, 64, 32, 16, 8):
        if rows > cand and rows % cand == 0:
            tr = cand
            break

    def body(w_ref, g_ref, m_ref, v_ref, d_ref, mo_ref, vo_ref):
        d_ref[...], mo_ref[...], vo_ref[...] = _adamw_math(w_ref[...], g_ref[...], m_ref[...], v_ref[...])

    spec = pl.BlockSpec((tr, cols), lambda i: (i, 0))
    outs = pl.pallas_call(
        body, name="adamw_" + name, grid=(rows // tr,), in_specs=[spec] * 4, out_specs=[spec] * 3,
        out_shape=[jax.ShapeDtypeStruct((rows, cols), F32)] * 3, compiler_params=_params(),
    )(*[a.reshape(rows, cols) for a in (w, g, m, v)])
    return [o.reshape(shape) for o in outs]


def kernel(x, p, positions, gla_w_in, gla_w_gate_up, gla_b_gate, gla_norm_g, gla_w_out, mla_w_in, mla_q_norm, mla_kv_norm, mla_w_uq, mla_w_ukv, mla_w_out, conv_w_in, conv_w, conv_w_out, ln_g, ln_b, mlp_w1, mlp_w2, ple_w_gate, ple_w_proj, loss_target, m_gla_w_in, m_gla_w_gate_up, m_gla_b_gate, m_gla_norm_g, m_gla_w_out, m_mla_w_in, m_mla_q_norm, m_mla_kv_norm, m_mla_w_uq, m_mla_w_ukv, m_mla_w_out, m_conv_w_in, m_conv_w, m_conv_w_out, m_ln_g, m_ln_b, m_mlp_w1, m_mlp_w2, m_ple_w_gate, m_ple_w_proj, v_gla_w_in, v_gla_w_gate_up, v_gla_b_gate, v_gla_norm_g, v_gla_w_out, v_mla_w_in, v_mla_q_norm, v_mla_kv_norm, v_mla_w_uq, v_mla_w_ukv, v_mla_w_out, v_conv_w_in, v_conv_w, v_conv_w_out, v_ln_g, v_ln_b, v_mlp_w1, v_mlp_w2, v_ple_w_gate, v_ple_w_proj):
    args = locals()
    shard = {n: args[n] for n in WEIGHT_NAMES}
    mom = {n: args["m_" + n] for n in WEIGHT_NAMES}
    var = {n: args["v_" + n] for n in WEIGHT_NAMES}
    mx, my, mc = _my_place()

    comm = _Overlap(shard, _pack([shard[n].reshape(-1) for n in SMALL], LANES, 8, F32))
    small_all = comm.landing[0]["small"]
    small = {n: shard[n] for n in REPLICATED}
    small.update({n: _unshard(blk, n) for n, blk in _unpack_blocks(small_all.reshape(N_DEV, -1), SMALL, lead=False).items()})
    loss_part, grad_x, small_grads = _step(x[0], p[:, 0], positions[0], loss_target[0], small, comm)

    chains = {}

    def update(name, j, g, g_at, transposed, tok):
        chains[name], tok = _adamw_layer(name, shard[name], mom[name], var[name], j, g, g_at, transposed,
                                         chains.get(name), [tok])
        return tok

    def update_layer(i, groups, tok):
        j, kind = i // 3, i % 3
        red = comm.reduced[i]
        if "slab" in groups:
            tok = update("mlp_w2", i, red["a"], REG_W2, False, tok)
            tok = update("mlp_w1", i, red["a"], REG_W1T, True, tok)
            tok = update(("gla_w_out", "mla_w_out", "conv_w_out")[kind], j, red["a"], REG_WOUT, False, tok)
            tok = update("ple_w_gate", i, red["a"], REG_WG, False, tok)
            tok = update("ple_w_proj", i, red["b"], REG_WPT, True, tok)
        if "mixer" in groups:
            if kind == 0:
                tok = update("gla_w_in", j, red["gla"].T, (0, D_MODEL), False, tok)
            elif kind == 2:
                tok = update("conv_w_in", j, red["conv"], REG_CONV, True, tok)
            else:
                for n, g in (("mla_w_in", red["mla_in"][:, :MLA_IN]), ("mla_w_ukv", red["mla_ukv"]),
                             ("mla_w_uq", red["mla_uq"][:, :MLA_NOPE + MLA_ROPE])):
                    tok = update(n, j, g, (0, g.shape[0]), False, tok)
        return tok

    tok = comm.tail_begin(grad_x)
    tok = update_layer(3, ("slab", "mixer"), tok)
    tok = update_layer(2, ("slab", "mixer"), tok)
    tok = comm.tail_middle(tok)
    small_parts = [loss_part[0, :1]] + [small_grads[n].reshape(-1) for n in SMALL + REPLICATED]
    small_gather = _small_gather_start(_pack(small_parts, LANES, 8, F32), "ag_small_grads", tok)
    tok = update_layer(1, ("slab", "mixer"), small_gather[1][2])
    tok = update_layer(0, ("slab",), tok)
    comm.tail_end(tok)
    tok = update_layer(0, ("mixer",), tok)
    red_small = _sum_blocks(_small_gather_finish(small_gather, tok)).reshape(-1)
    loss = red_small[0]
    off = 1
    dev = 4 * mx + 2 * my + mc
    for n in SMALL + REPLICATED:
        shape, axis = WEIGHTS[n]
        full_g = red_small[off:off + _size(shape)].reshape(shape)
        off += _size(shape)
        if axis is not None:
            width = shape[axis] // N_DEV
            full_g = lax.dynamic_slice_in_dim(full_g, dev * width, width, axis=axis)
        chains[n] = [full_g, *_adamw(shard[n], full_g, mom[n], var[n], n)]
    return (loss, grad_x[None], *[chains[n][k] for k in range(4) for n in WEIGHT_NAMES])
```

```python
import functools

import jax
import jax.numpy as jnp
from jax import lax
from jax.experimental import pallas as pl
from jax.experimental.pallas import tpu as pltpu

F32, BF16 = jnp.float32, jnp.bfloat16
HIGHEST = lax.Precision.HIGHEST
MESH_AXES = ("x", "y", "c")
N_DEV = 8

D_MODEL = 1024
SEQ = 2048
DEPTH = 4
CHUNK = 64
ALPHA = (2 * DEPTH) ** 0.25
LN_EPS = 1e-5
RMS_EPS = 1e-6
PLE_DIM = 256
D_FF = 4 * D_MODEL
GLA_HEADS = 4
GLA_DK = 128
GLA_DV = 256
GLA_RANK = 16
GLA_TAU = 16.0
GLA_HK = GLA_HEADS * GLA_DK
GLA_HV = GLA_HEADS * GLA_DV
GLA_MAIN = 2 * GLA_HK + GLA_HV + D_MODEL
MLA_HEADS = 8
MLA_NOPE = 128
MLA_ROPE = 64
MLA_V = 128
MLA_RANK = 256
MLA_IN = 2 * MLA_RANK + MLA_ROPE
MLA_IN_PAD = 640
ROPE_BASE = 10000.0
LANES = 128
ADAM_LR, ADAM_B1, ADAM_B2, ADAM_EPS, ADAM_WD, ADAM_STEP = 0.001, 0.9, 0.999, 1e-08, 0.01, 10

V7X_VMEM_LIMIT_BYTES = 56 * 1024 * 1024
PACK_COLS = 1024
PACK_ROW_TILE = 256

WEIGHTS = {
    "gla_w_in": ((2, 1024, 3088), 2), "gla_w_gate_up": ((2, 16, 512), 2), "gla_b_gate": ((2, 512), 1),
    "gla_norm_g": ((2, 256), 1), "gla_w_out": ((2, 1024, 1024), 1), "mla_w_in": ((1, 1024, 576), 1),
    "mla_q_norm": ((1, 256), None), "mla_kv_norm": ((1, 256), None), "mla_w_uq": ((1, 256, 1536), 2),
    "mla_w_ukv": ((1, 256, 2048), 2), "mla_w_out": ((1, 1024, 1024), 1), "conv_w_in": ((1, 1024, 3072), 2),
    "conv_w": ((1, 3, 1024), 2), "conv_w_out": ((1, 1024, 1024), 1), "ln_g": ((4, 2, 1024), 2),
    "ln_b": ((4, 2, 1024), 2), "mlp_w1": ((4, 1024, 4096), 2), "mlp_w2": ((4, 4096, 1024), 1),
    "ple_w_gate": ((4, 1024, 1024), 1), "ple_w_proj": ((4, 256, 1024), 2),
}
WEIGHT_NAMES = list(WEIGHTS)
REG_W2, REG_W1T, REG_WOUT, REG_WG = (0, 512), (1, 512), (8, 128), (9, 128)
A_ROWS = 1280
REG_CONV = (0, 384)
REG_WPT = (0, 128)
REG_MLA_IN = (0, 128)
MLA_HEAD_PAD = 2 * LANES
SMALL = ["gla_w_gate_up", "gla_b_gate", "gla_norm_g", "conv_w", "ln_g", "ln_b"]
REPLICATED = ["mla_q_norm", "mla_kv_norm"]


def _params(**kw):
    return pltpu.CompilerParams(vmem_limit_bytes=V7X_VMEM_LIMIT_BYTES, **kw)


def _dot(a, b, ca, cb, precision=None):
    return lax.dot_general(a, b, (((ca,), (cb,)), ((), ())), precision=precision, preferred_element_type=F32)


def _nn(a, b):
    return _dot(a.astype(BF16), b.astype(BF16), 1, 0)


def _nt(a, b):
    return _dot(a.astype(BF16), b.astype(BF16), 1, 1)


def _tn(a, b):
    return _dot(a.astype(BF16), b.astype(BF16), 0, 0)


@jax.custom_vjp
def mm_nn(a, b):
    return _nn(a, b)


def _mm_nn_fwd(a, b):
    return _nn(a, b), (a, b)


def _mm_nn_bwd(res, g):
    a, b = res
    return _nt(g, b).astype(a.dtype), _tn(a, g).astype(b.dtype)


mm_nn.defvjp(_mm_nn_fwd, _mm_nn_bwd)


@jax.custom_vjp
def mm_nt(a, b):
    return _nt(a, b)


def _mm_nt_fwd(a, b):
    return _nt(a, b), (a, b)


def _mm_nt_bwd(res, g):
    a, b = res
    return _nn(g, b).astype(a.dtype), _tn(g, a).astype(b.dtype)


mm_nt.defvjp(_mm_nt_fwd, _mm_nt_bwd)


@jax.custom_vjp
def mm_tn(a, b):
    return _tn(a, b)


def _mm_tn_fwd(a, b):
    return _tn(a, b), (a, b)


def _mm_tn_bwd(res, g):
    a, b = res
    return _nt(b, g).astype(a.dtype), _nn(a, g).astype(b.dtype)


mm_tn.defvjp(_mm_tn_fwd, _mm_tn_bwd)


def _iota2(shape, dim):
    return lax.broadcasted_iota(jnp.int32, shape, dim)


@jax.custom_vjp
def cumsum_rows(x):
    n = x.shape[0]
    tri = (_iota2((n, n), 0) >= _iota2((n, n), 1)).astype(F32)
    return _dot(tri, x, 1, 0, precision=HIGHEST)


def _cumsum_fwd(x):
    return cumsum_rows(x), None


def _cumsum_bwd(_, g):
    n = g.shape[0]
    tri_t = (_iota2((n, n), 0) <= _iota2((n, n), 1)).astype(F32)
    return (_dot(tri_t, g, 1, 0, precision=HIGHEST),)


cumsum_rows.defvjp(_cumsum_fwd, _cumsum_bwd)


def _rot_matrix(transposed):
    i, j = _iota2((LANES, LANES), 0), _iota2((LANES, LANES), 1)
    if transposed:
        i, j = j, i
    half = MLA_ROPE // 2
    plus = (i == j - half) & (j >= half) & (j < MLA_ROPE)
    minus = (i == j + half) & (j < half)
    return plus.astype(F32) - minus.astype(F32)


@jax.custom_vjp
def rot_half(x):
    return _dot(x, _rot_matrix(False), 1, 0, precision=HIGHEST)


def _rot_fwd(x):
    return rot_half(x), None


def _rot_bwd(_, g):
    return (_dot(g, _rot_matrix(True), 1, 0, precision=HIGHEST),)


rot_half.defvjp(_rot_fwd, _rot_bwd)


def _shift_rows_raw(x, s):
    n = x.shape[0]
    row = _iota2(x.shape, 0)
    rolled = pltpu.roll(x, s % n, 0)
    keep = (row >= s) if s > 0 else (row < n + s)
    return jnp.where(keep, rolled, 0.0)


@functools.partial(jax.custom_vjp, nondiff_argnums=(1,))
def shift_rows(x, s):
    return _shift_rows_raw(x, s)


def _shift_fwd(x, s):
    return _shift_rows_raw(x, s), None


def _shift_bwd(s, _, g):
    return (_shift_rows_raw(g, -s),)


shift_rows.defvjp(_shift_fwd, _shift_bwd)


def _layer_norm(a, g, b):
    mu = jnp.mean(a, -1, keepdims=True)
    xc = a - mu
    var = jnp.mean(xc * xc, -1, keepdims=True)
    return xc * lax.rsqrt(var + LN_EPS) * g + b


def _rms_norm(a, g):
    return a * lax.rsqrt(jnp.mean(a * a, -1, keepdims=True) + RMS_EPS) * g


def _log_sigmoid(z):
    return jnp.minimum(z, 0.0) - jnp.log(1.0 + jnp.exp(-jnp.abs(z)))


def _matmul(a, b, *, name, ta=False, tb=False, tm=512, tn=512, a_fn=None, epi=None, epi_ins=(), out_dtypes=(BF16,),
            b_at=None, out_at=None, out_buf=None, after=None, n_row_sums=0, a_ins=(), a_out_dtypes=()):
    m = a.shape[1] if ta else a.shape[0]
    k = a.shape[0] if ta else a.shape[1]
    if b_at is None:
        n, kb = (b.shape[0], b.shape[1]) if tb else (b.shape[1], b.shape[0])
    else:
        rb, r = b_at
        n, kb = (N_DEV * r, b.shape[2]) if tb else (b.shape[2], N_DEV * r)
    assert kb == k, (name, a.shape, b.shape, k, kb)
    tm, tn = min(tm, m), min(tn, n)
    assert m % tm == 0 and n % tn == 0, (name, m, n, tm, tn)
    a_spec = pl.BlockSpec((k, tm), lambda i, j: (0, i)) if ta else pl.BlockSpec((tm, k), lambda i, j: (i, 0))
    if b_at is None:
        b_spec = pl.BlockSpec((tn, k), lambda i, j: (j, 0)) if tb else pl.BlockSpec((k, tn), lambda i, j: (0, j))
        load_b = lambda ref: ref[...]
    elif tb and tn == n:
        b_spec = pl.BlockSpec((N_DEV, r, k), lambda i, j: (0, rb, 0))
        load_b = lambda ref: ref[...].reshape(n, k)
    elif tb:
        assert tn == r, (name, tn, r)
        b_spec = pl.BlockSpec((1, r, k), lambda i, j: (j, rb, 0))
        load_b = lambda ref: ref[0]
    else:
        b_spec = pl.BlockSpec((N_DEV, r, tn), lambda i, j: (0, rb, j))
        load_b = lambda ref: ref[...].reshape(k, tn)
    e_specs = []
    for e in epi_ins:
        if e.shape == (1, n):
            e_specs.append(pl.BlockSpec((1, tn), lambda i, j: (0, j)))
        else:
            assert e.shape == (m, n), (name, e.shape, m, n)
            e_specs.append(pl.BlockSpec((tm, tn), lambda i, j: (i, j)))
    n_epi, n_ain, n_aout = len(epi_ins), len(a_ins), len(a_out_dtypes)
    assert n_aout == 0 or (tn == n and not ta and out_at is None), name
    ca, cb = (0 if ta else 1), (1 if tb else 0)
    operands = [a, b, *a_ins, *epi_ins]
    in_specs = [a_spec, b_spec, *[a_spec] * n_ain, *e_specs]
    if out_at is None:
        assert n_row_sums == 0 or tn == n, (name, tn, n)
        out_specs = [pl.BlockSpec((tm, tn), lambda i, j: (i, j)) for _ in out_dtypes]
        out_specs += [pl.BlockSpec((tm, k), lambda i, j: (i, 0))] * n_aout
        out_specs += [pl.BlockSpec((1, n), lambda i, j: (0, 0))] * n_row_sums
        out_shape = [jax.ShapeDtypeStruct((m, n), dt) for dt in out_dtypes]
        out_shape += [jax.ShapeDtypeStruct((m, k), dt) for dt in a_out_dtypes]
        out_shape += [jax.ShapeDtypeStruct((1, n), F32)] * n_row_sums
        aliases, n_buf = {}, 0
    else:
        orb, orows = out_at
        assert len(out_dtypes) == 1 and m == N_DEV * orows and n == out_buf.shape[2], (name, m, n)
        if tm > orows:
            assert tm % orows == 0, (name, tm, orows)
            out_specs = [pl.BlockSpec((tm // orows, orows, tn), lambda i, j: (i, orb, j))]
        else:
            per = orows // tm
            out_specs = [pl.BlockSpec((1, tm, tn), lambda i, j: (i // per, orb * per + i % per, j))]
        out_shape = [jax.ShapeDtypeStruct(out_buf.shape, out_buf.dtype)]
        operands.append(out_buf)
        in_specs.append(pl.BlockSpec(memory_space=pl.ANY))
        aliases, n_buf = {len(operands) - 1: 0}, 1
    for dep in ([] if after is None else after if isinstance(after, (list, tuple)) else [after]):
        if dep is not None:
            operands.append(dep)
            in_specs.append(pl.BlockSpec(memory_space=pl.ANY))
            n_buf += 1

    def body(a_ref, b_ref, *rest):
        av, a_outs = a_ref[...], ()
        if a_fn is not None:
            av = a_fn(av, *[r_[...] for r_ in rest[:n_ain]])
            if n_aout:
                av, *a_outs = av
        acc = _dot(av.astype(BF16), load_b(b_ref).astype(BF16), ca, cb)
        outs = epi(acc, *[r_[...] for r_ in rest[n_ain:n_ain + n_epi]]) if epi is not None else (acc,)
        o_refs = rest[n_ain + n_epi + n_buf:]
        n_tiles = len(o_refs) - n_row_sums - n_aout
        for o_ref, val in zip(o_refs[:n_tiles + n_aout], (*outs[:n_tiles], *a_outs)):
            o_ref[...] = val.astype(o_ref.dtype).reshape(o_ref.shape)
        if n_row_sums:
            @pl.when(pl.program_id(0) == 0)
            def _():
                for o_ref in o_refs[n_tiles + n_aout:]:
                    o_ref[...] = jnp.zeros_like(o_ref)

            for o_ref, val in zip(o_refs[n_tiles + n_aout:], outs[n_tiles:]):
                o_ref[...] += val

    outs = pl.pallas_call(
        body, name=name, grid=(m // tm, n // tn), in_specs=in_specs, out_specs=out_specs, out_shape=out_shape,
        input_output_aliases=aliases, compiler_params=_params(),
    )(*operands)
    return outs[0] if len(outs) == 1 else tuple(outs)


def _tile_fwd(f, tiled, params, out_dtypes, *, tm, name):
    t = tiled[0].shape[0]
    assert t % tm == 0
    out_avals = jax.eval_shape(f, *[jax.ShapeDtypeStruct((tm, x.shape[1]), F32) for x in tiled],
                               *[jax.ShapeDtypeStruct(p.shape, F32) for p in params])
    nt, npar = len(tiled), len(params)

    def body(*refs):
        ins = [r[...].astype(F32) for r in refs[:nt + npar]]
        outs = f(*ins)
        for o_ref, val in zip(refs[nt + npar:], outs):
            o_ref[...] = val.astype(o_ref.dtype)

    return pl.pallas_call(
        body, name=name, grid=(t // tm,),
        in_specs=[pl.BlockSpec((tm, x.shape[1]), lambda i: (i, 0)) for x in tiled]
        + [pl.BlockSpec(p.shape, lambda i: (0, 0)) for p in params],
        out_specs=[pl.BlockSpec((tm, o.shape[1]), lambda i: (i, 0)) for o in out_avals],
        out_shape=[jax.ShapeDtypeStruct((t, o.shape[1]), dt) for o, dt in zip(out_avals, out_dtypes)],
        compiler_params=_params(),
    )(*tiled, *params)


def _tile_bwd(f, tiled, params, cots, d_tiled_dtypes, *, tm, name, diff_tiled=None):
    t = tiled[0].shape[0]
    assert t % tm == 0
    nt, npar, nc = len(tiled), len(params), len(cots)
    diff_tiled = list(range(nt)) if diff_tiled is None else diff_tiled

    def body(*refs):
        ins = [r[...].astype(F32) for r in refs[:nt + npar]]
        cts = [r[...].astype(F32) for r in refs[nt + npar:nt + npar + nc]]
        o_refs = refs[nt + npar + nc:]
        _, vjp = jax.vjp(f, *ins)
        grads = vjp(tuple(cts))
        for o_ref, idx in zip(o_refs[:len(diff_tiled)], diff_tiled):
            o_ref[...] = grads[idx].astype(o_ref.dtype)
        p_refs = o_refs[len(diff_tiled):]

        @pl.when(pl.program_id(0) == 0)
        def _():
            for p_ref in p_refs:
                p_ref[...] = jnp.zeros_like(p_ref)

        for p_ref, gp in zip(p_refs, grads[nt:]):
            p_ref[...] += gp

    outs = pl.pallas_call(
        body, name=name, grid=(t // tm,),
        in_specs=[pl.BlockSpec((tm, x.shape[1]), lambda i: (i, 0)) for x in tiled]
        + [pl.BlockSpec(p.shape, lambda i: (0, 0)) for p in params]
        + [pl.BlockSpec((tm, c.shape[1]), lambda i: (i, 0)) for c in cots],
        out_specs=[pl.BlockSpec((tm, tiled[idx].shape[1]), lambda i: (i, 0)) for idx in diff_tiled]
        + [pl.BlockSpec(p.shape, lambda i: (0, 0)) for p in params],
        out_shape=[jax.ShapeDtypeStruct(tiled[idx].shape, dt) for idx, dt in zip(diff_tiled, d_tiled_dtypes)]
        + [jax.ShapeDtypeStruct(p.shape, F32) for p in params],
        compiler_params=_params(),
    )(*tiled, *params, *cots)
    return outs[:len(diff_tiled)], outs[len(diff_tiled):]


def _gla_head(q, k, v, r, z, g, st):
    c = q.shape[0]
    causal = _iota2((c, c), 0) >= _iota2((c, c), 1)
    la = _log_sigmoid(z) * (1.0 / GLA_TAU)
    big_l = cumsum_rows(la)
    ep, en = jnp.exp(big_l), jnp.exp(-big_l)
    qs = q * (GLA_DK ** -0.5)
    qp = qs * ep
    s = jnp.where(causal, mm_nt(qp, k * en), mm_nt(qs * en, k * ep))
    o = mm_nn(s, v) + mm_nt(qp, st)
    l_end = jnp.sum(la, axis=0, keepdims=True)
    st_new = st * jnp.exp(l_end) + mm_tn(v, k * jnp.exp(l_end - big_l))
    u = _rms_norm(o, g) * (r * jax.nn.sigmoid(r))
    return u, st_new


def _gla_slices(h):
    q = slice(GLA_DK * h, GLA_DK * (h + 1))
    k = slice(GLA_HK + GLA_DK * h, GLA_HK + GLA_DK * (h + 1))
    v = slice(2 * GLA_HK + GLA_DV * h, 2 * GLA_HK + GLA_DV * (h + 1))
    r = slice(2 * GLA_HK + GLA_HV + GLA_DV * h, 2 * GLA_HK + GLA_HV + GLA_DV * (h + 1))
    return q, k, v, r


GLA_CHUNKS_PER_STEP = 2


def _gla_fwd(proj, z, norm_g, after):
    t = proj.shape[0]
    nc, per = t // CHUNK, GLA_CHUNKS_PER_STEP
    rows_per_step = per * CHUNK
    after = [a for a in after if a is not None]

    def body(proj_ref, z_ref, g_ref, *rest):
        u_ref, st_save_ref, st_ref = rest[len(after):]

        @pl.when(pl.program_id(0) == 0)
        def _():
            st_ref[...] = jnp.zeros_like(st_ref)

        g = g_ref[...]
        for h in range(GLA_HEADS):
            sq, sk, sv, sr = _gla_slices(h)
            st = st_ref[h]
            for c in range(per):
                rows = slice(c * CHUNK, (c + 1) * CHUNK)
                st_save_ref[c, h] = st
                u, st = _gla_head(proj_ref[rows, sq].astype(F32), proj_ref[rows, sk].astype(F32),
                                  proj_ref[rows, sv].astype(F32), proj_ref[rows, sr].astype(F32),
                                  z_ref[rows, GLA_DK * h:GLA_DK * (h + 1)], g, st)
                u_ref[rows, GLA_DV * h:GLA_DV * (h + 1)] = u.astype(u_ref.dtype)
            st_ref[h] = st

    return pl.pallas_call(
        body, name="gla_fwd", grid=(nc // per,),
        in_specs=[pl.BlockSpec((rows_per_step, GLA_MAIN), lambda i: (i, 0)),
                  pl.BlockSpec((rows_per_step, GLA_HK), lambda i: (i, 0)), pl.BlockSpec((1, GLA_DV), lambda i: (0, 0))]
        + [pl.BlockSpec(memory_space=pl.ANY)] * len(after),
        out_specs=[pl.BlockSpec((rows_per_step, GLA_HV), lambda i: (i, 0)),
                   pl.BlockSpec((per, GLA_HEADS, GLA_DV, GLA_DK), lambda i: (i, 0, 0, 0))],
        out_shape=[jax.ShapeDtypeStruct((t, GLA_HV), BF16), jax.ShapeDtypeStruct((nc, GLA_HEADS, GLA_DV, GLA_DK), F32)],
        scratch_shapes=[pltpu.VMEM((GLA_HEADS, GLA_DV, GLA_DK), F32)],
        compiler_params=_params(),
    )(proj, z, norm_g, *after)


def _gla_bwd(proj, z, norm_g, states, du, after):
    t = proj.shape[0]
    nc, per = t // CHUNK, GLA_CHUNKS_PER_STEP
    rows_per_step = per * CHUNK
    n_steps = nc // per
    after = [a for a in after if a is not None]

    def body(proj_ref, z_ref, g_ref, st_in_ref, du_ref, *rest):
        dproj_ref, dz_ref, dg_ref, dst_ref = rest[len(after):]

        @pl.when(pl.program_id(0) == 0)
        def _():
            dst_ref[...] = jnp.zeros_like(dst_ref)
            dg_ref[...] = jnp.zeros_like(dg_ref)

        g = g_ref[...]
        for h in range(GLA_HEADS):
            sq, sk, sv, sr = _gla_slices(h)
            dst = dst_ref[h]
            for c in reversed(range(per)):
                rows = slice(c * CHUNK, (c + 1) * CHUNK)
                ins = (proj_ref[rows, sq].astype(F32), proj_ref[rows, sk].astype(F32), proj_ref[rows, sv].astype(F32),
                       proj_ref[rows, sr].astype(F32), z_ref[rows, GLA_DK * h:GLA_DK * (h + 1)], g, st_in_ref[c, h])
                _, vjp = jax.vjp(_gla_head, *ins)
                dq, dk, dv, dr, dz, dg, dst = vjp((du_ref[rows, GLA_DV * h:GLA_DV * (h + 1)], dst))
                dproj_ref[rows, sq] = dq.astype(dproj_ref.dtype)
                dproj_ref[rows, sk] = dk.astype(dproj_ref.dtype)
                dproj_ref[rows, sv] = dv.astype(dproj_ref.dtype)
                dproj_ref[rows, sr] = dr.astype(dproj_ref.dtype)
                dz_ref[rows, GLA_DK * h:GLA_DK * (h + 1)] = dz
                dg_ref[...] += dg
            dst_ref[h] = dst

    rev = lambda i: (n_steps - 1 - i, 0)
    return pl.pallas_call(
        body, name="gla_bwd", grid=(n_steps,),
        in_specs=[pl.BlockSpec((rows_per_step, GLA_MAIN), rev), pl.BlockSpec((rows_per_step, GLA_HK), rev),
                  pl.BlockSpec((1, GLA_DV), lambda i: (0, 0)),
                  pl.BlockSpec((per, GLA_HEADS, GLA_DV, GLA_DK), lambda i: (n_steps - 1 - i, 0, 0, 0)),
                  pl.BlockSpec((rows_per_step, GLA_HV), rev)] + [pl.BlockSpec(memory_space=pl.ANY)] * len(after),
        out_specs=[pl.BlockSpec((rows_per_step, GLA_MAIN), rev), pl.BlockSpec((rows_per_step, GLA_HK), rev),
                   pl.BlockSpec((1, GLA_DV), lambda i: (0, 0))],
        out_shape=[jax.ShapeDtypeStruct((t, GLA_MAIN), BF16), jax.ShapeDtypeStruct((t, GLA_HK), F32),
                   jax.ShapeDtypeStruct((1, GLA_DV), F32)],
        scratch_shapes=[pltpu.VMEM((GLA_HEADS, GLA_DV, GLA_DK), F32)],
        compiler_params=_params(),
    )(proj, z, norm_g, states, du, *after)


def _mla_pre(cq, cos, sin, gq, gkv, w_uq, w_ukv):
    qlat = _rms_norm(cq[:, :MLA_RANK], gq)
    kvlat = _rms_norm(cq[:, MLA_RANK:2 * MLA_RANK], gkv)
    kr = cq[:, 2 * MLA_RANK:]
    q = mm_nn(qlat, w_uq) * ((MLA_NOPE + MLA_ROPE) ** -0.5)
    kv = mm_nn(kvlat, w_ukv)
    pieces = []
    for h in range(MLA_HEADS):
        qr = q[:, MLA_HEAD_PAD * h + MLA_NOPE:MLA_HEAD_PAD * (h + 1)]
        pieces += [q[:, MLA_HEAD_PAD * h:MLA_HEAD_PAD * h + MLA_NOPE], qr * cos + rot_half(qr) * sin]
    return jnp.concatenate(pieces, axis=1), kv, kr * cos + rot_half(kr) * sin


MLA_Q_TILE = 256


def _mla_attn_block(qn, qr, kv, kr, q0):
    tq, nk = qn.shape[0], kv.shape[0]
    s = mm_nt(qn, kv[:, :MLA_NOPE]) + mm_nt(qr, kr)
    visible = (_iota2((tq, nk), 1) // CHUNK) <= ((q0 + _iota2((tq, nk), 0)) // CHUNK)
    s = jnp.where(visible, s, -1e30)
    e = jnp.exp(s - jnp.max(s, -1, keepdims=True))
    p = e / jnp.sum(e, -1, keepdims=True)
    return mm_nn(p, kv[:, MLA_NOPE:])


def _mla_attn_fwd(q, kv, kr, after):
    t = q.shape[0]
    after = [a for a in after if a is not None]

    def body(q_ref, kv_ref, kr_ref, *rest):
        (o_ref,) = rest[len(after):]
        for i in range(t // MLA_Q_TILE):
            rows = slice(i * MLA_Q_TILE, (i + 1) * MLA_Q_TILE)
            keys = slice(0, (i + 1) * MLA_Q_TILE)
            o = _mla_attn_block(q_ref[rows, :MLA_NOPE].astype(F32), q_ref[rows, MLA_NOPE:].astype(F32),
                                kv_ref[keys, :].astype(F32), kr_ref[keys, :].astype(F32), i * MLA_Q_TILE)
            o_ref[rows, :] = o.astype(o_ref.dtype)

    return pl.pallas_call(
        body, name="mla_attn_fwd", grid=(MLA_HEADS,),
        in_specs=[pl.BlockSpec((t, MLA_HEAD_PAD), lambda h: (0, h)),
                  pl.BlockSpec((t, MLA_NOPE + MLA_V), lambda h: (0, h)), pl.BlockSpec((t, LANES), lambda h: (0, 0))]
        + [pl.BlockSpec(memory_space=pl.ANY)] * len(after),
        out_specs=pl.BlockSpec((t, MLA_V), lambda h: (0, h)),
        out_shape=jax.ShapeDtypeStruct((t, MLA_HEADS * MLA_V), BF16),
        compiler_params=_params(),
    )(q, kv, kr, *after)


def _mla_attn_bwd(q, kv, kr, do, after):
    t = q.shape[0]
    after = [a for a in after if a is not None]

    def body(q_ref, kv_ref, kr_ref, do_ref, *rest):
        dq_ref, dkv_ref, dkr_ref = rest[len(after):]
        dkv_ref[...] = jnp.zeros_like(dkv_ref)

        @pl.when(pl.program_id(0) == 0)
        def _():
            dkr_ref[...] = jnp.zeros_like(dkr_ref)

        for i in range(t // MLA_Q_TILE):
            rows = slice(i * MLA_Q_TILE, (i + 1) * MLA_Q_TILE)
            keys = slice(0, (i + 1) * MLA_Q_TILE)
            f = functools.partial(_mla_attn_block, q0=i * MLA_Q_TILE)
            _, vjp = jax.vjp(f, q_ref[rows, :MLA_NOPE].astype(F32), q_ref[rows, MLA_NOPE:].astype(F32),
                             kv_ref[keys, :].astype(F32), kr_ref[keys, :].astype(F32))
            dqn, dqr, dkv, dkr = vjp(do_ref[rows, :].astype(F32))
            dq_ref[rows, :MLA_NOPE] = dqn
            dq_ref[rows, MLA_NOPE:] = dqr
            dkv_ref[keys, :] += dkv
            dkr_ref[keys, :] += dkr

    return pl.pallas_call(
        body, name="mla_attn_bwd", grid=(MLA_HEADS,),
        in_specs=[pl.BlockSpec((t, MLA_HEAD_PAD), lambda h: (0, h)),
                  pl.BlockSpec((t, MLA_NOPE + MLA_V), lambda h: (0, h)), pl.BlockSpec((t, LANES), lambda h: (0, 0)),
                  pl.BlockSpec((t, MLA_V), lambda h: (0, h))] + [pl.BlockSpec(memory_space=pl.ANY)] * len(after),
        out_specs=[pl.BlockSpec((t, MLA_HEAD_PAD), lambda h: (0, h)),
                   pl.BlockSpec((t, MLA_NOPE + MLA_V), lambda h: (0, h)), pl.BlockSpec((t, LANES), lambda h: (0, 0))],
        out_shape=[jax.ShapeDtypeStruct(q.shape, F32), jax.ShapeDtypeStruct(kv.shape, F32),
                   jax.ShapeDtypeStruct(kr.shape, F32)],
        compiler_params=_params(),
    )(q, kv, kr, do, *after)


def _rope_tables(pos_col, inv_freq_row):
    t = pos_col.shape[0]

    def body(pos_ref, f_ref, cos_ref, sin_ref):
        ang = pos_ref[...].astype(F32) * f_ref[...]
        live = _iota2(ang.shape, 1) < MLA_ROPE
        cos_ref[...] = jnp.where(live, jnp.cos(ang), 0.0)
        sin_ref[...] = jnp.where(live, jnp.sin(ang), 0.0)

    return pl.pallas_call(
        body, name="rope_tables", out_shape=[jax.ShapeDtypeStruct((t, LANES), F32)] * 2, compiler_params=_params(),
    )(pos_col, inv_freq_row)


CONV_COL_TILE = 256


def _conv_gate(b, c, u, w0, w1, w2):
    cu = c * u
    return b * (w2 * cu + w1 * shift_rows(cu, 1) + w0 * shift_rows(cu, 2))


def _conv_specs(t):
    nb = D_MODEL // CONV_COL_TILE
    return [pl.BlockSpec((t, CONV_COL_TILE), lambda j, part=part: (0, part * nb + j)) for part in range(3)]


def _conv_fwd(bcu, w, after):
    t = bcu.shape[0]
    after = [a for a in after if a is not None]

    def body(b_ref, c_ref, u_ref, w_ref, *rest):
        (o_ref,) = rest[len(after):]
        o_ref[...] = _conv_gate(b_ref[...], c_ref[...], u_ref[...], w_ref[0:1, :], w_ref[1:2, :],
                                w_ref[2:3, :]).astype(o_ref.dtype)

    return pl.pallas_call(
        body, name="conv_fwd", grid=(D_MODEL // CONV_COL_TILE,),
        in_specs=_conv_specs(t) + [pl.BlockSpec((3, CONV_COL_TILE), lambda j: (0, j))]
        + [pl.BlockSpec(memory_space=pl.ANY)] * len(after),
        out_specs=pl.BlockSpec((t, CONV_COL_TILE), lambda j: (0, j)),
        out_shape=jax.ShapeDtypeStruct((t, D_MODEL), BF16), compiler_params=_params(),
    )(bcu, bcu, bcu, w, *after)


def _conv_bwd(bcu, w, dout, after):
    t = bcu.shape[0]
    after = [a for a in after if a is not None]

    def body(b_ref, c_ref, u_ref, w_ref, do_ref, *rest):
        db_ref, dc_ref, du_ref, dw_ref = rest[len(after):]
        _, vjp = jax.vjp(_conv_gate, b_ref[...], c_ref[...], u_ref[...], w_ref[0:1, :], w_ref[1:2, :], w_ref[2:3, :])
        db, dc, du, dw0, dw1, dw2 = vjp(do_ref[...])
        db_ref[...] = db.astype(db_ref.dtype)
        dc_ref[...] = dc.astype(dc_ref.dtype)
        du_ref[...] = du.astype(du_ref.dtype)
        dw_ref[0:1, :] = dw0
        dw_ref[1:2, :] = dw1
        dw_ref[2:3, :] = dw2

    col = pl.BlockSpec((t, CONV_COL_TILE), lambda j: (0, j))
    return pl.pallas_call(
        body, name="conv_bwd", grid=(D_MODEL // CONV_COL_TILE,),
        in_specs=_conv_specs(t) + [pl.BlockSpec((3, CONV_COL_TILE), lambda j: (0, j)), col]
        + [pl.BlockSpec(memory_space=pl.ANY)] * len(after),
        out_specs=[col, col, col, pl.BlockSpec((3, CONV_COL_TILE), lambda j: (0, j))],
        out_shape=[jax.ShapeDtypeStruct((t, D_MODEL), BF16)] * 3 + [jax.ShapeDtypeStruct((3, D_MODEL), F32)],
        compiler_params=_params(),
    )(bcu, bcu, bcu, w, dout, *after)


def _loss_head(y, target):
    t, d = y.shape
    tm = 256

    def body(y_ref, t_ref, loss_ref, dy_ref):
        @pl.when(pl.program_id(0) == 0)
        def _():
            loss_ref[...] = jnp.zeros_like(loss_ref)

        err = y_ref[...] - t_ref[...]
        dy_ref[...] = err * (1.0 / d)
        loss_ref[...] += 0.5 * jnp.sum(jnp.sum(err * err, axis=-1, keepdims=True) * (1.0 / d))

    tile = pl.BlockSpec((tm, d), lambda i: (i, 0))
    return pl.pallas_call(
        body, name="loss_head", grid=(t // tm,), in_specs=[tile, tile],
        out_specs=[pl.BlockSpec((8, LANES), lambda i: (0, 0)), tile],
        out_shape=[jax.ShapeDtypeStruct((8, LANES), F32), jax.ShapeDtypeStruct((t, d), F32)],
        compiler_params=_params(),
    )(y, target)


def _ln_epi(acc, res, g, b):
    a = ALPHA * res + acc
    y = _layer_norm(a, g, b)
    return a, y, y


def _ln_fn(a, g, b):
    return (_layer_norm(a, g, b),)


def _ln_bwd_epi(scale):
    def epi(acc, res, a, g, b):
        _, vjp = jax.vjp(_ln_fn, a, g, b)
        da, dg, db = vjp((acc + scale * res,))
        return da, da, dg, db
    return epi


def _relu_sq(h):
    r = jnp.maximum(h, 0)
    return r * r


def _pad_cols(w, n):
    return jnp.pad(w, ((0, 0), (0, n - w.shape[1])))


def _pad_rows(w, n):
    return jnp.pad(w, ((0, n - w.shape[0]), (0, 0)))


def _step(x, p, positions, target, small, comm):
    t = x.shape[0]
    w = small
    freqs = ROPE_BASE ** (-jnp.arange(0, MLA_ROPE // 2, dtype=F32) * (2.0 / MLA_ROPE))
    freq_row = jnp.concatenate([freqs, freqs, jnp.zeros((LANES - MLA_ROPE,), F32)])[None, :]
    cos, sin = _rope_tables(positions.reshape(t, 1), freq_row)

    saved = []
    xb = x.astype(BF16)
    for i in range(DEPTH):
        j, kind = i // 3, i % 3
        wl = comm.mixer_weights(i)
        s = {"x": xb, "wl": wl}
        tok = comm.at("fwd", i, "begin", x)
        if kind == 0:
            s["w_main"] = wl["gla_w_in_t"][:GLA_MAIN]
            s["w_lr"] = _pad_rows(wl["gla_w_in_t"][GLA_MAIN:], LANES)
            s["w_up"] = _pad_rows(w["gla_w_gate_up"][j], LANES).astype(BF16)
            s["proj"] = _matmul(xb, s["w_main"], name="gla_proj", tb=True, tn=1024, after=tok)
            s["glr"] = _matmul(xb, s["w_lr"], name="gla_lr", tb=True, out_dtypes=(F32,))
            s["z"] = _matmul(s["glr"], s["w_up"], name="gla_gate", epi=lambda acc, b: (acc + b,),
                             epi_ins=(w["gla_b_gate"][j][None, :],), out_dtypes=(F32,))
            tok = comm.at("fwd", i, "proj_done", s["z"]) or []
            s["u"], s["states"] = _gla_fwd(s["proj"], s["z"], w["gla_norm_g"][j][None, :], tok)
        elif kind == 1:
            s["cq"] = _matmul(xb, wl["mla_in"], name="mla_proj", tn=MLA_IN_PAD, b_at=REG_MLA_IN, out_dtypes=(F32,),
                              after=tok)
            s["pre_params"] = (w["mla_q_norm"][j][None, :], w["mla_kv_norm"][j][None, :], wl["mla_w_uq"], wl["mla_w_ukv"])
            s["q"], s["kv"], s["kr"] = _tile_fwd(_mla_pre, (s["cq"], cos, sin), s["pre_params"], (BF16, BF16, BF16),
                                                 tm=256, name="mla_pre_fwd")
            tok = comm.at("fwd", i, "proj_done", s["kv"]) or []
            s["u"] = _mla_attn_fwd(s["q"], s["kv"], s["kr"], tok)
        else:
            s["bcu"] = _matmul(xb, wl["conv"], name="conv_proj", tb=True, tm=256, tn=3 * D_MODEL, b_at=REG_CONV,
                               out_dtypes=(F32,), after=tok)
            tok = comm.at("fwd", i, "proj_done", s["bcu"]) or []
            s["u"] = _conv_fwd(s["bcu"], w["conv_w"][j], tok)
        g0, b0 = w["ln_g"][i, 0][None, :], w["ln_b"][i, 0][None, :]
        g1, b1 = w["ln_g"][i, 1][None, :], w["ln_b"][i, 1][None, :]
        wa, wb = s["wa"], _ = comm.slab_weights(i, s["u"])
        s["a1"], s["x1"], s["x1b"] = _matmul(s["u"], wa, name="mixer_out_ln", tm=256, tn=D_MODEL, b_at=REG_WOUT,
                                             epi=_ln_epi, epi_ins=(x, g0, b0), out_dtypes=(F32, F32, BF16))
        s["hh"] = _matmul(s["x1b"], wa, name="mlp_up", tb=True, tm=256, tn=D_FF, b_at=REG_W1T)
        tok = comm.at("fwd", i, "mid", s["hh"])
        s["a2"], s["x2"], s["x2b"] = _matmul(s["hh"], wa, name="mlp_down_ln", tm=256, tn=D_MODEL, b_at=REG_W2,
                                             a_fn=_relu_sq, epi=_ln_epi, epi_ins=(s["x1"], g1, b1),
                                             out_dtypes=(F32, F32, BF16), after=tok)
        s["pp"] = _matmul(p[i], wb, name="ple_proj", tb=True, tn=D_MODEL, b_at=REG_WPT)
        tok = comm.at("fwd", i, "end", s["pp"])
        def ple_epi(acc, xr, pp):
            y = xr + jax.nn.sigmoid(acc) * pp.astype(F32)
            return y, y, acc

        x, xb, s["gt"] = _matmul(s["x2b"], wa, name="ple_gate", tn=1024, b_at=REG_WG, epi=ple_epi,
                                 epi_ins=(s["x2"], s["pp"]), out_dtypes=(F32, BF16, BF16), after=tok)
        saved.append(s)

    loss_part, dx = _loss_head(x, target)

    gw = {n: [None] * WEIGHTS[n][0][0] for n in SMALL + REPLICATED}
    ln_g_grads, ln_b_grads = [[None, None] for _ in range(DEPTH)], [[None, None] for _ in range(DEPTH)]
    resid = lambda acc, r: (acc + ALPHA * r,)
    plus = lambda acc, r: (acc + r,)
    for i in reversed(range(DEPTH)):
        j, kind = i // 3, i % 3
        s = saved[i]
        wa = s["wa"]
        ga = lax.empty((N_DEV, A_ROWS, D_MODEL), BF16)
        gb = lax.empty((N_DEV, REG_WPT[1], PLE_DIM), BF16)
        layer_grads = {}
        tok = comm.at("bwd", i, "begin", dx)

        def ple_bwd(dxo, gt, pp):
            sg = jax.nn.sigmoid(gt.astype(F32))
            d_gt = dxo * pp.astype(F32) * sg * (1.0 - sg)
            return d_gt, d_gt, dxo * sg

        g1, b1 = w["ln_g"][i, 1][None, :], w["ln_b"][i, 1][None, :]
        d_a2, d_a2b, d_gt, d_pp, ln_g_grads[i][1], ln_b_grads[i][1] = _matmul(
            dx, wa, name="ple_gate_dx_ln", tb=True, tm=256, tn=D_MODEL, b_at=REG_WG, a_fn=ple_bwd,
            a_ins=(s["gt"], s["pp"]), a_out_dtypes=(BF16, BF16), epi=_ln_bwd_epi(1.0), epi_ins=(dx, s["a2"], g1, b1),
            out_dtypes=(F32, BF16), n_row_sums=2, after=tok)
        gb = _matmul(d_pp, p[i], name="ple_proj_dw", ta=True, tm=512, tn=PLE_DIM, out_at=REG_WPT, out_buf=gb)
        ga = _matmul(s["x2b"], d_gt, name="ple_gate_dw", ta=True, tm=512, tn=1024, out_at=REG_WG, out_buf=ga)
        tok = comm.at("bwd", i, "ln", d_a2)
        ga = _matmul(s["hh"], d_a2b, name="mlp_down_dw", ta=True, tm=REG_W2[1], tn=1024, a_fn=_relu_sq, out_at=REG_W2,
                     out_buf=ga, after=tok)
        d_hh = _matmul(d_a2b, wa, name="mlp_down_dx", tb=True, tm=256, tn=D_FF, b_at=REG_W2, after=[ga, gb],
                       epi=lambda acc, hh: (acc * 2.0 * jnp.maximum(hh.astype(F32), 0.0),), epi_ins=(s["hh"],))
        ga = _matmul(d_hh, s["x1b"], name="mlp_up_dw", ta=True, tm=REG_W1T[1], tn=1024, out_at=REG_W1T, out_buf=ga)
        g0, b0 = w["ln_g"][i, 0][None, :], w["ln_b"][i, 0][None, :]
        d_a1, d_a1b, ln_g_grads[i][0], ln_b_grads[i][0] = _matmul(
            d_hh, wa, name="mlp_up_dx_ln", tm=256, tn=D_MODEL, b_at=REG_W1T, epi=_ln_bwd_epi(ALPHA),
            epi_ins=(d_a2, s["a1"], g0, b0), out_dtypes=(F32, BF16), n_row_sums=2, after=ga)
        ga = _matmul(s["u"], d_a1b, name="mixer_out_dw", ta=True, tm=512, tn=1024, out_at=REG_WOUT, out_buf=ga)
        du = _matmul(d_a1b, wa, name="mixer_out_dx", tb=True, tn=1024, b_at=REG_WOUT, out_dtypes=(F32,), after=ga)
        comm.slab_grads(i, ga, gb)
        tok = comm.at("bwd", i, "slab_done", du) or []
        if kind == 0:
            dproj, dz, dg = _gla_bwd(s["proj"], s["z"], w["gla_norm_g"][j][None, :], s["states"], du, tok)
            tok = comm.at("bwd", i, "mixer_done", dproj)
            gw["gla_norm_g"][j] = dg[0]
            gw["gla_b_gate"][j] = _tile_bwd(lambda zz, b: (zz + b,), (s["z"],), (w["gla_b_gate"][j][None, :],), (dz,), (),
                                            tm=256, name="gla_bias_bwd", diff_tiled=[])[1][0][0]
            gw["gla_w_gate_up"][j] = _matmul(s["glr"], dz, name="gla_gate_dw", ta=True, out_dtypes=(F32,),
                                             after=tok)[:GLA_RANK]
            dglr = _matmul(dz, s["w_up"], name="gla_gate_dx", tb=True, out_dtypes=(F32,))
            dw_main = _matmul(dproj, s["x"], name="gla_proj_dw", ta=True, tn=1024, out_dtypes=(F32,))
            dw_lr = _matmul(dglr, s["x"], name="gla_lr_dw", ta=True, tn=1024, out_dtypes=(F32,))[:GLA_RANK]
            layer_grads["gla_w_in_t"] = jnp.concatenate([dw_main, dw_lr], axis=0)
            dx = _matmul(dproj, s["w_main"], name="gla_proj_dx", tn=1024, epi=resid, epi_ins=(d_a1,),
                         out_dtypes=(F32,), after=[dw_main, dw_lr, gw["gla_w_gate_up"][j]])
            dx = _matmul(dglr, s["w_lr"], name="gla_lr_dx", tn=1024, epi=plus, epi_ins=(dx,), out_dtypes=(F32,))
        elif kind == 1:
            dq, dkv, dkr = _mla_attn_bwd(s["q"], s["kv"], s["kr"], du, tok)
            tok = comm.at("bwd", i, "mixer_done", dq)
            (d_cq,), (dgq, dgkv, layer_grads["mla_uq"], layer_grads["mla_ukv"]) = _tile_bwd(
                _mla_pre, (s["cq"], cos, sin), s["pre_params"], (dq, dkv, dkr), (BF16,), tm=256, name="mla_pre_bwd",
                diff_tiled=[0])
            gw["mla_q_norm"][j], gw["mla_kv_norm"][j] = dgq[0], dgkv[0]
            layer_grads["mla_in"] = _matmul(s["x"], d_cq, name="mla_proj_dw", ta=True, tm=512, tn=MLA_IN_PAD,
                                            out_at=REG_MLA_IN, after=tok,
                                            out_buf=lax.empty((N_DEV, REG_MLA_IN[1], MLA_IN_PAD), BF16))
            dx = _matmul(d_cq, s["wl"]["mla_in"], name="mla_proj_dx", tb=True, tn=1024, b_at=REG_MLA_IN, epi=resid,
                         epi_ins=(d_a1,), out_dtypes=(F32,), after=layer_grads["mla_in"])
        else:
            db, dc, du_, dcw = _conv_bwd(s["bcu"], w["conv_w"][j], du, tok)
            tok = comm.at("bwd", i, "mixer_done", db)
            gw["conv_w"][j] = dcw
            dbcu = jnp.concatenate([db, dc, du_], axis=1)
            layer_grads["conv"] = _matmul(dbcu, s["x"], name="conv_proj_dw", ta=True, tm=REG_CONV[1], tn=1024,
                                          out_at=REG_CONV, out_buf=lax.empty((N_DEV, REG_CONV[1], D_MODEL), BF16),
                                          after=tok)
            dx = _matmul(dbcu, s["wl"]["conv"], name="conv_proj_dx", tn=1024, b_at=REG_CONV, epi=resid, epi_ins=(d_a1,),
                         out_dtypes=(F32,), after=layer_grads["conv"])
        comm.mixer_grads(i, layer_grads)

    gw["ln_g"] = [jnp.concatenate([a, b], axis=0) for a, b in ln_g_grads]
    gw["ln_b"] = [jnp.concatenate([a, b], axis=0) for a, b in ln_b_grads]
    return loss_part, dx, {n: jnp.stack(gw[n]).astype(F32) for n in gw}


MESH_IDS = pl.DeviceIdType.MESH
ANY = pl.BlockSpec(memory_space=pl.ANY)
HBM_SPEC = pl.BlockSpec(memory_space=pltpu.HBM)
SEM_SPEC = pl.BlockSpec(memory_space=pltpu.SEMAPHORE)
DATAFLOW_EFFECT = pltpu.SideEffectType.DATAFLOW_SIDE_EFFECTING
CORE_COPIES, CHIP_COPIES = 4, 3


def _my_place():
    return lax.axis_index("x"), lax.axis_index("y"), lax.axis_index("c")


def _other_chips(mx, my):
    return [(1 - mx, my), (mx, 1 - my), (1 - mx, 1 - my)]


def _remote(src, dst, send_sems, recv_sems, k, to):
    return pltpu.make_async_remote_copy(src_ref=src, dst_ref=dst, send_sem=send_sems.at[k], recv_sem=recv_sems.at[k],
                                        device_id=to, device_id_type=MESH_IDS)


def _gather_first_copies(n_arr):
    def make(bufs, send_sems, recv_sems):
        mx, my, mc = _my_place()
        mine = 4 * mx + 2 * my + mc
        peers = [(mx, my, 1 - mc)] + [(cx, cy, mc) for cx, cy in _other_chips(mx, my)]
        return [_remote(bufs[a].at[mine], bufs[a].at[mine], send_sems, recv_sems, (1 + CHIP_COPIES) * a + k, to)
                for a in range(n_arr) for k, to in enumerate(peers)]
    return make, (1 + CHIP_COPIES) * n_arr


def _gather_forward_copies(n_arr):
    def make(bufs, send_sems, recv_sems):
        mx, my, mc = _my_place()
        blocks = [4 * cx + 2 * cy + mc for cx, cy in _other_chips(mx, my)]
        return [_remote(bufs[a].at[blk], bufs[a].at[blk], send_sems, recv_sems, CHIP_COPIES * a + k, (mx, my, 1 - mc))
                for a in range(n_arr) for k, blk in enumerate(blocks)]
    return make, CHIP_COPIES * n_arr


def _scatter_core_copies(n_arr):
    def make(bufs, send_sems, recv_sems):
        mx, my, mc = _my_place()
        return [_remote(bufs[a].at[2 * k + (1 - mc)], bufs[n_arr + a].at[k], send_sems, recv_sems, CORE_COPIES * a + k,
                        (mx, my, 1 - mc)) for a in range(n_arr) for k in range(CORE_COPIES)]
    return make, CORE_COPIES * n_arr


def _scatter_chip_copies(n_arr):
    def make(bufs, send_sems, recv_sems):
        mx, my, mc = _my_place()
        return [_remote(bufs[a].at[2 * cx + cy], bufs[n_arr + a].at[k], send_sems, recv_sems, CHIP_COPIES * a + k,
                        (cx, cy, mc)) for a in range(n_arr) for k, (cx, cy) in enumerate(_other_chips(mx, my))]
    return make, CHIP_COPIES * n_arr


def _exchange(name, bufs, copies):
    make, n_copies = copies
    n = len(bufs)

    def body(*refs):
        descs = make(refs[:n], refs[2 * n], refs[2 * n + 1])
        for cp in descs:
            cp.start()
        for cp in descs:
            cp.wait()

    return pl.pallas_call(
        body, name=name, out_shape=[jax.ShapeDtypeStruct(b.shape, b.dtype) for b in bufs], in_specs=[ANY] * n,
        out_specs=[ANY] * n, input_output_aliases={i: i for i in range(n)},
        scratch_shapes=[pltpu.SemaphoreType.DMA((n_copies,)), pltpu.SemaphoreType.DMA((n_copies,))],
    )(*bufs)


def _exchange_start(name, bufs, copies, after):
    make, n_copies = copies
    n = len(bufs)

    def body(*refs):
        for cp in make(refs[:n], refs[n + 1], refs[n + 2]):
            cp.start()
        refs[-1][...] = jnp.zeros_like(refs[-1])

    outs = pl.pallas_call(
        body, name=name,
        out_shape=(pltpu.SemaphoreType.DMA((n_copies,)), pltpu.SemaphoreType.DMA((n_copies,)),
                   *[pltpu.HBM(b.shape, b.dtype) for b in bufs], jax.ShapeDtypeStruct((8, LANES), F32)),
        in_specs=[HBM_SPEC] * n + [ANY],
        out_specs=(SEM_SPEC, SEM_SPEC, *[HBM_SPEC] * n, pl.BlockSpec(memory_space=pltpu.VMEM)),
        input_output_aliases={i: 2 + i for i in range(n)},
        compiler_params=pltpu.CompilerParams(has_side_effects=DATAFLOW_EFFECT),
    )(*[pltpu.with_memory_space_constraint(b, pltpu.HBM) for b in bufs], after)
    return (outs[0], outs[1]), list(outs[2:2 + n]), outs[-1]


def _exchange_wait(name, sems, bufs, copies, after):
    make, _ = copies
    n = len(bufs)

    def body(*refs):
        for cp in make(refs[:n], refs[n], refs[n + 1]):
            cp.wait_send()
            cp.wait_recv()

    return list(pl.pallas_call(
        body, name=name, out_shape=[pltpu.HBM(b.shape, b.dtype) for b in bufs],
        in_specs=[HBM_SPEC] * n + [SEM_SPEC, SEM_SPEC, ANY], out_specs=[HBM_SPEC] * n,
        input_output_aliases={i: i for i in range(n)},
        compiler_params=pltpu.CompilerParams(has_side_effects=DATAFLOW_EFFECT),
    )(*bufs, *sems, after))


SUM_TILE_BYTES = 2 * 1024 * 1024


def _row_tile(r, c):
    best = None
    for cand in range(16, r + 1, 16):
        if r % cand == 0 and cand * c * 2 <= SUM_TILE_BYTES:
            best = cand
    return r if best is None else best


def _pair_sum(g, recv, my_c):
    _, r, c = g.shape
    tr = _row_tile(r, c)

    def body(c_ref, g_ref, r_ref, o_ref):
        o_ref[...] = (g_ref[...].astype(F32) + r_ref[...].astype(F32)).astype(o_ref.dtype)

    return pl.pallas_call(
        body, name="rs_pair_sum", out_shape=jax.ShapeDtypeStruct((4, r, c), g.dtype),
        grid_spec=pltpu.PrefetchScalarGridSpec(
            num_scalar_prefetch=1, grid=(4, r // tr),
            in_specs=[pl.BlockSpec((1, tr, c), lambda n, i, cr: (2 * n + cr[0], i, 0)),
                      pl.BlockSpec((1, tr, c), lambda n, i, cr: (n, i, 0))],
            out_specs=pl.BlockSpec((1, tr, c), lambda n, i, cr: (n, i, 0))),
        compiler_params=_params(),
    )(my_c, g, recv)


def _chip_sum(h, recv, my_chip):
    _, r, c = h.shape
    tr = _row_tile(r, c)

    def body(j_ref, h_ref, r0_ref, r1_ref, r2_ref, o_ref):
        o_ref[...] = ((h_ref[0].astype(F32) + r0_ref[0].astype(F32)) + r1_ref[0].astype(F32)) + r2_ref[0].astype(F32)

    return pl.pallas_call(
        body, name="rs_chip_sum", out_shape=jax.ShapeDtypeStruct((r, c), F32),
        grid_spec=pltpu.PrefetchScalarGridSpec(
            num_scalar_prefetch=1, grid=(r // tr,),
            in_specs=[pl.BlockSpec((1, tr, c), lambda i, jr: (jr[0], i, 0))]
            + [pl.BlockSpec((1, tr, c), lambda i, jr, n=n: (n, i, 0)) for n in range(3)],
            out_specs=pl.BlockSpec((tr, c), lambda i, jr: (i, 0))),
        compiler_params=_params(),
    )(my_chip, h, recv, recv, recv)


def _sum_blocks(g):
    n, r, c = g.shape

    def body(g_ref, o_ref):
        acc = g_ref[0]
        for k in range(1, n):
            acc = acc + g_ref[k]
        o_ref[...] = acc

    return pl.pallas_call(body, name="sum_blocks", out_shape=jax.ShapeDtypeStruct((r, c), F32), compiler_params=_params())(g)


def _pack(flat_parts, cols, row_multiple, dtype):
    flat = jnp.concatenate([f.astype(dtype) for f in flat_parts])
    per_row_block = cols * row_multiple
    padded = -(-flat.shape[0] // per_row_block) * per_row_block
    return jnp.pad(flat, (0, padded - flat.shape[0])).reshape(padded // cols, cols)


def _shard_shape(name):
    shape, axis = WEIGHTS[name]
    if axis is None:
        return shape
    return tuple(s // N_DEV if a == axis else s for a, s in enumerate(shape))


def _size(shape):
    n = 1
    for s in shape:
        n *= s
    return n


def _unshard(blocks, name):
    _, axis = WEIGHTS[name]
    return jnp.concatenate([blocks[k] for k in range(N_DEV)], axis=axis)


def _unpack_blocks(flat, names, lead):
    out, off = {}, 0
    for n in names:
        shp = _shard_shape(n)[1:] if lead else _shard_shape(n)
        out[n] = flat[..., off:off + _size(shp)].reshape(flat.shape[:-1] + shp)
        off += _size(shp)
    return out


def _layer_slabs(shard, i):
    j, kind = i // 3, i % 3
    w_out = (shard["gla_w_out"], shard["mla_w_out"], shard["conv_w_out"])[kind][j]
    out = {"a": jnp.concatenate([shard["mlp_w2"][i], shard["mlp_w1"][i].T, w_out, shard["ple_w_gate"][i]], axis=0).astype(BF16),
           "b": shard["ple_w_proj"][i].T.astype(BF16)}
    if kind == 0:
        out["gla"] = shard["gla_w_in"][j].T.astype(BF16)
    elif kind == 1:
        out["mla_in"] = _pad_cols(shard["mla_w_in"][j], MLA_IN_PAD).astype(BF16)
        out["mla_uq"] = _pad_cols(shard["mla_w_uq"][j], MLA_HEAD_PAD).astype(BF16)
        out["mla_ukv"] = shard["mla_w_ukv"][j].astype(BF16)
    else:
        out["conv"] = shard["conv_w_in"][j].T.astype(BF16)
    return out


def _mixer_weights(landed, i):
    kind = i % 3
    if kind == 0:
        return {"gla_w_in_t": landed["gla"].reshape(-1, D_MODEL)}
    if kind == 2:
        return {"conv": landed["conv"]}
    heads_side_by_side = lambda g: g.transpose(1, 0, 2).reshape(g.shape[1], -1)
    return {"mla_in": landed["mla_in"], "mla_w_uq": heads_side_by_side(landed["mla_uq"]),
            "mla_w_ukv": heads_side_by_side(landed["mla_ukv"])}


def _mixer_grad_buffers(layer_grads, i):
    kind = i % 3
    if kind == 0:
        return {"gla": layer_grads["gla_w_in_t"].reshape(N_DEV, -1, D_MODEL).astype(BF16)}
    if kind == 2:
        return {"conv": layer_grads["conv"]}
    head_blocks = lambda g: g.reshape(g.shape[0], N_DEV, -1).transpose(1, 0, 2).astype(BF16)
    return {"mla_in": layer_grads["mla_in"], "mla_uq": head_blocks(layer_grads["mla_uq"]),
            "mla_ukv": head_blocks(layer_grads["mla_ukv"])}


SLAB_KEYS = ("a", "b")


class _Overlap:
    def __init__(self, shard, small_pack):
        mx, my, mc = _my_place()
        self.my_c = mc.astype(jnp.int32).reshape(1)
        self.my_chip = (2 * mx + my).astype(jnp.int32).reshape(1)
        mine = 4 * mx + 2 * my + mc
        slabs = [_layer_slabs(shard, i) for i in range(DEPTH)]
        slabs[0]["small"] = small_pack
        self.landing = [{k: lax.dynamic_update_index_in_dim(lax.empty((N_DEV, *v.shape), v.dtype), v, mine, 0)
                         for k, v in slabs[i].items()} for i in range(DEPTH)]
        self.fly = {}
        self.grads = [{} for _ in range(DEPTH)]
        self.reduced = [{} for _ in range(DEPTH)]
        tok = self._gather_first(0, "mixer", shard["ln_g"])
        tok = self._gather_first(0, "slab", tok)
        bufs = self._wait("ag_first_mixer_l0", tok)
        self.landing[0].update(zip(self._keys(self.landing[0], "mixer"),
                                   _exchange("ag_forward_mixer_l0", bufs, _gather_forward_copies(len(bufs)))))

    @staticmethod
    def _keys(names, group):
        return [k for k in names if (k in SLAB_KEYS) == (group == "slab")]

    def _start(self, name, bufs, copies, after):
        sems, bufs, tok = _exchange_start(name + "_start", bufs, copies, after)
        self.fly[name] = (sems, bufs, copies)
        return tok

    def _wait(self, name, after):
        sems, bufs, copies = self.fly.pop(name)
        return _exchange_wait(name + "_wait", sems, bufs, copies, after)

    def mixer_weights(self, i):
        return _mixer_weights(self.landing[i], i)

    def slab_weights(self, i, dep):
        self._gather_done(i, "slab", dep)
        return self.landing[i]["a"], self.landing[i]["b"]

    def slab_grads(self, i, ga, gb):
        self.grads[i].update(a=ga, b=gb)

    def mixer_grads(self, i, layer_grads):
        self.grads[i].update(_mixer_grad_buffers(layer_grads, i))

    def at(self, phase, i, point, dep):
        toks = []
        if phase == "fwd":
            if point == "begin" and i == 0:
                toks.append(self._gather_first(1, "mixer", self.landing[0][self._keys(self.landing[0], "mixer")[0]]))
                toks.append(self._gather_first(1, "slab", toks[-1]))
            if point == "proj_done":
                toks.append(self._gather_forward(i, "slab", dep))
            if point == "mid" and i + 1 < DEPTH:
                toks.append(self._gather_forward(i + 1, "mixer", dep))
                if i + 2 < DEPTH:
                    toks.append(self._gather_first(i + 2, "mixer", dep))
                    toks.append(self._gather_first(i + 2, "slab", toks[-1]))
            if point == "end" and i + 1 < DEPTH:
                self._gather_done(i + 1, "mixer", dep)
        else:
            if point == "begin" and i + 1 < DEPTH:
                toks.append(self._scatter_cores(i + 1, "mixer", dep))
            if point == "ln" and i + 1 < DEPTH:
                toks.append(self._scatter_chips(i + 1, "mixer", dep))
            if point == "slab_done":
                if i + 1 < DEPTH:
                    self._scatter_done(i + 1, "slab", dep)
                    self._scatter_done(i + 1, "mixer", dep)
                toks.append(self._scatter_cores(i, "slab", dep))
            if point == "mixer_done":
                toks.append(self._scatter_chips(i, "slab", dep))
        return toks or None

    def _gather_first(self, i, group, after):
        bufs = [self.landing[i][k] for k in self._keys(self.landing[i], group)]
        return self._start(f"ag_first_{group}_l{i}", bufs, _gather_first_copies(len(bufs)), after)

    def _gather_forward(self, i, group, after):
        bufs = self._wait(f"ag_first_{group}_l{i}", after)
        return self._start(f"ag_forward_{group}_l{i}", bufs, _gather_forward_copies(len(bufs)), after)

    def _gather_done(self, i, group, after):
        keys = self._keys(self.landing[i], group)
        self.landing[i].update(zip(keys, self._wait(f"ag_forward_{group}_l{i}", after)))

    def _scatter_cores(self, i, group, after):
        gs = [self.grads[i][k] for k in self._keys(self.grads[i], group)]
        land = [lax.empty((4, *g.shape[1:]), g.dtype) for g in gs]
        return self._start(f"rs_cores_{group}_l{i}", gs + land, _scatter_core_copies(len(gs)), after)

    def _pair_sums(self, bufs):
        n = len(bufs) // 2
        hs = [_pair_sum(g, r, self.my_c) for g, r in zip(bufs[:n], bufs[n:])]
        return hs + [lax.empty((3, *h.shape[1:]), h.dtype) for h in hs]

    def _scatter_chips(self, i, group, after):
        bufs = self._pair_sums(self._wait(f"rs_cores_{group}_l{i}", after))
        return self._start(f"rs_chips_{group}_l{i}", bufs, _scatter_chip_copies(len(bufs) // 2), after)

    def _chip_sums(self, i, group, bufs):
        n = len(bufs) // 2
        for k, h, r in zip(self._keys(self.grads[i], group), bufs[:n], bufs[n:]):
            self.reduced[i][k] = _chip_sum(h, r, self.my_chip)

    def _scatter_done(self, i, group, after):
        self._chip_sums(i, group, self._wait(f"rs_chips_{group}_l{i}", after))

    def tail_begin(self, dep):
        return self._scatter_cores(0, "mixer", dep)

    def tail_middle(self, dep):
        self._scatter_done(0, "slab", dep)
        return self._scatter_chips(0, "mixer", dep)

    def tail_end(self, dep):
        self._scatter_done(0, "mixer", dep)


def _small_gather_start(x, name, after):
    mx, my, mc = _my_place()
    land = lax.dynamic_update_index_in_dim(lax.empty((N_DEV, *x.shape), x.dtype), x, 4 * mx + 2 * my + mc, 0)
    return name, _exchange_start(name + "_first_start", [land], _gather_first_copies(1), after)


def _small_gather_finish(started, after):
    name, (sems, bufs, _) = started
    bufs = _exchange_wait(name + "_first_wait", sems, bufs, _gather_first_copies(1), after)
    return _exchange(name + "_forward", bufs, _gather_forward_copies(1))[0]


def _adamw_math(w, g, m, v):
    m2 = ADAM_B1 * m + (1.0 - ADAM_B1) * g
    v2 = ADAM_B2 * v + (1.0 - ADAM_B2) * (g * g)
    m_hat = m2 / (1.0 - ADAM_B1 ** ADAM_STEP)
    v_hat = v2 / (1.0 - ADAM_B2 ** ADAM_STEP)
    return -ADAM_LR * (m_hat / (jnp.sqrt(v_hat) + ADAM_EPS) + ADAM_WD * w), m2, v2


ADAMW_TILE_BYTES = 1024 * 1024


def _adamw_layer(name, w, m, v, j, g, g_at, transposed, chain, after):
    n_layers, r, c = w.shape
    tr = max(t for t in range(8, r + 1, 8) if r % t == 0 and (t * c * 4 <= ADAMW_TILE_BYTES or t == 8))
    rb, rows = g_at
    if transposed:
        assert rows == c and g.shape[1] == r, (name, g.shape, g_at)
        g_spec = pl.BlockSpec((rows, tr), lambda i: (rb, i))
    else:
        assert rows == r and g.shape[1] == c, (name, g.shape, g_at)
        g_spec = pl.BlockSpec((tr, c), lambda i: (rb * (r // tr) + i, 0))
    extra = list(chain or []) + [a for a in (after or []) if a is not None]
    n_chain = 4 if chain else 0

    def body(w_ref, m_ref, v_ref, g_ref, *rest):
        g_out, d_out, m_out, v_out, tok_ref = rest[len(extra):]
        gv = g_ref[...].T if transposed else g_ref[...]
        g_out[0] = gv
        d_out[0], m_out[0], v_out[0] = _adamw_math(w_ref[0], gv, m_ref[0], v_ref[0])
        tok_ref[...] = jnp.zeros_like(tok_ref)

    layer_spec = pl.BlockSpec((1, tr, c), lambda i: (j, i, 0))
    outs = pl.pallas_call(
        body, name=f"adamw_{name}_l{j}", grid=(r // tr,),
        in_specs=[layer_spec] * 3 + [g_spec] + [pl.BlockSpec(memory_space=pl.ANY)] * len(extra),
        out_specs=[layer_spec] * 4 + [pl.BlockSpec((8, LANES), lambda i: (0, 0))],
        out_shape=[jax.ShapeDtypeStruct(w.shape, F32)] * 4 + [jax.ShapeDtypeStruct((8, LANES), F32)],
        input_output_aliases={4 + k: k for k in range(n_chain)}, compiler_params=_params(),
    )(w, m, v, g, *extra)
    return list(outs[:4]), outs[4]


def _adamw(w, g, m, v, name):
    shape = w.shape
    cols = shape[-1]
    rows = _size(shape) // cols
    tr = rows
    for cand in (512, 256, 128, 64, 32, 16, 8):
        if rows > cand and rows % cand == 0:
            tr = cand
            break

    def body(w_ref, g_ref, m_ref, v_ref, d_ref, mo_ref, vo_ref):
        d_ref[...], mo_ref[...], vo_ref[...] = _adamw_math(w_ref[...], g_ref[...], m_ref[...], v_ref[...])

    spec = pl.BlockSpec((tr, cols), lambda i: (i, 0))
    outs = pl.pallas_call(
        body, name="adamw_" + name, grid=(rows // tr,), in_specs=[spec] * 4, out_specs=[spec] * 3,
        out_shape=[jax.ShapeDtypeStruct((rows, cols), F32)] * 3, compiler_params=_params(),
    )(*[a.reshape(rows, cols) for a in (w, g, m, v)])
    return [o.reshape(shape) for o in outs]


def kernel(x, p, positions, gla_w_in, gla_w_gate_up, gla_b_gate, gla_norm_g, gla_w_out, mla_w_in, mla_q_norm, mla_kv_norm, mla_w_uq, mla_w_ukv, mla_w_out, conv_w_in, conv_w, conv_w_out, ln_g, ln_b, mlp_w1, mlp_w2, ple_w_gate, ple_w_proj, loss_target, m_gla_w_in, m_gla_w_gate_up, m_gla_b_gate, m_gla_norm_g, m_gla_w_out, m_mla_w_in, m_mla_q_norm, m_mla_kv_norm, m_mla_w_uq, m_mla_w_ukv, m_mla_w_out, m_conv_w_in, m_conv_w, m_conv_w_out, m_ln_g, m_ln_b, m_mlp_w1, m_mlp_w2, m_ple_w_gate, m_ple_w_proj, v_gla_w_in, v_gla_w_gate_up, v_gla_b_gate, v_gla_norm_g, v_gla_w_out, v_mla_w_in, v_mla_q_norm, v_mla_kv_norm, v_mla_w_uq, v_mla_w_ukv, v_mla_w_out, v_conv_w_in, v_conv_w, v_conv_w_out, v_ln_g, v_ln_b, v_mlp_w1, v_mlp_w2, v_ple_w_gate, v_ple_w_proj):
    args = locals()
    shard = {n: args[n] for n in WEIGHT_NAMES}
    mom = {n: args["m_" + n] for n in WEIGHT_NAMES}
    var = {n: args["v_" + n] for n in WEIGHT_NAMES}
    mx, my, mc = _my_place()

    comm = _Overlap(shard, _pack([shard[n].reshape(-1) for n in SMALL], LANES, 8, F32))
    small_all = comm.landing[0]["small"]
    small = {n: shard[n] for n in REPLICATED}
    small.update({n: _unshard(blk, n) for n, blk in _unpack_blocks(small_all.reshape(N_DEV, -1), SMALL, lead=False).items()})
    loss_part, grad_x, small_grads = _step(x[0], p[:, 0], positions[0], loss_target[0], small, comm)

    chains = {}

    def update(name, j, g, g_at, transposed, tok):
        chains[name], tok = _adamw_layer(name, shard[name], mom[name], var[name], j, g, g_at, transposed,
                                         chains.get(name), [tok])
        return tok

    def update_layer(i, groups, tok):
        j, kind = i // 3, i % 3
        red = comm.reduced[i]
        if "slab" in groups:
            tok = update("mlp_w2", i, red["a"], REG_W2, False, tok)
            tok = update("mlp_w1", i, red["a"], REG_W1T, True, tok)
            tok = update(("gla_w_out", "mla_w_out", "conv_w_out")[kind], j, red["a"], REG_WOUT, False, tok)
            tok = update("ple_w_gate", i, red["a"], REG_WG, False, tok)
            tok = update("ple_w_proj", i, red["b"], REG_WPT, True, tok)
        if "mixer" in groups:
            if kind == 0:
                tok = update("gla_w_in", j, red["gla"].T, (0, D_MODEL), False, tok)
            elif kind == 2:
                tok = update("conv_w_in", j, red["conv"], REG_CONV, True, tok)
            else:
                for n, g in (("mla_w_in", red["mla_in"][:, :MLA_IN]), ("mla_w_ukv", red["mla_ukv"]),
                             ("mla_w_uq", red["mla_uq"][:, :MLA_NOPE + MLA_ROPE])):
                    tok = update(n, j, g, (0, g.shape[0]), False, tok)
        return tok

    tok = comm.tail_begin(grad_x)
    tok = update_layer(3, ("slab", "mixer"), tok)
    tok = update_layer(2, ("slab", "mixer"), tok)
    tok = comm.tail_middle(tok)
    small_parts = [loss_part[0, :1]] + [small_grads[n].reshape(-1) for n in SMALL + REPLICATED]
    small_gather = _small_gather_start(_pack(small_parts, LANES, 8, F32), "ag_small_grads", tok)
    tok = update_layer(1, ("slab", "mixer"), small_gather[1][2])
    tok = update_layer(0, ("slab",), tok)
    comm.tail_end(tok)
    tok = update_layer(0, ("mixer",), tok)
    red_small = _sum_blocks(_small_gather_finish(small_gather, tok)).reshape(-1)
    loss = red_small[0]
    off = 1
    dev = 4 * mx + 2 * my + mc
    for n in SMALL + REPLICATED:
        shape, axis = WEIGHTS[n]
        full_g = red_small[off:off + _size(shape)].reshape(shape)
        off += _size(shape)
        if axis is not None:
            width = shape[axis] // N_DEV
            full_g = lax.dynamic_slice_in_dim(full_g, dev * width, width, axis=axis)
        chains[n] = [full_g, *_adamw(shard[n], full_g, mom[n], var[n], n)]
    return (loss, grad_x[None], *[chains[n][k] for k in range(4) for n in WEIGHT_NAMES])
```

```python
import functools

import jax
import jax.numpy as jnp
from jax import lax
from jax.experimental import pallas as pl
from jax.experimental.pallas import tpu as pltpu

F32, BF16 = jnp.float32, jnp.bfloat16
HIGHEST = lax.Precision.HIGHEST
MESH_AXES = ("x", "y", "c")
N_DEV = 8

D_MODEL = 1024
SEQ = 2048
DEPTH = 4
CHUNK = 64
ALPHA = (2 * DEPTH) ** 0.25
LN_EPS = 1e-5
RMS_EPS = 1e-6
PLE_DIM = 256
D_FF = 4 * D_MODEL
GLA_HEADS = 4
GLA_DK = 128
GLA_DV = 256
GLA_RANK = 16
GLA_TAU = 16.0
GLA_HK = GLA_HEADS * GLA_DK
GLA_HV = GLA_HEADS * GLA_DV
GLA_MAIN = 2 * GLA_HK + GLA_HV + D_MODEL
MLA_HEADS = 8
MLA_NOPE = 128
MLA_ROPE = 64
MLA_V = 128
MLA_RANK = 256
MLA_IN = 2 * MLA_RANK + MLA_ROPE
MLA_IN_PAD = 640
ROPE_BASE = 10000.0
LANES = 128
ADAM_LR, ADAM_B1, ADAM_B2, ADAM_EPS, ADAM_WD, ADAM_STEP = 0.001, 0.9, 0.999, 1e-08, 0.01, 10

V7X_VMEM_LIMIT_BYTES = 56 * 1024 * 1024
PACK_COLS = 1024
PACK_ROW_TILE = 256

WEIGHTS = {
    "gla_w_in": ((2, 1024, 3088), 2), "gla_w_gate_up": ((2, 16, 512), 2), "gla_b_gate": ((2, 512), 1),
    "gla_norm_g": ((2, 256), 1), "gla_w_out": ((2, 1024, 1024), 1), "mla_w_in": ((1, 1024, 576), 1),
    "mla_q_norm": ((1, 256), None), "mla_kv_norm": ((1, 256), None), "mla_w_uq": ((1, 256, 1536), 2),
    "mla_w_ukv": ((1, 256, 2048), 2), "mla_w_out": ((1, 1024, 1024), 1), "conv_w_in": ((1, 1024, 3072), 2),
    "conv_w": ((1, 3, 1024), 2), "conv_w_out": ((1, 1024, 1024), 1), "ln_g": ((4, 2, 1024), 2),
    "ln_b": ((4, 2, 1024), 2), "mlp_w1": ((4, 1024, 4096), 2), "mlp_w2": ((4, 4096, 1024), 1),
    "ple_w_gate": ((4, 1024, 1024), 1), "ple_w_proj": ((4, 256, 1024), 2),
}
WEIGHT_NAMES = list(WEIGHTS)
REG_W2, REG_W1T, REG_WOUT, REG_WG = (0, 512), (1, 512), (8, 128), (9, 128)
A_ROWS = 1280
REG_CONV = (0, 384)
REG_WPT = (0, 128)
REG_MLA_IN = (0, 128)
MLA_HEAD_PAD = 2 * LANES
SMALL = ["gla_w_gate_up", "gla_b_gate", "gla_norm_g", "conv_w", "ln_g", "ln_b"]
REPLICATED = ["mla_q_norm", "mla_kv_norm"]


def _params(**kw):
    return pltpu.CompilerParams(vmem_limit_bytes=V7X_VMEM_LIMIT_BYTES, **kw)


def _dot(a, b, ca, cb, precision=None):
    return lax.dot_general(a, b, (((ca,), (cb,)), ((), ())), precision=precision, preferred_element_type=F32)


def _nn(a, b):
    return _dot(a.astype(BF16), b.astype(BF16), 1, 0)


def _nt(a, b):
    return _dot(a.astype(BF16), b.astype(BF16), 1, 1)


def _tn(a, b):
    return _dot(a.astype(BF16), b.astype(BF16), 0, 0)


@jax.custom_vjp
def mm_nn(a, b):
    return _nn(a, b)


def _mm_nn_fwd(a, b):
    return _nn(a, b), (a, b)


def _mm_nn_bwd(res, g):
    a, b = res
    return _nt(g, b).astype(a.dtype), _tn(a, g).astype(b.dtype)


mm_nn.defvjp(_mm_nn_fwd, _mm_nn_bwd)


@jax.custom_vjp
def mm_nt(a, b):
    return _nt(a, b)


def _mm_nt_fwd(a, b):
    return _nt(a, b), (a, b)


def _mm_nt_bwd(res, g):
    a, b = res
    return _nn(g, b).astype(a.dtype), _tn(g, a).astype(b.dtype)


mm_nt.defvjp(_mm_nt_fwd, _mm_nt_bwd)


@jax.custom_vjp
def mm_tn(a, b):
    return _tn(a, b)


def _mm_tn_fwd(a, b):
    return _tn(a, b), (a, b)


def _mm_tn_bwd(res, g):
    a, b = res
    return _nt(b, g).astype(a.dtype), _nn(a, g).astype(b.dtype)


mm_tn.defvjp(_mm_tn_fwd, _mm_tn_bwd)


def _iota2(shape, dim):
    return lax.broadcasted_iota(jnp.int32, shape, dim)


@jax.custom_vjp
def cumsum_rows(x):
    n = x.shape[0]
    tri = (_iota2((n, n), 0) >= _iota2((n, n), 1)).astype(F32)
    return _dot(tri, x, 1, 0, precision=HIGHEST)


def _cumsum_fwd(x):
    return cumsum_rows(x), None


def _cumsum_bwd(_, g):
    n = g.shape[0]
    tri_t = (_iota2((n, n), 0) <= _iota2((n, n), 1)).astype(F32)
    return (_dot(tri_t, g, 1, 0, precision=HIGHEST),)


cumsum_rows.defvjp(_cumsum_fwd, _cumsum_bwd)


def _rot_matrix(transposed):
    i, j = _iota2((LANES, LANES), 0), _iota2((LANES, LANES), 1)
    if transposed:
        i, j = j, i
    half = MLA_ROPE // 2
    plus = (i == j - half) & (j >= half) & (j < MLA_ROPE)
    minus = (i == j + half) & (j < half)
    return plus.astype(F32) - minus.astype(F32)


@jax.custom_vjp
def rot_half(x):
    return _dot(x, _rot_matrix(False), 1, 0, precision=HIGHEST)


def _rot_fwd(x):
    return rot_half(x), None


def _rot_bwd(_, g):
    return (_dot(g, _rot_matrix(True), 1, 0, precision=HIGHEST),)


rot_half.defvjp(_rot_fwd, _rot_bwd)


def _shift_rows_raw(x, s):
    n = x.shape[0]
    row = _iota2(x.shape, 0)
    rolled = pltpu.roll(x, s % n, 0)
    keep = (row >= s) if s > 0 else (row < n + s)
    return jnp.where(keep, rolled, 0.0)


@functools.partial(jax.custom_vjp, nondiff_argnums=(1,))
def shift_rows(x, s):
    return _shift_rows_raw(x, s)


def _shift_fwd(x, s):
    return _shift_rows_raw(x, s), None


def _shift_bwd(s, _, g):
    return (_shift_rows_raw(g, -s),)


shift_rows.defvjp(_shift_fwd, _shift_bwd)


def _layer_norm(a, g, b):
    mu = jnp.mean(a, -1, keepdims=True)
    xc = a - mu
    var = jnp.mean(xc * xc, -1, keepdims=True)
    return xc * lax.rsqrt(var + LN_EPS) * g + b


def _rms_norm(a, g):
    return a * lax.rsqrt(jnp.mean(a * a, -1, keepdims=True) + RMS_EPS) * g


def _log_sigmoid(z):
    return jnp.minimum(z, 0.0) - jnp.log(1.0 + jnp.exp(-jnp.abs(z)))


def _matmul(a, b, *, name, ta=False, tb=False, tm=512, tn=512, a_fn=None, epi=None, epi_ins=(), out_dtypes=(BF16,),
            b_at=None, out_at=None, out_buf=None, after=None, n_row_sums=0, a_ins=(), a_out_dtypes=()):
    m = a.shape[1] if ta else a.shape[0]
    k = a.shape[0] if ta else a.shape[1]
    if b_at is None:
        n, kb = (b.shape[0], b.shape[1]) if tb else (b.shape[1], b.shape[0])
    else:
        rb, r = b_at
        n, kb = (N_DEV * r, b.shape[2]) if tb else (b.shape[2], N_DEV * r)
    assert kb == k, (name, a.shape, b.shape, k, kb)
    tm, tn = min(tm, m), min(tn, n)
    assert m % tm == 0 and n % tn == 0, (name, m, n, tm, tn)
    a_spec = pl.BlockSpec((k, tm), lambda i, j: (0, i)) if ta else pl.BlockSpec((tm, k), lambda i, j: (i, 0))
    if b_at is None:
        b_spec = pl.BlockSpec((tn, k), lambda i, j: (j, 0)) if tb else pl.BlockSpec((k, tn), lambda i, j: (0, j))
        load_b = lambda ref: ref[...]
    elif tb and tn == n:
        b_spec = pl.BlockSpec((N_DEV, r, k), lambda i, j: (0, rb, 0))
        load_b = lambda ref: ref[...].reshape(n, k)
    elif tb:
        assert tn == r, (name, tn, r)
        b_spec = pl.BlockSpec((1, r, k), lambda i, j: (j, rb, 0))
        load_b = lambda ref: ref[0]
    else:
        b_spec = pl.BlockSpec((N_DEV, r, tn), lambda i, j: (0, rb, j))
        load_b = lambda ref: ref[...].reshape(k, tn)
    e_specs = []
    for e in epi_ins:
        if e.shape == (1, n):
            e_specs.append(pl.BlockSpec((1, tn), lambda i, j: (0, j)))
        else:
            assert e.shape == (m, n), (name, e.shape, m, n)
            e_specs.append(pl.BlockSpec((tm, tn), lambda i, j: (i, j)))
    n_epi, n_ain, n_aout = len(epi_ins), len(a_ins), len(a_out_dtypes)
    assert n_aout == 0 or (tn == n and not ta and out_at is None), name
    ca, cb = (0 if ta else 1), (1 if tb else 0)
    operands = [a, b, *a_ins, *epi_ins]
    in_specs = [a_spec, b_spec, *[a_spec] * n_ain, *e_specs]
    if out_at is None:
        assert n_row_sums == 0 or tn == n, (name, tn, n)
        out_specs = [pl.BlockSpec((tm, tn), lambda i, j: (i, j)) for _ in out_dtypes]
        out_specs += [pl.BlockSpec((tm, k), lambda i, j: (i, 0))] * n_aout
        out_specs += [pl.BlockSpec((1, n), lambda i, j: (0, 0))] * n_row_sums
        out_shape = [jax.ShapeDtypeStruct((m, n), dt) for dt in out_dtypes]
        out_shape += [jax.ShapeDtypeStruct((m, k), dt) for dt in a_out_dtypes]
        out_shape += [jax.ShapeDtypeStruct((1, n), F32)] * n_row_sums
        aliases, n_buf = {}, 0
    else:
        orb, orows = out_at
        assert len(out_dtypes) == 1 and m == N_DEV * orows and n == out_buf.shape[2], (name, m, n)
        if tm > orows:
            assert tm % orows == 0, (name, tm, orows)
            out_specs = [pl.BlockSpec((tm // orows, orows, tn), lambda i, j: (i, orb, j))]
        else:
            per = orows // tm
            out_specs = [pl.BlockSpec((1, tm, tn), lambda i, j: (i // per, orb * per + i % per, j))]
        out_shape = [jax.ShapeDtypeStruct(out_buf.shape, out_buf.dtype)]
        operands.append(out_buf)
        in_specs.append(pl.BlockSpec(memory_space=pl.ANY))
        aliases, n_buf = {len(operands) - 1: 0}, 1
    for dep in ([] if after is None else after if isinstance(after, (list, tuple)) else [after]):
        if dep is not None:
            operands.append(dep)
            in_specs.append(pl.BlockSpec(memory_space=pl.ANY))
            n_buf += 1

    def body(a_ref, b_ref, *rest):
        av, a_outs = a_ref[...], ()
        if a_fn is not None:
            av = a_fn(av, *[r_[...] for r_ in rest[:n_ain]])
            if n_aout:
                av, *a_outs = av
        acc = _dot(av.astype(BF16), load_b(b_ref).astype(BF16), ca, cb)
        outs = epi(acc, *[r_[...] for r_ in rest[n_ain:n_ain + n_epi]]) if epi is not None else (acc,)
        o_refs = rest[n_ain + n_epi + n_buf:]
        n_tiles = len(o_refs) - n_row_sums - n_aout
        for o_ref, val in zip(o_refs[:n_tiles + n_aout], (*outs[:n_tiles], *a_outs)):
            o_ref[...] = val.astype(o_ref.dtype).reshape(o_ref.shape)
        if n_row_sums:
            @pl.when(pl.program_id(0) == 0)
            def _():
                for o_ref in o_refs[n_tiles + n_aout:]:
                    o_ref[...] = jnp.zeros_like(o_ref)

            for o_ref, val in zip(o_refs[n_tiles + n_aout:], outs[n_tiles:]):
                o_ref[...] += val

    outs = pl.pallas_call(
        body, name=name, grid=(m // tm, n // tn), in_specs=in_specs, out_specs=out_specs, out_shape=out_shape,
        input_output_aliases=aliases, compiler_params=_params(),
    )(*operands)
    return outs[0] if len(outs) == 1 else tuple(outs)


def _tile_fwd(f, tiled, params, out_dtypes, *, tm, name):
    t = tiled[0].shape[0]
    assert t % tm == 0
    out_avals = jax.eval_shape(f, *[jax.ShapeDtypeStruct((tm, x.shape[1]), F32) for x in tiled],
                               *[jax.ShapeDtypeStruct(p.shape, F32) for p in params])
    nt, npar = len(tiled), len(params)

    def body(*refs):
        ins = [r[...].astype(F32) for r in refs[:nt + npar]]
        outs = f(*ins)
        for o_ref, val in zip(refs[nt + npar:], outs):
            o_ref[...] = val.astype(o_ref.dtype)

    return pl.pallas_call(
        body, name=name, grid=(t // tm,),
        in_specs=[pl.BlockSpec((tm, x.shape[1]), lambda i: (i, 0)) for x in tiled]
        + [pl.BlockSpec(p.shape, lambda i: (0, 0)) for p in params],
        out_specs=[pl.BlockSpec((tm, o.shape[1]), lambda i: (i, 0)) for o in out_avals],
        out_shape=[jax.ShapeDtypeStruct((t, o.shape[1]), dt) for o, dt in zip(out_avals, out_dtypes)],
        compiler_params=_params(),
    )(*tiled, *params)


def _tile_bwd(f, tiled, params, cots, d_tiled_dtypes, *, tm, name, diff_tiled=None):
    t = tiled[0].shape[0]
    assert t % tm == 0
    nt, npar, nc = len(tiled), len(params), len(cots)
    diff_tiled = list(range(nt)) if diff_tiled is None else diff_tiled

    def body(*refs):
        ins = [r[...].astype(F32) for r in refs[:nt + npar]]
        cts = [r[...].astype(F32) for r in refs[nt + npar:nt + npar + nc]]
        o_refs = refs[nt + npar + nc:]
        _, vjp = jax.vjp(f, *ins)
        grads = vjp(tuple(cts))
        for o_ref, idx in zip(o_refs[:len(diff_tiled)], diff_tiled):
            o_ref[...] = grads[idx].astype(o_ref.dtype)
        p_refs = o_refs[len(diff_tiled):]

        @pl.when(pl.program_id(0) == 0)
        def _():
            for p_ref in p_refs:
                p_ref[...] = jnp.zeros_like(p_ref)

        for p_ref, gp in zip(p_refs, grads[nt:]):
            p_ref[...] += gp

    outs = pl.pallas_call(
        body, name=name, grid=(t // tm,),
        in_specs=[pl.BlockSpec((tm, x.shape[1]), lambda i: (i, 0)) for x in tiled]
        + [pl.BlockSpec(p.shape, lambda i: (0, 0)) for p in params]
        + [pl.BlockSpec((tm, c.shape[1]), lambda i: (i, 0)) for c in cots],
        out_specs=[pl.BlockSpec((tm, tiled[idx].shape[1]), lambda i: (i, 0)) for idx in diff_tiled]
        + [pl.BlockSpec(p.shape, lambda i: (0, 0)) for p in params],
        out_shape=[jax.ShapeDtypeStruct(tiled[idx].shape, dt) for idx, dt in zip(diff_tiled, d_tiled_dtypes)]
        + [jax.ShapeDtypeStruct(p.shape, F32) for p in params],
        compiler_params=_params(),
    )(*tiled, *params, *cots)
    return outs[:len(diff_tiled)], outs[len(diff_tiled):]


def _gla_head(q, k, v, r, z, g, st):
    c = q.shape[0]
    causal = _iota2((c, c), 0) >= _iota2((c, c), 1)
    la = _log_sigmoid(z) * (1.0 / GLA_TAU)
    big_l = cumsum_rows(la)
    ep, en = jnp.exp(big_l), jnp.exp(-big_l)
    qs = q * (GLA_DK ** -0.5)
    qp = qs * ep
    s = jnp.where(causal, mm_nt(qp, k * en), mm_nt(qs * en, k * ep))
    o = mm_nn(s, v) + mm_nt(qp, st)
    l_end = jnp.sum(la, axis=0, keepdims=True)
    st_new = st * jnp.exp(l_end) + mm_tn(v, k * jnp.exp(l_end - big_l))
    u = _rms_norm(o, g) * (r * jax.nn.sigmoid(r))
    return u, st_new


def _gla_slices(h):
    q = slice(GLA_DK * h, GLA_DK * (h + 1))
    k = slice(GLA_HK + GLA_DK * h, GLA_HK + GLA_DK * (h + 1))
    v = slice(2 * GLA_HK + GLA_DV * h, 2 * GLA_HK + GLA_DV * (h + 1))
    r = slice(2 * GLA_HK + GLA_HV + GLA_DV * h, 2 * GLA_HK + GLA_HV + GLA_DV * (h + 1))
    return q, k, v, r


GLA_CHUNKS_PER_STEP = 2


def _gla_fwd(proj, z, norm_g, after):
    t = proj.shape[0]
    nc, per = t // CHUNK, GLA_CHUNKS_PER_STEP
    rows_per_step = per * CHUNK
    after = [a for a in after if a is not None]

    def body(proj_ref, z_ref, g_ref, *rest):
        u_ref, st_save_ref, st_ref = rest[len(after):]

        @pl.when(pl.program_id(0) == 0)
        def _():
            st_ref[...] = jnp.zeros_like(st_ref)

        g = g_ref[...]
        for h in range(GLA_HEADS):
            sq, sk, sv, sr = _gla_slices(h)
            st = st_ref[h]
            for c in range(per):
                rows = slice(c * CHUNK, (c + 1) * CHUNK)
                st_save_ref[c, h] = st
                u, st = _gla_head(proj_ref[rows, sq].astype(F32), proj_ref[rows, sk].astype(F32),
                                  proj_ref[rows, sv].astype(F32), proj_ref[rows, sr].astype(F32),
                                  z_ref[rows, GLA_DK * h:GLA_DK * (h + 1)], g, st)
                u_ref[rows, GLA_DV * h:GLA_DV * (h + 1)] = u.astype(u_ref.dtype)
            st_ref[h] = st

    return pl.pallas_call(
        body, name="gla_fwd", grid=(nc // per,),
        in_specs=[pl.BlockSpec((rows_per_step, GLA_MAIN), lambda i: (i, 0)),
                  pl.BlockSpec((rows_per_step, GLA_HK), lambda i: (i, 0)), pl.BlockSpec((1, GLA_DV), lambda i: (0, 0))]
        + [pl.BlockSpec(memory_space=pl.ANY)] * len(after),
        out_specs=[pl.BlockSpec((rows_per_step, GLA_HV), lambda i: (i, 0)),
                   pl.BlockSpec((per, GLA_HEADS, GLA_DV, GLA_DK), lambda i: (i, 0, 0, 0))],
        out_shape=[jax.ShapeDtypeStruct((t, GLA_HV), BF16), jax.ShapeDtypeStruct((nc, GLA_HEADS, GLA_DV, GLA_DK), F32)],
        scratch_shapes=[pltpu.VMEM((GLA_HEADS, GLA_DV, GLA_DK), F32)],
        compiler_params=_params(),
    )(proj, z, norm_g, *after)


def _gla_bwd(proj, z, norm_g, states, du, after):
    t = proj.shape[0]
    nc, per = t // CHUNK, GLA_CHUNKS_PER_STEP
    rows_per_step = per * CHUNK
    n_steps = nc // per
    after = [a for a in after if a is not None]

    def body(proj_ref, z_ref, g_ref, st_in_ref, du_ref, *rest):
        dproj_ref, dz_ref, dg_ref, dzsum_ref, dst_ref = rest[len(after):]

        @pl.when(pl.program_id(0) == 0)
        def _():
            dst_ref[...] = jnp.zeros_like(dst_ref)
            dg_ref[...] = jnp.zeros_like(dg_ref)
            dzsum_ref[...] = jnp.zeros_like(dzsum_ref)

        g = g_ref[...]
        for h in range(GLA_HEADS):
            sq, sk, sv, sr = _gla_slices(h)
            dst = dst_ref[h]
            for c in reversed(range(per)):
                rows = slice(c * CHUNK, (c + 1) * CHUNK)
                ins = (proj_ref[rows, sq].astype(F32), proj_ref[rows, sk].astype(F32), proj_ref[rows, sv].astype(F32),
                       proj_ref[rows, sr].astype(F32), z_ref[rows, GLA_DK * h:GLA_DK * (h + 1)], g, st_in_ref[c, h])
                _, vjp = jax.vjp(_gla_head, *ins)
                dq, dk, dv, dr, dz, dg, dst = vjp((du_ref[rows, GLA_DV * h:GLA_DV * (h + 1)], dst))
                dproj_ref[rows, sq] = dq.astype(dproj_ref.dtype)
                dproj_ref[rows, sk] = dk.astype(dproj_ref.dtype)
                dproj_ref[rows, sv] = dv.astype(dproj_ref.dtype)
                dproj_ref[rows, sr] = dr.astype(dproj_ref.dtype)
                dz_ref[rows, GLA_DK * h:GLA_DK * (h + 1)] = dz
                dzsum_ref[:, GLA_DK * h:GLA_DK * (h + 1)] += jnp.sum(dz, axis=0, keepdims=True)
                dg_ref[...] += dg
            dst_ref[h] = dst

    rev = lambda i: (n_steps - 1 - i, 0)
    return pl.pallas_call(
        body, name="gla_bwd", grid=(n_steps,),
        in_specs=[pl.BlockSpec((rows_per_step, GLA_MAIN), rev), pl.BlockSpec((rows_per_step, GLA_HK), rev),
                  pl.BlockSpec((1, GLA_DV), lambda i: (0, 0)),
                  pl.BlockSpec((per, GLA_HEADS, GLA_DV, GLA_DK), lambda i: (n_steps - 1 - i, 0, 0, 0)),
                  pl.BlockSpec((rows_per_step, GLA_HV), rev)] + [pl.BlockSpec(memory_space=pl.ANY)] * len(after),
        out_specs=[pl.BlockSpec((rows_per_step, GLA_MAIN), rev), pl.BlockSpec((rows_per_step, GLA_HK), rev),
                   pl.BlockSpec((1, GLA_DV), lambda i: (0, 0)), pl.BlockSpec((1, GLA_HK), lambda i: (0, 0))],
        out_shape=[jax.ShapeDtypeStruct((t, GLA_MAIN), BF16), jax.ShapeDtypeStruct((t, GLA_HK), F32),
                   jax.ShapeDtypeStruct((1, GLA_DV), F32), jax.ShapeDtypeStruct((1, GLA_HK), F32)],
        scratch_shapes=[pltpu.VMEM((GLA_HEADS, GLA_DV, GLA_DK), F32)],
        compiler_params=_params(),
    )(proj, z, norm_g, states, du, *after)


def _mla_pre(cq, cos, sin, gq, gkv, w_uq, w_ukv):
    qlat = _rms_norm(cq[:, :MLA_RANK], gq)
    kvlat = _rms_norm(cq[:, MLA_RANK:2 * MLA_RANK], gkv)
    kr = cq[:, 2 * MLA_RANK:]
    q = mm_nn(qlat, w_uq) * ((MLA_NOPE + MLA_ROPE) ** -0.5)
    kv = mm_nn(kvlat, w_ukv)
    pieces = []
    for h in range(MLA_HEADS):
        qr = q[:, MLA_HEAD_PAD * h + MLA_NOPE:MLA_HEAD_PAD * (h + 1)]
        pieces += [q[:, MLA_HEAD_PAD * h:MLA_HEAD_PAD * h + MLA_NOPE], qr * cos + rot_half(qr) * sin]
    return jnp.concatenate(pieces, axis=1), kv, kr * cos + rot_half(kr) * sin


MLA_Q_TILE = 256


def _mla_attn_block(qn, qr, kv, kr, q0):
    tq, nk = qn.shape[0], kv.shape[0]
    s = mm_nt(qn, kv[:, :MLA_NOPE]) + mm_nt(qr, kr)
    visible = (_iota2((tq, nk), 1) // CHUNK) <= ((q0 + _iota2((tq, nk), 0)) // CHUNK)
    s = jnp.where(visible, s, -1e30)
    e = jnp.exp(s - jnp.max(s, -1, keepdims=True))
    p = e / jnp.sum(e, -1, keepdims=True)
    return mm_nn(p, kv[:, MLA_NOPE:])


def _mla_attn_fwd(q, kv, kr, after):
    t = q.shape[0]
    after = [a for a in after if a is not None]

    def body(q_ref, kv_ref, kr_ref, *rest):
        (o_ref,) = rest[len(after):]
        for i in range(t // MLA_Q_TILE):
            rows = slice(i * MLA_Q_TILE, (i + 1) * MLA_Q_TILE)
            keys = slice(0, (i + 1) * MLA_Q_TILE)
            o = _mla_attn_block(q_ref[rows, :MLA_NOPE].astype(F32), q_ref[rows, MLA_NOPE:].astype(F32),
                                kv_ref[keys, :].astype(F32), kr_ref[keys, :].astype(F32), i * MLA_Q_TILE)
            o_ref[rows, :] = o.astype(o_ref.dtype)

    return pl.pallas_call(
        body, name="mla_attn_fwd", grid=(MLA_HEADS,),
        in_specs=[pl.BlockSpec((t, MLA_HEAD_PAD), lambda h: (0, h)),
                  pl.BlockSpec((t, MLA_NOPE + MLA_V), lambda h: (0, h)), pl.BlockSpec((t, LANES), lambda h: (0, 0))]
        + [pl.BlockSpec(memory_space=pl.ANY)] * len(after),
        out_specs=pl.BlockSpec((t, MLA_V), lambda h: (0, h)),
        out_shape=jax.ShapeDtypeStruct((t, MLA_HEADS * MLA_V), BF16),
        compiler_params=_params(),
    )(q, kv, kr, *after)


def _mla_attn_bwd(q, kv, kr, do, after):
    t = q.shape[0]
    after = [a for a in after if a is not None]

    def body(q_ref, kv_ref, kr_ref, do_ref, *rest):
        dq_ref, dkv_ref, dkr_ref = rest[len(after):]
        dkv_ref[...] = jnp.zeros_like(dkv_ref)

        @pl.when(pl.program_id(0) == 0)
        def _():
            dkr_ref[...] = jnp.zeros_like(dkr_ref)

        for i in range(t // MLA_Q_TILE):
            rows = slice(i * MLA_Q_TILE, (i + 1) * MLA_Q_TILE)
            keys = slice(0, (i + 1) * MLA_Q_TILE)
            f = functools.partial(_mla_attn_block, q0=i * MLA_Q_TILE)
            _, vjp = jax.vjp(f, q_ref[rows, :MLA_NOPE].astype(F32), q_ref[rows, MLA_NOPE:].astype(F32),
                             kv_ref[keys, :].astype(F32), kr_ref[keys, :].astype(F32))
            dqn, dqr, dkv, dkr = vjp(do_ref[rows, :].astype(F32))
            dq_ref[rows, :MLA_NOPE] = dqn
            dq_ref[rows, MLA_NOPE:] = dqr
            dkv_ref[keys, :] += dkv
            dkr_ref[keys, :] += dkr

    return pl.pallas_call(
        body, name="mla_attn_bwd", grid=(MLA_HEADS,),
        in_specs=[pl.BlockSpec((t, MLA_HEAD_PAD), lambda h: (0, h)),
                  pl.BlockSpec((t, MLA_NOPE + MLA_V), lambda h: (0, h)), pl.BlockSpec((t, LANES), lambda h: (0, 0)),
                  pl.BlockSpec((t, MLA_V), lambda h: (0, h))] + [pl.BlockSpec(memory_space=pl.ANY)] * len(after),
        out_specs=[pl.BlockSpec((t, MLA_HEAD_PAD), lambda h: (0, h)),
                   pl.BlockSpec((t, MLA_NOPE + MLA_V), lambda h: (0, h)), pl.BlockSpec((t, LANES), lambda h: (0, 0))],
        out_shape=[jax.ShapeDtypeStruct(q.shape, F32), jax.ShapeDtypeStruct(kv.shape, F32),
                   jax.ShapeDtypeStruct(kr.shape, F32)],
        compiler_params=_params(),
    )(q, kv, kr, do, *after)


def _rope_tables(pos_col, inv_freq_row):
    t = pos_col.shape[0]

    def body(pos_ref, f_ref, cos_ref, sin_ref):
        ang = pos_ref[...].astype(F32) * f_ref[...]
        live = _iota2(ang.shape, 1) < MLA_ROPE
        cos_ref[...] = jnp.where(live, jnp.cos(ang), 0.0)
        sin_ref[...] = jnp.where(live, jnp.sin(ang), 0.0)

    return pl.pallas_call(
        body, name="rope_tables", out_shape=[jax.ShapeDtypeStruct((t, LANES), F32)] * 2, compiler_params=_params(),
    )(pos_col, inv_freq_row)


CONV_COL_TILE = 256


def _conv_gate(b, c, u, w0, w1, w2):
    cu = c * u
    return b * (w2 * cu + w1 * shift_rows(cu, 1) + w0 * shift_rows(cu, 2))


def _conv_specs(t):
    nb = D_MODEL // CONV_COL_TILE
    return [pl.BlockSpec((t, CONV_COL_TILE), lambda j, part=part: (0, part * nb + j)) for part in range(3)]


def _conv_fwd(bcu, w, after):
    t = bcu.shape[0]
    after = [a for a in after if a is not None]

    def body(b_ref, c_ref, u_ref, w_ref, *rest):
        (o_ref,) = rest[len(after):]
        o_ref[...] = _conv_gate(b_ref[...], c_ref[...], u_ref[...], w_ref[0:1, :], w_ref[1:2, :],
                                w_ref[2:3, :]).astype(o_ref.dtype)

    return pl.pallas_call(
        body, name="conv_fwd", grid=(D_MODEL // CONV_COL_TILE,),
        in_specs=_conv_specs(t) + [pl.BlockSpec((3, CONV_COL_TILE), lambda j: (0, j))]
        + [pl.BlockSpec(memory_space=pl.ANY)] * len(after),
        out_specs=pl.BlockSpec((t, CONV_COL_TILE), lambda j: (0, j)),
        out_shape=jax.ShapeDtypeStruct((t, D_MODEL), BF16), compiler_params=_params(),
    )(bcu, bcu, bcu, w, *after)


def _conv_bwd(bcu, w, dout, after):
    t = bcu.shape[0]
    after = [a for a in after if a is not None]

    def body(b_ref, c_ref, u_ref, w_ref, do_ref, *rest):
        db_ref, dc_ref, du_ref, dw_ref = rest[len(after):]
        _, vjp = jax.vjp(_conv_gate, b_ref[...], c_ref[...], u_ref[...], w_ref[0:1, :], w_ref[1:2, :], w_ref[2:3, :])
        db, dc, du, dw0, dw1, dw2 = vjp(do_ref[...])
        db_ref[...] = db.astype(db_ref.dtype)
        dc_ref[...] = dc.astype(dc_ref.dtype)
        du_ref[...] = du.astype(du_ref.dtype)
        dw_ref[0:1, :] = dw0
        dw_ref[1:2, :] = dw1
        dw_ref[2:3, :] = dw2

    col = pl.BlockSpec((t, CONV_COL_TILE), lambda j: (0, j))
    return pl.pallas_call(
        body, name="conv_bwd", grid=(D_MODEL // CONV_COL_TILE,),
        in_specs=_conv_specs(t) + [pl.BlockSpec((3, CONV_COL_TILE), lambda j: (0, j)), col]
        + [pl.BlockSpec(memory_space=pl.ANY)] * len(after),
        out_specs=[col, col, col, pl.BlockSpec((3, CONV_COL_TILE), lambda j: (0, j))],
        out_shape=[jax.ShapeDtypeStruct((t, D_MODEL), BF16)] * 3 + [jax.ShapeDtypeStruct((3, D_MODEL), F32)],
        compiler_params=_params(),
    )(bcu, bcu, bcu, w, dout, *after)


def _loss_head(y, target):
    t, d = y.shape
    tm = 256

    def body(y_ref, t_ref, loss_ref, dy_ref):
        @pl.when(pl.program_id(0) == 0)
        def _():
            loss_ref[...] = jnp.zeros_like(loss_ref)

        err = y_ref[...] - t_ref[...]
        dy_ref[...] = err * (1.0 / d)
        loss_ref[...] += 0.5 * jnp.sum(jnp.sum(err * err, axis=-1, keepdims=True) * (1.0 / d))

    tile = pl.BlockSpec((tm, d), lambda i: (i, 0))
    return pl.pallas_call(
        body, name="loss_head", grid=(t // tm,), in_specs=[tile, tile],
        out_specs=[pl.BlockSpec((8, LANES), lambda i: (0, 0)), tile],
        out_shape=[jax.ShapeDtypeStruct((8, LANES), F32), jax.ShapeDtypeStruct((t, d), F32)],
        compiler_params=_params(),
    )(y, target)


def _ln_epi(acc, res, g, b):
    a = ALPHA * res + acc
    y = _layer_norm(a, g, b)
    return a, y, y


def _ln_fn(a, g, b):
    return (_layer_norm(a, g, b),)


def _ln_bwd_epi(scale):
    def epi(acc, res, a, g, b):
        _, vjp = jax.vjp(_ln_fn, a, g, b)
        da, dg, db = vjp((acc + scale * res,))
        return da, da, dg, db
    return epi


def _relu_sq(h):
    r = jnp.maximum(h, 0)
    return r * r


def _pad_cols(w, n):
    return jnp.pad(w, ((0, 0), (0, n - w.shape[1])))


def _pad_rows(w, n):
    return jnp.pad(w, ((0, n - w.shape[0]), (0, 0)))


def _step(x, p, positions, target, small, comm):
    t = x.shape[0]
    w = small
    freqs = ROPE_BASE ** (-jnp.arange(0, MLA_ROPE // 2, dtype=F32) * (2.0 / MLA_ROPE))
    freq_row = jnp.concatenate([freqs, freqs, jnp.zeros((LANES - MLA_ROPE,), F32)])[None, :]
    cos, sin = _rope_tables(positions.reshape(t, 1), freq_row)

    saved = []
    xb = x.astype(BF16)
    for i in range(DEPTH):
        j, kind = i // 3, i % 3
        wl = comm.mixer_weights(i)
        s = {"x": xb, "wl": wl}
        tok = comm.at("fwd", i, "begin", x)
        if kind == 0:
            s["w_main"] = wl["gla_w_in_t"][:GLA_MAIN]
            s["w_lr"] = _pad_rows(wl["gla_w_in_t"][GLA_MAIN:], LANES)
            s["w_up"] = _pad_rows(w["gla_w_gate_up"][j], LANES).astype(BF16)
            s["proj"] = _matmul(xb, s["w_main"], name="gla_proj", tb=True, tn=1024, after=tok)
            s["glr"] = _matmul(xb, s["w_lr"], name="gla_lr", tb=True, out_dtypes=(F32,))
            s["z"] = _matmul(s["glr"], s["w_up"], name="gla_gate", epi=lambda acc, b: (acc + b,),
                             epi_ins=(w["gla_b_gate"][j][None, :],), out_dtypes=(F32,))
            tok = comm.at("fwd", i, "proj_done", s["z"]) or []
            s["u"], s["states"] = _gla_fwd(s["proj"], s["z"], w["gla_norm_g"][j][None, :], tok)
        elif kind == 1:
            s["cq"] = _matmul(xb, wl["mla_in"], name="mla_proj", tn=MLA_IN_PAD, b_at=REG_MLA_IN, out_dtypes=(F32,),
                              after=tok)
            s["pre_params"] = (w["mla_q_norm"][j][None, :], w["mla_kv_norm"][j][None, :], wl["mla_w_uq"], wl["mla_w_ukv"])
            s["q"], s["kv"], s["kr"] = _tile_fwd(_mla_pre, (s["cq"], cos, sin), s["pre_params"], (BF16, BF16, BF16),
                                                 tm=256, name="mla_pre_fwd")
            tok = comm.at("fwd", i, "proj_done", s["kv"]) or []
            s["u"] = _mla_attn_fwd(s["q"], s["kv"], s["kr"], tok)
        else:
            s["bcu"] = _matmul(xb, wl["conv"], name="conv_proj", tb=True, tm=256, tn=3 * D_MODEL, b_at=REG_CONV,
                               out_dtypes=(F32,), after=tok)
            tok = comm.at("fwd", i, "proj_done", s["bcu"]) or []
            s["u"] = _conv_fwd(s["bcu"], w["conv_w"][j], tok)
        g0, b0 = w["ln_g"][i, 0][None, :], w["ln_b"][i, 0][None, :]
        g1, b1 = w["ln_g"][i, 1][None, :], w["ln_b"][i, 1][None, :]
        wa, wb = s["wa"], _ = comm.slab_weights(i, s["u"])
        s["a1"], s["x1"], s["x1b"] = _matmul(s["u"], wa, name="mixer_out_ln", tm=256, tn=D_MODEL, b_at=REG_WOUT,
                                             epi=_ln_epi, epi_ins=(x, g0, b0), out_dtypes=(F32, F32, BF16))
        s["hh"] = _matmul(s["x1b"], wa, name="mlp_up", tb=True, tm=256, tn=D_FF, b_at=REG_W1T)
        tok = comm.at("fwd", i, "mid", s["hh"])
        s["a2"], s["x2"], s["x2b"] = _matmul(s["hh"], wa, name="mlp_down_ln", tm=256, tn=D_MODEL, b_at=REG_W2,
                                             a_fn=_relu_sq, epi=_ln_epi, epi_ins=(s["x1"], g1, b1),
                                             out_dtypes=(F32, F32, BF16), after=tok)
        s["pp"] = _matmul(p[i], wb, name="ple_proj", tb=True, tn=D_MODEL, b_at=REG_WPT)
        tok = comm.at("fwd", i, "end", s["pp"])
        def ple_epi(acc, xr, pp):
            y = xr + jax.nn.sigmoid(acc) * pp.astype(F32)
            return y, y, acc

        x, xb, s["gt"] = _matmul(s["x2b"], wa, name="ple_gate", tn=1024, b_at=REG_WG, epi=ple_epi,
                                 epi_ins=(s["x2"], s["pp"]), out_dtypes=(F32, BF16, BF16), after=tok)
        saved.append(s)

    loss_part, dx = _loss_head(x, target)

    gw = {n: [None] * WEIGHTS[n][0][0] for n in SMALL + REPLICATED}
    ln_g_grads, ln_b_grads = [[None, None] for _ in range(DEPTH)], [[None, None] for _ in range(DEPTH)]
    resid = lambda acc, r: (acc + ALPHA * r,)
    plus = lambda acc, r: (acc + r,)
    for i in reversed(range(DEPTH)):
        j, kind = i // 3, i % 3
        s = saved[i]
        wa = s["wa"]
        ga = lax.empty((N_DEV, A_ROWS, D_MODEL), BF16)
        gb = lax.empty((N_DEV, REG_WPT[1], PLE_DIM), BF16)
        layer_grads = {}
        tok = comm.at("bwd", i, "begin", dx)

        def ple_bwd(dxo, gt, pp):
            sg = jax.nn.sigmoid(gt.astype(F32))
            d_gt = dxo * pp.astype(F32) * sg * (1.0 - sg)
            return d_gt, d_gt, dxo * sg

        g1, b1 = w["ln_g"][i, 1][None, :], w["ln_b"][i, 1][None, :]
        d_a2, d_a2b, d_gt, d_pp, ln_g_grads[i][1], ln_b_grads[i][1] = _matmul(
            dx, wa, name="ple_gate_dx_ln", tb=True, tm=256, tn=D_MODEL, b_at=REG_WG, a_fn=ple_bwd,
            a_ins=(s["gt"], s["pp"]), a_out_dtypes=(BF16, BF16), epi=_ln_bwd_epi(1.0), epi_ins=(dx, s["a2"], g1, b1),
            out_dtypes=(F32, BF16), n_row_sums=2, after=tok)
        gb = _matmul(d_pp, p[i], name="ple_proj_dw", ta=True, tm=512, tn=PLE_DIM, out_at=REG_WPT, out_buf=gb)
        ga = _matmul(s["x2b"], d_gt, name="ple_gate_dw", ta=True, tm=512, tn=1024, out_at=REG_WG, out_buf=ga)
        tok = comm.at("bwd", i, "ln", d_a2)
        ga = _matmul(s["hh"], d_a2b, name="mlp_down_dw", ta=True, tm=REG_W2[1], tn=1024, a_fn=_relu_sq, out_at=REG_W2,
                     out_buf=ga, after=tok)
        d_hh = _matmul(d_a2b, wa, name="mlp_down_dx", tb=True, tm=256, tn=D_FF, b_at=REG_W2, after=[ga, gb],
                       epi=lambda acc, hh: (acc * 2.0 * jnp.maximum(hh.astype(F32), 0.0),), epi_ins=(s["hh"],))
        ga = _matmul(d_hh, s["x1b"], name="mlp_up_dw", ta=True, tm=REG_W1T[1], tn=1024, out_at=REG_W1T, out_buf=ga)
        g0, b0 = w["ln_g"][i, 0][None, :], w["ln_b"][i, 0][None, :]
        d_a1, d_a1b, ln_g_grads[i][0], ln_b_grads[i][0] = _matmul(
            d_hh, wa, name="mlp_up_dx_ln", tm=256, tn=D_MODEL, b_at=REG_W1T, epi=_ln_bwd_epi(ALPHA),
            epi_ins=(d_a2, s["a1"], g0, b0), out_dtypes=(F32, BF16), n_row_sums=2, after=ga)
        ga = _matmul(s["u"], d_a1b, name="mixer_out_dw", ta=True, tm=512, tn=1024, out_at=REG_WOUT, out_buf=ga)
        du = _matmul(d_a1b, wa, name="mixer_out_dx", tb=True, tn=1024, b_at=REG_WOUT, out_dtypes=(F32,), after=ga)
        comm.slab_grads(i, ga, gb)
        tok = comm.at("bwd", i, "slab_done", du) or []
        if kind == 0:
            dproj, dz, dg, dz_sum = _gla_bwd(s["proj"], s["z"], w["gla_norm_g"][j][None, :], s["states"], du, tok)
            tok = comm.at("bwd", i, "mixer_done", dproj)
            gw["gla_norm_g"][j] = dg[0]
            gw["gla_b_gate"][j] = dz_sum[0]
            gw["gla_w_gate_up"][j] = _matmul(s["glr"], dz, name="gla_gate_dw", ta=True, out_dtypes=(F32,),
                                             after=tok)[:GLA_RANK]
            dglr = _matmul(dz, s["w_up"], name="gla_gate_dx", tb=True, out_dtypes=(F32,))
            dw_main = _matmul(dproj, s["x"], name="gla_proj_dw", ta=True, tn=1024, out_dtypes=(F32,))
            dw_lr = _matmul(dglr, s["x"], name="gla_lr_dw", ta=True, tn=1024, out_dtypes=(F32,))[:GLA_RANK]
            layer_grads["gla_w_in_t"] = jnp.concatenate([dw_main, dw_lr], axis=0)
            dx = _matmul(dproj, s["w_main"], name="gla_proj_dx", tn=1024, epi=resid, epi_ins=(d_a1,),
                         out_dtypes=(F32,), after=[dw_main, dw_lr, gw["gla_w_gate_up"][j]])
            dx = _matmul(dglr, s["w_lr"], name="gla_lr_dx", tn=1024, epi=plus, epi_ins=(dx,), out_dtypes=(F32,))
        elif kind == 1:
            dq, dkv, dkr = _mla_attn_bwd(s["q"], s["kv"], s["kr"], du, tok)
            tok = comm.at("bwd", i, "mixer_done", dq)
            (d_cq,), (dgq, dgkv, layer_grads["mla_uq"], layer_grads["mla_ukv"]) = _tile_bwd(
                _mla_pre, (s["cq"], cos, sin), s["pre_params"], (dq, dkv, dkr), (BF16,), tm=256, name="mla_pre_bwd",
                diff_tiled=[0])
            gw["mla_q_norm"][j], gw["mla_kv_norm"][j] = dgq[0], dgkv[0]
            layer_grads["mla_in"] = _matmul(s["x"], d_cq, name="mla_proj_dw", ta=True, tm=512, tn=MLA_IN_PAD,
                                            out_at=REG_MLA_IN, after=tok,
                                            out_buf=lax.empty((N_DEV, REG_MLA_IN[1], MLA_IN_PAD), BF16))
            dx = _matmul(d_cq, s["wl"]["mla_in"], name="mla_proj_dx", tb=True, tn=1024, b_at=REG_MLA_IN, epi=resid,
                         epi_ins=(d_a1,), out_dtypes=(F32,), after=layer_grads["mla_in"])
        else:
            db, dc, du_, dcw = _conv_bwd(s["bcu"], w["conv_w"][j], du, tok)
            tok = comm.at("bwd", i, "mixer_done", db)
            gw["conv_w"][j] = dcw
            dbcu = jnp.concatenate([db, dc, du_], axis=1)
            layer_grads["conv"] = _matmul(dbcu, s["x"], name="conv_proj_dw", ta=True, tm=REG_CONV[1], tn=1024,
                                          out_at=REG_CONV, out_buf=lax.empty((N_DEV, REG_CONV[1], D_MODEL), BF16),
                                          after=tok)
            dx = _matmul(dbcu, s["wl"]["conv"], name="conv_proj_dx", tn=1024, b_at=REG_CONV, epi=resid, epi_ins=(d_a1,),
                         out_dtypes=(F32,), after=layer_grads["conv"])
        comm.mixer_grads(i, layer_grads)

    gw["ln_g"] = [jnp.concatenate([a, b], axis=0) for a, b in ln_g_grads]
    gw["ln_b"] = [jnp.concatenate([a, b], axis=0) for a, b in ln_b_grads]
    return loss_part, dx, {n: jnp.stack(gw[n]).astype(F32) for n in gw}


MESH_IDS = pl.DeviceIdType.MESH
ANY = pl.BlockSpec(memory_space=pl.ANY)
HBM_SPEC = pl.BlockSpec(memory_space=pltpu.HBM)
SEM_SPEC = pl.BlockSpec(memory_space=pltpu.SEMAPHORE)
DATAFLOW_EFFECT = pltpu.SideEffectType.DATAFLOW_SIDE_EFFECTING
CORE_COPIES, CHIP_COPIES = 4, 3


def _my_place():
    return lax.axis_index("x"), lax.axis_index("y"), lax.axis_index("c")


def _other_chips(mx, my):
    return [(1 - mx, my), (mx, 1 - my), (1 - mx, 1 - my)]


def _remote(src, dst, send_sems, recv_sems, k, to):
    return pltpu.make_async_remote_copy(src_ref=src, dst_ref=dst, send_sem=send_sems.at[k], recv_sem=recv_sems.at[k],
                                        device_id=to, device_id_type=MESH_IDS)


def _gather_first_copies(n_arr):
    def make(bufs, send_sems, recv_sems):
        mx, my, mc = _my_place()
        mine = 4 * mx + 2 * my + mc
        peers = [(mx, my, 1 - mc)] + [(cx, cy, mc) for cx, cy in _other_chips(mx, my)]
        return [_remote(bufs[a].at[mine], bufs[a].at[mine], send_sems, recv_sems, (1 + CHIP_COPIES) * a + k, to)
                for a in range(n_arr) for k, to in enumerate(peers)]
    return make, (1 + CHIP_COPIES) * n_arr


def _gather_forward_copies(n_arr):
    def make(bufs, send_sems, recv_sems):
        mx, my, mc = _my_place()
        blocks = [4 * cx + 2 * cy + mc for cx, cy in _other_chips(mx, my)]
        return [_remote(bufs[a].at[blk], bufs[a].at[blk], send_sems, recv_sems, CHIP_COPIES * a + k, (mx, my, 1 - mc))
                for a in range(n_arr) for k, blk in enumerate(blocks)]
    return make, CHIP_COPIES * n_arr


def _scatter_core_copies(n_arr):
    def make(bufs, send_sems, recv_sems):
        mx, my, mc = _my_place()
        return [_remote(bufs[a].at[2 * k + (1 - mc)], bufs[n_arr + a].at[k], send_sems, recv_sems, CORE_COPIES * a + k,
                        (mx, my, 1 - mc)) for a in range(n_arr) for k in range(CORE_COPIES)]
    return make, CORE_COPIES * n_arr


def _scatter_chip_copies(n_arr):
    def make(bufs, send_sems, recv_sems):
        mx, my, mc = _my_place()
        return [_remote(bufs[a].at[2 * cx + cy], bufs[n_arr + a].at[k], send_sems, recv_sems, CHIP_COPIES * a + k,
                        (cx, cy, mc)) for a in range(n_arr) for k, (cx, cy) in enumerate(_other_chips(mx, my))]
    return make, CHIP_COPIES * n_arr


def _exchange(name, bufs, copies):
    make, n_copies = copies
    n = len(bufs)

    def body(*refs):
        descs = make(refs[:n], refs[2 * n], refs[2 * n + 1])
        for cp in descs:
            cp.start()
        for cp in descs:
            cp.wait()

    return pl.pallas_call(
        body, name=name, out_shape=[jax.ShapeDtypeStruct(b.shape, b.dtype) for b in bufs], in_specs=[ANY] * n,
        out_specs=[ANY] * n, input_output_aliases={i: i for i in range(n)},
        scratch_shapes=[pltpu.SemaphoreType.DMA((n_copies,)), pltpu.SemaphoreType.DMA((n_copies,))],
    )(*bufs)


def _exchange_start(name, bufs, copies, after):
    make, n_copies = copies
    n = len(bufs)

    def body(*refs):
        for cp in make(refs[:n], refs[n + 1], refs[n + 2]):
            cp.start()
        refs[-1][...] = jnp.zeros_like(refs[-1])

    outs = pl.pallas_call(
        body, name=name,
        out_shape=(pltpu.SemaphoreType.DMA((n_copies,)), pltpu.SemaphoreType.DMA((n_copies,)),
                   *[pltpu.HBM(b.shape, b.dtype) for b in bufs], jax.ShapeDtypeStruct((8, LANES), F32)),
        in_specs=[HBM_SPEC] * n + [ANY],
        out_specs=(SEM_SPEC, SEM_SPEC, *[HBM_SPEC] * n, pl.BlockSpec(memory_space=pltpu.VMEM)),
        input_output_aliases={i: 2 + i for i in range(n)},
        compiler_params=pltpu.CompilerParams(has_side_effects=DATAFLOW_EFFECT),
    )(*[pltpu.with_memory_space_constraint(b, pltpu.HBM) for b in bufs], after)
    return (outs[0], outs[1]), list(outs[2:2 + n]), outs[-1]


def _exchange_wait(name, sems, bufs, copies, after):
    make, _ = copies
    n = len(bufs)

    def body(*refs):
        for cp in make(refs[:n], refs[n], refs[n + 1]):
            cp.wait_send()
            cp.wait_recv()

    return list(pl.pallas_call(
        body, name=name, out_shape=[pltpu.HBM(b.shape, b.dtype) for b in bufs],
        in_specs=[HBM_SPEC] * n + [SEM_SPEC, SEM_SPEC, ANY], out_specs=[HBM_SPEC] * n,
        input_output_aliases={i: i for i in range(n)},
        compiler_params=pltpu.CompilerParams(has_side_effects=DATAFLOW_EFFECT),
    )(*bufs, *sems, after))


SUM_TILE_BYTES = 2 * 1024 * 1024


def _row_tile(r, c):
    best = None
    for cand in range(16, r + 1, 16):
        if r % cand == 0 and cand * c * 2 <= SUM_TILE_BYTES:
            best = cand
    return r if best is None else best


def _pair_sum(g, recv, my_c):
    _, r, c = g.shape
    tr = _row_tile(r, c)

    def body(c_ref, g_ref, r_ref, o_ref):
        o_ref[...] = (g_ref[...].astype(F32) + r_ref[...].astype(F32)).astype(o_ref.dtype)

    return pl.pallas_call(
        body, name="rs_pair_sum", out_shape=jax.ShapeDtypeStruct((4, r, c), g.dtype),
        grid_spec=pltpu.PrefetchScalarGridSpec(
            num_scalar_prefetch=1, grid=(4, r // tr),
            in_specs=[pl.BlockSpec((1, tr, c), lambda n, i, cr: (2 * n + cr[0], i, 0)),
                      pl.BlockSpec((1, tr, c), lambda n, i, cr: (n, i, 0))],
            out_specs=pl.BlockSpec((1, tr, c), lambda n, i, cr: (n, i, 0))),
        compiler_params=_params(),
    )(my_c, g, recv)


def _chip_sum(h, recv, my_chip):
    _, r, c = h.shape
    tr = _row_tile(r, c)

    def body(j_ref, h_ref, r0_ref, r1_ref, r2_ref, o_ref):
        o_ref[...] = ((h_ref[0].astype(F32) + r0_ref[0].astype(F32)) + r1_ref[0].astype(F32)) + r2_ref[0].astype(F32)

    return pl.pallas_call(
        body, name="rs_chip_sum", out_shape=jax.ShapeDtypeStruct((r, c), F32),
        grid_spec=pltpu.PrefetchScalarGridSpec(
            num_scalar_prefetch=1, grid=(r // tr,),
            in_specs=[pl.BlockSpec((1, tr, c), lambda i, jr: (jr[0], i, 0))]
            + [pl.BlockSpec((1, tr, c), lambda i, jr, n=n: (n, i, 0)) for n in range(3)],
            out_specs=pl.BlockSpec((tr, c), lambda i, jr: (i, 0))),
        compiler_params=_params(),
    )(my_chip, h, recv, recv, recv)


def _sum_blocks(g):
    n, r, c = g.shape

    def body(g_ref, o_ref):
        acc = g_ref[0]
        for k in range(1, n):
            acc = acc + g_ref[k]
        o_ref[...] = acc

    return pl.pallas_call(body, name="sum_blocks", out_shape=jax.ShapeDtypeStruct((r, c), F32), compiler_params=_params())(g)


def _pack(flat_parts, cols, row_multiple, dtype):
    flat = jnp.concatenate([f.astype(dtype) for f in flat_parts])
    per_row_block = cols * row_multiple
    padded = -(-flat.shape[0] // per_row_block) * per_row_block
    return jnp.pad(flat, (0, padded - flat.shape[0])).reshape(padded // cols, cols)


def _shard_shape(name):
    shape, axis = WEIGHTS[name]
    if axis is None:
        return shape
    return tuple(s // N_DEV if a == axis else s for a, s in enumerate(shape))


def _size(shape):
    n = 1
    for s in shape:
        n *= s
    return n


def _unshard(blocks, name):
    _, axis = WEIGHTS[name]
    return jnp.concatenate([blocks[k] for k in range(N_DEV)], axis=axis)


def _unpack_blocks(flat, names, lead):
    out, off = {}, 0
    for n in names:
        shp = _shard_shape(n)[1:] if lead else _shard_shape(n)
        out[n] = flat[..., off:off + _size(shp)].reshape(flat.shape[:-1] + shp)
        off += _size(shp)
    return out


def _layer_slabs(shard, i):
    j, kind = i // 3, i % 3
    w_out = (shard["gla_w_out"], shard["mla_w_out"], shard["conv_w_out"])[kind][j]
    out = {"a": jnp.concatenate([shard["mlp_w2"][i], shard["mlp_w1"][i].T, w_out, shard["ple_w_gate"][i]], axis=0).astype(BF16),
           "b": shard["ple_w_proj"][i].T.astype(BF16)}
    if kind == 0:
        out["gla"] = shard["gla_w_in"][j].T.astype(BF16)
    elif kind == 1:
        out["mla_in"] = _pad_cols(shard["mla_w_in"][j], MLA_IN_PAD).astype(BF16)
        out["mla_uq"] = _pad_cols(shard["mla_w_uq"][j], MLA_HEAD_PAD).astype(BF16)
        out["mla_ukv"] = shard["mla_w_ukv"][j].astype(BF16)
    else:
        out["conv"] = shard["conv_w_in"][j].T.astype(BF16)
    return out


def _mixer_weights(landed, i):
    kind = i % 3
    if kind == 0:
        return {"gla_w_in_t": landed["gla"].reshape(-1, D_MODEL)}
    if kind == 2:
        return {"conv": landed["conv"]}
    heads_side_by_side = lambda g: g.transpose(1, 0, 2).reshape(g.shape[1], -1)
    return {"mla_in": landed["mla_in"], "mla_w_uq": heads_side_by_side(landed["mla_uq"]),
            "mla_w_ukv": heads_side_by_side(landed["mla_ukv"])}


def _mixer_grad_buffers(layer_grads, i):
    kind = i % 3
    if kind == 0:
        return {"gla": layer_grads["gla_w_in_t"].reshape(N_DEV, -1, D_MODEL).astype(BF16)}
    if kind == 2:
        return {"conv": layer_grads["conv"]}
    head_blocks = lambda g: g.reshape(g.shape[0], N_DEV, -1).transpose(1, 0, 2).astype(BF16)
    return {"mla_in": layer_grads["mla_in"], "mla_uq": head_blocks(layer_grads["mla_uq"]),
            "mla_ukv": head_blocks(layer_grads["mla_ukv"])}


SLAB_KEYS = ("a", "b")


class _Overlap:
    def __init__(self, shard, small_pack):
        mx, my, mc = _my_place()
        self.my_c = mc.astype(jnp.int32).reshape(1)
        self.my_chip = (2 * mx + my).astype(jnp.int32).reshape(1)
        mine = 4 * mx + 2 * my + mc
        def landing_of(slabs):
            return {k: lax.dynamic_update_index_in_dim(lax.empty((N_DEV, *v.shape), v.dtype), v, mine, 0)
                    for k, v in slabs.items()}

        first = _layer_slabs(shard, 0)
        first["small"] = small_pack
        self.landing = [landing_of(first)]
        self.fly = {}
        self.grads = [{} for _ in range(DEPTH)]
        self.reduced = [{} for _ in range(DEPTH)]
        tok = self._gather_first(0, "mixer", shard["ln_g"])
        tok = self._gather_first(0, "slab", tok)
        shard, tok = lax.optimization_barrier((shard, tok))
        self.landing += [landing_of(_layer_slabs(shard, i)) for i in range(1, DEPTH)]
        bufs = self._wait("ag_first_mixer_l0", tok)
        self.landing[0].update(zip(self._keys(self.landing[0], "mixer"),
                                   _exchange("ag_forward_mixer_l0", bufs, _gather_forward_copies(len(bufs)))))

    @staticmethod
    def _keys(names, group):
        return [k for k in names if (k in SLAB_KEYS) == (group == "slab")]

    def _start(self, name, bufs, copies, after):
        sems, bufs, tok = _exchange_start(name + "_start", bufs, copies, after)
        self.fly[name] = (sems, bufs, copies)
        return tok

    def _wait(self, name, after):
        sems, bufs, copies = self.fly.pop(name)
        return _exchange_wait(name + "_wait", sems, bufs, copies, after)

    def mixer_weights(self, i):
        return _mixer_weights(self.landing[i], i)

    def slab_weights(self, i, dep):
        self._gather_done(i, "slab", dep)
        return self.landing[i]["a"], self.landing[i]["b"]

    def slab_grads(self, i, ga, gb):
        self.grads[i].update(a=ga, b=gb)

    def mixer_grads(self, i, layer_grads):
        self.grads[i].update(_mixer_grad_buffers(layer_grads, i))

    def at(self, phase, i, point, dep):
        toks = []
        if phase == "fwd":
            if point == "begin" and i == 0:
                toks.append(self._gather_first(1, "mixer", self.landing[0][self._keys(self.landing[0], "mixer")[0]]))
                toks.append(self._gather_first(1, "slab", toks[-1]))
            if point == "proj_done":
                toks.append(self._gather_forward(i, "slab", dep))
            if point == "mid" and i + 1 < DEPTH:
                toks.append(self._gather_forward(i + 1, "mixer", dep))
                if i + 2 < DEPTH:
                    toks.append(self._gather_first(i + 2, "mixer", dep))
                    toks.append(self._gather_first(i + 2, "slab", toks[-1]))
            if point == "end" and i + 1 < DEPTH:
                self._gather_done(i + 1, "mixer", dep)
        else:
            if point == "begin" and i + 1 < DEPTH:
                toks.append(self._scatter_cores(i + 1, "mixer", dep))
            if point == "ln" and i + 1 < DEPTH:
                toks.append(self._scatter_chips(i + 1, "mixer", dep))
            if point == "slab_done":
                if i + 1 < DEPTH:
                    self._scatter_done(i + 1, "slab", dep)
                    self._scatter_done(i + 1, "mixer", dep)
                toks.append(self._scatter_cores(i, "slab", dep))
            if point == "mixer_done":
                toks.append(self._scatter_chips(i, "slab", dep))
        return toks or None

    def _gather_first(self, i, group, after):
        bufs = [self.landing[i][k] for k in self._keys(self.landing[i], group)]
        return self._start(f"ag_first_{group}_l{i}", bufs, _gather_first_copies(len(bufs)), after)

    def _gather_forward(self, i, group, after):
        bufs = self._wait(f"ag_first_{group}_l{i}", after)
        return self._start(f"ag_forward_{group}_l{i}", bufs, _gather_forward_copies(len(bufs)), after)

    def _gather_done(self, i, group, after):
        keys = self._keys(self.landing[i], group)
        self.landing[i].update(zip(keys, self._wait(f"ag_forward_{group}_l{i}", after)))

    def _scatter_cores(self, i, group, after):
        gs = [self.grads[i][k] for k in self._keys(self.grads[i], group)]
        land = [lax.empty((4, *g.shape[1:]), g.dtype) for g in gs]
        return self._start(f"rs_cores_{group}_l{i}", gs + land, _scatter_core_copies(len(gs)), after)

    def _pair_sums(self, bufs):
        n = len(bufs) // 2
        hs = [_pair_sum(g, r, self.my_c) for g, r in zip(bufs[:n], bufs[n:])]
        return hs + [lax.empty((3, *h.shape[1:]), h.dtype) for h in hs]

    def _scatter_chips(self, i, group, after):
        bufs = self._pair_sums(self._wait(f"rs_cores_{group}_l{i}", after))
        return self._start(f"rs_chips_{group}_l{i}", bufs, _scatter_chip_copies(len(bufs) // 2), after)

    def _chip_sums(self, i, group, bufs):
        n = len(bufs) // 2
        for k, h, r in zip(self._keys(self.grads[i], group), bufs[:n], bufs[n:]):
            self.reduced[i][k] = _chip_sum(h, r, self.my_chip)

    def _scatter_done(self, i, group, after):
        self._chip_sums(i, group, self._wait(f"rs_chips_{group}_l{i}", after))

    def tail_begin(self, dep):
        return self._scatter_cores(0, "mixer", dep)

    def tail_middle(self, dep):
        self._scatter_done(0, "slab", dep)
        return self._scatter_chips(0, "mixer", dep)

    def tail_end(self, dep):
        self._scatter_done(0, "mixer", dep)


def _small_gather_start(x, name, after):
    mx, my, mc = _my_place()
    land = lax.dynamic_update_index_in_dim(lax.empty((N_DEV, *x.shape), x.dtype), x, 4 * mx + 2 * my + mc, 0)
    return name, _exchange_start(name + "_first_start", [land], _gather_first_copies(1), after)


def _small_gather_finish(started, after):
    name, (sems, bufs, _) = started
    bufs = _exchange_wait(name + "_first_wait", sems, bufs, _gather_first_copies(1), after)
    return _exchange(name + "_forward", bufs, _gather_forward_copies(1))[0]


def _adamw_math(w, g, m, v):
    m2 = ADAM_B1 * m + (1.0 - ADAM_B1) * g
    v2 = ADAM_B2 * v + (1.0 - ADAM_B2) * (g * g)
    m_hat = m2 / (1.0 - ADAM_B1 ** ADAM_STEP)
    v_hat = v2 / (1.0 - ADAM_B2 ** ADAM_STEP)
    return -ADAM_LR * (m_hat / (jnp.sqrt(v_hat) + ADAM_EPS) + ADAM_WD * w), m2, v2


ADAMW_TILE_BYTES = 1024 * 1024


def _adamw_layer(name, w, m, v, j, g, g_at, transposed, chain, after):
    n_layers, r, c = w.shape
    tr = max(t for t in range(8, r + 1, 8) if r % t == 0 and (t * c * 4 <= ADAMW_TILE_BYTES or t == 8))
    rb, rows = g_at
    if transposed:
        assert rows == c and g.shape[1] == r, (name, g.shape, g_at)
        g_spec = pl.BlockSpec((rows, tr), lambda i: (rb, i))
    else:
        assert rows == r and g.shape[1] == c, (name, g.shape, g_at)
        g_spec = pl.BlockSpec((tr, c), lambda i: (rb * (r // tr) + i, 0))
    extra = list(chain or []) + [a for a in (after or []) if a is not None]
    n_chain = 4 if chain else 0

    def body(w_ref, m_ref, v_ref, g_ref, *rest):
        g_out, d_out, m_out, v_out, tok_ref = rest[len(extra):]
        gv = g_ref[...].T if transposed else g_ref[...]
        g_out[0] = gv
        d_out[0], m_out[0], v_out[0] = _adamw_math(w_ref[0], gv, m_ref[0], v_ref[0])
        tok_ref[...] = jnp.zeros_like(tok_ref)

    layer_spec = pl.BlockSpec((1, tr, c), lambda i: (j, i, 0))
    outs = pl.pallas_call(
        body, name=f"adamw_{name}_l{j}", grid=(r // tr,),
        in_specs=[layer_spec] * 3 + [g_spec] + [pl.BlockSpec(memory_space=pl.ANY)] * len(extra),
        out_specs=[layer_spec] * 4 + [pl.BlockSpec((8, LANES), lambda i: (0, 0))],
        out_shape=[jax.ShapeDtypeStruct(w.shape, F32)] * 4 + [jax.ShapeDtypeStruct((8, LANES), F32)],
        input_output_aliases={4 + k: k for k in range(n_chain)}, compiler_params=_params(),
    )(w, m, v, g, *extra)
    return list(outs[:4]), outs[4]


def _adamw(w, g, m, v, name):
    shape = w.shape
    cols = shape[-1]
    rows = _size(shape) // cols
    tr = rows
    for cand in (512, 256, 128, 64, 32, 16, 8):
        if rows > cand and rows % cand == 0:
            tr = cand
            break

    def body(w_ref, g_ref, m_ref, v_ref, d_ref, mo_ref, vo_ref):
        d_ref[...], mo_ref[...], vo_ref[...] = _adamw_math(w_ref[...], g_ref[...], m_ref[...], v_ref[...])

    spec = pl.BlockSpec((tr, cols), lambda i: (i, 0))
    outs = pl.pallas_call(
        body, name="adamw_" + name, grid=(rows // tr,), in_specs=[spec] * 4, out_specs=[spec] * 3,
        out_shape=[jax.ShapeDtypeStruct((rows, cols), F32)] * 3, compiler_params=_params(),
    )(*[a.reshape(rows, cols) for a in (w, g, m, v)])
    return [o.reshape(shape) for o in outs]


def kernel(x, p, positions, gla_w_in, gla_w_gate_up, gla_b_gate, gla_norm_g, gla_w_out, mla_w_in, mla_q_norm, mla_kv_norm, mla_w_uq, mla_w_ukv, mla_w_out, conv_w_in, conv_w, conv_w_out, ln_g, ln_b, mlp_w1, mlp_w2, ple_w_gate, ple_w_proj, loss_target, m_gla_w_in, m_gla_w_gate_up, m_gla_b_gate, m_gla_norm_g, m_gla_w_out, m_mla_w_in, m_mla_q_norm, m_mla_kv_norm, m_mla_w_uq, m_mla_w_ukv, m_mla_w_out, m_conv_w_in, m_conv_w, m_conv_w_out, m_ln_g, m_ln_b, m_mlp_w1, m_mlp_w2, m_ple_w_gate, m_ple_w_proj, v_gla_w_in, v_gla_w_gate_up, v_gla_b_gate, v_gla_norm_g, v_gla_w_out, v_mla_w_in, v_mla_q_norm, v_mla_kv_norm, v_mla_w_uq, v_mla_w_ukv, v_mla_w_out, v_conv_w_in, v_conv_w, v_conv_w_out, v_ln_g, v_ln_b, v_mlp_w1, v_mlp_w2, v_ple_w_gate, v_ple_w_proj):
    args = locals()
    shard = {n: args[n] for n in WEIGHT_NAMES}
    mom = {n: args["m_" + n] for n in WEIGHT_NAMES}
    var = {n: args["v_" + n] for n in WEIGHT_NAMES}
    mx, my, mc = _my_place()

    comm = _Overlap(shard, _pack([shard[n].reshape(-1) for n in SMALL], LANES, 8, F32))
    small_all = comm.landing[0]["small"]
    small = {n: shard[n] for n in REPLICATED}
    small.update({n: _unshard(blk, n) for n, blk in _unpack_blocks(small_all.reshape(N_DEV, -1), SMALL, lead=False).items()})
    loss_part, grad_x, small_grads = _step(x[0], p[:, 0], positions[0], loss_target[0], small, comm)

    chains = {}

    def update(name, j, g, g_at, transposed, tok):
        chains[name], tok = _adamw_layer(name, shard[name], mom[name], var[name], j, g, g_at, transposed,
                                         chains.get(name), [tok])
        return tok

    def update_layer(i, groups, tok):
        j, kind = i // 3, i % 3
        red = comm.reduced[i]
        if "slab" in groups:
            tok = update("mlp_w2", i, red["a"], REG_W2, False, tok)
            tok = update("mlp_w1", i, red["a"], REG_W1T, True, tok)
            tok = update(("gla_w_out", "mla_w_out", "conv_w_out")[kind], j, red["a"], REG_WOUT, False, tok)
            tok = update("ple_w_gate", i, red["a"], REG_WG, False, tok)
            tok = update("ple_w_proj", i, red["b"], REG_WPT, True, tok)
        if "mixer" in groups:
            if kind == 0:
                tok = update("gla_w_in", j, red["gla"].T, (0, D_MODEL), False, tok)
            elif kind == 2:
                tok = update("conv_w_in", j, red["conv"], REG_CONV, True, tok)
            else:
                for n, g in (("mla_w_in", red["mla_in"][:, :MLA_IN]), ("mla_w_ukv", red["mla_ukv"]),
                             ("mla_w_uq", red["mla_uq"][:, :MLA_NOPE + MLA_ROPE])):
                    tok = update(n, j, g, (0, g.shape[0]), False, tok)
        return tok

    tok = comm.tail_begin(grad_x)
    tok = update_layer(3, ("slab", "mixer"), tok)
    tok = update_layer(2, ("slab", "mixer"), tok)
    tok = comm.tail_middle(tok)
    small_parts = [loss_part[0, :1]] + [small_grads[n].reshape(-1) for n in SMALL + REPLICATED]
    small_gather = _small_gather_start(_pack(small_parts, LANES, 8, F32), "ag_small_grads", tok)
    tok = update_layer(1, ("slab", "mixer"), small_gather[1][2])
    tok = update_layer(0, ("slab",), tok)
    comm.tail_end(tok)
    tok = update_layer(0, ("mixer",), tok)
    red_small = _sum_blocks(_small_gather_finish(small_gather, tok)).reshape(-1)
    loss = red_small[0]
    off = 1
    dev = 4 * mx + 2 * my + mc
    for n in SMALL + REPLICATED:
        shape, axis = WEIGHTS[n]
        full_g = red_small[off:off + _size(shape)].reshape(shape)
        off += _size(shape)
        if axis is not None:
            width = shape[axis] // N_DEV
            full_g = lax.dynamic_slice_in_dim(full_g, dev * width, width, axis=axis)
        chains[n] = [full_g, *_adamw(shard[n], full_g, mom[n], var[n], n)]
    return (loss, grad_x[None], *[chains[n][k] for k in range(4) for n in WEIGHT_NAMES])
```

```python
import functools

import jax
import jax.numpy as jnp
from jax import lax
from jax.experimental import pallas as pl
from jax.experimental.pallas import tpu as pltpu

F32, BF16 = jnp.float32, jnp.bfloat16
N_DEV = 8

D_MODEL = 1024
DEPTH = 4
CHUNK = 64
ALPHA = (2 * DEPTH) ** 0.25
LN_EPS = 1e-5
RMS_EPS = 1e-6
PLE_DIM = 256
D_FF = 4 * D_MODEL
GLA_HEADS = 4
GLA_DK = 128
GLA_DV = 256
GLA_RANK = 16
GLA_TAU = 16.0
GLA_HK = GLA_HEADS * GLA_DK
GLA_HV = GLA_HEADS * GLA_DV
GLA_MAIN = 2 * GLA_HK + GLA_HV + D_MODEL
MLA_HEADS = 8
MLA_NOPE = 128
MLA_ROPE = 64
MLA_V = 128
MLA_RANK = 256
MLA_IN = 2 * MLA_RANK + MLA_ROPE
MLA_IN_PAD = 640
ROPE_BASE = 10000.0
LANES = 128
ADAM_LR, ADAM_B1, ADAM_B2, ADAM_EPS, ADAM_WD, ADAM_STEP = 0.001, 0.9, 0.999, 1e-08, 0.01, 10

V7X_VMEM_LIMIT_BYTES = 56 * 1024 * 1024

WEIGHTS = {
    "gla_w_in": ((2, 1024, 3088), 2), "gla_w_gate_up": ((2, 16, 512), 2), "gla_b_gate": ((2, 512), 1),
    "gla_norm_g": ((2, 256), 1), "gla_w_out": ((2, 1024, 1024), 1), "mla_w_in": ((1, 1024, 576), 1),
    "mla_q_norm": ((1, 256), None), "mla_kv_norm": ((1, 256), None), "mla_w_uq": ((1, 256, 1536), 2),
    "mla_w_ukv": ((1, 256, 2048), 2), "mla_w_out": ((1, 1024, 1024), 1), "conv_w_in": ((1, 1024, 3072), 2),
    "conv_w": ((1, 3, 1024), 2), "conv_w_out": ((1, 1024, 1024), 1), "ln_g": ((4, 2, 1024), 2),
    "ln_b": ((4, 2, 1024), 2), "mlp_w1": ((4, 1024, 4096), 2), "mlp_w2": ((4, 4096, 1024), 1),
    "ple_w_gate": ((4, 1024, 1024), 1), "ple_w_proj": ((4, 256, 1024), 2),
}
WEIGHT_NAMES = list(WEIGHTS)
REG_W2, REG_W1T, REG_WOUT, REG_WG = (0, 512), (1, 512), (8, 128), (9, 128)
A_ROWS = 1280
REG_CONV = (0, 384)
REG_WPT = (0, 128)
REG_MLA_IN = (0, 128)
MLA_HEAD_PAD = 2 * LANES
SMALL = ["gla_w_gate_up", "gla_b_gate", "gla_norm_g", "conv_w", "ln_g", "ln_b"]
REPLICATED = ["mla_q_norm", "mla_kv_norm"]


def _params(**kw):
    return pltpu.CompilerParams(vmem_limit_bytes=V7X_VMEM_LIMIT_BYTES, **kw)


def _dot(a, b, ca, cb):
    return lax.dot_general(a, b, (((ca,), (cb,)), ((), ())), preferred_element_type=F32)


def _nn(a, b):
    return _dot(a.astype(BF16), b.astype(BF16), 1, 0)


def _nt(a, b):
    return _dot(a.astype(BF16), b.astype(BF16), 1, 1)


def _tn(a, b):
    return _dot(a.astype(BF16), b.astype(BF16), 0, 0)


@jax.custom_vjp
def mm_nn(a, b):
    return _nn(a, b)


def _mm_nn_fwd(a, b):
    return _nn(a, b), (a, b)


def _mm_nn_bwd(res, g):
    a, b = res
    return _nt(g, b).astype(a.dtype), _tn(a, g).astype(b.dtype)


mm_nn.defvjp(_mm_nn_fwd, _mm_nn_bwd)


@jax.custom_vjp
def mm_nt(a, b):
    return _nt(a, b)


def _mm_nt_fwd(a, b):
    return _nt(a, b), (a, b)


def _mm_nt_bwd(res, g):
    a, b = res
    return _nn(g, b).astype(a.dtype), _tn(g, a).astype(b.dtype)


mm_nt.defvjp(_mm_nt_fwd, _mm_nt_bwd)


@jax.custom_vjp
def mm_tn(a, b):
    return _tn(a, b)


def _mm_tn_fwd(a, b):
    return _tn(a, b), (a, b)


def _mm_tn_bwd(res, g):
    a, b = res
    return _nt(b, g).astype(a.dtype), _nn(a, g).astype(b.dtype)


mm_tn.defvjp(_mm_tn_fwd, _mm_tn_bwd)


def _iota2(shape, dim):
    return lax.broadcasted_iota(jnp.int32, shape, dim)


def _split3(x):
    hi = x.astype(BF16)
    rest = x - hi.astype(F32)
    mid = rest.astype(BF16)
    return hi, mid, (rest - mid.astype(F32)).astype(BF16)


def _tri_dot(tri, x):
    return sum(_dot(tri.astype(BF16), piece, 1, 0) for piece in _split3(x))


@jax.custom_vjp
def cumsum_rows(x):
    n = x.shape[0]
    return _tri_dot(_iota2((n, n), 0) >= _iota2((n, n), 1), x)


def _cumsum_fwd(x):
    return cumsum_rows(x), None


def _cumsum_bwd(_, g):
    n = g.shape[0]
    return (_tri_dot(_iota2((n, n), 0) <= _iota2((n, n), 1), g),)


cumsum_rows.defvjp(_cumsum_fwd, _cumsum_bwd)


def _rot_matrix(transposed):
    i, j = _iota2((LANES, LANES), 0), _iota2((LANES, LANES), 1)
    if transposed:
        i, j = j, i
    half = MLA_ROPE // 2
    plus = (i == j - half) & (j >= half) & (j < MLA_ROPE)
    minus = (i == j + half) & (j < half)
    return (plus.astype(F32) - minus.astype(F32)).astype(BF16)


def _rot_dot(x, transposed):
    return sum(_dot(piece, _rot_matrix(transposed), 1, 0) for piece in _split3(x))


@jax.custom_vjp
def rot_half(x):
    return _rot_dot(x, False)


def _rot_fwd(x):
    return rot_half(x), None


def _rot_bwd(_, g):
    return (_rot_dot(g, True),)


rot_half.defvjp(_rot_fwd, _rot_bwd)


def _shift_rows_raw(x, s):
    n = x.shape[0]
    row = _iota2(x.shape, 0)
    rolled = pltpu.roll(x, s % n, 0)
    keep = (row >= s) if s > 0 else (row < n + s)
    return jnp.where(keep, rolled, 0.0)


@functools.partial(jax.custom_vjp, nondiff_argnums=(1,))
def shift_rows(x, s):
    return _shift_rows_raw(x, s)


def _shift_fwd(x, s):
    return _shift_rows_raw(x, s), None


def _shift_bwd(s, _, g):
    return (_shift_rows_raw(g, -s),)


shift_rows.defvjp(_shift_fwd, _shift_bwd)


def _layer_norm(a, g, b):
    mu = jnp.mean(a, -1, keepdims=True)
    xc = a - mu
    var = jnp.mean(xc * xc, -1, keepdims=True)
    return xc * lax.rsqrt(var + LN_EPS) * g + b


def _rms_norm(a, g):
    return a * lax.rsqrt(jnp.mean(a * a, -1, keepdims=True) + RMS_EPS) * g


def _log_sigmoid(z):
    return jnp.minimum(z, 0.0) - jnp.log(1.0 + jnp.exp(-jnp.abs(z)))


def _matmul(a, b, *, name, ta=False, tb=False, tm=512, tn=512, a_fn=None, epi=None, epi_ins=(), out_dtypes=(BF16,),
            b_at=None, out_at=None, out_buf=None, after=None, n_row_sums=0, a_ins=(), a_out_dtypes=()):
    m = a.shape[1] if ta else a.shape[0]
    k = a.shape[0] if ta else a.shape[1]
    if b_at is None:
        n, kb = (b.shape[0], b.shape[1]) if tb else (b.shape[1], b.shape[0])
    else:
        rb, r = b_at
        n, kb = (N_DEV * r, b.shape[2]) if tb else (b.shape[2], N_DEV * r)
    assert kb == k, (name, a.shape, b.shape, k, kb)
    tm, tn = min(tm, m), min(tn, n)
    assert m % tm == 0 and n % tn == 0, (name, m, n, tm, tn)
    a_spec = pl.BlockSpec((k, tm), lambda i, j: (0, i)) if ta else pl.BlockSpec((tm, k), lambda i, j: (i, 0))
    if b_at is None:
        b_spec = pl.BlockSpec((tn, k), lambda i, j: (j, 0)) if tb else pl.BlockSpec((k, tn), lambda i, j: (0, j))
        load_b = lambda ref: ref[...]
    elif tb and tn == n:
        b_spec = pl.BlockSpec((N_DEV, r, k), lambda i, j: (0, rb, 0))
        load_b = lambda ref: ref[...].reshape(n, k)
    elif tb:
        assert tn == r, (name, tn, r)
        b_spec = pl.BlockSpec((1, r, k), lambda i, j: (j, rb, 0))
        load_b = lambda ref: ref[0]
    else:
        b_spec = pl.BlockSpec((N_DEV, r, tn), lambda i, j: (0, rb, j))
        load_b = lambda ref: ref[...].reshape(k, tn)
    e_specs = []
    for e in epi_ins:
        if e.shape == (1, n):
            e_specs.append(pl.BlockSpec((1, tn), lambda i, j: (0, j)))
        else:
            assert e.shape == (m, n), (name, e.shape, m, n)
            e_specs.append(pl.BlockSpec((tm, tn), lambda i, j: (i, j)))
    n_epi, n_ain, n_aout = len(epi_ins), len(a_ins), len(a_out_dtypes)
    assert n_aout == 0 or (tn == n and not ta and out_at is None), name
    ca, cb = (0 if ta else 1), (1 if tb else 0)
    operands = [a, b, *a_ins, *epi_ins]
    in_specs = [a_spec, b_spec, *[a_spec] * n_ain, *e_specs]
    if out_at is None:
        assert n_row_sums == 0 or tn == n, (name, tn, n)
        out_specs = [pl.BlockSpec((tm, tn), lambda i, j: (i, j)) for _ in out_dtypes]
        out_specs += [pl.BlockSpec((tm, k), lambda i, j: (i, 0))] * n_aout
        out_specs += [pl.BlockSpec((1, n), lambda i, j: (0, 0))] * n_row_sums
        out_shape = [jax.ShapeDtypeStruct((m, n), dt) for dt in out_dtypes]
        out_shape += [jax.ShapeDtypeStruct((m, k), dt) for dt in a_out_dtypes]
        out_shape += [jax.ShapeDtypeStruct((1, n), F32)] * n_row_sums
        aliases, n_buf = {}, 0
    else:
        orb, orows = out_at
        assert len(out_dtypes) == 1 and m == N_DEV * orows and n == out_buf.shape[2], (name, m, n)
        if tm > orows:
            assert tm % orows == 0, (name, tm, orows)
            out_specs = [pl.BlockSpec((tm // orows, orows, tn), lambda i, j: (i, orb, j))]
        else:
            per = orows // tm
            out_specs = [pl.BlockSpec((1, tm, tn), lambda i, j: (i // per, orb * per + i % per, j))]
        out_shape = [jax.ShapeDtypeStruct(out_buf.shape, out_buf.dtype)]
        operands.append(out_buf)
        in_specs.append(pl.BlockSpec(memory_space=pl.ANY))
        aliases, n_buf = {len(operands) - 1: 0}, 1
    for dep in ([] if after is None else after if isinstance(after, (list, tuple)) else [after]):
        if dep is not None:
            operands.append(dep)
            in_specs.append(pl.BlockSpec(memory_space=pl.ANY))
            n_buf += 1

    def body(a_ref, b_ref, *rest):
        av, a_outs = a_ref[...], ()
        if a_fn is not None:
            av = a_fn(av, *[r_[...] for r_ in rest[:n_ain]])
            if n_aout:
                av, *a_outs = av
        acc = _dot(av.astype(BF16), load_b(b_ref).astype(BF16), ca, cb)
        outs = epi(acc, *[r_[...] for r_ in rest[n_ain:n_ain + n_epi]]) if epi is not None else (acc,)
        o_refs = rest[n_ain + n_epi + n_buf:]
        n_tiles = len(o_refs) - n_row_sums - n_aout
        for o_ref, val in zip(o_refs[:n_tiles + n_aout], (*outs[:n_tiles], *a_outs)):
            o_ref[...] = val.astype(o_ref.dtype).reshape(o_ref.shape)
        if n_row_sums:
            @pl.when(pl.program_id(0) == 0)
            def _():
                for o_ref in o_refs[n_tiles + n_aout:]:
                    o_ref[...] = jnp.zeros_like(o_ref)

            for o_ref, val in zip(o_refs[n_tiles + n_aout:], outs[n_tiles:]):
                o_ref[...] += val

    outs = pl.pallas_call(
        body, name=name, grid=(m // tm, n // tn), in_specs=in_specs, out_specs=out_specs, out_shape=out_shape,
        input_output_aliases=aliases, compiler_params=_params(),
    )(*operands)
    return outs[0] if len(outs) == 1 else tuple(outs)


def _tile_fwd(f, tiled, params, out_dtypes, *, tm, name):
    t = tiled[0].shape[0]
    assert t % tm == 0
    out_avals = jax.eval_shape(f, *[jax.ShapeDtypeStruct((tm, x.shape[1]), F32) for x in tiled],
                               *[jax.ShapeDtypeStruct(p.shape, F32) for p in params])
    nt, npar = len(tiled), len(params)

    def body(*refs):
        ins = [r[...].astype(F32) for r in refs[:nt + npar]]
        outs = f(*ins)
        for o_ref, val in zip(refs[nt + npar:], outs):
            o_ref[...] = val.astype(o_ref.dtype)

    return pl.pallas_call(
        body, name=name, grid=(t // tm,),
        in_specs=[pl.BlockSpec((tm, x.shape[1]), lambda i: (i, 0)) for x in tiled]
        + [pl.BlockSpec(p.shape, lambda i: (0, 0)) for p in params],
        out_specs=[pl.BlockSpec((tm, o.shape[1]), lambda i: (i, 0)) for o in out_avals],
        out_shape=[jax.ShapeDtypeStruct((t, o.shape[1]), dt) for o, dt in zip(out_avals, out_dtypes)],
        compiler_params=_params(),
    )(*tiled, *params)


def _tile_bwd(f, tiled, params, cots, d_tiled_dtypes, *, tm, name, diff_tiled=None):
    t = tiled[0].shape[0]
    assert t % tm == 0
    nt, npar, nc = len(tiled), len(params), len(cots)
    diff_tiled = list(range(nt)) if diff_tiled is None else diff_tiled

    def body(*refs):
        ins = [r[...].astype(F32) for r in refs[:nt + npar]]
        cts = [r[...].astype(F32) for r in refs[nt + npar:nt + npar + nc]]
        o_refs = refs[nt + npar + nc:]
        _, vjp = jax.vjp(f, *ins)
        grads = vjp(tuple(cts))
        for o_ref, idx in zip(o_refs[:len(diff_tiled)], diff_tiled):
            o_ref[...] = grads[idx].astype(o_ref.dtype)
        p_refs = o_refs[len(diff_tiled):]

        @pl.when(pl.program_id(0) == 0)
        def _():
            for p_ref in p_refs:
                p_ref[...] = jnp.zeros_like(p_ref)

        for p_ref, gp in zip(p_refs, grads[nt:]):
            p_ref[...] += gp

    outs = pl.pallas_call(
        body, name=name, grid=(t // tm,),
        in_specs=[pl.BlockSpec((tm, x.shape[1]), lambda i: (i, 0)) for x in tiled]
        + [pl.BlockSpec(p.shape, lambda i: (0, 0)) for p in params]
        + [pl.BlockSpec((tm, c.shape[1]), lambda i: (i, 0)) for c in cots],
        out_specs=[pl.BlockSpec((tm, tiled[idx].shape[1]), lambda i: (i, 0)) for idx in diff_tiled]
        + [pl.BlockSpec(p.shape, lambda i: (0, 0)) for p in params],
        out_shape=[jax.ShapeDtypeStruct(tiled[idx].shape, dt) for idx, dt in zip(diff_tiled, d_tiled_dtypes)]
        + [jax.ShapeDtypeStruct(p.shape, F32) for p in params],
        compiler_params=_params(),
    )(*tiled, *params, *cots)
    return outs[:len(diff_tiled)], outs[len(diff_tiled):]


def _gla_head(q, k, v, r, z, g, st):
    c = q.shape[0]
    causal = _iota2((c, c), 0) >= _iota2((c, c), 1)
    la = _log_sigmoid(z) * (1.0 / GLA_TAU)
    big_l = cumsum_rows(la)
    ep, en = jnp.exp(big_l), jnp.exp(-big_l)
    qs = q * (GLA_DK ** -0.5)
    qp = qs * ep
    s = jnp.where(causal, mm_nt(qp, k * en), mm_nt(qs * en, k * ep))
    o = mm_nn(s, v) + mm_nt(qp, st)
    l_end = jnp.sum(la, axis=0, keepdims=True)
    st_new = st * jnp.exp(l_end) + mm_tn(v, k * jnp.exp(l_end - big_l))
    u = _rms_norm(o, g) * (r * jax.nn.sigmoid(r))
    return u, st_new


def _gla_slices(h):
    q = slice(GLA_DK * h, GLA_DK * (h + 1))
    k = slice(GLA_HK + GLA_DK * h, GLA_HK + GLA_DK * (h + 1))
    v = slice(2 * GLA_HK + GLA_DV * h, 2 * GLA_HK + GLA_DV * (h + 1))
    r = slice(2 * GLA_HK + GLA_HV + GLA_DV * h, 2 * GLA_HK + GLA_HV + GLA_DV * (h + 1))
    return q, k, v, r


GLA_CHUNKS_PER_STEP = 2


def _gla_fwd(proj, z, norm_g, after):
    t = proj.shape[0]
    nc, per = t // CHUNK, GLA_CHUNKS_PER_STEP
    rows_per_step = per * CHUNK
    after = [a for a in after if a is not None]

    def body(proj_ref, z_ref, g_ref, *rest):
        u_ref, st_save_ref, st_ref = rest[len(after):]

        @pl.when(pl.program_id(0) == 0)
        def _():
            st_ref[...] = jnp.zeros_like(st_ref)

        g = g_ref[...]
        for h in range(GLA_HEADS):
            sq, sk, sv, sr = _gla_slices(h)
            st = st_ref[h]
            for c in range(per):
                rows = slice(c * CHUNK, (c + 1) * CHUNK)
                st_save_ref[c, h] = st
                u, st = _gla_head(proj_ref[rows, sq].astype(F32), proj_ref[rows, sk].astype(F32),
                                  proj_ref[rows, sv].astype(F32), proj_ref[rows, sr].astype(F32),
                                  z_ref[rows, GLA_DK * h:GLA_DK * (h + 1)], g, st)
                u_ref[rows, GLA_DV * h:GLA_DV * (h + 1)] = u.astype(u_ref.dtype)
            st_ref[h] = st

    return pl.pallas_call(
        body, name="gla_fwd", grid=(nc // per,),
        in_specs=[pl.BlockSpec((rows_per_step, GLA_MAIN), lambda i: (i, 0)),
                  pl.BlockSpec((rows_per_step, GLA_HK), lambda i: (i, 0)), pl.BlockSpec((1, GLA_DV), lambda i: (0, 0))]
        + [pl.BlockSpec(memory_space=pl.ANY)] * len(after),
        out_specs=[pl.BlockSpec((rows_per_step, GLA_HV), lambda i: (i, 0)),
                   pl.BlockSpec((per, GLA_HEADS, GLA_DV, GLA_DK), lambda i: (i, 0, 0, 0))],
        out_shape=[jax.ShapeDtypeStruct((t, GLA_HV), BF16), jax.ShapeDtypeStruct((nc, GLA_HEADS, GLA_DV, GLA_DK), F32)],
        scratch_shapes=[pltpu.VMEM((GLA_HEADS, GLA_DV, GLA_DK), F32)],
        compiler_params=_params(),
    )(proj, z, norm_g, *after)


def _gla_bwd(proj, z, norm_g, states, du, after):
    t = proj.shape[0]
    nc, per = t // CHUNK, GLA_CHUNKS_PER_STEP
    rows_per_step = per * CHUNK
    n_steps = nc // per
    after = [a for a in after if a is not None]

    def body(proj_ref, z_ref, g_ref, st_in_ref, du_ref, *rest):
        dproj_ref, dz_ref, dg_ref, dzsum_ref, dst_ref = rest[len(after):]

        @pl.when(pl.program_id(0) == 0)
        def _():
            dst_ref[...] = jnp.zeros_like(dst_ref)
            dg_ref[...] = jnp.zeros_like(dg_ref)
            dzsum_ref[...] = jnp.zeros_like(dzsum_ref)

        g = g_ref[...]
        for h in range(GLA_HEADS):
            sq, sk, sv, sr = _gla_slices(h)
            dst = dst_ref[h]
            for c in reversed(range(per)):
                rows = slice(c * CHUNK, (c + 1) * CHUNK)
                ins = (proj_ref[rows, sq].astype(F32), proj_ref[rows, sk].astype(F32), proj_ref[rows, sv].astype(F32),
                       proj_ref[rows, sr].astype(F32), z_ref[rows, GLA_DK * h:GLA_DK * (h + 1)], g, st_in_ref[c, h])
                _, vjp = jax.vjp(_gla_head, *ins)
                dq, dk, dv, dr, dz, dg, dst = vjp((du_ref[rows, GLA_DV * h:GLA_DV * (h + 1)], dst))
                dproj_ref[rows, sq] = dq.astype(dproj_ref.dtype)
                dproj_ref[rows, sk] = dk.astype(dproj_ref.dtype)
                dproj_ref[rows, sv] = dv.astype(dproj_ref.dtype)
                dproj_ref[rows, sr] = dr.astype(dproj_ref.dtype)
                dz_ref[rows, GLA_DK * h:GLA_DK * (h + 1)] = dz
                dzsum_ref[:, GLA_DK * h:GLA_DK * (h + 1)] += jnp.sum(dz, axis=0, keepdims=True)
                dg_ref[...] += dg
            dst_ref[h] = dst

    rev = lambda i: (n_steps - 1 - i, 0)
    return pl.pallas_call(
        body, name="gla_bwd", grid=(n_steps,),
        in_specs=[pl.BlockSpec((rows_per_step, GLA_MAIN), rev), pl.BlockSpec((rows_per_step, GLA_HK), rev),
                  pl.BlockSpec((1, GLA_DV), lambda i: (0, 0)),
                  pl.BlockSpec((per, GLA_HEADS, GLA_DV, GLA_DK), lambda i: (n_steps - 1 - i, 0, 0, 0)),
                  pl.BlockSpec((rows_per_step, GLA_HV), rev)] + [pl.BlockSpec(memory_space=pl.ANY)] * len(after),
        out_specs=[pl.BlockSpec((rows_per_step, GLA_MAIN), rev), pl.BlockSpec((rows_per_step, GLA_HK), rev),
                   pl.BlockSpec((1, GLA_DV), lambda i: (0, 0)), pl.BlockSpec((1, GLA_HK), lambda i: (0, 0))],
        out_shape=[jax.ShapeDtypeStruct((t, GLA_MAIN), BF16), jax.ShapeDtypeStruct((t, GLA_HK), F32),
                   jax.ShapeDtypeStruct((1, GLA_DV), F32), jax.ShapeDtypeStruct((1, GLA_HK), F32)],
        scratch_shapes=[pltpu.VMEM((GLA_HEADS, GLA_DV, GLA_DK), F32)],
        compiler_params=_params(),
    )(proj, z, norm_g, states, du, *after)


def _mla_pre(cq, cos, sin, gq, gkv, w_uq, w_ukv):
    qlat = _rms_norm(cq[:, :MLA_RANK], gq)
    kvlat = _rms_norm(cq[:, MLA_RANK:2 * MLA_RANK], gkv)
    kr = cq[:, 2 * MLA_RANK:]
    q = mm_nn(qlat, w_uq) * ((MLA_NOPE + MLA_ROPE) ** -0.5)
    kv = mm_nn(kvlat, w_ukv)
    pieces = []
    for h in range(MLA_HEADS):
        qr = q[:, MLA_HEAD_PAD * h + MLA_NOPE:MLA_HEAD_PAD * (h + 1)]
        pieces += [q[:, MLA_HEAD_PAD * h:MLA_HEAD_PAD * h + MLA_NOPE], qr * cos + rot_half(qr) * sin]
    return jnp.concatenate(pieces, axis=1), kv, kr * cos + rot_half(kr) * sin


MLA_Q_TILE = 256


def _mla_attn_block(qn, qr, kv, kr, q0):
    tq, nk = qn.shape[0], kv.shape[0]
    s = mm_nt(qn, kv[:, :MLA_NOPE]) + mm_nt(qr, kr)
    visible = (_iota2((tq, nk), 1) // CHUNK) <= ((q0 + _iota2((tq, nk), 0)) // CHUNK)
    s = jnp.where(visible, s, -1e30)
    e = jnp.exp(s - jnp.max(s, -1, keepdims=True))
    p = e / jnp.sum(e, -1, keepdims=True)
    return mm_nn(p, kv[:, MLA_NOPE:])


def _mla_attn_fwd(q, kv, kr, after):
    t = q.shape[0]
    after = [a for a in after if a is not None]

    def body(q_ref, kv_ref, kr_ref, *rest):
        (o_ref,) = rest[len(after):]
        for i in range(t // MLA_Q_TILE):
            rows = slice(i * MLA_Q_TILE, (i + 1) * MLA_Q_TILE)
            keys = slice(0, (i + 1) * MLA_Q_TILE)
            o = _mla_attn_block(q_ref[rows, :MLA_NOPE].astype(F32), q_ref[rows, MLA_NOPE:].astype(F32),
                                kv_ref[keys, :].astype(F32), kr_ref[keys, :].astype(F32), i * MLA_Q_TILE)
            o_ref[rows, :] = o.astype(o_ref.dtype)

    return pl.pallas_call(
        body, name="mla_attn_fwd", grid=(MLA_HEADS,),
        in_specs=[pl.BlockSpec((t, MLA_HEAD_PAD), lambda h: (0, h)),
                  pl.BlockSpec((t, MLA_NOPE + MLA_V), lambda h: (0, h)), pl.BlockSpec((t, LANES), lambda h: (0, 0))]
        + [pl.BlockSpec(memory_space=pl.ANY)] * len(after),
        out_specs=pl.BlockSpec((t, MLA_V), lambda h: (0, h)),
        out_shape=jax.ShapeDtypeStruct((t, MLA_HEADS * MLA_V), BF16),
        compiler_params=_params(),
    )(q, kv, kr, *after)


def _mla_attn_bwd(q, kv, kr, do, after):
    t = q.shape[0]
    after = [a for a in after if a is not None]

    def body(q_ref, kv_ref, kr_ref, do_ref, *rest):
        dq_ref, dkv_ref, dkr_ref = rest[len(after):]
        dkv_ref[...] = jnp.zeros_like(dkv_ref)

        @pl.when(pl.program_id(0) == 0)
        def _():
            dkr_ref[...] = jnp.zeros_like(dkr_ref)

        for i in range(t // MLA_Q_TILE):
            rows = slice(i * MLA_Q_TILE, (i + 1) * MLA_Q_TILE)
            keys = slice(0, (i + 1) * MLA_Q_TILE)
            f = functools.partial(_mla_attn_block, q0=i * MLA_Q_TILE)
            _, vjp = jax.vjp(f, q_ref[rows, :MLA_NOPE].astype(F32), q_ref[rows, MLA_NOPE:].astype(F32),
                             kv_ref[keys, :].astype(F32), kr_ref[keys, :].astype(F32))
            dqn, dqr, dkv, dkr = vjp(do_ref[rows, :].astype(F32))
            dq_ref[rows, :MLA_NOPE] = dqn
            dq_ref[rows, MLA_NOPE:] = dqr
            dkv_ref[keys, :] += dkv
            dkr_ref[keys, :] += dkr

    return pl.pallas_call(
        body, name="mla_attn_bwd", grid=(MLA_HEADS,),
        in_specs=[pl.BlockSpec((t, MLA_HEAD_PAD), lambda h: (0, h)),
                  pl.BlockSpec((t, MLA_NOPE + MLA_V), lambda h: (0, h)), pl.BlockSpec((t, LANES), lambda h: (0, 0)),
                  pl.BlockSpec((t, MLA_V), lambda h: (0, h))] + [pl.BlockSpec(memory_space=pl.ANY)] * len(after),
        out_specs=[pl.BlockSpec((t, MLA_HEAD_PAD), lambda h: (0, h)),
                   pl.BlockSpec((t, MLA_NOPE + MLA_V), lambda h: (0, h)), pl.BlockSpec((t, LANES), lambda h: (0, 0))],
        out_shape=[jax.ShapeDtypeStruct(q.shape, F32), jax.ShapeDtypeStruct(kv.shape, F32),
                   jax.ShapeDtypeStruct(kr.shape, F32)],
        compiler_params=_params(),
    )(q, kv, kr, do, *after)


def _rope_tables(pos_col, inv_freq_row):
    t = pos_col.shape[0]

    def body(pos_ref, f_ref, cos_ref, sin_ref):
        ang = pos_ref[...].astype(F32) * f_ref[...]
        live = _iota2(ang.shape, 1) < MLA_ROPE
        cos_ref[...] = jnp.where(live, jnp.cos(ang), 0.0)
        sin_ref[...] = jnp.where(live, jnp.sin(ang), 0.0)

    return pl.pallas_call(
        body, name="rope_tables", out_shape=[jax.ShapeDtypeStruct((t, LANES), F32)] * 2, compiler_params=_params(),
    )(pos_col, inv_freq_row)


CONV_COL_TILE = 256


def _conv_gate(b, c, u, w0, w1, w2):
    cu = c * u
    return b * (w2 * cu + w1 * shift_rows(cu, 1) + w0 * shift_rows(cu, 2))


def _conv_specs(t):
    nb = D_MODEL // CONV_COL_TILE
    return [pl.BlockSpec((t, CONV_COL_TILE), lambda j, part=part: (0, part * nb + j)) for part in range(3)]


def _conv_fwd(bcu, w, after):
    t = bcu.shape[0]
    after = [a for a in after if a is not None]

    def body(b_ref, c_ref, u_ref, w_ref, *rest):
        (o_ref,) = rest[len(after):]
        o_ref[...] = _conv_gate(b_ref[...], c_ref[...], u_ref[...], w_ref[0:1, :], w_ref[1:2, :],
                                w_ref[2:3, :]).astype(o_ref.dtype)

    return pl.pallas_call(
        body, name="conv_fwd", grid=(D_MODEL // CONV_COL_TILE,),
        in_specs=_conv_specs(t) + [pl.BlockSpec((3, CONV_COL_TILE), lambda j: (0, j))]
        + [pl.BlockSpec(memory_space=pl.ANY)] * len(after),
        out_specs=pl.BlockSpec((t, CONV_COL_TILE), lambda j: (0, j)),
        out_shape=jax.ShapeDtypeStruct((t, D_MODEL), BF16), compiler_params=_params(),
    )(bcu, bcu, bcu, w, *after)


def _conv_bwd(bcu, w, dout, after):
    t = bcu.shape[0]
    after = [a for a in after if a is not None]

    def body(b_ref, c_ref, u_ref, w_ref, do_ref, *rest):
        db_ref, dc_ref, du_ref, dw_ref = rest[len(after):]
        _, vjp = jax.vjp(_conv_gate, b_ref[...], c_ref[...], u_ref[...], w_ref[0:1, :], w_ref[1:2, :], w_ref[2:3, :])
        db, dc, du, dw0, dw1, dw2 = vjp(do_ref[...])
        db_ref[...] = db.astype(db_ref.dtype)
        dc_ref[...] = dc.astype(dc_ref.dtype)
        du_ref[...] = du.astype(du_ref.dtype)
        dw_ref[0:1, :] = dw0
        dw_ref[1:2, :] = dw1
        dw_ref[2:3, :] = dw2

    col = pl.BlockSpec((t, CONV_COL_TILE), lambda j: (0, j))
    return pl.pallas_call(
        body, name="conv_bwd", grid=(D_MODEL // CONV_COL_TILE,),
        in_specs=_conv_specs(t) + [pl.BlockSpec((3, CONV_COL_TILE), lambda j: (0, j)), col]
        + [pl.BlockSpec(memory_space=pl.ANY)] * len(after),
        out_specs=[col, col, col, pl.BlockSpec((3, CONV_COL_TILE), lambda j: (0, j))],
        out_shape=[jax.ShapeDtypeStruct((t, D_MODEL), BF16)] * 3 + [jax.ShapeDtypeStruct((3, D_MODEL), F32)],
        compiler_params=_params(),
    )(bcu, bcu, bcu, w, dout, *after)


def _loss_head(y, target):
    t, d = y.shape
    tm = 256

    def body(y_ref, t_ref, loss_ref, dy_ref):
        @pl.when(pl.program_id(0) == 0)
        def _():
            loss_ref[...] = jnp.zeros_like(loss_ref)

        err = y_ref[...] - t_ref[...]
        dy_ref[...] = err * (1.0 / d)
        loss_ref[...] += 0.5 * jnp.sum(jnp.sum(err * err, axis=-1, keepdims=True) * (1.0 / d))

    tile = pl.BlockSpec((tm, d), lambda i: (i, 0))
    return pl.pallas_call(
        body, name="loss_head", grid=(t // tm,), in_specs=[tile, tile],
        out_specs=[pl.BlockSpec((8, LANES), lambda i: (0, 0)), tile],
        out_shape=[jax.ShapeDtypeStruct((8, LANES), F32), jax.ShapeDtypeStruct((t, d), F32)],
        compiler_params=_params(),
    )(y, target)


def _ln_epi(acc, res, g, b):
    a = ALPHA * res + acc
    y = _layer_norm(a, g, b)
    return a, y, y


def _ln_fn(a, g, b):
    return (_layer_norm(a, g, b),)


def _ln_bwd_epi(scale):
    def epi(acc, res, a, g, b):
        _, vjp = jax.vjp(_ln_fn, a, g, b)
        da, dg, db = vjp((acc + scale * res,))
        return da, da, dg, db
    return epi


def _relu_sq(h):
    r = jnp.maximum(h, 0)
    return r * r


def _pad_cols(w, n):
    return jnp.pad(w, ((0, 0), (0, n - w.shape[1])))


def _pad_rows(w, n):
    return jnp.pad(w, ((0, n - w.shape[0]), (0, 0)))


def _step(x, p, positions, target, small, comm):
    t = x.shape[0]
    w = small
    freqs = ROPE_BASE ** (-jnp.arange(0, MLA_ROPE // 2, dtype=F32) * (2.0 / MLA_ROPE))
    freq_row = jnp.concatenate([freqs, freqs, jnp.zeros((LANES - MLA_ROPE,), F32)])[None, :]
    cos, sin = _rope_tables(positions.reshape(t, 1), freq_row)

    saved = []
    xb = x.astype(BF16)
    for i in range(DEPTH):
        j, kind = i // 3, i % 3
        wl = comm.mixer_weights(i)
        s = {"x": xb, "wl": wl}
        tok = comm.at("fwd", i, "begin", x)
        if kind == 0:
            s["w_main"] = wl["gla_w_in_t"][:GLA_MAIN]
            s["w_lr"] = _pad_rows(wl["gla_w_in_t"][GLA_MAIN:], LANES)
            s["w_up"] = _pad_rows(w["gla_w_gate_up"][j], LANES).astype(BF16)
            s["proj"] = _matmul(xb, s["w_main"], name="gla_proj", tb=True, tn=1024, after=tok)
            s["glr"] = _matmul(xb, s["w_lr"], name="gla_lr", tb=True, out_dtypes=(F32,))
            s["z"] = _matmul(s["glr"], s["w_up"], name="gla_gate", epi=lambda acc, b: (acc + b,),
                             epi_ins=(w["gla_b_gate"][j][None, :],), out_dtypes=(F32,))
            tok = comm.at("fwd", i, "proj_done", s["z"]) or []
            s["u"], s["states"] = _gla_fwd(s["proj"], s["z"], w["gla_norm_g"][j][None, :], tok)
        elif kind == 1:
            s["cq"] = _matmul(xb, wl["mla_in"], name="mla_proj", tn=MLA_IN_PAD, b_at=REG_MLA_IN, out_dtypes=(F32,),
                              after=tok)
            s["pre_params"] = (w["mla_q_norm"][j][None, :], w["mla_kv_norm"][j][None, :], wl["mla_w_uq"], wl["mla_w_ukv"])
            s["q"], s["kv"], s["kr"] = _tile_fwd(_mla_pre, (s["cq"], cos, sin), s["pre_params"], (BF16, BF16, BF16),
                                                 tm=256, name="mla_pre_fwd")
            tok = comm.at("fwd", i, "proj_done", s["kv"]) or []
            s["u"] = _mla_attn_fwd(s["q"], s["kv"], s["kr"], tok)
        else:
            s["bcu"] = _matmul(xb, wl["conv"], name="conv_proj", tb=True, tm=256, tn=3 * D_MODEL, b_at=REG_CONV,
                               out_dtypes=(F32,), after=tok)
            tok = comm.at("fwd", i, "proj_done", s["bcu"]) or []
            s["u"] = _conv_fwd(s["bcu"], w["conv_w"][j], tok)
        g0, b0 = w["ln_g"][i, 0][None, :], w["ln_b"][i, 0][None, :]
        g1, b1 = w["ln_g"][i, 1][None, :], w["ln_b"][i, 1][None, :]
        wa, wb = s["wa"], _ = comm.slab_weights(i, s["u"])
        s["a1"], s["x1"], s["x1b"] = _matmul(s["u"], wa, name="mixer_out_ln", tm=256, tn=D_MODEL, b_at=REG_WOUT,
                                             epi=_ln_epi, epi_ins=(x, g0, b0), out_dtypes=(F32, F32, BF16))
        s["hh"] = _matmul(s["x1b"], wa, name="mlp_up", tb=True, tm=256, tn=D_FF, b_at=REG_W1T)
        tok = comm.at("fwd", i, "mid", s["hh"])
        s["a2"], s["x2"], s["x2b"] = _matmul(s["hh"], wa, name="mlp_down_ln", tm=256, tn=D_MODEL, b_at=REG_W2,
                                             a_fn=_relu_sq, epi=_ln_epi, epi_ins=(s["x1"], g1, b1),
                                             out_dtypes=(F32, F32, BF16), after=tok)
        s["pp"] = _matmul(p[i], wb, name="ple_proj", tb=True, tn=D_MODEL, b_at=REG_WPT)
        tok = comm.at("fwd", i, "end", s["pp"])
        def ple_epi(acc, xr, pp):
            y = xr + jax.nn.sigmoid(acc) * pp.astype(F32)
            return y, y, acc

        x, xb, s["gt"] = _matmul(s["x2b"], wa, name="ple_gate", tn=1024, b_at=REG_WG, epi=ple_epi,
                                 epi_ins=(s["x2"], s["pp"]), out_dtypes=(F32, BF16, BF16), after=tok)
        saved.append(s)

    loss_part, dx = _loss_head(x, target)

    gw = {n: [None] * WEIGHTS[n][0][0] for n in SMALL + REPLICATED}
    ln_g_grads, ln_b_grads = [[None, None] for _ in range(DEPTH)], [[None, None] for _ in range(DEPTH)]
    resid = lambda acc, r: (acc + ALPHA * r,)
    plus = lambda acc, r: (acc + r,)
    for i in reversed(range(DEPTH)):
        j, kind = i // 3, i % 3
        s = saved[i]
        wa = s["wa"]
        ga = lax.empty((N_DEV, A_ROWS, D_MODEL), BF16)
        gb = lax.empty((N_DEV, REG_WPT[1], PLE_DIM), BF16)
        layer_grads = {}
        tok = comm.at("bwd", i, "begin", dx)

        def ple_bwd(dxo, gt, pp):
            sg = jax.nn.sigmoid(gt.astype(F32))
            d_gt = dxo * pp.astype(F32) * sg * (1.0 - sg)
            return d_gt, d_gt, dxo * sg

        g1, b1 = w["ln_g"][i, 1][None, :], w["ln_b"][i, 1][None, :]
        d_a2, d_a2b, d_gt, d_pp, ln_g_grads[i][1], ln_b_grads[i][1] = _matmul(
            dx, wa, name="ple_gate_dx_ln", tb=True, tm=256, tn=D_MODEL, b_at=REG_WG, a_fn=ple_bwd,
            a_ins=(s["gt"], s["pp"]), a_out_dtypes=(BF16, BF16), epi=_ln_bwd_epi(1.0), epi_ins=(dx, s["a2"], g1, b1),
            out_dtypes=(F32, BF16), n_row_sums=2, after=tok)
        gb = _matmul(d_pp, p[i], name="ple_proj_dw", ta=True, tm=512, tn=PLE_DIM, out_at=REG_WPT, out_buf=gb)
        ga = _matmul(s["x2b"], d_gt, name="ple_gate_dw", ta=True, tm=512, tn=1024, out_at=REG_WG, out_buf=ga)
        tok = comm.at("bwd", i, "ln", d_a2)
        ga = _matmul(s["hh"], d_a2b, name="mlp_down_dw", ta=True, tm=REG_W2[1], tn=1024, a_fn=_relu_sq, out_at=REG_W2,
                     out_buf=ga, after=tok)
        d_hh = _matmul(d_a2b, wa, name="mlp_down_dx", tb=True, tm=256, tn=D_FF, b_at=REG_W2, after=[ga, gb],
                       epi=lambda acc, hh: (acc * 2.0 * jnp.maximum(hh.astype(F32), 0.0),), epi_ins=(s["hh"],))
        ga = _matmul(d_hh, s["x1b"], name="mlp_up_dw", ta=True, tm=REG_W1T[1], tn=1024, out_at=REG_W1T, out_buf=ga)
        g0, b0 = w["ln_g"][i, 0][None, :], w["ln_b"][i, 0][None, :]
        d_a1, d_a1b, ln_g_grads[i][0], ln_b_grads[i][0] = _matmul(
            d_hh, wa, name="mlp_up_dx_ln", tm=256, tn=D_MODEL, b_at=REG_W1T, epi=_ln_bwd_epi(ALPHA),
            epi_ins=(d_a2, s["a1"], g0, b0), out_dtypes=(F32, BF16), n_row_sums=2, after=ga)
        ga = _matmul(s["u"], d_a1b, name="mixer_out_dw", ta=True, tm=512, tn=1024, out_at=REG_WOUT, out_buf=ga)
        du = _matmul(d_a1b, wa, name="mixer_out_dx", tb=True, tn=1024, b_at=REG_WOUT, out_dtypes=(F32,), after=ga)
        comm.slab_grads(i, ga, gb)
        tok = comm.at("bwd", i, "slab_done", du) or []
        if kind == 0:
            dproj, dz, dg, dz_sum = _gla_bwd(s["proj"], s["z"], w["gla_norm_g"][j][None, :], s["states"], du, tok)
            tok = comm.at("bwd", i, "mixer_done", dproj)
            gw["gla_norm_g"][j] = dg[0]
            gw["gla_b_gate"][j] = dz_sum[0]
            gw["gla_w_gate_up"][j] = _matmul(s["glr"], dz, name="gla_gate_dw", ta=True, out_dtypes=(F32,),
                                             after=tok)[:GLA_RANK]
            dglr = _matmul(dz, s["w_up"], name="gla_gate_dx", tb=True, out_dtypes=(F32,))
            dw_main = _matmul(dproj, s["x"], name="gla_proj_dw", ta=True, tn=1024, out_dtypes=(F32,))
            dw_lr = _matmul(dglr, s["x"], name="gla_lr_dw", ta=True, tn=1024, out_dtypes=(F32,))[:GLA_RANK]
            layer_grads["gla_w_in_t"] = jnp.concatenate([dw_main, dw_lr], axis=0)
            dx = _matmul(dproj, s["w_main"], name="gla_proj_dx", tn=1024, epi=resid, epi_ins=(d_a1,),
                         out_dtypes=(F32,), after=[dw_main, dw_lr, gw["gla_w_gate_up"][j]])
            dx = _matmul(dglr, s["w_lr"], name="gla_lr_dx", tn=1024, epi=plus, epi_ins=(dx,), out_dtypes=(F32,))
        elif kind == 1:
            dq, dkv, dkr = _mla_attn_bwd(s["q"], s["kv"], s["kr"], du, tok)
            tok = comm.at("bwd", i, "mixer_done", dq)
            (d_cq,), (dgq, dgkv, layer_grads["mla_uq"], layer_grads["mla_ukv"]) = _tile_bwd(
                _mla_pre, (s["cq"], cos, sin), s["pre_params"], (dq, dkv, dkr), (BF16,), tm=256, name="mla_pre_bwd",
                diff_tiled=[0])
            gw["mla_q_norm"][j], gw["mla_kv_norm"][j] = dgq[0], dgkv[0]
            layer_grads["mla_in"] = _matmul(s["x"], d_cq, name="mla_proj_dw", ta=True, tm=512, tn=MLA_IN_PAD,
                                            out_at=REG_MLA_IN, after=tok,
                                            out_buf=lax.empty((N_DEV, REG_MLA_IN[1], MLA_IN_PAD), BF16))
            dx = _matmul(d_cq, s["wl"]["mla_in"], name="mla_proj_dx", tb=True, tn=1024, b_at=REG_MLA_IN, epi=resid,
                         epi_ins=(d_a1,), out_dtypes=(F32,), after=layer_grads["mla_in"])
        else:
            db, dc, du_, dcw = _conv_bwd(s["bcu"], w["conv_w"][j], du, tok)
            tok = comm.at("bwd", i, "mixer_done", db)
            gw["conv_w"][j] = dcw
            dbcu = jnp.concatenate([db, dc, du_], axis=1)
            layer_grads["conv"] = _matmul(dbcu, s["x"], name="conv_proj_dw", ta=True, tm=REG_CONV[1], tn=1024,
                                          out_at=REG_CONV, out_buf=lax.empty((N_DEV, REG_CONV[1], D_MODEL), BF16),
                                          after=tok)
            dx = _matmul(dbcu, s["wl"]["conv"], name="conv_proj_dx", tn=1024, b_at=REG_CONV, epi=resid, epi_ins=(d_a1,),
                         out_dtypes=(F32,), after=layer_grads["conv"])
        comm.mixer_grads(i, layer_grads)

    gw["ln_g"] = [jnp.concatenate([a, b], axis=0) for a, b in ln_g_grads]
    gw["ln_b"] = [jnp.concatenate([a, b], axis=0) for a, b in ln_b_grads]
    return loss_part, dx, {n: jnp.stack(gw[n]).astype(F32) for n in gw}


MESH_IDS = pl.DeviceIdType.MESH
ANY = pl.BlockSpec(memory_space=pl.ANY)
HBM_SPEC = pl.BlockSpec(memory_space=pltpu.HBM)
SEM_SPEC = pl.BlockSpec(memory_space=pltpu.SEMAPHORE)
DATAFLOW_EFFECT = pltpu.SideEffectType.DATAFLOW_SIDE_EFFECTING
CORE_COPIES, CHIP_COPIES = 4, 3


def _my_place():
    return lax.axis_index("x"), lax.axis_index("y"), lax.axis_index("c")


def _other_chips(mx, my):
    return [(1 - mx, my), (mx, 1 - my), (1 - mx, 1 - my)]


def _remote(src, dst, send_sems, recv_sems, k, to):
    return pltpu.make_async_remote_copy(src_ref=src, dst_ref=dst, send_sem=send_sems.at[k], recv_sem=recv_sems.at[k],
                                        device_id=to, device_id_type=MESH_IDS)


def _gather_first_copies(n_arr):
    def make(bufs, send_sems, recv_sems):
        mx, my, mc = _my_place()
        mine = 4 * mx + 2 * my + mc
        peers = [(mx, my, 1 - mc)] + [(cx, cy, mc) for cx, cy in _other_chips(mx, my)]
        return [_remote(bufs[a].at[mine], bufs[a].at[mine], send_sems, recv_sems, (1 + CHIP_COPIES) * a + k, to)
                for a in range(n_arr) for k, to in enumerate(peers)]
    return make, (1 + CHIP_COPIES) * n_arr


def _gather_forward_copies(n_arr):
    def make(bufs, send_sems, recv_sems):
        mx, my, mc = _my_place()
        blocks = [4 * cx + 2 * cy + mc for cx, cy in _other_chips(mx, my)]
        return [_remote(bufs[a].at[blk], bufs[a].at[blk], send_sems, recv_sems, CHIP_COPIES * a + k, (mx, my, 1 - mc))
                for a in range(n_arr) for k, blk in enumerate(blocks)]
    return make, CHIP_COPIES * n_arr


def _scatter_core_copies(n_arr):
    def make(bufs, send_sems, recv_sems):
        mx, my, mc = _my_place()
        return [_remote(bufs[a].at[2 * k + (1 - mc)], bufs[n_arr + a].at[k], send_sems, recv_sems, CORE_COPIES * a + k,
                        (mx, my, 1 - mc)) for a in range(n_arr) for k in range(CORE_COPIES)]
    return make, CORE_COPIES * n_arr


def _scatter_chip_copies(n_arr):
    def make(bufs, send_sems, recv_sems):
        mx, my, mc = _my_place()
        return [_remote(bufs[a].at[2 * cx + cy], bufs[n_arr + a].at[k], send_sems, recv_sems, CHIP_COPIES * a + k,
                        (cx, cy, mc)) for a in range(n_arr) for k, (cx, cy) in enumerate(_other_chips(mx, my))]
    return make, CHIP_COPIES * n_arr


def _exchange(name, bufs, copies):
    make, n_copies = copies
    n = len(bufs)

    def body(*refs):
        descs = make(refs[:n], refs[2 * n], refs[2 * n + 1])
        for cp in descs:
            cp.start()
        for cp in descs:
            cp.wait()

    return pl.pallas_call(
        body, name=name, out_shape=[jax.ShapeDtypeStruct(b.shape, b.dtype) for b in bufs], in_specs=[ANY] * n,
        out_specs=[ANY] * n, input_output_aliases={i: i for i in range(n)},
        scratch_shapes=[pltpu.SemaphoreType.DMA((n_copies,)), pltpu.SemaphoreType.DMA((n_copies,))],
    )(*bufs)


def _exchange_start(name, bufs, copies, after):
    make, n_copies = copies
    n = len(bufs)

    def body(*refs):
        for cp in make(refs[:n], refs[n + 1], refs[n + 2]):
            cp.start()
        refs[-1][...] = jnp.zeros_like(refs[-1])

    outs = pl.pallas_call(
        body, name=name,
        out_shape=(pltpu.SemaphoreType.DMA((n_copies,)), pltpu.SemaphoreType.DMA((n_copies,)),
                   *[pltpu.HBM(b.shape, b.dtype) for b in bufs], jax.ShapeDtypeStruct((8, LANES), F32)),
        in_specs=[HBM_SPEC] * n + [ANY],
        out_specs=(SEM_SPEC, SEM_SPEC, *[HBM_SPEC] * n, pl.BlockSpec(memory_space=pltpu.VMEM)),
        input_output_aliases={i: 2 + i for i in range(n)},
        compiler_params=pltpu.CompilerParams(has_side_effects=DATAFLOW_EFFECT),
    )(*[pltpu.with_memory_space_constraint(b, pltpu.HBM) for b in bufs], after)
    return (outs[0], outs[1]), list(outs[2:2 + n]), outs[-1]


def _exchange_wait(name, sems, bufs, copies, after):
    make, _ = copies
    n = len(bufs)

    def body(*refs):
        for cp in make(refs[:n], refs[n], refs[n + 1]):
            cp.wait_send()
            cp.wait_recv()

    return list(pl.pallas_call(
        body, name=name, out_shape=[pltpu.HBM(b.shape, b.dtype) for b in bufs],
        in_specs=[HBM_SPEC] * n + [SEM_SPEC, SEM_SPEC, ANY], out_specs=[HBM_SPEC] * n,
        input_output_aliases={i: i for i in range(n)},
        compiler_params=pltpu.CompilerParams(has_side_effects=DATAFLOW_EFFECT),
    )(*bufs, *sems, after))


SUM_TILE_BYTES = 2 * 1024 * 1024


def _row_tile(r, c):
    best = None
    for cand in range(16, r + 1, 16):
        if r % cand == 0 and cand * c * 2 <= SUM_TILE_BYTES:
            best = cand
    return r if best is None else best


def _pair_sum(g, recv, my_c):
    _, r, c = g.shape
    tr = _row_tile(r, c)

    def body(c_ref, g_ref, r_ref, o_ref):
        o_ref[...] = (g_ref[...].astype(F32) + r_ref[...].astype(F32)).astype(o_ref.dtype)

    return pl.pallas_call(
        body, name="rs_pair_sum", out_shape=jax.ShapeDtypeStruct((4, r, c), g.dtype),
        grid_spec=pltpu.PrefetchScalarGridSpec(
            num_scalar_prefetch=1, grid=(4, r // tr),
            in_specs=[pl.BlockSpec((1, tr, c), lambda n, i, cr: (2 * n + cr[0], i, 0)),
                      pl.BlockSpec((1, tr, c), lambda n, i, cr: (n, i, 0))],
            out_specs=pl.BlockSpec((1, tr, c), lambda n, i, cr: (n, i, 0))),
        compiler_params=_params(),
    )(my_c, g, recv)


def _chip_sum(h, recv, my_chip):
    _, r, c = h.shape
    tr = _row_tile(r, c)

    def body(j_ref, h_ref, r0_ref, r1_ref, r2_ref, o_ref):
        o_ref[...] = ((h_ref[0].astype(F32) + r0_ref[0].astype(F32)) + r1_ref[0].astype(F32)) + r2_ref[0].astype(F32)

    return pl.pallas_call(
        body, name="rs_chip_sum", out_shape=jax.ShapeDtypeStruct((r, c), F32),
        grid_spec=pltpu.PrefetchScalarGridSpec(
            num_scalar_prefetch=1, grid=(r // tr,),
            in_specs=[pl.BlockSpec((1, tr, c), lambda i, jr: (jr[0], i, 0))]
            + [pl.BlockSpec((1, tr, c), lambda i, jr, n=n: (n, i, 0)) for n in range(3)],
            out_specs=pl.BlockSpec((tr, c), lambda i, jr: (i, 0))),
        compiler_params=_params(),
    )(my_chip, h, recv, recv, recv)


def _sum_blocks(g):
    n, r, c = g.shape

    def body(g_ref, o_ref):
        acc = g_ref[0]
        for k in range(1, n):
            acc = acc + g_ref[k]
        o_ref[...] = acc

    return pl.pallas_call(body, name="sum_blocks", out_shape=jax.ShapeDtypeStruct((r, c), F32), compiler_params=_params())(g)


def _pack(flat_parts, cols, row_multiple, dtype):
    flat = jnp.concatenate([f.astype(dtype) for f in flat_parts])
    per_row_block = cols * row_multiple
    padded = -(-flat.shape[0] // per_row_block) * per_row_block
    return jnp.pad(flat, (0, padded - flat.shape[0])).reshape(padded // cols, cols)


def _shard_shape(name):
    shape, axis = WEIGHTS[name]
    if axis is None:
        return shape
    return tuple(s // N_DEV if a == axis else s for a, s in enumerate(shape))


def _size(shape):
    n = 1
    for s in shape:
        n *= s
    return n


def _unshard(blocks, name):
    _, axis = WEIGHTS[name]
    return jnp.concatenate([blocks[k] for k in range(N_DEV)], axis=axis)


def _unpack_blocks(flat, names):
    out, off = {}, 0
    for n in names:
        shp = _shard_shape(n)
        out[n] = flat[..., off:off + _size(shp)].reshape(flat.shape[:-1] + shp)
        off += _size(shp)
    return out


def _layer_slabs(shard, i):
    j, kind = i // 3, i % 3
    w_out = (shard["gla_w_out"], shard["mla_w_out"], shard["conv_w_out"])[kind][j]
    out = {"a": jnp.concatenate([shard["mlp_w2"][i], shard["mlp_w1"][i].T, w_out, shard["ple_w_gate"][i]], axis=0).astype(BF16),
           "b": shard["ple_w_proj"][i].T.astype(BF16)}
    if kind == 0:
        out["gla"] = shard["gla_w_in"][j].T.astype(BF16)
    elif kind == 1:
        out["mla_in"] = _pad_cols(shard["mla_w_in"][j], MLA_IN_PAD).astype(BF16)
        out["mla_uq"] = _pad_cols(shard["mla_w_uq"][j], MLA_HEAD_PAD).astype(BF16)
        out["mla_ukv"] = shard["mla_w_ukv"][j].astype(BF16)
    else:
        out["conv"] = shard["conv_w_in"][j].T.astype(BF16)
    return out


def _mixer_weights(landed, i):
    kind = i % 3
    if kind == 0:
        return {"gla_w_in_t": landed["gla"].reshape(-1, D_MODEL)}
    if kind == 2:
        return {"conv": landed["conv"]}
    heads_side_by_side = lambda g: g.transpose(1, 0, 2).reshape(g.shape[1], -1)
    return {"mla_in": landed["mla_in"], "mla_w_uq": heads_side_by_side(landed["mla_uq"]),
            "mla_w_ukv": heads_side_by_side(landed["mla_ukv"])}


def _mixer_grad_buffers(layer_grads, i):
    kind = i % 3
    if kind == 0:
        return {"gla": layer_grads["gla_w_in_t"].reshape(N_DEV, -1, D_MODEL).astype(BF16)}
    if kind == 2:
        return {"conv": layer_grads["conv"]}
    head_blocks = lambda g: g.reshape(g.shape[0], N_DEV, -1).transpose(1, 0, 2).astype(BF16)
    return {"mla_in": layer_grads["mla_in"], "mla_uq": head_blocks(layer_grads["mla_uq"]),
            "mla_ukv": head_blocks(layer_grads["mla_ukv"])}


SLAB_KEYS = ("a", "b")


class _Overlap:
    def __init__(self, shard, small_pack):
        mx, my, mc = _my_place()
        self.my_c = mc.astype(jnp.int32).reshape(1)
        self.my_chip = (2 * mx + my).astype(jnp.int32).reshape(1)
        mine = 4 * mx + 2 * my + mc
        def landing_of(slabs):
            return {k: lax.dynamic_update_index_in_dim(lax.empty((N_DEV, *v.shape), v.dtype), v, mine, 0)
                    for k, v in slabs.items()}

        first = _layer_slabs(shard, 0)
        first["small"] = small_pack
        self.landing = [landing_of(first)]
        self.fly = {}
        self.grads = [{} for _ in range(DEPTH)]
        self.reduced = [{} for _ in range(DEPTH)]
        tok = self._gather_first(0, "mixer", shard["ln_g"])
        tok = self._gather_first(0, "slab", tok)
        shard, tok = lax.optimization_barrier((shard, tok))
        self.landing += [landing_of(_layer_slabs(shard, i)) for i in range(1, DEPTH)]
        bufs = self._wait("ag_first_mixer_l0", tok)
        self.landing[0].update(zip(self._keys(self.landing[0], "mixer"),
                                   _exchange("ag_forward_mixer_l0", bufs, _gather_forward_copies(len(bufs)))))

    @staticmethod
    def _keys(names, group):
        return [k for k in names if (k in SLAB_KEYS) == (group == "slab")]

    def _start(self, name, bufs, copies, after):
        sems, bufs, tok = _exchange_start(name + "_start", bufs, copies, after)
        self.fly[name] = (sems, bufs, copies)
        return tok

    def _wait(self, name, after):
        sems, bufs, copies = self.fly.pop(name)
        return _exchange_wait(name + "_wait", sems, bufs, copies, after)

    def mixer_weights(self, i):
        return _mixer_weights(self.landing[i], i)

    def slab_weights(self, i, dep):
        self._gather_done(i, "slab", dep)
        return self.landing[i]["a"], self.landing[i]["b"]

    def slab_grads(self, i, ga, gb):
        self.grads[i].update(a=ga, b=gb)

    def mixer_grads(self, i, layer_grads):
        self.grads[i].update(_mixer_grad_buffers(layer_grads, i))

    def at(self, phase, i, point, dep):
        toks = []
        if phase == "fwd":
            if point == "begin" and i == 0:
                toks.append(self._gather_first(1, "mixer", self.landing[0][self._keys(self.landing[0], "mixer")[0]]))
                toks.append(self._gather_first(1, "slab", toks[-1]))
            if point == "proj_done":
                toks.append(self._gather_forward(i, "slab", dep))
            if point == "mid" and i + 1 < DEPTH:
                toks.append(self._gather_forward(i + 1, "mixer", dep))
                if i + 2 < DEPTH:
                    toks.append(self._gather_first(i + 2, "mixer", dep))
                    toks.append(self._gather_first(i + 2, "slab", toks[-1]))
            if point == "end" and i + 1 < DEPTH:
                self._gather_done(i + 1, "mixer", dep)
        else:
            if point == "begin" and i + 1 < DEPTH:
                toks.append(self._scatter_cores(i + 1, "mixer", dep))
            if point == "ln" and i + 1 < DEPTH:
                toks.append(self._scatter_chips(i + 1, "mixer", dep))
            if point == "slab_done":
                if i + 1 < DEPTH:
                    self._scatter_done(i + 1, "slab", dep)
                    self._scatter_done(i + 1, "mixer", dep)
                toks.append(self._scatter_cores(i, "slab", dep))
            if point == "mixer_done":
                toks.append(self._scatter_chips(i, "slab", dep))
        return toks or None

    def _gather_first(self, i, group, after):
        bufs = [self.landing[i][k] for k in self._keys(self.landing[i], group)]
        return self._start(f"ag_first_{group}_l{i}", bufs, _gather_first_copies(len(bufs)), after)

    def _gather_forward(self, i, group, after):
        bufs = self._wait(f"ag_first_{group}_l{i}", after)
        return self._start(f"ag_forward_{group}_l{i}", bufs, _gather_forward_copies(len(bufs)), after)

    def _gather_done(self, i, group, after):
        keys = self._keys(self.landing[i], group)
        self.landing[i].update(zip(keys, self._wait(f"ag_forward_{group}_l{i}", after)))

    def _scatter_cores(self, i, group, after):
        gs = [self.grads[i][k] for k in self._keys(self.grads[i], group)]
        land = [lax.empty((4, *g.shape[1:]), g.dtype) for g in gs]
        return self._start(f"rs_cores_{group}_l{i}", gs + land, _scatter_core_copies(len(gs)), after)

    def _pair_sums(self, bufs):
        n = len(bufs) // 2
        hs = [_pair_sum(g, r, self.my_c) for g, r in zip(bufs[:n], bufs[n:])]
        return hs + [lax.empty((3, *h.shape[1:]), h.dtype) for h in hs]

    def _scatter_chips(self, i, group, after):
        bufs = self._pair_sums(self._wait(f"rs_cores_{group}_l{i}", after))
        return self._start(f"rs_chips_{group}_l{i}", bufs, _scatter_chip_copies(len(bufs) // 2), after)

    def _chip_sums(self, i, group, bufs):
        n = len(bufs) // 2
        for k, h, r in zip(self._keys(self.grads[i], group), bufs[:n], bufs[n:]):
            self.reduced[i][k] = _chip_sum(h, r, self.my_chip)

    def _scatter_done(self, i, group, after):
        self._chip_sums(i, group, self._wait(f"rs_chips_{group}_l{i}", after))

    def tail_begin(self, dep):
        return self._scatter_cores(0, "mixer", dep)

    def tail_middle(self, dep):
        self._scatter_done(0, "slab", dep)
        return self._scatter_chips(0, "mixer", dep)

    def tail_end(self, dep):
        self._scatter_done(0, "mixer", dep)


def _small_gather_start(x, name, after):
    mx, my, mc = _my_place()
    land = lax.dynamic_update_index_in_dim(lax.empty((N_DEV, *x.shape), x.dtype), x, 4 * mx + 2 * my + mc, 0)
    return name, _exchange_start(name + "_first_start", [land], _gather_first_copies(1), after)


def _small_gather_finish(started, after):
    name, (sems, bufs, _) = started
    bufs = _exchange_wait(name + "_first_wait", sems, bufs, _gather_first_copies(1), after)
    return _exchange(name + "_forward", bufs, _gather_forward_copies(1))[0]


def _adamw_math(w, g, m, v):
    m2 = ADAM_B1 * m + (1.0 - ADAM_B1) * g
    v2 = ADAM_B2 * v + (1.0 - ADAM_B2) * (g * g)
    m_hat = m2 / (1.0 - ADAM_B1 ** ADAM_STEP)
    v_hat = v2 / (1.0 - ADAM_B2 ** ADAM_STEP)
    return -ADAM_LR * (m_hat / (jnp.sqrt(v_hat) + ADAM_EPS) + ADAM_WD * w), m2, v2


ADAMW_TILE_BYTES = 1024 * 1024


def _adamw_layer(name, w, m, v, j, g, g_at, transposed, chain, after):
    n_layers, r, c = w.shape
    tr = max(t for t in range(8, r + 1, 8) if r % t == 0 and (t * c * 4 <= ADAMW_TILE_BYTES or t == 8))
    rb, rows = g_at
    if transposed:
        assert rows == c and g.shape[1] == r, (name, g.shape, g_at)
        g_spec = pl.BlockSpec((rows, tr), lambda i: (rb, i))
    else:
        assert rows == r and g.shape[1] == c, (name, g.shape, g_at)
        g_spec = pl.BlockSpec((tr, c), lambda i: (rb * (r // tr) + i, 0))
    extra = list(chain or []) + [a for a in (after or []) if a is not None]
    n_chain = 4 if chain else 0

    def body(w_ref, m_ref, v_ref, g_ref, *rest):
        g_out, d_out, m_out, v_out, tok_ref = rest[len(extra):]
        gv = g_ref[...].T if transposed else g_ref[...]
        g_out[0] = gv
        d_out[0], m_out[0], v_out[0] = _adamw_math(w_ref[0], gv, m_ref[0], v_ref[0])
        tok_ref[...] = jnp.zeros_like(tok_ref)

    layer_spec = pl.BlockSpec((1, tr, c), lambda i: (j, i, 0))
    outs = pl.pallas_call(
        body, name=f"adamw_{name}_l{j}", grid=(r // tr,),
        in_specs=[layer_spec] * 3 + [g_spec] + [pl.BlockSpec(memory_space=pl.ANY)] * len(extra),
        out_specs=[layer_spec] * 4 + [pl.BlockSpec((8, LANES), lambda i: (0, 0))],
        out_shape=[jax.ShapeDtypeStruct(w.shape, F32)] * 4 + [jax.ShapeDtypeStruct((8, LANES), F32)],
        input_output_aliases={4 + k: k for k in range(n_chain)}, compiler_params=_params(),
    )(w, m, v, g, *extra)
    return list(outs[:4]), outs[4]


def _adamw(w, g, m, v, name):
    shape = w.shape
    cols = shape[-1]
    rows = _size(shape) // cols
    tr = rows
    for cand in (512, 256, 128, 64, 32, 16, 8):
        if rows > cand and rows % cand == 0:
            tr = cand
            break

    def body(w_ref, g_ref, m_ref, v_ref, d_ref, mo_ref, vo_ref):
        d_ref[...], mo_ref[...], vo_ref[...] = _adamw_math(w_ref[...], g_ref[...], m_ref[...], v_ref[...])

    spec = pl.BlockSpec((tr, cols), lambda i: (i, 0))
    outs = pl.pallas_call(
        body, name="adamw_" + name, grid=(rows // tr,), in_specs=[spec] * 4, out_specs=[spec] * 3,
        out_shape=[jax.ShapeDtypeStruct((rows, cols), F32)] * 3, compiler_params=_params(),
    )(*[a.reshape(rows, cols) for a in (w, g, m, v)])
    return [o.reshape(shape) for o in outs]


def kernel(x, p, positions, gla_w_in, gla_w_gate_up, gla_b_gate, gla_norm_g, gla_w_out, mla_w_in, mla_q_norm, mla_kv_norm, mla_w_uq, mla_w_ukv, mla_w_out, conv_w_in, conv_w, conv_w_out, ln_g, ln_b, mlp_w1, mlp_w2, ple_w_gate, ple_w_proj, loss_target, m_gla_w_in, m_gla_w_gate_up, m_gla_b_gate, m_gla_norm_g, m_gla_w_out, m_mla_w_in, m_mla_q_norm, m_mla_kv_norm, m_mla_w_uq, m_mla_w_ukv, m_mla_w_out, m_conv_w_in, m_conv_w, m_conv_w_out, m_ln_g, m_ln_b, m_mlp_w1, m_mlp_w2, m_ple_w_gate, m_ple_w_proj, v_gla_w_in, v_gla_w_gate_up, v_gla_b_gate, v_gla_norm_g, v_gla_w_out, v_mla_w_in, v_mla_q_norm, v_mla_kv_norm, v_mla_w_uq, v_mla_w_ukv, v_mla_w_out, v_conv_w_in, v_conv_w, v_conv_w_out, v_ln_g, v_ln_b, v_mlp_w1, v_mlp_w2, v_ple_w_gate, v_ple_w_proj):
    args = locals()
    shard = {n: args[n] for n in WEIGHT_NAMES}
    mom = {n: args["m_" + n] for n in WEIGHT_NAMES}
    var = {n: args["v_" + n] for n in WEIGHT_NAMES}
    mx, my, mc = _my_place()

    comm = _Overlap(shard, _pack([shard[n].reshape(-1) for n in SMALL], LANES, 8, F32))
    small_all = comm.landing[0]["small"]
    small = {n: shard[n] for n in REPLICATED}
    small.update({n: _unshard(blk, n) for n, blk in _unpack_blocks(small_all.reshape(N_DEV, -1), SMALL).items()})
    loss_part, grad_x, small_grads = _step(x[0], p[:, 0], positions[0], loss_target[0], small, comm)

    chains = {}

    def update(name, j, g, g_at, transposed, tok):
        chains[name], tok = _adamw_layer(name, shard[name], mom[name], var[name], j, g, g_at, transposed,
                                         chains.get(name), [tok])
        return tok

    def update_layer(i, groups, tok):
        j, kind = i // 3, i % 3
        red = comm.reduced[i]
        if "slab" in groups:
            tok = update("mlp_w2", i, red["a"], REG_W2, False, tok)
            tok = update("mlp_w1", i, red["a"], REG_W1T, True, tok)
            tok = update(("gla_w_out", "mla_w_out", "conv_w_out")[kind], j, red["a"], REG_WOUT, False, tok)
            tok = update("ple_w_gate", i, red["a"], REG_WG, False, tok)
            tok = update("ple_w_proj", i, red["b"], REG_WPT, True, tok)
        if "mixer" in groups:
            if kind == 0:
                tok = update("gla_w_in", j, red["gla"].T, (0, D_MODEL), False, tok)
            elif kind == 2:
                tok = update("conv_w_in", j, red["conv"], REG_CONV, True, tok)
            else:
                for n, g in (("mla_w_in", red["mla_in"][:, :MLA_IN]), ("mla_w_ukv", red["mla_ukv"]),
                             ("mla_w_uq", red["mla_uq"][:, :MLA_NOPE + MLA_ROPE])):
                    tok = update(n, j, g, (0, g.shape[0]), False, tok)
        return tok

    tok = comm.tail_begin(grad_x)
    tok = update_layer(3, ("slab", "mixer"), tok)
    tok = update_layer(2, ("slab", "mixer"), tok)
    tok = comm.tail_middle(tok)
    small_parts = [loss_part[0, :1]] + [small_grads[n].reshape(-1) for n in SMALL + REPLICATED]
    small_gather = _small_gather_start(_pack(small_parts, LANES, 8, F32), "ag_small_grads", tok)
    tok = update_layer(1, ("slab", "mixer"), small_gather[1][2])
    tok = update_layer(0, ("slab",), tok)
    comm.tail_end(tok)
    tok = update_layer(0, ("mixer",), tok)
    red_small = _sum_blocks(_small_gather_finish(small_gather, tok)).reshape(-1)
    loss = red_small[0]
    off = 1
    dev = 4 * mx + 2 * my + mc
    for n in SMALL + REPLICATED:
        shape, axis = WEIGHTS[n]
        full_g = red_small[off:off + _size(shape)].reshape(shape)
        off += _size(shape)
        if axis is not None:
            width = shape[axis] // N_DEV
            full_g = lax.dynamic_slice_in_dim(full_g, dev * width, width, axis=axis)
        chains[n] = [full_g, *_adamw(shard[n], full_g, mom[n], var[n], n)]
    return (loss, grad_x[None], *[chains[n][k] for k in range(4) for n in WEIGHT_NAMES])
```

```python
import functools

import jax
import jax.numpy as jnp
from jax import lax
from jax.experimental import pallas as pl
from jax.experimental.pallas import tpu as pltpu

F32, BF16 = jnp.float32, jnp.bfloat16
N_DEV = 8

D_MODEL = 1024
DEPTH = 4
CHUNK = 64
ALPHA = (2 * DEPTH) ** 0.25
LN_EPS = 1e-5
RMS_EPS = 1e-6
PLE_DIM = 256
D_FF = 4 * D_MODEL
GLA_HEADS = 4
GLA_DK = 128
GLA_DV = 256
GLA_RANK = 16
GLA_TAU = 16.0
GLA_HK = GLA_HEADS * GLA_DK
GLA_HV = GLA_HEADS * GLA_DV
GLA_MAIN = 2 * GLA_HK + GLA_HV + D_MODEL
MLA_HEADS = 8
MLA_NOPE = 128
MLA_ROPE = 64
MLA_V = 128
MLA_RANK = 256
MLA_IN = 2 * MLA_RANK + MLA_ROPE
MLA_IN_PAD = 640
ROPE_BASE = 10000.0
LANES = 128
ADAM_LR, ADAM_B1, ADAM_B2, ADAM_EPS, ADAM_WD, ADAM_STEP = 0.001, 0.9, 0.999, 1e-08, 0.01, 10

V7X_VMEM_LIMIT_BYTES = 56 * 1024 * 1024

WEIGHTS = {
    "gla_w_in": ((2, 1024, 3088), 2), "gla_w_gate_up": ((2, 16, 512), 2), "gla_b_gate": ((2, 512), 1),
    "gla_norm_g": ((2, 256), 1), "gla_w_out": ((2, 1024, 1024), 1), "mla_w_in": ((1, 1024, 576), 1),
    "mla_q_norm": ((1, 256), None), "mla_kv_norm": ((1, 256), None), "mla_w_uq": ((1, 256, 1536), 2),
    "mla_w_ukv": ((1, 256, 2048), 2), "mla_w_out": ((1, 1024, 1024), 1), "conv_w_in": ((1, 1024, 3072), 2),
    "conv_w": ((1, 3, 1024), 2), "conv_w_out": ((1, 1024, 1024), 1), "ln_g": ((4, 2, 1024), 2),
    "ln_b": ((4, 2, 1024), 2), "mlp_w1": ((4, 1024, 4096), 2), "mlp_w2": ((4, 4096, 1024), 1),
    "ple_w_gate": ((4, 1024, 1024), 1), "ple_w_proj": ((4, 256, 1024), 2),
}
WEIGHT_NAMES = list(WEIGHTS)
REG_W2, REG_W1T, REG_WOUT, REG_WG = (0, 512), (1, 512), (8, 128), (9, 128)
A_ROWS = 1280
REG_CONV = (0, 384)
REG_WPT = (0, 128)
REG_MLA_IN = (0, 128)
MLA_HEAD_PAD = 2 * LANES
SMALL = ["gla_w_gate_up", "gla_b_gate", "gla_norm_g", "conv_w", "ln_g", "ln_b"]
REPLICATED = ["mla_q_norm", "mla_kv_norm"]


def _params(**kw):
    return pltpu.CompilerParams(vmem_limit_bytes=V7X_VMEM_LIMIT_BYTES, **kw)


def _dot(a, b, ca, cb):
    return lax.dot_general(a, b, (((ca,), (cb,)), ((), ())), preferred_element_type=F32)


def _nn(a, b):
    return _dot(a.astype(BF16), b.astype(BF16), 1, 0)


def _nt(a, b):
    return _dot(a.astype(BF16), b.astype(BF16), 1, 1)


def _tn(a, b):
    return _dot(a.astype(BF16), b.astype(BF16), 0, 0)


@jax.custom_vjp
def mm_nn(a, b):
    return _nn(a, b)


def _mm_nn_fwd(a, b):
    return _nn(a, b), (a, b)


def _mm_nn_bwd(res, g):
    a, b = res
    return _nt(g, b).astype(a.dtype), _tn(a, g).astype(b.dtype)


mm_nn.defvjp(_mm_nn_fwd, _mm_nn_bwd)


@jax.custom_vjp
def mm_nt(a, b):
    return _nt(a, b)


def _mm_nt_fwd(a, b):
    return _nt(a, b), (a, b)


def _mm_nt_bwd(res, g):
    a, b = res
    return _nn(g, b).astype(a.dtype), _tn(g, a).astype(b.dtype)


mm_nt.defvjp(_mm_nt_fwd, _mm_nt_bwd)


@jax.custom_vjp
def mm_tn(a, b):
    return _tn(a, b)


def _mm_tn_fwd(a, b):
    return _tn(a, b), (a, b)


def _mm_tn_bwd(res, g):
    a, b = res
    return _nt(b, g).astype(a.dtype), _nn(a, g).astype(b.dtype)


mm_tn.defvjp(_mm_tn_fwd, _mm_tn_bwd)


def _iota2(shape, dim):
    return lax.broadcasted_iota(jnp.int32, shape, dim)


def _split3(x):
    hi = x.astype(BF16)
    rest = x - hi.astype(F32)
    mid = rest.astype(BF16)
    return hi, mid, (rest - mid.astype(F32)).astype(BF16)


def _tri_dot(tri, x):
    return sum(_dot(tri.astype(BF16), piece, 1, 0) for piece in _split3(x))


@jax.custom_vjp
def cumsum_rows(x):
    n = x.shape[0]
    return _tri_dot(_iota2((n, n), 0) >= _iota2((n, n), 1), x)


def _cumsum_fwd(x):
    return cumsum_rows(x), None


def _cumsum_bwd(_, g):
    n = g.shape[0]
    return (_tri_dot(_iota2((n, n), 0) <= _iota2((n, n), 1), g),)


cumsum_rows.defvjp(_cumsum_fwd, _cumsum_bwd)


def _rot_matrix(transposed):
    i, j = _iota2((LANES, LANES), 0), _iota2((LANES, LANES), 1)
    if transposed:
        i, j = j, i
    half = MLA_ROPE // 2
    plus = (i == j - half) & (j >= half) & (j < MLA_ROPE)
    minus = (i == j + half) & (j < half)
    return (plus.astype(F32) - minus.astype(F32)).astype(BF16)


def _rot_dot(x, transposed):
    return sum(_dot(piece, _rot_matrix(transposed), 1, 0) for piece in _split3(x))


@jax.custom_vjp
def rot_half(x):
    return _rot_dot(x, False)


def _rot_fwd(x):
    return rot_half(x), None


def _rot_bwd(_, g):
    return (_rot_dot(g, True),)


rot_half.defvjp(_rot_fwd, _rot_bwd)


def _shift_rows_raw(x, s):
    n = x.shape[0]
    row = _iota2(x.shape, 0)
    rolled = pltpu.roll(x, s % n, 0)
    keep = (row >= s) if s > 0 else (row < n + s)
    return jnp.where(keep, rolled, 0.0)


@functools.partial(jax.custom_vjp, nondiff_argnums=(1,))
def shift_rows(x, s):
    return _shift_rows_raw(x, s)


def _shift_fwd(x, s):
    return _shift_rows_raw(x, s), None


def _shift_bwd(s, _, g):
    return (_shift_rows_raw(g, -s),)


shift_rows.defvjp(_shift_fwd, _shift_bwd)


def _layer_norm(a, g, b):
    mu = jnp.mean(a, -1, keepdims=True)
    xc = a - mu
    var = jnp.mean(xc * xc, -1, keepdims=True)
    return xc * lax.rsqrt(var + LN_EPS) * g + b


def _rms_norm(a, g):
    return a * lax.rsqrt(jnp.mean(a * a, -1, keepdims=True) + RMS_EPS) * g


def _log_sigmoid(z):
    return jnp.minimum(z, 0.0) - jnp.log(1.0 + jnp.exp(-jnp.abs(z)))


def _matmul(a, b, *, name, ta=False, tb=False, tm=512, tn=512, a_fn=None, epi=None, epi_ins=(), out_dtypes=(BF16,),
            b_at=None, out_at=None, out_buf=None, after=None, n_row_sums=0, a_ins=(), a_out_dtypes=()):
    m = a.shape[1] if ta else a.shape[0]
    k = a.shape[0] if ta else a.shape[1]
    if b_at is None:
        n, kb = (b.shape[0], b.shape[1]) if tb else (b.shape[1], b.shape[0])
    else:
        rb, r = b_at
        n, kb = (N_DEV * r, b.shape[2]) if tb else (b.shape[2], N_DEV * r)
    assert kb == k, (name, a.shape, b.shape, k, kb)
    tm, tn = min(tm, m), min(tn, n)
    assert m % tm == 0 and n % tn == 0, (name, m, n, tm, tn)
    a_spec = pl.BlockSpec((k, tm), lambda i, j: (0, i)) if ta else pl.BlockSpec((tm, k), lambda i, j: (i, 0))
    if b_at is None:
        b_spec = pl.BlockSpec((tn, k), lambda i, j: (j, 0)) if tb else pl.BlockSpec((k, tn), lambda i, j: (0, j))
        load_b = lambda ref: ref[...]
    elif tb and tn == n:
        b_spec = pl.BlockSpec((N_DEV, r, k), lambda i, j: (0, rb, 0))
        load_b = lambda ref: ref[...].reshape(n, k)
    elif tb:
        assert tn == r, (name, tn, r)
        b_spec = pl.BlockSpec((1, r, k), lambda i, j: (j, rb, 0))
        load_b = lambda ref: ref[0]
    else:
        b_spec = pl.BlockSpec((N_DEV, r, tn), lambda i, j: (0, rb, j))
        load_b = lambda ref: ref[...].reshape(k, tn)
    e_specs = []
    for e in epi_ins:
        if e.shape == (1, n):
            e_specs.append(pl.BlockSpec((1, tn), lambda i, j: (0, j)))
        else:
            assert e.shape == (m, n), (name, e.shape, m, n)
            e_specs.append(pl.BlockSpec((tm, tn), lambda i, j: (i, j)))
    n_epi, n_ain, n_aout = len(epi_ins), len(a_ins), len(a_out_dtypes)
    assert n_aout == 0 or (tn == n and not ta and out_at is None), name
    ca, cb = (0 if ta else 1), (1 if tb else 0)
    operands = [a, b, *a_ins, *epi_ins]
    in_specs = [a_spec, b_spec, *[a_spec] * n_ain, *e_specs]
    if out_at is None:
        assert n_row_sums == 0 or tn == n, (name, tn, n)
        out_specs = [pl.BlockSpec((tm, tn), lambda i, j: (i, j)) for _ in out_dtypes]
        out_specs += [pl.BlockSpec((tm, k), lambda i, j: (i, 0))] * n_aout
        out_specs += [pl.BlockSpec((1, n), lambda i, j: (0, 0))] * n_row_sums
        out_shape = [jax.ShapeDtypeStruct((m, n), dt) for dt in out_dtypes]
        out_shape += [jax.ShapeDtypeStruct((m, k), dt) for dt in a_out_dtypes]
        out_shape += [jax.ShapeDtypeStruct((1, n), F32)] * n_row_sums
        aliases, n_buf = {}, 0
    else:
        orb, orows = out_at
        assert len(out_dtypes) == 1 and m == N_DEV * orows and n == out_buf.shape[2], (name, m, n)
        if tm > orows:
            assert tm % orows == 0, (name, tm, orows)
            out_specs = [pl.BlockSpec((tm // orows, orows, tn), lambda i, j: (i, orb, j))]
        else:
            per = orows // tm
            out_specs = [pl.BlockSpec((1, tm, tn), lambda i, j: (i // per, orb * per + i % per, j))]
        out_shape = [jax.ShapeDtypeStruct(out_buf.shape, out_buf.dtype)]
        operands.append(out_buf)
        in_specs.append(pl.BlockSpec(memory_space=pl.ANY))
        aliases, n_buf = {len(operands) - 1: 0}, 1
    for dep in ([] if after is None else after if isinstance(after, (list, tuple)) else [after]):
        if dep is not None:
            operands.append(dep)
            in_specs.append(pl.BlockSpec(memory_space=pl.ANY))
            n_buf += 1

    def body(a_ref, b_ref, *rest):
        av, a_outs = a_ref[...], ()
        if a_fn is not None:
            av = a_fn(av, *[r_[...] for r_ in rest[:n_ain]])
            if n_aout:
                av, *a_outs = av
        acc = _dot(av.astype(BF16), load_b(b_ref).astype(BF16), ca, cb)
        outs = epi(acc, *[r_[...] for r_ in rest[n_ain:n_ain + n_epi]]) if epi is not None else (acc,)
        o_refs = rest[n_ain + n_epi + n_buf:]
        n_tiles = len(o_refs) - n_row_sums - n_aout
        for o_ref, val in zip(o_refs[:n_tiles + n_aout], (*outs[:n_tiles], *a_outs)):
            o_ref[...] = val.astype(o_ref.dtype).reshape(o_ref.shape)
        if n_row_sums:
            @pl.when(pl.program_id(0) == 0)
            def _():
                for o_ref in o_refs[n_tiles + n_aout:]:
                    o_ref[...] = jnp.zeros_like(o_ref)

            for o_ref, val in zip(o_refs[n_tiles + n_aout:], outs[n_tiles:]):
                o_ref[...] += val

    outs = pl.pallas_call(
        body, name=name, grid=(m // tm, n // tn), in_specs=in_specs, out_specs=out_specs, out_shape=out_shape,
        input_output_aliases=aliases, compiler_params=_params(),
    )(*operands)
    return outs[0] if len(outs) == 1 else tuple(outs)


def _tile_fwd(f, tiled, params, out_dtypes, *, tm, name):
    t = tiled[0].shape[0]
    assert t % tm == 0
    out_avals = jax.eval_shape(f, *[jax.ShapeDtypeStruct((tm, x.shape[1]), F32) for x in tiled],
                               *[jax.ShapeDtypeStruct(p.shape, F32) for p in params])
    nt, npar = len(tiled), len(params)

    def body(*refs):
        ins = [r[...].astype(F32) for r in refs[:nt + npar]]
        outs = f(*ins)
        for o_ref, val in zip(refs[nt + npar:], outs):
            o_ref[...] = val.astype(o_ref.dtype)

    return pl.pallas_call(
        body, name=name, grid=(t // tm,),
        in_specs=[pl.BlockSpec((tm, x.shape[1]), lambda i: (i, 0)) for x in tiled]
        + [pl.BlockSpec(p.shape, lambda i: (0, 0)) for p in params],
        out_specs=[pl.BlockSpec((tm, o.shape[1]), lambda i: (i, 0)) for o in out_avals],
        out_shape=[jax.ShapeDtypeStruct((t, o.shape[1]), dt) for o, dt in zip(out_avals, out_dtypes)],
        compiler_params=_params(),
    )(*tiled, *params)


def _tile_bwd(f, tiled, params, cots, d_tiled_dtypes, *, tm, name, diff_tiled=None):
    t = tiled[0].shape[0]
    assert t % tm == 0
    nt, npar, nc = len(tiled), len(params), len(cots)
    diff_tiled = list(range(nt)) if diff_tiled is None else diff_tiled

    def body(*refs):
        ins = [r[...].astype(F32) for r in refs[:nt + npar]]
        cts = [r[...].astype(F32) for r in refs[nt + npar:nt + npar + nc]]
        o_refs = refs[nt + npar + nc:]
        _, vjp = jax.vjp(f, *ins)
        grads = vjp(tuple(cts))
        for o_ref, idx in zip(o_refs[:len(diff_tiled)], diff_tiled):
            o_ref[...] = grads[idx].astype(o_ref.dtype)
        p_refs = o_refs[len(diff_tiled):]

        @pl.when(pl.program_id(0) == 0)
        def _():
            for p_ref in p_refs:
                p_ref[...] = jnp.zeros_like(p_ref)

        for p_ref, gp in zip(p_refs, grads[nt:]):
            p_ref[...] += gp

    outs = pl.pallas_call(
        body, name=name, grid=(t // tm,),
        in_specs=[pl.BlockSpec((tm, x.shape[1]), lambda i: (i, 0)) for x in tiled]
        + [pl.BlockSpec(p.shape, lambda i: (0, 0)) for p in params]
        + [pl.BlockSpec((tm, c.shape[1]), lambda i: (i, 0)) for c in cots],
        out_specs=[pl.BlockSpec((tm, tiled[idx].shape[1]), lambda i: (i, 0)) for idx in diff_tiled]
        + [pl.BlockSpec(p.shape, lambda i: (0, 0)) for p in params],
        out_shape=[jax.ShapeDtypeStruct(tiled[idx].shape, dt) for idx, dt in zip(diff_tiled, d_tiled_dtypes)]
        + [jax.ShapeDtypeStruct(p.shape, F32) for p in params],
        compiler_params=_params(),
    )(*tiled, *params, *cots)
    return outs[:len(diff_tiled)], outs[len(diff_tiled):]


def _gla_head(q, k, v, r, z, g, st):
    c = q.shape[0]
    causal = _iota2((c, c), 0) >= _iota2((c, c), 1)
    la = _log_sigmoid(z) * (1.0 / GLA_TAU)
    big_l = cumsum_rows(la)
    ep, en = jnp.exp(big_l), jnp.exp(-big_l)
    qs = q * (GLA_DK ** -0.5)
    qp = qs * ep
    s = jnp.where(causal, mm_nt(qp, k * en), mm_nt(qs * en, k * ep))
    o = mm_nn(s, v) + mm_nt(qp, st)
    l_end = jnp.sum(la, axis=0, keepdims=True)
    st_new = st * jnp.exp(l_end) + mm_tn(v, k * jnp.exp(l_end - big_l))
    u = _rms_norm(o, g) * (r * jax.nn.sigmoid(r))
    return u, st_new


def _gla_slices(h):
    q = slice(GLA_DK * h, GLA_DK * (h + 1))
    k = slice(GLA_HK + GLA_DK * h, GLA_HK + GLA_DK * (h + 1))
    v = slice(2 * GLA_HK + GLA_DV * h, 2 * GLA_HK + GLA_DV * (h + 1))
    r = slice(2 * GLA_HK + GLA_HV + GLA_DV * h, 2 * GLA_HK + GLA_HV + GLA_DV * (h + 1))
    return q, k, v, r


GLA_CHUNKS_PER_STEP = 4


def _gla_fwd(proj, z, norm_g, after):
    t = proj.shape[0]
    nc, per = t // CHUNK, GLA_CHUNKS_PER_STEP
    rows_per_step = per * CHUNK
    after = [a for a in after if a is not None]

    def body(proj_ref, z_ref, g_ref, *rest):
        u_ref, st_save_ref, st_ref = rest[len(after):]

        @pl.when(pl.program_id(0) == 0)
        def _():
            st_ref[...] = jnp.zeros_like(st_ref)

        g = g_ref[...]
        for h in range(GLA_HEADS):
            sq, sk, sv, sr = _gla_slices(h)
            st = st_ref[h]
            for c in range(per):
                rows = slice(c * CHUNK, (c + 1) * CHUNK)
                st_save_ref[c, h] = st
                u, st = _gla_head(proj_ref[rows, sq].astype(F32), proj_ref[rows, sk].astype(F32),
                                  proj_ref[rows, sv].astype(F32), proj_ref[rows, sr].astype(F32),
                                  z_ref[rows, GLA_DK * h:GLA_DK * (h + 1)], g, st)
                u_ref[rows, GLA_DV * h:GLA_DV * (h + 1)] = u.astype(u_ref.dtype)
            st_ref[h] = st

    return pl.pallas_call(
        body, name="gla_fwd", grid=(nc // per,),
        in_specs=[pl.BlockSpec((rows_per_step, GLA_MAIN), lambda i: (i, 0)),
                  pl.BlockSpec((rows_per_step, GLA_HK), lambda i: (i, 0)), pl.BlockSpec((1, GLA_DV), lambda i: (0, 0))]
        + [pl.BlockSpec(memory_space=pl.ANY)] * len(after),
        out_specs=[pl.BlockSpec((rows_per_step, GLA_HV), lambda i: (i, 0)),
                   pl.BlockSpec((per, GLA_HEADS, GLA_DV, GLA_DK), lambda i: (i, 0, 0, 0))],
        out_shape=[jax.ShapeDtypeStruct((t, GLA_HV), BF16), jax.ShapeDtypeStruct((nc, GLA_HEADS, GLA_DV, GLA_DK), F32)],
        scratch_shapes=[pltpu.VMEM((GLA_HEADS, GLA_DV, GLA_DK), F32)],
        compiler_params=_params(),
    )(proj, z, norm_g, *after)


def _gla_bwd(proj, z, norm_g, states, du, after):
    t = proj.shape[0]
    nc, per = t // CHUNK, GLA_CHUNKS_PER_STEP
    rows_per_step = per * CHUNK
    n_steps = nc // per
    after = [a for a in after if a is not None]

    def body(proj_ref, z_ref, g_ref, st_in_ref, du_ref, *rest):
        dproj_ref, dz_ref, dg_ref, dzsum_ref, dst_ref = rest[len(after):]

        @pl.when(pl.program_id(0) == 0)
        def _():
            dst_ref[...] = jnp.zeros_like(dst_ref)
            dg_ref[...] = jnp.zeros_like(dg_ref)
            dzsum_ref[...] = jnp.zeros_like(dzsum_ref)

        g = g_ref[...]
        for h in range(GLA_HEADS):
            sq, sk, sv, sr = _gla_slices(h)
            dst = dst_ref[h]
            for c in reversed(range(per)):
                rows = slice(c * CHUNK, (c + 1) * CHUNK)
                ins = (proj_ref[rows, sq].astype(F32), proj_ref[rows, sk].astype(F32), proj_ref[rows, sv].astype(F32),
                       proj_ref[rows, sr].astype(F32), z_ref[rows, GLA_DK * h:GLA_DK * (h + 1)], g, st_in_ref[c, h])
                _, vjp = jax.vjp(_gla_head, *ins)
                dq, dk, dv, dr, dz, dg, dst = vjp((du_ref[rows, GLA_DV * h:GLA_DV * (h + 1)], dst))
                dproj_ref[rows, sq] = dq.astype(dproj_ref.dtype)
                dproj_ref[rows, sk] = dk.astype(dproj_ref.dtype)
                dproj_ref[rows, sv] = dv.astype(dproj_ref.dtype)
                dproj_ref[rows, sr] = dr.astype(dproj_ref.dtype)
                dz_ref[rows, GLA_DK * h:GLA_DK * (h + 1)] = dz
                dzsum_ref[:, GLA_DK * h:GLA_DK * (h + 1)] += jnp.sum(dz, axis=0, keepdims=True)
                dg_ref[...] += dg
            dst_ref[h] = dst

    rev = lambda i: (n_steps - 1 - i, 0)
    return pl.pallas_call(
        body, name="gla_bwd", grid=(n_steps,),
        in_specs=[pl.BlockSpec((rows_per_step, GLA_MAIN), rev), pl.BlockSpec((rows_per_step, GLA_HK), rev),
                  pl.BlockSpec((1, GLA_DV), lambda i: (0, 0)),
                  pl.BlockSpec((per, GLA_HEADS, GLA_DV, GLA_DK), lambda i: (n_steps - 1 - i, 0, 0, 0)),
                  pl.BlockSpec((rows_per_step, GLA_HV), rev)] + [pl.BlockSpec(memory_space=pl.ANY)] * len(after),
        out_specs=[pl.BlockSpec((rows_per_step, GLA_MAIN), rev), pl.BlockSpec((rows_per_step, GLA_HK), rev),
                   pl.BlockSpec((1, GLA_DV), lambda i: (0, 0)), pl.BlockSpec((1, GLA_HK), lambda i: (0, 0))],
        out_shape=[jax.ShapeDtypeStruct((t, GLA_MAIN), BF16), jax.ShapeDtypeStruct((t, GLA_HK), F32),
                   jax.ShapeDtypeStruct((1, GLA_DV), F32), jax.ShapeDtypeStruct((1, GLA_HK), F32)],
        scratch_shapes=[pltpu.VMEM((GLA_HEADS, GLA_DV, GLA_DK), F32)],
        compiler_params=_params(),
    )(proj, z, norm_g, states, du, *after)


def _mla_pre(cq, cos, sin, gq, gkv, w_uq, w_ukv):
    qlat = _rms_norm(cq[:, :MLA_RANK], gq)
    kvlat = _rms_norm(cq[:, MLA_RANK:2 * MLA_RANK], gkv)
    kr = cq[:, 2 * MLA_RANK:]
    q = mm_nn(qlat, w_uq) * ((MLA_NOPE + MLA_ROPE) ** -0.5)
    kv = mm_nn(kvlat, w_ukv)
    pieces = []
    for h in range(MLA_HEADS):
        qr = q[:, MLA_HEAD_PAD * h + MLA_NOPE:MLA_HEAD_PAD * (h + 1)]
        pieces += [q[:, MLA_HEAD_PAD * h:MLA_HEAD_PAD * h + MLA_NOPE], qr * cos + rot_half(qr) * sin]
    return jnp.concatenate(pieces, axis=1), kv, kr * cos + rot_half(kr) * sin


MLA_Q_TILE = 512


def _mla_attn_block(qn, qr, kv, kr, q0):
    tq, nk = qn.shape[0], kv.shape[0]
    s = mm_nt(qn, kv[:, :MLA_NOPE]) + mm_nt(qr, kr)
    visible = (_iota2((tq, nk), 1) // CHUNK) <= ((q0 + _iota2((tq, nk), 0)) // CHUNK)
    s = jnp.where(visible, s, -1e30)
    e = jnp.exp(s - jnp.max(s, -1, keepdims=True))
    p = e / jnp.sum(e, -1, keepdims=True)
    return mm_nn(p, kv[:, MLA_NOPE:])


def _mla_attn_fwd(q, kv, kr, after):
    t = q.shape[0]
    after = [a for a in after if a is not None]

    def body(q_ref, kv_ref, kr_ref, *rest):
        (o_ref,) = rest[len(after):]
        for i in range(t // MLA_Q_TILE):
            rows = slice(i * MLA_Q_TILE, (i + 1) * MLA_Q_TILE)
            keys = slice(0, (i + 1) * MLA_Q_TILE)
            o = _mla_attn_block(q_ref[rows, :MLA_NOPE].astype(F32), q_ref[rows, MLA_NOPE:].astype(F32),
                                kv_ref[keys, :].astype(F32), kr_ref[keys, :].astype(F32), i * MLA_Q_TILE)
            o_ref[rows, :] = o.astype(o_ref.dtype)

    return pl.pallas_call(
        body, name="mla_attn_fwd", grid=(MLA_HEADS,),
        in_specs=[pl.BlockSpec((t, MLA_HEAD_PAD), lambda h: (0, h)),
                  pl.BlockSpec((t, MLA_NOPE + MLA_V), lambda h: (0, h)), pl.BlockSpec((t, LANES), lambda h: (0, 0))]
        + [pl.BlockSpec(memory_space=pl.ANY)] * len(after),
        out_specs=pl.BlockSpec((t, MLA_V), lambda h: (0, h)),
        out_shape=jax.ShapeDtypeStruct((t, MLA_HEADS * MLA_V), BF16),
        compiler_params=_params(),
    )(q, kv, kr, *after)


def _mla_attn_bwd(q, kv, kr, do, after):
    t = q.shape[0]
    after = [a for a in after if a is not None]

    def body(q_ref, kv_ref, kr_ref, do_ref, *rest):
        dq_ref, dkv_ref, dkr_ref = rest[len(after):]
        dkv_ref[...] = jnp.zeros_like(dkv_ref)

        @pl.when(pl.program_id(0) == 0)
        def _():
            dkr_ref[...] = jnp.zeros_like(dkr_ref)

        for i in range(t // MLA_Q_TILE):
            rows = slice(i * MLA_Q_TILE, (i + 1) * MLA_Q_TILE)
            keys = slice(0, (i + 1) * MLA_Q_TILE)
            f = functools.partial(_mla_attn_block, q0=i * MLA_Q_TILE)
            _, vjp = jax.vjp(f, q_ref[rows, :MLA_NOPE].astype(F32), q_ref[rows, MLA_NOPE:].astype(F32),
                             kv_ref[keys, :].astype(F32), kr_ref[keys, :].astype(F32))
            dqn, dqr, dkv, dkr = vjp(do_ref[rows, :].astype(F32))
            dq_ref[rows, :MLA_NOPE] = dqn
            dq_ref[rows, MLA_NOPE:] = dqr
            dkv_ref[keys, :] += dkv
            dkr_ref[keys, :] += dkr

    return pl.pallas_call(
        body, name="mla_attn_bwd", grid=(MLA_HEADS,),
        in_specs=[pl.BlockSpec((t, MLA_HEAD_PAD), lambda h: (0, h)),
                  pl.BlockSpec((t, MLA_NOPE + MLA_V), lambda h: (0, h)), pl.BlockSpec((t, LANES), lambda h: (0, 0)),
                  pl.BlockSpec((t, MLA_V), lambda h: (0, h))] + [pl.BlockSpec(memory_space=pl.ANY)] * len(after),
        out_specs=[pl.BlockSpec((t, MLA_HEAD_PAD), lambda h: (0, h)),
                   pl.BlockSpec((t, MLA_NOPE + MLA_V), lambda h: (0, h)), pl.BlockSpec((t, LANES), lambda h: (0, 0))],
        out_shape=[jax.ShapeDtypeStruct(q.shape, F32), jax.ShapeDtypeStruct(kv.shape, F32),
                   jax.ShapeDtypeStruct(kr.shape, F32)],
        compiler_params=_params(),
    )(q, kv, kr, do, *after)


def _rope_tables(pos_col, inv_freq_row):
    t = pos_col.shape[0]

    def body(pos_ref, f_ref, cos_ref, sin_ref):
        ang = pos_ref[...].astype(F32) * f_ref[...]
        live = _iota2(ang.shape, 1) < MLA_ROPE
        cos_ref[...] = jnp.where(live, jnp.cos(ang), 0.0)
        sin_ref[...] = jnp.where(live, jnp.sin(ang), 0.0)

    return pl.pallas_call(
        body, name="rope_tables", out_shape=[jax.ShapeDtypeStruct((t, LANES), F32)] * 2, compiler_params=_params(),
    )(pos_col, inv_freq_row)


CONV_COL_TILE = 256


def _conv_gate(b, c, u, w0, w1, w2):
    cu = c * u
    return b * (w2 * cu + w1 * shift_rows(cu, 1) + w0 * shift_rows(cu, 2))


def _conv_specs(t):
    nb = D_MODEL // CONV_COL_TILE
    return [pl.BlockSpec((t, CONV_COL_TILE), lambda j, part=part: (0, part * nb + j)) for part in range(3)]


def _conv_fwd(bcu, w, after):
    t = bcu.shape[0]
    after = [a for a in after if a is not None]

    def body(b_ref, c_ref, u_ref, w_ref, *rest):
        (o_ref,) = rest[len(after):]
        o_ref[...] = _conv_gate(b_ref[...], c_ref[...], u_ref[...], w_ref[0:1, :], w_ref[1:2, :],
                                w_ref[2:3, :]).astype(o_ref.dtype)

    return pl.pallas_call(
        body, name="conv_fwd", grid=(D_MODEL // CONV_COL_TILE,),
        in_specs=_conv_specs(t) + [pl.BlockSpec((3, CONV_COL_TILE), lambda j: (0, j))]
        + [pl.BlockSpec(memory_space=pl.ANY)] * len(after),
        out_specs=pl.BlockSpec((t, CONV_COL_TILE), lambda j: (0, j)),
        out_shape=jax.ShapeDtypeStruct((t, D_MODEL), BF16), compiler_params=_params(),
    )(bcu, bcu, bcu, w, *after)


def _conv_bwd(bcu, w, dout, after):
    t = bcu.shape[0]
    after = [a for a in after if a is not None]

    def body(b_ref, c_ref, u_ref, w_ref, do_ref, *rest):
        db_ref, dc_ref, du_ref, dw_ref = rest[len(after):]
        _, vjp = jax.vjp(_conv_gate, b_ref[...], c_ref[...], u_ref[...], w_ref[0:1, :], w_ref[1:2, :], w_ref[2:3, :])
        db, dc, du, dw0, dw1, dw2 = vjp(do_ref[...])
        db_ref[...] = db.astype(db_ref.dtype)
        dc_ref[...] = dc.astype(dc_ref.dtype)
        du_ref[...] = du.astype(du_ref.dtype)
        dw_ref[0:1, :] = dw0
        dw_ref[1:2, :] = dw1
        dw_ref[2:3, :] = dw2

    col = pl.BlockSpec((t, CONV_COL_TILE), lambda j: (0, j))
    return pl.pallas_call(
        body, name="conv_bwd", grid=(D_MODEL // CONV_COL_TILE,),
        in_specs=_conv_specs(t) + [pl.BlockSpec((3, CONV_COL_TILE), lambda j: (0, j)), col]
        + [pl.BlockSpec(memory_space=pl.ANY)] * len(after),
        out_specs=[col, col, col, pl.BlockSpec((3, CONV_COL_TILE), lambda j: (0, j))],
        out_shape=[jax.ShapeDtypeStruct((t, D_MODEL), BF16)] * 3 + [jax.ShapeDtypeStruct((3, D_MODEL), F32)],
        compiler_params=_params(),
    )(bcu, bcu, bcu, w, dout, *after)


def _loss_head(y, target):
    t, d = y.shape
    tm = 256

    def body(y_ref, t_ref, loss_ref, dy_ref):
        @pl.when(pl.program_id(0) == 0)
        def _():
            loss_ref[...] = jnp.zeros_like(loss_ref)

        err = y_ref[...] - t_ref[...]
        dy_ref[...] = err * (1.0 / d)
        loss_ref[...] += 0.5 * jnp.sum(jnp.sum(err * err, axis=-1, keepdims=True) * (1.0 / d))

    tile = pl.BlockSpec((tm, d), lambda i: (i, 0))
    return pl.pallas_call(
        body, name="loss_head", grid=(t // tm,), in_specs=[tile, tile],
        out_specs=[pl.BlockSpec((8, LANES), lambda i: (0, 0)), tile],
        out_shape=[jax.ShapeDtypeStruct((8, LANES), F32), jax.ShapeDtypeStruct((t, d), F32)],
        compiler_params=_params(),
    )(y, target)


def _ln_epi(acc, res, g, b):
    a = ALPHA * res + acc
    y = _layer_norm(a, g, b)
    return a, y, y


def _ln_fn(a, g, b):
    return (_layer_norm(a, g, b),)


def _ln_bwd_epi(scale):
    def epi(acc, res, a, g, b):
        _, vjp = jax.vjp(_ln_fn, a, g, b)
        da, dg, db = vjp((acc + scale * res,))
        return da, da, dg, db
    return epi


def _relu_sq(h):
    r = jnp.maximum(h, 0)
    return r * r


def _pad_cols(w, n):
    return jnp.pad(w, ((0, 0), (0, n - w.shape[1])))


def _pad_rows(w, n):
    return jnp.pad(w, ((0, n - w.shape[0]), (0, 0)))


def _step(x, p, positions, target, small, comm):
    t = x.shape[0]
    w = small
    freqs = ROPE_BASE ** (-jnp.arange(0, MLA_ROPE // 2, dtype=F32) * (2.0 / MLA_ROPE))
    freq_row = jnp.concatenate([freqs, freqs, jnp.zeros((LANES - MLA_ROPE,), F32)])[None, :]
    cos, sin = _rope_tables(positions.reshape(t, 1), freq_row)

    saved = []
    xb = x.astype(BF16)
    for i in range(DEPTH):
        j, kind = i // 3, i % 3
        wl = comm.mixer_weights(i)
        s = {"x": xb, "wl": wl}
        tok = comm.at("fwd", i, "begin", x)
        if kind == 0:
            s["w_main"] = wl["gla_w_in_t"][:GLA_MAIN]
            s["w_lr"] = _pad_rows(wl["gla_w_in_t"][GLA_MAIN:], LANES)
            s["w_up"] = _pad_rows(w["gla_w_gate_up"][j], LANES).astype(BF16)
            s["proj"] = _matmul(xb, s["w_main"], name="gla_proj", tb=True, tn=1024, after=tok)
            s["glr"] = _matmul(xb, s["w_lr"], name="gla_lr", tb=True, out_dtypes=(F32,))
            s["z"] = _matmul(s["glr"], s["w_up"], name="gla_gate", epi=lambda acc, b: (acc + b,),
                             epi_ins=(w["gla_b_gate"][j][None, :],), out_dtypes=(F32,))
            tok = comm.at("fwd", i, "proj_done", s["z"]) or []
            s["u"], s["states"] = _gla_fwd(s["proj"], s["z"], w["gla_norm_g"][j][None, :], tok)
        elif kind == 1:
            s["cq"] = _matmul(xb, wl["mla_in"], name="mla_proj", tn=MLA_IN_PAD, b_at=REG_MLA_IN, out_dtypes=(F32,),
                              after=tok)
            s["pre_params"] = (w["mla_q_norm"][j][None, :], w["mla_kv_norm"][j][None, :], wl["mla_w_uq"], wl["mla_w_ukv"])
            s["q"], s["kv"], s["kr"] = _tile_fwd(_mla_pre, (s["cq"], cos, sin), s["pre_params"], (BF16, BF16, BF16),
                                                 tm=256, name="mla_pre_fwd")
            tok = comm.at("fwd", i, "proj_done", s["kv"]) or []
            s["u"] = _mla_attn_fwd(s["q"], s["kv"], s["kr"], tok)
        else:
            s["bcu"] = _matmul(xb, wl["conv"], name="conv_proj", tb=True, tm=256, tn=3 * D_MODEL, b_at=REG_CONV,
                               out_dtypes=(F32,), after=tok)
            tok = comm.at("fwd", i, "proj_done", s["bcu"]) or []
            s["u"] = _conv_fwd(s["bcu"], w["conv_w"][j], tok)
        g0, b0 = w["ln_g"][i, 0][None, :], w["ln_b"][i, 0][None, :]
        g1, b1 = w["ln_g"][i, 1][None, :], w["ln_b"][i, 1][None, :]
        wa, wb = s["wa"], _ = comm.slab_weights(i, s["u"])
        s["a1"], s["x1"], s["x1b"] = _matmul(s["u"], wa, name="mixer_out_ln", tm=256, tn=D_MODEL, b_at=REG_WOUT,
                                             epi=_ln_epi, epi_ins=(x, g0, b0), out_dtypes=(F32, F32, BF16))
        s["hh"] = _matmul(s["x1b"], wa, name="mlp_up", tb=True, tm=256, tn=D_FF, b_at=REG_W1T)
        tok = comm.at("fwd", i, "mid", s["hh"])
        s["a2"], s["x2"], s["x2b"] = _matmul(s["hh"], wa, name="mlp_down_ln", tm=256, tn=D_MODEL, b_at=REG_W2,
                                             a_fn=_relu_sq, epi=_ln_epi, epi_ins=(s["x1"], g1, b1),
                                             out_dtypes=(F32, F32, BF16), after=tok)
        s["pp"] = _matmul(p[i], wb, name="ple_proj", tb=True, tn=D_MODEL, b_at=REG_WPT)
        tok = comm.at("fwd", i, "end", s["pp"])
        def ple_epi(acc, xr, pp):
            y = xr + jax.nn.sigmoid(acc) * pp.astype(F32)
            return y, y, acc

        x, xb, s["gt"] = _matmul(s["x2b"], wa, name="ple_gate", tn=1024, b_at=REG_WG, epi=ple_epi,
                                 epi_ins=(s["x2"], s["pp"]), out_dtypes=(F32, BF16, BF16), after=tok)
        saved.append(s)

    loss_part, dx = _loss_head(x, target)

    gw = {n: [None] * WEIGHTS[n][0][0] for n in SMALL + REPLICATED}
    ln_g_grads, ln_b_grads = [[None, None] for _ in range(DEPTH)], [[None, None] for _ in range(DEPTH)]
    resid = lambda acc, r: (acc + ALPHA * r,)
    plus = lambda acc, r: (acc + r,)
    for i in reversed(range(DEPTH)):
        j, kind = i // 3, i % 3
        s = saved[i]
        wa = s["wa"]
        ga = lax.empty((N_DEV, A_ROWS, D_MODEL), BF16)
        gb = lax.empty((N_DEV, REG_WPT[1], PLE_DIM), BF16)
        layer_grads = {}
        tok = comm.at("bwd", i, "begin", dx)

        def ple_bwd(dxo, gt, pp):
            sg = jax.nn.sigmoid(gt.astype(F32))
            d_gt = dxo * pp.astype(F32) * sg * (1.0 - sg)
            return d_gt, d_gt, dxo * sg

        g1, b1 = w["ln_g"][i, 1][None, :], w["ln_b"][i, 1][None, :]
        d_a2, d_a2b, d_gt, d_pp, ln_g_grads[i][1], ln_b_grads[i][1] = _matmul(
            dx, wa, name="ple_gate_dx_ln", tb=True, tm=256, tn=D_MODEL, b_at=REG_WG, a_fn=ple_bwd,
            a_ins=(s["gt"], s["pp"]), a_out_dtypes=(BF16, BF16), epi=_ln_bwd_epi(1.0), epi_ins=(dx, s["a2"], g1, b1),
            out_dtypes=(F32, BF16), n_row_sums=2, after=tok)
        gb = _matmul(d_pp, p[i], name="ple_proj_dw", ta=True, tm=512, tn=PLE_DIM, out_at=REG_WPT, out_buf=gb)
        ga = _matmul(s["x2b"], d_gt, name="ple_gate_dw", ta=True, tm=512, tn=1024, out_at=REG_WG, out_buf=ga)
        tok = comm.at("bwd", i, "ln", d_a2)
        ga = _matmul(s["hh"], d_a2b, name="mlp_down_dw", ta=True, tm=REG_W2[1], tn=1024, a_fn=_relu_sq, out_at=REG_W2,
                     out_buf=ga, after=tok)
        d_hh = _matmul(d_a2b, wa, name="mlp_down_dx", tb=True, tm=256, tn=D_FF, b_at=REG_W2, after=[ga, gb],
                       epi=lambda acc, hh: (acc * 2.0 * jnp.maximum(hh.astype(F32), 0.0),), epi_ins=(s["hh"],))
        ga = _matmul(d_hh, s["x1b"], name="mlp_up_dw", ta=True, tm=REG_W1T[1], tn=1024, out_at=REG_W1T, out_buf=ga)
        g0, b0 = w["ln_g"][i, 0][None, :], w["ln_b"][i, 0][None, :]
        d_a1, d_a1b, ln_g_grads[i][0], ln_b_grads[i][0] = _matmul(
            d_hh, wa, name="mlp_up_dx_ln", tm=256, tn=D_MODEL, b_at=REG_W1T, epi=_ln_bwd_epi(ALPHA),
            epi_ins=(d_a2, s["a1"], g0, b0), out_dtypes=(F32, BF16), n_row_sums=2, after=ga)
        ga = _matmul(s["u"], d_a1b, name="mixer_out_dw", ta=True, tm=512, tn=1024, out_at=REG_WOUT, out_buf=ga)
        du = _matmul(d_a1b, wa, name="mixer_out_dx", tb=True, tn=1024, b_at=REG_WOUT, out_dtypes=(F32,), after=ga)
        comm.slab_grads(i, ga, gb)
        tok = comm.at("bwd", i, "slab_done", du) or []
        if kind == 0:
            dproj, dz, dg, dz_sum = _gla_bwd(s["proj"], s["z"], w["gla_norm_g"][j][None, :], s["states"], du, tok)
            tok = comm.at("bwd", i, "mixer_done", dproj)
            gw["gla_norm_g"][j] = dg[0]
            gw["gla_b_gate"][j] = dz_sum[0]
            gw["gla_w_gate_up"][j] = _matmul(s["glr"], dz, name="gla_gate_dw", ta=True, out_dtypes=(F32,),
                                             after=tok)[:GLA_RANK]
            dglr = _matmul(dz, s["w_up"], name="gla_gate_dx", tb=True, out_dtypes=(F32,))
            dw_main = _matmul(dproj, s["x"], name="gla_proj_dw", ta=True, tn=1024, out_dtypes=(F32,))
            dw_lr = _matmul(dglr, s["x"], name="gla_lr_dw", ta=True, tn=1024, out_dtypes=(F32,))[:GLA_RANK]
            layer_grads["gla_w_in_t"] = jnp.concatenate([dw_main, dw_lr], axis=0)
            dx = _matmul(dproj, s["w_main"], name="gla_proj_dx", tn=1024, epi=resid, epi_ins=(d_a1,),
                         out_dtypes=(F32,), after=[dw_main, dw_lr, gw["gla_w_gate_up"][j]])
            dx = _matmul(dglr, s["w_lr"], name="gla_lr_dx", tn=1024, epi=plus, epi_ins=(dx,), out_dtypes=(F32,))
        elif kind == 1:
            dq, dkv, dkr = _mla_attn_bwd(s["q"], s["kv"], s["kr"], du, tok)
            tok = comm.at("bwd", i, "mixer_done", dq)
            (d_cq,), (dgq, dgkv, layer_grads["mla_uq"], layer_grads["mla_ukv"]) = _tile_bwd(
                _mla_pre, (s["cq"], cos, sin), s["pre_params"], (dq, dkv, dkr), (BF16,), tm=256, name="mla_pre_bwd",
                diff_tiled=[0])
            gw["mla_q_norm"][j], gw["mla_kv_norm"][j] = dgq[0], dgkv[0]
            layer_grads["mla_in"] = _matmul(s["x"], d_cq, name="mla_proj_dw", ta=True, tm=512, tn=MLA_IN_PAD,
                                            out_at=REG_MLA_IN, after=tok,
                                            out_buf=lax.empty((N_DEV, REG_MLA_IN[1], MLA_IN_PAD), BF16))
            dx = _matmul(d_cq, s["wl"]["mla_in"], name="mla_proj_dx", tb=True, tn=1024, b_at=REG_MLA_IN, epi=resid,
                         epi_ins=(d_a1,), out_dtypes=(F32,), after=layer_grads["mla_in"])
        else:
            db, dc, du_, dcw = _conv_bwd(s["bcu"], w["conv_w"][j], du, tok)
            tok = comm.at("bwd", i, "mixer_done", db)
            gw["conv_w"][j] = dcw
            dbcu = jnp.concatenate([db, dc, du_], axis=1)
            layer_grads["conv"] = _matmul(dbcu, s["x"], name="conv_proj_dw", ta=True, tm=REG_CONV[1], tn=1024,
                                          out_at=REG_CONV, out_buf=lax.empty((N_DEV, REG_CONV[1], D_MODEL), BF16),
                                          after=tok)
            dx = _matmul(dbcu, s["wl"]["conv"], name="conv_proj_dx", tn=1024, b_at=REG_CONV, epi=resid, epi_ins=(d_a1,),
                         out_dtypes=(F32,), after=layer_grads["conv"])
        comm.mixer_grads(i, layer_grads)

    gw["ln_g"] = [jnp.concatenate([a, b], axis=0) for a, b in ln_g_grads]
    gw["ln_b"] = [jnp.concatenate([a, b], axis=0) for a, b in ln_b_grads]
    return loss_part, dx, {n: jnp.stack(gw[n]).astype(F32) for n in gw}


MESH_IDS = pl.DeviceIdType.MESH
ANY = pl.BlockSpec(memory_space=pl.ANY)
HBM_SPEC = pl.BlockSpec(memory_space=pltpu.HBM)
SEM_SPEC = pl.BlockSpec(memory_space=pltpu.SEMAPHORE)
DATAFLOW_EFFECT = pltpu.SideEffectType.DATAFLOW_SIDE_EFFECTING
CORE_COPIES, CHIP_COPIES = 4, 3


def _my_place():
    return lax.axis_index("x"), lax.axis_index("y"), lax.axis_index("c")


def _other_chips(mx, my):
    return [(1 - mx, my), (mx, 1 - my), (1 - mx, 1 - my)]


def _remote(src, dst, send_sems, recv_sems, k, to):
    return pltpu.make_async_remote_copy(src_ref=src, dst_ref=dst, send_sem=send_sems.at[k], recv_sem=recv_sems.at[k],
                                        device_id=to, device_id_type=MESH_IDS)


def _gather_first_copies(n_arr):
    def make(bufs, send_sems, recv_sems):
        mx, my, mc = _my_place()
        mine = 4 * mx + 2 * my + mc
        peers = [(mx, my, 1 - mc)] + [(cx, cy, mc) for cx, cy in _other_chips(mx, my)]
        return [_remote(bufs[a].at[mine], bufs[a].at[mine], send_sems, recv_sems, (1 + CHIP_COPIES) * a + k, to)
                for a in range(n_arr) for k, to in enumerate(peers)]
    return make, (1 + CHIP_COPIES) * n_arr


def _gather_forward_copies(n_arr):
    def make(bufs, send_sems, recv_sems):
        mx, my, mc = _my_place()
        blocks = [4 * cx + 2 * cy + mc for cx, cy in _other_chips(mx, my)]
        return [_remote(bufs[a].at[blk], bufs[a].at[blk], send_sems, recv_sems, CHIP_COPIES * a + k, (mx, my, 1 - mc))
                for a in range(n_arr) for k, blk in enumerate(blocks)]
    return make, CHIP_COPIES * n_arr


def _scatter_core_copies(n_arr):
    def make(bufs, send_sems, recv_sems):
        mx, my, mc = _my_place()
        return [_remote(bufs[a].at[2 * k + (1 - mc)], bufs[n_arr + a].at[k], send_sems, recv_sems, CORE_COPIES * a + k,
                        (mx, my, 1 - mc)) for a in range(n_arr) for k in range(CORE_COPIES)]
    return make, CORE_COPIES * n_arr


def _scatter_chip_copies(n_arr):
    def make(bufs, send_sems, recv_sems):
        mx, my, mc = _my_place()
        return [_remote(bufs[a].at[2 * cx + cy], bufs[n_arr + a].at[k], send_sems, recv_sems, CHIP_COPIES * a + k,
                        (cx, cy, mc)) for a in range(n_arr) for k, (cx, cy) in enumerate(_other_chips(mx, my))]
    return make, CHIP_COPIES * n_arr


def _exchange(name, bufs, copies):
    make, n_copies = copies
    n = len(bufs)

    def body(*refs):
        descs = make(refs[:n], refs[2 * n], refs[2 * n + 1])
        for cp in descs:
            cp.start()
        for cp in descs:
            cp.wait()

    return pl.pallas_call(
        body, name=name, out_shape=[jax.ShapeDtypeStruct(b.shape, b.dtype) for b in bufs], in_specs=[ANY] * n,
        out_specs=[ANY] * n, input_output_aliases={i: i for i in range(n)},
        scratch_shapes=[pltpu.SemaphoreType.DMA((n_copies,)), pltpu.SemaphoreType.DMA((n_copies,))],
    )(*bufs)


def _exchange_start(name, bufs, copies, after):
    make, n_copies = copies
    n = len(bufs)

    def body(*refs):
        for cp in make(refs[:n], refs[n + 1], refs[n + 2]):
            cp.start()
        refs[-1][...] = jnp.zeros_like(refs[-1])

    outs = pl.pallas_call(
        body, name=name,
        out_shape=(pltpu.SemaphoreType.DMA((n_copies,)), pltpu.SemaphoreType.DMA((n_copies,)),
                   *[pltpu.HBM(b.shape, b.dtype) for b in bufs], jax.ShapeDtypeStruct((8, LANES), F32)),
        in_specs=[HBM_SPEC] * n + [ANY],
        out_specs=(SEM_SPEC, SEM_SPEC, *[HBM_SPEC] * n, pl.BlockSpec(memory_space=pltpu.VMEM)),
        input_output_aliases={i: 2 + i for i in range(n)},
        compiler_params=pltpu.CompilerParams(has_side_effects=DATAFLOW_EFFECT),
    )(*[pltpu.with_memory_space_constraint(b, pltpu.HBM) for b in bufs], after)
    return (outs[0], outs[1]), list(outs[2:2 + n]), outs[-1]


def _exchange_wait(name, sems, bufs, copies, after):
    make, _ = copies
    n = len(bufs)

    def body(*refs):
        for cp in make(refs[:n], refs[n], refs[n + 1]):
            cp.wait_send()
            cp.wait_recv()

    return list(pl.pallas_call(
        body, name=name, out_shape=[pltpu.HBM(b.shape, b.dtype) for b in bufs],
        in_specs=[HBM_SPEC] * n + [SEM_SPEC, SEM_SPEC, ANY], out_specs=[HBM_SPEC] * n,
        input_output_aliases={i: i for i in range(n)},
        compiler_params=pltpu.CompilerParams(has_side_effects=DATAFLOW_EFFECT),
    )(*bufs, *sems, after))


SUM_TILE_BYTES = 4 * 1024 * 1024


def _row_tile(r, c):
    best = None
    for cand in range(16, r + 1, 16):
        if r % cand == 0 and cand * c * 2 <= SUM_TILE_BYTES:
            best = cand
    return r if best is None else best


def _pair_sum(g, recv, my_c):
    _, r, c = g.shape
    tr = _row_tile(r, c)

    def body(c_ref, g_ref, r_ref, o_ref):
        o_ref[...] = (g_ref[...].astype(F32) + r_ref[...].astype(F32)).astype(o_ref.dtype)

    return pl.pallas_call(
        body, name="rs_pair_sum", out_shape=jax.ShapeDtypeStruct((4, r, c), g.dtype),
        grid_spec=pltpu.PrefetchScalarGridSpec(
            num_scalar_prefetch=1, grid=(4, r // tr),
            in_specs=[pl.BlockSpec((1, tr, c), lambda n, i, cr: (2 * n + cr[0], i, 0)),
                      pl.BlockSpec((1, tr, c), lambda n, i, cr: (n, i, 0))],
            out_specs=pl.BlockSpec((1, tr, c), lambda n, i, cr: (n, i, 0))),
        compiler_params=_params(),
    )(my_c, g, recv)


def _chip_sum(h, recv, my_chip):
    _, r, c = h.shape
    tr = _row_tile(r, c)

    def body(j_ref, h_ref, r0_ref, r1_ref, r2_ref, o_ref):
        o_ref[...] = ((h_ref[0].astype(F32) + r0_ref[0].astype(F32)) + r1_ref[0].astype(F32)) + r2_ref[0].astype(F32)

    return pl.pallas_call(
        body, name="rs_chip_sum", out_shape=jax.ShapeDtypeStruct((r, c), F32),
        grid_spec=pltpu.PrefetchScalarGridSpec(
            num_scalar_prefetch=1, grid=(r // tr,),
            in_specs=[pl.BlockSpec((1, tr, c), lambda i, jr: (jr[0], i, 0))]
            + [pl.BlockSpec((1, tr, c), lambda i, jr, n=n: (n, i, 0)) for n in range(3)],
            out_specs=pl.BlockSpec((tr, c), lambda i, jr: (i, 0))),
        compiler_params=_params(),
    )(my_chip, h, recv, recv, recv)


def _sum_blocks(g):
    n, r, c = g.shape

    def body(g_ref, o_ref):
        acc = g_ref[0]
        for k in range(1, n):
            acc = acc + g_ref[k]
        o_ref[...] = acc

    return pl.pallas_call(body, name="sum_blocks", out_shape=jax.ShapeDtypeStruct((r, c), F32), compiler_params=_params())(g)


def _pack(flat_parts, cols, row_multiple, dtype):
    flat = jnp.concatenate([f.astype(dtype) for f in flat_parts])
    per_row_block = cols * row_multiple
    padded = -(-flat.shape[0] // per_row_block) * per_row_block
    return jnp.pad(flat, (0, padded - flat.shape[0])).reshape(padded // cols, cols)


def _shard_shape(name):
    shape, axis = WEIGHTS[name]
    if axis is None:
        return shape
    return tuple(s // N_DEV if a == axis else s for a, s in enumerate(shape))


def _size(shape):
    n = 1
    for s in shape:
        n *= s
    return n


def _unshard(blocks, name):
    _, axis = WEIGHTS[name]
    return jnp.concatenate([blocks[k] for k in range(N_DEV)], axis=axis)


def _unpack_blocks(flat, names):
    out, off = {}, 0
    for n in names:
        shp = _shard_shape(n)
        out[n] = flat[..., off:off + _size(shp)].reshape(flat.shape[:-1] + shp)
        off += _size(shp)
    return out


def _layer_slabs(shard, i):
    j, kind = i // 3, i % 3
    w_out = (shard["gla_w_out"], shard["mla_w_out"], shard["conv_w_out"])[kind][j]
    out = {"a": jnp.concatenate([shard["mlp_w2"][i], shard["mlp_w1"][i].T, w_out, shard["ple_w_gate"][i]], axis=0).astype(BF16),
           "b": shard["ple_w_proj"][i].T.astype(BF16)}
    if kind == 0:
        out["gla"] = shard["gla_w_in"][j].T.astype(BF16)
    elif kind == 1:
        out["mla_in"] = _pad_cols(shard["mla_w_in"][j], MLA_IN_PAD).astype(BF16)
        out["mla_uq"] = _pad_cols(shard["mla_w_uq"][j], MLA_HEAD_PAD).astype(BF16)
        out["mla_ukv"] = shard["mla_w_ukv"][j].astype(BF16)
    else:
        out["conv"] = shard["conv_w_in"][j].T.astype(BF16)
    return out


def _mixer_weights(landed, i):
    kind = i % 3
    if kind == 0:
        return {"gla_w_in_t": landed["gla"].reshape(-1, D_MODEL)}
    if kind == 2:
        return {"conv": landed["conv"]}
    heads_side_by_side = lambda g: g.transpose(1, 0, 2).reshape(g.shape[1], -1)
    return {"mla_in": landed["mla_in"], "mla_w_uq": heads_side_by_side(landed["mla_uq"]),
            "mla_w_ukv": heads_side_by_side(landed["mla_ukv"])}


def _mixer_grad_buffers(layer_grads, i):
    kind = i % 3
    if kind == 0:
        return {"gla": layer_grads["gla_w_in_t"].reshape(N_DEV, -1, D_MODEL).astype(BF16)}
    if kind == 2:
        return {"conv": layer_grads["conv"]}
    head_blocks = lambda g: g.reshape(g.shape[0], N_DEV, -1).transpose(1, 0, 2).astype(BF16)
    return {"mla_in": layer_grads["mla_in"], "mla_uq": head_blocks(layer_grads["mla_uq"]),
            "mla_ukv": head_blocks(layer_grads["mla_ukv"])}


SLAB_KEYS = ("a", "b")


class _Overlap:
    def __init__(self, shard, small_pack):
        mx, my, mc = _my_place()
        self.my_c = mc.astype(jnp.int32).reshape(1)
        self.my_chip = (2 * mx + my).astype(jnp.int32).reshape(1)
        mine = 4 * mx + 2 * my + mc
        def landing_of(slabs):
            return {k: lax.dynamic_update_index_in_dim(lax.empty((N_DEV, *v.shape), v.dtype), v, mine, 0)
                    for k, v in slabs.items()}

        first = _layer_slabs(shard, 0)
        first["small"] = small_pack
        self.landing = [landing_of(first)]
        self.fly = {}
        self.grads = [{} for _ in range(DEPTH)]
        self.reduced = [{} for _ in range(DEPTH)]
        tok = self._gather_first(0, "mixer", shard["ln_g"])
        tok = self._gather_first(0, "slab", tok)
        shard, tok = lax.optimization_barrier((shard, tok))
        self.landing += [landing_of(_layer_slabs(shard, i)) for i in range(1, DEPTH)]
        bufs = self._wait("ag_first_mixer_l0", tok)
        self.landing[0].update(zip(self._keys(self.landing[0], "mixer"),
                                   _exchange("ag_forward_mixer_l0", bufs, _gather_forward_copies(len(bufs)))))

    @staticmethod
    def _keys(names, group):
        return [k for k in names if (k in SLAB_KEYS) == (group == "slab")]

    def _start(self, name, bufs, copies, after):
        sems, bufs, tok = _exchange_start(name + "_start", bufs, copies, after)
        self.fly[name] = (sems, bufs, copies)
        return tok

    def _wait(self, name, after):
        sems, bufs, copies = self.fly.pop(name)
        return _exchange_wait(name + "_wait", sems, bufs, copies, after)

    def mixer_weights(self, i):
        return _mixer_weights(self.landing[i], i)

    def slab_weights(self, i, dep):
        self._gather_done(i, "slab", dep)
        return self.landing[i]["a"], self.landing[i]["b"]

    def slab_grads(self, i, ga, gb):
        self.grads[i].update(a=ga, b=gb)

    def mixer_grads(self, i, layer_grads):
        self.grads[i].update(_mixer_grad_buffers(layer_grads, i))

    def at(self, phase, i, point, dep):
        toks = []
        if phase == "fwd":
            if point == "begin" and i == 0:
                toks.append(self._gather_first(1, "mixer", self.landing[0][self._keys(self.landing[0], "mixer")[0]]))
                toks.append(self._gather_first(1, "slab", toks[-1]))
            if point == "proj_done":
                toks.append(self._gather_forward(i, "slab", dep))
            if point == "mid" and i + 1 < DEPTH:
                toks.append(self._gather_forward(i + 1, "mixer", dep))
                if i + 2 < DEPTH:
                    toks.append(self._gather_first(i + 2, "mixer", dep))
                    toks.append(self._gather_first(i + 2, "slab", toks[-1]))
            if point == "end" and i + 1 < DEPTH:
                self._gather_done(i + 1, "mixer", dep)
        else:
            if point == "begin" and i + 1 < DEPTH:
                toks.append(self._scatter_cores(i + 1, "mixer", dep))
            if point == "ln" and i + 1 < DEPTH:
                toks.append(self._scatter_chips(i + 1, "mixer", dep))
            if point == "slab_done":
                if i + 1 < DEPTH:
                    self._scatter_done(i + 1, "slab", dep)
                    self._scatter_done(i + 1, "mixer", dep)
                toks.append(self._scatter_cores(i, "slab", dep))
            if point == "mixer_done":
                toks.append(self._scatter_chips(i, "slab", dep))
        return toks or None

    def _gather_first(self, i, group, after):
        bufs = [self.landing[i][k] for k in self._keys(self.landing[i], group)]
        return self._start(f"ag_first_{group}_l{i}", bufs, _gather_first_copies(len(bufs)), after)

    def _gather_forward(self, i, group, after):
        bufs = self._wait(f"ag_first_{group}_l{i}", after)
        return self._start(f"ag_forward_{group}_l{i}", bufs, _gather_forward_copies(len(bufs)), after)

    def _gather_done(self, i, group, after):
        keys = self._keys(self.landing[i], group)
        self.landing[i].update(zip(keys, self._wait(f"ag_forward_{group}_l{i}", after)))

    def _scatter_cores(self, i, group, after):
        gs = [self.grads[i][k] for k in self._keys(self.grads[i], group)]
        land = [lax.empty((4, *g.shape[1:]), g.dtype) for g in gs]
        return self._start(f"rs_cores_{group}_l{i}", gs + land, _scatter_core_copies(len(gs)), after)

    def _pair_sums(self, bufs):
        n = len(bufs) // 2
        hs = [_pair_sum(g, r, self.my_c) for g, r in zip(bufs[:n], bufs[n:])]
        return hs + [lax.empty((3, *h.shape[1:]), h.dtype) for h in hs]

    def _scatter_chips(self, i, group, after):
        bufs = self._pair_sums(self._wait(f"rs_cores_{group}_l{i}", after))
        return self._start(f"rs_chips_{group}_l{i}", bufs, _scatter_chip_copies(len(bufs) // 2), after)

    def _chip_sums(self, i, group, bufs):
        n = len(bufs) // 2
        for k, h, r in zip(self._keys(self.grads[i], group), bufs[:n], bufs[n:]):
            self.reduced[i][k] = _chip_sum(h, r, self.my_chip)

    def _scatter_done(self, i, group, after):
        self._chip_sums(i, group, self._wait(f"rs_chips_{group}_l{i}", after))

    def tail_begin(self, dep):
        return self._scatter_cores(0, "mixer", dep)

    def tail_middle(self, dep):
        self._scatter_done(0, "slab", dep)
        return self._scatter_chips(0, "mixer", dep)

    def tail_end(self, dep):
        self._scatter_done(0, "mixer", dep)


def _small_gather_start(x, name, after):
    mx, my, mc = _my_place()
    land = lax.dynamic_update_index_in_dim(lax.empty((N_DEV, *x.shape), x.dtype), x, 4 * mx + 2 * my + mc, 0)
    return name, _exchange_start(name + "_first_start", [land], _gather_first_copies(1), after)


def _small_gather_finish(started, after):
    name, (sems, bufs, _) = started
    bufs = _exchange_wait(name + "_first_wait", sems, bufs, _gather_first_copies(1), after)
    return _exchange(name + "_forward", bufs, _gather_forward_copies(1))[0]


def _adamw_math(w, g, m, v):
    m2 = ADAM_B1 * m + (1.0 - ADAM_B1) * g
    v2 = ADAM_B2 * v + (1.0 - ADAM_B2) * (g * g)
    m_hat = m2 / (1.0 - ADAM_B1 ** ADAM_STEP)
    v_hat = v2 / (1.0 - ADAM_B2 ** ADAM_STEP)
    return -ADAM_LR * (m_hat / (jnp.sqrt(v_hat) + ADAM_EPS) + ADAM_WD * w), m2, v2


ADAMW_TILE_BYTES = 1024 * 1024


def _adamw_layer(name, w, m, v, j, g, g_at, transposed, chain, after):
    n_layers, r, c = w.shape
    tr = max(t for t in range(8, r + 1, 8) if r % t == 0 and (t * c * 4 <= ADAMW_TILE_BYTES or t == 8))
    rb, rows = g_at
    if transposed:
        assert rows == c and g.shape[1] == r, (name, g.shape, g_at)
        g_spec = pl.BlockSpec((rows, tr), lambda i: (rb, i))
    else:
        assert rows == r and g.shape[1] == c, (name, g.shape, g_at)
        g_spec = pl.BlockSpec((tr, c), lambda i: (rb * (r // tr) + i, 0))
    extra = list(chain or []) + [a for a in (after or []) if a is not None]
    n_chain = 4 if chain else 0

    def body(w_ref, m_ref, v_ref, g_ref, *rest):
        g_out, d_out, m_out, v_out, tok_ref = rest[len(extra):]
        gv = g_ref[...].T if transposed else g_ref[...]
        g_out[0] = gv
        d_out[0], m_out[0], v_out[0] = _adamw_math(w_ref[0], gv, m_ref[0], v_ref[0])
        tok_ref[...] = jnp.zeros_like(tok_ref)

    layer_spec = pl.BlockSpec((1, tr, c), lambda i: (j, i, 0))
    outs = pl.pallas_call(
        body, name=f"adamw_{name}_l{j}", grid=(r // tr,),
        in_specs=[layer_spec] * 3 + [g_spec] + [pl.BlockSpec(memory_space=pl.ANY)] * len(extra),
        out_specs=[layer_spec] * 4 + [pl.BlockSpec((8, LANES), lambda i: (0, 0))],
        out_shape=[jax.ShapeDtypeStruct(w.shape, F32)] * 4 + [jax.ShapeDtypeStruct((8, LANES), F32)],
        input_output_aliases={4 + k: k for k in range(n_chain)}, compiler_params=_params(),
    )(w, m, v, g, *extra)
    return list(outs[:4]), outs[4]


def _adamw(w, g, m, v, name):
    shape = w.shape
    cols = shape[-1]
    rows = _size(shape) // cols
    tr = rows
    for cand in (512, 256, 128, 64, 32, 16, 8):
        if rows > cand and rows % cand == 0:
            tr = cand
            break

    def body(w_ref, g_ref, m_ref, v_ref, d_ref, mo_ref, vo_ref):
        d_ref[...], mo_ref[...], vo_ref[...] = _adamw_math(w_ref[...], g_ref[...], m_ref[...], v_ref[...])

    spec = pl.BlockSpec((tr, cols), lambda i: (i, 0))
    outs = pl.pallas_call(
        body, name="adamw_" + name, grid=(rows // tr,), in_specs=[spec] * 4, out_specs=[spec] * 3,
        out_shape=[jax.ShapeDtypeStruct((rows, cols), F32)] * 3, compiler_params=_params(),
    )(*[a.reshape(rows, cols) for a in (w, g, m, v)])
    return [o.reshape(shape) for o in outs]


def kernel(x, p, positions, gla_w_in, gla_w_gate_up, gla_b_gate, gla_norm_g, gla_w_out, mla_w_in, mla_q_norm, mla_kv_norm, mla_w_uq, mla_w_ukv, mla_w_out, conv_w_in, conv_w, conv_w_out, ln_g, ln_b, mlp_w1, mlp_w2, ple_w_gate, ple_w_proj, loss_target, m_gla_w_in, m_gla_w_gate_up, m_gla_b_gate, m_gla_norm_g, m_gla_w_out, m_mla_w_in, m_mla_q_norm, m_mla_kv_norm, m_mla_w_uq, m_mla_w_ukv, m_mla_w_out, m_conv_w_in, m_conv_w, m_conv_w_out, m_ln_g, m_ln_b, m_mlp_w1, m_mlp_w2, m_ple_w_gate, m_ple_w_proj, v_gla_w_in, v_gla_w_gate_up, v_gla_b_gate, v_gla_norm_g, v_gla_w_out, v_mla_w_in, v_mla_q_norm, v_mla_kv_norm, v_mla_w_uq, v_mla_w_ukv, v_mla_w_out, v_conv_w_in, v_conv_w, v_conv_w_out, v_ln_g, v_ln_b, v_mlp_w1, v_mlp_w2, v_ple_w_gate, v_ple_w_proj):
    args = locals()
    shard = {n: args[n] for n in WEIGHT_NAMES}
    mom = {n: args["m_" + n] for n in WEIGHT_NAMES}
    var = {n: args["v_" + n] for n in WEIGHT_NAMES}
    mx, my, mc = _my_place()

    comm = _Overlap(shard, _pack([shard[n].reshape(-1) for n in SMALL], LANES, 8, F32))
    small_all = comm.landing[0]["small"]
    small = {n: shard[n] for n in REPLICATED}
    small.update({n: _unshard(blk, n) for n, blk in _unpack_blocks(small_all.reshape(N_DEV, -1), SMALL).items()})
    loss_part, grad_x, small_grads = _step(x[0], p[:, 0], positions[0], loss_target[0], small, comm)

    chains = {}

    def update(name, j, g, g_at, transposed, tok):
        chains[name], tok = _adamw_layer(name, shard[name], mom[name], var[name], j, g, g_at, transposed,
                                         chains.get(name), [tok])
        return tok

    def update_layer(i, groups, tok):
        j, kind = i // 3, i % 3
        red = comm.reduced[i]
        if "slab" in groups:
            tok = update("mlp_w2", i, red["a"], REG_W2, False, tok)
            tok = update("mlp_w1", i, red["a"], REG_W1T, True, tok)
            tok = update(("gla_w_out", "mla_w_out", "conv_w_out")[kind], j, red["a"], REG_WOUT, False, tok)
            tok = update("ple_w_gate", i, red["a"], REG_WG, False, tok)
            tok = update("ple_w_proj", i, red["b"], REG_WPT, True, tok)
        if "mixer" in groups:
            if kind == 0:
                tok = update("gla_w_in", j, red["gla"].T, (0, D_MODEL), False, tok)
            elif kind == 2:
                tok = update("conv_w_in", j, red["conv"], REG_CONV, True, tok)
            else:
                for n, g in (("mla_w_in", red["mla_in"][:, :MLA_IN]), ("mla_w_ukv", red["mla_ukv"]),
                             ("mla_w_uq", red["mla_uq"][:, :MLA_NOPE + MLA_ROPE])):
                    tok = update(n, j, g, (0, g.shape[0]), False, tok)
        return tok

    tok = comm.tail_begin(grad_x)
    tok = update_layer(3, ("slab", "mixer"), tok)
    tok = update_layer(2, ("slab", "mixer"), tok)
    tok = comm.tail_middle(tok)
    small_parts = [loss_part[0, :1]] + [small_grads[n].reshape(-1) for n in SMALL + REPLICATED]
    small_gather = _small_gather_start(_pack(small_parts, LANES, 8, F32), "ag_small_grads", tok)
    tok = update_layer(1, ("slab", "mixer"), small_gather[1][2])
    tok = update_layer(0, ("slab",), tok)
    comm.tail_end(tok)
    tok = update_layer(0, ("mixer",), tok)
    red_small = _sum_blocks(_small_gather_finish(small_gather, tok)).reshape(-1)
    loss = red_small[0]
    off = 1
    dev = 4 * mx + 2 * my + mc
    for n in SMALL + REPLICATED:
        shape, axis = WEIGHTS[n]
        full_g = red_small[off:off + _size(shape)].reshape(shape)
        off += _size(shape)
        if axis is not None:
            width = shape[axis] // N_DEV
            full_g = lax.dynamic_slice_in_dim(full_g, dev * width, width, axis=axis)
        chains[n] = [full_g, *_adamw(shard[n], full_g, mom[n], var[n], n)]
    return (loss, grad_x[None], *[chains[n][k] for k in range(4) for n in WEIGHT_NAMES])
```

```python
import functools

import jax
import jax.numpy as jnp
from jax import lax
from jax.experimental import pallas as pl
from jax.experimental.pallas import tpu as pltpu

F32, BF16 = jnp.float32, jnp.bfloat16
N_DEV = 8

D_MODEL = 1024
DEPTH = 4
CHUNK = 64
ALPHA = (2 * DEPTH) ** 0.25
LN_EPS = 1e-5
RMS_EPS = 1e-6
PLE_DIM = 256
D_FF = 4 * D_MODEL
GLA_HEADS = 4
GLA_DK = 128
GLA_DV = 256
GLA_RANK = 16
GLA_TAU = 16.0
GLA_HK = GLA_HEADS * GLA_DK
GLA_HV = GLA_HEADS * GLA_DV
GLA_MAIN = 2 * GLA_HK + GLA_HV + D_MODEL
MLA_HEADS = 8
MLA_NOPE = 128
MLA_ROPE = 64
MLA_V = 128
MLA_RANK = 256
MLA_IN = 2 * MLA_RANK + MLA_ROPE
MLA_IN_PAD = 640
ROPE_BASE = 10000.0
LANES = 128
ADAM_LR, ADAM_B1, ADAM_B2, ADAM_EPS, ADAM_WD, ADAM_STEP = 0.001, 0.9, 0.999, 1e-08, 0.01, 10

V7X_VMEM_LIMIT_BYTES = 56 * 1024 * 1024

WEIGHTS = {
    "gla_w_in": ((2, 1024, 3088), 2), "gla_w_gate_up": ((2, 16, 512), 2), "gla_b_gate": ((2, 512), 1),
    "gla_norm_g": ((2, 256), 1), "gla_w_out": ((2, 1024, 1024), 1), "mla_w_in": ((1, 1024, 576), 1),
    "mla_q_norm": ((1, 256), None), "mla_kv_norm": ((1, 256), None), "mla_w_uq": ((1, 256, 1536), 2),
    "mla_w_ukv": ((1, 256, 2048), 2), "mla_w_out": ((1, 1024, 1024), 1), "conv_w_in": ((1, 1024, 3072), 2),
    "conv_w": ((1, 3, 1024), 2), "conv_w_out": ((1, 1024, 1024), 1), "ln_g": ((4, 2, 1024), 2),
    "ln_b": ((4, 2, 1024), 2), "mlp_w1": ((4, 1024, 4096), 2), "mlp_w2": ((4, 4096, 1024), 1),
    "ple_w_gate": ((4, 1024, 1024), 1), "ple_w_proj": ((4, 256, 1024), 2),
}
WEIGHT_NAMES = list(WEIGHTS)
REG_W2, REG_W1T, REG_WOUT, REG_WG = (0, 512), (1, 512), (8, 128), (9, 128)
A_ROWS = 1280
REG_CONV = (0, 384)
REG_WPT = (0, 128)
REG_MLA_IN = (0, 128)
MLA_HEAD_PAD = 2 * LANES
SMALL = ["gla_w_gate_up", "gla_b_gate", "gla_norm_g", "conv_w", "ln_g", "ln_b"]
REPLICATED = ["mla_q_norm", "mla_kv_norm"]


def _params(**kw):
    return pltpu.CompilerParams(vmem_limit_bytes=V7X_VMEM_LIMIT_BYTES, **kw)


def _dot(a, b, ca, cb):
    return lax.dot_general(a, b, (((ca,), (cb,)), ((), ())), preferred_element_type=F32)


def _nn(a, b):
    return _dot(a.astype(BF16), b.astype(BF16), 1, 0)


def _nt(a, b):
    return _dot(a.astype(BF16), b.astype(BF16), 1, 1)


def _tn(a, b):
    return _dot(a.astype(BF16), b.astype(BF16), 0, 0)


@jax.custom_vjp
def mm_nn(a, b):
    return _nn(a, b)


def _mm_nn_fwd(a, b):
    return _nn(a, b), (a, b)


def _mm_nn_bwd(res, g):
    a, b = res
    return _nt(g, b).astype(a.dtype), _tn(a, g).astype(b.dtype)


mm_nn.defvjp(_mm_nn_fwd, _mm_nn_bwd)


@jax.custom_vjp
def mm_nt(a, b):
    return _nt(a, b)


def _mm_nt_fwd(a, b):
    return _nt(a, b), (a, b)


def _mm_nt_bwd(res, g):
    a, b = res
    return _nn(g, b).astype(a.dtype), _tn(g, a).astype(b.dtype)


mm_nt.defvjp(_mm_nt_fwd, _mm_nt_bwd)


@jax.custom_vjp
def mm_tn(a, b):
    return _tn(a, b)


def _mm_tn_fwd(a, b):
    return _tn(a, b), (a, b)


def _mm_tn_bwd(res, g):
    a, b = res
    return _nt(b, g).astype(a.dtype), _nn(a, g).astype(b.dtype)


mm_tn.defvjp(_mm_tn_fwd, _mm_tn_bwd)


def _iota2(shape, dim):
    return lax.broadcasted_iota(jnp.int32, shape, dim)


def _split3(x):
    hi = x.astype(BF16)
    rest = x - hi.astype(F32)
    mid = rest.astype(BF16)
    return hi, mid, (rest - mid.astype(F32)).astype(BF16)


def _tri_dot(tri, x):
    return sum(_dot(tri.astype(BF16), piece, 1, 0) for piece in _split3(x))


@jax.custom_vjp
def cumsum_rows(x):
    n = x.shape[0]
    return _tri_dot(_iota2((n, n), 0) >= _iota2((n, n), 1), x)


def _cumsum_fwd(x):
    return cumsum_rows(x), None


def _cumsum_bwd(_, g):
    n = g.shape[0]
    return (_tri_dot(_iota2((n, n), 0) <= _iota2((n, n), 1), g),)


cumsum_rows.defvjp(_cumsum_fwd, _cumsum_bwd)


def _rot_matrix(transposed):
    i, j = _iota2((LANES, LANES), 0), _iota2((LANES, LANES), 1)
    if transposed:
        i, j = j, i
    half = MLA_ROPE // 2
    plus = (i == j - half) & (j >= half) & (j < MLA_ROPE)
    minus = (i == j + half) & (j < half)
    return (plus.astype(F32) - minus.astype(F32)).astype(BF16)


def _rot_dot(x, transposed):
    return sum(_dot(piece, _rot_matrix(transposed), 1, 0) for piece in _split3(x))


@jax.custom_vjp
def rot_half(x):
    return _rot_dot(x, False)


def _rot_fwd(x):
    return rot_half(x), None


def _rot_bwd(_, g):
    return (_rot_dot(g, True),)


rot_half.defvjp(_rot_fwd, _rot_bwd)


def _shift_rows_raw(x, s):
    n = x.shape[0]
    row = _iota2(x.shape, 0)
    rolled = pltpu.roll(x, s % n, 0)
    keep = (row >= s) if s > 0 else (row < n + s)
    return jnp.where(keep, rolled, 0.0)


@functools.partial(jax.custom_vjp, nondiff_argnums=(1,))
def shift_rows(x, s):
    return _shift_rows_raw(x, s)


def _shift_fwd(x, s):
    return _shift_rows_raw(x, s), None


def _shift_bwd(s, _, g):
    return (_shift_rows_raw(g, -s),)


shift_rows.defvjp(_shift_fwd, _shift_bwd)


def _layer_norm(a, g, b):
    mu = jnp.mean(a, -1, keepdims=True)
    xc = a - mu
    var = jnp.mean(xc * xc, -1, keepdims=True)
    return xc * lax.rsqrt(var + LN_EPS) * g + b


def _rms_norm(a, g):
    return a * lax.rsqrt(jnp.mean(a * a, -1, keepdims=True) + RMS_EPS) * g


def _log_sigmoid(z):
    return jnp.minimum(z, 0.0) - jnp.log(1.0 + jnp.exp(-jnp.abs(z)))


def _matmul(a, b, *, name, ta=False, tb=False, tm=512, tn=512, a_fn=None, epi=None, epi_ins=(), out_dtypes=(BF16,),
            b_at=None, out_at=None, out_buf=None, after=None, n_row_sums=0, a_ins=(), a_out_dtypes=()):
    m = a.shape[1] if ta else a.shape[0]
    k = a.shape[0] if ta else a.shape[1]
    if b_at is None:
        n, kb = (b.shape[0], b.shape[1]) if tb else (b.shape[1], b.shape[0])
    else:
        rb, r = b_at
        n, kb = (N_DEV * r, b.shape[2]) if tb else (b.shape[2], N_DEV * r)
    assert kb == k, (name, a.shape, b.shape, k, kb)
    tm, tn = min(tm, m), min(tn, n)
    assert m % tm == 0 and n % tn == 0, (name, m, n, tm, tn)
    a_spec = pl.BlockSpec((k, tm), lambda i, j: (0, i)) if ta else pl.BlockSpec((tm, k), lambda i, j: (i, 0))
    if b_at is None:
        b_spec = pl.BlockSpec((tn, k), lambda i, j: (j, 0)) if tb else pl.BlockSpec((k, tn), lambda i, j: (0, j))
        load_b = lambda ref: ref[...]
    elif tb and tn == n:
        b_spec = pl.BlockSpec((N_DEV, r, k), lambda i, j: (0, rb, 0))
        load_b = lambda ref: ref[...].reshape(n, k)
    elif tb:
        assert tn == r, (name, tn, r)
        b_spec = pl.BlockSpec((1, r, k), lambda i, j: (j, rb, 0))
        load_b = lambda ref: ref[0]
    else:
        b_spec = pl.BlockSpec((N_DEV, r, tn), lambda i, j: (0, rb, j))
        load_b = lambda ref: ref[...].reshape(k, tn)
    e_specs = []
    for e in epi_ins:
        if e.shape == (1, n):
            e_specs.append(pl.BlockSpec((1, tn), lambda i, j: (0, j)))
        else:
            assert e.shape == (m, n), (name, e.shape, m, n)
            e_specs.append(pl.BlockSpec((tm, tn), lambda i, j: (i, j)))
    n_epi, n_ain, n_aout = len(epi_ins), len(a_ins), len(a_out_dtypes)
    assert n_aout == 0 or (tn == n and not ta and out_at is None), name
    ca, cb = (0 if ta else 1), (1 if tb else 0)
    operands = [a, b, *a_ins, *epi_ins]
    in_specs = [a_spec, b_spec, *[a_spec] * n_ain, *e_specs]
    if out_at is None:
        assert n_row_sums == 0 or tn == n, (name, tn, n)
        out_specs = [pl.BlockSpec((tm, tn), lambda i, j: (i, j)) for _ in out_dtypes]
        out_specs += [pl.BlockSpec((tm, k), lambda i, j: (i, 0))] * n_aout
        out_specs += [pl.BlockSpec((1, n), lambda i, j: (0, 0))] * n_row_sums
        out_shape = [jax.ShapeDtypeStruct((m, n), dt) for dt in out_dtypes]
        out_shape += [jax.ShapeDtypeStruct((m, k), dt) for dt in a_out_dtypes]
        out_shape += [jax.ShapeDtypeStruct((1, n), F32)] * n_row_sums
        aliases, n_buf = {}, 0
    else:
        orb, orows = out_at
        assert len(out_dtypes) == 1 and m == N_DEV * orows and n == out_buf.shape[2], (name, m, n)
        if tm > orows:
            assert tm % orows == 0, (name, tm, orows)
            out_specs = [pl.BlockSpec((tm // orows, orows, tn), lambda i, j: (i, orb, j))]
        else:
            per = orows // tm
            out_specs = [pl.BlockSpec((1, tm, tn), lambda i, j: (i // per, orb * per + i % per, j))]
        out_shape = [jax.ShapeDtypeStruct(out_buf.shape, out_buf.dtype)]
        operands.append(out_buf)
        in_specs.append(pl.BlockSpec(memory_space=pl.ANY))
        aliases, n_buf = {len(operands) - 1: 0}, 1
    for dep in ([] if after is None else after if isinstance(after, (list, tuple)) else [after]):
        if dep is not None:
            operands.append(dep)
            in_specs.append(pl.BlockSpec(memory_space=pl.ANY))
            n_buf += 1

    def body(a_ref, b_ref, *rest):
        av, a_outs = a_ref[...], ()
        if a_fn is not None:
            av = a_fn(av, *[r_[...] for r_ in rest[:n_ain]])
            if n_aout:
                av, *a_outs = av
        acc = _dot(av.astype(BF16), load_b(b_ref).astype(BF16), ca, cb)
        outs = epi(acc, *[r_[...] for r_ in rest[n_ain:n_ain + n_epi]]) if epi is not None else (acc,)
        o_refs = rest[n_ain + n_epi + n_buf:]
        n_tiles = len(o_refs) - n_row_sums - n_aout
        for o_ref, val in zip(o_refs[:n_tiles + n_aout], (*outs[:n_tiles], *a_outs)):
            o_ref[...] = val.astype(o_ref.dtype).reshape(o_ref.shape)
        if n_row_sums:
            @pl.when(pl.program_id(0) == 0)
            def _():
                for o_ref in o_refs[n_tiles + n_aout:]:
                    o_ref[...] = jnp.zeros_like(o_ref)

            for o_ref, val in zip(o_refs[n_tiles + n_aout:], outs[n_tiles:]):
                o_ref[...] += val

    outs = pl.pallas_call(
        body, name=name, grid=(m // tm, n // tn), in_specs=in_specs, out_specs=out_specs, out_shape=out_shape,
        input_output_aliases=aliases, compiler_params=_params(),
    )(*operands)
    return outs[0] if len(outs) == 1 else tuple(outs)


def _tile_fwd(f, tiled, params, out_dtypes, *, tm, name):
    t = tiled[0].shape[0]
    assert t % tm == 0
    out_avals = jax.eval_shape(f, *[jax.ShapeDtypeStruct((tm, x.shape[1]), F32) for x in tiled],
                               *[jax.ShapeDtypeStruct(p.shape, F32) for p in params])
    nt, npar = len(tiled), len(params)

    def body(*refs):
        ins = [r[...].astype(F32) for r in refs[:nt + npar]]
        outs = f(*ins)
        for o_ref, val in zip(refs[nt + npar:], outs):
            o_ref[...] = val.astype(o_ref.dtype)

    return pl.pallas_call(
        body, name=name, grid=(t // tm,),
        in_specs=[pl.BlockSpec((tm, x.shape[1]), lambda i: (i, 0)) for x in tiled]
        + [pl.BlockSpec(p.shape, lambda i: (0, 0)) for p in params],
        out_specs=[pl.BlockSpec((tm, o.shape[1]), lambda i: (i, 0)) for o in out_avals],
        out_shape=[jax.ShapeDtypeStruct((t, o.shape[1]), dt) for o, dt in zip(out_avals, out_dtypes)],
        compiler_params=_params(),
    )(*tiled, *params)


def _tile_bwd(f, tiled, params, cots, d_tiled_dtypes, *, tm, name, diff_tiled=None):
    t = tiled[0].shape[0]
    assert t % tm == 0
    nt, npar, nc = len(tiled), len(params), len(cots)
    diff_tiled = list(range(nt)) if diff_tiled is None else diff_tiled

    def body(*refs):
        ins = [r[...].astype(F32) for r in refs[:nt + npar]]
        cts = [r[...].astype(F32) for r in refs[nt + npar:nt + npar + nc]]
        o_refs = refs[nt + npar + nc:]
        _, vjp = jax.vjp(f, *ins)
        grads = vjp(tuple(cts))
        for o_ref, idx in zip(o_refs[:len(diff_tiled)], diff_tiled):
            o_ref[...] = grads[idx].astype(o_ref.dtype)
        p_refs = o_refs[len(diff_tiled):]

        @pl.when(pl.program_id(0) == 0)
        def _():
            for p_ref in p_refs:
                p_ref[...] = jnp.zeros_like(p_ref)

        for p_ref, gp in zip(p_refs, grads[nt:]):
            p_ref[...] += gp

    outs = pl.pallas_call(
        body, name=name, grid=(t // tm,),
        in_specs=[pl.BlockSpec((tm, x.shape[1]), lambda i: (i, 0)) for x in tiled]
        + [pl.BlockSpec(p.shape, lambda i: (0, 0)) for p in params]
        + [pl.BlockSpec((tm, c.shape[1]), lambda i: (i, 0)) for c in cots],
        out_specs=[pl.BlockSpec((tm, tiled[idx].shape[1]), lambda i: (i, 0)) for idx in diff_tiled]
        + [pl.BlockSpec(p.shape, lambda i: (0, 0)) for p in params],
        out_shape=[jax.ShapeDtypeStruct(tiled[idx].shape, dt) for idx, dt in zip(diff_tiled, d_tiled_dtypes)]
        + [jax.ShapeDtypeStruct(p.shape, F32) for p in params],
        compiler_params=_params(),
    )(*tiled, *params, *cots)
    return outs[:len(diff_tiled)], outs[len(diff_tiled):]


def _gla_head(q, k, v, r, z, g, st):
    c = q.shape[0]
    causal = _iota2((c, c), 0) >= _iota2((c, c), 1)
    la = _log_sigmoid(z) * (1.0 / GLA_TAU)
    big_l = cumsum_rows(la)
    ep, en = jnp.exp(big_l), jnp.exp(-big_l)
    qs = q * (GLA_DK ** -0.5)
    qp = qs * ep
    s = jnp.where(causal, mm_nt(qp, k * en), mm_nt(qs * en, k * ep))
    o = mm_nn(s, v) + mm_nt(qp, st)
    l_end = jnp.sum(la, axis=0, keepdims=True)
    st_new = st * jnp.exp(l_end) + mm_tn(v, k * jnp.exp(l_end - big_l))
    u = _rms_norm(o, g) * (r * jax.nn.sigmoid(r))
    return u, st_new


def _gla_slices(h):
    q = slice(GLA_DK * h, GLA_DK * (h + 1))
    k = slice(GLA_HK + GLA_DK * h, GLA_HK + GLA_DK * (h + 1))
    v = slice(2 * GLA_HK + GLA_DV * h, 2 * GLA_HK + GLA_DV * (h + 1))
    r = slice(2 * GLA_HK + GLA_HV + GLA_DV * h, 2 * GLA_HK + GLA_HV + GLA_DV * (h + 1))
    return q, k, v, r


GLA_CHUNKS_PER_STEP = 8


def _gla_fwd(proj, z, norm_g, after):
    t = proj.shape[0]
    nc, per = t // CHUNK, GLA_CHUNKS_PER_STEP
    rows_per_step = per * CHUNK
    after = [a for a in after if a is not None]

    def body(proj_ref, z_ref, g_ref, *rest):
        u_ref, st_save_ref, st_ref = rest[len(after):]

        @pl.when(pl.program_id(0) == 0)
        def _():
            st_ref[...] = jnp.zeros_like(st_ref)

        g = g_ref[...]
        for h in range(GLA_HEADS):
            sq, sk, sv, sr = _gla_slices(h)
            st = st_ref[h]
            for c in range(per):
                rows = slice(c * CHUNK, (c + 1) * CHUNK)
                st_save_ref[c, h] = st
                u, st = _gla_head(proj_ref[rows, sq].astype(F32), proj_ref[rows, sk].astype(F32),
                                  proj_ref[rows, sv].astype(F32), proj_ref[rows, sr].astype(F32),
                                  z_ref[rows, GLA_DK * h:GLA_DK * (h + 1)], g, st)
                u_ref[rows, GLA_DV * h:GLA_DV * (h + 1)] = u.astype(u_ref.dtype)
            st_ref[h] = st

    return pl.pallas_call(
        body, name="gla_fwd", grid=(nc // per,),
        in_specs=[pl.BlockSpec((rows_per_step, GLA_MAIN), lambda i: (i, 0)),
                  pl.BlockSpec((rows_per_step, GLA_HK), lambda i: (i, 0)), pl.BlockSpec((1, GLA_DV), lambda i: (0, 0))]
        + [pl.BlockSpec(memory_space=pl.ANY)] * len(after),
        out_specs=[pl.BlockSpec((rows_per_step, GLA_HV), lambda i: (i, 0)),
                   pl.BlockSpec((per, GLA_HEADS, GLA_DV, GLA_DK), lambda i: (i, 0, 0, 0))],
        out_shape=[jax.ShapeDtypeStruct((t, GLA_HV), BF16), jax.ShapeDtypeStruct((nc, GLA_HEADS, GLA_DV, GLA_DK), F32)],
        scratch_shapes=[pltpu.VMEM((GLA_HEADS, GLA_DV, GLA_DK), F32)],
        compiler_params=_params(),
    )(proj, z, norm_g, *after)


def _gla_bwd(proj, z, norm_g, states, du, after):
    t = proj.shape[0]
    nc, per = t // CHUNK, GLA_CHUNKS_PER_STEP
    rows_per_step = per * CHUNK
    n_steps = nc // per
    after = [a for a in after if a is not None]

    def body(proj_ref, z_ref, g_ref, st_in_ref, du_ref, *rest):
        dproj_ref, dz_ref, dg_ref, dzsum_ref, dst_ref = rest[len(after):]

        @pl.when(pl.program_id(0) == 0)
        def _():
            dst_ref[...] = jnp.zeros_like(dst_ref)
            dg_ref[...] = jnp.zeros_like(dg_ref)
            dzsum_ref[...] = jnp.zeros_like(dzsum_ref)

        g = g_ref[...]
        for h in range(GLA_HEADS):
            sq, sk, sv, sr = _gla_slices(h)
            dst = dst_ref[h]
            for c in reversed(range(per)):
                rows = slice(c * CHUNK, (c + 1) * CHUNK)
                ins = (proj_ref[rows, sq].astype(F32), proj_ref[rows, sk].astype(F32), proj_ref[rows, sv].astype(F32),
                       proj_ref[rows, sr].astype(F32), z_ref[rows, GLA_DK * h:GLA_DK * (h + 1)], g, st_in_ref[c, h])
                _, vjp = jax.vjp(_gla_head, *ins)
                dq, dk, dv, dr, dz, dg, dst = vjp((du_ref[rows, GLA_DV * h:GLA_DV * (h + 1)], dst))
                dproj_ref[rows, sq] = dq.astype(dproj_ref.dtype)
                dproj_ref[rows, sk] = dk.astype(dproj_ref.dtype)
                dproj_ref[rows, sv] = dv.astype(dproj_ref.dtype)
                dproj_ref[rows, sr] = dr.astype(dproj_ref.dtype)
                dz_ref[rows, GLA_DK * h:GLA_DK * (h + 1)] = dz
                dzsum_ref[:, GLA_DK * h:GLA_DK * (h + 1)] += jnp.sum(dz, axis=0, keepdims=True)
                dg_ref[...] += dg
            dst_ref[h] = dst

    rev = lambda i: (n_steps - 1 - i, 0)
    return pl.pallas_call(
        body, name="gla_bwd", grid=(n_steps,),
        in_specs=[pl.BlockSpec((rows_per_step, GLA_MAIN), rev), pl.BlockSpec((rows_per_step, GLA_HK), rev),
                  pl.BlockSpec((1, GLA_DV), lambda i: (0, 0)),
                  pl.BlockSpec((per, GLA_HEADS, GLA_DV, GLA_DK), lambda i: (n_steps - 1 - i, 0, 0, 0)),
                  pl.BlockSpec((rows_per_step, GLA_HV), rev)] + [pl.BlockSpec(memory_space=pl.ANY)] * len(after),
        out_specs=[pl.BlockSpec((rows_per_step, GLA_MAIN), rev), pl.BlockSpec((rows_per_step, GLA_HK), rev),
                   pl.BlockSpec((1, GLA_DV), lambda i: (0, 0)), pl.BlockSpec((1, GLA_HK), lambda i: (0, 0))],
        out_shape=[jax.ShapeDtypeStruct((t, GLA_MAIN), BF16), jax.ShapeDtypeStruct((t, GLA_HK), F32),
                   jax.ShapeDtypeStruct((1, GLA_DV), F32), jax.ShapeDtypeStruct((1, GLA_HK), F32)],
        scratch_shapes=[pltpu.VMEM((GLA_HEADS, GLA_DV, GLA_DK), F32)],
        compiler_params=_params(),
    )(proj, z, norm_g, states, du, *after)


def _mla_pre(cq, cos, sin, gq, gkv, w_uq, w_ukv):
    qlat = _rms_norm(cq[:, :MLA_RANK], gq)
    kvlat = _rms_norm(cq[:, MLA_RANK:2 * MLA_RANK], gkv)
    kr = cq[:, 2 * MLA_RANK:]
    q = mm_nn(qlat, w_uq) * ((MLA_NOPE + MLA_ROPE) ** -0.5)
    kv = mm_nn(kvlat, w_ukv)
    pieces = []
    for h in range(MLA_HEADS):
        qr = q[:, MLA_HEAD_PAD * h + MLA_NOPE:MLA_HEAD_PAD * (h + 1)]
        pieces += [q[:, MLA_HEAD_PAD * h:MLA_HEAD_PAD * h + MLA_NOPE], qr * cos + rot_half(qr) * sin]
    return jnp.concatenate(pieces, axis=1), kv, kr * cos + rot_half(kr) * sin


MLA_Q_TILE = 512


def _mla_attn_block(qn, qr, kv, kr, q0):
    tq, nk = qn.shape[0], kv.shape[0]
    s = mm_nt(qn, kv[:, :MLA_NOPE]) + mm_nt(qr, kr)
    visible = (_iota2((tq, nk), 1) // CHUNK) <= ((q0 + _iota2((tq, nk), 0)) // CHUNK)
    s = jnp.where(visible, s, -1e30)
    e = jnp.exp(s - jnp.max(s, -1, keepdims=True))
    p = e / jnp.sum(e, -1, keepdims=True)
    return mm_nn(p, kv[:, MLA_NOPE:])


def _mla_attn_fwd(q, kv, kr, after):
    t = q.shape[0]
    after = [a for a in after if a is not None]

    def body(q_ref, kv_ref, kr_ref, *rest):
        (o_ref,) = rest[len(after):]
        for i in range(t // MLA_Q_TILE):
            rows = slice(i * MLA_Q_TILE, (i + 1) * MLA_Q_TILE)
            keys = slice(0, (i + 1) * MLA_Q_TILE)
            o = _mla_attn_block(q_ref[rows, :MLA_NOPE].astype(F32), q_ref[rows, MLA_NOPE:].astype(F32),
                                kv_ref[keys, :].astype(F32), kr_ref[keys, :].astype(F32), i * MLA_Q_TILE)
            o_ref[rows, :] = o.astype(o_ref.dtype)

    return pl.pallas_call(
        body, name="mla_attn_fwd", grid=(MLA_HEADS,),
        in_specs=[pl.BlockSpec((t, MLA_HEAD_PAD), lambda h: (0, h)),
                  pl.BlockSpec((t, MLA_NOPE + MLA_V), lambda h: (0, h)), pl.BlockSpec((t, LANES), lambda h: (0, 0))]
        + [pl.BlockSpec(memory_space=pl.ANY)] * len(after),
        out_specs=pl.BlockSpec((t, MLA_V), lambda h: (0, h)),
        out_shape=jax.ShapeDtypeStruct((t, MLA_HEADS * MLA_V), BF16),
        compiler_params=_params(),
    )(q, kv, kr, *after)


def _mla_attn_bwd(q, kv, kr, do, after):
    t = q.shape[0]
    after = [a for a in after if a is not None]

    def body(q_ref, kv_ref, kr_ref, do_ref, *rest):
        dq_ref, dkv_ref, dkr_ref = rest[len(after):]
        dkv_ref[...] = jnp.zeros_like(dkv_ref)

        @pl.when(pl.program_id(0) == 0)
        def _():
            dkr_ref[...] = jnp.zeros_like(dkr_ref)

        for i in range(t // MLA_Q_TILE):
            rows = slice(i * MLA_Q_TILE, (i + 1) * MLA_Q_TILE)
            keys = slice(0, (i + 1) * MLA_Q_TILE)
            f = functools.partial(_mla_attn_block, q0=i * MLA_Q_TILE)
            _, vjp = jax.vjp(f, q_ref[rows, :MLA_NOPE].astype(F32), q_ref[rows, MLA_NOPE:].astype(F32),
                             kv_ref[keys, :].astype(F32), kr_ref[keys, :].astype(F32))
            dqn, dqr, dkv, dkr = vjp(do_ref[rows, :].astype(F32))
            dq_ref[rows, :MLA_NOPE] = dqn
            dq_ref[rows, MLA_NOPE:] = dqr
            dkv_ref[keys, :] += dkv
            dkr_ref[keys, :] += dkr

    return pl.pallas_call(
        body, name="mla_attn_bwd", grid=(MLA_HEADS,),
        in_specs=[pl.BlockSpec((t, MLA_HEAD_PAD), lambda h: (0, h)),
                  pl.BlockSpec((t, MLA_NOPE + MLA_V), lambda h: (0, h)), pl.BlockSpec((t, LANES), lambda h: (0, 0)),
                  pl.BlockSpec((t, MLA_V), lambda h: (0, h))] + [pl.BlockSpec(memory_space=pl.ANY)] * len(after),
        out_specs=[pl.BlockSpec((t, MLA_HEAD_PAD), lambda h: (0, h)),
                   pl.BlockSpec((t, MLA_NOPE + MLA_V), lambda h: (0, h)), pl.BlockSpec((t, LANES), lambda h: (0, 0))],
        out_shape=[jax.ShapeDtypeStruct(q.shape, F32), jax.ShapeDtypeStruct(kv.shape, F32),
                   jax.ShapeDtypeStruct(kr.shape, F32)],
        compiler_params=_params(),
    )(q, kv, kr, do, *after)


def _rope_tables(pos_col, inv_freq_row):
    t = pos_col.shape[0]

    def body(pos_ref, f_ref, cos_ref, sin_ref):
        ang = pos_ref[...].astype(F32) * f_ref[...]
        live = _iota2(ang.shape, 1) < MLA_ROPE
        cos_ref[...] = jnp.where(live, jnp.cos(ang), 0.0)
        sin_ref[...] = jnp.where(live, jnp.sin(ang), 0.0)

    return pl.pallas_call(
        body, name="rope_tables", out_shape=[jax.ShapeDtypeStruct((t, LANES), F32)] * 2, compiler_params=_params(),
    )(pos_col, inv_freq_row)


CONV_COL_TILE = 256


def _conv_gate(b, c, u, w0, w1, w2):
    cu = c * u
    return b * (w2 * cu + w1 * shift_rows(cu, 1) + w0 * shift_rows(cu, 2))


def _conv_specs(t):
    nb = D_MODEL // CONV_COL_TILE
    return [pl.BlockSpec((t, CONV_COL_TILE), lambda j, part=part: (0, part * nb + j)) for part in range(3)]


def _conv_fwd(bcu, w, after):
    t = bcu.shape[0]
    after = [a for a in after if a is not None]

    def body(b_ref, c_ref, u_ref, w_ref, *rest):
        (o_ref,) = rest[len(after):]
        o_ref[...] = _conv_gate(b_ref[...], c_ref[...], u_ref[...], w_ref[0:1, :], w_ref[1:2, :],
                                w_ref[2:3, :]).astype(o_ref.dtype)

    return pl.pallas_call(
        body, name="conv_fwd", grid=(D_MODEL // CONV_COL_TILE,),
        in_specs=_conv_specs(t) + [pl.BlockSpec((3, CONV_COL_TILE), lambda j: (0, j))]
        + [pl.BlockSpec(memory_space=pl.ANY)] * len(after),
        out_specs=pl.BlockSpec((t, CONV_COL_TILE), lambda j: (0, j)),
        out_shape=jax.ShapeDtypeStruct((t, D_MODEL), BF16), compiler_params=_params(),
    )(bcu, bcu, bcu, w, *after)


def _conv_bwd(bcu, w, dout, after):
    t = bcu.shape[0]
    after = [a for a in after if a is not None]

    def body(b_ref, c_ref, u_ref, w_ref, do_ref, *rest):
        db_ref, dc_ref, du_ref, dw_ref = rest[len(after):]
        _, vjp = jax.vjp(_conv_gate, b_ref[...], c_ref[...], u_ref[...], w_ref[0:1, :], w_ref[1:2, :], w_ref[2:3, :])
        db, dc, du, dw0, dw1, dw2 = vjp(do_ref[...])
        db_ref[...] = db.astype(db_ref.dtype)
        dc_ref[...] = dc.astype(dc_ref.dtype)
        du_ref[...] = du.astype(du_ref.dtype)
        dw_ref[0:1, :] = dw0
        dw_ref[1:2, :] = dw1
        dw_ref[2:3, :] = dw2

    col = pl.BlockSpec((t, CONV_COL_TILE), lambda j: (0, j))
    return pl.pallas_call(
        body, name="conv_bwd", grid=(D_MODEL // CONV_COL_TILE,),
        in_specs=_conv_specs(t) + [pl.BlockSpec((3, CONV_COL_TILE), lambda j: (0, j)), col]
        + [pl.BlockSpec(memory_space=pl.ANY)] * len(after),
        out_specs=[col, col, col, pl.BlockSpec((3, CONV_COL_TILE), lambda j: (0, j))],
        out_shape=[jax.ShapeDtypeStruct((t, D_MODEL), BF16)] * 3 + [jax.ShapeDtypeStruct((3, D_MODEL), F32)],
        compiler_params=_params(),
    )(bcu, bcu, bcu, w, dout, *after)


def _loss_head(y, target):
    t, d = y.shape
    tm = 256

    def body(y_ref, t_ref, loss_ref, dy_ref):
        @pl.when(pl.program_id(0) == 0)
        def _():
            loss_ref[...] = jnp.zeros_like(loss_ref)

        err = y_ref[...] - t_ref[...]
        dy_ref[...] = err * (1.0 / d)
        loss_ref[...] += 0.5 * jnp.sum(jnp.sum(err * err, axis=-1, keepdims=True) * (1.0 / d))

    tile = pl.BlockSpec((tm, d), lambda i: (i, 0))
    return pl.pallas_call(
        body, name="loss_head", grid=(t // tm,), in_specs=[tile, tile],
        out_specs=[pl.BlockSpec((8, LANES), lambda i: (0, 0)), tile],
        out_shape=[jax.ShapeDtypeStruct((8, LANES), F32), jax.ShapeDtypeStruct((t, d), F32)],
        compiler_params=_params(),
    )(y, target)


def _ln_epi(acc, res, g, b):
    a = ALPHA * res + acc
    y = _layer_norm(a, g, b)
    return a, y, y


def _ln_fn(a, g, b):
    return (_layer_norm(a, g, b),)


def _ln_bwd_epi(scale):
    def epi(acc, res, a, g, b):
        _, vjp = jax.vjp(_ln_fn, a, g, b)
        da, dg, db = vjp((acc + scale * res,))
        return da, da, dg, db
    return epi


def _relu_sq(h):
    r = jnp.maximum(h, 0)
    return r * r


def _pad_cols(w, n):
    return jnp.pad(w, ((0, 0), (0, n - w.shape[1])))


def _pad_rows(w, n):
    return jnp.pad(w, ((0, n - w.shape[0]), (0, 0)))


def _step(x, p, positions, target, small, comm):
    t = x.shape[0]
    w = small
    freqs = ROPE_BASE ** (-jnp.arange(0, MLA_ROPE // 2, dtype=F32) * (2.0 / MLA_ROPE))
    freq_row = jnp.concatenate([freqs, freqs, jnp.zeros((LANES - MLA_ROPE,), F32)])[None, :]
    cos, sin = _rope_tables(positions.reshape(t, 1), freq_row)

    saved = []
    xb = x.astype(BF16)
    for i in range(DEPTH):
        j, kind = i // 3, i % 3
        wl = comm.mixer_weights(i)
        s = {"x": xb, "wl": wl}
        tok = comm.at("fwd", i, "begin", x)
        if kind == 0:
            s["w_main"] = wl["gla_w_in_t"][:GLA_MAIN]
            s["w_lr"] = _pad_rows(wl["gla_w_in_t"][GLA_MAIN:], LANES)
            s["w_up"] = _pad_rows(w["gla_w_gate_up"][j], LANES).astype(BF16)
            s["proj"] = _matmul(xb, s["w_main"], name="gla_proj", tb=True, tn=1024, after=tok)
            s["glr"] = _matmul(xb, s["w_lr"], name="gla_lr", tb=True, out_dtypes=(F32,))
            s["z"] = _matmul(s["glr"], s["w_up"], name="gla_gate", epi=lambda acc, b: (acc + b,),
                             epi_ins=(w["gla_b_gate"][j][None, :],), out_dtypes=(F32,))
            tok = comm.at("fwd", i, "proj_done", s["z"]) or []
            s["u"], s["states"] = _gla_fwd(s["proj"], s["z"], w["gla_norm_g"][j][None, :], tok)
        elif kind == 1:
            s["cq"] = _matmul(xb, wl["mla_in"], name="mla_proj", tn=MLA_IN_PAD, b_at=REG_MLA_IN, out_dtypes=(F32,),
                              after=tok)
            s["pre_params"] = (w["mla_q_norm"][j][None, :], w["mla_kv_norm"][j][None, :], wl["mla_w_uq"], wl["mla_w_ukv"])
            s["q"], s["kv"], s["kr"] = _tile_fwd(_mla_pre, (s["cq"], cos, sin), s["pre_params"], (BF16, BF16, BF16),
                                                 tm=256, name="mla_pre_fwd")
            tok = comm.at("fwd", i, "proj_done", s["kv"]) or []
            s["u"] = _mla_attn_fwd(s["q"], s["kv"], s["kr"], tok)
        else:
            s["bcu"] = _matmul(xb, wl["conv"], name="conv_proj", tb=True, tm=256, tn=3 * D_MODEL, b_at=REG_CONV,
                               out_dtypes=(F32,), after=tok)
            tok = comm.at("fwd", i, "proj_done", s["bcu"]) or []
            s["u"] = _conv_fwd(s["bcu"], w["conv_w"][j], tok)
        g0, b0 = w["ln_g"][i, 0][None, :], w["ln_b"][i, 0][None, :]
        g1, b1 = w["ln_g"][i, 1][None, :], w["ln_b"][i, 1][None, :]
        wa, wb = s["wa"], _ = comm.slab_weights(i, s["u"])
        s["a1"], s["x1"], s["x1b"] = _matmul(s["u"], wa, name="mixer_out_ln", tm=256, tn=D_MODEL, b_at=REG_WOUT,
                                             epi=_ln_epi, epi_ins=(x, g0, b0), out_dtypes=(F32, F32, BF16))
        s["hh"] = _matmul(s["x1b"], wa, name="mlp_up", tb=True, tm=512, tn=D_FF, b_at=REG_W1T)
        tok = comm.at("fwd", i, "mid", s["hh"])
        s["a2"], s["x2"], s["x2b"] = _matmul(s["hh"], wa, name="mlp_down_ln", tm=256, tn=D_MODEL, b_at=REG_W2,
                                             a_fn=_relu_sq, epi=_ln_epi, epi_ins=(s["x1"], g1, b1),
                                             out_dtypes=(F32, F32, BF16), after=tok)
        s["pp"] = _matmul(p[i], wb, name="ple_proj", tb=True, tn=D_MODEL, b_at=REG_WPT)
        tok = comm.at("fwd", i, "end", s["pp"])
        def ple_epi(acc, xr, pp):
            y = xr + jax.nn.sigmoid(acc) * pp.astype(F32)
            return y, y, acc

        x, xb, s["gt"] = _matmul(s["x2b"], wa, name="ple_gate", tn=1024, b_at=REG_WG, epi=ple_epi,
                                 epi_ins=(s["x2"], s["pp"]), out_dtypes=(F32, BF16, BF16), after=tok)
        saved.append(s)

    loss_part, dx = _loss_head(x, target)

    gw = {n: [None] * WEIGHTS[n][0][0] for n in SMALL + REPLICATED}
    ln_g_grads, ln_b_grads = [[None, None] for _ in range(DEPTH)], [[None, None] for _ in range(DEPTH)]
    resid = lambda acc, r: (acc + ALPHA * r,)
    plus = lambda acc, r: (acc + r,)
    for i in reversed(range(DEPTH)):
        j, kind = i // 3, i % 3
        s = saved[i]
        wa = s["wa"]
        ga = lax.empty((N_DEV, A_ROWS, D_MODEL), BF16)
        gb = lax.empty((N_DEV, REG_WPT[1], PLE_DIM), BF16)
        layer_grads = {}
        tok = comm.at("bwd", i, "begin", dx)

        def ple_bwd(dxo, gt, pp):
            sg = jax.nn.sigmoid(gt.astype(F32))
            d_gt = dxo * pp.astype(F32) * sg * (1.0 - sg)
            return d_gt, d_gt, dxo * sg

        g1, b1 = w["ln_g"][i, 1][None, :], w["ln_b"][i, 1][None, :]
        d_a2, d_a2b, d_gt, d_pp, ln_g_grads[i][1], ln_b_grads[i][1] = _matmul(
            dx, wa, name="ple_gate_dx_ln", tb=True, tm=256, tn=D_MODEL, b_at=REG_WG, a_fn=ple_bwd,
            a_ins=(s["gt"], s["pp"]), a_out_dtypes=(BF16, BF16), epi=_ln_bwd_epi(1.0), epi_ins=(dx, s["a2"], g1, b1),
            out_dtypes=(F32, BF16), n_row_sums=2, after=tok)
        gb = _matmul(d_pp, p[i], name="ple_proj_dw", ta=True, tm=512, tn=PLE_DIM, out_at=REG_WPT, out_buf=gb)
        ga = _matmul(s["x2b"], d_gt, name="ple_gate_dw", ta=True, tm=512, tn=1024, out_at=REG_WG, out_buf=ga)
        tok = comm.at("bwd", i, "ln", d_a2)
        ga = _matmul(s["hh"], d_a2b, name="mlp_down_dw", ta=True, tm=REG_W2[1], tn=1024, a_fn=_relu_sq, out_at=REG_W2,
                     out_buf=ga, after=tok)
        d_hh = _matmul(d_a2b, wa, name="mlp_down_dx", tb=True, tm=512, tn=D_FF, b_at=REG_W2, after=[ga, gb],
                       epi=lambda acc, hh: (acc * 2.0 * jnp.maximum(hh.astype(F32), 0.0),), epi_ins=(s["hh"],))
        ga = _matmul(d_hh, s["x1b"], name="mlp_up_dw", ta=True, tm=REG_W1T[1], tn=1024, out_at=REG_W1T, out_buf=ga)
        g0, b0 = w["ln_g"][i, 0][None, :], w["ln_b"][i, 0][None, :]
        d_a1, d_a1b, ln_g_grads[i][0], ln_b_grads[i][0] = _matmul(
            d_hh, wa, name="mlp_up_dx_ln", tm=256, tn=D_MODEL, b_at=REG_W1T, epi=_ln_bwd_epi(ALPHA),
            epi_ins=(d_a2, s["a1"], g0, b0), out_dtypes=(F32, BF16), n_row_sums=2, after=ga)
        ga = _matmul(s["u"], d_a1b, name="mixer_out_dw", ta=True, tm=512, tn=1024, out_at=REG_WOUT, out_buf=ga)
        du = _matmul(d_a1b, wa, name="mixer_out_dx", tb=True, tn=1024, b_at=REG_WOUT, out_dtypes=(F32,), after=ga)
        comm.slab_grads(i, ga, gb)
        tok = comm.at("bwd", i, "slab_done", du) or []
        if kind == 0:
            dproj, dz, dg, dz_sum = _gla_bwd(s["proj"], s["z"], w["gla_norm_g"][j][None, :], s["states"], du, tok)
            tok = comm.at("bwd", i, "mixer_done", dproj)
            gw["gla_norm_g"][j] = dg[0]
            gw["gla_b_gate"][j] = dz_sum[0]
            gw["gla_w_gate_up"][j] = _matmul(s["glr"], dz, name="gla_gate_dw", ta=True, out_dtypes=(F32,),
                                             after=tok)[:GLA_RANK]
            dglr = _matmul(dz, s["w_up"], name="gla_gate_dx", tb=True, out_dtypes=(F32,))
            dw_main = _matmul(dproj, s["x"], name="gla_proj_dw", ta=True, tn=1024, out_dtypes=(F32,))
            dw_lr = _matmul(dglr, s["x"], name="gla_lr_dw", ta=True, tn=1024, out_dtypes=(F32,))[:GLA_RANK]
            layer_grads["gla_w_in_t"] = jnp.concatenate([dw_main, dw_lr], axis=0)
            dx = _matmul(dproj, s["w_main"], name="gla_proj_dx", tn=1024, epi=resid, epi_ins=(d_a1,),
                         out_dtypes=(F32,), after=[dw_main, dw_lr, gw["gla_w_gate_up"][j]])
            dx = _matmul(dglr, s["w_lr"], name="gla_lr_dx", tn=1024, epi=plus, epi_ins=(dx,), out_dtypes=(F32,))
        elif kind == 1:
            dq, dkv, dkr = _mla_attn_bwd(s["q"], s["kv"], s["kr"], du, tok)
            tok = comm.at("bwd", i, "mixer_done", dq)
            (d_cq,), (dgq, dgkv, layer_grads["mla_uq"], layer_grads["mla_ukv"]) = _tile_bwd(
                _mla_pre, (s["cq"], cos, sin), s["pre_params"], (dq, dkv, dkr), (BF16,), tm=256, name="mla_pre_bwd",
                diff_tiled=[0])
            gw["mla_q_norm"][j], gw["mla_kv_norm"][j] = dgq[0], dgkv[0]
            layer_grads["mla_in"] = _matmul(s["x"], d_cq, name="mla_proj_dw", ta=True, tm=512, tn=MLA_IN_PAD,
                                            out_at=REG_MLA_IN, after=tok,
                                            out_buf=lax.empty((N_DEV, REG_MLA_IN[1], MLA_IN_PAD), BF16))
            dx = _matmul(d_cq, s["wl"]["mla_in"], name="mla_proj_dx", tb=True, tn=1024, b_at=REG_MLA_IN, epi=resid,
                         epi_ins=(d_a1,), out_dtypes=(F32,), after=layer_grads["mla_in"])
        else:
            db, dc, du_, dcw = _conv_bwd(s["bcu"], w["conv_w"][j], du, tok)
            tok = comm.at("bwd", i, "mixer_done", db)
            gw["conv_w"][j] = dcw
            dbcu = jnp.concatenate([db, dc, du_], axis=1)
            layer_grads["conv"] = _matmul(dbcu, s["x"], name="conv_proj_dw", ta=True, tm=REG_CONV[1], tn=1024,
                                          out_at=REG_CONV, out_buf=lax.empty((N_DEV, REG_CONV[1], D_MODEL), BF16),
                                          after=tok)
            dx = _matmul(dbcu, s["wl"]["conv"], name="conv_proj_dx", tn=1024, b_at=REG_CONV, epi=resid, epi_ins=(d_a1,),
                         out_dtypes=(F32,), after=layer_grads["conv"])
        comm.mixer_grads(i, layer_grads)

    gw["ln_g"] = [jnp.concatenate([a, b], axis=0) for a, b in ln_g_grads]
    gw["ln_b"] = [jnp.concatenate([a, b], axis=0) for a, b in ln_b_grads]
    return loss_part, dx, {n: jnp.stack(gw[n]).astype(F32) for n in gw}


MESH_IDS = pl.DeviceIdType.MESH
ANY = pl.BlockSpec(memory_space=pl.ANY)
HBM_SPEC = pl.BlockSpec(memory_space=pltpu.HBM)
SEM_SPEC = pl.BlockSpec(memory_space=pltpu.SEMAPHORE)
DATAFLOW_EFFECT = pltpu.SideEffectType.DATAFLOW_SIDE_EFFECTING
CORE_COPIES, CHIP_COPIES = 4, 3


def _my_place():
    return lax.axis_index("x"), lax.axis_index("y"), lax.axis_index("c")


def _other_chips(mx, my):
    return [(1 - mx, my), (mx, 1 - my), (1 - mx, 1 - my)]


def _remote(src, dst, send_sems, recv_sems, k, to):
    return pltpu.make_async_remote_copy(src_ref=src, dst_ref=dst, send_sem=send_sems.at[k], recv_sem=recv_sems.at[k],
                                        device_id=to, device_id_type=MESH_IDS)


def _gather_first_copies(n_arr):
    def make(bufs, send_sems, recv_sems):
        mx, my, mc = _my_place()
        mine = 4 * mx + 2 * my + mc
        peers = [(mx, my, 1 - mc)] + [(cx, cy, mc) for cx, cy in _other_chips(mx, my)]
        return [_remote(bufs[a].at[mine], bufs[a].at[mine], send_sems, recv_sems, (1 + CHIP_COPIES) * a + k, to)
                for a in range(n_arr) for k, to in enumerate(peers)]
    return make, (1 + CHIP_COPIES) * n_arr


def _gather_forward_copies(n_arr):
    def make(bufs, send_sems, recv_sems):
        mx, my, mc = _my_place()
        blocks = [4 * cx + 2 * cy + mc for cx, cy in _other_chips(mx, my)]
        return [_remote(bufs[a].at[blk], bufs[a].at[blk], send_sems, recv_sems, CHIP_COPIES * a + k, (mx, my, 1 - mc))
                for a in range(n_arr) for k, blk in enumerate(blocks)]
    return make, CHIP_COPIES * n_arr


def _scatter_core_copies(n_arr):
    def make(bufs, send_sems, recv_sems):
        mx, my, mc = _my_place()
        return [_remote(bufs[a].at[2 * k + (1 - mc)], bufs[n_arr + a].at[k], send_sems, recv_sems, CORE_COPIES * a + k,
                        (mx, my, 1 - mc)) for a in range(n_arr) for k in range(CORE_COPIES)]
    return make, CORE_COPIES * n_arr


def _scatter_chip_copies(n_arr):
    def make(bufs, send_sems, recv_sems):
        mx, my, mc = _my_place()
        return [_remote(bufs[a].at[2 * cx + cy], bufs[n_arr + a].at[k], send_sems, recv_sems, CHIP_COPIES * a + k,
                        (cx, cy, mc)) for a in range(n_arr) for k, (cx, cy) in enumerate(_other_chips(mx, my))]
    return make, CHIP_COPIES * n_arr


def _exchange(name, bufs, copies):
    make, n_copies = copies
    n = len(bufs)

    def body(*refs):
        descs = make(refs[:n], refs[2 * n], refs[2 * n + 1])
        for cp in descs:
            cp.start()
        for cp in descs:
            cp.wait()

    return pl.pallas_call(
        body, name=name, out_shape=[jax.ShapeDtypeStruct(b.shape, b.dtype) for b in bufs], in_specs=[ANY] * n,
        out_specs=[ANY] * n, input_output_aliases={i: i for i in range(n)},
        scratch_shapes=[pltpu.SemaphoreType.DMA((n_copies,)), pltpu.SemaphoreType.DMA((n_copies,))],
    )(*bufs)


def _exchange_start(name, bufs, copies, after):
    make, n_copies = copies
    n = len(bufs)

    def body(*refs):
        for cp in make(refs[:n], refs[n + 1], refs[n + 2]):
            cp.start()
        refs[-1][...] = jnp.zeros_like(refs[-1])

    outs = pl.pallas_call(
        body, name=name,
        out_shape=(pltpu.SemaphoreType.DMA((n_copies,)), pltpu.SemaphoreType.DMA((n_copies,)),
                   *[pltpu.HBM(b.shape, b.dtype) for b in bufs], jax.ShapeDtypeStruct((8, LANES), F32)),
        in_specs=[HBM_SPEC] * n + [ANY],
        out_specs=(SEM_SPEC, SEM_SPEC, *[HBM_SPEC] * n, pl.BlockSpec(memory_space=pltpu.VMEM)),
        input_output_aliases={i: 2 + i for i in range(n)},
        compiler_params=pltpu.CompilerParams(has_side_effects=DATAFLOW_EFFECT),
    )(*[pltpu.with_memory_space_constraint(b, pltpu.HBM) for b in bufs], after)
    return (outs[0], outs[1]), list(outs[2:2 + n]), outs[-1]


def _exchange_wait(name, sems, bufs, copies, after):
    make, _ = copies
    n = len(bufs)

    def body(*refs):
        for cp in make(refs[:n], refs[n], refs[n + 1]):
            cp.wait_send()
            cp.wait_recv()

    return list(pl.pallas_call(
        body, name=name, out_shape=[pltpu.HBM(b.shape, b.dtype) for b in bufs],
        in_specs=[HBM_SPEC] * n + [SEM_SPEC, SEM_SPEC, ANY], out_specs=[HBM_SPEC] * n,
        input_output_aliases={i: i for i in range(n)},
        compiler_params=pltpu.CompilerParams(has_side_effects=DATAFLOW_EFFECT),
    )(*bufs, *sems, after))


SUM_TILE_BYTES = 4 * 1024 * 1024


def _row_tile(r, c):
    best = None
    for cand in range(16, r + 1, 16):
        if r % cand == 0 and cand * c * 2 <= SUM_TILE_BYTES:
            best = cand
    return r if best is None else best


def _pair_sum(g, recv, my_c):
    _, r, c = g.shape
    tr = _row_tile(r, c)

    def body(c_ref, g_ref, r_ref, o_ref):
        o_ref[...] = (g_ref[...].astype(F32) + r_ref[...].astype(F32)).astype(o_ref.dtype)

    return pl.pallas_call(
        body, name="rs_pair_sum", out_shape=jax.ShapeDtypeStruct((4, r, c), g.dtype),
        grid_spec=pltpu.PrefetchScalarGridSpec(
            num_scalar_prefetch=1, grid=(4, r // tr),
            in_specs=[pl.BlockSpec((1, tr, c), lambda n, i, cr: (2 * n + cr[0], i, 0)),
                      pl.BlockSpec((1, tr, c), lambda n, i, cr: (n, i, 0))],
            out_specs=pl.BlockSpec((1, tr, c), lambda n, i, cr: (n, i, 0))),
        compiler_params=_params(),
    )(my_c, g, recv)


def _chip_sum(h, recv, my_chip):
    _, r, c = h.shape
    tr = _row_tile(r, c)

    def body(j_ref, h_ref, r0_ref, r1_ref, r2_ref, o_ref):
        o_ref[...] = ((h_ref[0].astype(F32) + r0_ref[0].astype(F32)) + r1_ref[0].astype(F32)) + r2_ref[0].astype(F32)

    return pl.pallas_call(
        body, name="rs_chip_sum", out_shape=jax.ShapeDtypeStruct((r, c), F32),
        grid_spec=pltpu.PrefetchScalarGridSpec(
            num_scalar_prefetch=1, grid=(r // tr,),
            in_specs=[pl.BlockSpec((1, tr, c), lambda i, jr: (jr[0], i, 0))]
            + [pl.BlockSpec((1, tr, c), lambda i, jr, n=n: (n, i, 0)) for n in range(3)],
            out_specs=pl.BlockSpec((tr, c), lambda i, jr: (i, 0))),
        compiler_params=_params(),
    )(my_chip, h, recv, recv, recv)


def _sum_blocks(g):
    n, r, c = g.shape

    def body(g_ref, o_ref):
        acc = g_ref[0]
        for k in range(1, n):
            acc = acc + g_ref[k]
        o_ref[...] = acc

    return pl.pallas_call(body, name="sum_blocks", out_shape=jax.ShapeDtypeStruct((r, c), F32), compiler_params=_params())(g)


def _pack(flat_parts, cols, row_multiple, dtype):
    flat = jnp.concatenate([f.astype(dtype) for f in flat_parts])
    per_row_block = cols * row_multiple
    padded = -(-flat.shape[0] // per_row_block) * per_row_block
    return jnp.pad(flat, (0, padded - flat.shape[0])).reshape(padded // cols, cols)


def _shard_shape(name):
    shape, axis = WEIGHTS[name]
    if axis is None:
        return shape
    return tuple(s // N_DEV if a == axis else s for a, s in enumerate(shape))


def _size(shape):
    n = 1
    for s in shape:
        n *= s
    return n


def _unshard(blocks, name):
    _, axis = WEIGHTS[name]
    return jnp.concatenate([blocks[k] for k in range(N_DEV)], axis=axis)


def _unpack_blocks(flat, names):
    out, off = {}, 0
    for n in names:
        shp = _shard_shape(n)
        out[n] = flat[..., off:off + _size(shp)].reshape(flat.shape[:-1] + shp)
        off += _size(shp)
    return out


def _layer_slabs(shard, i):
    j, kind = i // 3, i % 3
    w_out = (shard["gla_w_out"], shard["mla_w_out"], shard["conv_w_out"])[kind][j]
    out = {"a": jnp.concatenate([shard["mlp_w2"][i], shard["mlp_w1"][i].T, w_out, shard["ple_w_gate"][i]], axis=0).astype(BF16),
           "b": shard["ple_w_proj"][i].T.astype(BF16)}
    if kind == 0:
        out["gla"] = shard["gla_w_in"][j].T.astype(BF16)
    elif kind == 1:
        out["mla_in"] = _pad_cols(shard["mla_w_in"][j], MLA_IN_PAD).astype(BF16)
        out["mla_uq"] = _pad_cols(shard["mla_w_uq"][j], MLA_HEAD_PAD).astype(BF16)
        out["mla_ukv"] = shard["mla_w_ukv"][j].astype(BF16)
    else:
        out["conv"] = shard["conv_w_in"][j].T.astype(BF16)
    return out


def _mixer_weights(landed, i):
    kind = i % 3
    if kind == 0:
        return {"gla_w_in_t": landed["gla"].reshape(-1, D_MODEL)}
    if kind == 2:
        return {"conv": landed["conv"]}
    heads_side_by_side = lambda g: g.transpose(1, 0, 2).reshape(g.shape[1], -1)
    return {"mla_in": landed["mla_in"], "mla_w_uq": heads_side_by_side(landed["mla_uq"]),
            "mla_w_ukv": heads_side_by_side(landed["mla_ukv"])}


def _mixer_grad_buffers(layer_grads, i):
    kind = i % 3
    if kind == 0:
        return {"gla": layer_grads["gla_w_in_t"].reshape(N_DEV, -1, D_MODEL).astype(BF16)}
    if kind == 2:
        return {"conv": layer_grads["conv"]}
    head_blocks = lambda g: g.reshape(g.shape[0], N_DEV, -1).transpose(1, 0, 2).astype(BF16)
    return {"mla_in": layer_grads["mla_in"], "mla_uq": head_blocks(layer_grads["mla_uq"]),
            "mla_ukv": head_blocks(layer_grads["mla_ukv"])}


SLAB_KEYS = ("a", "b")


class _Overlap:
    def __init__(self, shard, small_pack):
        mx, my, mc = _my_place()
        self.my_c = mc.astype(jnp.int32).reshape(1)
        self.my_chip = (2 * mx + my).astype(jnp.int32).reshape(1)
        mine = 4 * mx + 2 * my + mc
        def landing_of(slabs):
            return {k: lax.dynamic_update_index_in_dim(lax.empty((N_DEV, *v.shape), v.dtype), v, mine, 0)
                    for k, v in slabs.items()}

        first = _layer_slabs(shard, 0)
        first["small"] = small_pack
        self.landing = [landing_of(first)]
        self.fly = {}
        self.grads = [{} for _ in range(DEPTH)]
        self.reduced = [{} for _ in range(DEPTH)]
        tok = self._gather_first(0, "mixer", shard["ln_g"])
        tok = self._gather_first(0, "slab", tok)
        shard, tok = lax.optimization_barrier((shard, tok))
        self.landing += [landing_of(_layer_slabs(shard, i)) for i in range(1, DEPTH)]
        bufs = self._wait("ag_first_mixer_l0", tok)
        self.landing[0].update(zip(self._keys(self.landing[0], "mixer"),
                                   _exchange("ag_forward_mixer_l0", bufs, _gather_forward_copies(len(bufs)))))

    @staticmethod
    def _keys(names, group):
        return [k for k in names if (k in SLAB_KEYS) == (group == "slab")]

    def _start(self, name, bufs, copies, after):
        sems, bufs, tok = _exchange_start(name + "_start", bufs, copies, after)
        self.fly[name] = (sems, bufs, copies)
        return tok

    def _wait(self, name, after):
        sems, bufs, copies = self.fly.pop(name)
        return _exchange_wait(name + "_wait", sems, bufs, copies, after)

    def mixer_weights(self, i):
        return _mixer_weights(self.landing[i], i)

    def slab_weights(self, i, dep):
        self._gather_done(i, "slab", dep)
        return self.landing[i]["a"], self.landing[i]["b"]

    def slab_grads(self, i, ga, gb):
        self.grads[i].update(a=ga, b=gb)

    def mixer_grads(self, i, layer_grads):
        self.grads[i].update(_mixer_grad_buffers(layer_grads, i))

    def at(self, phase, i, point, dep):
        toks = []
        if phase == "fwd":
            if point == "begin" and i == 0:
                toks.append(self._gather_first(1, "mixer", self.landing[0][self._keys(self.landing[0], "mixer")[0]]))
                toks.append(self._gather_first(1, "slab", toks[-1]))
            if point == "proj_done":
                toks.append(self._gather_forward(i, "slab", dep))
            if point == "mid" and i + 1 < DEPTH:
                toks.append(self._gather_forward(i + 1, "mixer", dep))
                if i + 2 < DEPTH:
                    toks.append(self._gather_first(i + 2, "mixer", dep))
                    toks.append(self._gather_first(i + 2, "slab", toks[-1]))
            if point == "end" and i + 1 < DEPTH:
                self._gather_done(i + 1, "mixer", dep)
        else:
            if point == "begin" and i + 1 < DEPTH:
                toks.append(self._scatter_cores(i + 1, "mixer", dep))
            if point == "ln" and i + 1 < DEPTH:
                toks.append(self._scatter_chips(i + 1, "mixer", dep))
            if point == "slab_done":
                if i + 1 < DEPTH:
                    self._scatter_done(i + 1, "slab", dep)
                    self._scatter_done(i + 1, "mixer", dep)
                toks.append(self._scatter_cores(i, "slab", dep))
            if point == "mixer_done":
                toks.append(self._scatter_chips(i, "slab", dep))
        return toks or None

    def _gather_first(self, i, group, after):
        bufs = [self.landing[i][k] for k in self._keys(self.landing[i], group)]
        return self._start(f"ag_first_{group}_l{i}", bufs, _gather_first_copies(len(bufs)), after)

    def _gather_forward(self, i, group, after):
        bufs = self._wait(f"ag_first_{group}_l{i}", after)
        return self._start(f"ag_forward_{group}_l{i}", bufs, _gather_forward_copies(len(bufs)), after)

    def _gather_done(self, i, group, after):
        keys = self._keys(self.landing[i], group)
        self.landing[i].update(zip(keys, self._wait(f"ag_forward_{group}_l{i}", after)))

    def _scatter_cores(self, i, group, after):
        gs = [self.grads[i][k] for k in self._keys(self.grads[i], group)]
        land = [lax.empty((4, *g.shape[1:]), g.dtype) for g in gs]
        return self._start(f"rs_cores_{group}_l{i}", gs + land, _scatter_core_copies(len(gs)), after)

    def _pair_sums(self, bufs):
        n = len(bufs) // 2
        hs = [_pair_sum(g, r, self.my_c) for g, r in zip(bufs[:n], bufs[n:])]
        return hs + [lax.empty((3, *h.shape[1:]), h.dtype) for h in hs]

    def _scatter_chips(self, i, group, after):
        bufs = self._pair_sums(self._wait(f"rs_cores_{group}_l{i}", after))
        return self._start(f"rs_chips_{group}_l{i}", bufs, _scatter_chip_copies(len(bufs) // 2), after)

    def _chip_sums(self, i, group, bufs):
        n = len(bufs) // 2
        for k, h, r in zip(self._keys(self.grads[i], group), bufs[:n], bufs[n:]):
            self.reduced[i][k] = _chip_sum(h, r, self.my_chip)

    def _scatter_done(self, i, group, after):
        self._chip_sums(i, group, self._wait(f"rs_chips_{group}_l{i}", after))

    def tail_begin(self, dep):
        return self._scatter_cores(0, "mixer", dep)

    def tail_middle(self, dep):
        self._scatter_done(0, "slab", dep)
        return self._scatter_chips(0, "mixer", dep)

    def tail_end(self, dep):
        self._scatter_done(0, "mixer", dep)


def _small_gather_start(x, name, after):
    mx, my, mc = _my_place()
    land = lax.dynamic_update_index_in_dim(lax.empty((N_DEV, *x.shape), x.dtype), x, 4 * mx + 2 * my + mc, 0)
    return name, _exchange_start(name + "_first_start", [land], _gather_first_copies(1), after)


def _small_gather_finish(started, after):
    name, (sems, bufs, _) = started
    bufs = _exchange_wait(name + "_first_wait", sems, bufs, _gather_first_copies(1), after)
    return _exchange(name + "_forward", bufs, _gather_forward_copies(1))[0]


def _adamw_math(w, g, m, v):
    m2 = ADAM_B1 * m + (1.0 - ADAM_B1) * g
    v2 = ADAM_B2 * v + (1.0 - ADAM_B2) * (g * g)
    m_hat = m2 / (1.0 - ADAM_B1 ** ADAM_STEP)
    v_hat = v2 / (1.0 - ADAM_B2 ** ADAM_STEP)
    return -ADAM_LR * (m_hat / (jnp.sqrt(v_hat) + ADAM_EPS) + ADAM_WD * w), m2, v2


ADAMW_TILE_BYTES = 1024 * 1024


def _adamw_layer(name, w, m, v, j, g, g_at, transposed, chain, after):
    n_layers, r, c = w.shape
    tr = max(t for t in range(8, r + 1, 8) if r % t == 0 and (t * c * 4 <= ADAMW_TILE_BYTES or t == 8))
    rb, rows = g_at
    if transposed:
        assert rows == c and g.shape[1] == r, (name, g.shape, g_at)
        g_spec = pl.BlockSpec((rows, tr), lambda i: (rb, i))
    else:
        assert rows == r and g.shape[1] == c, (name, g.shape, g_at)
        g_spec = pl.BlockSpec((tr, c), lambda i: (rb * (r // tr) + i, 0))
    extra = list(chain or []) + [a for a in (after or []) if a is not None]
    n_chain = 4 if chain else 0

    def body(w_ref, m_ref, v_ref, g_ref, *rest):
        g_out, d_out, m_out, v_out, tok_ref = rest[len(extra):]
        gv = g_ref[...].T if transposed else g_ref[...]
        g_out[0] = gv
        d_out[0], m_out[0], v_out[0] = _adamw_math(w_ref[0], gv, m_ref[0], v_ref[0])
        tok_ref[...] = jnp.zeros_like(tok_ref)

    layer_spec = pl.BlockSpec((1, tr, c), lambda i: (j, i, 0))
    outs = pl.pallas_call(
        body, name=f"adamw_{name}_l{j}", grid=(r // tr,),
        in_specs=[layer_spec] * 3 + [g_spec] + [pl.BlockSpec(memory_space=pl.ANY)] * len(extra),
        out_specs=[layer_spec] * 4 + [pl.BlockSpec((8, LANES), lambda i: (0, 0))],
        out_shape=[jax.ShapeDtypeStruct(w.shape, F32)] * 4 + [jax.ShapeDtypeStruct((8, LANES), F32)],
        input_output_aliases={4 + k: k for k in range(n_chain)}, compiler_params=_params(),
    )(w, m, v, g, *extra)
    return list(outs[:4]), outs[4]


def _adamw(w, g, m, v, name):
    shape = w.shape
    cols = shape[-1]
    rows = _size(shape) // cols
    tr = rows
    for cand in (512, 256, 128, 64, 32, 16, 8):
        if rows > cand and rows % cand == 0:
            tr = cand
            break

    def body(w_ref, g_ref, m_ref, v_ref, d_ref, mo_ref, vo_ref):
        d_ref[...], mo_ref[...], vo_ref[...] = _adamw_math(w_ref[...], g_ref[...], m_ref[...], v_ref[...])

    spec = pl.BlockSpec((tr, cols), lambda i: (i, 0))
    outs = pl.pallas_call(
        body, name="adamw_" + name, grid=(rows // tr,), in_specs=[spec] * 4, out_specs=[spec] * 3,
        out_shape=[jax.ShapeDtypeStruct((rows, cols), F32)] * 3, compiler_params=_params(),
    )(*[a.reshape(rows, cols) for a in (w, g, m, v)])
    return [o.reshape(shape) for o in outs]


def kernel(x, p, positions, gla_w_in, gla_w_gate_up, gla_b_gate, gla_norm_g, gla_w_out, mla_w_in, mla_q_norm, mla_kv_norm, mla_w_uq, mla_w_ukv, mla_w_out, conv_w_in, conv_w, conv_w_out, ln_g, ln_b, mlp_w1, mlp_w2, ple_w_gate, ple_w_proj, loss_target, m_gla_w_in, m_gla_w_gate_up, m_gla_b_gate, m_gla_norm_g, m_gla_w_out, m_mla_w_in, m_mla_q_norm, m_mla_kv_norm, m_mla_w_uq, m_mla_w_ukv, m_mla_w_out, m_conv_w_in, m_conv_w, m_conv_w_out, m_ln_g, m_ln_b, m_mlp_w1, m_mlp_w2, m_ple_w_gate, m_ple_w_proj, v_gla_w_in, v_gla_w_gate_up, v_gla_b_gate, v_gla_norm_g, v_gla_w_out, v_mla_w_in, v_mla_q_norm, v_mla_kv_norm, v_mla_w_uq, v_mla_w_ukv, v_mla_w_out, v_conv_w_in, v_conv_w, v_conv_w_out, v_ln_g, v_ln_b, v_mlp_w1, v_mlp_w2, v_ple_w_gate, v_ple_w_proj):
    args = locals()
    shard = {n: args[n] for n in WEIGHT_NAMES}
    mom = {n: args["m_" + n] for n in WEIGHT_NAMES}
    var = {n: args["v_" + n] for n in WEIGHT_NAMES}
    mx, my, mc = _my_place()

    comm = _Overlap(shard, _pack([shard[n].reshape(-1) for n in SMALL], LANES, 8, F32))
    small_all = comm.landing[0]["small"]
    small = {n: shard[n] for n in REPLICATED}
    small.update({n: _unshard(blk, n) for n, blk in _unpack_blocks(small_all.reshape(N_DEV, -1), SMALL).items()})
    loss_part, grad_x, small_grads = _step(x[0], p[:, 0], positions[0], loss_target[0], small, comm)

    chains = {}

    def update(name, j, g, g_at, transposed, tok):
        chains[name], tok = _adamw_layer(name, shard[name], mom[name], var[name], j, g, g_at, transposed,
                                         chains.get(name), [tok])
        return tok

    def update_layer(i, groups, tok):
        j, kind = i // 3, i % 3
        red = comm.reduced[i]
        if "slab" in groups:
            tok = update("mlp_w2", i, red["a"], REG_W2, False, tok)
            tok = update("mlp_w1", i, red["a"], REG_W1T, True, tok)
            tok = update(("gla_w_out", "mla_w_out", "conv_w_out")[kind], j, red["a"], REG_WOUT, False, tok)
            tok = update("ple_w_gate", i, red["a"], REG_WG, False, tok)
            tok = update("ple_w_proj", i, red["b"], REG_WPT, True, tok)
        if "mixer" in groups:
            if kind == 0:
                tok = update("gla_w_in", j, red["gla"].T, (0, D_MODEL), False, tok)
            elif kind == 2:
                tok = update("conv_w_in", j, red["conv"], REG_CONV, True, tok)
            else:
                for n, g in (("mla_w_in", red["mla_in"][:, :MLA_IN]), ("mla_w_ukv", red["mla_ukv"]),
                             ("mla_w_uq", red["mla_uq"][:, :MLA_NOPE + MLA_ROPE])):
                    tok = update(n, j, g, (0, g.shape[0]), False, tok)
        return tok

    tok = comm.tail_begin(grad_x)
    tok = update_layer(3, ("slab", "mixer"), tok)
    tok = update_layer(2, ("slab", "mixer"), tok)
    tok = comm.tail_middle(tok)
    small_parts = [loss_part[0, :1]] + [small_grads[n].reshape(-1) for n in SMALL + REPLICATED]
    small_gather = _small_gather_start(_pack(small_parts, LANES, 8, F32), "ag_small_grads", tok)
    tok = update_layer(1, ("slab", "mixer"), small_gather[1][2])
    tok = update_layer(0, ("slab",), tok)
    comm.tail_end(tok)
    tok = update_layer(0, ("mixer",), tok)
    red_small = _sum_blocks(_small_gather_finish(small_gather, tok)).reshape(-1)
    loss = red_small[0]
    off = 1
    dev = 4 * mx + 2 * my + mc
    for n in SMALL + REPLICATED:
        shape, axis = WEIGHTS[n]
        full_g = red_small[off:off + _size(shape)].reshape(shape)
        off += _size(shape)
        if axis is not None:
            width = shape[axis] // N_DEV
            full_g = lax.dynamic_slice_in_dim(full_g, dev * width, width, axis=axis)
        chains[n] = [full_g, *_adamw(shard[n], full_g, mom[n], var[n], n)]
    return (loss, grad_x[None], *[chains[n][k] for k in range(4) for n in WEIGHT_NAMES])
```

```python
import functools

import jax
import jax.numpy as jnp
from jax import lax
from jax.experimental import pallas as pl
from jax.experimental.pallas import tpu as pltpu

F32, BF16 = jnp.float32, jnp.bfloat16
N_DEV = 8

D_MODEL = 1024
DEPTH = 4
CHUNK = 64
ALPHA = (2 * DEPTH) ** 0.25
LN_EPS = 1e-5
RMS_EPS = 1e-6
PLE_DIM = 256
D_FF = 4 * D_MODEL
GLA_HEADS = 4
GLA_DK = 128
GLA_DV = 256
GLA_RANK = 16
GLA_TAU = 16.0
GLA_HK = GLA_HEADS * GLA_DK
GLA_HV = GLA_HEADS * GLA_DV
GLA_MAIN = 2 * GLA_HK + GLA_HV + D_MODEL
MLA_HEADS = 8
MLA_NOPE = 128
MLA_ROPE = 64
MLA_V = 128
MLA_RANK = 256
MLA_IN = 2 * MLA_RANK + MLA_ROPE
MLA_IN_PAD = 640
ROPE_BASE = 10000.0
LANES = 128
ADAM_LR, ADAM_B1, ADAM_B2, ADAM_EPS, ADAM_WD, ADAM_STEP = 0.001, 0.9, 0.999, 1e-08, 0.01, 10

V7X_VMEM_LIMIT_BYTES = 56 * 1024 * 1024

WEIGHTS = {
    "gla_w_in": ((2, 1024, 3088), 2), "gla_w_gate_up": ((2, 16, 512), 2), "gla_b_gate": ((2, 512), 1),
    "gla_norm_g": ((2, 256), 1), "gla_w_out": ((2, 1024, 1024), 1), "mla_w_in": ((1, 1024, 576), 1),
    "mla_q_norm": ((1, 256), None), "mla_kv_norm": ((1, 256), None), "mla_w_uq": ((1, 256, 1536), 2),
    "mla_w_ukv": ((1, 256, 2048), 2), "mla_w_out": ((1, 1024, 1024), 1), "conv_w_in": ((1, 1024, 3072), 2),
    "conv_w": ((1, 3, 1024), 2), "conv_w_out": ((1, 1024, 1024), 1), "ln_g": ((4, 2, 1024), 2),
    "ln_b": ((4, 2, 1024), 2), "mlp_w1": ((4, 1024, 4096), 2), "mlp_w2": ((4, 4096, 1024), 1),
    "ple_w_gate": ((4, 1024, 1024), 1), "ple_w_proj": ((4, 256, 1024), 2),
}
WEIGHT_NAMES = list(WEIGHTS)
REG_W2, REG_W1T, REG_WOUT, REG_WG = (0, 512), (1, 512), (8, 128), (9, 128)
A_ROWS = 1280
REG_CONV = (0, 384)
REG_WPT = (0, 128)
REG_MLA_IN = (0, 128)
MLA_HEAD_PAD = 2 * LANES
SMALL = ["gla_w_gate_up", "gla_b_gate", "gla_norm_g", "conv_w", "ln_g", "ln_b"]
REPLICATED = ["mla_q_norm", "mla_kv_norm"]


def _params(**kw):
    return pltpu.CompilerParams(vmem_limit_bytes=V7X_VMEM_LIMIT_BYTES, **kw)


def _dot(a, b, ca, cb):
    return lax.dot_general(a, b, (((ca,), (cb,)), ((), ())), preferred_element_type=F32)


def _nn(a, b):
    return _dot(a.astype(BF16), b.astype(BF16), 1, 0)


def _nt(a, b):
    return _dot(a.astype(BF16), b.astype(BF16), 1, 1)


def _tn(a, b):
    return _dot(a.astype(BF16), b.astype(BF16), 0, 0)


@jax.custom_vjp
def mm_nn(a, b):
    return _nn(a, b)


def _mm_nn_fwd(a, b):
    return _nn(a, b), (a, b)


def _mm_nn_bwd(res, g):
    a, b = res
    return _nt(g, b).astype(a.dtype), _tn(a, g).astype(b.dtype)


mm_nn.defvjp(_mm_nn_fwd, _mm_nn_bwd)


@jax.custom_vjp
def mm_nt(a, b):
    return _nt(a, b)


def _mm_nt_fwd(a, b):
    return _nt(a, b), (a, b)


def _mm_nt_bwd(res, g):
    a, b = res
    return _nn(g, b).astype(a.dtype), _tn(g, a).astype(b.dtype)


mm_nt.defvjp(_mm_nt_fwd, _mm_nt_bwd)


@jax.custom_vjp
def mm_tn(a, b):
    return _tn(a, b)


def _mm_tn_fwd(a, b):
    return _tn(a, b), (a, b)


def _mm_tn_bwd(res, g):
    a, b = res
    return _nt(b, g).astype(a.dtype), _nn(a, g).astype(b.dtype)


mm_tn.defvjp(_mm_tn_fwd, _mm_tn_bwd)


def _iota2(shape, dim):
    return lax.broadcasted_iota(jnp.int32, shape, dim)


def _split3(x):
    hi = x.astype(BF16)
    rest = x - hi.astype(F32)
    mid = rest.astype(BF16)
    return hi, mid, (rest - mid.astype(F32)).astype(BF16)


def _tri_dot(tri, x):
    return sum(_dot(tri.astype(BF16), piece, 1, 0) for piece in _split3(x))


@jax.custom_vjp
def cumsum_rows(x):
    n = x.shape[0]
    return _tri_dot(_iota2((n, n), 0) >= _iota2((n, n), 1), x)


def _cumsum_fwd(x):
    return cumsum_rows(x), None


def _cumsum_bwd(_, g):
    n = g.shape[0]
    return (_tri_dot(_iota2((n, n), 0) <= _iota2((n, n), 1), g),)


cumsum_rows.defvjp(_cumsum_fwd, _cumsum_bwd)


def _rot_matrix(transposed):
    i, j = _iota2((LANES, LANES), 0), _iota2((LANES, LANES), 1)
    if transposed:
        i, j = j, i
    half = MLA_ROPE // 2
    plus = (i == j - half) & (j >= half) & (j < MLA_ROPE)
    minus = (i == j + half) & (j < half)
    return (plus.astype(F32) - minus.astype(F32)).astype(BF16)


def _rot_dot(x, transposed):
    return sum(_dot(piece, _rot_matrix(transposed), 1, 0) for piece in _split3(x))


@jax.custom_vjp
def rot_half(x):
    return _rot_dot(x, False)


def _rot_fwd(x):
    return rot_half(x), None


def _rot_bwd(_, g):
    return (_rot_dot(g, True),)


rot_half.defvjp(_rot_fwd, _rot_bwd)


def _shift_rows_raw(x, s):
    n = x.shape[0]
    row = _iota2(x.shape, 0)
    rolled = pltpu.roll(x, s % n, 0)
    keep = (row >= s) if s > 0 else (row < n + s)
    return jnp.where(keep, rolled, 0.0)


@functools.partial(jax.custom_vjp, nondiff_argnums=(1,))
def shift_rows(x, s):
    return _shift_rows_raw(x, s)


def _shift_fwd(x, s):
    return _shift_rows_raw(x, s), None


def _shift_bwd(s, _, g):
    return (_shift_rows_raw(g, -s),)


shift_rows.defvjp(_shift_fwd, _shift_bwd)


def _layer_norm(a, g, b):
    mu = jnp.mean(a, -1, keepdims=True)
    xc = a - mu
    var = jnp.mean(xc * xc, -1, keepdims=True)
    return xc * lax.rsqrt(var + LN_EPS) * g + b


def _rms_norm(a, g):
    return a * lax.rsqrt(jnp.mean(a * a, -1, keepdims=True) + RMS_EPS) * g


def _log_sigmoid(z):
    return jnp.minimum(z, 0.0) - jnp.log(1.0 + jnp.exp(-jnp.abs(z)))


def _matmul(a, b, *, name, ta=False, tb=False, tm=512, tn=512, a_fn=None, epi=None, epi_ins=(), out_dtypes=(BF16,),
            b_at=None, out_at=None, out_buf=None, after=None, n_row_sums=0, a_ins=(), a_out_dtypes=()):
    m = a.shape[1] if ta else a.shape[0]
    k = a.shape[0] if ta else a.shape[1]
    if b_at is None:
        n, kb = (b.shape[0], b.shape[1]) if tb else (b.shape[1], b.shape[0])
    else:
        rb, r = b_at
        n, kb = (N_DEV * r, b.shape[2]) if tb else (b.shape[2], N_DEV * r)
    assert kb == k, (name, a.shape, b.shape, k, kb)
    tm, tn = min(tm, m), min(tn, n)
    assert m % tm == 0 and n % tn == 0, (name, m, n, tm, tn)
    a_spec = pl.BlockSpec((k, tm), lambda i, j: (0, i)) if ta else pl.BlockSpec((tm, k), lambda i, j: (i, 0))
    if b_at is None:
        b_spec = pl.BlockSpec((tn, k), lambda i, j: (j, 0)) if tb else pl.BlockSpec((k, tn), lambda i, j: (0, j))
        load_b = lambda ref: ref[...]
    elif tb and tn == n:
        b_spec = pl.BlockSpec((N_DEV, r, k), lambda i, j: (0, rb, 0))
        load_b = lambda ref: ref[...].reshape(n, k)
    elif tb:
        assert tn == r, (name, tn, r)
        b_spec = pl.BlockSpec((1, r, k), lambda i, j: (j, rb, 0))
        load_b = lambda ref: ref[0]
    else:
        b_spec = pl.BlockSpec((N_DEV, r, tn), lambda i, j: (0, rb, j))
        load_b = lambda ref: ref[...].reshape(k, tn)
    e_specs = []
    for e in epi_ins:
        if e.shape == (1, n):
            e_specs.append(pl.BlockSpec((1, tn), lambda i, j: (0, j)))
        else:
            assert e.shape == (m, n), (name, e.shape, m, n)
            e_specs.append(pl.BlockSpec((tm, tn), lambda i, j: (i, j)))
    n_epi, n_ain, n_aout = len(epi_ins), len(a_ins), len(a_out_dtypes)
    assert n_aout == 0 or (tn == n and not ta and out_at is None), name
    ca, cb = (0 if ta else 1), (1 if tb else 0)
    operands = [a, b, *a_ins, *epi_ins]
    in_specs = [a_spec, b_spec, *[a_spec] * n_ain, *e_specs]
    if out_at is None:
        assert n_row_sums == 0 or tn == n, (name, tn, n)
        out_specs = [pl.BlockSpec((tm, tn), lambda i, j: (i, j)) for _ in out_dtypes]
        out_specs += [pl.BlockSpec((tm, k), lambda i, j: (i, 0))] * n_aout
        out_specs += [pl.BlockSpec((1, n), lambda i, j: (0, 0))] * n_row_sums
        out_shape = [jax.ShapeDtypeStruct((m, n), dt) for dt in out_dtypes]
        out_shape += [jax.ShapeDtypeStruct((m, k), dt) for dt in a_out_dtypes]
        out_shape += [jax.ShapeDtypeStruct((1, n), F32)] * n_row_sums
        aliases, n_buf = {}, 0
    else:
        orb, orows = out_at
        assert len(out_dtypes) == 1 and m == N_DEV * orows and n == out_buf.shape[2], (name, m, n)
        if tm > orows:
            assert tm % orows == 0, (name, tm, orows)
            out_specs = [pl.BlockSpec((tm // orows, orows, tn), lambda i, j: (i, orb, j))]
        else:
            per = orows // tm
            out_specs = [pl.BlockSpec((1, tm, tn), lambda i, j: (i // per, orb * per + i % per, j))]
        out_shape = [jax.ShapeDtypeStruct(out_buf.shape, out_buf.dtype)]
        operands.append(out_buf)
        in_specs.append(pl.BlockSpec(memory_space=pl.ANY))
        aliases, n_buf = {len(operands) - 1: 0}, 1
    for dep in ([] if after is None else after if isinstance(after, (list, tuple)) else [after]):
        if dep is not None:
            operands.append(dep)
            in_specs.append(pl.BlockSpec(memory_space=pl.ANY))
            n_buf += 1

    def body(a_ref, b_ref, *rest):
        av, a_outs = a_ref[...], ()
        if a_fn is not None:
            av = a_fn(av, *[r_[...] for r_ in rest[:n_ain]])
            if n_aout:
                av, *a_outs = av
        acc = _dot(av.astype(BF16), load_b(b_ref).astype(BF16), ca, cb)
        outs = epi(acc, *[r_[...] for r_ in rest[n_ain:n_ain + n_epi]]) if epi is not None else (acc,)
        o_refs = rest[n_ain + n_epi + n_buf:]
        n_tiles = len(o_refs) - n_row_sums - n_aout
        for o_ref, val in zip(o_refs[:n_tiles + n_aout], (*outs[:n_tiles], *a_outs)):
            o_ref[...] = val.astype(o_ref.dtype).reshape(o_ref.shape)
        if n_row_sums:
            @pl.when(pl.program_id(0) == 0)
            def _():
                for o_ref in o_refs[n_tiles + n_aout:]:
                    o_ref[...] = jnp.zeros_like(o_ref)

            for o_ref, val in zip(o_refs[n_tiles + n_aout:], outs[n_tiles:]):
                o_ref[...] += val

    outs = pl.pallas_call(
        body, name=name, grid=(m // tm, n // tn), in_specs=in_specs, out_specs=out_specs, out_shape=out_shape,
        input_output_aliases=aliases, compiler_params=_params(),
    )(*operands)
    return outs[0] if len(outs) == 1 else tuple(outs)


def _tile_fwd(f, tiled, params, out_dtypes, *, tm, name):
    t = tiled[0].shape[0]
    assert t % tm == 0
    out_avals = jax.eval_shape(f, *[jax.ShapeDtypeStruct((tm, x.shape[1]), F32) for x in tiled],
                               *[jax.ShapeDtypeStruct(p.shape, F32) for p in params])
    nt, npar = len(tiled), len(params)

    def body(*refs):
        ins = [r[...].astype(F32) for r in refs[:nt + npar]]
        outs = f(*ins)
        for o_ref, val in zip(refs[nt + npar:], outs):
            o_ref[...] = val.astype(o_ref.dtype)

    return pl.pallas_call(
        body, name=name, grid=(t // tm,),
        in_specs=[pl.BlockSpec((tm, x.shape[1]), lambda i: (i, 0)) for x in tiled]
        + [pl.BlockSpec(p.shape, lambda i: (0, 0)) for p in params],
        out_specs=[pl.BlockSpec((tm, o.shape[1]), lambda i: (i, 0)) for o in out_avals],
        out_shape=[jax.ShapeDtypeStruct((t, o.shape[1]), dt) for o, dt in zip(out_avals, out_dtypes)],
        compiler_params=_params(),
    )(*tiled, *params)


def _tile_bwd(f, tiled, params, cots, d_tiled_dtypes, *, tm, name, diff_tiled=None):
    t = tiled[0].shape[0]
    assert t % tm == 0
    nt, npar, nc = len(tiled), len(params), len(cots)
    diff_tiled = list(range(nt)) if diff_tiled is None else diff_tiled

    def body(*refs):
        ins = [r[...].astype(F32) for r in refs[:nt + npar]]
        cts = [r[...].astype(F32) for r in refs[nt + npar:nt + npar + nc]]
        o_refs = refs[nt + npar + nc:]
        _, vjp = jax.vjp(f, *ins)
        grads = vjp(tuple(cts))
        for o_ref, idx in zip(o_refs[:len(diff_tiled)], diff_tiled):
            o_ref[...] = grads[idx].astype(o_ref.dtype)
        p_refs = o_refs[len(diff_tiled):]

        @pl.when(pl.program_id(0) == 0)
        def _():
            for p_ref in p_refs:
                p_ref[...] = jnp.zeros_like(p_ref)

        for p_ref, gp in zip(p_refs, grads[nt:]):
            p_ref[...] += gp

    outs = pl.pallas_call(
        body, name=name, grid=(t // tm,),
        in_specs=[pl.BlockSpec((tm, x.shape[1]), lambda i: (i, 0)) for x in tiled]
        + [pl.BlockSpec(p.shape, lambda i: (0, 0)) for p in params]
        + [pl.BlockSpec((tm, c.shape[1]), lambda i: (i, 0)) for c in cots],
        out_specs=[pl.BlockSpec((tm, tiled[idx].shape[1]), lambda i: (i, 0)) for idx in diff_tiled]
        + [pl.BlockSpec(p.shape, lambda i: (0, 0)) for p in params],
        out_shape=[jax.ShapeDtypeStruct(tiled[idx].shape, dt) for idx, dt in zip(diff_tiled, d_tiled_dtypes)]
        + [jax.ShapeDtypeStruct(p.shape, F32) for p in params],
        compiler_params=_params(),
    )(*tiled, *params, *cots)
    return outs[:len(diff_tiled)], outs[len(diff_tiled):]


def _gla_head(q, k, v, r, z, g, st):
    c = q.shape[0]
    causal = _iota2((c, c), 0) >= _iota2((c, c), 1)
    la = _log_sigmoid(z) * (1.0 / GLA_TAU)
    big_l = cumsum_rows(la)
    ep, en = jnp.exp(big_l), jnp.exp(-big_l)
    qs = q * (GLA_DK ** -0.5)
    qp = qs * ep
    s = jnp.where(causal, mm_nt(qp, k * en), mm_nt(qs * en, k * ep))
    o = mm_nn(s, v) + mm_nt(qp, st)
    l_end = jnp.sum(la, axis=0, keepdims=True)
    st_new = st * jnp.exp(l_end) + mm_tn(v, k * jnp.exp(l_end - big_l))
    u = _rms_norm(o, g) * (r * jax.nn.sigmoid(r))
    return u, st_new


def _gla_slices(h):
    q = slice(GLA_DK * h, GLA_DK * (h + 1))
    k = slice(GLA_HK + GLA_DK * h, GLA_HK + GLA_DK * (h + 1))
    v = slice(2 * GLA_HK + GLA_DV * h, 2 * GLA_HK + GLA_DV * (h + 1))
    r = slice(2 * GLA_HK + GLA_HV + GLA_DV * h, 2 * GLA_HK + GLA_HV + GLA_DV * (h + 1))
    return q, k, v, r


GLA_CHUNKS_PER_STEP = 4


def _gla_fwd(proj, z, norm_g, after):
    t = proj.shape[0]
    nc, per = t // CHUNK, GLA_CHUNKS_PER_STEP
    rows_per_step = per * CHUNK
    after = [a for a in after if a is not None]

    def body(proj_ref, z_ref, g_ref, *rest):
        u_ref, st_save_ref, st_ref = rest[len(after):]

        @pl.when(pl.program_id(0) == 0)
        def _():
            st_ref[...] = jnp.zeros_like(st_ref)

        g = g_ref[...]
        for h in range(GLA_HEADS):
            sq, sk, sv, sr = _gla_slices(h)
            st = st_ref[h]
            for c in range(per):
                rows = slice(c * CHUNK, (c + 1) * CHUNK)
                st_save_ref[c, h] = st
                u, st = _gla_head(proj_ref[rows, sq].astype(F32), proj_ref[rows, sk].astype(F32),
                                  proj_ref[rows, sv].astype(F32), proj_ref[rows, sr].astype(F32),
                                  z_ref[rows, GLA_DK * h:GLA_DK * (h + 1)], g, st)
                u_ref[rows, GLA_DV * h:GLA_DV * (h + 1)] = u.astype(u_ref.dtype)
            st_ref[h] = st

    return pl.pallas_call(
        body, name="gla_fwd", grid=(nc // per,),
        in_specs=[pl.BlockSpec((rows_per_step, GLA_MAIN), lambda i: (i, 0)),
                  pl.BlockSpec((rows_per_step, GLA_HK), lambda i: (i, 0)), pl.BlockSpec((1, GLA_DV), lambda i: (0, 0))]
        + [pl.BlockSpec(memory_space=pl.ANY)] * len(after),
        out_specs=[pl.BlockSpec((rows_per_step, GLA_HV), lambda i: (i, 0)),
                   pl.BlockSpec((per, GLA_HEADS, GLA_DV, GLA_DK), lambda i: (i, 0, 0, 0))],
        out_shape=[jax.ShapeDtypeStruct((t, GLA_HV), BF16), jax.ShapeDtypeStruct((nc, GLA_HEADS, GLA_DV, GLA_DK), F32)],
        scratch_shapes=[pltpu.VMEM((GLA_HEADS, GLA_DV, GLA_DK), F32)],
        compiler_params=_params(),
    )(proj, z, norm_g, *after)


def _gla_bwd(proj, z, norm_g, states, du, after):
    t = proj.shape[0]
    nc, per = t // CHUNK, GLA_CHUNKS_PER_STEP
    rows_per_step = per * CHUNK
    n_steps = nc // per
    after = [a for a in after if a is not None]

    def body(proj_ref, z_ref, g_ref, st_in_ref, du_ref, *rest):
        dproj_ref, dz_ref, dg_ref, dzsum_ref, dst_ref = rest[len(after):]

        @pl.when(pl.program_id(0) == 0)
        def _():
            dst_ref[...] = jnp.zeros_like(dst_ref)
            dg_ref[...] = jnp.zeros_like(dg_ref)
            dzsum_ref[...] = jnp.zeros_like(dzsum_ref)

        g = g_ref[...]
        for h in range(GLA_HEADS):
            sq, sk, sv, sr = _gla_slices(h)
            dst = dst_ref[h]
            for c in reversed(range(per)):
                rows = slice(c * CHUNK, (c + 1) * CHUNK)
                ins = (proj_ref[rows, sq].astype(F32), proj_ref[rows, sk].astype(F32), proj_ref[rows, sv].astype(F32),
                       proj_ref[rows, sr].astype(F32), z_ref[rows, GLA_DK * h:GLA_DK * (h + 1)], g, st_in_ref[c, h])
                _, vjp = jax.vjp(_gla_head, *ins)
                dq, dk, dv, dr, dz, dg, dst = vjp((du_ref[rows, GLA_DV * h:GLA_DV * (h + 1)], dst))
                dproj_ref[rows, sq] = dq.astype(dproj_ref.dtype)
                dproj_ref[rows, sk] = dk.astype(dproj_ref.dtype)
                dproj_ref[rows, sv] = dv.astype(dproj_ref.dtype)
                dproj_ref[rows, sr] = dr.astype(dproj_ref.dtype)
                dz_ref[rows, GLA_DK * h:GLA_DK * (h + 1)] = dz
                dzsum_ref[:, GLA_DK * h:GLA_DK * (h + 1)] += jnp.sum(dz, axis=0, keepdims=True)
                dg_ref[...] += dg
            dst_ref[h] = dst

    rev = lambda i: (n_steps - 1 - i, 0)
    return pl.pallas_call(
        body, name="gla_bwd", grid=(n_steps,),
        in_specs=[pl.BlockSpec((rows_per_step, GLA_MAIN), rev), pl.BlockSpec((rows_per_step, GLA_HK), rev),
                  pl.BlockSpec((1, GLA_DV), lambda i: (0, 0)),
                  pl.BlockSpec((per, GLA_HEADS, GLA_DV, GLA_DK), lambda i: (n_steps - 1 - i, 0, 0, 0)),
                  pl.BlockSpec((rows_per_step, GLA_HV), rev)] + [pl.BlockSpec(memory_space=pl.ANY)] * len(after),
        out_specs=[pl.BlockSpec((rows_per_step, GLA_MAIN), rev), pl.BlockSpec((rows_per_step, GLA_HK), rev),
                   pl.BlockSpec((1, GLA_DV), lambda i: (0, 0)), pl.BlockSpec((1, GLA_HK), lambda i: (0, 0))],
        out_shape=[jax.ShapeDtypeStruct((t, GLA_MAIN), BF16), jax.ShapeDtypeStruct((t, GLA_HK), F32),
                   jax.ShapeDtypeStruct((1, GLA_DV), F32), jax.ShapeDtypeStruct((1, GLA_HK), F32)],
        scratch_shapes=[pltpu.VMEM((GLA_HEADS, GLA_DV, GLA_DK), F32)],
        compiler_params=_params(),
    )(proj, z, norm_g, states, du, *after)


def _mla_pre(cq, cos, sin, gq, gkv, w_uq, w_ukv):
    qlat = _rms_norm(cq[:, :MLA_RANK], gq)
    kvlat = _rms_norm(cq[:, MLA_RANK:2 * MLA_RANK], gkv)
    kr = cq[:, 2 * MLA_RANK:]
    q = mm_nn(qlat, w_uq) * ((MLA_NOPE + MLA_ROPE) ** -0.5)
    kv = mm_nn(kvlat, w_ukv)
    pieces = []
    for h in range(MLA_HEADS):
        qr = q[:, MLA_HEAD_PAD * h + MLA_NOPE:MLA_HEAD_PAD * (h + 1)]
        pieces += [q[:, MLA_HEAD_PAD * h:MLA_HEAD_PAD * h + MLA_NOPE], qr * cos + rot_half(qr) * sin]
    return jnp.concatenate(pieces, axis=1), kv, kr * cos + rot_half(kr) * sin


MLA_Q_TILE = 512


def _mla_attn_block(qn, qr, kv, kr, q0):
    tq, nk = qn.shape[0], kv.shape[0]
    s = mm_nt(qn, kv[:, :MLA_NOPE]) + mm_nt(qr, kr)
    visible = (_iota2((tq, nk), 1) // CHUNK) <= ((q0 + _iota2((tq, nk), 0)) // CHUNK)
    s = jnp.where(visible, s, -1e30)
    e = jnp.exp(s - jnp.max(s, -1, keepdims=True))
    p = e / jnp.sum(e, -1, keepdims=True)
    return mm_nn(p, kv[:, MLA_NOPE:])


def _mla_attn_fwd(q, kv, kr, after):
    t = q.shape[0]
    after = [a for a in after if a is not None]

    def body(q_ref, kv_ref, kr_ref, *rest):
        (o_ref,) = rest[len(after):]
        for i in range(t // MLA_Q_TILE):
            rows = slice(i * MLA_Q_TILE, (i + 1) * MLA_Q_TILE)
            keys = slice(0, (i + 1) * MLA_Q_TILE)
            o = _mla_attn_block(q_ref[rows, :MLA_NOPE].astype(F32), q_ref[rows, MLA_NOPE:].astype(F32),
                                kv_ref[keys, :].astype(F32), kr_ref[keys, :].astype(F32), i * MLA_Q_TILE)
            o_ref[rows, :] = o.astype(o_ref.dtype)

    return pl.pallas_call(
        body, name="mla_attn_fwd", grid=(MLA_HEADS,),
        in_specs=[pl.BlockSpec((t, MLA_HEAD_PAD), lambda h: (0, h)),
                  pl.BlockSpec((t, MLA_NOPE + MLA_V), lambda h: (0, h)), pl.BlockSpec((t, LANES), lambda h: (0, 0))]
        + [pl.BlockSpec(memory_space=pl.ANY)] * len(after),
        out_specs=pl.BlockSpec((t, MLA_V), lambda h: (0, h)),
        out_shape=jax.ShapeDtypeStruct((t, MLA_HEADS * MLA_V), BF16),
        compiler_params=_params(),
    )(q, kv, kr, *after)


def _mla_attn_bwd(q, kv, kr, do, after):
    t = q.shape[0]
    after = [a for a in after if a is not None]

    def body(q_ref, kv_ref, kr_ref, do_ref, *rest):
        dq_ref, dkv_ref, dkr_ref = rest[len(after):]
        dkv_ref[...] = jnp.zeros_like(dkv_ref)

        @pl.when(pl.program_id(0) == 0)
        def _():
            dkr_ref[...] = jnp.zeros_like(dkr_ref)

        for i in range(t // MLA_Q_TILE):
            rows = slice(i * MLA_Q_TILE, (i + 1) * MLA_Q_TILE)
            keys = slice(0, (i + 1) * MLA_Q_TILE)
            f = functools.partial(_mla_attn_block, q0=i * MLA_Q_TILE)
            _, vjp = jax.vjp(f, q_ref[rows, :MLA_NOPE].astype(F32), q_ref[rows, MLA_NOPE:].astype(F32),
                             kv_ref[keys, :].astype(F32), kr_ref[keys, :].astype(F32))
            dqn, dqr, dkv, dkr = vjp(do_ref[rows, :].astype(F32))
            dq_ref[rows, :MLA_NOPE] = dqn
            dq_ref[rows, MLA_NOPE:] = dqr
            dkv_ref[keys, :] += dkv
            dkr_ref[keys, :] += dkr

    return pl.pallas_call(
        body, name="mla_attn_bwd", grid=(MLA_HEADS,),
        in_specs=[pl.BlockSpec((t, MLA_HEAD_PAD), lambda h: (0, h)),
                  pl.BlockSpec((t, MLA_NOPE + MLA_V), lambda h: (0, h)), pl.BlockSpec((t, LANES), lambda h: (0, 0)),
                  pl.BlockSpec((t, MLA_V), lambda h: (0, h))] + [pl.BlockSpec(memory_space=pl.ANY)] * len(after),
        out_specs=[pl.BlockSpec((t, MLA_HEAD_PAD), lambda h: (0, h)),
                   pl.BlockSpec((t, MLA_NOPE + MLA_V), lambda h: (0, h)), pl.BlockSpec((t, LANES), lambda h: (0, 0))],
        out_shape=[jax.ShapeDtypeStruct(q.shape, F32), jax.ShapeDtypeStruct(kv.shape, F32),
                   jax.ShapeDtypeStruct(kr.shape, F32)],
        compiler_params=_params(),
    )(q, kv, kr, do, *after)


def _rope_tables(pos_col, inv_freq_row):
    t = pos_col.shape[0]

    def body(pos_ref, f_ref, cos_ref, sin_ref):
        ang = pos_ref[...].astype(F32) * f_ref[...]
        live = _iota2(ang.shape, 1) < MLA_ROPE
        cos_ref[...] = jnp.where(live, jnp.cos(ang), 0.0)
        sin_ref[...] = jnp.where(live, jnp.sin(ang), 0.0)

    return pl.pallas_call(
        body, name="rope_tables", out_shape=[jax.ShapeDtypeStruct((t, LANES), F32)] * 2, compiler_params=_params(),
    )(pos_col, inv_freq_row)


CONV_COL_TILE = 256


def _conv_gate(b, c, u, w0, w1, w2):
    cu = c * u
    return b * (w2 * cu + w1 * shift_rows(cu, 1) + w0 * shift_rows(cu, 2))


def _conv_specs(t):
    nb = D_MODEL // CONV_COL_TILE
    return [pl.BlockSpec((t, CONV_COL_TILE), lambda j, part=part: (0, part * nb + j)) for part in range(3)]


def _conv_fwd(bcu, w, after):
    t = bcu.shape[0]
    after = [a for a in after if a is not None]

    def body(b_ref, c_ref, u_ref, w_ref, *rest):
        (o_ref,) = rest[len(after):]
        o_ref[...] = _conv_gate(b_ref[...], c_ref[...], u_ref[...], w_ref[0:1, :], w_ref[1:2, :],
                                w_ref[2:3, :]).astype(o_ref.dtype)

    return pl.pallas_call(
        body, name="conv_fwd", grid=(D_MODEL // CONV_COL_TILE,),
        in_specs=_conv_specs(t) + [pl.BlockSpec((3, CONV_COL_TILE), lambda j: (0, j))]
        + [pl.BlockSpec(memory_space=pl.ANY)] * len(after),
        out_specs=pl.BlockSpec((t, CONV_COL_TILE), lambda j: (0, j)),
        out_shape=jax.ShapeDtypeStruct((t, D_MODEL), BF16), compiler_params=_params(),
    )(bcu, bcu, bcu, w, *after)


def _conv_bwd(bcu, w, dout, after):
    t = bcu.shape[0]
    after = [a for a in after if a is not None]

    def body(b_ref, c_ref, u_ref, w_ref, do_ref, *rest):
        db_ref, dc_ref, du_ref, dw_ref = rest[len(after):]
        _, vjp = jax.vjp(_conv_gate, b_ref[...], c_ref[...], u_ref[...], w_ref[0:1, :], w_ref[1:2, :], w_ref[2:3, :])
        db, dc, du, dw0, dw1, dw2 = vjp(do_ref[...])
        db_ref[...] = db.astype(db_ref.dtype)
        dc_ref[...] = dc.astype(dc_ref.dtype)
        du_ref[...] = du.astype(du_ref.dtype)
        dw_ref[0:1, :] = dw0
        dw_ref[1:2, :] = dw1
        dw_ref[2:3, :] = dw2

    col = pl.BlockSpec((t, CONV_COL_TILE), lambda j: (0, j))
    return pl.pallas_call(
        body, name="conv_bwd", grid=(D_MODEL // CONV_COL_TILE,),
        in_specs=_conv_specs(t) + [pl.BlockSpec((3, CONV_COL_TILE), lambda j: (0, j)), col]
        + [pl.BlockSpec(memory_space=pl.ANY)] * len(after),
        out_specs=[col, col, col, pl.BlockSpec((3, CONV_COL_TILE), lambda j: (0, j))],
        out_shape=[jax.ShapeDtypeStruct((t, D_MODEL), BF16)] * 3 + [jax.ShapeDtypeStruct((3, D_MODEL), F32)],
        compiler_params=_params(),
    )(bcu, bcu, bcu, w, dout, *after)


def _loss_head(y, target):
    t, d = y.shape
    tm = 256

    def body(y_ref, t_ref, loss_ref, dy_ref):
        @pl.when(pl.program_id(0) == 0)
        def _():
            loss_ref[...] = jnp.zeros_like(loss_ref)

        err = y_ref[...] - t_ref[...]
        dy_ref[...] = err * (1.0 / d)
        loss_ref[...] += 0.5 * jnp.sum(jnp.sum(err * err, axis=-1, keepdims=True) * (1.0 / d))

    tile = pl.BlockSpec((tm, d), lambda i: (i, 0))
    return pl.pallas_call(
        body, name="loss_head", grid=(t // tm,), in_specs=[tile, tile],
        out_specs=[pl.BlockSpec((8, LANES), lambda i: (0, 0)), tile],
        out_shape=[jax.ShapeDtypeStruct((8, LANES), F32), jax.ShapeDtypeStruct((t, d), F32)],
        compiler_params=_params(),
    )(y, target)


def _ln_epi(acc, res, g, b):
    a = ALPHA * res + acc
    y = _layer_norm(a, g, b)
    return a, y, y


def _ln_fn(a, g, b):
    return (_layer_norm(a, g, b),)


def _ln_bwd_epi(scale):
    def epi(acc, res, a, g, b):
        _, vjp = jax.vjp(_ln_fn, a, g, b)
        da, dg, db = vjp((acc + scale * res,))
        return da, da, dg, db
    return epi


def _relu_sq(h):
    r = jnp.maximum(h, 0)
    return r * r


def _pad_cols(w, n):
    return jnp.pad(w, ((0, 0), (0, n - w.shape[1])))


def _pad_rows(w, n):
    return jnp.pad(w, ((0, n - w.shape[0]), (0, 0)))


def _step(x, p, positions, target, small, comm):
    t = x.shape[0]
    w = small
    freqs = ROPE_BASE ** (-jnp.arange(0, MLA_ROPE // 2, dtype=F32) * (2.0 / MLA_ROPE))
    freq_row = jnp.concatenate([freqs, freqs, jnp.zeros((LANES - MLA_ROPE,), F32)])[None, :]
    cos, sin = _rope_tables(positions.reshape(t, 1), freq_row)

    saved = []
    xb = x.astype(BF16)
    for i in range(DEPTH):
        j, kind = i // 3, i % 3
        wl = comm.mixer_weights(i)
        s = {"x": xb, "wl": wl}
        tok = comm.at("fwd", i, "begin", x)
        if kind == 0:
            s["w_main"] = wl["gla_w_in_t"][:GLA_MAIN]
            s["w_lr"] = _pad_rows(wl["gla_w_in_t"][GLA_MAIN:], LANES)
            s["w_up"] = _pad_rows(w["gla_w_gate_up"][j], LANES).astype(BF16)
            s["proj"] = _matmul(xb, s["w_main"], name="gla_proj", tb=True, tn=1024, after=tok)
            s["glr"] = _matmul(xb, s["w_lr"], name="gla_lr", tb=True, out_dtypes=(F32,))
            s["z"] = _matmul(s["glr"], s["w_up"], name="gla_gate", epi=lambda acc, b: (acc + b,),
                             epi_ins=(w["gla_b_gate"][j][None, :],), out_dtypes=(F32,))
            tok = comm.at("fwd", i, "proj_done", s["z"]) or []
            s["u"], s["states"] = _gla_fwd(s["proj"], s["z"], w["gla_norm_g"][j][None, :], tok)
        elif kind == 1:
            s["cq"] = _matmul(xb, wl["mla_in"], name="mla_proj", tn=MLA_IN_PAD, b_at=REG_MLA_IN, out_dtypes=(F32,),
                              after=tok)
            s["pre_params"] = (w["mla_q_norm"][j][None, :], w["mla_kv_norm"][j][None, :], wl["mla_w_uq"], wl["mla_w_ukv"])
            s["q"], s["kv"], s["kr"] = _tile_fwd(_mla_pre, (s["cq"], cos, sin), s["pre_params"], (BF16, BF16, BF16),
                                                 tm=256, name="mla_pre_fwd")
            tok = comm.at("fwd", i, "proj_done", s["kv"]) or []
            s["u"] = _mla_attn_fwd(s["q"], s["kv"], s["kr"], tok)
        else:
            s["bcu"] = _matmul(xb, wl["conv"], name="conv_proj", tb=True, tm=256, tn=3 * D_MODEL, b_at=REG_CONV,
                               out_dtypes=(F32,), after=tok)
            tok = comm.at("fwd", i, "proj_done", s["bcu"]) or []
            s["u"] = _conv_fwd(s["bcu"], w["conv_w"][j], tok)
        g0, b0 = w["ln_g"][i, 0][None, :], w["ln_b"][i, 0][None, :]
        g1, b1 = w["ln_g"][i, 1][None, :], w["ln_b"][i, 1][None, :]
        wa, wb = s["wa"], _ = comm.slab_weights(i, s["u"])
        s["a1"], s["x1"], s["x1b"] = _matmul(s["u"], wa, name="mixer_out_ln", tm=256, tn=D_MODEL, b_at=REG_WOUT,
                                             epi=_ln_epi, epi_ins=(x, g0, b0), out_dtypes=(F32, F32, BF16))
        s["hh"] = _matmul(s["x1b"], wa, name="mlp_up", tb=True, tm=256, tn=D_FF, b_at=REG_W1T)
        tok = comm.at("fwd", i, "mid", s["hh"])
        s["a2"], s["x2"], s["x2b"] = _matmul(s["hh"], wa, name="mlp_down_ln", tm=256, tn=D_MODEL, b_at=REG_W2,
                                             a_fn=_relu_sq, epi=_ln_epi, epi_ins=(s["x1"], g1, b1),
                                             out_dtypes=(F32, F32, BF16), after=tok)
        s["pp"] = _matmul(p[i], wb, name="ple_proj", tb=True, tn=D_MODEL, b_at=REG_WPT)
        tok = comm.at("fwd", i, "end", s["pp"])
        def ple_epi(acc, xr, pp):
            y = xr + jax.nn.sigmoid(acc) * pp.astype(F32)
            return y, y, acc

        x, xb, s["gt"] = _matmul(s["x2b"], wa, name="ple_gate", tn=1024, b_at=REG_WG, epi=ple_epi,
                                 epi_ins=(s["x2"], s["pp"]), out_dtypes=(F32, BF16, BF16), after=tok)
        saved.append(s)

    loss_part, dx = _loss_head(x, target)

    gw = {n: [None] * WEIGHTS[n][0][0] for n in SMALL + REPLICATED}
    ln_g_grads, ln_b_grads = [[None, None] for _ in range(DEPTH)], [[None, None] for _ in range(DEPTH)]
    resid = lambda acc, r: (acc + ALPHA * r,)
    plus = lambda acc, r: (acc + r,)
    for i in reversed(range(DEPTH)):
        j, kind = i // 3, i % 3
        s = saved[i]
        wa = s["wa"]
        ga = lax.empty((N_DEV, A_ROWS, D_MODEL), BF16)
        gb = lax.empty((N_DEV, REG_WPT[1], PLE_DIM), BF16)
        layer_grads = {}
        tok = comm.at("bwd", i, "begin", dx)

        def ple_bwd(dxo, gt, pp):
            sg = jax.nn.sigmoid(gt.astype(F32))
            d_gt = dxo * pp.astype(F32) * sg * (1.0 - sg)
            return d_gt, d_gt, dxo * sg

        g1, b1 = w["ln_g"][i, 1][None, :], w["ln_b"][i, 1][None, :]
        d_a2, d_a2b, d_gt, d_pp, ln_g_grads[i][1], ln_b_grads[i][1] = _matmul(
            dx, wa, name="ple_gate_dx_ln", tb=True, tm=256, tn=D_MODEL, b_at=REG_WG, a_fn=ple_bwd,
            a_ins=(s["gt"], s["pp"]), a_out_dtypes=(BF16, BF16), epi=_ln_bwd_epi(1.0), epi_ins=(dx, s["a2"], g1, b1),
            out_dtypes=(F32, BF16), n_row_sums=2, after=tok)
        gb = _matmul(d_pp, p[i], name="ple_proj_dw", ta=True, tm=512, tn=PLE_DIM, out_at=REG_WPT, out_buf=gb)
        ga = _matmul(s["x2b"], d_gt, name="ple_gate_dw", ta=True, tm=512, tn=1024, out_at=REG_WG, out_buf=ga)
        tok = comm.at("bwd", i, "ln", d_a2)
        ga = _matmul(s["hh"], d_a2b, name="mlp_down_dw", ta=True, tm=REG_W2[1], tn=1024, a_fn=_relu_sq, out_at=REG_W2,
                     out_buf=ga, after=tok)
        d_hh = _matmul(d_a2b, wa, name="mlp_down_dx", tb=True, tm=256, tn=D_FF, b_at=REG_W2, after=[ga, gb],
                       epi=lambda acc, hh: (acc * 2.0 * jnp.maximum(hh.astype(F32), 0.0),), epi_ins=(s["hh"],))
        ga = _matmul(d_hh, s["x1b"], name="mlp_up_dw", ta=True, tm=REG_W1T[1], tn=1024, out_at=REG_W1T, out_buf=ga)
        g0, b0 = w["ln_g"][i, 0][None, :], w["ln_b"][i, 0][None, :]
        d_a1, d_a1b, ln_g_grads[i][0], ln_b_grads[i][0] = _matmul(
            d_hh, wa, name="mlp_up_dx_ln", tm=256, tn=D_MODEL, b_at=REG_W1T, epi=_ln_bwd_epi(ALPHA),
            epi_ins=(d_a2, s["a1"], g0, b0), out_dtypes=(F32, BF16), n_row_sums=2, after=ga)
        ga = _matmul(s["u"], d_a1b, name="mixer_out_dw", ta=True, tm=512, tn=1024, out_at=REG_WOUT, out_buf=ga)
        du = _matmul(d_a1b, wa, name="mixer_out_dx", tb=True, tn=1024, b_at=REG_WOUT, out_dtypes=(F32,), after=ga)
        comm.slab_grads(i, ga, gb)
        tok = comm.at("bwd", i, "slab_done", du) or []
        if kind == 0:
            dproj, dz, dg, dz_sum = _gla_bwd(s["proj"], s["z"], w["gla_norm_g"][j][None, :], s["states"], du, tok)
            tok = comm.at("bwd", i, "mixer_done", dproj)
            gw["gla_norm_g"][j] = dg[0]
            gw["gla_b_gate"][j] = dz_sum[0]
            gw["gla_w_gate_up"][j] = _matmul(s["glr"], dz, name="gla_gate_dw", ta=True, out_dtypes=(F32,),
                                             after=tok)[:GLA_RANK]
            dglr = _matmul(dz, s["w_up"], name="gla_gate_dx", tb=True, out_dtypes=(F32,))
            dw_main = _matmul(dproj, s["x"], name="gla_proj_dw", ta=True, tn=1024, out_dtypes=(F32,))
            dw_lr = _matmul(dglr, s["x"], name="gla_lr_dw", ta=True, tn=1024, out_dtypes=(F32,))[:GLA_RANK]
            layer_grads["gla_w_in_t"] = jnp.concatenate([dw_main, dw_lr], axis=0)
            dx = _matmul(dproj, s["w_main"], name="gla_proj_dx", tn=1024, epi=resid, epi_ins=(d_a1,),
                         out_dtypes=(F32,), after=[dw_main, dw_lr, gw["gla_w_gate_up"][j]])
            dx = _matmul(dglr, s["w_lr"], name="gla_lr_dx", tn=1024, epi=plus, epi_ins=(dx,), out_dtypes=(F32,))
        elif kind == 1:
            dq, dkv, dkr = _mla_attn_bwd(s["q"], s["kv"], s["kr"], du, tok)
            tok = comm.at("bwd", i, "mixer_done", dq)
            (d_cq,), (dgq, dgkv, layer_grads["mla_uq"], layer_grads["mla_ukv"]) = _tile_bwd(
                _mla_pre, (s["cq"], cos, sin), s["pre_params"], (dq, dkv, dkr), (BF16,), tm=256, name="mla_pre_bwd",
                diff_tiled=[0])
            gw["mla_q_norm"][j], gw["mla_kv_norm"][j] = dgq[0], dgkv[0]
            layer_grads["mla_in"] = _matmul(s["x"], d_cq, name="mla_proj_dw", ta=True, tm=512, tn=MLA_IN_PAD,
                                            out_at=REG_MLA_IN, after=tok,
                                            out_buf=lax.empty((N_DEV, REG_MLA_IN[1], MLA_IN_PAD), BF16))
            dx = _matmul(d_cq, s["wl"]["mla_in"], name="mla_proj_dx", tb=True, tn=1024, b_at=REG_MLA_IN, epi=resid,
                         epi_ins=(d_a1,), out_dtypes=(F32,), after=layer_grads["mla_in"])
        else:
            db, dc, du_, dcw = _conv_bwd(s["bcu"], w["conv_w"][j], du, tok)
            tok = comm.at("bwd", i, "mixer_done", db)
            gw["conv_w"][j] = dcw
            dbcu = jnp.concatenate([db, dc, du_], axis=1)
            layer_grads["conv"] = _matmul(dbcu, s["x"], name="conv_proj_dw", ta=True, tm=REG_CONV[1], tn=1024,
                                          out_at=REG_CONV, out_buf=lax.empty((N_DEV, REG_CONV[1], D_MODEL), BF16),
                                          after=tok)
            dx = _matmul(dbcu, s["wl"]["conv"], name="conv_proj_dx", tn=1024, b_at=REG_CONV, epi=resid, epi_ins=(d_a1,),
                         out_dtypes=(F32,), after=layer_grads["conv"])
        comm.mixer_grads(i, layer_grads)

    gw["ln_g"] = [jnp.concatenate([a, b], axis=0) for a, b in ln_g_grads]
    gw["ln_b"] = [jnp.concatenate([a, b], axis=0) for a, b in ln_b_grads]
    return loss_part, dx, {n: jnp.stack(gw[n]).astype(F32) for n in gw}


MESH_IDS = pl.DeviceIdType.MESH
ANY = pl.BlockSpec(memory_space=pl.ANY)
HBM_SPEC = pl.BlockSpec(memory_space=pltpu.HBM)
SEM_SPEC = pl.BlockSpec(memory_space=pltpu.SEMAPHORE)
DATAFLOW_EFFECT = pltpu.SideEffectType.DATAFLOW_SIDE_EFFECTING
CORE_COPIES, CHIP_COPIES = 4, 3


def _my_place():
    return lax.axis_index("x"), lax.axis_index("y"), lax.axis_index("c")


def _other_chips(mx, my):
    return [(1 - mx, my), (mx, 1 - my), (1 - mx, 1 - my)]


def _remote(src, dst, send_sems, recv_sems, k, to):
    return pltpu.make_async_remote_copy(src_ref=src, dst_ref=dst, send_sem=send_sems.at[k], recv_sem=recv_sems.at[k],
                                        device_id=to, device_id_type=MESH_IDS)


def _gather_first_copies(n_arr):
    def make(bufs, send_sems, recv_sems):
        mx, my, mc = _my_place()
        mine = 4 * mx + 2 * my + mc
        peers = [(mx, my, 1 - mc)] + [(cx, cy, mc) for cx, cy in _other_chips(mx, my)]
        return [_remote(bufs[a].at[mine], bufs[a].at[mine], send_sems, recv_sems, (1 + CHIP_COPIES) * a + k, to)
                for a in range(n_arr) for k, to in enumerate(peers)]
    return make, (1 + CHIP_COPIES) * n_arr


def _gather_forward_copies(n_arr):
    def make(bufs, send_sems, recv_sems):
        mx, my, mc = _my_place()
        blocks = [4 * cx + 2 * cy + mc for cx, cy in _other_chips(mx, my)]
        return [_remote(bufs[a].at[blk], bufs[a].at[blk], send_sems, recv_sems, CHIP_COPIES * a + k, (mx, my, 1 - mc))
                for a in range(n_arr) for k, blk in enumerate(blocks)]
    return make, CHIP_COPIES * n_arr


def _scatter_core_copies(n_arr):
    def make(bufs, send_sems, recv_sems):
        mx, my, mc = _my_place()
        return [_remote(bufs[a].at[2 * k + (1 - mc)], bufs[n_arr + a].at[k], send_sems, recv_sems, CORE_COPIES * a + k,
                        (mx, my, 1 - mc)) for a in range(n_arr) for k in range(CORE_COPIES)]
    return make, CORE_COPIES * n_arr


def _scatter_chip_copies(n_arr):
    def make(bufs, send_sems, recv_sems):
        mx, my, mc = _my_place()
        return [_remote(bufs[a].at[2 * cx + cy], bufs[n_arr + a].at[k], send_sems, recv_sems, CHIP_COPIES * a + k,
                        (cx, cy, mc)) for a in range(n_arr) for k, (cx, cy) in enumerate(_other_chips(mx, my))]
    return make, CHIP_COPIES * n_arr


def _exchange(name, bufs, copies):
    make, n_copies = copies
    n = len(bufs)

    def body(*refs):
        descs = make(refs[:n], refs[2 * n], refs[2 * n + 1])
        for cp in descs:
            cp.start()
        for cp in descs:
            cp.wait()

    return pl.pallas_call(
        body, name=name, out_shape=[jax.ShapeDtypeStruct(b.shape, b.dtype) for b in bufs], in_specs=[ANY] * n,
        out_specs=[ANY] * n, input_output_aliases={i: i for i in range(n)},
        scratch_shapes=[pltpu.SemaphoreType.DMA((n_copies,)), pltpu.SemaphoreType.DMA((n_copies,))],
    )(*bufs)


def _exchange_start(name, parts, after):
    sizes = [len(bufs) for bufs, _ in parts]
    all_bufs = [b for bufs, _ in parts for b in bufs]
    n, n_sems = len(all_bufs), 2 * len(parts)

    def body(*refs):
        offset = 0
        for p, ((_, (make, _)), size) in enumerate(zip(parts, sizes)):
            for cp in make(refs[offset:offset + size], refs[n + 1 + 2 * p], refs[n + 2 + 2 * p]):
                cp.start()
            offset += size
        refs[-1][...] = jnp.zeros_like(refs[-1])

    outs = pl.pallas_call(
        body, name=name,
        out_shape=(*[pltpu.SemaphoreType.DMA((n_copies,)) for _, (_, n_copies) in parts for _ in range(2)],
                   *[pltpu.HBM(b.shape, b.dtype) for b in all_bufs], jax.ShapeDtypeStruct((8, LANES), F32)),
        in_specs=[HBM_SPEC] * n + [ANY],
        out_specs=(*[SEM_SPEC] * n_sems, *[HBM_SPEC] * n, pl.BlockSpec(memory_space=pltpu.VMEM)),
        input_output_aliases={i: n_sems + i for i in range(n)},
        compiler_params=pltpu.CompilerParams(has_side_effects=DATAFLOW_EFFECT),
    )(*[pltpu.with_memory_space_constraint(b, pltpu.HBM) for b in all_bufs], after)
    started, offset = [], n_sems
    for p, size in enumerate(sizes):
        started.append(((outs[2 * p], outs[2 * p + 1]), list(outs[offset:offset + size])))
        offset += size
    return started, outs[-1]


def _exchange_wait(name, sems, bufs, copies, after):
    make, _ = copies
    n = len(bufs)

    def body(*refs):
        for cp in make(refs[:n], refs[n], refs[n + 1]):
            cp.wait_send()
            cp.wait_recv()

    return list(pl.pallas_call(
        body, name=name, out_shape=[pltpu.HBM(b.shape, b.dtype) for b in bufs],
        in_specs=[HBM_SPEC] * n + [SEM_SPEC, SEM_SPEC, ANY], out_specs=[HBM_SPEC] * n,
        input_output_aliases={i: i for i in range(n)},
        compiler_params=pltpu.CompilerParams(has_side_effects=DATAFLOW_EFFECT),
    )(*bufs, *sems, after))


SUM_TILE_BYTES = 4 * 1024 * 1024


def _row_tile(r, c):
    best = None
    for cand in range(16, r + 1, 16):
        if r % cand == 0 and cand * c * 2 <= SUM_TILE_BYTES:
            best = cand
    return r if best is None else best


def _pair_sum(g, recv, my_c):
    _, r, c = g.shape
    tr = _row_tile(r, c)

    def body(c_ref, g_ref, r_ref, o_ref):
        o_ref[...] = (g_ref[...].astype(F32) + r_ref[...].astype(F32)).astype(o_ref.dtype)

    return pl.pallas_call(
        body, name="rs_pair_sum", out_shape=jax.ShapeDtypeStruct((4, r, c), g.dtype),
        grid_spec=pltpu.PrefetchScalarGridSpec(
            num_scalar_prefetch=1, grid=(4, r // tr),
            in_specs=[pl.BlockSpec((1, tr, c), lambda n, i, cr: (2 * n + cr[0], i, 0)),
                      pl.BlockSpec((1, tr, c), lambda n, i, cr: (n, i, 0))],
            out_specs=pl.BlockSpec((1, tr, c), lambda n, i, cr: (n, i, 0))),
        compiler_params=_params(),
    )(my_c, g, recv)


def _chip_sum(h, recv, my_chip):
    _, r, c = h.shape
    tr = _row_tile(r, c)

    def body(j_ref, h_ref, r0_ref, r1_ref, r2_ref, o_ref):
        o_ref[...] = ((h_ref[0].astype(F32) + r0_ref[0].astype(F32)) + r1_ref[0].astype(F32)) + r2_ref[0].astype(F32)

    return pl.pallas_call(
        body, name="rs_chip_sum", out_shape=jax.ShapeDtypeStruct((r, c), F32),
        grid_spec=pltpu.PrefetchScalarGridSpec(
            num_scalar_prefetch=1, grid=(r // tr,),
            in_specs=[pl.BlockSpec((1, tr, c), lambda i, jr: (jr[0], i, 0))]
            + [pl.BlockSpec((1, tr, c), lambda i, jr, n=n: (n, i, 0)) for n in range(3)],
            out_specs=pl.BlockSpec((tr, c), lambda i, jr: (i, 0))),
        compiler_params=_params(),
    )(my_chip, h, recv, recv, recv)


def _sum_blocks(g):
    n, r, c = g.shape

    def body(g_ref, o_ref):
        acc = g_ref[0]
        for k in range(1, n):
            acc = acc + g_ref[k]
        o_ref[...] = acc

    return pl.pallas_call(body, name="sum_blocks", out_shape=jax.ShapeDtypeStruct((r, c), F32), compiler_params=_params())(g)


def _pack(flat_parts, cols, row_multiple, dtype):
    flat = jnp.concatenate([f.astype(dtype) for f in flat_parts])
    per_row_block = cols * row_multiple
    padded = -(-flat.shape[0] // per_row_block) * per_row_block
    return jnp.pad(flat, (0, padded - flat.shape[0])).reshape(padded // cols, cols)


def _shard_shape(name):
    shape, axis = WEIGHTS[name]
    if axis is None:
        return shape
    return tuple(s // N_DEV if a == axis else s for a, s in enumerate(shape))


def _size(shape):
    n = 1
    for s in shape:
        n *= s
    return n


def _unshard(blocks, name):
    _, axis = WEIGHTS[name]
    return jnp.concatenate([blocks[k] for k in range(N_DEV)], axis=axis)


def _unpack_blocks(flat, names):
    out, off = {}, 0
    for n in names:
        shp = _shard_shape(n)
        out[n] = flat[..., off:off + _size(shp)].reshape(flat.shape[:-1] + shp)
        off += _size(shp)
    return out


def _layer_slabs(shard, i):
    j, kind = i // 3, i % 3
    w_out = (shard["gla_w_out"], shard["mla_w_out"], shard["conv_w_out"])[kind][j]
    out = {"a": jnp.concatenate([shard["mlp_w2"][i], shard["mlp_w1"][i].T, w_out, shard["ple_w_gate"][i]], axis=0).astype(BF16),
           "b": shard["ple_w_proj"][i].T.astype(BF16)}
    if kind == 0:
        out["gla"] = shard["gla_w_in"][j].T.astype(BF16)
    elif kind == 1:
        out["mla_in"] = _pad_cols(shard["mla_w_in"][j], MLA_IN_PAD).astype(BF16)
        out["mla_uq"] = _pad_cols(shard["mla_w_uq"][j], MLA_HEAD_PAD).astype(BF16)
        out["mla_ukv"] = shard["mla_w_ukv"][j].astype(BF16)
    else:
        out["conv"] = shard["conv_w_in"][j].T.astype(BF16)
    return out


def _mixer_weights(landed, i):
    kind = i % 3
    if kind == 0:
        return {"gla_w_in_t": landed["gla"].reshape(-1, D_MODEL)}
    if kind == 2:
        return {"conv": landed["conv"]}
    heads_side_by_side = lambda g: g.transpose(1, 0, 2).reshape(g.shape[1], -1)
    return {"mla_in": landed["mla_in"], "mla_w_uq": heads_side_by_side(landed["mla_uq"]),
            "mla_w_ukv": heads_side_by_side(landed["mla_ukv"])}


def _mixer_grad_buffers(layer_grads, i):
    kind = i % 3
    if kind == 0:
        return {"gla": layer_grads["gla_w_in_t"].reshape(N_DEV, -1, D_MODEL).astype(BF16)}
    if kind == 2:
        return {"conv": layer_grads["conv"]}
    head_blocks = lambda g: g.reshape(g.shape[0], N_DEV, -1).transpose(1, 0, 2).astype(BF16)
    return {"mla_in": layer_grads["mla_in"], "mla_uq": head_blocks(layer_grads["mla_uq"]),
            "mla_ukv": head_blocks(layer_grads["mla_ukv"])}


SLAB_KEYS = ("a", "b")


class _Overlap:
    def __init__(self, shard, small_pack):
        mx, my, mc = _my_place()
        self.my_c = mc.astype(jnp.int32).reshape(1)
        self.my_chip = (2 * mx + my).astype(jnp.int32).reshape(1)
        mine = 4 * mx + 2 * my + mc
        def landing_of(slabs):
            return {k: lax.dynamic_update_index_in_dim(lax.empty((N_DEV, *v.shape), v.dtype), v, mine, 0)
                    for k, v in slabs.items()}

        first = _layer_slabs(shard, 0)
        first["small"] = small_pack
        self.landing = [landing_of(first)]
        self.fly = {}
        self.grads = [{} for _ in range(DEPTH)]
        self.reduced = [{} for _ in range(DEPTH)]
        tok = self._start_many([self._first_spec(0, "mixer"), self._first_spec(0, "slab")], shard["ln_g"])
        shard, tok = lax.optimization_barrier((shard, tok))
        self.landing += [landing_of(_layer_slabs(shard, i)) for i in range(1, DEPTH)]
        bufs = self._wait("ag_first_mixer_l0", tok)
        self.landing[0].update(zip(self._keys(self.landing[0], "mixer"),
                                   _exchange("ag_forward_mixer_l0", bufs, _gather_forward_copies(len(bufs)))))

    @staticmethod
    def _keys(names, group):
        return [k for k in names if (k in SLAB_KEYS) == (group == "slab")]

    def _start(self, name, bufs, copies, after):
        return self._start_many([(name, bufs, copies)], after)

    def _start_many(self, specs, after):
        call = specs[0][0] + ("" if len(specs) == 1 else f"_and_{len(specs) - 1}_more") + "_start"
        started, tok = _exchange_start(call, [(bufs, copies) for _, bufs, copies in specs], after)
        for (name, _, copies), (sems, bufs) in zip(specs, started):
            self.fly[name] = (sems, bufs, copies)
        return tok

    def _wait(self, name, after):
        sems, bufs, copies = self.fly.pop(name)
        return _exchange_wait(name + "_wait", sems, bufs, copies, after)

    def mixer_weights(self, i):
        return _mixer_weights(self.landing[i], i)

    def slab_weights(self, i, dep):
        self._gather_done(i, "slab", dep)
        return self.landing[i]["a"], self.landing[i]["b"]

    def slab_grads(self, i, ga, gb):
        self.grads[i].update(a=ga, b=gb)

    def mixer_grads(self, i, layer_grads):
        self.grads[i].update(_mixer_grad_buffers(layer_grads, i))

    def at(self, phase, i, point, dep):
        toks = []
        if phase == "fwd":
            if point == "begin" and i == 0:
                toks.append(self._start_many([self._first_spec(1, "mixer"), self._first_spec(1, "slab")],
                                             self.landing[0][self._keys(self.landing[0], "mixer")[0]]))
            if point == "proj_done":
                toks.append(self._gather_forward(i, "slab", dep))
            if point == "mid" and i + 1 < DEPTH:
                bufs = self._wait(f"ag_first_mixer_l{i + 1}", dep)
                specs = [(f"ag_forward_mixer_l{i + 1}", bufs, _gather_forward_copies(len(bufs)))]
                if i + 2 < DEPTH:
                    specs += [self._first_spec(i + 2, "mixer"), self._first_spec(i + 2, "slab")]
                toks.append(self._start_many(specs, dep))
            if point == "end" and i + 1 < DEPTH:
                self._gather_done(i + 1, "mixer", dep)
        else:
            if point == "begin" and i + 1 < DEPTH:
                toks.append(self._scatter_cores(i + 1, "mixer", dep))
            if point == "ln" and i + 1 < DEPTH:
                toks.append(self._scatter_chips(i + 1, "mixer", dep))
            if point == "slab_done":
                if i + 1 < DEPTH:
                    self._scatter_done(i + 1, "slab", dep)
                    self._scatter_done(i + 1, "mixer", dep)
                toks.append(self._scatter_cores(i, "slab", dep))
            if point == "mixer_done":
                toks.append(self._scatter_chips(i, "slab", dep))
        return toks or None

    def _first_spec(self, i, group):
        bufs = [self.landing[i][k] for k in self._keys(self.landing[i], group)]
        return f"ag_first_{group}_l{i}", bufs, _gather_first_copies(len(bufs))

    def _gather_forward(self, i, group, after):
        bufs = self._wait(f"ag_first_{group}_l{i}", after)
        return self._start(f"ag_forward_{group}_l{i}", bufs, _gather_forward_copies(len(bufs)), after)

    def _gather_done(self, i, group, after):
        keys = self._keys(self.landing[i], group)
        self.landing[i].update(zip(keys, self._wait(f"ag_forward_{group}_l{i}", after)))

    def _scatter_cores(self, i, group, after):
        gs = [self.grads[i][k] for k in self._keys(self.grads[i], group)]
        land = [lax.empty((4, *g.shape[1:]), g.dtype) for g in gs]
        return self._start(f"rs_cores_{group}_l{i}", gs + land, _scatter_core_copies(len(gs)), after)

    def _pair_sums(self, bufs):
        n = len(bufs) // 2
        hs = [_pair_sum(g, r, self.my_c) for g, r in zip(bufs[:n], bufs[n:])]
        return hs + [lax.empty((3, *h.shape[1:]), h.dtype) for h in hs]

    def _scatter_chips(self, i, group, after):
        bufs = self._pair_sums(self._wait(f"rs_cores_{group}_l{i}", after))
        return self._start(f"rs_chips_{group}_l{i}", bufs, _scatter_chip_copies(len(bufs) // 2), after)

    def _chip_sums(self, i, group, bufs):
        n = len(bufs) // 2
        for k, h, r in zip(self._keys(self.grads[i], group), bufs[:n], bufs[n:]):
            self.reduced[i][k] = _chip_sum(h, r, self.my_chip)

    def _scatter_done(self, i, group, after):
        self._chip_sums(i, group, self._wait(f"rs_chips_{group}_l{i}", after))

    def tail_begin(self, dep):
        return self._scatter_cores(0, "mixer", dep)

    def tail_middle(self, dep):
        self._scatter_done(0, "slab", dep)
        return self._scatter_chips(0, "mixer", dep)

    def tail_end(self, dep):
        self._scatter_done(0, "mixer", dep)


def _small_gather_start(x, name, after):
    mx, my, mc = _my_place()
    land = lax.dynamic_update_index_in_dim(lax.empty((N_DEV, *x.shape), x.dtype), x, 4 * mx + 2 * my + mc, 0)
    started, tok = _exchange_start(name + "_first_start", [([land], _gather_first_copies(1))], after)
    return name, started[0], tok


def _small_gather_finish(started, after):
    name, (sems, bufs), _ = started
    bufs = _exchange_wait(name + "_first_wait", sems, bufs, _gather_first_copies(1), after)
    return _exchange(name + "_forward", bufs, _gather_forward_copies(1))[0]


def _adamw_math(w, g, m, v):
    m2 = ADAM_B1 * m + (1.0 - ADAM_B1) * g
    v2 = ADAM_B2 * v + (1.0 - ADAM_B2) * (g * g)
    m_hat = m2 / (1.0 - ADAM_B1 ** ADAM_STEP)
    v_hat = v2 / (1.0 - ADAM_B2 ** ADAM_STEP)
    return -ADAM_LR * (m_hat / (jnp.sqrt(v_hat) + ADAM_EPS) + ADAM_WD * w), m2, v2


ADAMW_TILE_BYTES = 1024 * 1024


def _adamw_layer(name, w, m, v, j, g, g_at, transposed, chain, after):
    n_layers, r, c = w.shape
    tr = max(t for t in range(8, r + 1, 8) if r % t == 0 and (t * c * 4 <= ADAMW_TILE_BYTES or t == 8))
    rb, rows = g_at
    if transposed:
        assert rows == c and g.shape[1] == r, (name, g.shape, g_at)
        g_spec = pl.BlockSpec((rows, tr), lambda i: (rb, i))
    else:
        assert rows == r and g.shape[1] == c, (name, g.shape, g_at)
        g_spec = pl.BlockSpec((tr, c), lambda i: (rb * (r // tr) + i, 0))
    extra = list(chain or []) + [a for a in (after or []) if a is not None]
    n_chain = 4 if chain else 0

    def body(w_ref, m_ref, v_ref, g_ref, *rest):
        g_out, d_out, m_out, v_out, tok_ref = rest[len(extra):]
        gv = g_ref[...].T if transposed else g_ref[...]
        g_out[0] = gv
        d_out[0], m_out[0], v_out[0] = _adamw_math(w_ref[0], gv, m_ref[0], v_ref[0])
        tok_ref[...] = jnp.zeros_like(tok_ref)

    layer_spec = pl.BlockSpec((1, tr, c), lambda i: (j, i, 0))
    outs = pl.pallas_call(
        body, name=f"adamw_{name}_l{j}", grid=(r // tr,),
        in_specs=[layer_spec] * 3 + [g_spec] + [pl.BlockSpec(memory_space=pl.ANY)] * len(extra),
        out_specs=[layer_spec] * 4 + [pl.BlockSpec((8, LANES), lambda i: (0, 0))],
        out_shape=[jax.ShapeDtypeStruct(w.shape, F32)] * 4 + [jax.ShapeDtypeStruct((8, LANES), F32)],
        input_output_aliases={4 + k: k for k in range(n_chain)}, compiler_params=_params(),
    )(w, m, v, g, *extra)
    return list(outs[:4]), outs[4]


def _adamw(w, g, m, v, name):
    shape = w.shape
    cols = shape[-1]
    rows = _size(shape) // cols
    tr = rows
    for cand in (512, 256, 128, 64, 32, 16, 8):
        if rows > cand and rows % cand == 0:
            tr = cand
            break

    def body(w_ref, g_ref, m_ref, v_ref, d_ref, mo_ref, vo_ref):
        d_ref[...], mo_ref[...], vo_ref[...] = _adamw_math(w_ref[...], g_ref[...], m_ref[...], v_ref[...])

    spec = pl.BlockSpec((tr, cols), lambda i: (i, 0))
    outs = pl.pallas_call(
        body, name="adamw_" + name, grid=(rows // tr,), in_specs=[spec] * 4, out_specs=[spec] * 3,
        out_shape=[jax.ShapeDtypeStruct((rows, cols), F32)] * 3, compiler_params=_params(),
    )(*[a.reshape(rows, cols) for a in (w, g, m, v)])
    return [o.reshape(shape) for o in outs]


def kernel(x, p, positions, gla_w_in, gla_w_gate_up, gla_b_gate, gla_norm_g, gla_w_out, mla_w_in, mla_q_norm, mla_kv_norm, mla_w_uq, mla_w_ukv, mla_w_out, conv_w_in, conv_w, conv_w_out, ln_g, ln_b, mlp_w1, mlp_w2, ple_w_gate, ple_w_proj, loss_target, m_gla_w_in, m_gla_w_gate_up, m_gla_b_gate, m_gla_norm_g, m_gla_w_out, m_mla_w_in, m_mla_q_norm, m_mla_kv_norm, m_mla_w_uq, m_mla_w_ukv, m_mla_w_out, m_conv_w_in, m_conv_w, m_conv_w_out, m_ln_g, m_ln_b, m_mlp_w1, m_mlp_w2, m_ple_w_gate, m_ple_w_proj, v_gla_w_in, v_gla_w_gate_up, v_gla_b_gate, v_gla_norm_g, v_gla_w_out, v_mla_w_in, v_mla_q_norm, v_mla_kv_norm, v_mla_w_uq, v_mla_w_ukv, v_mla_w_out, v_conv_w_in, v_conv_w, v_conv_w_out, v_ln_g, v_ln_b, v_mlp_w1, v_mlp_w2, v_ple_w_gate, v_ple_w_proj):
    args = locals()
    shard = {n: args[n] for n in WEIGHT_NAMES}
    mom = {n: args["m_" + n] for n in WEIGHT_NAMES}
    var = {n: args["v_" + n] for n in WEIGHT_NAMES}
    mx, my, mc = _my_place()

    comm = _Overlap(shard, _pack([shard[n].reshape(-1) for n in SMALL], LANES, 8, F32))
    small_all = comm.landing[0]["small"]
    small = {n: shard[n] for n in REPLICATED}
    small.update({n: _unshard(blk, n) for n, blk in _unpack_blocks(small_all.reshape(N_DEV, -1), SMALL).items()})
    loss_part, grad_x, small_grads = _step(x[0], p[:, 0], positions[0], loss_target[0], small, comm)

    chains = {}

    def update(name, j, g, g_at, transposed, tok):
        chains[name], tok = _adamw_layer(name, shard[name], mom[name], var[name], j, g, g_at, transposed,
                                         chains.get(name), [tok])
        return tok

    def update_layer(i, groups, tok):
        j, kind = i // 3, i % 3
        red = comm.reduced[i]
        if "slab" in groups:
            tok = update("mlp_w2", i, red["a"], REG_W2, False, tok)
            tok = update("mlp_w1", i, red["a"], REG_W1T, True, tok)
            tok = update(("gla_w_out", "mla_w_out", "conv_w_out")[kind], j, red["a"], REG_WOUT, False, tok)
            tok = update("ple_w_gate", i, red["a"], REG_WG, False, tok)
            tok = update("ple_w_proj", i, red["b"], REG_WPT, True, tok)
        if "mixer" in groups:
            if kind == 0:
                tok = update("gla_w_in", j, red["gla"].T, (0, D_MODEL), False, tok)
            elif kind == 2:
                tok = update("conv_w_in", j, red["conv"], REG_CONV, True, tok)
            else:
                for n, g in (("mla_w_in", red["mla_in"][:, :MLA_IN]), ("mla_w_ukv", red["mla_ukv"]),
                             ("mla_w_uq", red["mla_uq"][:, :MLA_NOPE + MLA_ROPE])):
                    tok = update(n, j, g, (0, g.shape[0]), False, tok)
        return tok

    tok = comm.tail_begin(grad_x)
    tok = update_layer(3, ("slab", "mixer"), tok)
    tok = update_layer(2, ("slab", "mixer"), tok)
    tok = comm.tail_middle(tok)
    small_parts = [loss_part[0, :1]] + [small_grads[n].reshape(-1) for n in SMALL + REPLICATED]
    small_gather = _small_gather_start(_pack(small_parts, LANES, 8, F32), "ag_small_grads", tok)
    tok = update_layer(1, ("slab", "mixer"), small_gather[2])
    tok = update_layer(0, ("slab",), tok)
    comm.tail_end(tok)
    tok = update_layer(0, ("mixer",), tok)
    red_small = _sum_blocks(_small_gather_finish(small_gather, tok)).reshape(-1)
    loss = red_small[0]
    off = 1
    dev = 4 * mx + 2 * my + mc
    for n in SMALL + REPLICATED:
        shape, axis = WEIGHTS[n]
        full_g = red_small[off:off + _size(shape)].reshape(shape)
        off += _size(shape)
        if axis is not None:
            width = shape[axis] // N_DEV
            full_g = lax.dynamic_slice_in_dim(full_g, dev * width, width, axis=axis)
        chains[n] = [full_g, *_adamw(shard[n], full_g, mom[n], var[n], n)]
    return (loss, grad_x[None], *[chains[n][k] for k in range(4) for n in WEIGHT_NAMES])
```

```python
import functools

import jax
import jax.numpy as jnp
from jax import lax
from jax.experimental import pallas as pl
from jax.experimental.pallas import tpu as pltpu

F32, BF16 = jnp.float32, jnp.bfloat16
N_DEV = 8

D_MODEL = 1024
DEPTH = 4
CHUNK = 64
ALPHA = (2 * DEPTH) ** 0.25
LN_EPS = 1e-5
RMS_EPS = 1e-6
PLE_DIM = 256
D_FF = 4 * D_MODEL
GLA_HEADS = 4
GLA_DK = 128
GLA_DV = 256
GLA_RANK = 16
GLA_TAU = 16.0
GLA_HK = GLA_HEADS * GLA_DK
GLA_HV = GLA_HEADS * GLA_DV
GLA_MAIN = 2 * GLA_HK + GLA_HV + D_MODEL
MLA_HEADS = 8
MLA_NOPE = 128
MLA_ROPE = 64
MLA_V = 128
MLA_RANK = 256
MLA_IN = 2 * MLA_RANK + MLA_ROPE
MLA_IN_PAD = 640
ROPE_BASE = 10000.0
LANES = 128
ADAM_LR, ADAM_B1, ADAM_B2, ADAM_EPS, ADAM_WD, ADAM_STEP = 0.001, 0.9, 0.999, 1e-08, 0.01, 10

V7X_VMEM_LIMIT_BYTES = 56 * 1024 * 1024

WEIGHTS = {
    "gla_w_in": ((2, 1024, 3088), 2), "gla_w_gate_up": ((2, 16, 512), 2), "gla_b_gate": ((2, 512), 1),
    "gla_norm_g": ((2, 256), 1), "gla_w_out": ((2, 1024, 1024), 1), "mla_w_in": ((1, 1024, 576), 1),
    "mla_q_norm": ((1, 256), None), "mla_kv_norm": ((1, 256), None), "mla_w_uq": ((1, 256, 1536), 2),
    "mla_w_ukv": ((1, 256, 2048), 2), "mla_w_out": ((1, 1024, 1024), 1), "conv_w_in": ((1, 1024, 3072), 2),
    "conv_w": ((1, 3, 1024), 2), "conv_w_out": ((1, 1024, 1024), 1), "ln_g": ((4, 2, 1024), 2),
    "ln_b": ((4, 2, 1024), 2), "mlp_w1": ((4, 1024, 4096), 2), "mlp_w2": ((4, 4096, 1024), 1),
    "ple_w_gate": ((4, 1024, 1024), 1), "ple_w_proj": ((4, 256, 1024), 2),
}
WEIGHT_NAMES = list(WEIGHTS)
REG_W2, REG_W1T, REG_WOUT, REG_WG = (0, 512), (1, 512), (8, 128), (9, 128)
A_ROWS = 1280
REG_CONV = (0, 384)
REG_WPT = (0, 128)
REG_MLA_IN = (0, 128)
MLA_HEAD_PAD = 2 * LANES
SMALL = ["gla_w_gate_up", "gla_b_gate", "gla_norm_g", "conv_w", "ln_g", "ln_b"]
REPLICATED = ["mla_q_norm", "mla_kv_norm"]


def _params(**kw):
    return pltpu.CompilerParams(vmem_limit_bytes=V7X_VMEM_LIMIT_BYTES, **kw)


def _dot(a, b, ca, cb):
    return lax.dot_general(a, b, (((ca,), (cb,)), ((), ())), preferred_element_type=F32)


def _nn(a, b):
    return _dot(a.astype(BF16), b.astype(BF16), 1, 0)


def _nt(a, b):
    return _dot(a.astype(BF16), b.astype(BF16), 1, 1)


def _tn(a, b):
    return _dot(a.astype(BF16), b.astype(BF16), 0, 0)


@jax.custom_vjp
def mm_nn(a, b):
    return _nn(a, b)


def _mm_nn_fwd(a, b):
    return _nn(a, b), (a, b)


def _mm_nn_bwd(res, g):
    a, b = res
    return _nt(g, b).astype(a.dtype), _tn(a, g).astype(b.dtype)


mm_nn.defvjp(_mm_nn_fwd, _mm_nn_bwd)


@jax.custom_vjp
def mm_nt(a, b):
    return _nt(a, b)


def _mm_nt_fwd(a, b):
    return _nt(a, b), (a, b)


def _mm_nt_bwd(res, g):
    a, b = res
    return _nn(g, b).astype(a.dtype), _tn(g, a).astype(b.dtype)


mm_nt.defvjp(_mm_nt_fwd, _mm_nt_bwd)


@jax.custom_vjp
def mm_tn(a, b):
    return _tn(a, b)


def _mm_tn_fwd(a, b):
    return _tn(a, b), (a, b)


def _mm_tn_bwd(res, g):
    a, b = res
    return _nt(b, g).astype(a.dtype), _nn(a, g).astype(b.dtype)


mm_tn.defvjp(_mm_tn_fwd, _mm_tn_bwd)


def _iota2(shape, dim):
    return lax.broadcasted_iota(jnp.int32, shape, dim)


def _split3(x):
    hi = x.astype(BF16)
    rest = x - hi.astype(F32)
    mid = rest.astype(BF16)
    return hi, mid, (rest - mid.astype(F32)).astype(BF16)


def _tri_dot(tri, x):
    return sum(_dot(tri.astype(BF16), piece, 1, 0) for piece in _split3(x))


@jax.custom_vjp
def cumsum_rows(x):
    n = x.shape[0]
    return _tri_dot(_iota2((n, n), 0) >= _iota2((n, n), 1), x)


def _cumsum_fwd(x):
    return cumsum_rows(x), None


def _cumsum_bwd(_, g):
    n = g.shape[0]
    return (_tri_dot(_iota2((n, n), 0) <= _iota2((n, n), 1), g),)


cumsum_rows.defvjp(_cumsum_fwd, _cumsum_bwd)


def _rot_matrix(transposed):
    i, j = _iota2((LANES, LANES), 0), _iota2((LANES, LANES), 1)
    if transposed:
        i, j = j, i
    half = MLA_ROPE // 2
    plus = (i == j - half) & (j >= half) & (j < MLA_ROPE)
    minus = (i == j + half) & (j < half)
    return (plus.astype(F32) - minus.astype(F32)).astype(BF16)


def _rot_dot(x, transposed):
    return sum(_dot(piece, _rot_matrix(transposed), 1, 0) for piece in _split3(x))


@jax.custom_vjp
def rot_half(x):
    return _rot_dot(x, False)


def _rot_fwd(x):
    return rot_half(x), None


def _rot_bwd(_, g):
    return (_rot_dot(g, True),)


rot_half.defvjp(_rot_fwd, _rot_bwd)


def _shift_rows_raw(x, s):
    n = x.shape[0]
    row = _iota2(x.shape, 0)
    rolled = pltpu.roll(x, s % n, 0)
    keep = (row >= s) if s > 0 else (row < n + s)
    return jnp.where(keep, rolled, 0.0)


@functools.partial(jax.custom_vjp, nondiff_argnums=(1,))
def shift_rows(x, s):
    return _shift_rows_raw(x, s)


def _shift_fwd(x, s):
    return _shift_rows_raw(x, s), None


def _shift_bwd(s, _, g):
    return (_shift_rows_raw(g, -s),)


shift_rows.defvjp(_shift_fwd, _shift_bwd)


def _layer_norm(a, g, b):
    mu = jnp.mean(a, -1, keepdims=True)
    xc = a - mu
    var = jnp.mean(xc * xc, -1, keepdims=True)
    return xc * lax.rsqrt(var + LN_EPS) * g + b


def _rms_norm(a, g):
    return a * lax.rsqrt(jnp.mean(a * a, -1, keepdims=True) + RMS_EPS) * g


def _log_sigmoid(z):
    return jnp.minimum(z, 0.0) - jnp.log(1.0 + jnp.exp(-jnp.abs(z)))


def _matmul(a, b, *, name, ta=False, tb=False, tm=512, tn=512, a_fn=None, epi=None, epi_ins=(), out_dtypes=(BF16,),
            b_at=None, out_at=None, out_buf=None, after=None, n_row_sums=0, a_ins=(), a_out_dtypes=()):
    m = a.shape[1] if ta else a.shape[0]
    k = a.shape[0] if ta else a.shape[1]
    if b_at is None:
        n, kb = (b.shape[0], b.shape[1]) if tb else (b.shape[1], b.shape[0])
    else:
        rb, r = b_at
        n, kb = (N_DEV * r, b.shape[2]) if tb else (b.shape[2], N_DEV * r)
    assert kb == k, (name, a.shape, b.shape, k, kb)
    tm, tn = min(tm, m), min(tn, n)
    assert m % tm == 0 and n % tn == 0, (name, m, n, tm, tn)
    a_spec = pl.BlockSpec((k, tm), lambda i, j: (0, i)) if ta else pl.BlockSpec((tm, k), lambda i, j: (i, 0))
    if b_at is None:
        b_spec = pl.BlockSpec((tn, k), lambda i, j: (j, 0)) if tb else pl.BlockSpec((k, tn), lambda i, j: (0, j))
        load_b = lambda ref: ref[...]
    elif tb and tn == n:
        b_spec = pl.BlockSpec((N_DEV, r, k), lambda i, j: (0, rb, 0))
        load_b = lambda ref: ref[...].reshape(n, k)
    elif tb:
        assert tn == r, (name, tn, r)
        b_spec = pl.BlockSpec((1, r, k), lambda i, j: (j, rb, 0))
        load_b = lambda ref: ref[0]
    else:
        b_spec = pl.BlockSpec((N_DEV, r, tn), lambda i, j: (0, rb, j))
        load_b = lambda ref: ref[...].reshape(k, tn)
    e_specs = []
    for e in epi_ins:
        if e.shape == (1, n):
            e_specs.append(pl.BlockSpec((1, tn), lambda i, j: (0, j)))
        else:
            assert e.shape == (m, n), (name, e.shape, m, n)
            e_specs.append(pl.BlockSpec((tm, tn), lambda i, j: (i, j)))
    n_epi, n_ain, n_aout = len(epi_ins), len(a_ins), len(a_out_dtypes)
    assert n_aout == 0 or (tn == n and not ta and out_at is None), name
    ca, cb = (0 if ta else 1), (1 if tb else 0)
    operands = [a, b, *a_ins, *epi_ins]
    in_specs = [a_spec, b_spec, *[a_spec] * n_ain, *e_specs]
    if out_at is None:
        assert n_row_sums == 0 or tn == n, (name, tn, n)
        out_specs = [pl.BlockSpec((tm, tn), lambda i, j: (i, j)) for _ in out_dtypes]
        out_specs += [pl.BlockSpec((tm, k), lambda i, j: (i, 0))] * n_aout
        out_specs += [pl.BlockSpec((1, n), lambda i, j: (0, 0))] * n_row_sums
        out_shape = [jax.ShapeDtypeStruct((m, n), dt) for dt in out_dtypes]
        out_shape += [jax.ShapeDtypeStruct((m, k), dt) for dt in a_out_dtypes]
        out_shape += [jax.ShapeDtypeStruct((1, n), F32)] * n_row_sums
        aliases, n_buf = {}, 0
    else:
        orb, orows = out_at
        assert len(out_dtypes) == 1 and m == N_DEV * orows and n == out_buf.shape[2], (name, m, n)
        if tm > orows:
            assert tm % orows == 0, (name, tm, orows)
            out_specs = [pl.BlockSpec((tm // orows, orows, tn), lambda i, j: (i, orb, j))]
        else:
            per = orows // tm
            out_specs = [pl.BlockSpec((1, tm, tn), lambda i, j: (i // per, orb * per + i % per, j))]
        out_shape = [jax.ShapeDtypeStruct(out_buf.shape, out_buf.dtype)]
        operands.append(out_buf)
        in_specs.append(pl.BlockSpec(memory_space=pl.ANY))
        aliases, n_buf = {len(operands) - 1: 0}, 1
    for dep in ([] if after is None else after if isinstance(after, (list, tuple)) else [after]):
        if dep is not None:
            operands.append(dep)
            in_specs.append(pl.BlockSpec(memory_space=pl.ANY))
            n_buf += 1

    def body(a_ref, b_ref, *rest):
        av, a_outs = a_ref[...], ()
        if a_fn is not None:
            av = a_fn(av, *[r_[...] for r_ in rest[:n_ain]])
            if n_aout:
                av, *a_outs = av
        acc = _dot(av.astype(BF16), load_b(b_ref).astype(BF16), ca, cb)
        outs = epi(acc, *[r_[...] for r_ in rest[n_ain:n_ain + n_epi]]) if epi is not None else (acc,)
        o_refs = rest[n_ain + n_epi + n_buf:]
        n_tiles = len(o_refs) - n_row_sums - n_aout
        for o_ref, val in zip(o_refs[:n_tiles + n_aout], (*outs[:n_tiles], *a_outs)):
            o_ref[...] = val.astype(o_ref.dtype).reshape(o_ref.shape)
        if n_row_sums:
            @pl.when(pl.program_id(0) == 0)
            def _():
                for o_ref in o_refs[n_tiles + n_aout:]:
                    o_ref[...] = jnp.zeros_like(o_ref)

            for o_ref, val in zip(o_refs[n_tiles + n_aout:], outs[n_tiles:]):
                o_ref[...] += val

    outs = pl.pallas_call(
        body, name=name, grid=(m // tm, n // tn), in_specs=in_specs, out_specs=out_specs, out_shape=out_shape,
        input_output_aliases=aliases, compiler_params=_params(),
    )(*operands)
    return outs[0] if len(outs) == 1 else tuple(outs)


def _tile_fwd(f, tiled, params, out_dtypes, *, tm, name):
    t = tiled[0].shape[0]
    assert t % tm == 0
    out_avals = jax.eval_shape(f, *[jax.ShapeDtypeStruct((tm, x.shape[1]), F32) for x in tiled],
                               *[jax.ShapeDtypeStruct(p.shape, F32) for p in params])
    nt, npar = len(tiled), len(params)

    def body(*refs):
        ins = [r[...].astype(F32) for r in refs[:nt + npar]]
        outs = f(*ins)
        for o_ref, val in zip(refs[nt + npar:], outs):
            o_ref[...] = val.astype(o_ref.dtype)

    return pl.pallas_call(
        body, name=name, grid=(t // tm,),
        in_specs=[pl.BlockSpec((tm, x.shape[1]), lambda i: (i, 0)) for x in tiled]
        + [pl.BlockSpec(p.shape, lambda i: (0, 0)) for p in params],
        out_specs=[pl.BlockSpec((tm, o.shape[1]), lambda i: (i, 0)) for o in out_avals],
        out_shape=[jax.ShapeDtypeStruct((t, o.shape[1]), dt) for o, dt in zip(out_avals, out_dtypes)],
        compiler_params=_params(),
    )(*tiled, *params)


def _tile_bwd(f, tiled, params, cots, d_tiled_dtypes, *, tm, name, diff_tiled=None):
    t = tiled[0].shape[0]
    assert t % tm == 0
    nt, npar, nc = len(tiled), len(params), len(cots)
    diff_tiled = list(range(nt)) if diff_tiled is None else diff_tiled

    def body(*refs):
        ins = [r[...].astype(F32) for r in refs[:nt + npar]]
        cts = [r[...].astype(F32) for r in refs[nt + npar:nt + npar + nc]]
        o_refs = refs[nt + npar + nc:]
        _, vjp = jax.vjp(f, *ins)
        grads = vjp(tuple(cts))
        for o_ref, idx in zip(o_refs[:len(diff_tiled)], diff_tiled):
            o_ref[...] = grads[idx].astype(o_ref.dtype)
        p_refs = o_refs[len(diff_tiled):]

        @pl.when(pl.program_id(0) == 0)
        def _():
            for p_ref in p_refs:
                p_ref[...] = jnp.zeros_like(p_ref)

        for p_ref, gp in zip(p_refs, grads[nt:]):
            p_ref[...] += gp

    outs = pl.pallas_call(
        body, name=name, grid=(t // tm,),
        in_specs=[pl.BlockSpec((tm, x.shape[1]), lambda i: (i, 0)) for x in tiled]
        + [pl.BlockSpec(p.shape, lambda i: (0, 0)) for p in params]
        + [pl.BlockSpec((tm, c.shape[1]), lambda i: (i, 0)) for c in cots],
        out_specs=[pl.BlockSpec((tm, tiled[idx].shape[1]), lambda i: (i, 0)) for idx in diff_tiled]
        + [pl.BlockSpec(p.shape, lambda i: (0, 0)) for p in params],
        out_shape=[jax.ShapeDtypeStruct(tiled[idx].shape, dt) for idx, dt in zip(diff_tiled, d_tiled_dtypes)]
        + [jax.ShapeDtypeStruct(p.shape, F32) for p in params],
        compiler_params=_params(),
    )(*tiled, *params, *cots)
    return outs[:len(diff_tiled)], outs[len(diff_tiled):]


def _gla_head(q, k, v, r, z, g, st):
    c = q.shape[0]
    causal = _iota2((c, c), 0) >= _iota2((c, c), 1)
    la = _log_sigmoid(z) * (1.0 / GLA_TAU)
    big_l = cumsum_rows(la)
    ep, en = jnp.exp(big_l), jnp.exp(-big_l)
    qs = q * (GLA_DK ** -0.5)
    qp = qs * ep
    s = jnp.where(causal, mm_nt(qp, k * en), mm_nt(qs * en, k * ep))
    o = mm_nn(s, v) + mm_nt(qp, st)
    l_end = jnp.sum(la, axis=0, keepdims=True)
    st_new = st * jnp.exp(l_end) + mm_tn(v, k * jnp.exp(l_end - big_l))
    u = _rms_norm(o, g) * (r * jax.nn.sigmoid(r))
    return u, st_new


def _gla_slices(h):
    q = slice(GLA_DK * h, GLA_DK * (h + 1))
    k = slice(GLA_HK + GLA_DK * h, GLA_HK + GLA_DK * (h + 1))
    v = slice(2 * GLA_HK + GLA_DV * h, 2 * GLA_HK + GLA_DV * (h + 1))
    r = slice(2 * GLA_HK + GLA_HV + GLA_DV * h, 2 * GLA_HK + GLA_HV + GLA_DV * (h + 1))
    return q, k, v, r


GLA_CHUNKS_PER_STEP = 4


def _gla_fwd(proj, z, norm_g, after):
    t = proj.shape[0]
    nc, per = t // CHUNK, GLA_CHUNKS_PER_STEP
    rows_per_step = per * CHUNK
    after = [a for a in after if a is not None]

    def body(proj_ref, z_ref, g_ref, *rest):
        u_ref, st_save_ref, st_ref = rest[len(after):]

        @pl.when(pl.program_id(0) == 0)
        def _():
            st_ref[...] = jnp.zeros_like(st_ref)

        g = g_ref[...]
        for h in range(GLA_HEADS):
            sq, sk, sv, sr = _gla_slices(h)
            st = st_ref[h]
            for c in range(per):
                rows = slice(c * CHUNK, (c + 1) * CHUNK)
                st_save_ref[c, h] = st
                u, st = _gla_head(proj_ref[rows, sq].astype(F32), proj_ref[rows, sk].astype(F32),
                                  proj_ref[rows, sv].astype(F32), proj_ref[rows, sr].astype(F32),
                                  z_ref[rows, GLA_DK * h:GLA_DK * (h + 1)], g, st)
                u_ref[rows, GLA_DV * h:GLA_DV * (h + 1)] = u.astype(u_ref.dtype)
            st_ref[h] = st

    return pl.pallas_call(
        body, name="gla_fwd", grid=(nc // per,),
        in_specs=[pl.BlockSpec((rows_per_step, GLA_MAIN), lambda i: (i, 0)),
                  pl.BlockSpec((rows_per_step, GLA_HK), lambda i: (i, 0)), pl.BlockSpec((1, GLA_DV), lambda i: (0, 0))]
        + [pl.BlockSpec(memory_space=pl.ANY)] * len(after),
        out_specs=[pl.BlockSpec((rows_per_step, GLA_HV), lambda i: (i, 0)),
                   pl.BlockSpec((per, GLA_HEADS, GLA_DV, GLA_DK), lambda i: (i, 0, 0, 0))],
        out_shape=[jax.ShapeDtypeStruct((t, GLA_HV), BF16), jax.ShapeDtypeStruct((nc, GLA_HEADS, GLA_DV, GLA_DK), F32)],
        scratch_shapes=[pltpu.VMEM((GLA_HEADS, GLA_DV, GLA_DK), F32)],
        compiler_params=_params(),
    )(proj, z, norm_g, *after)


def _gla_bwd(proj, z, norm_g, states, du, after):
    t = proj.shape[0]
    nc, per = t // CHUNK, GLA_CHUNKS_PER_STEP
    rows_per_step = per * CHUNK
    n_steps = nc // per
    after = [a for a in after if a is not None]

    def body(proj_ref, z_ref, g_ref, st_in_ref, du_ref, *rest):
        dproj_ref, dz_ref, dg_ref, dzsum_ref, dst_ref = rest[len(after):]

        @pl.when(pl.program_id(0) == 0)
        def _():
            dst_ref[...] = jnp.zeros_like(dst_ref)
            dg_ref[...] = jnp.zeros_like(dg_ref)
            dzsum_ref[...] = jnp.zeros_like(dzsum_ref)

        g = g_ref[...]
        for h in range(GLA_HEADS):
            sq, sk, sv, sr = _gla_slices(h)
            dst = dst_ref[h]
            for c in reversed(range(per)):
                rows = slice(c * CHUNK, (c + 1) * CHUNK)
                ins = (proj_ref[rows, sq].astype(F32), proj_ref[rows, sk].astype(F32), proj_ref[rows, sv].astype(F32),
                       proj_ref[rows, sr].astype(F32), z_ref[rows, GLA_DK * h:GLA_DK * (h + 1)], g, st_in_ref[c, h])
                _, vjp = jax.vjp(_gla_head, *ins)
                dq, dk, dv, dr, dz, dg, dst = vjp((du_ref[rows, GLA_DV * h:GLA_DV * (h + 1)], dst))
                dproj_ref[rows, sq] = dq.astype(dproj_ref.dtype)
                dproj_ref[rows, sk] = dk.astype(dproj_ref.dtype)
                dproj_ref[rows, sv] = dv.astype(dproj_ref.dtype)
                dproj_ref[rows, sr] = dr.astype(dproj_ref.dtype)
                dz_ref[rows, GLA_DK * h:GLA_DK * (h + 1)] = dz
                dzsum_ref[:, GLA_DK * h:GLA_DK * (h + 1)] += jnp.sum(dz, axis=0, keepdims=True)
                dg_ref[...] += dg
            dst_ref[h] = dst

    rev = lambda i: (n_steps - 1 - i, 0)
    return pl.pallas_call(
        body, name="gla_bwd", grid=(n_steps,),
        in_specs=[pl.BlockSpec((rows_per_step, GLA_MAIN), rev), pl.BlockSpec((rows_per_step, GLA_HK), rev),
                  pl.BlockSpec((1, GLA_DV), lambda i: (0, 0)),
                  pl.BlockSpec((per, GLA_HEADS, GLA_DV, GLA_DK), lambda i: (n_steps - 1 - i, 0, 0, 0)),
                  pl.BlockSpec((rows_per_step, GLA_HV), rev)] + [pl.BlockSpec(memory_space=pl.ANY)] * len(after),
        out_specs=[pl.BlockSpec((rows_per_step, GLA_MAIN), rev), pl.BlockSpec((rows_per_step, GLA_HK), rev),
                   pl.BlockSpec((1, GLA_DV), lambda i: (0, 0)), pl.BlockSpec((1, GLA_HK), lambda i: (0, 0))],
        out_shape=[jax.ShapeDtypeStruct((t, GLA_MAIN), BF16), jax.ShapeDtypeStruct((t, GLA_HK), F32),
                   jax.ShapeDtypeStruct((1, GLA_DV), F32), jax.ShapeDtypeStruct((1, GLA_HK), F32)],
        scratch_shapes=[pltpu.VMEM((GLA_HEADS, GLA_DV, GLA_DK), F32)],
        compiler_params=_params(),
    )(proj, z, norm_g, states, du, *after)


def _mla_pre(cq, cos, sin, gq, gkv, w_uq, w_ukv):
    qlat = _rms_norm(cq[:, :MLA_RANK], gq)
    kvlat = _rms_norm(cq[:, MLA_RANK:2 * MLA_RANK], gkv)
    kr = cq[:, 2 * MLA_RANK:]
    q = mm_nn(qlat, w_uq) * ((MLA_NOPE + MLA_ROPE) ** -0.5)
    kv = mm_nn(kvlat, w_ukv)
    pieces = []
    for h in range(MLA_HEADS):
        qr = q[:, MLA_HEAD_PAD * h + MLA_NOPE:MLA_HEAD_PAD * (h + 1)]
        pieces += [q[:, MLA_HEAD_PAD * h:MLA_HEAD_PAD * h + MLA_NOPE], qr * cos + rot_half(qr) * sin]
    return jnp.concatenate(pieces, axis=1), kv, kr * cos + rot_half(kr) * sin


MLA_Q_TILE = 512


def _mla_attn_block(qn, qr, kv, kr, q0):
    tq, nk = qn.shape[0], kv.shape[0]
    s = mm_nt(qn, kv[:, :MLA_NOPE]) + mm_nt(qr, kr)
    visible = (_iota2((tq, nk), 1) // CHUNK) <= ((q0 + _iota2((tq, nk), 0)) // CHUNK)
    s = jnp.where(visible, s, -1e30)
    e = jnp.exp(s - jnp.max(s, -1, keepdims=True))
    p = e / jnp.sum(e, -1, keepdims=True)
    return mm_nn(p, kv[:, MLA_NOPE:])


def _mla_attn_fwd(q, kv, kr, after):
    t = q.shape[0]
    after = [a for a in after if a is not None]

    def body(q_ref, kv_ref, kr_ref, *rest):
        (o_ref,) = rest[len(after):]
        for i in range(t // MLA_Q_TILE):
            rows = slice(i * MLA_Q_TILE, (i + 1) * MLA_Q_TILE)
            keys = slice(0, (i + 1) * MLA_Q_TILE)
            o = _mla_attn_block(q_ref[rows, :MLA_NOPE].astype(F32), q_ref[rows, MLA_NOPE:].astype(F32),
                                kv_ref[keys, :].astype(F32), kr_ref[keys, :].astype(F32), i * MLA_Q_TILE)
            o_ref[rows, :] = o.astype(o_ref.dtype)

    return pl.pallas_call(
        body, name="mla_attn_fwd", grid=(MLA_HEADS,),
        in_specs=[pl.BlockSpec((t, MLA_HEAD_PAD), lambda h: (0, h)),
                  pl.BlockSpec((t, MLA_NOPE + MLA_V), lambda h: (0, h)), pl.BlockSpec((t, LANES), lambda h: (0, 0))]
        + [pl.BlockSpec(memory_space=pl.ANY)] * len(after),
        out_specs=pl.BlockSpec((t, MLA_V), lambda h: (0, h)),
        out_shape=jax.ShapeDtypeStruct((t, MLA_HEADS * MLA_V), BF16),
        compiler_params=_params(),
    )(q, kv, kr, *after)


def _mla_attn_bwd(q, kv, kr, do, after):
    t = q.shape[0]
    after = [a for a in after if a is not None]

    def body(q_ref, kv_ref, kr_ref, do_ref, *rest):
        dq_ref, dkv_ref, dkr_ref = rest[len(after):]
        dkv_ref[...] = jnp.zeros_like(dkv_ref)

        @pl.when(pl.program_id(0) == 0)
        def _():
            dkr_ref[...] = jnp.zeros_like(dkr_ref)

        for i in range(t // MLA_Q_TILE):
            rows = slice(i * MLA_Q_TILE, (i + 1) * MLA_Q_TILE)
            keys = slice(0, (i + 1) * MLA_Q_TILE)
            f = functools.partial(_mla_attn_block, q0=i * MLA_Q_TILE)
            _, vjp = jax.vjp(f, q_ref[rows, :MLA_NOPE].astype(F32), q_ref[rows, MLA_NOPE:].astype(F32),
                             kv_ref[keys, :].astype(F32), kr_ref[keys, :].astype(F32))
            dqn, dqr, dkv, dkr = vjp(do_ref[rows, :].astype(F32))
            dq_ref[rows, :MLA_NOPE] = dqn
            dq_ref[rows, MLA_NOPE:] = dqr
            dkv_ref[keys, :] += dkv
            dkr_ref[keys, :] += dkr

    return pl.pallas_call(
        body, name="mla_attn_bwd", grid=(MLA_HEADS,),
        in_specs=[pl.BlockSpec((t, MLA_HEAD_PAD), lambda h: (0, h)),
                  pl.BlockSpec((t, MLA_NOPE + MLA_V), lambda h: (0, h)), pl.BlockSpec((t, LANES), lambda h: (0, 0)),
                  pl.BlockSpec((t, MLA_V), lambda h: (0, h))] + [pl.BlockSpec(memory_space=pl.ANY)] * len(after),
        out_specs=[pl.BlockSpec((t, MLA_HEAD_PAD), lambda h: (0, h)),
                   pl.BlockSpec((t, MLA_NOPE + MLA_V), lambda h: (0, h)), pl.BlockSpec((t, LANES), lambda h: (0, 0))],
        out_shape=[jax.ShapeDtypeStruct(q.shape, F32), jax.ShapeDtypeStruct(kv.shape, F32),
                   jax.ShapeDtypeStruct(kr.shape, F32)],
        compiler_params=_params(),
    )(q, kv, kr, do, *after)


def _rope_tables(pos_col, inv_freq_row):
    t = pos_col.shape[0]

    def body(pos_ref, f_ref, cos_ref, sin_ref):
        ang = pos_ref[...].astype(F32) * f_ref[...]
        live = _iota2(ang.shape, 1) < MLA_ROPE
        cos_ref[...] = jnp.where(live, jnp.cos(ang), 0.0)
        sin_ref[...] = jnp.where(live, jnp.sin(ang), 0.0)

    return pl.pallas_call(
        body, name="rope_tables", out_shape=[jax.ShapeDtypeStruct((t, LANES), F32)] * 2, compiler_params=_params(),
    )(pos_col, inv_freq_row)


CONV_COL_TILE = 256


def _conv_gate(b, c, u, w0, w1, w2):
    cu = c * u
    return b * (w2 * cu + w1 * shift_rows(cu, 1) + w0 * shift_rows(cu, 2))


def _conv_specs(t):
    nb = D_MODEL // CONV_COL_TILE
    return [pl.BlockSpec((t, CONV_COL_TILE), lambda j, part=part: (0, part * nb + j)) for part in range(3)]


def _conv_fwd(bcu, w, after):
    t = bcu.shape[0]
    after = [a for a in after if a is not None]

    def body(b_ref, c_ref, u_ref, w_ref, *rest):
        (o_ref,) = rest[len(after):]
        o_ref[...] = _conv_gate(b_ref[...], c_ref[...], u_ref[...], w_ref[0:1, :], w_ref[1:2, :],
                                w_ref[2:3, :]).astype(o_ref.dtype)

    return pl.pallas_call(
        body, name="conv_fwd", grid=(D_MODEL // CONV_COL_TILE,),
        in_specs=_conv_specs(t) + [pl.BlockSpec((3, CONV_COL_TILE), lambda j: (0, j))]
        + [pl.BlockSpec(memory_space=pl.ANY)] * len(after),
        out_specs=pl.BlockSpec((t, CONV_COL_TILE), lambda j: (0, j)),
        out_shape=jax.ShapeDtypeStruct((t, D_MODEL), BF16), compiler_params=_params(),
    )(bcu, bcu, bcu, w, *after)


def _conv_bwd(bcu, w, dout, after):
    t = bcu.shape[0]
    after = [a for a in after if a is not None]

    def body(b_ref, c_ref, u_ref, w_ref, do_ref, *rest):
        db_ref, dc_ref, du_ref, dw_ref = rest[len(after):]
        _, vjp = jax.vjp(_conv_gate, b_ref[...], c_ref[...], u_ref[...], w_ref[0:1, :], w_ref[1:2, :], w_ref[2:3, :])
        db, dc, du, dw0, dw1, dw2 = vjp(do_ref[...])
        db_ref[...] = db.astype(db_ref.dtype)
        dc_ref[...] = dc.astype(dc_ref.dtype)
        du_ref[...] = du.astype(du_ref.dtype)
        dw_ref[0:1, :] = dw0
        dw_ref[1:2, :] = dw1
        dw_ref[2:3, :] = dw2

    col = pl.BlockSpec((t, CONV_COL_TILE), lambda j: (0, j))
    return pl.pallas_call(
        body, name="conv_bwd", grid=(D_MODEL // CONV_COL_TILE,),
        in_specs=_conv_specs(t) + [pl.BlockSpec((3, CONV_COL_TILE), lambda j: (0, j)), col]
        + [pl.BlockSpec(memory_space=pl.ANY)] * len(after),
        out_specs=[col, col, col, pl.BlockSpec((3, CONV_COL_TILE), lambda j: (0, j))],
        out_shape=[jax.ShapeDtypeStruct((t, D_MODEL), BF16)] * 3 + [jax.ShapeDtypeStruct((3, D_MODEL), F32)],
        compiler_params=_params(),
    )(bcu, bcu, bcu, w, dout, *after)


def _loss_head(y, target):
    t, d = y.shape
    tm = 256

    def body(y_ref, t_ref, loss_ref, dy_ref):
        @pl.when(pl.program_id(0) == 0)
        def _():
            loss_ref[...] = jnp.zeros_like(loss_ref)

        err = y_ref[...] - t_ref[...]
        dy_ref[...] = err * (1.0 / d)
        loss_ref[...] += 0.5 * jnp.sum(jnp.sum(err * err, axis=-1, keepdims=True) * (1.0 / d))

    tile = pl.BlockSpec((tm, d), lambda i: (i, 0))
    return pl.pallas_call(
        body, name="loss_head", grid=(t // tm,), in_specs=[tile, tile],
        out_specs=[pl.BlockSpec((8, LANES), lambda i: (0, 0)), tile],
        out_shape=[jax.ShapeDtypeStruct((8, LANES), F32), jax.ShapeDtypeStruct((t, d), F32)],
        compiler_params=_params(),
    )(y, target)


def _ln_epi(acc, res, g, b):
    a = ALPHA * res + acc
    y = _layer_norm(a, g, b)
    return a, y, y


def _ln_fn(a, g, b):
    return (_layer_norm(a, g, b),)


def _ln_bwd_epi(scale):
    def epi(acc, res, a, g, b):
        _, vjp = jax.vjp(_ln_fn, a, g, b)
        da, dg, db = vjp((acc + scale * res,))
        return da, da, dg, db
    return epi


def _relu_sq(h):
    r = jnp.maximum(h, 0)
    return r * r


MLP_ROW_TILE = 256


def _mlp_fwd(x1b, x1, wa, g, b, after):
    t = x1b.shape[0]
    after = [a for a in after if a is not None]

    def body(xb_ref, x_ref, w1_ref, w2_ref, g_ref, b_ref, *rest):
        hh_ref, a_ref, y_ref, yb_ref = rest[len(after):]
        hh = _dot(xb_ref[...], w1_ref[...].reshape(D_FF, D_MODEL), 1, 1).astype(BF16)
        hh_ref[...] = hh
        acc = _dot(_relu_sq(hh), w2_ref[...].reshape(D_FF, D_MODEL), 1, 0)
        a, y, _ = _ln_epi(acc, x_ref[...], g_ref[...], b_ref[...])
        a_ref[...] = a
        y_ref[...] = y
        yb_ref[...] = y.astype(BF16)

    rows = lambda cols: pl.BlockSpec((MLP_ROW_TILE, cols), lambda i: (i, 0))
    region = lambda reg: pl.BlockSpec((N_DEV, reg[1], D_MODEL), lambda i: (0, reg[0], 0))
    vec = pl.BlockSpec((1, D_MODEL), lambda i: (0, 0))
    return pl.pallas_call(
        body, name="mlp_fwd", grid=(t // MLP_ROW_TILE,),
        in_specs=[rows(D_MODEL), rows(D_MODEL), region(REG_W1T), region(REG_W2), vec, vec]
        + [pl.BlockSpec(memory_space=pl.ANY)] * len(after),
        out_specs=[rows(D_FF), rows(D_MODEL), rows(D_MODEL), rows(D_MODEL)],
        out_shape=[jax.ShapeDtypeStruct((t, D_FF), BF16), jax.ShapeDtypeStruct((t, D_MODEL), F32),
                   jax.ShapeDtypeStruct((t, D_MODEL), F32), jax.ShapeDtypeStruct((t, D_MODEL), BF16)],
        compiler_params=_params(),
    )(x1b, x1, wa, wa, g, b, *after)


def _pad_cols(w, n):
    return jnp.pad(w, ((0, 0), (0, n - w.shape[1])))


def _pad_rows(w, n):
    return jnp.pad(w, ((0, n - w.shape[0]), (0, 0)))


def _step(x, p, positions, target, small, comm):
    t = x.shape[0]
    w = small
    freqs = ROPE_BASE ** (-jnp.arange(0, MLA_ROPE // 2, dtype=F32) * (2.0 / MLA_ROPE))
    freq_row = jnp.concatenate([freqs, freqs, jnp.zeros((LANES - MLA_ROPE,), F32)])[None, :]
    cos, sin = _rope_tables(positions.reshape(t, 1), freq_row)

    saved = []
    xb = x.astype(BF16)
    for i in range(DEPTH):
        j, kind = i // 3, i % 3
        wl = comm.mixer_weights(i)
        s = {"x": xb, "wl": wl}
        tok = comm.at("fwd", i, "begin", x)
        if kind == 0:
            s["w_main"] = wl["gla_w_in_t"][:GLA_MAIN]
            s["w_lr"] = _pad_rows(wl["gla_w_in_t"][GLA_MAIN:], LANES)
            s["w_up"] = _pad_rows(w["gla_w_gate_up"][j], LANES).astype(BF16)
            s["proj"] = _matmul(xb, s["w_main"], name="gla_proj", tb=True, tn=1024, after=tok)
            s["glr"] = _matmul(xb, s["w_lr"], name="gla_lr", tb=True, out_dtypes=(F32,))
            s["z"] = _matmul(s["glr"], s["w_up"], name="gla_gate", epi=lambda acc, b: (acc + b,),
                             epi_ins=(w["gla_b_gate"][j][None, :],), out_dtypes=(F32,))
            tok = comm.at("fwd", i, "proj_done", s["z"]) or []
            s["u"], s["states"] = _gla_fwd(s["proj"], s["z"], w["gla_norm_g"][j][None, :], tok)
        elif kind == 1:
            s["cq"] = _matmul(xb, wl["mla_in"], name="mla_proj", tn=MLA_IN_PAD, b_at=REG_MLA_IN, out_dtypes=(F32,),
                              after=tok)
            s["pre_params"] = (w["mla_q_norm"][j][None, :], w["mla_kv_norm"][j][None, :], wl["mla_w_uq"], wl["mla_w_ukv"])
            s["q"], s["kv"], s["kr"] = _tile_fwd(_mla_pre, (s["cq"], cos, sin), s["pre_params"], (BF16, BF16, BF16),
                                                 tm=256, name="mla_pre_fwd")
            tok = comm.at("fwd", i, "proj_done", s["kv"]) or []
            s["u"] = _mla_attn_fwd(s["q"], s["kv"], s["kr"], tok)
        else:
            s["bcu"] = _matmul(xb, wl["conv"], name="conv_proj", tb=True, tm=256, tn=3 * D_MODEL, b_at=REG_CONV,
                               out_dtypes=(F32,), after=tok)
            tok = comm.at("fwd", i, "proj_done", s["bcu"]) or []
            s["u"] = _conv_fwd(s["bcu"], w["conv_w"][j], tok)
        g0, b0 = w["ln_g"][i, 0][None, :], w["ln_b"][i, 0][None, :]
        g1, b1 = w["ln_g"][i, 1][None, :], w["ln_b"][i, 1][None, :]
        wa, wb = s["wa"], _ = comm.slab_weights(i, s["u"])
        s["a1"], s["x1"], s["x1b"] = _matmul(s["u"], wa, name="mixer_out_ln", tm=256, tn=D_MODEL, b_at=REG_WOUT,
                                             epi=_ln_epi, epi_ins=(x, g0, b0), out_dtypes=(F32, F32, BF16))
        tok = comm.at("fwd", i, "mid", s["x1b"]) or []
        s["hh"], s["a2"], s["x2"], s["x2b"] = _mlp_fwd(s["x1b"], s["x1"], wa, g1, b1, tok)
        s["pp"] = _matmul(p[i], wb, name="ple_proj", tb=True, tn=D_MODEL, b_at=REG_WPT)
        tok = comm.at("fwd", i, "end", s["pp"])
        def ple_epi(acc, xr, pp):
            y = xr + jax.nn.sigmoid(acc) * pp.astype(F32)
            return y, y, acc

        x, xb, s["gt"] = _matmul(s["x2b"], wa, name="ple_gate", tn=1024, b_at=REG_WG, epi=ple_epi,
                                 epi_ins=(s["x2"], s["pp"]), out_dtypes=(F32, BF16, BF16), after=tok)
        saved.append(s)

    loss_part, dx = _loss_head(x, target)

    gw = {n: [None] * WEIGHTS[n][0][0] for n in SMALL + REPLICATED}
    ln_g_grads, ln_b_grads = [[None, None] for _ in range(DEPTH)], [[None, None] for _ in range(DEPTH)]
    resid = lambda acc, r: (acc + ALPHA * r,)
    plus = lambda acc, r: (acc + r,)
    for i in reversed(range(DEPTH)):
        j, kind = i // 3, i % 3
        s = saved[i]
        wa = s["wa"]
        ga = lax.empty((N_DEV, A_ROWS, D_MODEL), BF16)
        gb = lax.empty((N_DEV, REG_WPT[1], PLE_DIM), BF16)
        layer_grads = {}
        tok = comm.at("bwd", i, "begin", dx)

        def ple_bwd(dxo, gt, pp):
            sg = jax.nn.sigmoid(gt.astype(F32))
            d_gt = dxo * pp.astype(F32) * sg * (1.0 - sg)
            return d_gt, d_gt, dxo * sg

        g1, b1 = w["ln_g"][i, 1][None, :], w["ln_b"][i, 1][None, :]
        d_a2, d_a2b, d_gt, d_pp, ln_g_grads[i][1], ln_b_grads[i][1] = _matmul(
            dx, wa, name="ple_gate_dx_ln", tb=True, tm=256, tn=D_MODEL, b_at=REG_WG, a_fn=ple_bwd,
            a_ins=(s["gt"], s["pp"]), a_out_dtypes=(BF16, BF16), epi=_ln_bwd_epi(1.0), epi_ins=(dx, s["a2"], g1, b1),
            out_dtypes=(F32, BF16), n_row_sums=2, after=tok)
        gb = _matmul(d_pp, p[i], name="ple_proj_dw", ta=True, tm=512, tn=PLE_DIM, out_at=REG_WPT, out_buf=gb)
        ga = _matmul(s["x2b"], d_gt, name="ple_gate_dw", ta=True, tm=512, tn=1024, out_at=REG_WG, out_buf=ga)
        tok = comm.at("bwd", i, "ln", d_a2)
        ga = _matmul(s["hh"], d_a2b, name="mlp_down_dw", ta=True, tm=REG_W2[1], tn=1024, a_fn=_relu_sq, out_at=REG_W2,
                     out_buf=ga, after=tok)
        d_hh = _matmul(d_a2b, wa, name="mlp_down_dx", tb=True, tm=256, tn=D_FF, b_at=REG_W2, after=[ga, gb],
                       epi=lambda acc, hh: (acc * 2.0 * jnp.maximum(hh.astype(F32), 0.0),), epi_ins=(s["hh"],))
        ga = _matmul(d_hh, s["x1b"], name="mlp_up_dw", ta=True, tm=REG_W1T[1], tn=1024, out_at=REG_W1T, out_buf=ga)
        g0, b0 = w["ln_g"][i, 0][None, :], w["ln_b"][i, 0][None, :]
        d_a1, d_a1b, ln_g_grads[i][0], ln_b_grads[i][0] = _matmul(
            d_hh, wa, name="mlp_up_dx_ln", tm=256, tn=D_MODEL, b_at=REG_W1T, epi=_ln_bwd_epi(ALPHA),
            epi_ins=(d_a2, s["a1"], g0, b0), out_dtypes=(F32, BF16), n_row_sums=2, after=ga)
        ga = _matmul(s["u"], d_a1b, name="mixer_out_dw", ta=True, tm=512, tn=1024, out_at=REG_WOUT, out_buf=ga)
        du = _matmul(d_a1b, wa, name="mixer_out_dx", tb=True, tn=1024, b_at=REG_WOUT, out_dtypes=(F32,), after=ga)
        comm.slab_grads(i, ga, gb)
        tok = comm.at("bwd", i, "slab_done", du) or []
        if kind == 0:
            dproj, dz, dg, dz_sum = _gla_bwd(s["proj"], s["z"], w["gla_norm_g"][j][None, :], s["states"], du, tok)
            tok = comm.at("bwd", i, "mixer_done", dproj)
            gw["gla_norm_g"][j] = dg[0]
            gw["gla_b_gate"][j] = dz_sum[0]
            gw["gla_w_gate_up"][j] = _matmul(s["glr"], dz, name="gla_gate_dw", ta=True, out_dtypes=(F32,),
                                             after=tok)[:GLA_RANK]
            dglr = _matmul(dz, s["w_up"], name="gla_gate_dx", tb=True, out_dtypes=(F32,))
            dw_main = _matmul(dproj, s["x"], name="gla_proj_dw", ta=True, tn=1024, out_dtypes=(F32,))
            dw_lr = _matmul(dglr, s["x"], name="gla_lr_dw", ta=True, tn=1024, out_dtypes=(F32,))[:GLA_RANK]
            layer_grads["gla_w_in_t"] = jnp.concatenate([dw_main, dw_lr], axis=0)
            dx = _matmul(dproj, s["w_main"], name="gla_proj_dx", tn=1024, epi=resid, epi_ins=(d_a1,),
                         out_dtypes=(F32,), after=[dw_main, dw_lr, gw["gla_w_gate_up"][j]])
            dx = _matmul(dglr, s["w_lr"], name="gla_lr_dx", tn=1024, epi=plus, epi_ins=(dx,), out_dtypes=(F32,))
        elif kind == 1:
            dq, dkv, dkr = _mla_attn_bwd(s["q"], s["kv"], s["kr"], du, tok)
            tok = comm.at("bwd", i, "mixer_done", dq)
            (d_cq,), (dgq, dgkv, layer_grads["mla_uq"], layer_grads["mla_ukv"]) = _tile_bwd(
                _mla_pre, (s["cq"], cos, sin), s["pre_params"], (dq, dkv, dkr), (BF16,), tm=256, name="mla_pre_bwd",
                diff_tiled=[0])
            gw["mla_q_norm"][j], gw["mla_kv_norm"][j] = dgq[0], dgkv[0]
            layer_grads["mla_in"] = _matmul(s["x"], d_cq, name="mla_proj_dw", ta=True, tm=512, tn=MLA_IN_PAD,
                                            out_at=REG_MLA_IN, after=tok,
                                            out_buf=lax.empty((N_DEV, REG_MLA_IN[1], MLA_IN_PAD), BF16))
            dx = _matmul(d_cq, s["wl"]["mla_in"], name="mla_proj_dx", tb=True, tn=1024, b_at=REG_MLA_IN, epi=resid,
                         epi_ins=(d_a1,), out_dtypes=(F32,), after=layer_grads["mla_in"])
        else:
            db, dc, du_, dcw = _conv_bwd(s["bcu"], w["conv_w"][j], du, tok)
            tok = comm.at("bwd", i, "mixer_done", db)
            gw["conv_w"][j] = dcw
            dbcu = jnp.concatenate([db, dc, du_], axis=1)
            layer_grads["conv"] = _matmul(dbcu, s["x"], name="conv_proj_dw", ta=True, tm=REG_CONV[1], tn=1024,
                                          out_at=REG_CONV, out_buf=lax.empty((N_DEV, REG_CONV[1], D_MODEL), BF16),
                                          after=tok)
            dx = _matmul(dbcu, s["wl"]["conv"], name="conv_proj_dx", tn=1024, b_at=REG_CONV, epi=resid, epi_ins=(d_a1,),
                         out_dtypes=(F32,), after=layer_grads["conv"])
        comm.mixer_grads(i, layer_grads)

    gw["ln_g"] = [jnp.concatenate([a, b], axis=0) for a, b in ln_g_grads]
    gw["ln_b"] = [jnp.concatenate([a, b], axis=0) for a, b in ln_b_grads]
    return loss_part, dx, {n: jnp.stack(gw[n]).astype(F32) for n in gw}


MESH_IDS = pl.DeviceIdType.MESH
ANY = pl.BlockSpec(memory_space=pl.ANY)
HBM_SPEC = pl.BlockSpec(memory_space=pltpu.HBM)
SEM_SPEC = pl.BlockSpec(memory_space=pltpu.SEMAPHORE)
DATAFLOW_EFFECT = pltpu.SideEffectType.DATAFLOW_SIDE_EFFECTING
CORE_COPIES, CHIP_COPIES = 4, 3


def _my_place():
    return lax.axis_index("x"), lax.axis_index("y"), lax.axis_index("c")


def _other_chips(mx, my):
    return [(1 - mx, my), (mx, 1 - my), (1 - mx, 1 - my)]


def _remote(src, dst, send_sems, recv_sems, k, to):
    return pltpu.make_async_remote_copy(src_ref=src, dst_ref=dst, send_sem=send_sems.at[k], recv_sem=recv_sems.at[k],
                                        device_id=to, device_id_type=MESH_IDS)


def _gather_first_copies(n_arr):
    def make(bufs, send_sems, recv_sems):
        mx, my, mc = _my_place()
        mine = 4 * mx + 2 * my + mc
        peers = [(mx, my, 1 - mc)] + [(cx, cy, mc) for cx, cy in _other_chips(mx, my)]
        return [_remote(bufs[a].at[mine], bufs[a].at[mine], send_sems, recv_sems, (1 + CHIP_COPIES) * a + k, to)
                for a in range(n_arr) for k, to in enumerate(peers)]
    return make, (1 + CHIP_COPIES) * n_arr


def _gather_forward_copies(n_arr):
    def make(bufs, send_sems, recv_sems):
        mx, my, mc = _my_place()
        blocks = [4 * cx + 2 * cy + mc for cx, cy in _other_chips(mx, my)]
        return [_remote(bufs[a].at[blk], bufs[a].at[blk], send_sems, recv_sems, CHIP_COPIES * a + k, (mx, my, 1 - mc))
                for a in range(n_arr) for k, blk in enumerate(blocks)]
    return make, CHIP_COPIES * n_arr


def _scatter_core_copies(n_arr):
    def make(bufs, send_sems, recv_sems):
        mx, my, mc = _my_place()
        return [_remote(bufs[a].at[2 * k + (1 - mc)], bufs[n_arr + a].at[k], send_sems, recv_sems, CORE_COPIES * a + k,
                        (mx, my, 1 - mc)) for a in range(n_arr) for k in range(CORE_COPIES)]
    return make, CORE_COPIES * n_arr


def _scatter_chip_copies(n_arr):
    def make(bufs, send_sems, recv_sems):
        mx, my, mc = _my_place()
        return [_remote(bufs[a].at[2 * cx + cy], bufs[n_arr + a].at[k], send_sems, recv_sems, CHIP_COPIES * a + k,
                        (cx, cy, mc)) for a in range(n_arr) for k, (cx, cy) in enumerate(_other_chips(mx, my))]
    return make, CHIP_COPIES * n_arr


def _exchange(name, bufs, copies):
    make, n_copies = copies
    n = len(bufs)

    def body(*refs):
        descs = make(refs[:n], refs[2 * n], refs[2 * n + 1])
        for cp in descs:
            cp.start()
        for cp in descs:
            cp.wait()

    return pl.pallas_call(
        body, name=name, out_shape=[jax.ShapeDtypeStruct(b.shape, b.dtype) for b in bufs], in_specs=[ANY] * n,
        out_specs=[ANY] * n, input_output_aliases={i: i for i in range(n)},
        scratch_shapes=[pltpu.SemaphoreType.DMA((n_copies,)), pltpu.SemaphoreType.DMA((n_copies,))],
    )(*bufs)


def _exchange_start(name, parts, after):
    sizes = [len(bufs) for bufs, _ in parts]
    all_bufs = [b for bufs, _ in parts for b in bufs]
    n, n_sems = len(all_bufs), 2 * len(parts)

    def body(*refs):
        offset = 0
        for p, ((_, (make, _)), size) in enumerate(zip(parts, sizes)):
            for cp in make(refs[offset:offset + size], refs[n + 1 + 2 * p], refs[n + 2 + 2 * p]):
                cp.start()
            offset += size
        refs[-1][...] = jnp.zeros_like(refs[-1])

    outs = pl.pallas_call(
        body, name=name,
        out_shape=(*[pltpu.SemaphoreType.DMA((n_copies,)) for _, (_, n_copies) in parts for _ in range(2)],
                   *[pltpu.HBM(b.shape, b.dtype) for b in all_bufs], jax.ShapeDtypeStruct((8, LANES), F32)),
        in_specs=[HBM_SPEC] * n + [ANY],
        out_specs=(*[SEM_SPEC] * n_sems, *[HBM_SPEC] * n, pl.BlockSpec(memory_space=pltpu.VMEM)),
        input_output_aliases={i: n_sems + i for i in range(n)},
        compiler_params=pltpu.CompilerParams(has_side_effects=DATAFLOW_EFFECT),
    )(*[pltpu.with_memory_space_constraint(b, pltpu.HBM) for b in all_bufs], after)
    started, offset = [], n_sems
    for p, size in enumerate(sizes):
        started.append(((outs[2 * p], outs[2 * p + 1]), list(outs[offset:offset + size])))
        offset += size
    return started, outs[-1]


def _exchange_wait(name, sems, bufs, copies, after):
    make, _ = copies
    n = len(bufs)

    def body(*refs):
        for cp in make(refs[:n], refs[n], refs[n + 1]):
            cp.wait_send()
            cp.wait_recv()

    return list(pl.pallas_call(
        body, name=name, out_shape=[pltpu.HBM(b.shape, b.dtype) for b in bufs],
        in_specs=[HBM_SPEC] * n + [SEM_SPEC, SEM_SPEC, ANY], out_specs=[HBM_SPEC] * n,
        input_output_aliases={i: i for i in range(n)},
        compiler_params=pltpu.CompilerParams(has_side_effects=DATAFLOW_EFFECT),
    )(*bufs, *sems, after))


SUM_TILE_BYTES = 4 * 1024 * 1024


def _row_tile(r, c):
    best = None
    for cand in range(16, r + 1, 16):
        if r % cand == 0 and cand * c * 2 <= SUM_TILE_BYTES:
            best = cand
    return r if best is None else best


def _pair_sum(g, recv, my_c):
    _, r, c = g.shape
    tr = _row_tile(r, c)

    def body(c_ref, g_ref, r_ref, o_ref):
        o_ref[...] = (g_ref[...].astype(F32) + r_ref[...].astype(F32)).astype(o_ref.dtype)

    return pl.pallas_call(
        body, name="rs_pair_sum", out_shape=jax.ShapeDtypeStruct((4, r, c), g.dtype),
        grid_spec=pltpu.PrefetchScalarGridSpec(
            num_scalar_prefetch=1, grid=(4, r // tr),
            in_specs=[pl.BlockSpec((1, tr, c), lambda n, i, cr: (2 * n + cr[0], i, 0)),
                      pl.BlockSpec((1, tr, c), lambda n, i, cr: (n, i, 0))],
            out_specs=pl.BlockSpec((1, tr, c), lambda n, i, cr: (n, i, 0))),
        compiler_params=_params(),
    )(my_c, g, recv)


def _chip_sum(h, recv, my_chip):
    _, r, c = h.shape
    tr = _row_tile(r, c)

    def body(j_ref, h_ref, r0_ref, r1_ref, r2_ref, o_ref):
        o_ref[...] = ((h_ref[0].astype(F32) + r0_ref[0].astype(F32)) + r1_ref[0].astype(F32)) + r2_ref[0].astype(F32)

    return pl.pallas_call(
        body, name="rs_chip_sum", out_shape=jax.ShapeDtypeStruct((r, c), F32),
        grid_spec=pltpu.PrefetchScalarGridSpec(
            num_scalar_prefetch=1, grid=(r // tr,),
            in_specs=[pl.BlockSpec((1, tr, c), lambda i, jr: (jr[0], i, 0))]
            + [pl.BlockSpec((1, tr, c), lambda i, jr, n=n: (n, i, 0)) for n in range(3)],
            out_specs=pl.BlockSpec((tr, c), lambda i, jr: (i, 0))),
        compiler_params=_params(),
    )(my_chip, h, recv, recv, recv)


def _sum_blocks(g):
    n, r, c = g.shape

    def body(g_ref, o_ref):
        acc = g_ref[0]
        for k in range(1, n):
            acc = acc + g_ref[k]
        o_ref[...] = acc

    return pl.pallas_call(body, name="sum_blocks", out_shape=jax.ShapeDtypeStruct((r, c), F32), compiler_params=_params())(g)


def _pack(flat_parts, cols, row_multiple, dtype):
    flat = jnp.concatenate([f.astype(dtype) for f in flat_parts])
    per_row_block = cols * row_multiple
    padded = -(-flat.shape[0] // per_row_block) * per_row_block
    return jnp.pad(flat, (0, padded - flat.shape[0])).reshape(padded // cols, cols)


def _shard_shape(name):
    shape, axis = WEIGHTS[name]
    if axis is None:
        return shape
    return tuple(s // N_DEV if a == axis else s for a, s in enumerate(shape))


def _size(shape):
    n = 1
    for s in shape:
        n *= s
    return n


def _unshard(blocks, name):
    _, axis = WEIGHTS[name]
    return jnp.concatenate([blocks[k] for k in range(N_DEV)], axis=axis)


def _unpack_blocks(flat, names):
    out, off = {}, 0
    for n in names:
        shp = _shard_shape(n)
        out[n] = flat[..., off:off + _size(shp)].reshape(flat.shape[:-1] + shp)
        off += _size(shp)
    return out


def _layer_slabs(shard, i):
    j, kind = i // 3, i % 3
    w_out = (shard["gla_w_out"], shard["mla_w_out"], shard["conv_w_out"])[kind][j]
    out = {"a": jnp.concatenate([shard["mlp_w2"][i], shard["mlp_w1"][i].T, w_out, shard["ple_w_gate"][i]], axis=0).astype(BF16),
           "b": shard["ple_w_proj"][i].T.astype(BF16)}
    if kind == 0:
        out["gla"] = shard["gla_w_in"][j].T.astype(BF16)
    elif kind == 1:
        out["mla_in"] = _pad_cols(shard["mla_w_in"][j], MLA_IN_PAD).astype(BF16)
        out["mla_uq"] = _pad_cols(shard["mla_w_uq"][j], MLA_HEAD_PAD).astype(BF16)
        out["mla_ukv"] = shard["mla_w_ukv"][j].astype(BF16)
    else:
        out["conv"] = shard["conv_w_in"][j].T.astype(BF16)
    return out


def _mixer_weights(landed, i):
    kind = i % 3
    if kind == 0:
        return {"gla_w_in_t": landed["gla"].reshape(-1, D_MODEL)}
    if kind == 2:
        return {"conv": landed["conv"]}
    heads_side_by_side = lambda g: g.transpose(1, 0, 2).reshape(g.shape[1], -1)
    return {"mla_in": landed["mla_in"], "mla_w_uq": heads_side_by_side(landed["mla_uq"]),
            "mla_w_ukv": heads_side_by_side(landed["mla_ukv"])}


def _mixer_grad_buffers(layer_grads, i):
    kind = i % 3
    if kind == 0:
        return {"gla": layer_grads["gla_w_in_t"].reshape(N_DEV, -1, D_MODEL).astype(BF16)}
    if kind == 2:
        return {"conv": layer_grads["conv"]}
    head_blocks = lambda g: g.reshape(g.shape[0], N_DEV, -1).transpose(1, 0, 2).astype(BF16)
    return {"mla_in": layer_grads["mla_in"], "mla_uq": head_blocks(layer_grads["mla_uq"]),
            "mla_ukv": head_blocks(layer_grads["mla_ukv"])}


SLAB_KEYS = ("a", "b")


class _Overlap:
    def __init__(self, shard, small_pack):
        mx, my, mc = _my_place()
        self.my_c = mc.astype(jnp.int32).reshape(1)
        self.my_chip = (2 * mx + my).astype(jnp.int32).reshape(1)
        mine = 4 * mx + 2 * my + mc
        def landing_of(slabs):
            return {k: lax.dynamic_update_index_in_dim(lax.empty((N_DEV, *v.shape), v.dtype), v, mine, 0)
                    for k, v in slabs.items()}

        first = _layer_slabs(shard, 0)
        first["small"] = small_pack
        self.landing = [landing_of(first)]
        self.fly = {}
        self.grads = [{} for _ in range(DEPTH)]
        self.reduced = [{} for _ in range(DEPTH)]
        tok = self._start_many([self._first_spec(0, "mixer"), self._first_spec(0, "slab")], shard["ln_g"])
        shard, tok = lax.optimization_barrier((shard, tok))
        self.landing += [landing_of(_layer_slabs(shard, i)) for i in range(1, DEPTH)]
        bufs = self._wait("ag_first_mixer_l0", tok)
        self.landing[0].update(zip(self._keys(self.landing[0], "mixer"),
                                   _exchange("ag_forward_mixer_l0", bufs, _gather_forward_copies(len(bufs)))))

    @staticmethod
    def _keys(names, group):
        return [k for k in names if (k in SLAB_KEYS) == (group == "slab")]

    def _start(self, name, bufs, copies, after):
        return self._start_many([(name, bufs, copies)], after)

    def _start_many(self, specs, after):
        call = specs[0][0] + ("" if len(specs) == 1 else f"_and_{len(specs) - 1}_more") + "_start"
        started, tok = _exchange_start(call, [(bufs, copies) for _, bufs, copies in specs], after)
        for (name, _, copies), (sems, bufs) in zip(specs, started):
            self.fly[name] = (sems, bufs, copies)
        return tok

    def _wait(self, name, after):
        sems, bufs, copies = self.fly.pop(name)
        return _exchange_wait(name + "_wait", sems, bufs, copies, after)

    def mixer_weights(self, i):
        return _mixer_weights(self.landing[i], i)

    def slab_weights(self, i, dep):
        self._gather_done(i, "slab", dep)
        return self.landing[i]["a"], self.landing[i]["b"]

    def slab_grads(self, i, ga, gb):
        self.grads[i].update(a=ga, b=gb)

    def mixer_grads(self, i, layer_grads):
        self.grads[i].update(_mixer_grad_buffers(layer_grads, i))

    def at(self, phase, i, point, dep):
        toks = []
        if phase == "fwd":
            if point == "begin" and i == 0:
                toks.append(self._start_many([self._first_spec(1, "mixer"), self._first_spec(1, "slab")],
                                             self.landing[0][self._keys(self.landing[0], "mixer")[0]]))
            if point == "proj_done":
                toks.append(self._gather_forward(i, "slab", dep))
            if point == "mid" and i + 1 < DEPTH:
                bufs = self._wait(f"ag_first_mixer_l{i + 1}", dep)
                specs = [(f"ag_forward_mixer_l{i + 1}", bufs, _gather_forward_copies(len(bufs)))]
                if i + 2 < DEPTH:
                    specs += [self._first_spec(i + 2, "mixer"), self._first_spec(i + 2, "slab")]
                toks.append(self._start_many(specs, dep))
            if point == "end" and i + 1 < DEPTH:
                self._gather_done(i + 1, "mixer", dep)
        else:
            if point == "begin" and i + 1 < DEPTH:
                toks.append(self._scatter_cores(i + 1, "mixer", dep))
            if point == "ln" and i + 1 < DEPTH:
                toks.append(self._scatter_chips(i + 1, "mixer", dep))
            if point == "slab_done":
                if i + 1 < DEPTH:
                    self._scatter_done(i + 1, "slab", dep)
                    self._scatter_done(i + 1, "mixer", dep)
                toks.append(self._scatter_cores(i, "slab", dep))
            if point == "mixer_done":
                toks.append(self._scatter_chips(i, "slab", dep))
        return toks or None

    def _first_spec(self, i, group):
        bufs = [self.landing[i][k] for k in self._keys(self.landing[i], group)]
        return f"ag_first_{group}_l{i}", bufs, _gather_first_copies(len(bufs))

    def _gather_forward(self, i, group, after):
        bufs = self._wait(f"ag_first_{group}_l{i}", after)
        return self._start(f"ag_forward_{group}_l{i}", bufs, _gather_forward_copies(len(bufs)), after)

    def _gather_done(self, i, group, after):
        keys = self._keys(self.landing[i], group)
        self.landing[i].update(zip(keys, self._wait(f"ag_forward_{group}_l{i}", after)))

    def _scatter_cores(self, i, group, after):
        gs = [self.grads[i][k] for k in self._keys(self.grads[i], group)]
        land = [lax.empty((4, *g.shape[1:]), g.dtype) for g in gs]
        return self._start(f"rs_cores_{group}_l{i}", gs + land, _scatter_core_copies(len(gs)), after)

    def _pair_sums(self, bufs):
        n = len(bufs) // 2
        hs = [_pair_sum(g, r, self.my_c) for g, r in zip(bufs[:n], bufs[n:])]
        return hs + [lax.empty((3, *h.shape[1:]), h.dtype) for h in hs]

    def _scatter_chips(self, i, group, after):
        bufs = self._pair_sums(self._wait(f"rs_cores_{group}_l{i}", after))
        return self._start(f"rs_chips_{group}_l{i}", bufs, _scatter_chip_copies(len(bufs) // 2), after)

    def _chip_sums(self, i, group, bufs):
        n = len(bufs) // 2
        for k, h, r in zip(self._keys(self.grads[i], group), bufs[:n], bufs[n:]):
            self.reduced[i][k] = _chip_sum(h, r, self.my_chip)

    def _scatter_done(self, i, group, after):
        self._chip_sums(i, group, self._wait(f"rs_chips_{group}_l{i}", after))

    def tail_begin(self, dep):
        return self._scatter_cores(0, "mixer", dep)

    def tail_middle(self, dep):
        self._scatter_done(0, "slab", dep)
        return self._scatter_chips(0, "mixer", dep)

    def tail_end(self, dep):
        self._scatter_done(0, "mixer", dep)


def _small_gather_start(x, name, after):
    mx, my, mc = _my_place()
    land = lax.dynamic_update_index_in_dim(lax.empty((N_DEV, *x.shape), x.dtype), x, 4 * mx + 2 * my + mc, 0)
    started, tok = _exchange_start(name + "_first_start", [([land], _gather_first_copies(1))], after)
    return name, started[0], tok


def _small_gather_finish(started, after):
    name, (sems, bufs), _ = started
    bufs = _exchange_wait(name + "_first_wait", sems, bufs, _gather_first_copies(1), after)
    return _exchange(name + "_forward", bufs, _gather_forward_copies(1))[0]


def _adamw_math(w, g, m, v):
    m2 = ADAM_B1 * m + (1.0 - ADAM_B1) * g
    v2 = ADAM_B2 * v + (1.0 - ADAM_B2) * (g * g)
    m_hat = m2 / (1.0 - ADAM_B1 ** ADAM_STEP)
    v_hat = v2 / (1.0 - ADAM_B2 ** ADAM_STEP)
    return -ADAM_LR * (m_hat / (jnp.sqrt(v_hat) + ADAM_EPS) + ADAM_WD * w), m2, v2


ADAMW_TILE_BYTES = 1024 * 1024


def _adamw_layer(name, w, m, v, j, g, g_at, transposed, chain, after):
    n_layers, r, c = w.shape
    tr = max(t for t in range(8, r + 1, 8) if r % t == 0 and (t * c * 4 <= ADAMW_TILE_BYTES or t == 8))
    rb, rows = g_at
    if transposed:
        assert rows == c and g.shape[1] == r, (name, g.shape, g_at)
        g_spec = pl.BlockSpec((rows, tr), lambda i: (rb, i))
    else:
        assert rows == r and g.shape[1] == c, (name, g.shape, g_at)
        g_spec = pl.BlockSpec((tr, c), lambda i: (rb * (r // tr) + i, 0))
    extra = list(chain or []) + [a for a in (after or []) if a is not None]
    n_chain = 4 if chain else 0

    def body(w_ref, m_ref, v_ref, g_ref, *rest):
        g_out, d_out, m_out, v_out, tok_ref = rest[len(extra):]
        gv = g_ref[...].T if transposed else g_ref[...]
        g_out[0] = gv
        d_out[0], m_out[0], v_out[0] = _adamw_math(w_ref[0], gv, m_ref[0], v_ref[0])
        tok_ref[...] = jnp.zeros_like(tok_ref)

    layer_spec = pl.BlockSpec((1, tr, c), lambda i: (j, i, 0))
    outs = pl.pallas_call(
        body, name=f"adamw_{name}_l{j}", grid=(r // tr,),
        in_specs=[layer_spec] * 3 + [g_spec] + [pl.BlockSpec(memory_space=pl.ANY)] * len(extra),
        out_specs=[layer_spec] * 4 + [pl.BlockSpec((8, LANES), lambda i: (0, 0))],
        out_shape=[jax.ShapeDtypeStruct(w.shape, F32)] * 4 + [jax.ShapeDtypeStruct((8, LANES), F32)],
        input_output_aliases={4 + k: k for k in range(n_chain)}, compiler_params=_params(),
    )(w, m, v, g, *extra)
    return list(outs[:4]), outs[4]


def _adamw(w, g, m, v, name):
    shape = w.shape
    cols = shape[-1]
    rows = _size(shape) // cols
    tr = rows
    for cand in (512, 256, 128, 64, 32, 16, 8):
        if rows > cand and rows % cand == 0:
            tr = cand
            break

    def body(w_ref, g_ref, m_ref, v_ref, d_ref, mo_ref, vo_ref):
        d_ref[...], mo_ref[...], vo_ref[...] = _adamw_math(w_ref[...], g_ref[...], m_ref[...], v_ref[...])

    spec = pl.BlockSpec((tr, cols), lambda i: (i, 0))
    outs = pl.pallas_call(
        body, name="adamw_" + name, grid=(rows // tr,), in_specs=[spec] * 4, out_specs=[spec] * 3,
        out_shape=[jax.ShapeDtypeStruct((rows, cols), F32)] * 3, compiler_params=_params(),
    )(*[a.reshape(rows, cols) for a in (w, g, m, v)])
    return [o.reshape(shape) for o in outs]


def kernel(x, p, positions, gla_w_in, gla_w_gate_up, gla_b_gate, gla_norm_g, gla_w_out, mla_w_in, mla_q_norm, mla_kv_norm, mla_w_uq, mla_w_ukv, mla_w_out, conv_w_in, conv_w, conv_w_out, ln_g, ln_b, mlp_w1, mlp_w2, ple_w_gate, ple_w_proj, loss_target, m_gla_w_in, m_gla_w_gate_up, m_gla_b_gate, m_gla_norm_g, m_gla_w_out, m_mla_w_in, m_mla_q_norm, m_mla_kv_norm, m_mla_w_uq, m_mla_w_ukv, m_mla_w_out, m_conv_w_in, m_conv_w, m_conv_w_out, m_ln_g, m_ln_b, m_mlp_w1, m_mlp_w2, m_ple_w_gate, m_ple_w_proj, v_gla_w_in, v_gla_w_gate_up, v_gla_b_gate, v_gla_norm_g, v_gla_w_out, v_mla_w_in, v_mla_q_norm, v_mla_kv_norm, v_mla_w_uq, v_mla_w_ukv, v_mla_w_out, v_conv_w_in, v_conv_w, v_conv_w_out, v_ln_g, v_ln_b, v_mlp_w1, v_mlp_w2, v_ple_w_gate, v_ple_w_proj):
    args = locals()
    shard = {n: args[n] for n in WEIGHT_NAMES}
    mom = {n: args["m_" + n] for n in WEIGHT_NAMES}
    var = {n: args["v_" + n] for n in WEIGHT_NAMES}
    mx, my, mc = _my_place()

    comm = _Overlap(shard, _pack([shard[n].reshape(-1) for n in SMALL], LANES, 8, F32))
    small_all = comm.landing[0]["small"]
    small = {n: shard[n] for n in REPLICATED}
    small.update({n: _unshard(blk, n) for n, blk in _unpack_blocks(small_all.reshape(N_DEV, -1), SMALL).items()})
    loss_part, grad_x, small_grads = _step(x[0], p[:, 0], positions[0], loss_target[0], small, comm)

    chains = {}

    def update(name, j, g, g_at, transposed, tok):
        chains[name], tok = _adamw_layer(name, shard[name], mom[name], var[name], j, g, g_at, transposed,
                                         chains.get(name), [tok])
        return tok

    def update_layer(i, groups, tok):
        j, kind = i // 3, i % 3
        red = comm.reduced[i]
        if "slab" in groups:
            tok = update("mlp_w2", i, red["a"], REG_W2, False, tok)
            tok = update("mlp_w1", i, red["a"], REG_W1T, True, tok)
            tok = update(("gla_w_out", "mla_w_out", "conv_w_out")[kind], j, red["a"], REG_WOUT, False, tok)
            tok = update("ple_w_gate", i, red["a"], REG_WG, False, tok)
            tok = update("ple_w_proj", i, red["b"], REG_WPT, True, tok)
        if "mixer" in groups:
            if kind == 0:
                tok = update("gla_w_in", j, red["gla"].T, (0, D_MODEL), False, tok)
            elif kind == 2:
                tok = update("conv_w_in", j, red["conv"], REG_CONV, True, tok)
            else:
                for n, g in (("mla_w_in", red["mla_in"][:, :MLA_IN]), ("mla_w_ukv", red["mla_ukv"]),
                             ("mla_w_uq", red["mla_uq"][:, :MLA_NOPE + MLA_ROPE])):
                    tok = update(n, j, g, (0, g.shape[0]), False, tok)
        return tok

    tok = comm.tail_begin(grad_x)
    tok = update_layer(3, ("slab", "mixer"), tok)
    tok = update_layer(2, ("slab", "mixer"), tok)
    tok = comm.tail_middle(tok)
    small_parts = [loss_part[0, :1]] + [small_grads[n].reshape(-1) for n in SMALL + REPLICATED]
    small_gather = _small_gather_start(_pack(small_parts, LANES, 8, F32), "ag_small_grads", tok)
    tok = update_layer(1, ("slab", "mixer"), small_gather[2])
    tok = update_layer(0, ("slab",), tok)
    comm.tail_end(tok)
    tok = update_layer(0, ("mixer",), tok)
    red_small = _sum_blocks(_small_gather_finish(small_gather, tok)).reshape(-1)
    loss = red_small[0]
    off = 1
    dev = 4 * mx + 2 * my + mc
    for n in SMALL + REPLICATED:
        shape, axis = WEIGHTS[n]
        full_g = red_small[off:off + _size(shape)].reshape(shape)
        off += _size(shape)
        if axis is not None:
            width = shape[axis] // N_DEV
            full_g = lax.dynamic_slice_in_dim(full_g, dev * width, width, axis=axis)
        chains[n] = [full_g, *_adamw(shard[n], full_g, mom[n], var[n], n)]
    return (loss, grad_x[None], *[chains[n][k] for k in range(4) for n in WEIGHT_NAMES])
```

```python
import functools

import jax
import jax.numpy as jnp
from jax import lax
from jax.experimental import pallas as pl
from jax.experimental.pallas import tpu as pltpu

F32, BF16 = jnp.float32, jnp.bfloat16
N_DEV = 8

D_MODEL = 1024
DEPTH = 4
CHUNK = 64
ALPHA = (2 * DEPTH) ** 0.25
LN_EPS = 1e-5
RMS_EPS = 1e-6
PLE_DIM = 256
D_FF = 4 * D_MODEL
GLA_HEADS = 4
GLA_DK = 128
GLA_DV = 256
GLA_RANK = 16
GLA_TAU = 16.0
GLA_HK = GLA_HEADS * GLA_DK
GLA_HV = GLA_HEADS * GLA_DV
GLA_MAIN = 2 * GLA_HK + GLA_HV + D_MODEL
MLA_HEADS = 8
MLA_NOPE = 128
MLA_ROPE = 64
MLA_V = 128
MLA_RANK = 256
MLA_IN = 2 * MLA_RANK + MLA_ROPE
MLA_IN_PAD = 640
ROPE_BASE = 10000.0
LANES = 128
ADAM_LR, ADAM_B1, ADAM_B2, ADAM_EPS, ADAM_WD, ADAM_STEP = 0.001, 0.9, 0.999, 1e-08, 0.01, 10

V7X_VMEM_LIMIT_BYTES = 56 * 1024 * 1024

WEIGHTS = {
    "gla_w_in": ((2, 1024, 3088), 2), "gla_w_gate_up": ((2, 16, 512), 2), "gla_b_gate": ((2, 512), 1),
    "gla_norm_g": ((2, 256), 1), "gla_w_out": ((2, 1024, 1024), 1), "mla_w_in": ((1, 1024, 576), 1),
    "mla_q_norm": ((1, 256), None), "mla_kv_norm": ((1, 256), None), "mla_w_uq": ((1, 256, 1536), 2),
    "mla_w_ukv": ((1, 256, 2048), 2), "mla_w_out": ((1, 1024, 1024), 1), "conv_w_in": ((1, 1024, 3072), 2),
    "conv_w": ((1, 3, 1024), 2), "conv_w_out": ((1, 1024, 1024), 1), "ln_g": ((4, 2, 1024), 2),
    "ln_b": ((4, 2, 1024), 2), "mlp_w1": ((4, 1024, 4096), 2), "mlp_w2": ((4, 4096, 1024), 1),
    "ple_w_gate": ((4, 1024, 1024), 1), "ple_w_proj": ((4, 256, 1024), 2),
}
WEIGHT_NAMES = list(WEIGHTS)
REG_W2, REG_W1T, REG_WOUT, REG_WG = (0, 512), (1, 512), (8, 128), (9, 128)
A_ROWS = 1280
REG_CONV = (0, 384)
REG_WPT = (0, 128)
REG_MLA_IN = (0, 128)
MLA_HEAD_PAD = 2 * LANES
SMALL = ["gla_w_gate_up", "gla_b_gate", "gla_norm_g", "conv_w", "ln_g", "ln_b"]
REPLICATED = ["mla_q_norm", "mla_kv_norm"]


def _params(**kw):
    return pltpu.CompilerParams(vmem_limit_bytes=V7X_VMEM_LIMIT_BYTES, **kw)


def _dot(a, b, ca, cb):
    return lax.dot_general(a, b, (((ca,), (cb,)), ((), ())), preferred_element_type=F32)


def _nn(a, b):
    return _dot(a.astype(BF16), b.astype(BF16), 1, 0)


def _nt(a, b):
    return _dot(a.astype(BF16), b.astype(BF16), 1, 1)


def _tn(a, b):
    return _dot(a.astype(BF16), b.astype(BF16), 0, 0)


@jax.custom_vjp
def mm_nn(a, b):
    return _nn(a, b)


def _mm_nn_fwd(a, b):
    return _nn(a, b), (a, b)


def _mm_nn_bwd(res, g):
    a, b = res
    return _nt(g, b).astype(a.dtype), _tn(a, g).astype(b.dtype)


mm_nn.defvjp(_mm_nn_fwd, _mm_nn_bwd)


@jax.custom_vjp
def mm_nt(a, b):
    return _nt(a, b)


def _mm_nt_fwd(a, b):
    return _nt(a, b), (a, b)


def _mm_nt_bwd(res, g):
    a, b = res
    return _nn(g, b).astype(a.dtype), _tn(g, a).astype(b.dtype)


mm_nt.defvjp(_mm_nt_fwd, _mm_nt_bwd)


@jax.custom_vjp
def mm_tn(a, b):
    return _tn(a, b)


def _mm_tn_fwd(a, b):
    return _tn(a, b), (a, b)


def _mm_tn_bwd(res, g):
    a, b = res
    return _nt(b, g).astype(a.dtype), _nn(a, g).astype(b.dtype)


mm_tn.defvjp(_mm_tn_fwd, _mm_tn_bwd)


def _iota2(shape, dim):
    return lax.broadcasted_iota(jnp.int32, shape, dim)


def _split3(x):
    hi = x.astype(BF16)
    rest = x - hi.astype(F32)
    mid = rest.astype(BF16)
    return hi, mid, (rest - mid.astype(F32)).astype(BF16)


def _tri_dot(tri, x):
    return sum(_dot(tri.astype(BF16), piece, 1, 0) for piece in _split3(x))


@jax.custom_vjp
def cumsum_rows(x):
    n = x.shape[0]
    return _tri_dot(_iota2((n, n), 0) >= _iota2((n, n), 1), x)


def _cumsum_fwd(x):
    return cumsum_rows(x), None


def _cumsum_bwd(_, g):
    n = g.shape[0]
    return (_tri_dot(_iota2((n, n), 0) <= _iota2((n, n), 1), g),)


cumsum_rows.defvjp(_cumsum_fwd, _cumsum_bwd)


def _rot_matrix(transposed):
    i, j = _iota2((LANES, LANES), 0), _iota2((LANES, LANES), 1)
    if transposed:
        i, j = j, i
    half = MLA_ROPE // 2
    plus = (i == j - half) & (j >= half) & (j < MLA_ROPE)
    minus = (i == j + half) & (j < half)
    return (plus.astype(F32) - minus.astype(F32)).astype(BF16)


def _rot_dot(x, transposed):
    return sum(_dot(piece, _rot_matrix(transposed), 1, 0) for piece in _split3(x))


@jax.custom_vjp
def rot_half(x):
    return _rot_dot(x, False)


def _rot_fwd(x):
    return rot_half(x), None


def _rot_bwd(_, g):
    return (_rot_dot(g, True),)


rot_half.defvjp(_rot_fwd, _rot_bwd)


def _shift_rows_raw(x, s):
    n = x.shape[0]
    row = _iota2(x.shape, 0)
    rolled = pltpu.roll(x, s % n, 0)
    keep = (row >= s) if s > 0 else (row < n + s)
    return jnp.where(keep, rolled, 0.0)


@functools.partial(jax.custom_vjp, nondiff_argnums=(1,))
def shift_rows(x, s):
    return _shift_rows_raw(x, s)


def _shift_fwd(x, s):
    return _shift_rows_raw(x, s), None


def _shift_bwd(s, _, g):
    return (_shift_rows_raw(g, -s),)


shift_rows.defvjp(_shift_fwd, _shift_bwd)


def _layer_norm(a, g, b):
    mu = jnp.mean(a, -1, keepdims=True)
    xc = a - mu
    var = jnp.mean(xc * xc, -1, keepdims=True)
    return xc * lax.rsqrt(var + LN_EPS) * g + b


def _rms_norm(a, g):
    return a * lax.rsqrt(jnp.mean(a * a, -1, keepdims=True) + RMS_EPS) * g


def _log_sigmoid(z):
    return jnp.minimum(z, 0.0) - jnp.log(1.0 + jnp.exp(-jnp.abs(z)))


def _matmul(a, b, *, name, ta=False, tb=False, tm=512, tn=512, a_fn=None, epi=None, epi_ins=(), out_dtypes=(BF16,),
            b_at=None, out_at=None, out_buf=None, after=None, n_row_sums=0, a_ins=(), a_out_dtypes=()):
    m = a.shape[1] if ta else a.shape[0]
    k = a.shape[0] if ta else a.shape[1]
    if b_at is None:
        n, kb = (b.shape[0], b.shape[1]) if tb else (b.shape[1], b.shape[0])
    else:
        rb, r = b_at
        n, kb = (N_DEV * r, b.shape[2]) if tb else (b.shape[2], N_DEV * r)
    assert kb == k, (name, a.shape, b.shape, k, kb)
    tm, tn = min(tm, m), min(tn, n)
    assert m % tm == 0 and n % tn == 0, (name, m, n, tm, tn)
    a_spec = pl.BlockSpec((k, tm), lambda i, j: (0, i)) if ta else pl.BlockSpec((tm, k), lambda i, j: (i, 0))
    if b_at is None:
        b_spec = pl.BlockSpec((tn, k), lambda i, j: (j, 0)) if tb else pl.BlockSpec((k, tn), lambda i, j: (0, j))
        load_b = lambda ref: ref[...]
    elif tb and tn == n:
        b_spec = pl.BlockSpec((N_DEV, r, k), lambda i, j: (0, rb, 0))
        load_b = lambda ref: ref[...].reshape(n, k)
    elif tb:
        assert tn == r, (name, tn, r)
        b_spec = pl.BlockSpec((1, r, k), lambda i, j: (j, rb, 0))
        load_b = lambda ref: ref[0]
    else:
        b_spec = pl.BlockSpec((N_DEV, r, tn), lambda i, j: (0, rb, j))
        load_b = lambda ref: ref[...].reshape(k, tn)
    e_specs = []
    for e in epi_ins:
        if e.shape == (1, n):
            e_specs.append(pl.BlockSpec((1, tn), lambda i, j: (0, j)))
        else:
            assert e.shape == (m, n), (name, e.shape, m, n)
            e_specs.append(pl.BlockSpec((tm, tn), lambda i, j: (i, j)))
    n_epi, n_ain, n_aout = len(epi_ins), len(a_ins), len(a_out_dtypes)
    assert n_aout == 0 or (tn == n and not ta and out_at is None), name
    ca, cb = (0 if ta else 1), (1 if tb else 0)
    operands = [a, b, *a_ins, *epi_ins]
    in_specs = [a_spec, b_spec, *[a_spec] * n_ain, *e_specs]
    if out_at is None:
        assert n_row_sums == 0 or tn == n, (name, tn, n)
        out_specs = [pl.BlockSpec((tm, tn), lambda i, j: (i, j)) for _ in out_dtypes]
        out_specs += [pl.BlockSpec((tm, k), lambda i, j: (i, 0))] * n_aout
        out_specs += [pl.BlockSpec((1, n), lambda i, j: (0, 0))] * n_row_sums
        out_shape = [jax.ShapeDtypeStruct((m, n), dt) for dt in out_dtypes]
        out_shape += [jax.ShapeDtypeStruct((m, k), dt) for dt in a_out_dtypes]
        out_shape += [jax.ShapeDtypeStruct((1, n), F32)] * n_row_sums
        aliases, n_buf = {}, 0
    else:
        orb, orows = out_at
        assert len(out_dtypes) == 1 and m == N_DEV * orows and n == out_buf.shape[2], (name, m, n)
        if tm > orows:
            assert tm % orows == 0, (name, tm, orows)
            out_specs = [pl.BlockSpec((tm // orows, orows, tn), lambda i, j: (i, orb, j))]
        else:
            per = orows // tm
            out_specs = [pl.BlockSpec((1, tm, tn), lambda i, j: (i // per, orb * per + i % per, j))]
        out_shape = [jax.ShapeDtypeStruct(out_buf.shape, out_buf.dtype)]
        operands.append(out_buf)
        in_specs.append(pl.BlockSpec(memory_space=pl.ANY))
        aliases, n_buf = {len(operands) - 1: 0}, 1
    for dep in ([] if after is None else after if isinstance(after, (list, tuple)) else [after]):
        if dep is not None:
            operands.append(dep)
            in_specs.append(pl.BlockSpec(memory_space=pl.ANY))
            n_buf += 1

    def body(a_ref, b_ref, *rest):
        av, a_outs = a_ref[...], ()
        if a_fn is not None:
            av = a_fn(av, *[r_[...] for r_ in rest[:n_ain]])
            if n_aout:
                av, *a_outs = av
        acc = _dot(av.astype(BF16), load_b(b_ref).astype(BF16), ca, cb)
        outs = epi(acc, *[r_[...] for r_ in rest[n_ain:n_ain + n_epi]]) if epi is not None else (acc,)
        o_refs = rest[n_ain + n_epi + n_buf:]
        n_tiles = len(o_refs) - n_row_sums - n_aout
        for o_ref, val in zip(o_refs[:n_tiles + n_aout], (*outs[:n_tiles], *a_outs)):
            o_ref[...] = val.astype(o_ref.dtype).reshape(o_ref.shape)
        if n_row_sums:
            @pl.when(pl.program_id(0) == 0)
            def _():
                for o_ref in o_refs[n_tiles + n_aout:]:
                    o_ref[...] = jnp.zeros_like(o_ref)

            for o_ref, val in zip(o_refs[n_tiles + n_aout:], outs[n_tiles:]):
                o_ref[...] += val

    outs = pl.pallas_call(
        body, name=name, grid=(m // tm, n // tn), in_specs=in_specs, out_specs=out_specs, out_shape=out_shape,
        input_output_aliases=aliases, compiler_params=_params(),
    )(*operands)
    return outs[0] if len(outs) == 1 else tuple(outs)


def _tile_fwd(f, tiled, params, out_dtypes, *, tm, name):
    t = tiled[0].shape[0]
    assert t % tm == 0
    out_avals = jax.eval_shape(f, *[jax.ShapeDtypeStruct((tm, x.shape[1]), F32) for x in tiled],
                               *[jax.ShapeDtypeStruct(p.shape, F32) for p in params])
    nt, npar = len(tiled), len(params)

    def body(*refs):
        ins = [r[...].astype(F32) for r in refs[:nt + npar]]
        outs = f(*ins)
        for o_ref, val in zip(refs[nt + npar:], outs):
            o_ref[...] = val.astype(o_ref.dtype)

    return pl.pallas_call(
        body, name=name, grid=(t // tm,),
        in_specs=[pl.BlockSpec((tm, x.shape[1]), lambda i: (i, 0)) for x in tiled]
        + [pl.BlockSpec(p.shape, lambda i: (0, 0)) for p in params],
        out_specs=[pl.BlockSpec((tm, o.shape[1]), lambda i: (i, 0)) for o in out_avals],
        out_shape=[jax.ShapeDtypeStruct((t, o.shape[1]), dt) for o, dt in zip(out_avals, out_dtypes)],
        compiler_params=_params(),
    )(*tiled, *params)


def _tile_bwd(f, tiled, params, cots, d_tiled_dtypes, *, tm, name, diff_tiled=None):
    t = tiled[0].shape[0]
    assert t % tm == 0
    nt, npar, nc = len(tiled), len(params), len(cots)
    diff_tiled = list(range(nt)) if diff_tiled is None else diff_tiled

    def body(*refs):
        ins = [r[...].astype(F32) for r in refs[:nt + npar]]
        cts = [r[...].astype(F32) for r in refs[nt + npar:nt + npar + nc]]
        o_refs = refs[nt + npar + nc:]
        _, vjp = jax.vjp(f, *ins)
        grads = vjp(tuple(cts))
        for o_ref, idx in zip(o_refs[:len(diff_tiled)], diff_tiled):
            o_ref[...] = grads[idx].astype(o_ref.dtype)
        p_refs = o_refs[len(diff_tiled):]

        @pl.when(pl.program_id(0) == 0)
        def _():
            for p_ref in p_refs:
                p_ref[...] = jnp.zeros_like(p_ref)

        for p_ref, gp in zip(p_refs, grads[nt:]):
            p_ref[...] += gp

    outs = pl.pallas_call(
        body, name=name, grid=(t // tm,),
        in_specs=[pl.BlockSpec((tm, x.shape[1]), lambda i: (i, 0)) for x in tiled]
        + [pl.BlockSpec(p.shape, lambda i: (0, 0)) for p in params]
        + [pl.BlockSpec((tm, c.shape[1]), lambda i: (i, 0)) for c in cots],
        out_specs=[pl.BlockSpec((tm, tiled[idx].shape[1]), lambda i: (i, 0)) for idx in diff_tiled]
        + [pl.BlockSpec(p.shape, lambda i: (0, 0)) for p in params],
        out_shape=[jax.ShapeDtypeStruct(tiled[idx].shape, dt) for idx, dt in zip(diff_tiled, d_tiled_dtypes)]
        + [jax.ShapeDtypeStruct(p.shape, F32) for p in params],
        compiler_params=_params(),
    )(*tiled, *params, *cots)
    return outs[:len(diff_tiled)], outs[len(diff_tiled):]


def _gla_head(q, k, v, r, z, g, st):
    c = q.shape[0]
    causal = _iota2((c, c), 0) >= _iota2((c, c), 1)
    la = _log_sigmoid(z) * (1.0 / GLA_TAU)
    big_l = cumsum_rows(la)
    ep, en = jnp.exp(big_l), jnp.exp(-big_l)
    qs = q * (GLA_DK ** -0.5)
    qp = qs * ep
    s = jnp.where(causal, mm_nt(qp, k * en), mm_nt(qs * en, k * ep))
    o = mm_nn(s, v) + mm_nt(qp, st)
    l_end = jnp.sum(la, axis=0, keepdims=True)
    st_new = st * jnp.exp(l_end) + mm_tn(v, k * jnp.exp(l_end - big_l))
    u = _rms_norm(o, g) * (r * jax.nn.sigmoid(r))
    return u, st_new


def _gla_slices(h):
    q = slice(GLA_DK * h, GLA_DK * (h + 1))
    k = slice(GLA_HK + GLA_DK * h, GLA_HK + GLA_DK * (h + 1))
    v = slice(2 * GLA_HK + GLA_DV * h, 2 * GLA_HK + GLA_DV * (h + 1))
    r = slice(2 * GLA_HK + GLA_HV + GLA_DV * h, 2 * GLA_HK + GLA_HV + GLA_DV * (h + 1))
    return q, k, v, r


GLA_CHUNKS_PER_STEP = 4


def _gla_fwd(proj, z, norm_g, after):
    t = proj.shape[0]
    nc, per = t // CHUNK, GLA_CHUNKS_PER_STEP
    rows_per_step = per * CHUNK
    after = [a for a in after if a is not None]

    def body(proj_ref, z_ref, g_ref, *rest):
        u_ref, st_save_ref, st_ref = rest[len(after):]

        @pl.when(pl.program_id(0) == 0)
        def _():
            st_ref[...] = jnp.zeros_like(st_ref)

        g = g_ref[...]
        for h in range(GLA_HEADS):
            sq, sk, sv, sr = _gla_slices(h)
            st = st_ref[h]
            for c in range(per):
                rows = slice(c * CHUNK, (c + 1) * CHUNK)
                st_save_ref[c, h] = st
                u, st = _gla_head(proj_ref[rows, sq].astype(F32), proj_ref[rows, sk].astype(F32),
                                  proj_ref[rows, sv].astype(F32), proj_ref[rows, sr].astype(F32),
                                  z_ref[rows, GLA_DK * h:GLA_DK * (h + 1)], g, st)
                u_ref[rows, GLA_DV * h:GLA_DV * (h + 1)] = u.astype(u_ref.dtype)
            st_ref[h] = st

    return pl.pallas_call(
        body, name="gla_fwd", grid=(nc // per,),
        in_specs=[pl.BlockSpec((rows_per_step, GLA_MAIN), lambda i: (i, 0)),
                  pl.BlockSpec((rows_per_step, GLA_HK), lambda i: (i, 0)), pl.BlockSpec((1, GLA_DV), lambda i: (0, 0))]
        + [pl.BlockSpec(memory_space=pl.ANY)] * len(after),
        out_specs=[pl.BlockSpec((rows_per_step, GLA_HV), lambda i: (i, 0)),
                   pl.BlockSpec((per, GLA_HEADS, GLA_DV, GLA_DK), lambda i: (i, 0, 0, 0))],
        out_shape=[jax.ShapeDtypeStruct((t, GLA_HV), BF16), jax.ShapeDtypeStruct((nc, GLA_HEADS, GLA_DV, GLA_DK), F32)],
        scratch_shapes=[pltpu.VMEM((GLA_HEADS, GLA_DV, GLA_DK), F32)],
        compiler_params=_params(),
    )(proj, z, norm_g, *after)


def _gla_bwd(proj, z, norm_g, states, du, after):
    t = proj.shape[0]
    nc, per = t // CHUNK, GLA_CHUNKS_PER_STEP
    rows_per_step = per * CHUNK
    n_steps = nc // per
    after = [a for a in after if a is not None]

    def body(proj_ref, z_ref, g_ref, st_in_ref, du_ref, *rest):
        dproj_ref, dz_ref, dg_ref, dzsum_ref, dst_ref = rest[len(after):]

        @pl.when(pl.program_id(0) == 0)
        def _():
            dst_ref[...] = jnp.zeros_like(dst_ref)
            dg_ref[...] = jnp.zeros_like(dg_ref)
            dzsum_ref[...] = jnp.zeros_like(dzsum_ref)

        g = g_ref[...]
        for h in range(GLA_HEADS):
            sq, sk, sv, sr = _gla_slices(h)
            dst = dst_ref[h]
            for c in reversed(range(per)):
                rows = slice(c * CHUNK, (c + 1) * CHUNK)
                ins = (proj_ref[rows, sq].astype(F32), proj_ref[rows, sk].astype(F32), proj_ref[rows, sv].astype(F32),
                       proj_ref[rows, sr].astype(F32), z_ref[rows, GLA_DK * h:GLA_DK * (h + 1)], g, st_in_ref[c, h])
                _, vjp = jax.vjp(_gla_head, *ins)
                dq, dk, dv, dr, dz, dg, dst = vjp((du_ref[rows, GLA_DV * h:GLA_DV * (h + 1)], dst))
                dproj_ref[rows, sq] = dq.astype(dproj_ref.dtype)
                dproj_ref[rows, sk] = dk.astype(dproj_ref.dtype)
                dproj_ref[rows, sv] = dv.astype(dproj_ref.dtype)
                dproj_ref[rows, sr] = dr.astype(dproj_ref.dtype)
                dz_ref[rows, GLA_DK * h:GLA_DK * (h + 1)] = dz
                dzsum_ref[:, GLA_DK * h:GLA_DK * (h + 1)] += jnp.sum(dz, axis=0, keepdims=True)
                dg_ref[...] += dg
            dst_ref[h] = dst

    rev = lambda i: (n_steps - 1 - i, 0)
    return pl.pallas_call(
        body, name="gla_bwd", grid=(n_steps,),
        in_specs=[pl.BlockSpec((rows_per_step, GLA_MAIN), rev), pl.BlockSpec((rows_per_step, GLA_HK), rev),
                  pl.BlockSpec((1, GLA_DV), lambda i: (0, 0)),
                  pl.BlockSpec((per, GLA_HEADS, GLA_DV, GLA_DK), lambda i: (n_steps - 1 - i, 0, 0, 0)),
                  pl.BlockSpec((rows_per_step, GLA_HV), rev)] + [pl.BlockSpec(memory_space=pl.ANY)] * len(after),
        out_specs=[pl.BlockSpec((rows_per_step, GLA_MAIN), rev), pl.BlockSpec((rows_per_step, GLA_HK), rev),
                   pl.BlockSpec((1, GLA_DV), lambda i: (0, 0)), pl.BlockSpec((1, GLA_HK), lambda i: (0, 0))],
        out_shape=[jax.ShapeDtypeStruct((t, GLA_MAIN), BF16), jax.ShapeDtypeStruct((t, GLA_HK), F32),
                   jax.ShapeDtypeStruct((1, GLA_DV), F32), jax.ShapeDtypeStruct((1, GLA_HK), F32)],
        scratch_shapes=[pltpu.VMEM((GLA_HEADS, GLA_DV, GLA_DK), F32)],
        compiler_params=_params(),
    )(proj, z, norm_g, states, du, *after)


def _mla_pre(cq, cos, sin, gq, gkv, w_uq, w_ukv):
    qlat = _rms_norm(cq[:, :MLA_RANK], gq)
    kvlat = _rms_norm(cq[:, MLA_RANK:2 * MLA_RANK], gkv)
    kr = cq[:, 2 * MLA_RANK:]
    q = mm_nn(qlat, w_uq) * ((MLA_NOPE + MLA_ROPE) ** -0.5)
    kv = mm_nn(kvlat, w_ukv)
    pieces = []
    for h in range(MLA_HEADS):
        qr = q[:, MLA_HEAD_PAD * h + MLA_NOPE:MLA_HEAD_PAD * (h + 1)]
        pieces += [q[:, MLA_HEAD_PAD * h:MLA_HEAD_PAD * h + MLA_NOPE], qr * cos + rot_half(qr) * sin]
    return jnp.concatenate(pieces, axis=1), kv, kr * cos + rot_half(kr) * sin


MLA_Q_TILE = 512


def _mla_attn_block(qn, qr, kv, kr, q0):
    tq, nk = qn.shape[0], kv.shape[0]
    s = mm_nt(qn, kv[:, :MLA_NOPE]) + mm_nt(qr, kr)
    visible = (_iota2((tq, nk), 1) // CHUNK) <= ((q0 + _iota2((tq, nk), 0)) // CHUNK)
    s = jnp.where(visible, s, -1e30)
    e = jnp.exp(s - jnp.max(s, -1, keepdims=True))
    p = e / jnp.sum(e, -1, keepdims=True)
    return mm_nn(p, kv[:, MLA_NOPE:])


def _mla_attn_fwd(q, kv, kr, after):
    t = q.shape[0]
    after = [a for a in after if a is not None]

    def body(q_ref, kv_ref, kr_ref, *rest):
        (o_ref,) = rest[len(after):]
        for i in range(t // MLA_Q_TILE):
            rows = slice(i * MLA_Q_TILE, (i + 1) * MLA_Q_TILE)
            keys = slice(0, (i + 1) * MLA_Q_TILE)
            o = _mla_attn_block(q_ref[rows, :MLA_NOPE].astype(F32), q_ref[rows, MLA_NOPE:].astype(F32),
                                kv_ref[keys, :].astype(F32), kr_ref[keys, :].astype(F32), i * MLA_Q_TILE)
            o_ref[rows, :] = o.astype(o_ref.dtype)

    return pl.pallas_call(
        body, name="mla_attn_fwd", grid=(MLA_HEADS,),
        in_specs=[pl.BlockSpec((t, MLA_HEAD_PAD), lambda h: (0, h)),
                  pl.BlockSpec((t, MLA_NOPE + MLA_V), lambda h: (0, h)), pl.BlockSpec((t, LANES), lambda h: (0, 0))]
        + [pl.BlockSpec(memory_space=pl.ANY)] * len(after),
        out_specs=pl.BlockSpec((t, MLA_V), lambda h: (0, h)),
        out_shape=jax.ShapeDtypeStruct((t, MLA_HEADS * MLA_V), BF16),
        compiler_params=_params(),
    )(q, kv, kr, *after)


def _mla_attn_bwd(q, kv, kr, do, after):
    t = q.shape[0]
    after = [a for a in after if a is not None]

    def body(q_ref, kv_ref, kr_ref, do_ref, *rest):
        dq_ref, dkv_ref, dkr_ref = rest[len(after):]
        dkv_ref[...] = jnp.zeros_like(dkv_ref)

        @pl.when(pl.program_id(0) == 0)
        def _():
            dkr_ref[...] = jnp.zeros_like(dkr_ref)

        for i in range(t // MLA_Q_TILE):
            rows = slice(i * MLA_Q_TILE, (i + 1) * MLA_Q_TILE)
            keys = slice(0, (i + 1) * MLA_Q_TILE)
            f = functools.partial(_mla_attn_block, q0=i * MLA_Q_TILE)
            _, vjp = jax.vjp(f, q_ref[rows, :MLA_NOPE].astype(F32), q_ref[rows, MLA_NOPE:].astype(F32),
                             kv_ref[keys, :].astype(F32), kr_ref[keys, :].astype(F32))
            dqn, dqr, dkv, dkr = vjp(do_ref[rows, :].astype(F32))
            dq_ref[rows, :MLA_NOPE] = dqn
            dq_ref[rows, MLA_NOPE:] = dqr
            dkv_ref[keys, :] += dkv
            dkr_ref[keys, :] += dkr

    return pl.pallas_call(
        body, name="mla_attn_bwd", grid=(MLA_HEADS,),
        in_specs=[pl.BlockSpec((t, MLA_HEAD_PAD), lambda h: (0, h)),
                  pl.BlockSpec((t, MLA_NOPE + MLA_V), lambda h: (0, h)), pl.BlockSpec((t, LANES), lambda h: (0, 0)),
                  pl.BlockSpec((t, MLA_V), lambda h: (0, h))] + [pl.BlockSpec(memory_space=pl.ANY)] * len(after),
        out_specs=[pl.BlockSpec((t, MLA_HEAD_PAD), lambda h: (0, h)),
                   pl.BlockSpec((t, MLA_NOPE + MLA_V), lambda h: (0, h)), pl.BlockSpec((t, LANES), lambda h: (0, 0))],
        out_shape=[jax.ShapeDtypeStruct(q.shape, F32), jax.ShapeDtypeStruct(kv.shape, F32),
                   jax.ShapeDtypeStruct(kr.shape, F32)],
        compiler_params=_params(),
    )(q, kv, kr, do, *after)


def _rope_tables(pos_col, inv_freq_row):
    t = pos_col.shape[0]

    def body(pos_ref, f_ref, cos_ref, sin_ref):
        ang = pos_ref[...].astype(F32) * f_ref[...]
        live = _iota2(ang.shape, 1) < MLA_ROPE
        cos_ref[...] = jnp.where(live, jnp.cos(ang), 0.0)
        sin_ref[...] = jnp.where(live, jnp.sin(ang), 0.0)

    return pl.pallas_call(
        body, name="rope_tables", out_shape=[jax.ShapeDtypeStruct((t, LANES), F32)] * 2, compiler_params=_params(),
    )(pos_col, inv_freq_row)


CONV_COL_TILE = 256


def _conv_gate(b, c, u, w0, w1, w2):
    cu = c * u
    return b * (w2 * cu + w1 * shift_rows(cu, 1) + w0 * shift_rows(cu, 2))


def _conv_specs(t):
    nb = D_MODEL // CONV_COL_TILE
    return [pl.BlockSpec((t, CONV_COL_TILE), lambda j, part=part: (0, part * nb + j)) for part in range(3)]


def _conv_fwd(bcu, w, after):
    t = bcu.shape[0]
    after = [a for a in after if a is not None]

    def body(b_ref, c_ref, u_ref, w_ref, *rest):
        (o_ref,) = rest[len(after):]
        o_ref[...] = _conv_gate(b_ref[...], c_ref[...], u_ref[...], w_ref[0:1, :], w_ref[1:2, :],
                                w_ref[2:3, :]).astype(o_ref.dtype)

    return pl.pallas_call(
        body, name="conv_fwd", grid=(D_MODEL // CONV_COL_TILE,),
        in_specs=_conv_specs(t) + [pl.BlockSpec((3, CONV_COL_TILE), lambda j: (0, j))]
        + [pl.BlockSpec(memory_space=pl.ANY)] * len(after),
        out_specs=pl.BlockSpec((t, CONV_COL_TILE), lambda j: (0, j)),
        out_shape=jax.ShapeDtypeStruct((t, D_MODEL), BF16), compiler_params=_params(),
    )(bcu, bcu, bcu, w, *after)


def _conv_bwd(bcu, w, dout, after):
    t = bcu.shape[0]
    after = [a for a in after if a is not None]

    def body(b_ref, c_ref, u_ref, w_ref, do_ref, *rest):
        db_ref, dc_ref, du_ref, dw_ref = rest[len(after):]
        _, vjp = jax.vjp(_conv_gate, b_ref[...], c_ref[...], u_ref[...], w_ref[0:1, :], w_ref[1:2, :], w_ref[2:3, :])
        db, dc, du, dw0, dw1, dw2 = vjp(do_ref[...])
        db_ref[...] = db.astype(db_ref.dtype)
        dc_ref[...] = dc.astype(dc_ref.dtype)
        du_ref[...] = du.astype(du_ref.dtype)
        dw_ref[0:1, :] = dw0
        dw_ref[1:2, :] = dw1
        dw_ref[2:3, :] = dw2

    col = pl.BlockSpec((t, CONV_COL_TILE), lambda j: (0, j))
    return pl.pallas_call(
        body, name="conv_bwd", grid=(D_MODEL // CONV_COL_TILE,),
        in_specs=_conv_specs(t) + [pl.BlockSpec((3, CONV_COL_TILE), lambda j: (0, j)), col]
        + [pl.BlockSpec(memory_space=pl.ANY)] * len(after),
        out_specs=[col, col, col, pl.BlockSpec((3, CONV_COL_TILE), lambda j: (0, j))],
        out_shape=[jax.ShapeDtypeStruct((t, D_MODEL), BF16)] * 3 + [jax.ShapeDtypeStruct((3, D_MODEL), F32)],
        compiler_params=_params(),
    )(bcu, bcu, bcu, w, dout, *after)


def _loss_head(y, target):
    t, d = y.shape
    tm = 256

    def body(y_ref, t_ref, loss_ref, dy_ref):
        @pl.when(pl.program_id(0) == 0)
        def _():
            loss_ref[...] = jnp.zeros_like(loss_ref)

        err = y_ref[...] - t_ref[...]
        dy_ref[...] = err * (1.0 / d)
        loss_ref[...] += 0.5 * jnp.sum(jnp.sum(err * err, axis=-1, keepdims=True) * (1.0 / d))

    tile = pl.BlockSpec((tm, d), lambda i: (i, 0))
    return pl.pallas_call(
        body, name="loss_head", grid=(t // tm,), in_specs=[tile, tile],
        out_specs=[pl.BlockSpec((8, LANES), lambda i: (0, 0)), tile],
        out_shape=[jax.ShapeDtypeStruct((8, LANES), F32), jax.ShapeDtypeStruct((t, d), F32)],
        compiler_params=_params(),
    )(y, target)


def _ln_epi(acc, res, g, b):
    a = ALPHA * res + acc
    y = _layer_norm(a, g, b)
    return a, y, y


def _ln_fn(a, g, b):
    return (_layer_norm(a, g, b),)


def _ln_bwd_epi(scale):
    def epi(acc, res, a, g, b):
        _, vjp = jax.vjp(_ln_fn, a, g, b)
        da, dg, db = vjp((acc + scale * res,))
        return da, da, dg, db
    return epi


def _relu_sq(h):
    r = jnp.maximum(h, 0)
    return r * r


MLP_ROW_TILE = 256


def _mlp_fwd(x1b, x1, wa, g, b, after):
    t = x1b.shape[0]
    after = [a for a in after if a is not None]

    def body(xb_ref, x_ref, w1_ref, w2_ref, g_ref, b_ref, *rest):
        hh_ref, a_ref, y_ref, yb_ref = rest[len(after):]
        hh = _dot(xb_ref[...], w1_ref[...].reshape(D_FF, D_MODEL), 1, 1).astype(BF16)
        hh_ref[...] = hh
        acc = _dot(_relu_sq(hh), w2_ref[...].reshape(D_FF, D_MODEL), 1, 0)
        a, y, _ = _ln_epi(acc, x_ref[...], g_ref[...], b_ref[...])
        a_ref[...] = a
        y_ref[...] = y
        yb_ref[...] = y.astype(BF16)

    rows = lambda cols: pl.BlockSpec((MLP_ROW_TILE, cols), lambda i: (i, 0))
    region = lambda reg: pl.BlockSpec((N_DEV, reg[1], D_MODEL), lambda i: (0, reg[0], 0))
    vec = pl.BlockSpec((1, D_MODEL), lambda i: (0, 0))
    return pl.pallas_call(
        body, name="mlp_fwd", grid=(t // MLP_ROW_TILE,),
        in_specs=[rows(D_MODEL), rows(D_MODEL), region(REG_W1T), region(REG_W2), vec, vec]
        + [pl.BlockSpec(memory_space=pl.ANY)] * len(after),
        out_specs=[rows(D_FF), rows(D_MODEL), rows(D_MODEL), rows(D_MODEL)],
        out_shape=[jax.ShapeDtypeStruct((t, D_FF), BF16), jax.ShapeDtypeStruct((t, D_MODEL), F32),
                   jax.ShapeDtypeStruct((t, D_MODEL), F32), jax.ShapeDtypeStruct((t, D_MODEL), BF16)],
        compiler_params=_params(),
    )(x1b, x1, wa, wa, g, b, *after)


def _mlp_bwd_dx(d_a2b, d_a2, hh, a1, wa, g, b, after):
    t = d_a2b.shape[0]
    after = [x for x in after if x is not None]

    def body(db_ref, d_ref, hh_ref, a1_ref, w2_ref, w1_ref, g_ref, b_ref, *rest):
        dhh_ref, da_ref, dab_ref, dg_ref, dbias_ref = rest[len(after):]
        acc = _dot(db_ref[...], w2_ref[...].reshape(D_FF, D_MODEL), 1, 1)
        d_hh = (acc * 2.0 * jnp.maximum(hh_ref[...].astype(F32), 0.0)).astype(BF16)
        dhh_ref[...] = d_hh
        acc = _dot(d_hh, w1_ref[...].reshape(D_FF, D_MODEL), 1, 0)
        da, _, dg, dbias = _ln_bwd_epi(ALPHA)(acc, d_ref[...], a1_ref[...], g_ref[...], b_ref[...])
        da_ref[...] = da
        dab_ref[...] = da.astype(BF16)

        @pl.when(pl.program_id(0) == 0)
        def _():
            dg_ref[...] = jnp.zeros_like(dg_ref)
            dbias_ref[...] = jnp.zeros_like(dbias_ref)

        dg_ref[...] += dg
        dbias_ref[...] += dbias

    rows = lambda cols: pl.BlockSpec((MLP_ROW_TILE, cols), lambda i: (i, 0))
    region = lambda reg: pl.BlockSpec((N_DEV, reg[1], D_MODEL), lambda i: (0, reg[0], 0))
    vec = pl.BlockSpec((1, D_MODEL), lambda i: (0, 0))
    return pl.pallas_call(
        body, name="mlp_bwd_dx", grid=(t // MLP_ROW_TILE,),
        in_specs=[rows(D_MODEL), rows(D_MODEL), rows(D_FF), rows(D_MODEL), region(REG_W2), region(REG_W1T), vec, vec]
        + [pl.BlockSpec(memory_space=pl.ANY)] * len(after),
        out_specs=[rows(D_FF), rows(D_MODEL), rows(D_MODEL), vec, vec],
        out_shape=[jax.ShapeDtypeStruct((t, D_FF), BF16), jax.ShapeDtypeStruct((t, D_MODEL), F32),
                   jax.ShapeDtypeStruct((t, D_MODEL), BF16), jax.ShapeDtypeStruct((1, D_MODEL), F32),
                   jax.ShapeDtypeStruct((1, D_MODEL), F32)],
        compiler_params=_params(),
    )(d_a2b, d_a2, hh, a1, wa, wa, g, b, *after)


def _pad_cols(w, n):
    return jnp.pad(w, ((0, 0), (0, n - w.shape[1])))


def _pad_rows(w, n):
    return jnp.pad(w, ((0, n - w.shape[0]), (0, 0)))


def _step(x, p, positions, target, small, comm):
    t = x.shape[0]
    w = small
    freqs = ROPE_BASE ** (-jnp.arange(0, MLA_ROPE // 2, dtype=F32) * (2.0 / MLA_ROPE))
    freq_row = jnp.concatenate([freqs, freqs, jnp.zeros((LANES - MLA_ROPE,), F32)])[None, :]
    cos, sin = _rope_tables(positions.reshape(t, 1), freq_row)

    saved = []
    xb = x.astype(BF16)
    for i in range(DEPTH):
        j, kind = i // 3, i % 3
        wl = comm.mixer_weights(i)
        s = {"x": xb, "wl": wl}
        tok = comm.at("fwd", i, "begin", x)
        if kind == 0:
            s["w_main"] = wl["gla_w_in_t"][:GLA_MAIN]
            s["w_lr"] = _pad_rows(wl["gla_w_in_t"][GLA_MAIN:], LANES)
            s["w_up"] = _pad_rows(w["gla_w_gate_up"][j], LANES).astype(BF16)
            s["proj"] = _matmul(xb, s["w_main"], name="gla_proj", tb=True, tn=1024, after=tok)
            s["glr"] = _matmul(xb, s["w_lr"], name="gla_lr", tb=True, out_dtypes=(F32,))
            s["z"] = _matmul(s["glr"], s["w_up"], name="gla_gate", epi=lambda acc, b: (acc + b,),
                             epi_ins=(w["gla_b_gate"][j][None, :],), out_dtypes=(F32,))
            tok = comm.at("fwd", i, "proj_done", s["z"]) or []
            s["u"], s["states"] = _gla_fwd(s["proj"], s["z"], w["gla_norm_g"][j][None, :], tok)
        elif kind == 1:
            s["cq"] = _matmul(xb, wl["mla_in"], name="mla_proj", tn=MLA_IN_PAD, b_at=REG_MLA_IN, out_dtypes=(F32,),
                              after=tok)
            s["pre_params"] = (w["mla_q_norm"][j][None, :], w["mla_kv_norm"][j][None, :], wl["mla_w_uq"], wl["mla_w_ukv"])
            s["q"], s["kv"], s["kr"] = _tile_fwd(_mla_pre, (s["cq"], cos, sin), s["pre_params"], (BF16, BF16, BF16),
                                                 tm=256, name="mla_pre_fwd")
            tok = comm.at("fwd", i, "proj_done", s["kv"]) or []
            s["u"] = _mla_attn_fwd(s["q"], s["kv"], s["kr"], tok)
        else:
            s["bcu"] = _matmul(xb, wl["conv"], name="conv_proj", tb=True, tm=256, tn=3 * D_MODEL, b_at=REG_CONV,
                               out_dtypes=(F32,), after=tok)
            tok = comm.at("fwd", i, "proj_done", s["bcu"]) or []
            s["u"] = _conv_fwd(s["bcu"], w["conv_w"][j], tok)
        g0, b0 = w["ln_g"][i, 0][None, :], w["ln_b"][i, 0][None, :]
        g1, b1 = w["ln_g"][i, 1][None, :], w["ln_b"][i, 1][None, :]
        wa, wb = s["wa"], _ = comm.slab_weights(i, s["u"])
        s["a1"], s["x1"], s["x1b"] = _matmul(s["u"], wa, name="mixer_out_ln", tm=256, tn=D_MODEL, b_at=REG_WOUT,
                                             epi=_ln_epi, epi_ins=(x, g0, b0), out_dtypes=(F32, F32, BF16))
        tok = comm.at("fwd", i, "mid", s["x1b"]) or []
        s["hh"], s["a2"], s["x2"], s["x2b"] = _mlp_fwd(s["x1b"], s["x1"], wa, g1, b1, tok)
        s["pp"] = _matmul(p[i], wb, name="ple_proj", tb=True, tn=D_MODEL, b_at=REG_WPT)
        tok = comm.at("fwd", i, "end", s["pp"])
        def ple_epi(acc, xr, pp):
            y = xr + jax.nn.sigmoid(acc) * pp.astype(F32)
            return y, y, acc

        x, xb, s["gt"] = _matmul(s["x2b"], wa, name="ple_gate", tn=1024, b_at=REG_WG, epi=ple_epi,
                                 epi_ins=(s["x2"], s["pp"]), out_dtypes=(F32, BF16, BF16), after=tok)
        saved.append(s)

    loss_part, dx = _loss_head(x, target)

    gw = {n: [None] * WEIGHTS[n][0][0] for n in SMALL + REPLICATED}
    ln_g_grads, ln_b_grads = [[None, None] for _ in range(DEPTH)], [[None, None] for _ in range(DEPTH)]
    resid = lambda acc, r: (acc + ALPHA * r,)
    plus = lambda acc, r: (acc + r,)
    for i in reversed(range(DEPTH)):
        j, kind = i // 3, i % 3
        s = saved[i]
        wa = s["wa"]
        ga = lax.empty((N_DEV, A_ROWS, D_MODEL), BF16)
        gb = lax.empty((N_DEV, REG_WPT[1], PLE_DIM), BF16)
        layer_grads = {}
        tok = comm.at("bwd", i, "begin", dx)

        def ple_bwd(dxo, gt, pp):
            sg = jax.nn.sigmoid(gt.astype(F32))
            d_gt = dxo * pp.astype(F32) * sg * (1.0 - sg)
            return d_gt, d_gt, dxo * sg

        g1, b1 = w["ln_g"][i, 1][None, :], w["ln_b"][i, 1][None, :]
        d_a2, d_a2b, d_gt, d_pp, ln_g_grads[i][1], ln_b_grads[i][1] = _matmul(
            dx, wa, name="ple_gate_dx_ln", tb=True, tm=256, tn=D_MODEL, b_at=REG_WG, a_fn=ple_bwd,
            a_ins=(s["gt"], s["pp"]), a_out_dtypes=(BF16, BF16), epi=_ln_bwd_epi(1.0), epi_ins=(dx, s["a2"], g1, b1),
            out_dtypes=(F32, BF16), n_row_sums=2, after=tok)
        gb = _matmul(d_pp, p[i], name="ple_proj_dw", ta=True, tm=512, tn=PLE_DIM, out_at=REG_WPT, out_buf=gb)
        ga = _matmul(s["x2b"], d_gt, name="ple_gate_dw", ta=True, tm=512, tn=1024, out_at=REG_WG, out_buf=ga)
        tok = comm.at("bwd", i, "ln", d_a2)
        ga = _matmul(s["hh"], d_a2b, name="mlp_down_dw", ta=True, tm=REG_W2[1], tn=1024, a_fn=_relu_sq, out_at=REG_W2,
                     out_buf=ga, after=tok)
        g0, b0 = w["ln_g"][i, 0][None, :], w["ln_b"][i, 0][None, :]
        d_hh, d_a1, d_a1b, ln_g_grads[i][0], ln_b_grads[i][0] = _mlp_bwd_dx(d_a2b, d_a2, s["hh"], s["a1"], wa, g0, b0,
                                                                            [ga, gb])
        ga = _matmul(d_hh, s["x1b"], name="mlp_up_dw", ta=True, tm=REG_W1T[1], tn=1024, out_at=REG_W1T, out_buf=ga)
        ga = _matmul(s["u"], d_a1b, name="mixer_out_dw", ta=True, tm=512, tn=1024, out_at=REG_WOUT, out_buf=ga)
        du = _matmul(d_a1b, wa, name="mixer_out_dx", tb=True, tn=1024, b_at=REG_WOUT, out_dtypes=(F32,), after=ga)
        comm.slab_grads(i, ga, gb)
        tok = comm.at("bwd", i, "slab_done", du) or []
        if kind == 0:
            dproj, dz, dg, dz_sum = _gla_bwd(s["proj"], s["z"], w["gla_norm_g"][j][None, :], s["states"], du, tok)
            tok = comm.at("bwd", i, "mixer_done", dproj)
            gw["gla_norm_g"][j] = dg[0]
            gw["gla_b_gate"][j] = dz_sum[0]
            gw["gla_w_gate_up"][j] = _matmul(s["glr"], dz, name="gla_gate_dw", ta=True, out_dtypes=(F32,),
                                             after=tok)[:GLA_RANK]
            dglr = _matmul(dz, s["w_up"], name="gla_gate_dx", tb=True, out_dtypes=(F32,))
            dw_main = _matmul(dproj, s["x"], name="gla_proj_dw", ta=True, tn=1024, out_dtypes=(F32,))
            dw_lr = _matmul(dglr, s["x"], name="gla_lr_dw", ta=True, tn=1024, out_dtypes=(F32,))[:GLA_RANK]
            layer_grads["gla_w_in_t"] = jnp.concatenate([dw_main, dw_lr], axis=0)
            dx = _matmul(dproj, s["w_main"], name="gla_proj_dx", tn=1024, epi=resid, epi_ins=(d_a1,),
                         out_dtypes=(F32,), after=[dw_main, dw_lr, gw["gla_w_gate_up"][j]])
            dx = _matmul(dglr, s["w_lr"], name="gla_lr_dx", tn=1024, epi=plus, epi_ins=(dx,), out_dtypes=(F32,))
        elif kind == 1:
            dq, dkv, dkr = _mla_attn_bwd(s["q"], s["kv"], s["kr"], du, tok)
            tok = comm.at("bwd", i, "mixer_done", dq)
            (d_cq,), (dgq, dgkv, layer_grads["mla_uq"], layer_grads["mla_ukv"]) = _tile_bwd(
                _mla_pre, (s["cq"], cos, sin), s["pre_params"], (dq, dkv, dkr), (BF16,), tm=256, name="mla_pre_bwd",
                diff_tiled=[0])
            gw["mla_q_norm"][j], gw["mla_kv_norm"][j] = dgq[0], dgkv[0]
            layer_grads["mla_in"] = _matmul(s["x"], d_cq, name="mla_proj_dw", ta=True, tm=512, tn=MLA_IN_PAD,
                                            out_at=REG_MLA_IN, after=tok,
                                            out_buf=lax.empty((N_DEV, REG_MLA_IN[1], MLA_IN_PAD), BF16))
            dx = _matmul(d_cq, s["wl"]["mla_in"], name="mla_proj_dx", tb=True, tn=1024, b_at=REG_MLA_IN, epi=resid,
                         epi_ins=(d_a1,), out_dtypes=(F32,), after=layer_grads["mla_in"])
        else:
            db, dc, du_, dcw = _conv_bwd(s["bcu"], w["conv_w"][j], du, tok)
            tok = comm.at("bwd", i, "mixer_done", db)
            gw["conv_w"][j] = dcw
            dbcu = jnp.concatenate([db, dc, du_], axis=1)
            layer_grads["conv"] = _matmul(dbcu, s["x"], name="conv_proj_dw", ta=True, tm=REG_CONV[1], tn=1024,
                                          out_at=REG_CONV, out_buf=lax.empty((N_DEV, REG_CONV[1], D_MODEL), BF16),
                                          after=tok)
            dx = _matmul(dbcu, s["wl"]["conv"], name="conv_proj_dx", tn=1024, b_at=REG_CONV, epi=resid, epi_ins=(d_a1,),
                         out_dtypes=(F32,), after=layer_grads["conv"])
        comm.mixer_grads(i, layer_grads)

    gw["ln_g"] = [jnp.concatenate([a, b], axis=0) for a, b in ln_g_grads]
    gw["ln_b"] = [jnp.concatenate([a, b], axis=0) for a, b in ln_b_grads]
    return loss_part, dx, {n: jnp.stack(gw[n]).astype(F32) for n in gw}


MESH_IDS = pl.DeviceIdType.MESH
ANY = pl.BlockSpec(memory_space=pl.ANY)
HBM_SPEC = pl.BlockSpec(memory_space=pltpu.HBM)
SEM_SPEC = pl.BlockSpec(memory_space=pltpu.SEMAPHORE)
DATAFLOW_EFFECT = pltpu.SideEffectType.DATAFLOW_SIDE_EFFECTING
CORE_COPIES, CHIP_COPIES = 4, 3


def _my_place():
    return lax.axis_index("x"), lax.axis_index("y"), lax.axis_index("c")


def _other_chips(mx, my):
    return [(1 - mx, my), (mx, 1 - my), (1 - mx, 1 - my)]


def _remote(src, dst, send_sems, recv_sems, k, to):
    return pltpu.make_async_remote_copy(src_ref=src, dst_ref=dst, send_sem=send_sems.at[k], recv_sem=recv_sems.at[k],
                                        device_id=to, device_id_type=MESH_IDS)


def _gather_first_copies(n_arr):
    def make(bufs, send_sems, recv_sems):
        mx, my, mc = _my_place()
        mine = 4 * mx + 2 * my + mc
        peers = [(mx, my, 1 - mc)] + [(cx, cy, mc) for cx, cy in _other_chips(mx, my)]
        return [_remote(bufs[a].at[mine], bufs[a].at[mine], send_sems, recv_sems, (1 + CHIP_COPIES) * a + k, to)
                for a in range(n_arr) for k, to in enumerate(peers)]
    return make, (1 + CHIP_COPIES) * n_arr


def _gather_forward_copies(n_arr):
    def make(bufs, send_sems, recv_sems):
        mx, my, mc = _my_place()
        blocks = [4 * cx + 2 * cy + mc for cx, cy in _other_chips(mx, my)]
        return [_remote(bufs[a].at[blk], bufs[a].at[blk], send_sems, recv_sems, CHIP_COPIES * a + k, (mx, my, 1 - mc))
                for a in range(n_arr) for k, blk in enumerate(blocks)]
    return make, CHIP_COPIES * n_arr


def _scatter_core_copies(n_arr):
    def make(bufs, send_sems, recv_sems):
        mx, my, mc = _my_place()
        return [_remote(bufs[a].at[2 * k + (1 - mc)], bufs[n_arr + a].at[k], send_sems, recv_sems, CORE_COPIES * a + k,
                        (mx, my, 1 - mc)) for a in range(n_arr) for k in range(CORE_COPIES)]
    return make, CORE_COPIES * n_arr


def _scatter_chip_copies(n_arr):
    def make(bufs, send_sems, recv_sems):
        mx, my, mc = _my_place()
        return [_remote(bufs[a].at[2 * cx + cy], bufs[n_arr + a].at[k], send_sems, recv_sems, CHIP_COPIES * a + k,
                        (cx, cy, mc)) for a in range(n_arr) for k, (cx, cy) in enumerate(_other_chips(mx, my))]
    return make, CHIP_COPIES * n_arr


def _exchange(name, bufs, copies):
    make, n_copies = copies
    n = len(bufs)

    def body(*refs):
        descs = make(refs[:n], refs[2 * n], refs[2 * n + 1])
        for cp in descs:
            cp.start()
        for cp in descs:
            cp.wait()

    return pl.pallas_call(
        body, name=name, out_shape=[jax.ShapeDtypeStruct(b.shape, b.dtype) for b in bufs], in_specs=[ANY] * n,
        out_specs=[ANY] * n, input_output_aliases={i: i for i in range(n)},
        scratch_shapes=[pltpu.SemaphoreType.DMA((n_copies,)), pltpu.SemaphoreType.DMA((n_copies,))],
    )(*bufs)


def _exchange_start(name, parts, after):
    sizes = [len(bufs) for bufs, _ in parts]
    all_bufs = [b for bufs, _ in parts for b in bufs]
    n, n_sems = len(all_bufs), 2 * len(parts)

    def body(*refs):
        offset = 0
        for p, ((_, (make, _)), size) in enumerate(zip(parts, sizes)):
            for cp in make(refs[offset:offset + size], refs[n + 1 + 2 * p], refs[n + 2 + 2 * p]):
                cp.start()
            offset += size
        refs[-1][...] = jnp.zeros_like(refs[-1])

    outs = pl.pallas_call(
        body, name=name,
        out_shape=(*[pltpu.SemaphoreType.DMA((n_copies,)) for _, (_, n_copies) in parts for _ in range(2)],
                   *[pltpu.HBM(b.shape, b.dtype) for b in all_bufs], jax.ShapeDtypeStruct((8, LANES), F32)),
        in_specs=[HBM_SPEC] * n + [ANY],
        out_specs=(*[SEM_SPEC] * n_sems, *[HBM_SPEC] * n, pl.BlockSpec(memory_space=pltpu.VMEM)),
        input_output_aliases={i: n_sems + i for i in range(n)},
        compiler_params=pltpu.CompilerParams(has_side_effects=DATAFLOW_EFFECT),
    )(*[pltpu.with_memory_space_constraint(b, pltpu.HBM) for b in all_bufs], after)
    started, offset = [], n_sems
    for p, size in enumerate(sizes):
        started.append(((outs[2 * p], outs[2 * p + 1]), list(outs[offset:offset + size])))
        offset += size
    return started, outs[-1]


def _exchange_wait(name, sems, bufs, copies, after):
    make, _ = copies
    n = len(bufs)

    def body(*refs):
        for cp in make(refs[:n], refs[n], refs[n + 1]):
            cp.wait_send()
            cp.wait_recv()

    return list(pl.pallas_call(
        body, name=name, out_shape=[pltpu.HBM(b.shape, b.dtype) for b in bufs],
        in_specs=[HBM_SPEC] * n + [SEM_SPEC, SEM_SPEC, ANY], out_specs=[HBM_SPEC] * n,
        input_output_aliases={i: i for i in range(n)},
        compiler_params=pltpu.CompilerParams(has_side_effects=DATAFLOW_EFFECT),
    )(*bufs, *sems, after))


SUM_TILE_BYTES = 4 * 1024 * 1024


def _row_tile(r, c):
    best = None
    for cand in range(16, r + 1, 16):
        if r % cand == 0 and cand * c * 2 <= SUM_TILE_BYTES:
            best = cand
    return r if best is None else best


def _pair_sum(g, recv, my_c):
    _, r, c = g.shape
    tr = _row_tile(r, c)

    def body(c_ref, g_ref, r_ref, o_ref):
        o_ref[...] = (g_ref[...].astype(F32) + r_ref[...].astype(F32)).astype(o_ref.dtype)

    return pl.pallas_call(
        body, name="rs_pair_sum", out_shape=jax.ShapeDtypeStruct((4, r, c), g.dtype),
        grid_spec=pltpu.PrefetchScalarGridSpec(
            num_scalar_prefetch=1, grid=(4, r // tr),
            in_specs=[pl.BlockSpec((1, tr, c), lambda n, i, cr: (2 * n + cr[0], i, 0)),
                      pl.BlockSpec((1, tr, c), lambda n, i, cr: (n, i, 0))],
            out_specs=pl.BlockSpec((1, tr, c), lambda n, i, cr: (n, i, 0))),
        compiler_params=_params(),
    )(my_c, g, recv)


def _chip_sum(h, recv, my_chip):
    _, r, c = h.shape
    tr = _row_tile(r, c)

    def body(j_ref, h_ref, r0_ref, r1_ref, r2_ref, o_ref):
        o_ref[...] = ((h_ref[0].astype(F32) + r0_ref[0].astype(F32)) + r1_ref[0].astype(F32)) + r2_ref[0].astype(F32)

    return pl.pallas_call(
        body, name="rs_chip_sum", out_shape=jax.ShapeDtypeStruct((r, c), F32),
        grid_spec=pltpu.PrefetchScalarGridSpec(
            num_scalar_prefetch=1, grid=(r // tr,),
            in_specs=[pl.BlockSpec((1, tr, c), lambda i, jr: (jr[0], i, 0))]
            + [pl.BlockSpec((1, tr, c), lambda i, jr, n=n: (n, i, 0)) for n in range(3)],
            out_specs=pl.BlockSpec((tr, c), lambda i, jr: (i, 0))),
        compiler_params=_params(),
    )(my_chip, h, recv, recv, recv)


def _sum_blocks(g):
    n, r, c = g.shape

    def body(g_ref, o_ref):
        acc = g_ref[0]
        for k in range(1, n):
            acc = acc + g_ref[k]
        o_ref[...] = acc

    return pl.pallas_call(body, name="sum_blocks", out_shape=jax.ShapeDtypeStruct((r, c), F32), compiler_params=_params())(g)


def _pack(flat_parts, cols, row_multiple, dtype):
    flat = jnp.concatenate([f.astype(dtype) for f in flat_parts])
    per_row_block = cols * row_multiple
    padded = -(-flat.shape[0] // per_row_block) * per_row_block
    return jnp.pad(flat, (0, padded - flat.shape[0])).reshape(padded // cols, cols)


def _shard_shape(name):
    shape, axis = WEIGHTS[name]
    if axis is None:
        return shape
    return tuple(s // N_DEV if a == axis else s for a, s in enumerate(shape))


def _size(shape):
    n = 1
    for s in shape:
        n *= s
    return n


def _unshard(blocks, name):
    _, axis = WEIGHTS[name]
    return jnp.concatenate([blocks[k] for k in range(N_DEV)], axis=axis)


def _unpack_blocks(flat, names):
    out, off = {}, 0
    for n in names:
        shp = _shard_shape(n)
        out[n] = flat[..., off:off + _size(shp)].reshape(flat.shape[:-1] + shp)
        off += _size(shp)
    return out


def _layer_slabs(shard, i):
    j, kind = i // 3, i % 3
    w_out = (shard["gla_w_out"], shard["mla_w_out"], shard["conv_w_out"])[kind][j]
    out = {"a": jnp.concatenate([shard["mlp_w2"][i], shard["mlp_w1"][i].T, w_out, shard["ple_w_gate"][i]], axis=0).astype(BF16),
           "b": shard["ple_w_proj"][i].T.astype(BF16)}
    if kind == 0:
        out["gla"] = shard["gla_w_in"][j].T.astype(BF16)
    elif kind == 1:
        out["mla_in"] = _pad_cols(shard["mla_w_in"][j], MLA_IN_PAD).astype(BF16)
        out["mla_uq"] = _pad_cols(shard["mla_w_uq"][j], MLA_HEAD_PAD).astype(BF16)
        out["mla_ukv"] = shard["mla_w_ukv"][j].astype(BF16)
    else:
        out["conv"] = shard["conv_w_in"][j].T.astype(BF16)
    return out


def _mixer_weights(landed, i):
    kind = i % 3
    if kind == 0:
        return {"gla_w_in_t": landed["gla"].reshape(-1, D_MODEL)}
    if kind == 2:
        return {"conv": landed["conv"]}
    heads_side_by_side = lambda g: g.transpose(1, 0, 2).reshape(g.shape[1], -1)
    return {"mla_in": landed["mla_in"], "mla_w_uq": heads_side_by_side(landed["mla_uq"]),
            "mla_w_ukv": heads_side_by_side(landed["mla_ukv"])}


def _mixer_grad_buffers(layer_grads, i):
    kind = i % 3
    if kind == 0:
        return {"gla": layer_grads["gla_w_in_t"].reshape(N_DEV, -1, D_MODEL).astype(BF16)}
    if kind == 2:
        return {"conv": layer_grads["conv"]}
    head_blocks = lambda g: g.reshape(g.shape[0], N_DEV, -1).transpose(1, 0, 2).astype(BF16)
    return {"mla_in": layer_grads["mla_in"], "mla_uq": head_blocks(layer_grads["mla_uq"]),
            "mla_ukv": head_blocks(layer_grads["mla_ukv"])}


SLAB_KEYS = ("a", "b")


class _Overlap:
    def __init__(self, shard, small_pack):
        mx, my, mc = _my_place()
        self.my_c = mc.astype(jnp.int32).reshape(1)
        self.my_chip = (2 * mx + my).astype(jnp.int32).reshape(1)
        mine = 4 * mx + 2 * my + mc
        def landing_of(slabs):
            return {k: lax.dynamic_update_index_in_dim(lax.empty((N_DEV, *v.shape), v.dtype), v, mine, 0)
                    for k, v in slabs.items()}

        first = _layer_slabs(shard, 0)
        first["small"] = small_pack
        self.landing = [landing_of(first)]
        self.fly = {}
        self.grads = [{} for _ in range(DEPTH)]
        self.reduced = [{} for _ in range(DEPTH)]
        tok = self._start_many([self._first_spec(0, "mixer"), self._first_spec(0, "slab")], shard["ln_g"])
        shard, tok = lax.optimization_barrier((shard, tok))
        self.landing += [landing_of(_layer_slabs(shard, i)) for i in range(1, DEPTH)]
        bufs = self._wait("ag_first_mixer_l0", tok)
        self.landing[0].update(zip(self._keys(self.landing[0], "mixer"),
                                   _exchange("ag_forward_mixer_l0", bufs, _gather_forward_copies(len(bufs)))))

    @staticmethod
    def _keys(names, group):
        return [k for k in names if (k in SLAB_KEYS) == (group == "slab")]

    def _start(self, name, bufs, copies, after):
        return self._start_many([(name, bufs, copies)], after)

    def _start_many(self, specs, after):
        call = specs[0][0] + ("" if len(specs) == 1 else f"_and_{len(specs) - 1}_more") + "_start"
        started, tok = _exchange_start(call, [(bufs, copies) for _, bufs, copies in specs], after)
        for (name, _, copies), (sems, bufs) in zip(specs, started):
            self.fly[name] = (sems, bufs, copies)
        return tok

    def _wait(self, name, after):
        sems, bufs, copies = self.fly.pop(name)
        return _exchange_wait(name + "_wait", sems, bufs, copies, after)

    def mixer_weights(self, i):
        return _mixer_weights(self.landing[i], i)

    def slab_weights(self, i, dep):
        self._gather_done(i, "slab", dep)
        return self.landing[i]["a"], self.landing[i]["b"]

    def slab_grads(self, i, ga, gb):
        self.grads[i].update(a=ga, b=gb)

    def mixer_grads(self, i, layer_grads):
        self.grads[i].update(_mixer_grad_buffers(layer_grads, i))

    def at(self, phase, i, point, dep):
        toks = []
        if phase == "fwd":
            if point == "begin" and i == 0:
                toks.append(self._start_many([self._first_spec(1, "mixer"), self._first_spec(1, "slab")],
                                             self.landing[0][self._keys(self.landing[0], "mixer")[0]]))
            if point == "proj_done":
                toks.append(self._gather_forward(i, "slab", dep))
            if point == "mid" and i + 1 < DEPTH:
                bufs = self._wait(f"ag_first_mixer_l{i + 1}", dep)
                specs = [(f"ag_forward_mixer_l{i + 1}", bufs, _gather_forward_copies(len(bufs)))]
                if i + 2 < DEPTH:
                    specs += [self._first_spec(i + 2, "mixer"), self._first_spec(i + 2, "slab")]
                toks.append(self._start_many(specs, dep))
            if point == "end" and i + 1 < DEPTH:
                self._gather_done(i + 1, "mixer", dep)
        else:
            if point == "begin" and i + 1 < DEPTH:
                toks.append(self._scatter_cores(i + 1, "mixer", dep))
            if point == "ln" and i + 1 < DEPTH:
                toks.append(self._scatter_chips(i + 1, "mixer", dep))
            if point == "slab_done":
                if i + 1 < DEPTH:
                    self._scatter_done(i + 1, "slab", dep)
                    self._scatter_done(i + 1, "mixer", dep)
                toks.append(self._scatter_cores(i, "slab", dep))
            if point == "mixer_done":
                toks.append(self._scatter_chips(i, "slab", dep))
        return toks or None

    def _first_spec(self, i, group):
        bufs = [self.landing[i][k] for k in self._keys(self.landing[i], group)]
        return f"ag_first_{group}_l{i}", bufs, _gather_first_copies(len(bufs))

    def _gather_forward(self, i, group, after):
        bufs = self._wait(f"ag_first_{group}_l{i}", after)
        return self._start(f"ag_forward_{group}_l{i}", bufs, _gather_forward_copies(len(bufs)), after)

    def _gather_done(self, i, group, after):
        keys = self._keys(self.landing[i], group)
        self.landing[i].update(zip(keys, self._wait(f"ag_forward_{group}_l{i}", after)))

    def _scatter_cores(self, i, group, after):
        gs = [self.grads[i][k] for k in self._keys(self.grads[i], group)]
        land = [lax.empty((4, *g.shape[1:]), g.dtype) for g in gs]
        return self._start(f"rs_cores_{group}_l{i}", gs + land, _scatter_core_copies(len(gs)), after)

    def _pair_sums(self, bufs):
        n = len(bufs) // 2
        hs = [_pair_sum(g, r, self.my_c) for g, r in zip(bufs[:n], bufs[n:])]
        return hs + [lax.empty((3, *h.shape[1:]), h.dtype) for h in hs]

    def _scatter_chips(self, i, group, after):
        bufs = self._pair_sums(self._wait(f"rs_cores_{group}_l{i}", after))
        return self._start(f"rs_chips_{group}_l{i}", bufs, _scatter_chip_copies(len(bufs) // 2), after)

    def _chip_sums(self, i, group, bufs):
        n = len(bufs) // 2
        for k, h, r in zip(self._keys(self.grads[i], group), bufs[:n], bufs[n:]):
            self.reduced[i][k] = _chip_sum(h, r, self.my_chip)

    def _scatter_done(self, i, group, after):
        self._chip_sums(i, group, self._wait(f"rs_chips_{group}_l{i}", after))

    def tail_begin(self, dep):
        return self._scatter_cores(0, "mixer", dep)

    def tail_middle(self, dep):
        self._scatter_done(0, "slab", dep)
        return self._scatter_chips(0, "mixer", dep)

    def tail_end(self, dep):
        self._scatter_done(0, "mixer", dep)


def _small_gather_start(x, name, after):
    mx, my, mc = _my_place()
    land = lax.dynamic_update_index_in_dim(lax.empty((N_DEV, *x.shape), x.dtype), x, 4 * mx + 2 * my + mc, 0)
    started, tok = _exchange_start(name + "_first_start", [([land], _gather_first_copies(1))], after)
    return name, started[0], tok


def _small_gather_finish(started, after):
    name, (sems, bufs), _ = started
    bufs = _exchange_wait(name + "_first_wait", sems, bufs, _gather_first_copies(1), after)
    return _exchange(name + "_forward", bufs, _gather_forward_copies(1))[0]


def _adamw_math(w, g, m, v):
    m2 = ADAM_B1 * m + (1.0 - ADAM_B1) * g
    v2 = ADAM_B2 * v + (1.0 - ADAM_B2) * (g * g)
    m_hat = m2 / (1.0 - ADAM_B1 ** ADAM_STEP)
    v_hat = v2 / (1.0 - ADAM_B2 ** ADAM_STEP)
    return -ADAM_LR * (m_hat / (jnp.sqrt(v_hat) + ADAM_EPS) + ADAM_WD * w), m2, v2


ADAMW_TILE_BYTES = 1024 * 1024


def _adamw_layer(name, w, m, v, j, g, g_at, transposed, chain, after):
    n_layers, r, c = w.shape
    tr = max(t for t in range(8, r + 1, 8) if r % t == 0 and (t * c * 4 <= ADAMW_TILE_BYTES or t == 8))
    rb, rows = g_at
    if transposed:
        assert rows == c and g.shape[1] == r, (name, g.shape, g_at)
        g_spec = pl.BlockSpec((rows, tr), lambda i: (rb, i))
    else:
        assert rows == r and g.shape[1] == c, (name, g.shape, g_at)
        g_spec = pl.BlockSpec((tr, c), lambda i: (rb * (r // tr) + i, 0))
    extra = list(chain or []) + [a for a in (after or []) if a is not None]
    n_chain = 4 if chain else 0

    def body(w_ref, m_ref, v_ref, g_ref, *rest):
        g_out, d_out, m_out, v_out, tok_ref = rest[len(extra):]
        gv = g_ref[...].T if transposed else g_ref[...]
        g_out[0] = gv
        d_out[0], m_out[0], v_out[0] = _adamw_math(w_ref[0], gv, m_ref[0], v_ref[0])
        tok_ref[...] = jnp.zeros_like(tok_ref)

    layer_spec = pl.BlockSpec((1, tr, c), lambda i: (j, i, 0))
    outs = pl.pallas_call(
        body, name=f"adamw_{name}_l{j}", grid=(r // tr,),
        in_specs=[layer_spec] * 3 + [g_spec] + [pl.BlockSpec(memory_space=pl.ANY)] * len(extra),
        out_specs=[layer_spec] * 4 + [pl.BlockSpec((8, LANES), lambda i: (0, 0))],
        out_shape=[jax.ShapeDtypeStruct(w.shape, F32)] * 4 + [jax.ShapeDtypeStruct((8, LANES), F32)],
        input_output_aliases={4 + k: k for k in range(n_chain)}, compiler_params=_params(),
    )(w, m, v, g, *extra)
    return list(outs[:4]), outs[4]


def _adamw(w, g, m, v, name):
    shape = w.shape
    cols = shape[-1]
    rows = _size(shape) // cols
    tr = rows
    for cand in (512, 256, 128, 64, 32, 16, 8):
        if rows > cand and rows % cand == 0:
            tr = cand
            break

    def body(w_ref, g_ref, m_ref, v_ref, d_ref, mo_ref, vo_ref):
        d_ref[...], mo_ref[...], vo_ref[...] = _adamw_math(w_ref[...], g_ref[...], m_ref[...], v_ref[...])

    spec = pl.BlockSpec((tr, cols), lambda i: (i, 0))
    outs = pl.pallas_call(
        body, name="adamw_" + name, grid=(rows // tr,), in_specs=[spec] * 4, out_specs=[spec] * 3,
        out_shape=[jax.ShapeDtypeStruct((rows, cols), F32)] * 3, compiler_params=_params(),
    )(*[a.reshape(rows, cols) for a in (w, g, m, v)])
    return [o.reshape(shape) for o in outs]


def kernel(x, p, positions, gla_w_in, gla_w_gate_up, gla_b_gate, gla_norm_g, gla_w_out, mla_w_in, mla_q_norm, mla_kv_norm, mla_w_uq, mla_w_ukv, mla_w_out, conv_w_in, conv_w, conv_w_out, ln_g, ln_b, mlp_w1, mlp_w2, ple_w_gate, ple_w_proj, loss_target, m_gla_w_in, m_gla_w_gate_up, m_gla_b_gate, m_gla_norm_g, m_gla_w_out, m_mla_w_in, m_mla_q_norm, m_mla_kv_norm, m_mla_w_uq, m_mla_w_ukv, m_mla_w_out, m_conv_w_in, m_conv_w, m_conv_w_out, m_ln_g, m_ln_b, m_mlp_w1, m_mlp_w2, m_ple_w_gate, m_ple_w_proj, v_gla_w_in, v_gla_w_gate_up, v_gla_b_gate, v_gla_norm_g, v_gla_w_out, v_mla_w_in, v_mla_q_norm, v_mla_kv_norm, v_mla_w_uq, v_mla_w_ukv, v_mla_w_out, v_conv_w_in, v_conv_w, v_conv_w_out, v_ln_g, v_ln_b, v_mlp_w1, v_mlp_w2, v_ple_w_gate, v_ple_w_proj):
    args = locals()
    shard = {n: args[n] for n in WEIGHT_NAMES}
    mom = {n: args["m_" + n] for n in WEIGHT_NAMES}
    var = {n: args["v_" + n] for n in WEIGHT_NAMES}
    mx, my, mc = _my_place()

    comm = _Overlap(shard, _pack([shard[n].reshape(-1) for n in SMALL], LANES, 8, F32))
    small_all = comm.landing[0]["small"]
    small = {n: shard[n] for n in REPLICATED}
    small.update({n: _unshard(blk, n) for n, blk in _unpack_blocks(small_all.reshape(N_DEV, -1), SMALL).items()})
    loss_part, grad_x, small_grads = _step(x[0], p[:, 0], positions[0], loss_target[0], small, comm)

    chains = {}

    def update(name, j, g, g_at, transposed, tok):
        chains[name], tok = _adamw_layer(name, shard[name], mom[name], var[name], j, g, g_at, transposed,
                                         chains.get(name), [tok])
        return tok

    def update_layer(i, groups, tok):
        j, kind = i // 3, i % 3
        red = comm.reduced[i]
        if "slab" in groups:
            tok = update("mlp_w2", i, red["a"], REG_W2, False, tok)
            tok = update("mlp_w1", i, red["a"], REG_W1T, True, tok)
            tok = update(("gla_w_out", "mla_w_out", "conv_w_out")[kind], j, red["a"], REG_WOUT, False, tok)
            tok = update("ple_w_gate", i, red["a"], REG_WG, False, tok)
            tok = update("ple_w_proj", i, red["b"], REG_WPT, True, tok)
        if "mixer" in groups:
            if kind == 0:
                tok = update("gla_w_in", j, red["gla"].T, (0, D_MODEL), False, tok)
            elif kind == 2:
                tok = update("conv_w_in", j, red["conv"], REG_CONV, True, tok)
            else:
                for n, g in (("mla_w_in", red["mla_in"][:, :MLA_IN]), ("mla_w_ukv", red["mla_ukv"]),
                             ("mla_w_uq", red["mla_uq"][:, :MLA_NOPE + MLA_ROPE])):
                    tok = update(n, j, g, (0, g.shape[0]), False, tok)
        return tok

    tok = comm.tail_begin(grad_x)
    tok = update_layer(3, ("slab", "mixer"), tok)
    tok = update_layer(2, ("slab", "mixer"), tok)
    tok = comm.tail_middle(tok)
    small_parts = [loss_part[0, :1]] + [small_grads[n].reshape(-1) for n in SMALL + REPLICATED]
    small_gather = _small_gather_start(_pack(small_parts, LANES, 8, F32), "ag_small_grads", tok)
    tok = update_layer(1, ("slab", "mixer"), small_gather[2])
    tok = update_layer(0, ("slab",), tok)
    comm.tail_end(tok)
    tok = update_layer(0, ("mixer",), tok)
    red_small = _sum_blocks(_small_gather_finish(small_gather, tok)).reshape(-1)
    loss = red_small[0]
    off = 1
    dev = 4 * mx + 2 * my + mc
    for n in SMALL + REPLICATED:
        shape, axis = WEIGHTS[n]
        full_g = red_small[off:off + _size(shape)].reshape(shape)
        off += _size(shape)
        if axis is not None:
            width = shape[axis] // N_DEV
            full_g = lax.dynamic_slice_in_dim(full_g, dev * width, width, axis=axis)
        chains[n] = [full_g, *_adamw(shard[n], full_g, mom[n], var[n], n)]
    return (loss, grad_x[None], *[chains[n][k] for k in range(4) for n in WEIGHT_NAMES])
```
